```python
import math
import jax, jax.numpy as jnp
from jax import lax
import numpy as np

D_MODEL = 1024
BATCH = 32
SEQ = 2048
DEPTH = 1

CTX_LEN = 256
GRID_W = 64

N_ATTN_HEADS = 16
QK_NOPE_DIM = 64
QK_ROPE_DIM = 32
V_HEAD_DIM = 64
Q_LORA_RANK = 384
KV_LORA_RANK = 256
ROPE_THETA = 10000.0
Q_BLOCK = 128
ATTN_SCALE = (QK_NOPE_DIM + QK_ROPE_DIM) ** -0.5

N_SSD_HEADS = 16
SSD_HEAD_DIM = 64
SSD_GROUPS = 2
HEADS_PER_GROUP = N_SSD_HEADS // SSD_GROUPS
SSD_STATE = 128
SSD_CONV = 5
SSD_CHUNK = 128
D_INNER = N_SSD_HEADS * SSD_HEAD_DIM
GN = SSD_GROUPS * SSD_STATE
XBC_WIDTH = D_INNER + 2 * GN

ATTN_WIDTH = N_ATTN_HEADS * V_HEAD_DIM
MIX_WIDTH = ATTN_WIDTH + D_INNER

D_FF = 2816
FFN_CONV = 3

N_MOD = 6
EPS = 1e-6

IN_SPLITS = (Q_LORA_RANK, KV_LORA_RANK, QK_ROPE_DIM, D_INNER, XBC_WIDTH, 2 * N_SSD_HEADS)
IN_WIDTH = sum(IN_SPLITS)
IN_OFFSETS = tuple(int(o) for o in np.cumsum(IN_SPLITS)[:-1])

kernel_name = "hybrid_mla_ssd_diffusion_layer"


def rms_norm(x, w):
    xf = x.astype(jnp.float32)
    y = xf * lax.rsqrt(jnp.mean(xf * xf, axis=-1, keepdims=True) + EPS)
    return (y * w.astype(jnp.float32)).astype(x.dtype)


def modulate(h, shift, scale):
    return h * (1 + scale) + shift


def in_split(h):
    return jnp.split(h, IN_OFFSETS, axis=-1)


def depthwise_conv(x, w, b):
    k = w.shape[0]
    out = lax.conv_general_dilated(
        x, w[:, None, :].astype(x.dtype), window_strides=(1,), padding=((k // 2, k // 2),),
        dimension_numbers=("NWC", "WIO", "NWC"), feature_group_count=x.shape[-1])
    return out + b


def axial_rope_tables(seq_len):
    n_rows = seq_len // GRID_W
    row = jnp.repeat(jnp.arange(n_rows), GRID_W).astype(jnp.float32)
    col = jnp.tile(jnp.arange(GRID_W), n_rows).astype(jnp.float32)
    axis_dim = QK_ROPE_DIM // 2
    inv_freq = ROPE_THETA ** (-jnp.arange(0, axis_dim, 2, dtype=jnp.float32) / axis_dim)
    ang_r = row[:, None] * inv_freq
    ang_c = col[:, None] * inv_freq
    ang = jnp.concatenate([ang_r, ang_r, ang_c, ang_c], axis=-1)
    return jnp.cos(ang), jnp.sin(ang)


def rotate_half_axial(x):
    def rh(t):
        a, b = jnp.split(t, 2, axis=-1)
        return jnp.concatenate([-b, a], axis=-1)
    xr, xc = jnp.split(x, 2, axis=-1)
    return jnp.concatenate([rh(xr), rh(xc)], axis=-1)


def apply_rope(x, cos, sin):
    xf = x.astype(jnp.float32)
    return (xf * cos + rotate_half_axial(xf) * sin).astype(x.dtype)


def mla_queries(cq, q_norm_w, w_q_up):
    b, l, _ = cq.shape
    q = (rms_norm(cq, q_norm_w) @ w_q_up).reshape(b, l, N_ATTN_HEADS, QK_NOPE_DIM + QK_ROPE_DIM)
    return q[..., :QK_NOPE_DIM], q[..., QK_NOPE_DIM:]


def mla_keys_values(ckv, kv_norm_w, w_kv_up):
    b, l, _ = ckv.shape
    kv = (rms_norm(ckv, kv_norm_w) @ w_kv_up).reshape(b, l, N_ATTN_HEADS, QK_NOPE_DIM + V_HEAD_DIM)
    return kv[..., :QK_NOPE_DIM], kv[..., QK_NOPE_DIM:]


def attend(q_nope, q_rope, k_nope, k_rope, v):
    s = (jnp.einsum("bqhd,bkhd->bhqk", q_nope, k_nope)
         + jnp.einsum("bqhr,bkr->bhqk", q_rope, k_rope)) * ATTN_SCALE
    p = jax.nn.softmax(s.astype(jnp.float32), axis=-1).astype(v.dtype)
    return jnp.einsum("bhqk,bkhd->bqhd", p, v)


def latent_attention(q_nope, q_rope, k_nope, k_rope, v):
    b, s, h, _ = q_nope.shape
    nb = s // Q_BLOCK

    def to_blocks(t):
        return jnp.moveaxis(t.reshape(b, nb, Q_BLOCK, *t.shape[2:]), 1, 0)

    out = lax.map(lambda qb: attend(qb[0], qb[1], k_nope, k_rope, v),
                  (to_blocks(q_nope), to_blocks(q_rope)))
    return jnp.moveaxis(out, 0, 1).reshape(b, s, h * V_HEAD_DIM)


def ssd_prepare(xbc, dt_raw, conv_w, conv_b, dt_bias):
    b, l, _ = xbc.shape
    xbc = jax.nn.silu(depthwise_conv(xbc, conv_w, conv_b))
    xs, bm, cm = jnp.split(xbc, [D_INNER, D_INNER + GN], axis=-1)
    xs = xs.reshape(b, l, SSD_GROUPS, HEADS_PER_GROUP, SSD_HEAD_DIM)
    bm = bm.reshape(b, l, SSD_GROUPS, SSD_STATE)
    cm = cm.reshape(b, l, SSD_GROUPS, SSD_STATE)
    dt = jax.nn.softplus(dt_raw.astype(jnp.float32).reshape(b, l, 2, SSD_GROUPS, HEADS_PER_GROUP)
                         + dt_bias.astype(jnp.float32).reshape(2, SSD_GROUPS, HEADS_PER_GROUP))
    return xs, bm, cm, dt


def segment_decay(cum):
    n = cum.shape[-1]
    diff = cum[..., :, None] - cum[..., None, :]
    mask = jnp.tril(jnp.ones((n, n), dtype=bool))
    return jnp.exp(jnp.where(mask, diff, -jnp.inf))


def ssd_states(xh, dt, A, bm, cm, init_state):
    b, l, g, e, p = xh.shape
    nc = l // SSD_CHUNK
    xd = (xh * dt[..., None]).reshape(b, nc, SSD_CHUNK, g, e, p)
    a_cum = jnp.cumsum(jnp.moveaxis((dt * A).reshape(b, nc, SSD_CHUNK, g, e), 2, -1), axis=-1)
    bc = bm.reshape(b, nc, SSD_CHUNK, g, -1)
    cc = cm.reshape(b, nc, SSD_CHUNK, g, -1)
    decay_to_end = jnp.exp(a_cum[..., -1:] - a_cum)
    chunk_states = jnp.einsum("bclgn,bcgel,bclgep->bcgepn", bc, decay_to_end, xd)
    states = jnp.concatenate([init_state[:, None], chunk_states], axis=1)
    chunk_cum = jnp.cumsum(jnp.pad(a_cum[..., -1], ((0, 0), (1, 0), (0, 0), (0, 0))), axis=1)
    decay_chunks = segment_decay(jnp.moveaxis(chunk_cum, 1, -1))
    states = jnp.einsum("bgezc,bcgepn->bzgepn", decay_chunks, states)
    return (xd, a_cum, bc, cc), states[:, :-1], states[:, -1]


def ssd_output(pieces, entering_states):
    xd, a_cum, bc, cc = pieces
    within = segment_decay(a_cum)
    cb = jnp.einsum("bclgn,bcsgn->bcgls", cc, bc)
    y_diag = jnp.einsum("bcgls,bcgels,bcsgep->bclgep", cb, within, xd)
    y_off = jnp.einsum("bclgn,bcgepn,bcgel->bclgep", cc, entering_states, jnp.exp(a_cum))
    b, nc, q, g, e, p = y_diag.shape
    return (y_diag + y_off).reshape(b, nc * q, g, e, p)


def ssd_finish(y, xs, z, d_skip, norm_w):
    b, l = y.shape[:2]
    y = y + d_skip.reshape(SSD_GROUPS, HEADS_PER_GROUP, 1) * xs
    y = y.reshape(b, l, D_INNER).astype(z.dtype)
    return rms_norm(y * jax.nn.silu(z), norm_w)


def conv_glu(h, w_up, conv_w, conv_b, w_down):
    gate, val = jnp.split(h @ w_up, 2, axis=-1)
    gate = depthwise_conv(gate, conv_w, conv_b)
    return (jax.nn.gelu(gate, approximate=False) * val) @ w_down


def _fwd_setup_inputs(seed: int = 0) -> dict:
    key = jax.random.key(seed)
    ks = jax.random.split(key, 32)
    f32 = jnp.float32
    L = DEPTH

    def dense(k, shape, fan_in):
        return jax.random.normal(k, shape, f32) * fan_in ** -0.5

    def gain(k, shape):
        return 1.0 + 0.1 * jax.random.normal(k, shape, f32)

    def bias(k, shape):
        return 0.02 * jax.random.normal(k, shape, f32)

    dt0 = jnp.exp(jax.random.uniform(ks[15], (L, 2, N_SSD_HEADS), f32, math.log(1e-3), math.log(1e-1)))
    return {
        "x": jax.random.normal(ks[0], (BATCH, SEQ, D_MODEL), f32),
        "c": jax.random.normal(ks[1], (BATCH, D_MODEL), f32),
        "ctx": jax.random.normal(ks[2], (BATCH, CTX_LEN, D_MODEL), f32),
        "c_ctx": jax.random.normal(ks[3], (D_MODEL,), f32),
        "w_mod": dense(ks[4], (L, D_MODEL, N_MOD * D_MODEL), D_MODEL),
        "b_mod": bias(ks[5], (L, N_MOD * D_MODEL)),
        "mix_pre_norm": gain(ks[6], (L, D_MODEL)),
        "mix_post_norm": gain(ks[7], (L, D_MODEL)),
        "w_in": dense(ks[8], (L, D_MODEL, IN_WIDTH), D_MODEL),
        "q_norm": gain(ks[9], (L, Q_LORA_RANK)),
        "w_q_up": dense(ks[10], (L, Q_LORA_RANK, N_ATTN_HEADS * (QK_NOPE_DIM + QK_ROPE_DIM)), Q_LORA_RANK),
        "kv_norm": gain(ks[11], (L, KV_LORA_RANK)),
        "w_kv_up": dense(ks[12], (L, KV_LORA_RANK, N_ATTN_HEADS * (QK_NOPE_DIM + V_HEAD_DIM)), KV_LORA_RANK),
        "ssd_conv_w": dense(ks[13], (L, SSD_CONV, XBC_WIDTH), SSD_CONV),
        "ssd_conv_b": bias(ks[14], (L, XBC_WIDTH)),
        "ssd_a_log": jnp.log(jax.random.uniform(ks[16], (L, 2, N_SSD_HEADS), f32, 1.0, 16.0)),
        "ssd_dt_bias": dt0 + jnp.log(-jnp.expm1(-dt0)),
        "ssd_d": gain(ks[17], (L, N_SSD_HEADS)),
        "ssd_norm": gain(ks[18], (L, D_INNER)),
        "w_out": dense(ks[19], (L, MIX_WIDTH, D_MODEL), MIX_WIDTH),
        "ffn_pre_norm": gain(ks[20], (L, D_MODEL)),
        "ffn_post_norm": gain(ks[21], (L, D_MODEL)),
        "w_up": dense(ks[22], (L, D_MODEL, 2 * D_FF), D_MODEL),
        "ffn_conv_w": dense(ks[23], (L, FFN_CONV, D_FF), FFN_CONV),
        "ffn_conv_b": bias(ks[24], (L, D_FF)),
        "w_down": dense(ks[25], (L, D_FF, D_MODEL), D_FF),
    }


def _fwd_reference(x, c, ctx, c_ctx, w_mod, b_mod, mix_pre_norm, mix_post_norm, w_in, q_norm, w_q_up,
              kv_norm, w_kv_up, ssd_conv_w, ssd_conv_b, ssd_a_log, ssd_dt_bias, ssd_d, ssd_norm, w_out,
              ffn_pre_norm, ffn_post_norm, w_up, ffn_conv_w, ffn_conv_b, w_down):
    bsz, seq, _ = x.shape
    cos, sin = axial_rope_tables(seq)
    cos_q, sin_q = cos[:, None, :], sin[:, None, :]

    def flip(t):
        return jnp.flip(t, axis=1)

    for l in range(DEPTH):
        last = l == DEPTH - 1
        mod_x = jnp.split((jax.nn.silu(c) @ w_mod[l] + b_mod[l])[:, None, :], N_MOD, axis=-1)
        mod_c = jnp.split((jax.nn.silu(c_ctx) @ w_mod[l] + b_mod[l])[None, None, :], N_MOD, axis=-1)

        cq_x, ckv_x, kr_x, z_x, xbc_x, dt_x = in_split(
            modulate(rms_norm(x, mix_pre_norm[l]), mod_x[0], mod_x[1]) @ w_in[l])
        cq_c, ckv_c, kr_c, z_c, xbc_c, dt_c = in_split(
            modulate(rms_norm(ctx, mix_pre_norm[l]), mod_c[0], mod_c[1]) @ w_in[l])

        q_nope_x, q_rope_x = mla_queries(cq_x, q_norm[l], w_q_up[l])
        q_rope_x = apply_rope(q_rope_x, cos_q, sin_q)
        k_nope_x, v_x = mla_keys_values(ckv_x, kv_norm[l], w_kv_up[l])
        k_rope_x = apply_rope(kr_x, cos, sin)
        k_nope_c, v_c = mla_keys_values(ckv_c, kv_norm[l], w_kv_up[l])
        attn_x = latent_attention(q_nope_x, q_rope_x,
                                  jnp.concatenate([k_nope_c, k_nope_x], axis=1),
                                  jnp.concatenate([kr_c, k_rope_x], axis=1),
                                  jnp.concatenate([v_c, v_x], axis=1))

        A = -jnp.exp(ssd_a_log[l].astype(jnp.float32)).reshape(2, SSD_GROUPS, HEADS_PER_GROUP)
        xs_x, b_x, c_x, dtv_x = ssd_prepare(xbc_x, dt_x, ssd_conv_w[l], ssd_conv_b[l], ssd_dt_bias[l])
        xs_c, b_c, c_c, dtv_c = ssd_prepare(xbc_c, dt_c, ssd_conv_w[l], ssd_conv_b[l], ssd_dt_bias[l])
        zero_state = jnp.zeros((bsz, SSD_GROUPS, HEADS_PER_GROUP, SSD_HEAD_DIM, SSD_STATE), xs_x.dtype)
        st_cf = ssd_states(xs_c, dtv_c[:, :, 0], A[0], b_c, c_c, zero_state)
        st_cb = ssd_states(flip(xs_c), flip(dtv_c[:, :, 1]), A[1], flip(b_c), flip(c_c), zero_state)
        st_xf = ssd_states(xs_x, dtv_x[:, :, 0], A[0], b_x, c_x, st_cf[2])
        st_xb = ssd_states(flip(xs_x), flip(dtv_x[:, :, 1]), A[1], flip(b_x), flip(c_x), st_cb[2])
        y_x = ssd_output(st_xf[0], st_xf[1]) + flip(ssd_output(st_xb[0], st_xb[1]))
        ssd_x = ssd_finish(y_x, xs_x, z_x, ssd_d[l], ssd_norm[l])

        mix_x = jnp.concatenate([attn_x, ssd_x], axis=-1) @ w_out[l]
        x = x + mod_x[2] * rms_norm(mix_x, mix_post_norm[l])

        if not last:
            q_nope_c, q_rope_c = mla_queries(cq_c, q_norm[l], w_q_up[l])
            attn_c = attend(q_nope_c, q_rope_c, k_nope_c, kr_c, v_c).reshape(bsz, -1, ATTN_WIDTH)
            y_c = ssd_output(st_cf[0], st_cf[1]) + flip(ssd_output(st_cb[0], st_cb[1]))
            ssd_c = ssd_finish(y_c, xs_c, z_c, ssd_d[l], ssd_norm[l])
            mix_c = jnp.concatenate([attn_c, ssd_c], axis=-1) @ w_out[l]
            ctx = ctx + mod_c[2] * rms_norm(mix_c, mix_post_norm[l])
            ffn_c = conv_glu(modulate(rms_norm(ctx, ffn_pre_norm[l]), mod_c[3], mod_c[4]),
                             w_up[l], ffn_conv_w[l], ffn_conv_b[l], w_down[l])
            ctx = ctx + mod_c[5] * rms_norm(ffn_c, ffn_post_norm[l])

        ffn_x = conv_glu(modulate(rms_norm(x, ffn_pre_norm[l]), mod_x[3], mod_x[4]),
                         w_up[l], ffn_conv_w[l], ffn_conv_b[l], w_down[l])
        x = x + mod_x[5] * rms_norm(ffn_x, ffn_post_norm[l])
    return x


import jax as _jax
import jax.numpy as _jnp

TWIN_FORMAT = 'train_step'
FWD_PARAMS = ['x', 'c', 'ctx', 'c_ctx', 'w_mod', 'b_mod', 'mix_pre_norm', 'mix_post_norm', 'w_in', 'q_norm', 'w_q_up', 'kv_norm', 'w_kv_up', 'ssd_conv_w', 'ssd_conv_b', 'ssd_a_log', 'ssd_dt_bias', 'ssd_d', 'ssd_norm', 'w_out', 'ffn_pre_norm', 'ffn_post_norm', 'w_up', 'ffn_conv_w', 'ffn_conv_b', 'w_down']
TWIN_WEIGHTS = ['c_ctx', 'w_mod', 'b_mod', 'mix_pre_norm', 'mix_post_norm', 'w_in', 'q_norm', 'w_q_up', 'kv_norm', 'w_kv_up', 'ssd_conv_w', 'ssd_conv_b', 'ssd_a_log', 'ssd_dt_bias', 'ssd_d', 'ssd_norm', 'w_out', 'ffn_pre_norm', 'ffn_post_norm', 'w_up', 'ffn_conv_w', 'ffn_conv_b', 'w_down']
TWIN_DIFF_INPUT = 'x'
TWIN_INPUTS = ['x', 'c', 'ctx', 'c_ctx', 'w_mod', 'b_mod', 'mix_pre_norm', 'mix_post_norm', 'w_in', 'q_norm', 'w_q_up', 'kv_norm', 'w_kv_up', 'ssd_conv_w', 'ssd_conv_b', 'ssd_a_log', 'ssd_dt_bias', 'ssd_d', 'ssd_norm', 'w_out', 'ffn_pre_norm', 'ffn_post_norm', 'w_up', 'ffn_conv_w', 'ffn_conv_b', 'w_down', 'loss_target', 'm_c_ctx', 'm_w_mod', 'm_b_mod', 'm_mix_pre_norm', 'm_mix_post_norm', 'm_w_in', 'm_q_norm', 'm_w_q_up', 'm_kv_norm', 'm_w_kv_up', 'm_ssd_conv_w', 'm_ssd_conv_b', 'm_ssd_a_log', 'm_ssd_dt_bias', 'm_ssd_d', 'm_ssd_norm', 'm_w_out', 'm_ffn_pre_norm', 'm_ffn_post_norm', 'm_w_up', 'm_ffn_conv_w', 'm_ffn_conv_b', 'm_w_down', 'v_c_ctx', 'v_w_mod', 'v_b_mod', 'v_mix_pre_norm', 'v_mix_post_norm', 'v_w_in', 'v_q_norm', 'v_w_q_up', 'v_kv_norm', 'v_w_kv_up', 'v_ssd_conv_w', 'v_ssd_conv_b', 'v_ssd_a_log', 'v_ssd_dt_bias', 'v_ssd_d', 'v_ssd_norm', 'v_w_out', 'v_ffn_pre_norm', 'v_ffn_post_norm', 'v_w_up', 'v_ffn_conv_w', 'v_ffn_conv_b', 'v_w_down']
TWIN_OUTPUTS = ['loss', 'grad_x', 'grad_c_ctx', 'grad_w_mod', 'grad_b_mod', 'grad_mix_pre_norm', 'grad_mix_post_norm', 'grad_w_in', 'grad_q_norm', 'grad_w_q_up', 'grad_kv_norm', 'grad_w_kv_up', 'grad_ssd_conv_w', 'grad_ssd_conv_b', 'grad_ssd_a_log', 'grad_ssd_dt_bias', 'grad_ssd_d', 'grad_ssd_norm', 'grad_w_out', 'grad_ffn_pre_norm', 'grad_ffn_post_norm', 'grad_w_up', 'grad_ffn_conv_w', 'grad_ffn_conv_b', 'grad_w_down', 'delta_c_ctx', 'delta_w_mod', 'delta_b_mod', 'delta_mix_pre_norm', 'delta_mix_post_norm', 'delta_w_in', 'delta_q_norm', 'delta_w_q_up', 'delta_kv_norm', 'delta_w_kv_up', 'delta_ssd_conv_w', 'delta_ssd_conv_b', 'delta_ssd_a_log', 'delta_ssd_dt_bias', 'delta_ssd_d', 'delta_ssd_norm', 'delta_w_out', 'delta_ffn_pre_norm', 'delta_ffn_post_norm', 'delta_w_up', 'delta_ffn_conv_w', 'delta_ffn_conv_b', 'delta_w_down', 'new_m_c_ctx', 'new_m_w_mod', 'new_m_b_mod', 'new_m_mix_pre_norm', 'new_m_mix_post_norm', 'new_m_w_in', 'new_m_q_norm', 'new_m_w_q_up', 'new_m_kv_norm', 'new_m_w_kv_up', 'new_m_ssd_conv_w', 'new_m_ssd_conv_b', 'new_m_ssd_a_log', 'new_m_ssd_dt_bias', 'new_m_ssd_d', 'new_m_ssd_norm', 'new_m_w_out', 'new_m_ffn_pre_norm', 'new_m_ffn_post_norm', 'new_m_w_up', 'new_m_ffn_conv_w', 'new_m_ffn_conv_b', 'new_m_w_down', 'new_v_c_ctx', 'new_v_w_mod', 'new_v_b_mod', 'new_v_mix_pre_norm', 'new_v_mix_post_norm', 'new_v_w_in', 'new_v_q_norm', 'new_v_w_q_up', 'new_v_kv_norm', 'new_v_w_kv_up', 'new_v_ssd_conv_w', 'new_v_ssd_conv_b', 'new_v_ssd_a_log', 'new_v_ssd_dt_bias', 'new_v_ssd_d', 'new_v_ssd_norm', 'new_v_w_out', 'new_v_ffn_pre_norm', 'new_v_ffn_post_norm', 'new_v_w_up', 'new_v_ffn_conv_w', 'new_v_ffn_conv_b', 'new_v_w_down']
TWIN_LEAF_KINDS = {'loss': 'loss', 'grad_x': 'grad_x', 'grad_c_ctx': 'grad_w', 'grad_w_mod': 'grad_w', 'grad_b_mod': 'grad_w', 'grad_mix_pre_norm': 'grad_w', 'grad_mix_post_norm': 'grad_w', 'grad_w_in': 'grad_w', 'grad_q_norm': 'grad_w', 'grad_w_q_up': 'grad_w', 'grad_kv_norm': 'grad_w', 'grad_w_kv_up': 'grad_w', 'grad_ssd_conv_w': 'grad_w', 'grad_ssd_conv_b': 'grad_w', 'grad_ssd_a_log': 'grad_w', 'grad_ssd_dt_bias': 'grad_w', 'grad_ssd_d': 'grad_w', 'grad_ssd_norm': 'grad_w', 'grad_w_out': 'grad_w', 'grad_ffn_pre_norm': 'grad_w', 'grad_ffn_post_norm': 'grad_w', 'grad_w_up': 'grad_w', 'grad_ffn_conv_w': 'grad_w', 'grad_ffn_conv_b': 'grad_w', 'grad_w_down': 'grad_w', 'delta_c_ctx': 'delta_w', 'delta_w_mod': 'delta_w', 'delta_b_mod': 'delta_w', 'delta_mix_pre_norm': 'delta_w', 'delta_mix_post_norm': 'delta_w', 'delta_w_in': 'delta_w', 'delta_q_norm': 'delta_w', 'delta_w_q_up': 'delta_w', 'delta_kv_norm': 'delta_w', 'delta_w_kv_up': 'delta_w', 'delta_ssd_conv_w': 'delta_w', 'delta_ssd_conv_b': 'delta_w', 'delta_ssd_a_log': 'delta_w', 'delta_ssd_dt_bias': 'delta_w', 'delta_ssd_d': 'delta_w', 'delta_ssd_norm': 'delta_w', 'delta_w_out': 'delta_w', 'delta_ffn_pre_norm': 'delta_w', 'delta_ffn_post_norm': 'delta_w', 'delta_w_up': 'delta_w', 'delta_ffn_conv_w': 'delta_w', 'delta_ffn_conv_b': 'delta_w', 'delta_w_down': 'delta_w', 'new_m_c_ctx': 'new_m', 'new_m_w_mod': 'new_m', 'new_m_b_mod': 'new_m', 'new_m_mix_pre_norm': 'new_m', 'new_m_mix_post_norm': 'new_m', 'new_m_w_in': 'new_m', 'new_m_q_norm': 'new_m', 'new_m_w_q_up': 'new_m', 'new_m_kv_norm': 'new_m', 'new_m_w_kv_up': 'new_m', 'new_m_ssd_conv_w': 'new_m', 'new_m_ssd_conv_b': 'new_m', 'new_m_ssd_a_log': 'new_m', 'new_m_ssd_dt_bias': 'new_m', 'new_m_ssd_d': 'new_m', 'new_m_ssd_norm': 'new_m', 'new_m_w_out': 'new_m', 'new_m_ffn_pre_norm': 'new_m', 'new_m_ffn_post_norm': 'new_m', 'new_m_w_up': 'new_m', 'new_m_ffn_conv_w': 'new_m', 'new_m_ffn_conv_b': 'new_m', 'new_m_w_down': 'new_m', 'new_v_c_ctx': 'new_v', 'new_v_w_mod': 'new_v', 'new_v_b_mod': 'new_v', 'new_v_mix_pre_norm': 'new_v', 'new_v_mix_post_norm': 'new_v', 'new_v_w_in': 'new_v', 'new_v_q_norm': 'new_v', 'new_v_w_q_up': 'new_v', 'new_v_kv_norm': 'new_v', 'new_v_w_kv_up': 'new_v', 'new_v_ssd_conv_w': 'new_v', 'new_v_ssd_conv_b': 'new_v', 'new_v_ssd_a_log': 'new_v', 'new_v_ssd_dt_bias': 'new_v', 'new_v_ssd_d': 'new_v', 'new_v_ssd_norm': 'new_v', 'new_v_w_out': 'new_v', 'new_v_ffn_pre_norm': 'new_v', 'new_v_ffn_post_norm': 'new_v', 'new_v_w_up': 'new_v', 'new_v_ffn_conv_w': 'new_v', 'new_v_ffn_conv_b': 'new_v', 'new_v_w_down': 'new_v'}


def _forward(args):
    return _fwd_reference(*[args[k] for k in FWD_PARAMS])


def _output_shape():
    out = _jax.eval_shape(lambda: _forward(_fwd_setup_inputs(0)))
    return out.shape, out.dtype

N_MICROBATCH = 1
ADAM_LR = 0.001
ADAM_B1 = 0.9
ADAM_B2 = 0.999
ADAM_EPS = 1e-08
ADAM_WD = 0.01
ADAM_STEP = 10
PER_EXAMPLE_BATCH_AXIS = {'x': 0, 'c': 0, 'ctx': 0, 'loss_target': 0}
SHARED_INPUTS = []
_WEIGHT_DTYPES = {'c_ctx': _jnp.float32, 'w_mod': _jnp.float32, 'b_mod': _jnp.float32, 'mix_pre_norm': _jnp.float32, 'mix_post_norm': _jnp.float32, 'w_in': _jnp.float32, 'q_norm': _jnp.float32, 'w_q_up': _jnp.float32, 'kv_norm': _jnp.float32, 'w_kv_up': _jnp.float32, 'ssd_conv_w': _jnp.float32, 'ssd_conv_b': _jnp.float32, 'ssd_a_log': _jnp.float32, 'ssd_dt_bias': _jnp.float32, 'ssd_d': _jnp.float32, 'ssd_norm': _jnp.float32, 'w_out': _jnp.float32, 'ffn_pre_norm': _jnp.float32, 'ffn_post_norm': _jnp.float32, 'w_up': _jnp.float32, 'ffn_conv_w': _jnp.float32, 'ffn_conv_b': _jnp.float32, 'w_down': _jnp.float32}
MOMENT_SCALE = {'c_ctx': 3.356604e-01, 'w_mod': 8.363755e+00, 'b_mod': 1.511280e+01, 'mix_pre_norm': 5.245446e-01, 'mix_post_norm': 3.065415e+01, 'w_in': 2.872243e+00, 'q_norm': 3.659152e-01, 'w_q_up': 1.583766e-01, 'kv_norm': 7.274366e+00, 'w_kv_up': 2.607830e+00, 'ssd_conv_w': 2.713574e+00, 'ssd_conv_b': 3.585095e+00, 'ssd_a_log': 1.051257e+01, 'ssd_dt_bias': 7.172658e-01, 'ssd_d': 3.326186e+00, 'ssd_norm': 3.306988e+00, 'w_out': 6.469214e+00, 'ffn_pre_norm': 1.285512e+00, 'ffn_post_norm': 3.116619e+01, 'w_up': 1.477991e+00, 'ffn_conv_w': 1.502333e+00, 'ffn_conv_b': 1.652477e+00, 'w_down': 3.227645e+00}


def _to_microbatches(a, axis):
    t = _jnp.moveaxis(a, axis, 0)
    t = t.reshape((N_MICROBATCH, t.shape[0] // N_MICROBATCH) + t.shape[1:])
    return _jnp.moveaxis(t, 1, axis + 1)


def setup_inputs(seed: int = 0) -> dict:
    inp = _fwd_setup_inputs(seed)
    key = _jax.random.fold_in(_jax.random.key(seed), 7919)
    shape, _ = _output_shape()
    out = dict(inp)
    out["loss_target"] = _jax.random.normal(_jax.random.fold_in(key, 0), shape, _jnp.float32)
    for i, name in enumerate(TWIN_WEIGHTS):
        w = inp[name].astype(_jnp.float32)
        if MOMENT_SCALE is None:
            s = _jnp.sqrt(_jnp.mean(_jnp.square(w)) + 1e-30)
        else:
            s = MOMENT_SCALE[name]
        km, kv = _jax.random.split(_jax.random.fold_in(key, i + 1))
        out[name] = w
        out["m_" + name] = s * _jax.random.normal(km, w.shape, _jnp.float32)
        out["v_" + name] = (s * s) * _jax.random.uniform(kv, w.shape, _jnp.float32, 0.5, 1.5)
    if N_MICROBATCH > 1:
        for name, axis in PER_EXAMPLE_BATCH_AXIS.items():
            out[name] = _to_microbatches(out[name], axis)
    return {'x': out['x'], 'c': out['c'], 'ctx': out['ctx'], 'c_ctx': out['c_ctx'], 'w_mod': out['w_mod'], 'b_mod': out['b_mod'], 'mix_pre_norm': out['mix_pre_norm'], 'mix_post_norm': out['mix_post_norm'], 'w_in': out['w_in'], 'q_norm': out['q_norm'], 'w_q_up': out['w_q_up'], 'kv_norm': out['kv_norm'], 'w_kv_up': out['w_kv_up'], 'ssd_conv_w': out['ssd_conv_w'], 'ssd_conv_b': out['ssd_conv_b'], 'ssd_a_log': out['ssd_a_log'], 'ssd_dt_bias': out['ssd_dt_bias'], 'ssd_d': out['ssd_d'], 'ssd_norm': out['ssd_norm'], 'w_out': out['w_out'], 'ffn_pre_norm': out['ffn_pre_norm'], 'ffn_post_norm': out['ffn_post_norm'], 'w_up': out['w_up'], 'ffn_conv_w': out['ffn_conv_w'], 'ffn_conv_b': out['ffn_conv_b'], 'w_down': out['w_down'], 'loss_target': out['loss_target'], 'm_c_ctx': out['m_c_ctx'], 'm_w_mod': out['m_w_mod'], 'm_b_mod': out['m_b_mod'], 'm_mix_pre_norm': out['m_mix_pre_norm'], 'm_mix_post_norm': out['m_mix_post_norm'], 'm_w_in': out['m_w_in'], 'm_q_norm': out['m_q_norm'], 'm_w_q_up': out['m_w_q_up'], 'm_kv_norm': out['m_kv_norm'], 'm_w_kv_up': out['m_w_kv_up'], 'm_ssd_conv_w': out['m_ssd_conv_w'], 'm_ssd_conv_b': out['m_ssd_conv_b'], 'm_ssd_a_log': out['m_ssd_a_log'], 'm_ssd_dt_bias': out['m_ssd_dt_bias'], 'm_ssd_d': out['m_ssd_d'], 'm_ssd_norm': out['m_ssd_norm'], 'm_w_out': out['m_w_out'], 'm_ffn_pre_norm': out['m_ffn_pre_norm'], 'm_ffn_post_norm': out['m_ffn_post_norm'], 'm_w_up': out['m_w_up'], 'm_ffn_conv_w': out['m_ffn_conv_w'], 'm_ffn_conv_b': out['m_ffn_conv_b'], 'm_w_down': out['m_w_down'], 'v_c_ctx': out['v_c_ctx'], 'v_w_mod': out['v_w_mod'], 'v_b_mod': out['v_b_mod'], 'v_mix_pre_norm': out['v_mix_pre_norm'], 'v_mix_post_norm': out['v_mix_post_norm'], 'v_w_in': out['v_w_in'], 'v_q_norm': out['v_q_norm'], 'v_w_q_up': out['v_w_q_up'], 'v_kv_norm': out['v_kv_norm'], 'v_w_kv_up': out['v_w_kv_up'], 'v_ssd_conv_w': out['v_ssd_conv_w'], 'v_ssd_conv_b': out['v_ssd_conv_b'], 'v_ssd_a_log': out['v_ssd_a_log'], 'v_ssd_dt_bias': out['v_ssd_dt_bias'], 'v_ssd_d': out['v_ssd_d'], 'v_ssd_norm': out['v_ssd_norm'], 'v_w_out': out['v_w_out'], 'v_ffn_pre_norm': out['v_ffn_pre_norm'], 'v_ffn_post_norm': out['v_ffn_post_norm'], 'v_w_up': out['v_w_up'], 'v_ffn_conv_w': out['v_ffn_conv_w'], 'v_ffn_conv_b': out['v_ffn_conv_b'], 'v_w_down': out['v_w_down']}


def _loss(weights, diff, rest, loss_target):
    with _jax.named_scope("forward"):
        args = {**rest, TWIN_DIFF_INPUT: diff, **{k: w.astype(_WEIGHT_DTYPES[k]) for k, w in weights.items()}}
        y = _forward(args)
    with _jax.named_scope("loss_head"):
        err = _jnp.square(y.astype(_jnp.float32) - loss_target)
        return 0.5 * _jnp.sum(_jnp.mean(err, axis=-1)) if err.ndim else 0.5 * err


def _adamw(w, g, m, v):
    m = ADAM_B1 * m + (1.0 - ADAM_B1) * g
    v = ADAM_B2 * v + (1.0 - ADAM_B2) * _jnp.square(g)
    m_hat = m / (1.0 - ADAM_B1 ** ADAM_STEP)
    v_hat = v / (1.0 - ADAM_B2 ** ADAM_STEP)
    delta = -ADAM_LR * (m_hat / (_jnp.sqrt(v_hat) + ADAM_EPS) + ADAM_WD * w)
    return delta, m, v


def reference(x, c, ctx, c_ctx, w_mod, b_mod, mix_pre_norm, mix_post_norm, w_in, q_norm, w_q_up, kv_norm, w_kv_up, ssd_conv_w, ssd_conv_b, ssd_a_log, ssd_dt_bias, ssd_d, ssd_norm, w_out, ffn_pre_norm, ffn_post_norm, w_up, ffn_conv_w, ffn_conv_b, w_down, loss_target, m_c_ctx, m_w_mod, m_b_mod, m_mix_pre_norm, m_mix_post_norm, m_w_in, m_q_norm, m_w_q_up, m_kv_norm, m_w_kv_up, m_ssd_conv_w, m_ssd_conv_b, m_ssd_a_log, m_ssd_dt_bias, m_ssd_d, m_ssd_norm, m_w_out, m_ffn_pre_norm, m_ffn_post_norm, m_w_up, m_ffn_conv_w, m_ffn_conv_b, m_w_down, v_c_ctx, v_w_mod, v_b_mod, v_mix_pre_norm, v_mix_post_norm, v_w_in, v_q_norm, v_w_q_up, v_kv_norm, v_w_kv_up, v_ssd_conv_w, v_ssd_conv_b, v_ssd_a_log, v_ssd_dt_bias, v_ssd_d, v_ssd_norm, v_w_out, v_ffn_pre_norm, v_ffn_post_norm, v_w_up, v_ffn_conv_w, v_ffn_conv_b, v_w_down):
    given = dict(x=x, c=c, ctx=ctx, c_ctx=c_ctx, w_mod=w_mod, b_mod=b_mod, mix_pre_norm=mix_pre_norm, mix_post_norm=mix_post_norm, w_in=w_in, q_norm=q_norm, w_q_up=w_q_up, kv_norm=kv_norm, w_kv_up=w_kv_up, ssd_conv_w=ssd_conv_w, ssd_conv_b=ssd_conv_b, ssd_a_log=ssd_a_log, ssd_dt_bias=ssd_dt_bias, ssd_d=ssd_d, ssd_norm=ssd_norm, w_out=w_out, ffn_pre_norm=ffn_pre_norm, ffn_post_norm=ffn_post_norm, w_up=w_up, ffn_conv_w=ffn_conv_w, ffn_conv_b=ffn_conv_b, w_down=w_down, loss_target=loss_target, m_c_ctx=m_c_ctx, m_w_mod=m_w_mod, m_b_mod=m_b_mod, m_mix_pre_norm=m_mix_pre_norm, m_mix_post_norm=m_mix_post_norm, m_w_in=m_w_in, m_q_norm=m_q_norm, m_w_q_up=m_w_q_up, m_kv_norm=m_kv_norm, m_w_kv_up=m_w_kv_up, m_ssd_conv_w=m_ssd_conv_w, m_ssd_conv_b=m_ssd_conv_b, m_ssd_a_log=m_ssd_a_log, m_ssd_dt_bias=m_ssd_dt_bias, m_ssd_d=m_ssd_d, m_ssd_norm=m_ssd_norm, m_w_out=m_w_out, m_ffn_pre_norm=m_ffn_pre_norm, m_ffn_post_norm=m_ffn_post_norm, m_w_up=m_w_up, m_ffn_conv_w=m_ffn_conv_w, m_ffn_conv_b=m_ffn_conv_b, m_w_down=m_w_down, v_c_ctx=v_c_ctx, v_w_mod=v_w_mod, v_b_mod=v_b_mod, v_mix_pre_norm=v_mix_pre_norm, v_mix_post_norm=v_mix_post_norm, v_w_in=v_w_in, v_q_norm=v_q_norm, v_w_q_up=v_w_q_up, v_kv_norm=v_kv_norm, v_w_kv_up=v_w_kv_up, v_ssd_conv_w=v_ssd_conv_w, v_ssd_conv_b=v_ssd_conv_b, v_ssd_a_log=v_ssd_a_log, v_ssd_dt_bias=v_ssd_dt_bias, v_ssd_d=v_ssd_d, v_ssd_norm=v_ssd_norm, v_w_out=v_w_out, v_ffn_pre_norm=v_ffn_pre_norm, v_ffn_post_norm=v_ffn_post_norm, v_w_up=v_w_up, v_ffn_conv_w=v_ffn_conv_w, v_ffn_conv_b=v_ffn_conv_b, v_w_down=v_w_down)
    weights = {n: given[n] for n in TWIN_WEIGHTS}
    shared = {n: given[n] for n in SHARED_INPUTS}
    per_example = {n: given[n] for n in ['x', 'c', 'ctx']}
    grad_fn = _jax.value_and_grad(_loss, argnums=(0, 1))

    def one_microbatch(ex, loss_target):
        ex = dict(ex)
        diff = ex.pop(TWIN_DIFF_INPUT)
        return grad_fn(weights, diff, {**shared, **ex}, loss_target)

    if N_MICROBATCH == 1:
        loss, (grad_w, grad_x) = one_microbatch(per_example, given["loss_target"])
    else:
        def body(carry, xs):
            loss_sum, grad_sum = carry
            l_k, (gw_k, gx_k) = one_microbatch(xs[0], xs[1])
            with _jax.named_scope("update"):
                return (loss_sum + l_k, _jax.tree.map(_jnp.add, grad_sum, gw_k)), gx_k

        init = (_jnp.zeros((), _jnp.float32), _jax.tree.map(_jnp.zeros_like, weights))
        (loss, grad_w), grad_x = _jax.lax.scan(body, init, (per_example, given["loss_target"]))
    with _jax.named_scope("update"):
        delta_w, new_m, new_v = {}, {}, {}
        for n in TWIN_WEIGHTS:
            delta_w[n], new_m[n], new_v[n] = _adamw(weights[n], grad_w[n], given["m_" + n], given["v_" + n])
    return (loss, grad_x, *[grad_w[n] for n in TWIN_WEIGHTS], *[delta_w[n] for n in TWIN_WEIGHTS],
            *[new_m[n] for n in TWIN_WEIGHTS], *[new_v[n] for n in TWIN_WEIGHTS])
```

```python
import functools
import math

import jax
import jax.numpy as jnp
import numpy as np
from jax import lax
from jax.experimental import pallas as pl
from jax.experimental.pallas import tpu as pltpu

F32 = jnp.float32
BF16 = jnp.bfloat16
MESH = pl.DeviceIdType.MESH

D_MODEL = 1024
GRID_W = 64
N_HEADS = 16
NOPE = 64
ROPE = 32
V_DIM = 64
Q_RANK = 384
KV_RANK = 256
ROPE_THETA = 10000.0
ATTN_SCALE = (NOPE + ROPE) ** -0.5
SSD_HEADS = 16
SSD_P = 64
SSD_GROUPS = 2
SSD_N = 128
SSD_K = 5
CHUNK = 128
D_INNER = SSD_HEADS * SSD_P
GN = SSD_GROUPS * SSD_N
XBC = D_INNER + 2 * GN
D_FF = 2816
FFN_K = 3
N_MOD = 6
EPS = 1e-6
IN_SPLITS = (Q_RANK, KV_RANK, ROPE, D_INNER, XBC, 2 * SSD_HEADS)
IN_WIDTH = sum(IN_SPLITS)
N_DEV = 8

ADAM_LR = 0.001
ADAM_B1 = 0.9
ADAM_B2 = 0.999
ADAM_EPS = 1e-08
ADAM_WD = 0.01
ADAM_STEP = 10

LANE = 128
HEAD_BLOCK = 128
OFF_CQ = 0
OFF_KR = 384
OFF_CKV = 512
OFF_Z = 1024
OFF_XBC = 2048
OFF_DT = 3584
WIN_P = 3712
KR_LANE = 64
QP = N_HEADS * HEAD_BLOCK

VMEM_LIMIT_V7X = 56 * 1024 * 1024
NEG_BIG = -1e30


def _cparams(*sem):
    return pltpu.CompilerParams(dimension_semantics=sem, vmem_limit_bytes=VMEM_LIMIT_V7X)


def _tile(n, target, mult=128):
    if n <= target:
        return n
    t = (target // mult) * mult
    while t >= mult:
        if n % t == 0:
            return t
        t -= mult
    return n


def _silu(x):
    return x * jax.nn.sigmoid(x)


def _rms(x, g):
    return x * lax.rsqrt(jnp.mean(x * x, axis=-1, keepdims=True) + EPS) * g


def matmul(name, pairs, mode, out_dtype, *, bias=None, silu_a=False, tm=512, tn=512):
    n_pairs = len(pairs)
    M = pairs[0][0].shape[0]
    N = pairs[0][1].shape[1] if mode == "nn" else pairs[0][1].shape[0]
    tm = _tile(M, tm, 8)
    tn = _tile(N, tn)
    dims = (((1,), (0,)), ((), ())) if mode == "nn" else (((1,), (1,)), ((), ()))

    def body(*refs):
        o_ref = refs[-1]
        acc = None
        for p in range(n_pairs):
            a = refs[2 * p][...]
            if silu_a:
                a = _silu(a.astype(F32))
            d = lax.dot_general(a.astype(BF16), refs[2 * p + 1][...].astype(BF16), dims, preferred_element_type=F32)
            acc = d if acc is None else acc + d
        if bias is not None:
            acc = acc + refs[2 * n_pairs][...]
        o_ref[...] = acc.astype(o_ref.dtype)

    in_specs, args = [], []
    for a, b in pairs:
        K = a.shape[1]
        in_specs.append(pl.BlockSpec((tm, K), lambda j, i: (i, 0)))
        in_specs.append(pl.BlockSpec((K, tn), lambda j, i: (0, j)) if mode == "nn" else pl.BlockSpec((tn, K), lambda j, i: (j, 0)))
        args += [a, b]
    if bias is not None:
        in_specs.append(pl.BlockSpec((1, tn), lambda j, i: (0, j)))
        args.append(bias)
    return pl.pallas_call(
        body, name=name, grid=(N // tn, M // tm), in_specs=in_specs,
        out_specs=pl.BlockSpec((tm, tn), lambda j, i: (i, j)),
        out_shape=jax.ShapeDtypeStruct((M, N), out_dtype),
        compiler_params=_cparams("arbitrary", "arbitrary"),
    )(*args)


def matmul_tn(name, a, b, out_dtype=F32, *, silu_a=False, tm=512, tn=512, tk=512):
    R, M = a.shape
    N = b.shape[1]
    tm = _tile(M, tm)
    tn = _tile(N, tn)
    tk = _tile(R, tk, 8)
    nk = R // tk

    def body(a_ref, b_ref, o_ref, acc):
        k = pl.program_id(2)

        @pl.when(k == 0)
        def _():
            acc[...] = jnp.zeros_like(acc)

        x = a_ref[...]
        if silu_a:
            x = _silu(x.astype(F32))
        acc[...] += lax.dot_general(x.astype(BF16), b_ref[...].astype(BF16), (((0,), (0,)), ((), ())),
                                    preferred_element_type=F32)

        @pl.when(k == nk - 1)
        def _():
            o_ref[...] = acc[...].astype(o_ref.dtype)

    return pl.pallas_call(
        body, name=name, grid=(M // tm, N // tn, nk),
        in_specs=[pl.BlockSpec((tk, tm), lambda i, j, k: (k, i)), pl.BlockSpec((tk, tn), lambda i, j, k: (k, j))],
        out_specs=pl.BlockSpec((tm, tn), lambda i, j, k: (i, j)),
        out_shape=jax.ShapeDtypeStruct((M, N), out_dtype),
        scratch_shapes=[pltpu.VMEM((tm, tn), F32)],
        compiler_params=_cparams("arbitrary", "arbitrary", "arbitrary"),
    )(a, b)


def _row_specs(rin, pbin, glin, tr):
    specs = [pl.BlockSpec((1, tr, w), lambda b, i, cb=cb, ro=ro: (b, i + ro, cb)) for (_, w, cb, ro) in rin]
    specs += [pl.BlockSpec((1, 1, a.shape[-1]), lambda b, i: (b, 0, 0)) for a in pbin]
    specs += [pl.BlockSpec((1, a.shape[-1]), lambda b, i: (0, 0)) for a in glin]
    return specs


def rows_fwd(name, fn, nb, nblk, tr, rin, pbin, glin, outs):
    nr, npb, ngl = len(rin), len(pbin), len(glin)
    n_in = nr + npb + ngl

    def body(*refs):
        args = [r[0].astype(F32) for r in refs[:nr + npb]] + [r[...] for r in refs[nr + npb:n_in]]
        res = fn(*args)
        for o, v in zip(refs[n_in:], res):
            o[0] = v.astype(o.dtype)

    return pl.pallas_call(
        body, name=name, grid=(nb, nblk), in_specs=_row_specs(rin, pbin, glin, tr),
        out_specs=[pl.BlockSpec((1, tr, w), lambda b, i: (b, i, 0)) for (w, _) in outs],
        out_shape=[jax.ShapeDtypeStruct((nb, nblk * tr, w), dt) for (w, dt) in outs],
        compiler_params=_cparams("arbitrary", "arbitrary"),
    )(*[a for (a, _, _, _) in rin], *pbin, *glin)


def rows_bwd(name, fn, nb, nblk, tr, rin, pbin, glin, cts, want):
    nr, npb, ngl, nct = len(rin), len(pbin), len(glin), len(cts)
    n_in = nr + npb + ngl

    def body(*refs):
        b, i = pl.program_id(0), pl.program_id(1)
        args = [r[0].astype(F32) for r in refs[:nr + npb]] + [r[...] for r in refs[nr + npb:n_in]]
        ct = tuple(r[0].astype(F32) for r in refs[n_in:n_in + nct])
        _, vjp = jax.vjp(fn, *args)
        g = vjp(ct)
        orefs = refs[n_in + nct:]
        for o, (idx, _) in zip(orefs, want):
            o[0] = g[idx].astype(o.dtype)
        pb_refs = orefs[len(want):len(want) + npb]
        gl_refs = orefs[len(want) + npb:]

        @pl.when(i == 0)
        def _():
            for o, v in zip(pb_refs, g[nr:nr + npb]):
                o[0] = v

        @pl.when(i > 0)
        def _():
            for o, v in zip(pb_refs, g[nr:nr + npb]):
                o[0] += v

        first = jnp.logical_and(b == 0, i == 0)

        @pl.when(first)
        def _():
            for o, v in zip(gl_refs, g[nr + npb:]):
                o[...] = v

        @pl.when(jnp.logical_not(first))
        def _():
            for o, v in zip(gl_refs, g[nr + npb:]):
                o[...] += v

    out_specs = [pl.BlockSpec((1, tr, rin[idx][1]), lambda b, i: (b, i, 0)) for (idx, _) in want]
    out_shape = [jax.ShapeDtypeStruct((nb, nblk * tr, rin[idx][1]), dt) for (idx, dt) in want]
    out_specs += [pl.BlockSpec((1, 1, a.shape[-1]), lambda b, i: (b, 0, 0)) for a in pbin]
    out_shape += [jax.ShapeDtypeStruct((nb, 1, a.shape[-1]), F32) for a in pbin]
    out_specs += [pl.BlockSpec((1, a.shape[-1]), lambda b, i: (0, 0)) for a in glin]
    out_shape += [jax.ShapeDtypeStruct((1, a.shape[-1]), F32) for a in glin]
    return pl.pallas_call(
        body, name=name, grid=(nb, nblk),
        in_specs=_row_specs(rin, pbin, glin, tr) + _row_specs(cts, [], [], tr),
        out_specs=out_specs, out_shape=out_shape,
        compiler_params=_cparams("arbitrary", "arbitrary"),
    )(*[a for (a, _, _, _) in rin], *pbin, *glin, *[a for (a, _, _, _) in cts])


def ew_call(name, fn, ins, outs):
    def body(*refs):
        res = fn(*[r[...] for r in refs[:len(ins)]])
        for o, v in zip(refs[len(ins):], res):
            o[...] = v.astype(o.dtype)

    return pl.pallas_call(body, name=name, out_shape=[jax.ShapeDtypeStruct(s, dt) for (s, dt) in outs])(*ins)


def fn_prenorm(x, shift, scale, g):
    return (_rms(x, g) * (1.0 + scale) + shift,)


def fn_rms(x, g):
    return (_rms(x, g),)


def fn_ssd_finish(yf, yr, xs, z, dexp, nw):
    y = yf + yr + dexp * xs
    return (_rms(y * _silu(z), nw),)


def fn_postmix(x, mix, gate1, scale2, shift2, post_g, pre_g):
    x1 = x + gate1 * _rms(mix, post_g)
    h2 = _rms(x1, pre_g) * (1.0 + scale2) + shift2
    return x1, h2


def final_call(x1, ffn, target, gate2, post_g, tr):
    nb, S, D = x1.shape
    nblk = S // tr

    def body(x1_ref, f_ref, t_ref, g2_ref, pg_ref, dx1_ref, df_ref, dg2_ref, dpg_ref, loss_ref):
        b, i = pl.program_id(0), pl.program_id(1)
        tgt = t_ref[0]

        def lossfn(x1v, fv, g2, pg):
            e = x1v + g2 * _rms(fv, pg) - tgt
            return 0.5 * jnp.sum(jnp.mean(e * e, axis=-1, keepdims=True))

        val, (dx1, df, dg2, dpg) = jax.value_and_grad(lossfn, argnums=(0, 1, 2, 3))(
            x1_ref[0], f_ref[0].astype(F32), g2_ref[0], pg_ref[...])
        dx1_ref[0] = dx1
        df_ref[0] = df.astype(df_ref.dtype)
        lv = jnp.full((1, LANE), val, F32)

        @pl.when(i == 0)
        def _():
            dg2_ref[0] = dg2

        @pl.when(i > 0)
        def _():
            dg2_ref[0] += dg2

        first = jnp.logical_and(b == 0, i == 0)

        @pl.when(first)
        def _():
            dpg_ref[...] = dpg
            loss_ref[...] = lv

        @pl.when(jnp.logical_not(first))
        def _():
            dpg_ref[...] += dpg
            loss_ref[...] += lv

    row = pl.BlockSpec((1, tr, D), lambda b, i: (b, i, 0))
    pb = pl.BlockSpec((1, 1, D), lambda b, i: (b, 0, 0))
    gl = pl.BlockSpec((1, D), lambda b, i: (0, 0))
    return pl.pallas_call(
        body, name="loss_head", grid=(nb, nblk), in_specs=[row, row, row, pb, gl],
        out_specs=[row, row, pb, gl, pl.BlockSpec((1, LANE), lambda b, i: (0, 0))],
        out_shape=[jax.ShapeDtypeStruct((nb, S, D), F32), jax.ShapeDtypeStruct((nb, S, D), BF16),
                   jax.ShapeDtypeStruct((nb, 1, D), F32), jax.ShapeDtypeStruct((1, D), F32),
                   jax.ShapeDtypeStruct((1, LANE), F32)],
        compiler_params=_cparams("arbitrary", "arbitrary"),
    )(x1, ffn, target, gate2, post_g)


def _rotate_half(t):
    lane = lax.broadcasted_iota(jnp.int32, t.shape, 1)
    return jnp.where((lane & 15) < 8, -pltpu.roll(t, LANE - 8, 1), pltpu.roll(t, 8, 1))


def rope_call(name, x, width, colblk, cos, sin, out_dtype, tr):
    nb = x.shape[0]
    R = cos.shape[0]
    nblk = R // tr

    def body(x_ref, c_ref, s_ref, o_ref):
        c, s = c_ref[...], s_ref[...]
        for h in range(width // LANE):
            t = x_ref[0, :, h * LANE:(h + 1) * LANE].astype(F32)
            o_ref[0, :, h * LANE:(h + 1) * LANE] = (t * c + _rotate_half(t) * s).astype(o_ref.dtype)

    tab = pl.BlockSpec((tr, LANE), lambda b, i: (i, 0))
    return pl.pallas_call(
        body, name=name, grid=(nb, nblk),
        in_specs=[pl.BlockSpec((1, tr, width), lambda b, i: (b, i, colblk)), tab, tab],
        out_specs=pl.BlockSpec((1, tr, width), lambda b, i: (b, i, 0)),
        out_shape=jax.ShapeDtypeStruct((nb, R, width), out_dtype),
        compiler_params=_cparams("arbitrary", "arbitrary"),
    )(x, cos, sin)


def rope_tables(n_ctx, seq):
    n_rows = seq // GRID_W
    row = np.repeat(np.arange(n_rows), GRID_W).astype(np.float32)
    col = np.tile(np.arange(GRID_W), n_rows).astype(np.float32)
    axis_dim = ROPE // 2
    inv_freq = jnp.asarray(ROPE_THETA, F32) ** (-jnp.arange(0, axis_dim, 2, dtype=F32) / axis_dim)
    ang_r = jnp.asarray(row)[:, None] * inv_freq
    ang_c = jnp.asarray(col)[:, None] * inv_freq
    ang = jnp.concatenate([ang_r, ang_r, ang_c, ang_c], axis=-1)
    cos = jnp.ones((n_ctx + seq, LANE), F32).at[n_ctx:, KR_LANE:KR_LANE + ROPE].set(jnp.cos(ang))
    sin = jnp.zeros((n_ctx + seq, LANE), F32).at[n_ctx:, KR_LANE:KR_LANE + ROPE].set(jnp.sin(ang))
    return cos, sin


def _attn_probs(q, kc):
    s = lax.dot_general(q, kc, (((1,), (1,)), ((), ())), preferred_element_type=F32) * ATTN_SCALE
    p = jnp.exp(s - jnp.max(s, axis=1, keepdims=True))
    return p * (1.0 / jnp.sum(p, axis=1, keepdims=True))


def _key_block(kv, kr):
    lane = lax.broadcasted_iota(jnp.int32, kv.shape, 1)
    return jnp.where(lane < NOPE, kv, kr)


def attn_fwd(q, kv, kr, tq):
    nb, S, _ = q.shape
    T = kv.shape[1]

    def body(q_ref, kv_ref, kr_ref, o_ref):
        kvv = kv_ref[0]
        p = _attn_probs(q_ref[0], _key_block(kvv, kr_ref[0]))
        o = lax.dot_general(p.astype(BF16), kvv, (((1,), (0,)), ((), ())), preferred_element_type=F32)
        lane = lax.broadcasted_iota(jnp.int32, o.shape, 1)
        o_ref[0] = jnp.where(lane >= NOPE, o, 0.0).astype(o_ref.dtype)

    return pl.pallas_call(
        body, name="attn_fwd", grid=(nb, N_HEADS, S // tq),
        in_specs=[pl.BlockSpec((1, tq, HEAD_BLOCK), lambda b, h, i: (b, i, h)),
                  pl.BlockSpec((1, T, HEAD_BLOCK), lambda b, h, i: (b, 0, h)),
                  pl.BlockSpec((1, T, HEAD_BLOCK), lambda b, h, i: (b, 0, 0))],
        out_specs=pl.BlockSpec((1, tq, HEAD_BLOCK), lambda b, h, i: (b, i, h)),
        out_shape=jax.ShapeDtypeStruct((nb, S, QP), BF16),
        compiler_params=_cparams("arbitrary", "arbitrary", "arbitrary"),
    )(q, kv, kr)


def attn_bwd(q, kv, kr, do, tq):
    nb, S, _ = q.shape
    T = kv.shape[1]

    def body(q_ref, kv_ref, kr_ref, do_ref, dq_ref, dkv_ref, dkr_ref):
        h, i = pl.program_id(1), pl.program_id(2)
        qv, kvv, dov = q_ref[0], kv_ref[0], do_ref[0]
        kc = _key_block(kvv, kr_ref[0])
        p = _attn_probs(qv, kc)
        dp = lax.dot_general(dov, kvv, (((1,), (1,)), ((), ())), preferred_element_type=F32)
        ds = (p * (dp - jnp.sum(dp * p, axis=1, keepdims=True)) * ATTN_SCALE).astype(BF16)
        dq_ref[0] = lax.dot_general(ds, kc, (((1,), (0,)), ((), ())), preferred_element_type=F32).astype(dq_ref.dtype)
        dkc = lax.dot_general(ds, qv, (((0,), (0,)), ((), ())), preferred_element_type=F32)
        dv = lax.dot_general(p.astype(BF16), dov, (((0,), (0,)), ((), ())), preferred_element_type=F32)
        lane = lax.broadcasted_iota(jnp.int32, dkc.shape, 1)
        dkv = jnp.where(lane < NOPE, dkc, dv)
        dkr = jnp.where(lane >= NOPE, dkc, 0.0)

        @pl.when(i == 0)
        def _():
            dkv_ref[0] = dkv

        @pl.when(i > 0)
        def _():
            dkv_ref[0] += dkv

        first = jnp.logical_and(h == 0, i == 0)

        @pl.when(first)
        def _():
            dkr_ref[0] = dkr

        @pl.when(jnp.logical_not(first))
        def _():
            dkr_ref[0] += dkr

    qspec = pl.BlockSpec((1, tq, HEAD_BLOCK), lambda b, h, i: (b, i, h))
    kspec = pl.BlockSpec((1, T, HEAD_BLOCK), lambda b, h, i: (b, 0, h))
    rspec = pl.BlockSpec((1, T, HEAD_BLOCK), lambda b, h, i: (b, 0, 0))
    return pl.pallas_call(
        body, name="attn_bwd", grid=(nb, N_HEADS, S // tq),
        in_specs=[qspec, kspec, rspec, qspec], out_specs=[qspec, kspec, rspec],
        out_shape=[jax.ShapeDtypeStruct((nb, S, QP), F32), jax.ShapeDtypeStruct((nb, T, QP), F32),
                   jax.ShapeDtypeStruct((nb, T, HEAD_BLOCK), F32)],
        compiler_params=_cparams("arbitrary", "arbitrary", "arbitrary"),
    )(q, kv, kr, do)


def _seg_bounds(n, n_ctx):
    t = lax.broadcasted_iota(jnp.int32, (n, 1), 0)
    if n_ctx == 0:
        return t, jnp.zeros_like(t), jnp.full_like(t, n)
    in_ctx = t < n_ctx
    return t, jnp.where(in_ctx, 0, n_ctx), jnp.where(in_ctx, n_ctx, n)


def _shift_rows(x, o, bounds):
    if o == 0:
        return x
    t, lo, hi = bounds
    n = x.shape[0]
    valid = jnp.logical_and(t + o >= lo, t + o < hi).astype(F32)
    return pltpu.roll(x, (-o) % n, 0) * valid


def _conv(x, w, bias, k, bounds):
    acc = bias
    for o in range(k):
        acc = acc + w[o:o + 1, :] * _shift_rows(x, o - k // 2, bounds)
    return acc


def _conv_bwd(x, w, dpre, k, bounds):
    dx = jnp.zeros_like(x)
    rows = []
    for o in range(k):
        dx = dx + w[o:o + 1, :] * _shift_rows(dpre, -(o - k // 2), bounds)
        rows.append(jnp.sum(dpre * _shift_rows(x, o - k // 2, bounds), axis=0, keepdims=True))
    rows.append(jnp.sum(dpre, axis=0, keepdims=True))
    sub8 = lax.broadcasted_iota(jnp.int32, (8, x.shape[1]), 0)
    out = jnp.zeros((8, x.shape[1]), F32)
    for o, r in enumerate(rows):
        out = out + jnp.where(sub8 == o, r, 0.0)
    return dx, out


def _gelu(x):
    return 0.5 * x * (1.0 + lax.erf(x * (1.0 / math.sqrt(2.0))))


def _gelu_grad(x):
    return 0.5 * (1.0 + lax.erf(x * (1.0 / math.sqrt(2.0)))) + x * jnp.exp(-0.5 * x * x) * (1.0 / math.sqrt(2.0 * math.pi))


def ssd_conv_fwd(u, w8, bias, n_ctx, tc):
    nb, T, _ = u.shape
    cb0 = OFF_XBC // tc

    def body(x_ref, w_ref, b_ref, o_ref):
        pre = _conv(x_ref[0], w_ref[...], b_ref[...], SSD_K, _seg_bounds(T, n_ctx))
        o_ref[0] = _silu(pre)

    return pl.pallas_call(
        body, name="ssd_conv_fwd", grid=(nb, XBC // tc),
        in_specs=[pl.BlockSpec((1, T, tc), lambda b, j: (b, 0, cb0 + j)),
                  pl.BlockSpec((8, tc), lambda b, j: (0, j)), pl.BlockSpec((1, tc), lambda b, j: (0, j))],
        out_specs=pl.BlockSpec((1, T, tc), lambda b, j: (b, 0, j)),
        out_shape=jax.ShapeDtypeStruct((nb, T, XBC), F32),
        compiler_params=_cparams("arbitrary", "arbitrary"),
    )(u, w8, bias)


def ssd_conv_bwd(u, w8, bias, dxbc, dxs_direct, n_ctx, tc):
    nb, T, _ = u.shape
    cb0 = OFF_XBC // tc
    n_direct = D_INNER // tc

    def body(x_ref, w_ref, b_ref, d0_ref, d1_ref, dd_ref, dx_ref, dw_ref, acc):
        j, b = pl.program_id(0), pl.program_id(1)
        acc[...] = d0_ref[0, 0] + d1_ref[0, 0]

        @pl.when(j < n_direct)
        def _():
            acc[n_ctx:, :] += dd_ref[0]

        bounds = _seg_bounds(T, n_ctx)
        x, w = x_ref[0], w_ref[...]
        pre = _conv(x, w, b_ref[...], SSD_K, bounds)
        sg = jax.nn.sigmoid(pre)
        dpre = acc[...] * (sg * (1.0 + pre * (1.0 - sg)))
        dx, rows = _conv_bwd(x, w, dpre, SSD_K, bounds)
        dx_ref[0] = dx.astype(dx_ref.dtype)

        @pl.when(b == 0)
        def _():
            dw_ref[...] = rows

        @pl.when(b > 0)
        def _():
            dw_ref[...] += rows

    dspec0 = pl.BlockSpec((1, 1, T, tc), lambda j, b: (0, b, 0, j))
    dspec1 = pl.BlockSpec((1, 1, T, tc), lambda j, b: (1, b, 0, j))
    return pl.pallas_call(
        body, name="ssd_conv_bwd", grid=(XBC // tc, nb),
        in_specs=[pl.BlockSpec((1, T, tc), lambda j, b: (b, 0, cb0 + j)),
                  pl.BlockSpec((8, tc), lambda j, b: (0, j)), pl.BlockSpec((1, tc), lambda j, b: (0, j)),
                  dspec0, dspec1,
                  pl.BlockSpec((1, T - n_ctx, tc), lambda j, b: (b, 0, jnp.minimum(j, n_direct - 1)))],
        out_specs=[pl.BlockSpec((1, T, tc), lambda j, b: (b, 0, j)), pl.BlockSpec((8, tc), lambda j, b: (0, j))],
        out_shape=[jax.ShapeDtypeStruct((nb, T, XBC), BF16), jax.ShapeDtypeStruct((8, XBC), F32)],
        scratch_shapes=[pltpu.VMEM((T, tc), F32)],
        compiler_params=_cparams("arbitrary", "arbitrary"),
    )(u, w8, bias, dxbc, dxbc, dxs_direct)


def glu_fwd(up, w8, bias, tc):
    nb, S, _ = up.shape
    nj = D_FF // tc

    def body(g_ref, v_ref, w_ref, b_ref, o_ref):
        gc = _conv(g_ref[0], w_ref[...], b_ref[...], FFN_K, _seg_bounds(S, 0))
        o_ref[0] = (_gelu(gc) * v_ref[0]).astype(o_ref.dtype)

    return pl.pallas_call(
        body, name="glu_fwd", grid=(nb, nj),
        in_specs=[pl.BlockSpec((1, S, tc), lambda b, j: (b, 0, j)), pl.BlockSpec((1, S, tc), lambda b, j: (b, 0, nj + j)),
                  pl.BlockSpec((8, tc), lambda b, j: (0, j)), pl.BlockSpec((1, tc), lambda b, j: (0, j))],
        out_specs=pl.BlockSpec((1, S, tc), lambda b, j: (b, 0, j)),
        out_shape=jax.ShapeDtypeStruct((nb, S, D_FF), BF16),
        compiler_params=_cparams("arbitrary", "arbitrary"),
    )(up, up, w8, bias)


def glu_bwd(up, w8, bias, dact, tc):
    nb, S, _ = up.shape
    nj = D_FF // tc

    def body(g_ref, v_ref, w_ref, b_ref, d_ref, dg_ref, dv_ref, dw_ref):
        b = pl.program_id(1)
        bounds = _seg_bounds(S, 0)
        x, w, val, d = g_ref[0], w_ref[...], v_ref[0], d_ref[0].astype(F32)
        gc = _conv(x, w, b_ref[...], FFN_K, bounds)
        dv_ref[0] = (d * _gelu(gc)).astype(dv_ref.dtype)
        dx, rows = _conv_bwd(x, w, d * val * _gelu_grad(gc), FFN_K, bounds)
        dg_ref[0] = dx.astype(dg_ref.dtype)

        @pl.when(b == 0)
        def _():
            dw_ref[...] = rows

        @pl.when(b > 0)
        def _():
            dw_ref[...] += rows

    col = pl.BlockSpec((1, S, tc), lambda j, b: (b, 0, j))
    return pl.pallas_call(
        body, name="glu_bwd", grid=(nj, nb),
        in_specs=[col, pl.BlockSpec((1, S, tc), lambda j, b: (b, 0, nj + j)),
                  pl.BlockSpec((8, tc), lambda j, b: (0, j)), pl.BlockSpec((1, tc), lambda j, b: (0, j)), col],
        out_specs=[col, col, pl.BlockSpec((8, tc), lambda j, b: (0, j))],
        out_shape=[jax.ShapeDtypeStruct((nb, S, D_FF), BF16), jax.ShapeDtypeStruct((nb, S, D_FF), BF16),
                   jax.ShapeDtypeStruct((8, D_FF), F32)],
        compiler_params=_cparams("arbitrary", "arbitrary"),
    )(up, up, w8, bias, dact)


def _chunk_of(d, k, n_cc, n_ch):
    rev = jnp.where(k < n_cc, n_cc - 1 - k, n_cc + n_ch - 1 - k)
    return jnp.where(d == 1, rev, k)


def _lane_pick(v, lane_iota, l):
    return jnp.sum(jnp.where(lane_iota == l, v, 0.0), axis=1, keepdims=True)


def _row_pick(v, sub_iota, l):
    return jnp.sum(jnp.where(sub_iota == l, v, 0.0), axis=0, keepdims=True)


def _softplus(x):
    return jnp.maximum(x, 0.0) + jnp.log(1.0 + jnp.exp(-jnp.abs(x)))


def _ssd_common(d, dt_raw, alog, dtb):
    Q = dt_raw.shape[0]
    row = lax.broadcasted_iota(jnp.int32, (Q, Q), 0)
    col = lax.broadcasted_iota(jnp.int32, (Q, Q), 1)
    rev = d == 1
    maskb = jnp.where(rev, row, col) <= jnp.where(rev, col, row)
    tri = maskb.astype(F32)
    A = -jnp.exp(alog)
    dtv = _softplus(dt_raw + dtb)
    a = dtv * A
    cum = lax.dot_general(tri, a, (((1,), (0,)), ((), ())), precision=lax.Precision.HIGHEST, preferred_element_type=F32)
    tot = jnp.sum(a, axis=0, keepdims=True)
    return maskb, tri, A, dtv, cum, tot


def ssd_fwd(xbc, u, alog, dtb, n_ctx):
    nb, T, _ = xbc.shape
    S = T - n_ctx
    n_ch, n_cc = T // CHUNK, n_ctx // CHUNK
    dt_cb = OFF_DT // LANE
    Q = CHUNK
    n_pairs = SSD_HEADS // 2

    def body(x_ref, dt_ref, al_ref, db_ref, y_ref, hin_ref, H):
        d, k = pl.program_id(1), pl.program_id(2)

        @pl.when(k == 0)
        def _():
            H[...] = jnp.zeros_like(H)

        maskb, tri, A, dtv, cum, tot = _ssd_common(d, dt_ref[0], al_ref[...], db_ref[...])
        cumT = cum.T
        hin_ref[0, 0, 0] = H[...].astype(BF16)
        lane = lax.broadcasted_iota(jnp.int32, (Q, LANE), 1)
        lane1 = lax.broadcasted_iota(jnp.int32, (1, LANE), 1)
        sub = lax.broadcasted_iota(jnp.int32, (LANE, Q), 0)
        subc = lax.broadcasted_iota(jnp.int32, (LANE, 1), 0)
        half = lane < SSD_P
        for g in range(SSD_GROUPS):
            Bg = x_ref[0, :, D_INNER + g * SSD_N:D_INNER + (g + 1) * SSD_N].astype(BF16)
            Cg = x_ref[0, :, D_INNER + GN + g * SSD_N:D_INNER + GN + (g + 1) * SSD_N].astype(BF16)
            Gm = lax.dot_general(Cg, Bg, (((1,), (1,)), ((), ())), preferred_element_type=F32)
            for pr in range(n_pairs // SSD_GROUPS):
                p = g * (n_pairs // SSD_GROUPS) + pr
                l0 = d * SSD_HEADS + 2 * p
                l1 = l0 + 1
                s0c, s1c = _lane_pick(cum, lane, l0), _lane_pick(cum, lane, l1)
                s0r, s1r = _row_pick(cumT, sub, l0), _row_pick(cumT, sub, l1)
                dtp = jnp.where(half, _lane_pick(dtv, lane, l0), _lane_pick(dtv, lane, l1))
                tot0, tot1 = _lane_pick(tot, lane1, l0), _lane_pick(tot, lane1, l1)
                sc = jnp.where(half, s0c, s1c)
                totp = jnp.where(half, tot0, tot1)
                M0 = (Gm * jnp.exp(jnp.where(maskb, s0c - s0r, NEG_BIG))).astype(BF16)
                M1 = (Gm * jnp.exp(jnp.where(maskb, s1c - s1r, NEG_BIG))).astype(BF16)
                xd = x_ref[0, :, p * LANE:(p + 1) * LANE] * dtp
                xdb = xd.astype(BF16)
                yd = jnp.where(half,
                               lax.dot_general(M0, xdb, (((1,), (0,)), ((), ())), preferred_element_type=F32),
                               lax.dot_general(M1, xdb, (((1,), (0,)), ((), ())), preferred_element_type=F32))
                Hp = H[p * LANE:(p + 1) * LANE, :]
                yo = lax.dot_general(Cg, Hp.astype(BF16), (((1,), (1,)), ((), ())), preferred_element_type=F32) * jnp.exp(sc)

                @pl.when(k >= n_cc)
                def _():
                    y_ref[0, 0, :, p * LANE:(p + 1) * LANE] = yd + yo

                xdw = (xd * jnp.exp(totp - sc)).astype(BF16)
                etot = jnp.exp(jnp.where(subc < SSD_P, tot0, tot1))
                H[p * LANE:(p + 1) * LANE, :] = Hp * etot + lax.dot_general(
                    xdw, Bg, (((0,), (0,)), ((), ())), preferred_element_type=F32)

    def ymap(b, d, k):
        return (d, b, _chunk_of(d, jnp.maximum(k, n_cc), n_cc, n_ch) - n_cc, 0)

    return pl.pallas_call(
        body, name="ssd_fwd", grid=(nb, 2, n_ch),
        in_specs=[pl.BlockSpec((1, Q, XBC), lambda b, d, k: (b, _chunk_of(d, k, n_cc, n_ch), 0)),
                  pl.BlockSpec((1, Q, LANE), lambda b, d, k: (b, _chunk_of(d, k, n_cc, n_ch), dt_cb)),
                  pl.BlockSpec((1, LANE), lambda b, d, k: (0, 0)), pl.BlockSpec((1, LANE), lambda b, d, k: (0, 0))],
        out_specs=[pl.BlockSpec((1, 1, Q, D_INNER), ymap),
                   pl.BlockSpec((1, 1, 1, D_INNER, SSD_N), lambda b, d, k: (d, b, k, 0, 0))],
        out_shape=[jax.ShapeDtypeStruct((2, nb, S, D_INNER), F32),
                   jax.ShapeDtypeStruct((2, nb, n_ch, D_INNER, SSD_N), BF16)],
        scratch_shapes=[pltpu.VMEM((D_INNER, SSD_N), F32)],
        compiler_params=_cparams("arbitrary", "arbitrary", "arbitrary"),
    )(xbc, u, alog, dtb)


def ssd_bwd(xbc, u, alog, dtb, hin, dy, n_ctx):
    nb, T, _ = xbc.shape
    n_ch, n_cc = T // CHUNK, n_ctx // CHUNK
    dt_cb = OFF_DT // LANE
    Q = CHUNK
    n_pairs = SSD_HEADS // 2
    NT = (((1,), (1,)), ((), ()))
    NN = (((1,), (0,)), ((), ()))
    TN = (((0,), (0,)), ((), ()))

    def dot(a, b, dims):
        return lax.dot_general(a.astype(BF16), b.astype(BF16), dims, preferred_element_type=F32)

    def body(x_ref, dt_ref, al_ref, db_ref, hin_ref, dy_ref, dx_ref, ddt_ref, st_ref, dH):
        d, kk = pl.program_id(1), pl.program_id(2)
        ks = n_ch - 1 - kk

        @pl.when(kk == 0)
        def _():
            dH[...] = jnp.zeros_like(dH)

        @pl.when(jnp.logical_and(jnp.logical_and(pl.program_id(0) == 0, d == 0), kk == 0))
        def _():
            st_ref[...] = jnp.zeros_like(st_ref)

        dt_raw = dt_ref[0]
        alog, dtb_v = al_ref[...], db_ref[...]
        maskb, tri, A, dtv, cum, tot = _ssd_common(d, dt_raw, alog, dtb_v)
        cumT = cum.T
        live = (ks >= n_cc).astype(F32)
        lane = lax.broadcasted_iota(jnp.int32, (Q, LANE), 1)
        lane1 = lax.broadcasted_iota(jnp.int32, (1, LANE), 1)
        sub = lax.broadcasted_iota(jnp.int32, (LANE, Q), 0)
        subc = lax.broadcasted_iota(jnp.int32, (LANE, 1), 0)
        half = lane < SSD_P
        halfc = subc < SSD_P
        dcum = jnp.zeros((Q, LANE), F32)
        dcumT = jnp.zeros((LANE, Q), F32)
        ddt = jnp.zeros((Q, LANE), F32)
        dtot = jnp.zeros((1, LANE), F32)
        for g in range(SSD_GROUPS):
            Bg = x_ref[0, :, D_INNER + g * SSD_N:D_INNER + (g + 1) * SSD_N].astype(BF16)
            Cg = x_ref[0, :, D_INNER + GN + g * SSD_N:D_INNER + GN + (g + 1) * SSD_N].astype(BF16)
            Gm = lax.dot_general(Cg, Bg, NT, preferred_element_type=F32)
            dG = jnp.zeros((Q, Q), F32)
            dC = jnp.zeros((Q, SSD_N), F32)
            dB = jnp.zeros((Q, SSD_N), F32)
            for pr in range(n_pairs // SSD_GROUPS):
                p = g * (n_pairs // SSD_GROUPS) + pr
                l0 = d * SSD_HEADS + 2 * p
                l1 = l0 + 1
                s0c, s1c = _lane_pick(cum, lane, l0), _lane_pick(cum, lane, l1)
                s0r, s1r = _row_pick(cumT, sub, l0), _row_pick(cumT, sub, l1)
                dtp = jnp.where(half, _lane_pick(dtv, lane, l0), _lane_pick(dtv, lane, l1))
                tot0, tot1 = _lane_pick(tot, lane1, l0), _lane_pick(tot, lane1, l1)
                sc = jnp.where(half, s0c, s1c)
                totp = jnp.where(half, tot0, tot1)
                L0 = jnp.exp(jnp.where(maskb, s0c - s0r, NEG_BIG))
                L1 = jnp.exp(jnp.where(maskb, s1c - s1r, NEG_BIG))
                M0, M1 = Gm * L0, Gm * L1
                xs = x_ref[0, :, p * LANE:(p + 1) * LANE]
                xd = xs * dtp
                es = jnp.exp(sc)
                dte = jnp.exp(totp - sc)
                etot = jnp.exp(jnp.where(halfc, tot0, tot1))
                dyp = dy_ref[0, :, p * LANE:(p + 1) * LANE] * live
                Hp = hin_ref[0, 0, 0, p * LANE:(p + 1) * LANE, :]
                dHp = dH[p * LANE:(p + 1) * LANE, :]
                bdh = dot(Bg, dHp, NT)
                dxd = jnp.where(half, dot(M0, dyp, TN), dot(M1, dyp, TN)) + bdh * dte
                dy0 = jnp.where(half, dyp, 0.0)
                dy1 = dyp - dy0
                dM0, dM1 = dot(dy0, xd, NT), dot(dy1, xd, NT)
                dG = dG + dM0 * L0 + dM1 * L1
                dyes = dyp * es
                xdw = xd * dte
                dC = dC + dot(dyes, Hp, NN)
                dB = dB + dot(xdw, dHp, NN)
                W0, W1 = dM0 * M0, dM1 * M1
                yoff = dot(Cg, Hp, NT) * es
                r_off = dyp * yoff
                r_st = xd * bdh * dte
                hh = jnp.sum(dHp * Hp.astype(F32), axis=1, keepdims=True) * etot
                for (l, W, hsel, hselc) in ((l0, W0, half, halfc),
                                            (l1, W1, jnp.logical_not(half), jnp.logical_not(halfc))):
                    col_g = (jnp.sum(W, axis=1, keepdims=True)
                             + jnp.sum(jnp.where(hsel, r_off - r_st, 0.0), axis=1, keepdims=True))
                    row_g = -jnp.sum(W, axis=0, keepdims=True)
                    tot_g = (jnp.sum(jnp.sum(jnp.where(hsel, r_st, 0.0), axis=1, keepdims=True), axis=0, keepdims=True)
                             + jnp.sum(jnp.where(hselc, hh, 0.0), axis=0, keepdims=True))
                    dcum = dcum + jnp.where(lane == l, col_g, 0.0)
                    dcumT = dcumT + jnp.where(sub == l, row_g, 0.0)
                    dtot = dtot + jnp.where(lane1 == l, tot_g, 0.0)
                    ddt = ddt + jnp.where(lane == l, jnp.sum(jnp.where(hsel, dxd * xs, 0.0), axis=1, keepdims=True), 0.0)
                dx_ref[0, 0, :, p * LANE:(p + 1) * LANE] = dxd * dtp
                dH[p * LANE:(p + 1) * LANE, :] = dHp * etot + dot(dyes, Cg, TN)
            dx_ref[0, 0, :, D_INNER + g * SSD_N:D_INNER + (g + 1) * SSD_N] = dB + dot(dG, Cg, TN)
            dx_ref[0, 0, :, D_INNER + GN + g * SSD_N:D_INNER + GN + (g + 1) * SSD_N] = dC + dot(dG, Bg, NN)
        dcum_all = dcum + dcumT.T
        da = lax.dot_general(tri, dcum_all, TN, precision=lax.Precision.HIGHEST, preferred_element_type=F32) + dtot
        ddtv = ddt + da * A
        ddt_raw = ddtv * jax.nn.sigmoid(dt_raw + dtb_v)
        ddt_ref[0, 0] = ddt_raw
        st_ref[0:1, :] += jnp.sum(da * dtv * A, axis=0, keepdims=True)
        st_ref[1:2, :] += jnp.sum(ddt_raw, axis=0, keepdims=True)

    def cmap(d, kk):
        return _chunk_of(d, n_ch - 1 - kk, n_cc, n_ch)

    def dymap(b, d, kk):
        return (b, _chunk_of(d, jnp.maximum(n_ch - 1 - kk, n_cc), n_cc, n_ch) - n_cc, 0)

    return pl.pallas_call(
        body, name="ssd_bwd", grid=(nb, 2, n_ch),
        in_specs=[pl.BlockSpec((1, Q, XBC), lambda b, d, kk: (b, cmap(d, kk), 0)),
                  pl.BlockSpec((1, Q, LANE), lambda b, d, kk: (b, cmap(d, kk), dt_cb)),
                  pl.BlockSpec((1, LANE), lambda b, d, kk: (0, 0)), pl.BlockSpec((1, LANE), lambda b, d, kk: (0, 0)),
                  pl.BlockSpec((1, 1, 1, D_INNER, SSD_N), lambda b, d, kk: (d, b, n_ch - 1 - kk, 0, 0)),
                  pl.BlockSpec((1, Q, D_INNER), dymap)],
        out_specs=[pl.BlockSpec((1, 1, Q, XBC), lambda b, d, kk: (d, b, cmap(d, kk), 0)),
                   pl.BlockSpec((1, 1, Q, LANE), lambda b, d, kk: (d, b, cmap(d, kk), 0)),
                   pl.BlockSpec((8, LANE), lambda b, d, kk: (0, 0))],
        out_shape=[jax.ShapeDtypeStruct((2, nb, T, XBC), F32), jax.ShapeDtypeStruct((2, nb, T, LANE), F32),
                   jax.ShapeDtypeStruct((8, LANE), F32)],
        scratch_shapes=[pltpu.VMEM((D_INNER, SSD_N), F32)],
        compiler_params=_cparams("arbitrary", "arbitrary", "arbitrary"),
    )(xbc, u, alog, dtb, hin, dy)


PACK_ROWS = 512


def adamw_call(w, g, m, v):
    R = w.shape[0]
    tr = PACK_ROWS

    def body(w_ref, g_ref, m_ref, v_ref, d_ref, mo_ref, vo_ref):
        gv = g_ref[...]
        mn = ADAM_B1 * m_ref[...] + (1.0 - ADAM_B1) * gv
        vn = ADAM_B2 * v_ref[...] + (1.0 - ADAM_B2) * jnp.square(gv)
        m_hat = mn / (1.0 - ADAM_B1 ** ADAM_STEP)
        v_hat = vn / (1.0 - ADAM_B2 ** ADAM_STEP)
        d_ref[...] = -ADAM_LR * (m_hat / (jnp.sqrt(v_hat) + ADAM_EPS) + ADAM_WD * w_ref[...])
        mo_ref[...] = mn
        vo_ref[...] = vn

    spec = pl.BlockSpec((tr, LANE), lambda i: (i, 0))
    return pl.pallas_call(
        body, name="adamw", grid=(R // tr,), in_specs=[spec] * 4, out_specs=[spec] * 3,
        out_shape=[jax.ShapeDtypeStruct((R, LANE), F32)] * 3,
        compiler_params=_cparams("arbitrary"),
    )(w, g, m, v)


def sum_slots(name, x):
    n, R, _ = x.shape
    tr = _tile(R, PACK_ROWS, 8)

    def body(x_ref, o_ref):
        acc = x_ref[0]
        for j in range(1, n):
            acc = acc + x_ref[j]
        o_ref[...] = acc

    return pl.pallas_call(
        body, name=name, grid=(R // tr,),
        in_specs=[pl.BlockSpec((n, tr, LANE), lambda i: (0, i, 0))],
        out_specs=pl.BlockSpec((tr, LANE), lambda i: (i, 0)),
        out_shape=jax.ShapeDtypeStruct((R, LANE), F32),
        compiler_params=_cparams("arbitrary"),
    )(x)


def _pack(parts, row_mult):
    flat = jnp.concatenate([p.reshape(-1) for p in parts])
    n = flat.shape[0]
    total = -(-n // (LANE * row_mult)) * LANE * row_mult
    return jnp.pad(flat, (0, total - n)).reshape(-1, LANE)


def _unpack(flat, shapes):
    out, off = [], 0
    for s in shapes:
        n = int(np.prod(s))
        out.append(flat[..., off:off + n].reshape(flat.shape[:-1] + tuple(s)))
        off += n
    return out


def _mesh_pos():
    return lax.axis_index("x"), lax.axis_index("y"), lax.axis_index("c")


def all_gather(name, v):
    R, C = v.shape

    def body(x_ref, out_ref, send_sems, recv_sems, local_sem):
        x, y, c = _mesh_pos()
        me, sibling = (x, y, c), (x, y, 1 - c)
        chips = [(1 - x, y), (x, 1 - y), (1 - x, 1 - y)]

        def slot(px, py, pc):
            return out_ref.at[4 * px + 2 * py + pc]

        def copy(k, block, to, src=None):
            return pltpu.make_async_remote_copy(
                src_ref=slot(*block) if src is None else src, dst_ref=slot(*block),
                send_sem=send_sems.at[k], recv_sem=recv_sems.at[k], device_id=to, device_id_type=MESH)

        mine = pltpu.make_async_copy(x_ref, slot(*me), local_sem)
        mine.start()
        first = [copy(0, me, sibling, src=x_ref)]
        first += [copy(1 + j, me, (*chip, c), src=x_ref) for j, chip in enumerate(chips)]
        for cp in first:
            cp.start()
        passed = [copy(4 + j, (*chip, c), sibling) for j, chip in enumerate(chips)]
        for j, chip in enumerate(chips):
            copy(1 + j, (*chip, c), me).wait_recv()
            passed[j].start()
        copy(0, sibling, me).wait_recv()
        for j, chip in enumerate(chips):
            copy(4 + j, (*chip, 1 - c), me).wait_recv()
        for cp in first + passed:
            cp.wait_send()
        mine.wait()

    return pl.pallas_call(
        body, name=name, out_shape=jax.ShapeDtypeStruct((N_DEV, R, C), v.dtype),
        in_specs=[pl.BlockSpec(memory_space=pl.ANY)], out_specs=pl.BlockSpec(memory_space=pl.ANY),
        scratch_shapes=[pltpu.SemaphoreType.DMA((7,)), pltpu.SemaphoreType.DMA((7,)), pltpu.SemaphoreType.DMA],
    )(v)


def all_to_all(name, v):
    _, R, C = v.shape

    def body(x_ref, out_ref, send_sems, recv_sems, local_sem):
        x, y, c = _mesh_pos()
        me = 4 * x + 2 * y + c
        mine = pltpu.make_async_copy(x_ref.at[me], out_ref.at[me], local_sem)
        mine.start()
        copies = []
        for k in range(1, N_DEV):
            px, py, pc = x ^ ((k >> 2) & 1), y ^ ((k >> 1) & 1), c ^ (k & 1)
            copies.append(pltpu.make_async_remote_copy(
                src_ref=x_ref.at[4 * px + 2 * py + pc], dst_ref=out_ref.at[me],
                send_sem=send_sems.at[k - 1], recv_sem=recv_sems.at[k - 1],
                device_id=(px, py, pc), device_id_type=MESH))
        for cp in copies:
            cp.start()
        for cp in copies:
            cp.wait_recv()
        for cp in copies:
            cp.wait_send()
        mine.wait()

    return pl.pallas_call(
        body, name=name, out_shape=jax.ShapeDtypeStruct((N_DEV, R, C), v.dtype),
        in_specs=[pl.BlockSpec(memory_space=pl.ANY)], out_specs=pl.BlockSpec(memory_space=pl.ANY),
        scratch_shapes=[pltpu.SemaphoreType.DMA((7,)), pltpu.SemaphoreType.DMA((7,)), pltpu.SemaphoreType.DMA],
    )(v)


def _taps8(w):
    return jnp.concatenate([w, jnp.zeros((8 - w.shape[0], w.shape[1]), w.dtype)], axis=0)


def _lanes128(v):
    v = v.reshape(1, -1)
    return jnp.pad(v, ((0, 0), (0, LANE - v.shape[1])))


def weights_to_internal(w_in, w_q_up, w_kv_up, w_out, w_up, w_down):
    cq, ckv, kr, z, xbc, dt = jnp.split(w_in, np.cumsum(IN_SPLITS)[:-1].tolist(), axis=1)
    K = w_in.shape[0]

    def zeros(n):
        return jnp.zeros((K, n), w_in.dtype)

    w_in_p = jnp.concatenate([cq, zeros(KR_LANE), kr, zeros(LANE - KR_LANE - ROPE), ckv, zeros(OFF_Z - OFF_CKV - KV_RANK),
                              z, xbc, dt, zeros(LANE - 2 * SSD_HEADS)], axis=1)
    w_q_p = jnp.pad(w_q_up.reshape(Q_RANK, N_HEADS, NOPE + ROPE), ((0, 0), (0, 0), (0, HEAD_BLOCK - NOPE - ROPE))).reshape(Q_RANK, QP)
    attn_rows = w_out[:N_HEADS * V_DIM].reshape(N_HEADS, V_DIM, -1)
    w_out_p = jnp.concatenate([jnp.pad(attn_rows, ((0, 0), (HEAD_BLOCK - V_DIM, 0), (0, 0))).reshape(QP, -1),
                               w_out[N_HEADS * V_DIM:]], axis=0)
    return dict(w_in_p=w_in_p, w_q_p=w_q_p, w_kv=w_kv_up, w_out_p=w_out_p, w_up=w_up, w_down=w_down)


def grads_from_internal(g_in_p, g_q_p, g_kv, g_out_p, g_up, g_down):
    g_in = jnp.concatenate([g_in_p[:, OFF_CQ:OFF_CQ + Q_RANK], g_in_p[:, OFF_CKV:OFF_CKV + KV_RANK],
                            g_in_p[:, OFF_KR + KR_LANE:OFF_KR + KR_LANE + ROPE], g_in_p[:, OFF_Z:OFF_Z + D_INNER],
                            g_in_p[:, OFF_XBC:OFF_XBC + XBC], g_in_p[:, OFF_DT:OFF_DT + 2 * SSD_HEADS]], axis=1)
    g_q = g_q_p.reshape(Q_RANK, N_HEADS, HEAD_BLOCK)[:, :, :NOPE + ROPE].reshape(Q_RANK, -1)
    g_out = jnp.concatenate([g_out_p[:QP].reshape(N_HEADS, HEAD_BLOCK, -1)[:, HEAD_BLOCK - V_DIM:].reshape(N_HEADS * V_DIM, -1),
                             g_out_p[QP:]], axis=0)
    return g_in, g_q, g_kv, g_out, g_up, g_down


def local_step(x, ctx, target, mod_x, mod_c, W, V):
    nb, S, D = x.shape
    C = ctx.shape[1]
    T = C + S
    tr = _tile(math.gcd(C, S), 256, 8)
    tq = _tile(S, 256, 8)
    tc = 256
    cblk = C // tr
    m = [mod_x[:, i * D:(i + 1) * D][:, None, :] for i in range(N_MOD)]
    mc = [mod_c[:, i * D:(i + 1) * D] for i in range(2)]
    ssd_w8, ffn_w8 = _taps8(V["ssd_conv_w"]), _taps8(V["ffn_conv_w"])
    alog, dtb = _lanes128(V["ssd_a_log"]), _lanes128(V["ssd_dt_bias"])
    dexp = jnp.repeat(V["ssd_d"].reshape(-1), SSD_P).reshape(1, D_INNER)
    cosT, sinT = rope_tables(C, S)
    cosS, sinS = cosT[C:], sinT[C:]

    (h1x,) = rows_fwd("prenorm_x", fn_prenorm, nb, S // tr, tr, [(x, D, 0, 0)], [m[0], m[1]], [V["mix_pre_norm"]], [(D, BF16)])
    (h1c,) = rows_fwd("prenorm_c", fn_prenorm, nb, C // tr, tr, [(ctx, D, 0, 0)], [], [mc[0], mc[1], V["mix_pre_norm"]], [(D, BF16)])
    h1 = jnp.concatenate([h1c, h1x], axis=1).reshape(nb * T, D)
    u = matmul("in_proj", [(h1, W["w_in_p"])], "nn", F32).reshape(nb, T, WIN_P)
    (qn,) = rows_fwd("q_norm", fn_rms, nb, S // tr, tr, [(u, Q_RANK, OFF_CQ // Q_RANK, cblk)], [], [V["q_norm"]], [(Q_RANK, BF16)])
    (kvn,) = rows_fwd("kv_norm", fn_rms, nb, T // tr, tr, [(u, KV_RANK, OFF_CKV // KV_RANK, 0)], [], [V["kv_norm"]], [(KV_RANK, BF16)])
    qn2, kvn2 = qn.reshape(nb * S, Q_RANK), kvn.reshape(nb * T, KV_RANK)
    q_raw = matmul("q_up", [(qn2, W["w_q_p"])], "nn", F32).reshape(nb, S, QP)
    kv = matmul("kv_up", [(kvn2, W["w_kv"])], "nn", BF16).reshape(nb, T, QP)
    q = rope_call("rope_q", q_raw, QP, 0, cosS, sinS, BF16, tr)
    kr = rope_call("rope_k", u, LANE, OFF_KR // LANE, cosT, sinT, BF16, tr)
    o = attn_fwd(q, kv, kr, tq)
    xbc = ssd_conv_fwd(u, ssd_w8, V["ssd_conv_b"], C, tc)
    y2, hin = ssd_fwd(xbc, u, alog, dtb, C)
    fin_rows = [(y2[0], D_INNER, 0, 0), (y2[1], D_INNER, 0, 0), (xbc, D_INNER, 0, cblk), (u, D_INNER, OFF_Z // D_INNER, cblk)]
    fin_gl = [dexp, V["ssd_norm"]]
    (ssd,) = rows_fwd("ssd_finish", fn_ssd_finish, nb, S // tr, tr, fin_rows, [], fin_gl, [(D_INNER, BF16)])
    o2, ssd2 = o.reshape(nb * S, QP), ssd.reshape(nb * S, D_INNER)
    mix = matmul("out_proj", [(o2, W["w_out_p"][:QP]), (ssd2, W["w_out_p"][QP:])], "nn", F32).reshape(nb, S, D)
    pm_rows = [(x, D, 0, 0), (mix, D, 0, 0)]
    pm_pb = [m[2], m[4], m[3]]
    pm_gl = [V["mix_post_norm"], V["ffn_pre_norm"]]
    x1, h2 = rows_fwd("postmix", fn_postmix, nb, S // tr, tr, pm_rows, pm_pb, pm_gl, [(D, F32), (D, BF16)])
    h22 = h2.reshape(nb * S, D)
    up = matmul("up_proj", [(h22, W["w_up"])], "nn", F32).reshape(nb, S, 2 * D_FF)
    act = glu_fwd(up, ffn_w8, V["ffn_conv_b"], tc)
    act2 = act.reshape(nb * S, D_FF)
    ffn = matmul("down_proj", [(act2, W["w_down"])], "nn", F32).reshape(nb, S, D)
    dx1, dffn, dgate2, d_ffn_post, loss = final_call(x1, ffn, target, m[5], V["ffn_post_norm"], tr)

    dffn2 = dffn.reshape(nb * S, D)
    dact = matmul("down_dgrad", [(dffn2, W["w_down"])], "nt", BF16).reshape(nb, S, D_FF)
    g_down = matmul_tn("down_wgrad", act2, dffn2)
    dgate, dval, ffn_rows = glu_bwd(up, ffn_w8, V["ffn_conv_b"], dact, tc)
    dgate_2, dval_2 = dgate.reshape(nb * S, D_FF), dval.reshape(nb * S, D_FF)
    dh2 = matmul("up_dgrad", [(dgate_2, W["w_up"][:, :D_FF]), (dval_2, W["w_up"][:, D_FF:])], "nt", BF16).reshape(nb, S, D)
    g_up = jnp.concatenate([matmul_tn("up_wgrad_gate", h22, dgate_2), matmul_tn("up_wgrad_val", h22, dval_2)], axis=1)
    dx_a, dmix, dgate1, dscale2, dshift2, d_mix_post, d_ffn_pre = rows_bwd(
        "postmix_bwd", fn_postmix, nb, S // tr, tr, pm_rows, pm_pb, pm_gl,
        [(dx1, D, 0, 0), (dh2, D, 0, 0)], [(0, F32), (1, BF16)])
    dmix2 = dmix.reshape(nb * S, D)
    dcat = matmul("out_dgrad", [(dmix2, W["w_out_p"])], "nt", BF16).reshape(nb, S, QP + D_INNER)
    g_out_p = jnp.concatenate([matmul_tn("out_wgrad_attn", o2, dmix2), matmul_tn("out_wgrad_ssd", ssd2, dmix2)], axis=0)
    dy, dxs_direct, dz, d_dexp, d_ssd_norm = rows_bwd(
        "ssd_finish_bwd", fn_ssd_finish, nb, S // tr, tr, fin_rows, [], fin_gl,
        [(dcat, D_INNER, QP // D_INNER, 0)], [(0, F32), (2, F32), (3, BF16)])
    dxbc2, ddt2, ssd_stats = ssd_bwd(xbc, u, alog, dtb, hin, dy, C)
    dxbc_raw, ssd_rows = ssd_conv_bwd(u, ssd_w8, V["ssd_conv_b"], dxbc2, dxs_direct, C, tc)
    dq, dkv, dkr = attn_bwd(q, kv, kr, dcat, tq)
    dq_pre = rope_call("rope_dq", dq, QP, 0, cosS, -sinS, BF16, tr).reshape(nb * S, QP)
    dkr_pre = rope_call("rope_dk", dkr, LANE, 0, cosT, -sinT, BF16, tr)
    dkv2 = dkv.reshape(nb * T, QP)
    dqn = matmul("q_dgrad", [(dq_pre, W["w_q_p"])], "nt", F32).reshape(nb, S, Q_RANK)
    g_q_p = matmul_tn("q_wgrad", qn2, dq_pre)
    dkvn = matmul("kv_dgrad", [(dkv2, W["w_kv"])], "nt", F32).reshape(nb, T, KV_RANK)
    g_kv = matmul_tn("kv_wgrad", kvn2, dkv2)
    dcq, d_q_norm = rows_bwd("q_norm_bwd", fn_rms, nb, S // tr, tr, [(u, Q_RANK, OFF_CQ // Q_RANK, cblk)], [], [V["q_norm"]],
                             [(dqn, Q_RANK, 0, 0)], [(0, BF16)])
    dckv, d_kv_norm = rows_bwd("kv_norm_bwd", fn_rms, nb, T // tr, tr, [(u, KV_RANK, OFF_CKV // KV_RANK, 0)], [], [V["kv_norm"]],
                               [(dkvn, KV_RANK, 0, 0)], [(0, BF16)])

    def ctx_rows(t):
        return jnp.pad(t, ((0, 0), (C, 0), (0, 0)))

    du = jnp.concatenate([ctx_rows(dcq), dkr_pre, dckv, jnp.zeros((nb, T, OFF_Z - OFF_CKV - KV_RANK), BF16), ctx_rows(dz),
                          dxbc_raw, (ddt2[0] + ddt2[1]).astype(BF16)], axis=-1).reshape(nb * T, WIN_P)
    dh1 = matmul("in_dgrad", [(du, W["w_in_p"])], "nt", BF16).reshape(nb, T, D)
    g_in_p = matmul_tn("in_wgrad", h1, du)

    def fn_prenorm_res(xv, shift, scale, g):
        return fn_prenorm(xv, shift, scale, g) + (xv,)

    grad_x, dshift1, dscale1, d_mix_pre_x = rows_bwd(
        "prenorm_x_bwd", fn_prenorm_res, nb, S // tr, tr, [(x, D, 0, 0)], [m[0], m[1]], [V["mix_pre_norm"]],
        [(dh1, D, 0, cblk), (dx_a, D, 0, 0)], [(0, F32)])
    dshift_c, dscale_c, d_mix_pre_c = rows_bwd(
        "prenorm_c_bwd", fn_prenorm, nb, C // tr, tr, [(ctx, D, 0, 0)], [], [mc[0], mc[1], V["mix_pre_norm"]],
        [(dh1, D, 0, 0)], [])

    dmod_x = jnp.concatenate([dshift1, dscale1, dgate1, dshift2, dscale2, dgate2], axis=-1).reshape(nb, N_MOD * D)
    dmod_c = jnp.concatenate([dshift_c, dscale_c, jnp.zeros((1, (N_MOD - 2) * D), F32)], axis=-1)
    gm = dict(w_in_p=g_in_p, w_q_p=g_q_p, w_kv=g_kv, w_out_p=g_out_p, w_up=g_up, w_down=g_down)
    gv = dict(
        mix_pre_norm=d_mix_pre_x + d_mix_pre_c, mix_post_norm=d_mix_post, q_norm=d_q_norm, kv_norm=d_kv_norm,
        ssd_conv_w=ssd_rows[:SSD_K], ssd_conv_b=ssd_rows[SSD_K:SSD_K + 1],
        ssd_a_log=ssd_stats[0:1, :2 * SSD_HEADS], ssd_dt_bias=ssd_stats[1:2, :2 * SSD_HEADS],
        ssd_d=jnp.sum(d_dexp.reshape(SSD_HEADS, SSD_P), axis=1).reshape(1, SSD_HEADS), ssd_norm=d_ssd_norm,
        ffn_pre_norm=d_ffn_pre, ffn_post_norm=d_ffn_post,
        ffn_conv_w=ffn_rows[:FFN_K], ffn_conv_b=ffn_rows[FFN_K:FFN_K + 1])
    return loss, grad_x, dmod_x, dmod_c, gm, gv


WEIGHT_ORDER = ("c_ctx", "w_mod", "b_mod", "mix_pre_norm", "mix_post_norm", "w_in", "q_norm", "w_q_up", "kv_norm",
                "w_kv_up", "ssd_conv_w", "ssd_conv_b", "ssd_a_log", "ssd_dt_bias", "ssd_d", "ssd_norm", "w_out",
                "ffn_pre_norm", "ffn_post_norm", "w_up", "ffn_conv_w", "ffn_conv_b", "w_down")
MATRICES = ("w_in", "w_q_up", "w_kv_up", "w_out", "w_up", "w_down")
ROW_SHARDED = ("w_out", "w_down")
SMALL_SUMMED = ("c_ctx", "mix_pre_norm", "mix_post_norm", "q_norm", "kv_norm", "ssd_conv_w", "ssd_conv_b", "ssd_a_log",
                "ssd_dt_bias", "ssd_d", "ssd_norm", "ffn_pre_norm", "ffn_post_norm", "ffn_conv_w", "ffn_conv_b")
MOD_ROWS = 8


def _whole(shards, name):
    if name in ROW_SHARDED:
        return shards.reshape(-1, shards.shape[-1])
    return jnp.transpose(shards, (1, 0, 2)).reshape(shards.shape[1], -1)


def _per_device(g, name):
    if name in ROW_SHARDED:
        return g.reshape(N_DEV, -1)
    k = g.shape[0]
    return jnp.transpose(g.reshape(k, N_DEV, -1), (1, 0, 2)).reshape(N_DEV, -1)


def kernel(x, c, ctx, c_ctx, w_mod, b_mod, mix_pre_norm, mix_post_norm, w_in, q_norm, w_q_up, kv_norm, w_kv_up, ssd_conv_w, ssd_conv_b, ssd_a_log, ssd_dt_bias, ssd_d, ssd_norm, w_out, ffn_pre_norm, ffn_post_norm, w_up, ffn_conv_w, ffn_conv_b, w_down, loss_target, m_c_ctx, m_w_mod, m_b_mod, m_mix_pre_norm, m_mix_post_norm, m_w_in, m_q_norm, m_w_q_up, m_kv_norm, m_w_kv_up, m_ssd_conv_w, m_ssd_conv_b, m_ssd_a_log, m_ssd_dt_bias, m_ssd_d, m_ssd_norm, m_w_out, m_ffn_pre_norm, m_ffn_post_norm, m_w_up, m_ffn_conv_w, m_ffn_conv_b, m_w_down, v_c_ctx, v_w_mod, v_b_mod, v_mix_pre_norm, v_mix_post_norm, v_w_in, v_q_norm, v_w_q_up, v_kv_norm, v_w_kv_up, v_ssd_conv_w, v_ssd_conv_b, v_ssd_a_log, v_ssd_dt_bias, v_ssd_d, v_ssd_norm, v_w_out, v_ffn_pre_norm, v_ffn_post_norm, v_w_up, v_ffn_conv_w, v_ffn_conv_b, v_w_down):
    weights = dict(c_ctx=c_ctx, w_mod=w_mod, b_mod=b_mod, mix_pre_norm=mix_pre_norm, mix_post_norm=mix_post_norm, w_in=w_in, q_norm=q_norm, w_q_up=w_q_up, kv_norm=kv_norm, w_kv_up=w_kv_up, ssd_conv_w=ssd_conv_w, ssd_conv_b=ssd_conv_b, ssd_a_log=ssd_a_log, ssd_dt_bias=ssd_dt_bias, ssd_d=ssd_d, ssd_norm=ssd_norm, w_out=w_out, ffn_pre_norm=ffn_pre_norm, ffn_post_norm=ffn_post_norm, w_up=w_up, ffn_conv_w=ffn_conv_w, ffn_conv_b=ffn_conv_b, w_down=w_down)
    mom1 = dict(c_ctx=m_c_ctx, w_mod=m_w_mod, b_mod=m_b_mod, mix_pre_norm=m_mix_pre_norm, mix_post_norm=m_mix_post_norm, w_in=m_w_in, q_norm=m_q_norm, w_q_up=m_w_q_up, kv_norm=m_kv_norm, w_kv_up=m_w_kv_up, ssd_conv_w=m_ssd_conv_w, ssd_conv_b=m_ssd_conv_b, ssd_a_log=m_ssd_a_log, ssd_dt_bias=m_ssd_dt_bias, ssd_d=m_ssd_d, ssd_norm=m_ssd_norm, w_out=m_w_out, ffn_pre_norm=m_ffn_pre_norm, ffn_post_norm=m_ffn_post_norm, w_up=m_w_up, ffn_conv_w=m_ffn_conv_w, ffn_conv_b=m_ffn_conv_b, w_down=m_w_down)
    mom2 = dict(c_ctx=v_c_ctx, w_mod=v_w_mod, b_mod=v_b_mod, mix_pre_norm=v_mix_pre_norm, mix_post_norm=v_mix_post_norm, w_in=v_w_in, q_norm=v_q_norm, w_q_up=v_w_q_up, kv_norm=v_kv_norm, w_kv_up=v_w_kv_up, ssd_conv_w=v_ssd_conv_w, ssd_conv_b=v_ssd_conv_b, ssd_a_log=v_ssd_a_log, ssd_dt_bias=v_ssd_dt_bias, ssd_d=v_ssd_d, ssd_norm=v_ssd_norm, w_out=v_w_out, ffn_pre_norm=v_ffn_pre_norm, ffn_post_norm=v_ffn_post_norm, w_up=v_w_up, ffn_conv_w=v_ffn_conv_w, ffn_conv_b=v_ffn_conv_b, w_down=v_w_down)
    nb, S, D = x.shape
    me = 4 * lax.axis_index("x") + 2 * lax.axis_index("y") + lax.axis_index("c")

    mat_shapes = [weights[n].shape[1:] for n in MATRICES]
    wall = all_gather("gather_weights", _pack([weights[n].astype(BF16) for n in MATRICES], 16)).reshape(N_DEV, -1)
    whole = {n: _whole(s, n) for n, s in zip(MATRICES, _unpack(wall, mat_shapes))}
    W = weights_to_internal(*[whole[n] for n in MATRICES])
    small_shapes = [c.shape, ssd_conv_w.shape[1:], ffn_conv_w.shape[1:]]
    sall = all_gather("gather_small", _pack([c, ssd_conv_w, ffn_conv_w], 8)).reshape(N_DEV, -1)
    c_all, ssd_w_sh, ffn_w_sh = _unpack(sall, small_shapes)
    V = {n: weights[n].reshape(1, -1) for n in SMALL_SUMMED if n != "c_ctx"}
    V["ssd_conv_w"] = _whole(ssd_w_sh, "ssd_conv_w")
    V["ffn_conv_w"] = _whole(ffn_w_sh, "ffn_conv_w")

    n_all = N_DEV * nb
    mod_rows = -(-(n_all + 1) // 8) * 8
    c_pad = jnp.concatenate([c_all.reshape(n_all, D), c_ctx.reshape(1, D), jnp.zeros((mod_rows - n_all - 1, D), F32)], axis=0)
    mod_cols = w_mod.shape[2]
    b_mine = lax.dynamic_slice(b_mod, (0, me * mod_cols), (1, mod_cols))
    mod_part = matmul("mod_proj", [(c_pad, w_mod[0])], "nn", F32, bias=b_mine, silu_a=True)
    mod_all = jnp.transpose(all_gather("gather_mod", mod_part), (1, 0, 2)).reshape(mod_rows, -1)
    mod_x = lax.dynamic_slice(mod_all, (me * nb, 0), (nb, mod_all.shape[1]))
    mod_c = mod_all[n_all:n_all + 1]

    loss, grad_x, dmod_x, dmod_c, gm, gv = local_step(x, ctx, loss_target, mod_x, mod_c, W, V)

    dmod_mine = jnp.concatenate([dmod_x, dmod_c, jnp.zeros((MOD_ROWS - nb - 1, dmod_x.shape[1]), F32)], axis=0)
    dmod_all = all_gather("gather_dmod", dmod_mine)
    dmod_ctx = sum_slots("sum_dmod_ctx", dmod_all[:, nb:nb + 1].reshape(N_DEV, -1, LANE)).reshape(1, -1)
    dmod_full = jnp.concatenate([dmod_all[:, :nb].reshape(n_all, -1), dmod_ctx,
                                 jnp.zeros((mod_rows - n_all - 1, dmod_ctx.shape[1]), F32)], axis=0)
    (g_b_mod,) = ew_call("mod_bias_grad", lambda t: (jnp.sum(t, axis=0, keepdims=True),), [dmod_full], [((1, dmod_full.shape[1]), F32)])
    dmod_cols = lax.dynamic_slice(dmod_full, (0, me * mod_cols), (mod_rows, mod_cols))
    g_w_mod = matmul_tn("mod_wgrad", c_pad, dmod_cols, silu_a=True)
    dsilu_ctx = matmul("mod_dgrad_ctx", [(dmod_cols[n_all:n_all + 8], w_mod[0])], "nt", F32)[0:1]

    def silu_vjp(cc, ct):
        return (jax.vjp(_silu, cc)[1](ct)[0],)

    (g_c_ctx_part,) = ew_call("c_ctx_grad", silu_vjp, [c_ctx.reshape(1, D), dsilu_ctx], [((1, D), F32)])

    gv = dict(gv, c_ctx=g_c_ctx_part)
    small_parts = [loss] + [gv[n] for n in SMALL_SUMMED]
    small_sum = sum_slots("sum_small", all_gather("gather_small_grads", _pack(small_parts, 8))).reshape(-1)
    summed = _unpack(small_sum, [p.shape for p in small_parts])
    loss_out = summed[0][0, 0]
    grads = {n: g.reshape(weights[n].shape) if n not in ("ssd_conv_w", "ffn_conv_w") else g for n, g in zip(SMALL_SUMMED, summed[1:])}
    for n in ("ssd_conv_w", "ffn_conv_w"):
        cols = weights[n].shape[2]
        grads[n] = lax.dynamic_slice(grads[n], (0, me * cols), (grads[n].shape[0], cols)).reshape(weights[n].shape)
    grads["b_mod"] = g_b_mod.reshape(b_mod.shape)
    grads["w_mod"] = g_w_mod.reshape(w_mod.shape)

    g_whole = grads_from_internal(*[gm[k] for k in ("w_in_p", "w_q_p", "w_kv", "w_out_p", "w_up", "w_down")])
    send = jnp.concatenate([_per_device(g, n) for g, n in zip(g_whole, MATRICES)], axis=1)
    n_send = send.shape[1]
    send = jnp.pad(send, ((0, 0), (0, -(-n_send // (LANE * PACK_ROWS)) * LANE * PACK_ROWS - n_send))).reshape(N_DEV, -1, LANE)
    g_shards = sum_slots("sum_matrix_grads", all_to_all("exchange_matrix_grads", send)).reshape(-1)
    for n, g in zip(MATRICES, _unpack(g_shards, mat_shapes)):
        grads[n] = g.reshape(weights[n].shape)

    shapes = [weights[n].shape for n in WEIGHT_ORDER]
    packs = [_pack([t[n] for n in WEIGHT_ORDER], PACK_ROWS) for t in (weights, grads, mom1, mom2)]
    delta, new_m, new_v = [_unpack(p.reshape(-1), shapes) for p in adamw_call(*packs)]
    return (loss_out, grad_x, *[grads[n] for n in WEIGHT_ORDER], *delta, *new_m, *new_v)
```

```python
import functools
import math

import jax
import jax.numpy as jnp
import numpy as np
from jax import lax
from jax.experimental import pallas as pl
from jax.experimental.pallas import tpu as pltpu

F32 = jnp.float32
BF16 = jnp.bfloat16
MESH = pl.DeviceIdType.MESH

D_MODEL = 1024
GRID_W = 64
N_HEADS = 16
NOPE = 64
ROPE = 32
V_DIM = 64
Q_RANK = 384
KV_RANK = 256
ROPE_THETA = 10000.0
ATTN_SCALE = (NOPE + ROPE) ** -0.5
SSD_HEADS = 16
SSD_P = 64
SSD_GROUPS = 2
SSD_N = 128
SSD_K = 5
CHUNK = 128
D_INNER = SSD_HEADS * SSD_P
GN = SSD_GROUPS * SSD_N
XBC = D_INNER + 2 * GN
D_FF = 2816
FFN_K = 3
N_MOD = 6
EPS = 1e-6
IN_SPLITS = (Q_RANK, KV_RANK, ROPE, D_INNER, XBC, 2 * SSD_HEADS)
IN_WIDTH = sum(IN_SPLITS)
N_DEV = 8

ADAM_LR = 0.001
ADAM_B1 = 0.9
ADAM_B2 = 0.999
ADAM_EPS = 1e-08
ADAM_WD = 0.01
ADAM_STEP = 10

LANE = 128
HEAD_BLOCK = 128
OFF_CQ = 0
OFF_KR = 384
OFF_CKV = 512
OFF_Z = 1024
OFF_XBC = 2048
OFF_DT = 3584
WIN_P = 3840
KR_LANE = 64
QP = N_HEADS * HEAD_BLOCK

VMEM_LIMIT_V7X = 56 * 1024 * 1024
NEG_BIG = -1e30


def _cparams(*sem):
    return pltpu.CompilerParams(dimension_semantics=sem, vmem_limit_bytes=VMEM_LIMIT_V7X)


def _tile(n, target, mult=128):
    if n <= target:
        return n
    t = (target // mult) * mult
    while t >= mult:
        if n % t == 0:
            return t
        t -= mult
    return n


def _silu(x):
    return x * jax.nn.sigmoid(x)


def _rms(x, g):
    return x * lax.rsqrt(jnp.mean(x * x, axis=-1, keepdims=True) + EPS) * g


def matmul(name, pairs, mode, out_dtype, *, bias=None, silu_a=False, tm=512, tn=512):
    n_pairs = len(pairs)
    M = pairs[0][0].shape[0]
    N = pairs[0][1].shape[1] if mode == "nn" else pairs[0][1].shape[0]
    tm = _tile(M, tm, 8)
    tn = _tile(N, tn)
    dims = (((1,), (0,)), ((), ())) if mode == "nn" else (((1,), (1,)), ((), ()))

    def body(*refs):
        o_ref = refs[-1]
        acc = None
        for p in range(n_pairs):
            a = refs[2 * p][...]
            if silu_a:
                a = _silu(a.astype(F32))
            d = lax.dot_general(a.astype(BF16), refs[2 * p + 1][...].astype(BF16), dims, preferred_element_type=F32)
            acc = d if acc is None else acc + d
        if bias is not None:
            acc = acc + refs[2 * n_pairs][...]
        o_ref[...] = acc.astype(o_ref.dtype)

    in_specs, args = [], []
    for a, b in pairs:
        K = a.shape[1]
        in_specs.append(pl.BlockSpec((tm, K), lambda j, i: (i, 0)))
        in_specs.append(pl.BlockSpec((K, tn), lambda j, i: (0, j)) if mode == "nn" else pl.BlockSpec((tn, K), lambda j, i: (j, 0)))
        args += [a, b]
    if bias is not None:
        in_specs.append(pl.BlockSpec((1, tn), lambda j, i: (0, j)))
        args.append(bias)
    return pl.pallas_call(
        body, name=name, grid=(N // tn, M // tm), in_specs=in_specs,
        out_specs=pl.BlockSpec((tm, tn), lambda j, i: (i, j)),
        out_shape=jax.ShapeDtypeStruct((M, N), out_dtype),
        compiler_params=_cparams("arbitrary", "arbitrary"),
    )(*args)


def matmul_tn(name, a, b, out_dtype=F32, *, silu_a=False, tm=512, tn=512, tk=512):
    R, M = a.shape
    N = b.shape[1]
    tm = _tile(M, tm)
    tn = _tile(N, tn)
    tk = _tile(R, tk, 8)
    nk = R // tk

    def body(a_ref, b_ref, o_ref, acc):
        k = pl.program_id(2)

        @pl.when(k == 0)
        def _():
            acc[...] = jnp.zeros_like(acc)

        x = a_ref[...]
        if silu_a:
            x = _silu(x.astype(F32))
        acc[...] += lax.dot_general(x.astype(BF16), b_ref[...].astype(BF16), (((0,), (0,)), ((), ())),
                                    preferred_element_type=F32)

        @pl.when(k == nk - 1)
        def _():
            o_ref[...] = acc[...].astype(o_ref.dtype)

    return pl.pallas_call(
        body, name=name, grid=(M // tm, N // tn, nk),
        in_specs=[pl.BlockSpec((tk, tm), lambda i, j, k: (k, i)), pl.BlockSpec((tk, tn), lambda i, j, k: (k, j))],
        out_specs=pl.BlockSpec((tm, tn), lambda i, j, k: (i, j)),
        out_shape=jax.ShapeDtypeStruct((M, N), out_dtype),
        scratch_shapes=[pltpu.VMEM((tm, tn), F32)],
        compiler_params=_cparams("arbitrary", "arbitrary", "arbitrary"),
    )(a, b)


def _row_specs(rin, pbin, glin, tr):
    specs = [pl.BlockSpec((1, tr, w), lambda b, i, cb=cb, ro=ro: (b, i + ro, cb)) for (_, w, cb, ro) in rin]
    specs += [pl.BlockSpec((1, 1, a.shape[-1]), lambda b, i: (b, 0, 0)) for a in pbin]
    specs += [pl.BlockSpec((1, a.shape[-1]), lambda b, i: (0, 0)) for a in glin]
    return specs


def rows_fwd(name, fn, nb, nblk, tr, rin, pbin, glin, outs):
    nr, npb, ngl = len(rin), len(pbin), len(glin)
    n_in = nr + npb + ngl

    def body(*refs):
        args = [r[0].astype(F32) for r in refs[:nr + npb]] + [r[...] for r in refs[nr + npb:n_in]]
        res = fn(*args)
        for o, v in zip(refs[n_in:], res):
            o[0] = v.astype(o.dtype)

    return pl.pallas_call(
        body, name=name, grid=(nb, nblk), in_specs=_row_specs(rin, pbin, glin, tr),
        out_specs=[pl.BlockSpec((1, tr, w), lambda b, i: (b, i, 0)) for (w, _) in outs],
        out_shape=[jax.ShapeDtypeStruct((nb, nblk * tr, w), dt) for (w, dt) in outs],
        compiler_params=_cparams("arbitrary", "arbitrary"),
    )(*[a for (a, _, _, _) in rin], *pbin, *glin)


def rows_bwd(name, fn, nb, nblk, tr, rin, pbin, glin, cts, want):
    nr, npb, ngl, nct = len(rin), len(pbin), len(glin), len(cts)
    n_in = nr + npb + ngl

    def body(*refs):
        b, i = pl.program_id(0), pl.program_id(1)
        args = [r[0].astype(F32) for r in refs[:nr + npb]] + [r[...] for r in refs[nr + npb:n_in]]
        ct = tuple(r[0].astype(F32) for r in refs[n_in:n_in + nct])
        _, vjp = jax.vjp(fn, *args)
        g = vjp(ct)
        orefs = refs[n_in + nct:]
        for o, (idx, _) in zip(orefs, want):
            o[0] = g[idx].astype(o.dtype)
        pb_refs = orefs[len(want):len(want) + npb]
        gl_refs = orefs[len(want) + npb:]

        @pl.when(i == 0)
        def _():
            for o, v in zip(pb_refs, g[nr:nr + npb]):
                o[0] = v

        @pl.when(i > 0)
        def _():
            for o, v in zip(pb_refs, g[nr:nr + npb]):
                o[0] += v

        first = jnp.logical_and(b == 0, i == 0)

        @pl.when(first)
        def _():
            for o, v in zip(gl_refs, g[nr + npb:]):
                o[...] = v

        @pl.when(jnp.logical_not(first))
        def _():
            for o, v in zip(gl_refs, g[nr + npb:]):
                o[...] += v

    out_specs = [pl.BlockSpec((1, tr, rin[idx][1]), lambda b, i: (b, i, 0)) for (idx, _) in want]
    out_shape = [jax.ShapeDtypeStruct((nb, nblk * tr, rin[idx][1]), dt) for (idx, dt) in want]
    out_specs += [pl.BlockSpec((1, 1, a.shape[-1]), lambda b, i: (b, 0, 0)) for a in pbin]
    out_shape += [jax.ShapeDtypeStruct((nb, 1, a.shape[-1]), F32) for a in pbin]
    out_specs += [pl.BlockSpec((1, a.shape[-1]), lambda b, i: (0, 0)) for a in glin]
    out_shape += [jax.ShapeDtypeStruct((1, a.shape[-1]), F32) for a in glin]
    return pl.pallas_call(
        body, name=name, grid=(nb, nblk),
        in_specs=_row_specs(rin, pbin, glin, tr) + _row_specs(cts, [], [], tr),
        out_specs=out_specs, out_shape=out_shape,
        compiler_params=_cparams("arbitrary", "arbitrary"),
    )(*[a for (a, _, _, _) in rin], *pbin, *glin, *[a for (a, _, _, _) in cts])


def ew_call(name, fn, ins, outs):
    def body(*refs):
        res = fn(*[r[...] for r in refs[:len(ins)]])
        for o, v in zip(refs[len(ins):], res):
            o[...] = v.astype(o.dtype)

    return pl.pallas_call(body, name=name, out_shape=[jax.ShapeDtypeStruct(s, dt) for (s, dt) in outs])(*ins)


def fn_prenorm(x, shift, scale, g):
    return (_rms(x, g) * (1.0 + scale) + shift,)


def fn_rms(x, g):
    return (_rms(x, g),)


def fn_ssd_finish(yf, yr, xs, z, dexp, nw):
    y = yf + yr + dexp * xs
    return (_rms(y * _silu(z), nw),)


def fn_postmix(x, mix, gate1, scale2, shift2, post_g, pre_g):
    x1 = x + gate1 * _rms(mix, post_g)
    h2 = _rms(x1, pre_g) * (1.0 + scale2) + shift2
    return x1, h2


def final_call(x1, ffn, target, gate2, post_g, tr):
    nb, S, D = x1.shape
    nblk = S // tr

    def body(x1_ref, f_ref, t_ref, g2_ref, pg_ref, dx1_ref, df_ref, dg2_ref, dpg_ref, loss_ref):
        b, i = pl.program_id(0), pl.program_id(1)
        tgt = t_ref[0]

        def lossfn(x1v, fv, g2, pg):
            e = x1v + g2 * _rms(fv, pg) - tgt
            return 0.5 * jnp.sum(jnp.mean(e * e, axis=-1, keepdims=True))

        val, (dx1, df, dg2, dpg) = jax.value_and_grad(lossfn, argnums=(0, 1, 2, 3))(
            x1_ref[0], f_ref[0].astype(F32), g2_ref[0], pg_ref[...])
        dx1_ref[0] = dx1
        df_ref[0] = df.astype(df_ref.dtype)
        lv = jnp.full((1, LANE), val, F32)

        @pl.when(i == 0)
        def _():
            dg2_ref[0] = dg2

        @pl.when(i > 0)
        def _():
            dg2_ref[0] += dg2

        first = jnp.logical_and(b == 0, i == 0)

        @pl.when(first)
        def _():
            dpg_ref[...] = dpg
            loss_ref[...] = lv

        @pl.when(jnp.logical_not(first))
        def _():
            dpg_ref[...] += dpg
            loss_ref[...] += lv

    row = pl.BlockSpec((1, tr, D), lambda b, i: (b, i, 0))
    pb = pl.BlockSpec((1, 1, D), lambda b, i: (b, 0, 0))
    gl = pl.BlockSpec((1, D), lambda b, i: (0, 0))
    return pl.pallas_call(
        body, name="loss_head", grid=(nb, nblk), in_specs=[row, row, row, pb, gl],
        out_specs=[row, row, pb, gl, pl.BlockSpec((1, LANE), lambda b, i: (0, 0))],
        out_shape=[jax.ShapeDtypeStruct((nb, S, D), F32), jax.ShapeDtypeStruct((nb, S, D), BF16),
                   jax.ShapeDtypeStruct((nb, 1, D), F32), jax.ShapeDtypeStruct((1, D), F32),
                   jax.ShapeDtypeStruct((1, LANE), F32)],
        compiler_params=_cparams("arbitrary", "arbitrary"),
    )(x1, ffn, target, gate2, post_g)


def _rotate_half(t):
    lane = lax.broadcasted_iota(jnp.int32, t.shape, 1)
    return jnp.where((lane & 15) < 8, -pltpu.roll(t, LANE - 8, 1), pltpu.roll(t, 8, 1))


def rope_call(name, x, width, colblk, cos, sin, out_dtype, tr):
    nb = x.shape[0]
    R = cos.shape[0]
    nblk = R // tr

    def body(x_ref, c_ref, s_ref, o_ref):
        c, s = c_ref[...], s_ref[...]
        for h in range(width // LANE):
            t = x_ref[0, :, h * LANE:(h + 1) * LANE].astype(F32)
            o_ref[0, :, h * LANE:(h + 1) * LANE] = (t * c + _rotate_half(t) * s).astype(o_ref.dtype)

    tab = pl.BlockSpec((tr, LANE), lambda b, i: (i, 0))
    return pl.pallas_call(
        body, name=name, grid=(nb, nblk),
        in_specs=[pl.BlockSpec((1, tr, width), lambda b, i: (b, i, colblk)), tab, tab],
        out_specs=pl.BlockSpec((1, tr, width), lambda b, i: (b, i, 0)),
        out_shape=jax.ShapeDtypeStruct((nb, R, width), out_dtype),
        compiler_params=_cparams("arbitrary", "arbitrary"),
    )(x, cos, sin)


def rope_tables(n_ctx, seq):
    n_rows = seq // GRID_W
    row = np.repeat(np.arange(n_rows), GRID_W).astype(np.float32)
    col = np.tile(np.arange(GRID_W), n_rows).astype(np.float32)
    axis_dim = ROPE // 2
    inv_freq = jnp.asarray(ROPE_THETA, F32) ** (-jnp.arange(0, axis_dim, 2, dtype=F32) / axis_dim)
    ang_r = jnp.asarray(row)[:, None] * inv_freq
    ang_c = jnp.asarray(col)[:, None] * inv_freq
    ang = jnp.concatenate([ang_r, ang_r, ang_c, ang_c], axis=-1)
    cos = jnp.ones((n_ctx + seq, LANE), F32).at[n_ctx:, KR_LANE:KR_LANE + ROPE].set(jnp.cos(ang))
    sin = jnp.zeros((n_ctx + seq, LANE), F32).at[n_ctx:, KR_LANE:KR_LANE + ROPE].set(jnp.sin(ang))
    return cos, sin


def _attn_probs(q, kc):
    s = lax.dot_general(q, kc, (((1,), (1,)), ((), ())), preferred_element_type=F32) * ATTN_SCALE
    p = jnp.exp(s - jnp.max(s, axis=1, keepdims=True))
    return p * (1.0 / jnp.sum(p, axis=1, keepdims=True))


def _key_block(kv, kr):
    lane = lax.broadcasted_iota(jnp.int32, kv.shape, 1)
    return jnp.where(lane < NOPE, kv, kr)


def attn_fwd(q, kv, kr, tq):
    nb, S, _ = q.shape
    T = kv.shape[1]

    def body(q_ref, kv_ref, kr_ref, o_ref):
        kvv = kv_ref[0]
        p = _attn_probs(q_ref[0], _key_block(kvv, kr_ref[0]))
        o = lax.dot_general(p.astype(BF16), kvv, (((1,), (0,)), ((), ())), preferred_element_type=F32)
        lane = lax.broadcasted_iota(jnp.int32, o.shape, 1)
        o_ref[0] = jnp.where(lane >= NOPE, o, 0.0).astype(o_ref.dtype)

    return pl.pallas_call(
        body, name="attn_fwd", grid=(nb, N_HEADS, S // tq),
        in_specs=[pl.BlockSpec((1, tq, HEAD_BLOCK), lambda b, h, i: (b, i, h)),
                  pl.BlockSpec((1, T, HEAD_BLOCK), lambda b, h, i: (b, 0, h)),
                  pl.BlockSpec((1, T, HEAD_BLOCK), lambda b, h, i: (b, 0, 0))],
        out_specs=pl.BlockSpec((1, tq, HEAD_BLOCK), lambda b, h, i: (b, i, h)),
        out_shape=jax.ShapeDtypeStruct((nb, S, QP), BF16),
        compiler_params=_cparams("arbitrary", "arbitrary", "arbitrary"),
    )(q, kv, kr)


def attn_bwd(q, kv, kr, do, tq):
    nb, S, _ = q.shape
    T = kv.shape[1]

    def body(q_ref, kv_ref, kr_ref, do_ref, dq_ref, dkv_ref, dkr_ref):
        h, i = pl.program_id(1), pl.program_id(2)
        qv, kvv, dov = q_ref[0], kv_ref[0], do_ref[0]
        kc = _key_block(kvv, kr_ref[0])
        p = _attn_probs(qv, kc)
        dp = lax.dot_general(dov, kvv, (((1,), (1,)), ((), ())), preferred_element_type=F32)
        ds = (p * (dp - jnp.sum(dp * p, axis=1, keepdims=True)) * ATTN_SCALE).astype(BF16)
        dq_ref[0] = lax.dot_general(ds, kc, (((1,), (0,)), ((), ())), preferred_element_type=F32).astype(dq_ref.dtype)
        dkc = lax.dot_general(ds, qv, (((0,), (0,)), ((), ())), preferred_element_type=F32)
        dv = lax.dot_general(p.astype(BF16), dov, (((0,), (0,)), ((), ())), preferred_element_type=F32)
        lane = lax.broadcasted_iota(jnp.int32, dkc.shape, 1)
        dkv = jnp.where(lane < NOPE, dkc, dv)
        dkr = jnp.where(lane >= NOPE, dkc, 0.0)

        @pl.when(i == 0)
        def _():
            dkv_ref[0] = dkv

        @pl.when(i > 0)
        def _():
            dkv_ref[0] += dkv

        first = jnp.logical_and(h == 0, i == 0)

        @pl.when(first)
        def _():
            dkr_ref[0] = dkr

        @pl.when(jnp.logical_not(first))
        def _():
            dkr_ref[0] += dkr

    qspec = pl.BlockSpec((1, tq, HEAD_BLOCK), lambda b, h, i: (b, i, h))
    kspec = pl.BlockSpec((1, T, HEAD_BLOCK), lambda b, h, i: (b, 0, h))
    rspec = pl.BlockSpec((1, T, HEAD_BLOCK), lambda b, h, i: (b, 0, 0))
    return pl.pallas_call(
        body, name="attn_bwd", grid=(nb, N_HEADS, S // tq),
        in_specs=[qspec, kspec, rspec, qspec], out_specs=[qspec, kspec, rspec],
        out_shape=[jax.ShapeDtypeStruct((nb, S, QP), F32), jax.ShapeDtypeStruct((nb, T, QP), F32),
                   jax.ShapeDtypeStruct((nb, T, HEAD_BLOCK), F32)],
        compiler_params=_cparams("arbitrary", "arbitrary", "arbitrary"),
    )(q, kv, kr, do)


def _seg_bounds(n, n_ctx):
    t = lax.broadcasted_iota(jnp.int32, (n, 1), 0)
    if n_ctx == 0:
        return t, jnp.zeros_like(t), jnp.full_like(t, n)
    in_ctx = t < n_ctx
    return t, jnp.where(in_ctx, 0, n_ctx), jnp.where(in_ctx, n_ctx, n)


def _shift_rows(x, o, bounds):
    if o == 0:
        return x
    t, lo, hi = bounds
    n = x.shape[0]
    valid = jnp.logical_and(t + o >= lo, t + o < hi).astype(F32)
    return pltpu.roll(x, (-o) % n, 0) * valid


def _conv(x, w, bias, k, bounds):
    acc = bias
    for o in range(k):
        acc = acc + w[o:o + 1, :] * _shift_rows(x, o - k // 2, bounds)
    return acc


def _conv_bwd(x, w, dpre, k, bounds):
    dx = jnp.zeros_like(x)
    rows = []
    for o in range(k):
        dx = dx + w[o:o + 1, :] * _shift_rows(dpre, -(o - k // 2), bounds)
        rows.append(jnp.sum(dpre * _shift_rows(x, o - k // 2, bounds), axis=0, keepdims=True))
    rows.append(jnp.sum(dpre, axis=0, keepdims=True))
    sub8 = lax.broadcasted_iota(jnp.int32, (8, x.shape[1]), 0)
    out = jnp.zeros((8, x.shape[1]), F32)
    for o, r in enumerate(rows):
        out = out + jnp.where(sub8 == o, r, 0.0)
    return dx, out


def _gelu(x):
    return 0.5 * x * (1.0 + lax.erf(x * (1.0 / math.sqrt(2.0))))


def _gelu_grad(x):
    return 0.5 * (1.0 + lax.erf(x * (1.0 / math.sqrt(2.0)))) + x * jnp.exp(-0.5 * x * x) * (1.0 / math.sqrt(2.0 * math.pi))


def ssd_conv_fwd(u, w8, bias, n_ctx, tc):
    nb, T, _ = u.shape
    cb0 = OFF_XBC // tc

    def body(x_ref, w_ref, b_ref, o_ref):
        pre = _conv(x_ref[0], w_ref[...], b_ref[...], SSD_K, _seg_bounds(T, n_ctx))
        o_ref[0] = _silu(pre)

    return pl.pallas_call(
        body, name="ssd_conv_fwd", grid=(nb, XBC // tc),
        in_specs=[pl.BlockSpec((1, T, tc), lambda b, j: (b, 0, cb0 + j)),
                  pl.BlockSpec((8, tc), lambda b, j: (0, j)), pl.BlockSpec((1, tc), lambda b, j: (0, j))],
        out_specs=pl.BlockSpec((1, T, tc), lambda b, j: (b, 0, j)),
        out_shape=jax.ShapeDtypeStruct((nb, T, XBC), F32),
        compiler_params=_cparams("arbitrary", "arbitrary"),
    )(u, w8, bias)


def ssd_conv_bwd(u, w8, bias, dxbc, dxs_direct, n_ctx, tc):
    nb, T, _ = u.shape
    cb0 = OFF_XBC // tc
    n_direct = D_INNER // tc

    def body(x_ref, w_ref, b_ref, d0_ref, d1_ref, dd_ref, dx_ref, dw_ref, acc):
        j, b = pl.program_id(0), pl.program_id(1)
        acc[...] = d0_ref[0, 0] + d1_ref[0, 0]

        @pl.when(j < n_direct)
        def _():
            acc[n_ctx:, :] += dd_ref[0]

        bounds = _seg_bounds(T, n_ctx)
        x, w = x_ref[0], w_ref[...]
        pre = _conv(x, w, b_ref[...], SSD_K, bounds)
        sg = jax.nn.sigmoid(pre)
        dpre = acc[...] * (sg * (1.0 + pre * (1.0 - sg)))
        dx, rows = _conv_bwd(x, w, dpre, SSD_K, bounds)
        dx_ref[0] = dx.astype(dx_ref.dtype)

        @pl.when(b == 0)
        def _():
            dw_ref[...] = rows

        @pl.when(b > 0)
        def _():
            dw_ref[...] += rows

    dspec0 = pl.BlockSpec((1, 1, T, tc), lambda j, b: (0, b, 0, j))
    dspec1 = pl.BlockSpec((1, 1, T, tc), lambda j, b: (1, b, 0, j))
    return pl.pallas_call(
        body, name="ssd_conv_bwd", grid=(XBC // tc, nb),
        in_specs=[pl.BlockSpec((1, T, tc), lambda j, b: (b, 0, cb0 + j)),
                  pl.BlockSpec((8, tc), lambda j, b: (0, j)), pl.BlockSpec((1, tc), lambda j, b: (0, j)),
                  dspec0, dspec1,
                  pl.BlockSpec((1, T - n_ctx, tc), lambda j, b: (b, 0, jnp.minimum(j, n_direct - 1)))],
        out_specs=[pl.BlockSpec((1, T, tc), lambda j, b: (b, 0, j)), pl.BlockSpec((8, tc), lambda j, b: (0, j))],
        out_shape=[jax.ShapeDtypeStruct((nb, T, XBC), BF16), jax.ShapeDtypeStruct((8, XBC), F32)],
        scratch_shapes=[pltpu.VMEM((T, tc), F32)],
        compiler_params=_cparams("arbitrary", "arbitrary"),
    )(u, w8, bias, dxbc, dxbc, dxs_direct)


def glu_fwd(up, w8, bias, tc):
    nb, S, _ = up.shape
    nj = D_FF // tc

    def body(g_ref, v_ref, w_ref, b_ref, o_ref):
        gc = _conv(g_ref[0], w_ref[...], b_ref[...], FFN_K, _seg_bounds(S, 0))
        o_ref[0] = (_gelu(gc) * v_ref[0]).astype(o_ref.dtype)

    return pl.pallas_call(
        body, name="glu_fwd", grid=(nb, nj),
        in_specs=[pl.BlockSpec((1, S, tc), lambda b, j: (b, 0, j)), pl.BlockSpec((1, S, tc), lambda b, j: (b, 0, nj + j)),
                  pl.BlockSpec((8, tc), lambda b, j: (0, j)), pl.BlockSpec((1, tc), lambda b, j: (0, j))],
        out_specs=pl.BlockSpec((1, S, tc), lambda b, j: (b, 0, j)),
        out_shape=jax.ShapeDtypeStruct((nb, S, D_FF), BF16),
        compiler_params=_cparams("arbitrary", "arbitrary"),
    )(up, up, w8, bias)


def glu_bwd(up, w8, bias, dact, tc):
    nb, S, _ = up.shape
    nj = D_FF // tc

    def body(g_ref, v_ref, w_ref, b_ref, d_ref, dg_ref, dv_ref, dw_ref):
        b = pl.program_id(1)
        bounds = _seg_bounds(S, 0)
        x, w, val, d = g_ref[0], w_ref[...], v_ref[0], d_ref[0].astype(F32)
        gc = _conv(x, w, b_ref[...], FFN_K, bounds)
        dv_ref[0] = (d * _gelu(gc)).astype(dv_ref.dtype)
        dx, rows = _conv_bwd(x, w, d * val * _gelu_grad(gc), FFN_K, bounds)
        dg_ref[0] = dx.astype(dg_ref.dtype)

        @pl.when(b == 0)
        def _():
            dw_ref[...] = rows

        @pl.when(b > 0)
        def _():
            dw_ref[...] += rows

    col = pl.BlockSpec((1, S, tc), lambda j, b: (b, 0, j))
    return pl.pallas_call(
        body, name="glu_bwd", grid=(nj, nb),
        in_specs=[col, pl.BlockSpec((1, S, tc), lambda j, b: (b, 0, nj + j)),
                  pl.BlockSpec((8, tc), lambda j, b: (0, j)), pl.BlockSpec((1, tc), lambda j, b: (0, j)), col],
        out_specs=[col, col, pl.BlockSpec((8, tc), lambda j, b: (0, j))],
        out_shape=[jax.ShapeDtypeStruct((nb, S, D_FF), BF16), jax.ShapeDtypeStruct((nb, S, D_FF), BF16),
                   jax.ShapeDtypeStruct((8, D_FF), F32)],
        compiler_params=_cparams("arbitrary", "arbitrary"),
    )(up, up, w8, bias, dact)


def _chunk_of(d, k, n_cc, n_ch):
    rev = jnp.where(k < n_cc, n_cc - 1 - k, n_cc + n_ch - 1 - k)
    return jnp.where(d == 1, rev, k)


def _lane_pick(v, lane_iota, l):
    return jnp.sum(jnp.where(lane_iota == l, v, 0.0), axis=1, keepdims=True)


def _row_pick(v, sub_iota, l):
    return jnp.sum(jnp.where(sub_iota == l, v, 0.0), axis=0, keepdims=True)


def _softplus(x):
    return jnp.maximum(x, 0.0) + jnp.log(1.0 + jnp.exp(-jnp.abs(x)))


def _ssd_common(d, dt_raw, alog, dtb):
    Q = dt_raw.shape[0]
    row = lax.broadcasted_iota(jnp.int32, (Q, Q), 0)
    col = lax.broadcasted_iota(jnp.int32, (Q, Q), 1)
    rev = d == 1
    maskb = jnp.where(rev, row, col) <= jnp.where(rev, col, row)
    tri = maskb.astype(F32)
    A = -jnp.exp(alog)
    dtv = _softplus(dt_raw + dtb)
    a = dtv * A
    cum = lax.dot_general(tri, a, (((1,), (0,)), ((), ())), precision=lax.Precision.HIGHEST, preferred_element_type=F32)
    tot = jnp.sum(a, axis=0, keepdims=True)
    return maskb, tri, A, dtv, cum, tot


def ssd_fwd(xbc, u, alog, dtb, n_ctx):
    nb, T, _ = xbc.shape
    S = T - n_ctx
    n_ch, n_cc = T // CHUNK, n_ctx // CHUNK
    dt_cb = OFF_DT // LANE
    Q = CHUNK
    n_pairs = SSD_HEADS // 2

    def body(x_ref, dt_ref, al_ref, db_ref, y_ref, hin_ref, H):
        d, k = pl.program_id(1), pl.program_id(2)

        @pl.when(k == 0)
        def _():
            H[...] = jnp.zeros_like(H)

        maskb, tri, A, dtv, cum, tot = _ssd_common(d, dt_ref[0], al_ref[...], db_ref[...])
        cumT = cum.T
        hin_ref[0, 0, 0] = H[...].astype(BF16)
        lane = lax.broadcasted_iota(jnp.int32, (Q, LANE), 1)
        lane1 = lax.broadcasted_iota(jnp.int32, (1, LANE), 1)
        sub = lax.broadcasted_iota(jnp.int32, (LANE, Q), 0)
        subc = lax.broadcasted_iota(jnp.int32, (LANE, 1), 0)
        half = lane < SSD_P
        for g in range(SSD_GROUPS):
            Bg = x_ref[0, :, D_INNER + g * SSD_N:D_INNER + (g + 1) * SSD_N].astype(BF16)
            Cg = x_ref[0, :, D_INNER + GN + g * SSD_N:D_INNER + GN + (g + 1) * SSD_N].astype(BF16)
            Gm = lax.dot_general(Cg, Bg, (((1,), (1,)), ((), ())), preferred_element_type=F32)
            for pr in range(n_pairs // SSD_GROUPS):
                p = g * (n_pairs // SSD_GROUPS) + pr
                l0 = d * SSD_HEADS + 2 * p
                l1 = l0 + 1
                s0c, s1c = _lane_pick(cum, lane, l0), _lane_pick(cum, lane, l1)
                s0r, s1r = _row_pick(cumT, sub, l0), _row_pick(cumT, sub, l1)
                dtp = jnp.where(half, _lane_pick(dtv, lane, l0), _lane_pick(dtv, lane, l1))
                tot0, tot1 = _lane_pick(tot, lane1, l0), _lane_pick(tot, lane1, l1)
                sc = jnp.where(half, s0c, s1c)
                totp = jnp.where(half, tot0, tot1)
                M0 = (Gm * jnp.exp(jnp.where(maskb, s0c - s0r, NEG_BIG))).astype(BF16)
                M1 = (Gm * jnp.exp(jnp.where(maskb, s1c - s1r, NEG_BIG))).astype(BF16)
                xd = x_ref[0, :, p * LANE:(p + 1) * LANE] * dtp
                xdb = xd.astype(BF16)
                yd = jnp.where(half,
                               lax.dot_general(M0, xdb, (((1,), (0,)), ((), ())), preferred_element_type=F32),
                               lax.dot_general(M1, xdb, (((1,), (0,)), ((), ())), preferred_element_type=F32))
                Hp = H[p * LANE:(p + 1) * LANE, :]
                yo = lax.dot_general(Cg, Hp.astype(BF16), (((1,), (1,)), ((), ())), preferred_element_type=F32) * jnp.exp(sc)

                @pl.when(k >= n_cc)
                def _():
                    y_ref[0, 0, :, p * LANE:(p + 1) * LANE] = yd + yo

                xdw = (xd * jnp.exp(totp - sc)).astype(BF16)
                etot = jnp.exp(jnp.where(subc < SSD_P, tot0, tot1))
                H[p * LANE:(p + 1) * LANE, :] = Hp * etot + lax.dot_general(
                    xdw, Bg, (((0,), (0,)), ((), ())), preferred_element_type=F32)

    def ymap(b, d, k):
        return (d, b, _chunk_of(d, jnp.maximum(k, n_cc), n_cc, n_ch) - n_cc, 0)

    return pl.pallas_call(
        body, name="ssd_fwd", grid=(nb, 2, n_ch),
        in_specs=[pl.BlockSpec((1, Q, XBC), lambda b, d, k: (b, _chunk_of(d, k, n_cc, n_ch), 0)),
                  pl.BlockSpec((1, Q, LANE), lambda b, d, k: (b, _chunk_of(d, k, n_cc, n_ch), dt_cb)),
                  pl.BlockSpec((1, LANE), lambda b, d, k: (0, 0)), pl.BlockSpec((1, LANE), lambda b, d, k: (0, 0))],
        out_specs=[pl.BlockSpec((1, 1, Q, D_INNER), ymap),
                   pl.BlockSpec((1, 1, 1, D_INNER, SSD_N), lambda b, d, k: (d, b, k, 0, 0))],
        out_shape=[jax.ShapeDtypeStruct((2, nb, S, D_INNER), F32),
                   jax.ShapeDtypeStruct((2, nb, n_ch, D_INNER, SSD_N), BF16)],
        scratch_shapes=[pltpu.VMEM((D_INNER, SSD_N), F32)],
        compiler_params=_cparams("arbitrary", "arbitrary", "arbitrary"),
    )(xbc, u, alog, dtb)


def ssd_bwd(xbc, u, alog, dtb, hin, dy, n_ctx):
    nb, T, _ = xbc.shape
    n_ch, n_cc = T // CHUNK, n_ctx // CHUNK
    dt_cb = OFF_DT // LANE
    Q = CHUNK
    n_pairs = SSD_HEADS // 2
    NT = (((1,), (1,)), ((), ()))
    NN = (((1,), (0,)), ((), ()))
    TN = (((0,), (0,)), ((), ()))

    def dot(a, b, dims):
        return lax.dot_general(a.astype(BF16), b.astype(BF16), dims, preferred_element_type=F32)

    def body(x_ref, dt_ref, al_ref, db_ref, hin_ref, dy_ref, dx_ref, ddt_ref, st_ref, dH):
        d, kk = pl.program_id(1), pl.program_id(2)
        ks = n_ch - 1 - kk

        @pl.when(kk == 0)
        def _():
            dH[...] = jnp.zeros_like(dH)

        @pl.when(jnp.logical_and(jnp.logical_and(pl.program_id(0) == 0, d == 0), kk == 0))
        def _():
            st_ref[...] = jnp.zeros_like(st_ref)

        dt_raw = dt_ref[0]
        alog, dtb_v = al_ref[...], db_ref[...]
        maskb, tri, A, dtv, cum, tot = _ssd_common(d, dt_raw, alog, dtb_v)
        cumT = cum.T
        live = (ks >= n_cc).astype(F32)
        lane = lax.broadcasted_iota(jnp.int32, (Q, LANE), 1)
        lane1 = lax.broadcasted_iota(jnp.int32, (1, LANE), 1)
        sub = lax.broadcasted_iota(jnp.int32, (LANE, Q), 0)
        subc = lax.broadcasted_iota(jnp.int32, (LANE, 1), 0)
        half = lane < SSD_P
        halfc = subc < SSD_P
        dcum = jnp.zeros((Q, LANE), F32)
        dcumT = jnp.zeros((LANE, Q), F32)
        ddt = jnp.zeros((Q, LANE), F32)
        dtot = jnp.zeros((1, LANE), F32)
        for g in range(SSD_GROUPS):
            Bg = x_ref[0, :, D_INNER + g * SSD_N:D_INNER + (g + 1) * SSD_N].astype(BF16)
            Cg = x_ref[0, :, D_INNER + GN + g * SSD_N:D_INNER + GN + (g + 1) * SSD_N].astype(BF16)
            Gm = lax.dot_general(Cg, Bg, NT, preferred_element_type=F32)
            dG = jnp.zeros((Q, Q), F32)
            dC = jnp.zeros((Q, SSD_N), F32)
            dB = jnp.zeros((Q, SSD_N), F32)
            for pr in range(n_pairs // SSD_GROUPS):
                p = g * (n_pairs // SSD_GROUPS) + pr
                l0 = d * SSD_HEADS + 2 * p
                l1 = l0 + 1
                s0c, s1c = _lane_pick(cum, lane, l0), _lane_pick(cum, lane, l1)
                s0r, s1r = _row_pick(cumT, sub, l0), _row_pick(cumT, sub, l1)
                dtp = jnp.where(half, _lane_pick(dtv, lane, l0), _lane_pick(dtv, lane, l1))
                tot0, tot1 = _lane_pick(tot, lane1, l0), _lane_pick(tot, lane1, l1)
                sc = jnp.where(half, s0c, s1c)
                totp = jnp.where(half, tot0, tot1)
                L0 = jnp.exp(jnp.where(maskb, s0c - s0r, NEG_BIG))
                L1 = jnp.exp(jnp.where(maskb, s1c - s1r, NEG_BIG))
                M0, M1 = Gm * L0, Gm * L1
                xs = x_ref[0, :, p * LANE:(p + 1) * LANE]
                xd = xs * dtp
                es = jnp.exp(sc)
                dte = jnp.exp(totp - sc)
                etot = jnp.exp(jnp.where(halfc, tot0, tot1))
                dyp = dy_ref[0, :, p * LANE:(p + 1) * LANE] * live
                Hp = hin_ref[0, 0, 0, p * LANE:(p + 1) * LANE, :]
                dHp = dH[p * LANE:(p + 1) * LANE, :]
                bdh = dot(Bg, dHp, NT)
                dxd = jnp.where(half, dot(M0, dyp, TN), dot(M1, dyp, TN)) + bdh * dte
                dy0 = jnp.where(half, dyp, 0.0)
                dy1 = dyp - dy0
                dM0, dM1 = dot(dy0, xd, NT), dot(dy1, xd, NT)
                dG = dG + dM0 * L0 + dM1 * L1
                dyes = dyp * es
                xdw = xd * dte
                dC = dC + dot(dyes, Hp, NN)
                dB = dB + dot(xdw, dHp, NN)
                W0, W1 = dM0 * M0, dM1 * M1
                yoff = dot(Cg, Hp, NT) * es
                r_off = dyp * yoff
                r_st = xd * bdh * dte
                hh = jnp.sum(dHp * Hp.astype(F32), axis=1, keepdims=True) * etot
                for (l, W, hsel, hselc) in ((l0, W0, half, halfc),
                                            (l1, W1, jnp.logical_not(half), jnp.logical_not(halfc))):
                    col_g = (jnp.sum(W, axis=1, keepdims=True)
                             + jnp.sum(jnp.where(hsel, r_off - r_st, 0.0), axis=1, keepdims=True))
                    row_g = -jnp.sum(W, axis=0, keepdims=True)
                    tot_g = (jnp.sum(jnp.sum(jnp.where(hsel, r_st, 0.0), axis=1, keepdims=True), axis=0, keepdims=True)
                             + jnp.sum(jnp.where(hselc, hh, 0.0), axis=0, keepdims=True))
                    dcum = dcum + jnp.where(lane == l, col_g, 0.0)
                    dcumT = dcumT + jnp.where(sub == l, row_g, 0.0)
                    dtot = dtot + jnp.where(lane1 == l, tot_g, 0.0)
                    ddt = ddt + jnp.where(lane == l, jnp.sum(jnp.where(hsel, dxd * xs, 0.0), axis=1, keepdims=True), 0.0)
                dx_ref[0, 0, :, p * LANE:(p + 1) * LANE] = dxd * dtp
                dH[p * LANE:(p + 1) * LANE, :] = dHp * etot + dot(dyes, Cg, TN)
            dx_ref[0, 0, :, D_INNER + g * SSD_N:D_INNER + (g + 1) * SSD_N] = dB + dot(dG, Cg, TN)
            dx_ref[0, 0, :, D_INNER + GN + g * SSD_N:D_INNER + GN + (g + 1) * SSD_N] = dC + dot(dG, Bg, NN)
        dcum_all = dcum + dcumT.T
        da = lax.dot_general(tri, dcum_all, TN, precision=lax.Precision.HIGHEST, preferred_element_type=F32) + dtot
        ddtv = ddt + da * A
        ddt_raw = ddtv * jax.nn.sigmoid(dt_raw + dtb_v)
        ddt_ref[0, 0] = ddt_raw
        st_ref[0:1, :] += jnp.sum(da * dtv * A, axis=0, keepdims=True)
        st_ref[1:2, :] += jnp.sum(ddt_raw, axis=0, keepdims=True)

    def cmap(d, kk):
        return _chunk_of(d, n_ch - 1 - kk, n_cc, n_ch)

    def dymap(b, d, kk):
        return (b, _chunk_of(d, jnp.maximum(n_ch - 1 - kk, n_cc), n_cc, n_ch) - n_cc, 0)

    return pl.pallas_call(
        body, name="ssd_bwd", grid=(nb, 2, n_ch),
        in_specs=[pl.BlockSpec((1, Q, XBC), lambda b, d, kk: (b, cmap(d, kk), 0)),
                  pl.BlockSpec((1, Q, LANE), lambda b, d, kk: (b, cmap(d, kk), dt_cb)),
                  pl.BlockSpec((1, LANE), lambda b, d, kk: (0, 0)), pl.BlockSpec((1, LANE), lambda b, d, kk: (0, 0)),
                  pl.BlockSpec((1, 1, 1, D_INNER, SSD_N), lambda b, d, kk: (d, b, n_ch - 1 - kk, 0, 0)),
                  pl.BlockSpec((1, Q, D_INNER), dymap)],
        out_specs=[pl.BlockSpec((1, 1, Q, XBC), lambda b, d, kk: (d, b, cmap(d, kk), 0)),
                   pl.BlockSpec((1, 1, Q, LANE), lambda b, d, kk: (d, b, cmap(d, kk), 0)),
                   pl.BlockSpec((8, LANE), lambda b, d, kk: (0, 0))],
        out_shape=[jax.ShapeDtypeStruct((2, nb, T, XBC), F32), jax.ShapeDtypeStruct((2, nb, T, LANE), F32),
                   jax.ShapeDtypeStruct((8, LANE), F32)],
        scratch_shapes=[pltpu.VMEM((D_INNER, SSD_N), F32)],
        compiler_params=_cparams("arbitrary", "arbitrary", "arbitrary"),
    )(xbc, u, alog, dtb, hin, dy)


def _adamw(w, g, m, v):
    mn = ADAM_B1 * m + (1.0 - ADAM_B1) * g
    vn = ADAM_B2 * v + (1.0 - ADAM_B2) * jnp.square(g)
    m_hat = mn / (1.0 - ADAM_B1 ** ADAM_STEP)
    v_hat = vn / (1.0 - ADAM_B2 ** ADAM_STEP)
    return -ADAM_LR * (m_hat / (jnp.sqrt(v_hat) + ADAM_EPS) + ADAM_WD * w), mn, vn


def adamw_matrix(name, w, g_slots, m, v):
    K, n = w.shape
    s = g_slots.shape[0]
    tr = _tile(K, 256, 8)

    def body(w_ref, g_ref, m_ref, v_ref, go_ref, d_ref, mo_ref, vo_ref):
        g = g_ref[0]
        for j in range(1, s):
            g = g + g_ref[j]
        go_ref[...] = g
        d_ref[...], mo_ref[...], vo_ref[...] = _adamw(w_ref[...], g, m_ref[...], v_ref[...])

    spec = pl.BlockSpec((tr, n), lambda i: (i, 0))
    return pl.pallas_call(
        body, name=name, grid=(K // tr,),
        in_specs=[spec, pl.BlockSpec((s, tr, n), lambda i: (0, i, 0)), spec, spec], out_specs=[spec] * 4,
        out_shape=[jax.ShapeDtypeStruct((K, n), F32)] * 4,
        compiler_params=_cparams("arbitrary"),
    )(w, g_slots, m, v)


def adamw_small(ws, gs, ms, vs):
    n = len(ws)

    def body(*refs):
        for i in range(n):
            d, mn, vn = _adamw(refs[i][...], refs[n + i][...], refs[2 * n + i][...], refs[3 * n + i][...])
            refs[4 * n + i][...] = d
            refs[5 * n + i][...] = mn
            refs[6 * n + i][...] = vn

    shapes = [jax.ShapeDtypeStruct(w.shape, F32) for w in ws]
    out = pl.pallas_call(body, name="adamw_small", out_shape=shapes * 3)(*ws, *gs, *ms, *vs)
    return out[:n], out[n:2 * n], out[2 * n:]


def sum_slots(name, x):
    n = x.shape[0]

    def fn(t):
        acc = t[0]
        for j in range(1, n):
            acc = acc + t[j]
        return (acc,)

    return ew_call(name, fn, [x], [(x.shape[1:], F32)])[0]


def _pack_rows(parts):
    rows = []
    for p in parts:
        flat = p.reshape(1, -1)
        n = flat.shape[1]
        rows.append(jnp.pad(flat, ((0, 0), (0, -(-n // (8 * LANE)) * 8 * LANE - n))).reshape(-1, LANE))
    return jnp.concatenate(rows, axis=0)


def _unpack_rows(pack, shapes):
    out, r = [], 0
    for s in shapes:
        n = int(np.prod(s))
        nr = -(-n // (8 * LANE)) * 8
        out.append(pack[r:r + nr].reshape(1, -1)[:, :n].reshape(s))
        r += nr
    return out


def _mesh_pos():
    return lax.axis_index("x"), lax.axis_index("y"), lax.axis_index("c")


N_PEERS = N_DEV - 1


def all_gather(name, vs):
    n = len(vs)

    def body(*refs):
        x_refs, out_refs = refs[:n], refs[n:2 * n]
        send_sems, recv_sems, local_sems = refs[2 * n:]
        x, y, c = _mesh_pos()
        me, sibling = (x, y, c), (x, y, 1 - c)
        chips = [(1 - x, y), (x, 1 - y), (1 - x, 1 - y)]

        def slot(a, px, py, pc):
            return out_refs[a].at[4 * px + 2 * py + pc]

        def copy(a, k, block, to, src=None):
            return pltpu.make_async_remote_copy(
                src_ref=slot(a, *block) if src is None else src, dst_ref=slot(a, *block),
                send_sem=send_sems.at[N_PEERS * a + k], recv_sem=recv_sems.at[N_PEERS * a + k],
                device_id=to, device_id_type=MESH)

        mine = [pltpu.make_async_copy(x_refs[a], slot(a, *me), local_sems.at[a]) for a in range(n)]
        for cp in mine:
            cp.start()
        first = []
        for a in range(n):
            first.append(copy(a, 0, me, sibling, src=x_refs[a]))
            first += [copy(a, 1 + j, me, (*chip, c), src=x_refs[a]) for j, chip in enumerate(chips)]
        for cp in first:
            cp.start()
        passed = []
        for j, chip in enumerate(chips):
            for a in range(n):
                copy(a, 1 + j, (*chip, c), me).wait_recv()
                passed.append(copy(a, 4 + j, (*chip, c), sibling))
                passed[-1].start()
        for a in range(n):
            copy(a, 0, sibling, me).wait_recv()
            for j, chip in enumerate(chips):
                copy(a, 4 + j, (*chip, 1 - c), me).wait_recv()
        for cp in first + passed:
            cp.wait_send()
        for cp in mine:
            cp.wait()

    hbm = pl.BlockSpec(memory_space=pl.ANY)
    return pl.pallas_call(
        body, name=name, out_shape=[jax.ShapeDtypeStruct((N_DEV,) + v.shape, v.dtype) for v in vs],
        in_specs=[hbm] * n, out_specs=[hbm] * n,
        scratch_shapes=[pltpu.SemaphoreType.DMA((N_PEERS * n,)), pltpu.SemaphoreType.DMA((N_PEERS * n,)),
                        pltpu.SemaphoreType.DMA((n,))],
    )(*vs)


def all_to_all(name, vs):
    n = len(vs)

    def body(*refs):
        x_refs, out_refs = refs[:n], refs[n:2 * n]
        send_sems, recv_sems, local_sems = refs[2 * n:]
        x, y, c = _mesh_pos()
        me = 4 * x + 2 * y + c
        mine = [pltpu.make_async_copy(x_refs[a].at[me], out_refs[a].at[me], local_sems.at[a]) for a in range(n)]
        for cp in mine:
            cp.start()
        copies = []
        for k in range(1, N_DEV):
            px, py, pc = x ^ ((k >> 2) & 1), y ^ ((k >> 1) & 1), c ^ (k & 1)
            for a in range(n):
                copies.append(pltpu.make_async_remote_copy(
                    src_ref=x_refs[a].at[4 * px + 2 * py + pc], dst_ref=out_refs[a].at[me],
                    send_sem=send_sems.at[N_PEERS * a + k - 1], recv_sem=recv_sems.at[N_PEERS * a + k - 1],
                    device_id=(px, py, pc), device_id_type=MESH))
        for cp in copies:
            cp.start()
        for cp in copies:
            cp.wait_recv()
        for cp in copies:
            cp.wait_send()
        for cp in mine:
            cp.wait()

    hbm = pl.BlockSpec(memory_space=pl.ANY)
    return pl.pallas_call(
        body, name=name, out_shape=[jax.ShapeDtypeStruct(v.shape, v.dtype) for v in vs],
        in_specs=[hbm] * n, out_specs=[hbm] * n,
        scratch_shapes=[pltpu.SemaphoreType.DMA((N_PEERS * n,)), pltpu.SemaphoreType.DMA((N_PEERS * n,)),
                        pltpu.SemaphoreType.DMA((n,))],
    )(*vs)


def _taps8(w):
    return jnp.concatenate([w, jnp.zeros((8 - w.shape[0], w.shape[1]), w.dtype)], axis=0)


def _lanes128(v):
    v = v.reshape(1, -1)
    return jnp.pad(v, ((0, 0), (0, LANE - v.shape[1])))


def weights_to_internal(w_in, w_q_up, w_kv_up, w_out, w_up, w_down):
    cq, ckv, kr, z, xbc, dt = jnp.split(w_in, np.cumsum(IN_SPLITS)[:-1].tolist(), axis=1)
    K = w_in.shape[0]

    def zeros(n):
        return jnp.zeros((K, n), w_in.dtype)

    w_in_p = jnp.concatenate([cq, zeros(KR_LANE), kr, zeros(LANE - KR_LANE - ROPE), ckv, zeros(OFF_Z - OFF_CKV - KV_RANK),
                              z, xbc, dt, zeros(WIN_P - OFF_DT - 2 * SSD_HEADS)], axis=1)
    w_q_p = jnp.pad(w_q_up.reshape(Q_RANK, N_HEADS, NOPE + ROPE), ((0, 0), (0, 0), (0, HEAD_BLOCK - NOPE - ROPE))).reshape(Q_RANK, QP)
    attn_rows = w_out[:N_HEADS * V_DIM].reshape(N_HEADS, V_DIM, -1)
    w_out_p = jnp.concatenate([jnp.pad(attn_rows, ((0, 0), (HEAD_BLOCK - V_DIM, 0), (0, 0))).reshape(QP, -1),
                               w_out[N_HEADS * V_DIM:]], axis=0)
    return dict(w_in_p=w_in_p, w_q_p=w_q_p, w_kv=w_kv_up, w_out_p=w_out_p, w_up=w_up, w_down=w_down)


def grads_from_internal(g_in_p, g_q_p, g_kv, g_out_p, g_up, g_down):
    g_in = jnp.concatenate([g_in_p[:, OFF_CQ:OFF_CQ + Q_RANK], g_in_p[:, OFF_CKV:OFF_CKV + KV_RANK],
                            g_in_p[:, OFF_KR + KR_LANE:OFF_KR + KR_LANE + ROPE], g_in_p[:, OFF_Z:OFF_Z + D_INNER],
                            g_in_p[:, OFF_XBC:OFF_XBC + XBC], g_in_p[:, OFF_DT:OFF_DT + 2 * SSD_HEADS]], axis=1)
    g_q = g_q_p.reshape(Q_RANK, N_HEADS, HEAD_BLOCK)[:, :, :NOPE + ROPE].reshape(Q_RANK, -1)
    g_out = jnp.concatenate([g_out_p[:QP].reshape(N_HEADS, HEAD_BLOCK, -1)[:, HEAD_BLOCK - V_DIM:].reshape(N_HEADS * V_DIM, -1),
                             g_out_p[QP:]], axis=0)
    return g_in, g_q, g_kv, g_out, g_up, g_down


def local_step(x, ctx, target, mod_x, mod_c, W, V):
    nb, S, D = x.shape
    C = ctx.shape[1]
    T = C + S
    tr = _tile(math.gcd(C, S), 256, 8)
    tq = _tile(S, 256, 8)
    tc = 256
    cblk = C // tr
    m = [mod_x[:, i * D:(i + 1) * D][:, None, :] for i in range(N_MOD)]
    mc = [mod_c[:, i * D:(i + 1) * D] for i in range(2)]
    ssd_w8, ffn_w8 = _taps8(V["ssd_conv_w"]), _taps8(V["ffn_conv_w"])
    alog, dtb = _lanes128(V["ssd_a_log"]), _lanes128(V["ssd_dt_bias"])
    dexp = jnp.repeat(V["ssd_d"].reshape(-1), SSD_P).reshape(1, D_INNER)
    cosT, sinT = rope_tables(C, S)
    cosS, sinS = cosT[C:], sinT[C:]

    (h1x,) = rows_fwd("prenorm_x", fn_prenorm, nb, S // tr, tr, [(x, D, 0, 0)], [m[0], m[1]], [V["mix_pre_norm"]], [(D, BF16)])
    (h1c,) = rows_fwd("prenorm_c", fn_prenorm, nb, C // tr, tr, [(ctx, D, 0, 0)], [], [mc[0], mc[1], V["mix_pre_norm"]], [(D, BF16)])
    h1 = jnp.concatenate([h1c, h1x], axis=1).reshape(nb * T, D)
    u = matmul("in_proj", [(h1, W["w_in_p"])], "nn", F32).reshape(nb, T, WIN_P)
    (qn,) = rows_fwd("q_norm", fn_rms, nb, S // tr, tr, [(u, Q_RANK, OFF_CQ // Q_RANK, cblk)], [], [V["q_norm"]], [(Q_RANK, BF16)])
    (kvn,) = rows_fwd("kv_norm", fn_rms, nb, T // tr, tr, [(u, KV_RANK, OFF_CKV // KV_RANK, 0)], [], [V["kv_norm"]], [(KV_RANK, BF16)])
    qn2, kvn2 = qn.reshape(nb * S, Q_RANK), kvn.reshape(nb * T, KV_RANK)
    q_raw = matmul("q_up", [(qn2, W["w_q_p"])], "nn", F32).reshape(nb, S, QP)
    kv = matmul("kv_up", [(kvn2, W["w_kv"])], "nn", BF16).reshape(nb, T, QP)
    q = rope_call("rope_q", q_raw, QP, 0, cosS, sinS, BF16, tr)
    kr = rope_call("rope_k", u, LANE, OFF_KR // LANE, cosT, sinT, BF16, tr)
    o = attn_fwd(q, kv, kr, tq)
    xbc = ssd_conv_fwd(u, ssd_w8, V["ssd_conv_b"], C, tc)
    y2, hin = ssd_fwd(xbc, u, alog, dtb, C)
    fin_rows = [(y2[0], D_INNER, 0, 0), (y2[1], D_INNER, 0, 0), (xbc, D_INNER, 0, cblk), (u, D_INNER, OFF_Z // D_INNER, cblk)]
    fin_gl = [dexp, V["ssd_norm"]]
    (ssd,) = rows_fwd("ssd_finish", fn_ssd_finish, nb, S // tr, tr, fin_rows, [], fin_gl, [(D_INNER, BF16)])
    o2, ssd2 = o.reshape(nb * S, QP), ssd.reshape(nb * S, D_INNER)
    mix = matmul("out_proj", [(o2, W["w_out_p"][:QP]), (ssd2, W["w_out_p"][QP:])], "nn", F32).reshape(nb, S, D)
    pm_rows = [(x, D, 0, 0), (mix, D, 0, 0)]
    pm_pb = [m[2], m[4], m[3]]
    pm_gl = [V["mix_post_norm"], V["ffn_pre_norm"]]
    x1, h2 = rows_fwd("postmix", fn_postmix, nb, S // tr, tr, pm_rows, pm_pb, pm_gl, [(D, F32), (D, BF16)])
    h22 = h2.reshape(nb * S, D)
    up = matmul("up_proj", [(h22, W["w_up"])], "nn", F32).reshape(nb, S, 2 * D_FF)
    act = glu_fwd(up, ffn_w8, V["ffn_conv_b"], tc)
    act2 = act.reshape(nb * S, D_FF)
    ffn = matmul("down_proj", [(act2, W["w_down"])], "nn", F32).reshape(nb, S, D)
    dx1, dffn, dgate2, d_ffn_post, loss = final_call(x1, ffn, target, m[5], V["ffn_post_norm"], tr)

    dffn2 = dffn.reshape(nb * S, D)
    dact = matmul("down_dgrad", [(dffn2, W["w_down"])], "nt", BF16).reshape(nb, S, D_FF)
    g_down = matmul_tn("down_wgrad", act2, dffn2)
    dgate, dval, ffn_rows = glu_bwd(up, ffn_w8, V["ffn_conv_b"], dact, tc)
    dgate_2, dval_2 = dgate.reshape(nb * S, D_FF), dval.reshape(nb * S, D_FF)
    dh2 = matmul("up_dgrad", [(dgate_2, W["w_up"][:, :D_FF]), (dval_2, W["w_up"][:, D_FF:])], "nt", BF16).reshape(nb, S, D)
    g_up = jnp.concatenate([matmul_tn("up_wgrad_gate", h22, dgate_2), matmul_tn("up_wgrad_val", h22, dval_2)], axis=1)
    dx_a, dmix, dgate1, dscale2, dshift2, d_mix_post, d_ffn_pre = rows_bwd(
        "postmix_bwd", fn_postmix, nb, S // tr, tr, pm_rows, pm_pb, pm_gl,
        [(dx1, D, 0, 0), (dh2, D, 0, 0)], [(0, F32), (1, BF16)])
    dmix2 = dmix.reshape(nb * S, D)
    dcat = matmul("out_dgrad", [(dmix2, W["w_out_p"])], "nt", BF16).reshape(nb, S, QP + D_INNER)
    g_out_p = jnp.concatenate([matmul_tn("out_wgrad_attn", o2, dmix2), matmul_tn("out_wgrad_ssd", ssd2, dmix2)], axis=0)
    dy, dxs_direct, dz, d_dexp, d_ssd_norm = rows_bwd(
        "ssd_finish_bwd", fn_ssd_finish, nb, S // tr, tr, fin_rows, [], fin_gl,
        [(dcat, D_INNER, QP // D_INNER, 0)], [(0, F32), (2, F32), (3, BF16)])
    dxbc2, ddt2, ssd_stats = ssd_bwd(xbc, u, alog, dtb, hin, dy, C)
    dxbc_raw, ssd_rows = ssd_conv_bwd(u, ssd_w8, V["ssd_conv_b"], dxbc2, dxs_direct, C, tc)
    dq, dkv, dkr = attn_bwd(q, kv, kr, dcat, tq)
    dq_pre = rope_call("rope_dq", dq, QP, 0, cosS, -sinS, BF16, tr).reshape(nb * S, QP)
    dkr_pre = rope_call("rope_dk", dkr, LANE, 0, cosT, -sinT, BF16, tr)
    dkv2 = dkv.reshape(nb * T, QP)
    dqn = matmul("q_dgrad", [(dq_pre, W["w_q_p"])], "nt", F32).reshape(nb, S, Q_RANK)
    g_q_p = matmul_tn("q_wgrad", qn2, dq_pre)
    dkvn = matmul("kv_dgrad", [(dkv2, W["w_kv"])], "nt", F32).reshape(nb, T, KV_RANK)
    g_kv = matmul_tn("kv_wgrad", kvn2, dkv2)
    dcq, d_q_norm = rows_bwd("q_norm_bwd", fn_rms, nb, S // tr, tr, [(u, Q_RANK, OFF_CQ // Q_RANK, cblk)], [], [V["q_norm"]],
                             [(dqn, Q_RANK, 0, 0)], [(0, BF16)])
    dckv, d_kv_norm = rows_bwd("kv_norm_bwd", fn_rms, nb, T // tr, tr, [(u, KV_RANK, OFF_CKV // KV_RANK, 0)], [], [V["kv_norm"]],
                               [(dkvn, KV_RANK, 0, 0)], [(0, BF16)])

    def ctx_rows(t):
        return jnp.pad(t, ((0, 0), (C, 0), (0, 0)))

    du = jnp.concatenate([ctx_rows(dcq), dkr_pre, dckv, jnp.zeros((nb, T, OFF_Z - OFF_CKV - KV_RANK), BF16), ctx_rows(dz),
                          dxbc_raw, (ddt2[0] + ddt2[1]).astype(BF16), jnp.zeros((nb, T, WIN_P - OFF_DT - LANE), BF16)],
                         axis=-1).reshape(nb * T, WIN_P)
    dh1 = matmul("in_dgrad", [(du, W["w_in_p"])], "nt", BF16).reshape(nb, T, D)
    g_in_p = matmul_tn("in_wgrad", h1, du)

    def fn_prenorm_res(xv, shift, scale, g):
        return fn_prenorm(xv, shift, scale, g) + (xv,)

    grad_x, dshift1, dscale1, d_mix_pre_x = rows_bwd(
        "prenorm_x_bwd", fn_prenorm_res, nb, S // tr, tr, [(x, D, 0, 0)], [m[0], m[1]], [V["mix_pre_norm"]],
        [(dh1, D, 0, cblk), (dx_a, D, 0, 0)], [(0, F32)])
    dshift_c, dscale_c, d_mix_pre_c = rows_bwd(
        "prenorm_c_bwd", fn_prenorm, nb, C // tr, tr, [(ctx, D, 0, 0)], [], [mc[0], mc[1], V["mix_pre_norm"]],
        [(dh1, D, 0, 0)], [])

    dmod_x = jnp.concatenate([dshift1, dscale1, dgate1, dshift2, dscale2, dgate2], axis=-1).reshape(nb, N_MOD * D)
    dmod_c = jnp.concatenate([dshift_c, dscale_c, jnp.zeros((1, (N_MOD - 2) * D), F32)], axis=-1)
    gm = dict(w_in_p=g_in_p, w_q_p=g_q_p, w_kv=g_kv, w_out_p=g_out_p, w_up=g_up, w_down=g_down)
    gv = dict(
        mix_pre_norm=d_mix_pre_x + d_mix_pre_c, mix_post_norm=d_mix_post, q_norm=d_q_norm, kv_norm=d_kv_norm,
        ssd_conv_w=ssd_rows[:SSD_K], ssd_conv_b=ssd_rows[SSD_K:SSD_K + 1],
        ssd_a_log=ssd_stats[0:1, :2 * SSD_HEADS], ssd_dt_bias=ssd_stats[1:2, :2 * SSD_HEADS],
        ssd_d=jnp.sum(d_dexp.reshape(SSD_HEADS, SSD_P), axis=1).reshape(1, SSD_HEADS), ssd_norm=d_ssd_norm,
        ffn_pre_norm=d_ffn_pre, ffn_post_norm=d_ffn_post,
        ffn_conv_w=ffn_rows[:FFN_K], ffn_conv_b=ffn_rows[FFN_K:FFN_K + 1])
    return loss, grad_x, dmod_x, dmod_c, gm, gv


WEIGHT_ORDER = ("c_ctx", "w_mod", "b_mod", "mix_pre_norm", "mix_post_norm", "w_in", "q_norm", "w_q_up", "kv_norm",
                "w_kv_up", "ssd_conv_w", "ssd_conv_b", "ssd_a_log", "ssd_dt_bias", "ssd_d", "ssd_norm", "w_out",
                "ffn_pre_norm", "ffn_post_norm", "w_up", "ffn_conv_w", "ffn_conv_b", "w_down")
MATRICES = ("w_in", "w_q_up", "w_kv_up", "w_out", "w_up", "w_down")
ROW_SHARDED = ("w_out", "w_down")
SMALL_SUMMED = ("c_ctx", "mix_pre_norm", "mix_post_norm", "q_norm", "kv_norm", "ssd_conv_w", "ssd_conv_b", "ssd_a_log",
                "ssd_dt_bias", "ssd_d", "ssd_norm", "ffn_pre_norm", "ffn_post_norm", "ffn_conv_w", "ffn_conv_b")
MOD_ROWS = 8


def _whole(shards, name):
    if name in ROW_SHARDED:
        return shards.reshape(-1, shards.shape[-1])
    return jnp.concatenate([shards[j] for j in range(N_DEV)], axis=1)


def _per_device(g, name):
    if name in ROW_SHARDED:
        return g.reshape(N_DEV, -1, g.shape[-1])
    return jnp.stack(jnp.split(g, N_DEV, axis=1))


def kernel(x, c, ctx, c_ctx, w_mod, b_mod, mix_pre_norm, mix_post_norm, w_in, q_norm, w_q_up, kv_norm, w_kv_up, ssd_conv_w, ssd_conv_b, ssd_a_log, ssd_dt_bias, ssd_d, ssd_norm, w_out, ffn_pre_norm, ffn_post_norm, w_up, ffn_conv_w, ffn_conv_b, w_down, loss_target, m_c_ctx, m_w_mod, m_b_mod, m_mix_pre_norm, m_mix_post_norm, m_w_in, m_q_norm, m_w_q_up, m_kv_norm, m_w_kv_up, m_ssd_conv_w, m_ssd_conv_b, m_ssd_a_log, m_ssd_dt_bias, m_ssd_d, m_ssd_norm, m_w_out, m_ffn_pre_norm, m_ffn_post_norm, m_w_up, m_ffn_conv_w, m_ffn_conv_b, m_w_down, v_c_ctx, v_w_mod, v_b_mod, v_mix_pre_norm, v_mix_post_norm, v_w_in, v_q_norm, v_w_q_up, v_kv_norm, v_w_kv_up, v_ssd_conv_w, v_ssd_conv_b, v_ssd_a_log, v_ssd_dt_bias, v_ssd_d, v_ssd_norm, v_w_out, v_ffn_pre_norm, v_ffn_post_norm, v_w_up, v_ffn_conv_w, v_ffn_conv_b, v_w_down):
    weights = dict(c_ctx=c_ctx, w_mod=w_mod, b_mod=b_mod, mix_pre_norm=mix_pre_norm, mix_post_norm=mix_post_norm, w_in=w_in, q_norm=q_norm, w_q_up=w_q_up, kv_norm=kv_norm, w_kv_up=w_kv_up, ssd_conv_w=ssd_conv_w, ssd_conv_b=ssd_conv_b, ssd_a_log=ssd_a_log, ssd_dt_bias=ssd_dt_bias, ssd_d=ssd_d, ssd_norm=ssd_norm, w_out=w_out, ffn_pre_norm=ffn_pre_norm, ffn_post_norm=ffn_post_norm, w_up=w_up, ffn_conv_w=ffn_conv_w, ffn_conv_b=ffn_conv_b, w_down=w_down)
    mom1 = dict(c_ctx=m_c_ctx, w_mod=m_w_mod, b_mod=m_b_mod, mix_pre_norm=m_mix_pre_norm, mix_post_norm=m_mix_post_norm, w_in=m_w_in, q_norm=m_q_norm, w_q_up=m_w_q_up, kv_norm=m_kv_norm, w_kv_up=m_w_kv_up, ssd_conv_w=m_ssd_conv_w, ssd_conv_b=m_ssd_conv_b, ssd_a_log=m_ssd_a_log, ssd_dt_bias=m_ssd_dt_bias, ssd_d=m_ssd_d, ssd_norm=m_ssd_norm, w_out=m_w_out, ffn_pre_norm=m_ffn_pre_norm, ffn_post_norm=m_ffn_post_norm, w_up=m_w_up, ffn_conv_w=m_ffn_conv_w, ffn_conv_b=m_ffn_conv_b, w_down=m_w_down)
    mom2 = dict(c_ctx=v_c_ctx, w_mod=v_w_mod, b_mod=v_b_mod, mix_pre_norm=v_mix_pre_norm, mix_post_norm=v_mix_post_norm, w_in=v_w_in, q_norm=v_q_norm, w_q_up=v_w_q_up, kv_norm=v_kv_norm, w_kv_up=v_w_kv_up, ssd_conv_w=v_ssd_conv_w, ssd_conv_b=v_ssd_conv_b, ssd_a_log=v_ssd_a_log, ssd_dt_bias=v_ssd_dt_bias, ssd_d=v_ssd_d, ssd_norm=v_ssd_norm, w_out=v_w_out, ffn_pre_norm=v_ffn_pre_norm, ffn_post_norm=v_ffn_post_norm, w_up=v_w_up, ffn_conv_w=v_ffn_conv_w, ffn_conv_b=v_ffn_conv_b, w_down=v_w_down)
    nb, S, D = x.shape
    me = 4 * lax.axis_index("x") + 2 * lax.axis_index("y") + lax.axis_index("c")

    gathered = all_gather("gather_weights", [weights[n][0].astype(BF16) for n in MATRICES])
    W = weights_to_internal(*[_whole(s, n) for n, s in zip(MATRICES, gathered)])
    c_all, ssd_w_sh, ffn_w_sh = all_gather("gather_small", [c, ssd_conv_w[0], ffn_conv_w[0]])
    V = {n: weights[n].reshape(1, -1) for n in SMALL_SUMMED if n != "c_ctx"}
    V["ssd_conv_w"] = _whole(ssd_w_sh, "ssd_conv_w")
    V["ffn_conv_w"] = _whole(ffn_w_sh, "ffn_conv_w")

    n_all = N_DEV * nb
    mod_rows = -(-(n_all + 1) // 8) * 8
    c_pad = jnp.concatenate([c_all.reshape(n_all, D), c_ctx.reshape(1, D), jnp.zeros((mod_rows - n_all - 1, D), F32)], axis=0)
    mod_cols = w_mod.shape[2]
    b_mine = lax.dynamic_slice(b_mod, (0, me * mod_cols), (1, mod_cols))
    mod_part = matmul("mod_proj", [(c_pad, w_mod[0])], "nn", F32, bias=b_mine, silu_a=True)
    mod_all = _whole(all_gather("gather_mod", [mod_part])[0], "w_mod")
    mod_x = lax.dynamic_slice(mod_all, (me * nb, 0), (nb, mod_all.shape[1]))
    mod_c = mod_all[n_all:n_all + 1]

    loss, grad_x, dmod_x, dmod_c, gm, gv = local_step(x, ctx, loss_target, mod_x, mod_c, W, V)

    dmod_mine = jnp.concatenate([dmod_x, dmod_c, jnp.zeros((MOD_ROWS - nb - 1, dmod_x.shape[1]), F32)], axis=0)
    dmod_all = all_gather("gather_dmod", [dmod_mine])[0]
    dmod_ctx = sum_slots("sum_dmod_ctx", dmod_all[:, nb:nb + 1].reshape(N_DEV, -1, LANE)).reshape(1, -1)
    dmod_full = jnp.concatenate([dmod_all[:, :nb].reshape(n_all, -1), dmod_ctx,
                                 jnp.zeros((mod_rows - n_all - 1, dmod_ctx.shape[1]), F32)], axis=0)
    (g_b_mod,) = ew_call("mod_bias_grad", lambda t: (jnp.sum(t, axis=0, keepdims=True),), [dmod_full], [((1, dmod_full.shape[1]), F32)])
    dmod_cols = lax.dynamic_slice(dmod_full, (0, me * mod_cols), (mod_rows, mod_cols))
    g_w_mod = matmul_tn("mod_wgrad", c_pad, dmod_cols, silu_a=True)
    dsilu_ctx = matmul("mod_dgrad_ctx", [(dmod_cols[n_all:n_all + 8], w_mod[0])], "nt", F32)[0:1]

    def silu_vjp(cc, ct):
        return (jax.vjp(_silu, cc)[1](ct)[0],)

    (g_c_ctx_part,) = ew_call("c_ctx_grad", silu_vjp, [c_ctx.reshape(1, D), dsilu_ctx], [((1, D), F32)])

    gv = dict(gv, c_ctx=g_c_ctx_part)
    small_parts = [loss] + [gv[n] for n in SMALL_SUMMED]
    small_sum = sum_slots("sum_small", all_gather("gather_small_grads", [_pack_rows(small_parts)])[0])
    summed = _unpack_rows(small_sum, [p.shape for p in small_parts])
    loss_out = summed[0][0, 0]
    grads = {n: g.reshape(weights[n].shape) if n not in ("ssd_conv_w", "ffn_conv_w") else g for n, g in zip(SMALL_SUMMED, summed[1:])}
    for n in ("ssd_conv_w", "ffn_conv_w"):
        cols = weights[n].shape[2]
        grads[n] = lax.dynamic_slice(grads[n], (0, me * cols), (grads[n].shape[0], cols)).reshape(weights[n].shape)
    grads["b_mod"] = g_b_mod.reshape(b_mod.shape)

    g_whole = grads_from_internal(*[gm[k] for k in ("w_in_p", "w_q_p", "w_kv", "w_out_p", "w_up", "w_down")])
    received = all_to_all("exchange_matrix_grads", [_per_device(g, n) for g, n in zip(g_whole, MATRICES)])
    slots = dict(zip(MATRICES, received), w_mod=g_w_mod[None])
    delta, new_m, new_v = {}, {}, {}
    for n in MATRICES + ("w_mod",):
        g, d, mn, vn = adamw_matrix("adamw_" + n, weights[n][0], slots[n], mom1[n][0], mom2[n][0])
        grads[n], delta[n], new_m[n], new_v[n] = [t.reshape(weights[n].shape) for t in (g, d, mn, vn)]
    small = [n for n in WEIGHT_ORDER if n not in slots]

    def two_d(t):
        return t.reshape(-1, t.shape[-1])

    ds, ms, vs = adamw_small(*[[two_d(t[n]) for n in small] for t in (weights, grads, mom1, mom2)])
    for n, d, mn, vn in zip(small, ds, ms, vs):
        delta[n], new_m[n], new_v[n] = [t.reshape(weights[n].shape) for t in (d, mn, vn)]
    return (loss_out, grad_x, *[t[n] for t in (grads, delta, new_m, new_v) for n in WEIGHT_ORDER])
```

```python
import functools
import math

import jax
import jax.numpy as jnp
import numpy as np
from jax import lax
from jax.experimental import pallas as pl
from jax.experimental.pallas import tpu as pltpu

F32 = jnp.float32
BF16 = jnp.bfloat16
MESH = pl.DeviceIdType.MESH

D_MODEL = 1024
GRID_W = 64
N_HEADS = 16
NOPE = 64
ROPE = 32
V_DIM = 64
Q_RANK = 384
KV_RANK = 256
ROPE_THETA = 10000.0
ATTN_SCALE = (NOPE + ROPE) ** -0.5
SSD_HEADS = 16
SSD_P = 64
SSD_GROUPS = 2
SSD_N = 128
SSD_K = 5
CHUNK = 128
D_INNER = SSD_HEADS * SSD_P
GN = SSD_GROUPS * SSD_N
XBC = D_INNER + 2 * GN
D_FF = 2816
FFN_K = 3
N_MOD = 6
EPS = 1e-6
IN_SPLITS = (Q_RANK, KV_RANK, ROPE, D_INNER, XBC, 2 * SSD_HEADS)
IN_WIDTH = sum(IN_SPLITS)
N_DEV = 8

ADAM_LR = 0.001
ADAM_B1 = 0.9
ADAM_B2 = 0.999
ADAM_EPS = 1e-08
ADAM_WD = 0.01
ADAM_STEP = 10

LANE = 128
HEAD_BLOCK = 128
OFF_CQ = 0
OFF_KR = 384
OFF_CKV = 512
OFF_Z = 1024
OFF_XBC = 2048
OFF_DT = 3584
WIN_P = 3840
KR_LANE = 64
QP = N_HEADS * HEAD_BLOCK

VMEM_LIMIT_V7X = 56 * 1024 * 1024
NEG_BIG = -1e30


def _cparams(*sem):
    return pltpu.CompilerParams(dimension_semantics=sem, vmem_limit_bytes=VMEM_LIMIT_V7X)


def _tile(n, target, mult=128):
    if n <= target:
        return n
    t = (target // mult) * mult
    while t >= mult:
        if n % t == 0:
            return t
        t -= mult
    return n


def _silu(x):
    return x * jax.nn.sigmoid(x)


def _rms(x, g):
    return x * lax.rsqrt(jnp.mean(x * x, axis=-1, keepdims=True) + EPS) * g


WHOLE_K_WIDE = 2048


def matmul(name, pairs, mode, out_dtype, *, bias=None, silu_a=False):
    n_pairs = len(pairs)
    M = pairs[0][0].shape[0]
    N = pairs[0][1].shape[1] if mode == "nn" else pairs[0][1].shape[0]
    k_total = sum(a.shape[1] for a, _ in pairs)
    tm = _tile(M, 1024 if k_total <= WHOLE_K_WIDE else 512, 8)
    tn = _tile(N, 1408 if k_total <= WHOLE_K_WIDE else 512)
    dims = (((1,), (0,)), ((), ())) if mode == "nn" else (((1,), (1,)), ((), ()))

    def body(*refs):
        o_ref = refs[-1]
        acc = None
        for p in range(n_pairs):
            a = refs[2 * p][...]
            if silu_a:
                a = _silu(a.astype(F32))
            d = lax.dot_general(a.astype(BF16), refs[2 * p + 1][...].astype(BF16), dims, preferred_element_type=F32)
            acc = d if acc is None else acc + d
        if bias is not None:
            acc = acc + refs[2 * n_pairs][...]
        o_ref[...] = acc.astype(o_ref.dtype)

    in_specs, args = [], []
    for a, b in pairs:
        K = a.shape[1]
        in_specs.append(pl.BlockSpec((tm, K), lambda j, i: (i, 0)))
        in_specs.append(pl.BlockSpec((K, tn), lambda j, i: (0, j)) if mode == "nn" else pl.BlockSpec((tn, K), lambda j, i: (j, 0)))
        args += [a, b]
    if bias is not None:
        in_specs.append(pl.BlockSpec((1, tn), lambda j, i: (0, j)))
        args.append(bias)
    return pl.pallas_call(
        body, name=name, grid=(N // tn, M // tm), in_specs=in_specs,
        out_specs=pl.BlockSpec((tm, tn), lambda j, i: (i, j)),
        out_shape=jax.ShapeDtypeStruct((M, N), out_dtype),
        compiler_params=_cparams("arbitrary", "arbitrary"),
    )(*args)


def matmul_tn(name, a, b, out_dtype=F32, *, silu_a=False, tm=1024, tn=512, tk=2048):
    R, M = a.shape
    N = b.shape[1]
    tm = _tile(M, tm)
    tn = _tile(N, tn)
    tk = _tile(R, tk, 8)
    nk = R // tk

    def body(a_ref, b_ref, o_ref, acc):
        k = pl.program_id(2)

        @pl.when(k == 0)
        def _():
            acc[...] = jnp.zeros_like(acc)

        x = a_ref[...]
        if silu_a:
            x = _silu(x.astype(F32))
        acc[...] += lax.dot_general(x.astype(BF16), b_ref[...].astype(BF16), (((0,), (0,)), ((), ())),
                                    preferred_element_type=F32)

        @pl.when(k == nk - 1)
        def _():
            o_ref[...] = acc[...].astype(o_ref.dtype)

    return pl.pallas_call(
        body, name=name, grid=(M // tm, N // tn, nk),
        in_specs=[pl.BlockSpec((tk, tm), lambda i, j, k: (k, i)), pl.BlockSpec((tk, tn), lambda i, j, k: (k, j))],
        out_specs=pl.BlockSpec((tm, tn), lambda i, j, k: (i, j)),
        out_shape=jax.ShapeDtypeStruct((M, N), out_dtype),
        scratch_shapes=[pltpu.VMEM((tm, tn), F32)],
        compiler_params=_cparams("arbitrary", "arbitrary", "arbitrary"),
    )(a, b)


def _row_specs(rin, pbin, glin, tr):
    specs = [pl.BlockSpec((1, tr, w), lambda b, i, cb=cb, ro=ro: (b, i + ro, cb)) for (_, w, cb, ro) in rin]
    specs += [pl.BlockSpec((1, 1, a.shape[-1]), lambda b, i: (b, 0, 0)) for a in pbin]
    specs += [pl.BlockSpec((1, a.shape[-1]), lambda b, i: (0, 0)) for a in glin]
    return specs


def rows_fwd(name, fn, nb, nblk, tr, rin, pbin, glin, outs):
    nr, npb, ngl = len(rin), len(pbin), len(glin)
    n_in = nr + npb + ngl

    def body(*refs):
        args = [r[0].astype(F32) for r in refs[:nr + npb]] + [r[...] for r in refs[nr + npb:n_in]]
        res = fn(*args)
        for o, v in zip(refs[n_in:], res):
            o[0] = v.astype(o.dtype)

    return pl.pallas_call(
        body, name=name, grid=(nb, nblk), in_specs=_row_specs(rin, pbin, glin, tr),
        out_specs=[pl.BlockSpec((1, tr, w), lambda b, i: (b, i, 0)) for (w, _) in outs],
        out_shape=[jax.ShapeDtypeStruct((nb, nblk * tr, w), dt) for (w, dt) in outs],
        compiler_params=_cparams("arbitrary", "arbitrary"),
    )(*[a for (a, _, _, _) in rin], *pbin, *glin)


def rows_bwd(name, fn, nb, nblk, tr, rin, pbin, glin, cts, want):
    nr, npb, ngl, nct = len(rin), len(pbin), len(glin), len(cts)
    n_in = nr + npb + ngl

    def body(*refs):
        b, i = pl.program_id(0), pl.program_id(1)
        args = [r[0].astype(F32) for r in refs[:nr + npb]] + [r[...] for r in refs[nr + npb:n_in]]
        ct = tuple(r[0].astype(F32) for r in refs[n_in:n_in + nct])
        _, vjp = jax.vjp(fn, *args)
        g = vjp(ct)
        orefs = refs[n_in + nct:]
        for o, (idx, _) in zip(orefs, want):
            o[0] = g[idx].astype(o.dtype)
        pb_refs = orefs[len(want):len(want) + npb]
        gl_refs = orefs[len(want) + npb:]

        @pl.when(i == 0)
        def _():
            for o, v in zip(pb_refs, g[nr:nr + npb]):
                o[0] = v

        @pl.when(i > 0)
        def _():
            for o, v in zip(pb_refs, g[nr:nr + npb]):
                o[0] += v

        first = jnp.logical_and(b == 0, i == 0)

        @pl.when(first)
        def _():
            for o, v in zip(gl_refs, g[nr + npb:]):
                o[...] = v

        @pl.when(jnp.logical_not(first))
        def _():
            for o, v in zip(gl_refs, g[nr + npb:]):
                o[...] += v

    out_specs = [pl.BlockSpec((1, tr, rin[idx][1]), lambda b, i: (b, i, 0)) for (idx, _) in want]
    out_shape = [jax.ShapeDtypeStruct((nb, nblk * tr, rin[idx][1]), dt) for (idx, dt) in want]
    out_specs += [pl.BlockSpec((1, 1, a.shape[-1]), lambda b, i: (b, 0, 0)) for a in pbin]
    out_shape += [jax.ShapeDtypeStruct((nb, 1, a.shape[-1]), F32) for a in pbin]
    out_specs += [pl.BlockSpec((1, a.shape[-1]), lambda b, i: (0, 0)) for a in glin]
    out_shape += [jax.ShapeDtypeStruct((1, a.shape[-1]), F32) for a in glin]
    return pl.pallas_call(
        body, name=name, grid=(nb, nblk),
        in_specs=_row_specs(rin, pbin, glin, tr) + _row_specs(cts, [], [], tr),
        out_specs=out_specs, out_shape=out_shape,
        compiler_params=_cparams("arbitrary", "arbitrary"),
    )(*[a for (a, _, _, _) in rin], *pbin, *glin, *[a for (a, _, _, _) in cts])


def ew_call(name, fn, ins, outs):
    def body(*refs):
        res = fn(*[r[...] for r in refs[:len(ins)]])
        for o, v in zip(refs[len(ins):], res):
            o[...] = v.astype(o.dtype)

    return pl.pallas_call(body, name=name, out_shape=[jax.ShapeDtypeStruct(s, dt) for (s, dt) in outs])(*ins)


def fn_prenorm(x, shift, scale, g):
    return (_rms(x, g) * (1.0 + scale) + shift,)


def fn_rms(x, g):
    return (_rms(x, g),)


def fn_ssd_finish(yf, yr, xs, z, dexp, nw):
    y = yf + yr + dexp * xs
    return (_rms(y * _silu(z), nw),)


def fn_postmix(x, mix, gate1, scale2, shift2, post_g, pre_g):
    x1 = x + gate1 * _rms(mix, post_g)
    h2 = _rms(x1, pre_g) * (1.0 + scale2) + shift2
    return x1, h2


def final_call(x1, ffn, target, gate2, post_g, tr):
    nb, S, D = x1.shape
    nblk = S // tr

    def body(x1_ref, f_ref, t_ref, g2_ref, pg_ref, dx1_ref, df_ref, dg2_ref, dpg_ref, loss_ref):
        b, i = pl.program_id(0), pl.program_id(1)
        tgt = t_ref[0]

        def lossfn(x1v, fv, g2, pg):
            e = x1v + g2 * _rms(fv, pg) - tgt
            return 0.5 * jnp.sum(jnp.mean(e * e, axis=-1, keepdims=True))

        val, (dx1, df, dg2, dpg) = jax.value_and_grad(lossfn, argnums=(0, 1, 2, 3))(
            x1_ref[0], f_ref[0].astype(F32), g2_ref[0], pg_ref[...])
        dx1_ref[0] = dx1
        df_ref[0] = df.astype(df_ref.dtype)
        lv = jnp.full((1, LANE), val, F32)

        @pl.when(i == 0)
        def _():
            dg2_ref[0] = dg2

        @pl.when(i > 0)
        def _():
            dg2_ref[0] += dg2

        first = jnp.logical_and(b == 0, i == 0)

        @pl.when(first)
        def _():
            dpg_ref[...] = dpg
            loss_ref[...] = lv

        @pl.when(jnp.logical_not(first))
        def _():
            dpg_ref[...] += dpg
            loss_ref[...] += lv

    row = pl.BlockSpec((1, tr, D), lambda b, i: (b, i, 0))
    pb = pl.BlockSpec((1, 1, D), lambda b, i: (b, 0, 0))
    gl = pl.BlockSpec((1, D), lambda b, i: (0, 0))
    return pl.pallas_call(
        body, name="loss_head", grid=(nb, nblk), in_specs=[row, row, row, pb, gl],
        out_specs=[row, row, pb, gl, pl.BlockSpec((1, LANE), lambda b, i: (0, 0))],
        out_shape=[jax.ShapeDtypeStruct((nb, S, D), F32), jax.ShapeDtypeStruct((nb, S, D), BF16),
                   jax.ShapeDtypeStruct((nb, 1, D), F32), jax.ShapeDtypeStruct((1, D), F32),
                   jax.ShapeDtypeStruct((1, LANE), F32)],
        compiler_params=_cparams("arbitrary", "arbitrary"),
    )(x1, ffn, target, gate2, post_g)


def _rotate_half(t):
    lane = lax.broadcasted_iota(jnp.int32, t.shape, 1)
    return jnp.where((lane & 15) < 8, -pltpu.roll(t, LANE - 8, 1), pltpu.roll(t, 8, 1))


def rope_call(name, x, width, colblk, cos, sin, out_dtype, tr):
    nb = x.shape[0]
    R = cos.shape[0]
    nblk = R // tr

    def body(x_ref, c_ref, s_ref, o_ref):
        c, s = c_ref[...], s_ref[...]
        for h in range(width // LANE):
            t = x_ref[0, :, h * LANE:(h + 1) * LANE].astype(F32)
            o_ref[0, :, h * LANE:(h + 1) * LANE] = (t * c + _rotate_half(t) * s).astype(o_ref.dtype)

    tab = pl.BlockSpec((tr, LANE), lambda b, i: (i, 0))
    return pl.pallas_call(
        body, name=name, grid=(nb, nblk),
        in_specs=[pl.BlockSpec((1, tr, width), lambda b, i: (b, i, colblk)), tab, tab],
        out_specs=pl.BlockSpec((1, tr, width), lambda b, i: (b, i, 0)),
        out_shape=jax.ShapeDtypeStruct((nb, R, width), out_dtype),
        compiler_params=_cparams("arbitrary", "arbitrary"),
    )(x, cos, sin)


def rope_tables(n_ctx, seq):
    n_rows = seq // GRID_W
    row = np.repeat(np.arange(n_rows), GRID_W).astype(np.float32)
    col = np.tile(np.arange(GRID_W), n_rows).astype(np.float32)
    axis_dim = ROPE // 2
    inv_freq = jnp.asarray(ROPE_THETA, F32) ** (-jnp.arange(0, axis_dim, 2, dtype=F32) / axis_dim)
    ang_r = jnp.asarray(row)[:, None] * inv_freq
    ang_c = jnp.asarray(col)[:, None] * inv_freq
    ang = jnp.concatenate([ang_r, ang_r, ang_c, ang_c], axis=-1)
    cos = jnp.ones((n_ctx + seq, LANE), F32).at[n_ctx:, KR_LANE:KR_LANE + ROPE].set(jnp.cos(ang))
    sin = jnp.zeros((n_ctx + seq, LANE), F32).at[n_ctx:, KR_LANE:KR_LANE + ROPE].set(jnp.sin(ang))
    return cos, sin


Q_PRESCALE = ATTN_SCALE * math.log2(math.e)


def _attn_weights(q, kc):
    s2 = lax.dot_general(q, kc, (((1,), (1,)), ((), ())), preferred_element_type=F32)
    e = jnp.exp2(s2 - jnp.max(s2, axis=1, keepdims=True))
    return e, 1.0 / jnp.sum(e, axis=1, keepdims=True)


def _key_block(kv, kr):
    lane = lax.broadcasted_iota(jnp.int32, kv.shape, 1)
    return jnp.where(lane < NOPE, kv, kr)


def attn_fwd(q, kv, kr, tq):
    nb, S, _ = q.shape
    T = kv.shape[1]

    def body(q_ref, kv_ref, kr_ref, o_ref):
        kvv = kv_ref[0]
        e, r = _attn_weights(q_ref[0], _key_block(kvv, kr_ref[0]))
        o = lax.dot_general(e.astype(BF16), kvv, (((1,), (0,)), ((), ())), preferred_element_type=F32) * r
        lane = lax.broadcasted_iota(jnp.int32, o.shape, 1)
        o_ref[0] = jnp.where(lane >= NOPE, o, 0.0).astype(o_ref.dtype)

    return pl.pallas_call(
        body, name="attn_fwd", grid=(nb, N_HEADS, S // tq),
        in_specs=[pl.BlockSpec((1, tq, HEAD_BLOCK), lambda b, h, i: (b, i, h)),
                  pl.BlockSpec((1, T, HEAD_BLOCK), lambda b, h, i: (b, 0, h)),
                  pl.BlockSpec((1, T, HEAD_BLOCK), lambda b, h, i: (b, 0, 0))],
        out_specs=pl.BlockSpec((1, tq, HEAD_BLOCK), lambda b, h, i: (b, i, h)),
        out_shape=jax.ShapeDtypeStruct((nb, S, QP), BF16),
        compiler_params=_cparams("arbitrary", "arbitrary", "arbitrary"),
    )(q, kv, kr)


def attn_bwd(q, kv, kr, do, tq):
    nb, S, _ = q.shape
    T = kv.shape[1]

    def body(q_ref, kv_ref, kr_ref, do_ref, dq_ref, dkv_ref, dkr_ref):
        h, i = pl.program_id(1), pl.program_id(2)
        qv, kvv, dov = q_ref[0], kv_ref[0], do_ref[0]
        kc = _key_block(kvv, kr_ref[0])
        e, r = _attn_weights(qv, kc)
        dor = (dov.astype(F32) * r).astype(BF16)
        dpr = lax.dot_general(dor, kvv, (((1,), (1,)), ((), ())), preferred_element_type=F32)
        ds = (e * (dpr - r * jnp.sum(dpr * e, axis=1, keepdims=True))).astype(BF16)
        dq = lax.dot_general(ds, kc, (((1,), (0,)), ((), ())), preferred_element_type=F32)
        dq_ref[0] = (dq * ATTN_SCALE).astype(dq_ref.dtype)
        dkc = lax.dot_general(ds, qv, (((0,), (0,)), ((), ())), preferred_element_type=F32) * math.log(2.0)
        dv = lax.dot_general(e.astype(BF16), dor, (((0,), (0,)), ((), ())), preferred_element_type=F32)
        lane = lax.broadcasted_iota(jnp.int32, dkc.shape, 1)
        dkv = jnp.where(lane < NOPE, dkc, dv)
        dkr = jnp.where(lane >= NOPE, dkc, 0.0)

        @pl.when(i == 0)
        def _():
            dkv_ref[0] = dkv

        @pl.when(i > 0)
        def _():
            dkv_ref[0] += dkv

        first = jnp.logical_and(h == 0, i == 0)

        @pl.when(first)
        def _():
            dkr_ref[0] = dkr

        @pl.when(jnp.logical_not(first))
        def _():
            dkr_ref[0] += dkr

    qspec = pl.BlockSpec((1, tq, HEAD_BLOCK), lambda b, h, i: (b, i, h))
    kspec = pl.BlockSpec((1, T, HEAD_BLOCK), lambda b, h, i: (b, 0, h))
    rspec = pl.BlockSpec((1, T, HEAD_BLOCK), lambda b, h, i: (b, 0, 0))
    return pl.pallas_call(
        body, name="attn_bwd", grid=(nb, N_HEADS, S // tq),
        in_specs=[qspec, kspec, rspec, qspec], out_specs=[qspec, kspec, rspec],
        out_shape=[jax.ShapeDtypeStruct((nb, S, QP), F32), jax.ShapeDtypeStruct((nb, T, QP), F32),
                   jax.ShapeDtypeStruct((nb, T, HEAD_BLOCK), F32)],
        compiler_params=_cparams("arbitrary", "arbitrary", "arbitrary"),
    )(q, kv, kr, do)


def _seg_bounds(n, n_ctx):
    t = lax.broadcasted_iota(jnp.int32, (n, 1), 0)
    if n_ctx == 0:
        return t, jnp.zeros_like(t), jnp.full_like(t, n)
    in_ctx = t < n_ctx
    return t, jnp.where(in_ctx, 0, n_ctx), jnp.where(in_ctx, n_ctx, n)


def _shift_rows(x, o, bounds):
    if o == 0:
        return x
    t, lo, hi = bounds
    n = x.shape[0]
    valid = jnp.logical_and(t + o >= lo, t + o < hi).astype(F32)
    return pltpu.roll(x, (-o) % n, 0) * valid


def _conv(x, w, bias, k, bounds):
    acc = bias
    for o in range(k):
        acc = acc + w[o:o + 1, :] * _shift_rows(x, o - k // 2, bounds)
    return acc


def _conv_bwd(x, w, dpre, k, bounds):
    dx = jnp.zeros_like(x)
    rows = []
    for o in range(k):
        dx = dx + w[o:o + 1, :] * _shift_rows(dpre, -(o - k // 2), bounds)
        rows.append(jnp.sum(dpre * _shift_rows(x, o - k // 2, bounds), axis=0, keepdims=True))
    rows.append(jnp.sum(dpre, axis=0, keepdims=True))
    sub8 = lax.broadcasted_iota(jnp.int32, (8, x.shape[1]), 0)
    out = jnp.zeros((8, x.shape[1]), F32)
    for o, r in enumerate(rows):
        out = out + jnp.where(sub8 == o, r, 0.0)
    return dx, out


def _gelu(x):
    return 0.5 * x * (1.0 + lax.erf(x * (1.0 / math.sqrt(2.0))))


def _gelu_grad(x):
    return 0.5 * (1.0 + lax.erf(x * (1.0 / math.sqrt(2.0)))) + x * jnp.exp(-0.5 * x * x) * (1.0 / math.sqrt(2.0 * math.pi))


def ssd_conv_fwd(u, w8, bias, n_ctx, tc):
    nb, T, _ = u.shape
    cb0 = OFF_XBC // tc

    def body(x_ref, w_ref, b_ref, o_ref):
        pre = _conv(x_ref[0], w_ref[...], b_ref[...], SSD_K, _seg_bounds(T, n_ctx))
        o_ref[0] = _silu(pre)

    return pl.pallas_call(
        body, name="ssd_conv_fwd", grid=(nb, XBC // tc),
        in_specs=[pl.BlockSpec((1, T, tc), lambda b, j: (b, 0, cb0 + j)),
                  pl.BlockSpec((8, tc), lambda b, j: (0, j)), pl.BlockSpec((1, tc), lambda b, j: (0, j))],
        out_specs=pl.BlockSpec((1, T, tc), lambda b, j: (b, 0, j)),
        out_shape=jax.ShapeDtypeStruct((nb, T, XBC), F32),
        compiler_params=_cparams("arbitrary", "arbitrary"),
    )(u, w8, bias)


def ssd_conv_bwd(u, w8, bias, dxbc, dxs_direct, n_ctx, tc):
    nb, T, _ = u.shape
    cb0 = OFF_XBC // tc
    n_direct = D_INNER // tc

    def body(x_ref, w_ref, b_ref, d0_ref, d1_ref, dd_ref, dx_ref, dw_ref, acc):
        j, b = pl.program_id(0), pl.program_id(1)
        acc[...] = d0_ref[0, 0] + d1_ref[0, 0]

        @pl.when(j < n_direct)
        def _():
            acc[n_ctx:, :] += dd_ref[0]

        bounds = _seg_bounds(T, n_ctx)
        x, w = x_ref[0], w_ref[...]
        pre = _conv(x, w, b_ref[...], SSD_K, bounds)
        sg = jax.nn.sigmoid(pre)
        dpre = acc[...] * (sg * (1.0 + pre * (1.0 - sg)))
        dx, rows = _conv_bwd(x, w, dpre, SSD_K, bounds)
        dx_ref[0] = dx.astype(dx_ref.dtype)

        @pl.when(b == 0)
        def _():
            dw_ref[...] = rows

        @pl.when(b > 0)
        def _():
            dw_ref[...] += rows

    dspec0 = pl.BlockSpec((1, 1, T, tc), lambda j, b: (0, b, 0, j))
    dspec1 = pl.BlockSpec((1, 1, T, tc), lambda j, b: (1, b, 0, j))
    return pl.pallas_call(
        body, name="ssd_conv_bwd", grid=(XBC // tc, nb),
        in_specs=[pl.BlockSpec((1, T, tc), lambda j, b: (b, 0, cb0 + j)),
                  pl.BlockSpec((8, tc), lambda j, b: (0, j)), pl.BlockSpec((1, tc), lambda j, b: (0, j)),
                  dspec0, dspec1,
                  pl.BlockSpec((1, T - n_ctx, tc), lambda j, b: (b, 0, jnp.minimum(j, n_direct - 1)))],
        out_specs=[pl.BlockSpec((1, T, tc), lambda j, b: (b, 0, j)), pl.BlockSpec((8, tc), lambda j, b: (0, j))],
        out_shape=[jax.ShapeDtypeStruct((nb, T, XBC), BF16), jax.ShapeDtypeStruct((8, XBC), F32)],
        scratch_shapes=[pltpu.VMEM((T, tc), F32)],
        compiler_params=_cparams("arbitrary", "arbitrary"),
    )(u, w8, bias, dxbc, dxbc, dxs_direct)


GLU_TC = 256


def glu_interleave(w_up):
    k = w_up.shape[0]
    return jnp.stack([w_up[:, :D_FF].reshape(k, -1, GLU_TC), w_up[:, D_FF:].reshape(k, -1, GLU_TC)], axis=2).reshape(k, 2 * D_FF)


def glu_deinterleave(g):
    k = g.shape[0]
    g4 = g.reshape(k, -1, 2, GLU_TC)
    return jnp.concatenate([g4[:, :, 0].reshape(k, D_FF), g4[:, :, 1].reshape(k, D_FF)], axis=1)


def glu_fwd(up, w8, bias):
    nb, S, _ = up.shape
    tc = GLU_TC

    def body(u_ref, w_ref, b_ref, o_ref):
        gc = _conv(u_ref[0, :, :tc], w_ref[...], b_ref[...], FFN_K, _seg_bounds(S, 0))
        o_ref[0] = (_gelu(gc) * u_ref[0, :, tc:]).astype(o_ref.dtype)

    return pl.pallas_call(
        body, name="glu_fwd", grid=(nb, D_FF // tc),
        in_specs=[pl.BlockSpec((1, S, 2 * tc), lambda b, j: (b, 0, j)),
                  pl.BlockSpec((8, tc), lambda b, j: (0, j)), pl.BlockSpec((1, tc), lambda b, j: (0, j))],
        out_specs=pl.BlockSpec((1, S, tc), lambda b, j: (b, 0, j)),
        out_shape=jax.ShapeDtypeStruct((nb, S, D_FF), BF16),
        compiler_params=_cparams("arbitrary", "arbitrary"),
    )(up, w8, bias)


def glu_bwd(up, w8, bias, dact):
    nb, S, _ = up.shape
    tc = GLU_TC

    def body(u_ref, w_ref, b_ref, d_ref, du_ref, dw_ref):
        b = pl.program_id(1)
        bounds = _seg_bounds(S, 0)
        x, w, val, d = u_ref[0, :, :tc], w_ref[...], u_ref[0, :, tc:], d_ref[0].astype(F32)
        gc = _conv(x, w, b_ref[...], FFN_K, bounds)
        du_ref[0, :, tc:] = (d * _gelu(gc)).astype(du_ref.dtype)
        dx, rows = _conv_bwd(x, w, d * val * _gelu_grad(gc), FFN_K, bounds)
        du_ref[0, :, :tc] = dx.astype(du_ref.dtype)

        @pl.when(b == 0)
        def _():
            dw_ref[...] = rows

        @pl.when(b > 0)
        def _():
            dw_ref[...] += rows

    pair = pl.BlockSpec((1, S, 2 * tc), lambda j, b: (b, 0, j))
    return pl.pallas_call(
        body, name="glu_bwd", grid=(D_FF // tc, nb),
        in_specs=[pair, pl.BlockSpec((8, tc), lambda j, b: (0, j)), pl.BlockSpec((1, tc), lambda j, b: (0, j)),
                  pl.BlockSpec((1, S, tc), lambda j, b: (b, 0, j))],
        out_specs=[pair, pl.BlockSpec((8, tc), lambda j, b: (0, j))],
        out_shape=[jax.ShapeDtypeStruct((nb, S, 2 * D_FF), BF16), jax.ShapeDtypeStruct((8, D_FF), F32)],
        compiler_params=_cparams("arbitrary", "arbitrary"),
    )(up, w8, bias, dact)


def _chunk_of(d, k, n_cc, n_ch):
    rev = jnp.where(k < n_cc, n_cc - 1 - k, n_cc + n_ch - 1 - k)
    return jnp.where(d == 1, rev, k)


def _lane_pick(v, lane_iota, l):
    return jnp.sum(jnp.where(lane_iota == l, v, 0.0), axis=1, keepdims=True)


def _row_pick(v, sub_iota, l):
    return jnp.sum(jnp.where(sub_iota == l, v, 0.0), axis=0, keepdims=True)


def _softplus(x):
    return jnp.maximum(x, 0.0) + jnp.log(1.0 + jnp.exp(-jnp.abs(x)))


def _ssd_common(d, dt_raw, alog, dtb):
    Q = dt_raw.shape[0]
    row = lax.broadcasted_iota(jnp.int32, (Q, Q), 0)
    col = lax.broadcasted_iota(jnp.int32, (Q, Q), 1)
    rev = d == 1
    maskb = jnp.where(rev, row, col) <= jnp.where(rev, col, row)
    tri = maskb.astype(F32)
    A = -jnp.exp(alog)
    dtv = _softplus(dt_raw + dtb)
    a = dtv * A
    cum = lax.dot_general(tri, a, (((1,), (0,)), ((), ())), precision=lax.Precision.HIGHEST, preferred_element_type=F32)
    tot = jnp.sum(a, axis=0, keepdims=True)
    return maskb, tri, A, dtv, cum, tot


def ssd_fwd(xbc, u, alog, dtb, n_ctx):
    nb, T, _ = xbc.shape
    S = T - n_ctx
    n_ch, n_cc = T // CHUNK, n_ctx // CHUNK
    dt_cb = OFF_DT // LANE
    Q = CHUNK
    n_pairs = SSD_HEADS // 2

    def body(x_ref, dt_ref, al_ref, db_ref, y_ref, hin_ref, H):
        d, k = pl.program_id(1), pl.program_id(2)

        @pl.when(k == 0)
        def _():
            H[...] = jnp.zeros_like(H)

        maskb, tri, A, dtv, cum, tot = _ssd_common(d, dt_ref[0], al_ref[...], db_ref[...])
        cumT = cum.T
        hin_ref[0, 0, 0] = H[...].astype(BF16)
        lane = lax.broadcasted_iota(jnp.int32, (Q, LANE), 1)
        lane1 = lax.broadcasted_iota(jnp.int32, (1, LANE), 1)
        sub = lax.broadcasted_iota(jnp.int32, (LANE, Q), 0)
        subc = lax.broadcasted_iota(jnp.int32, (LANE, 1), 0)
        half = lane < SSD_P
        for g in range(SSD_GROUPS):
            Bg = x_ref[0, :, D_INNER + g * SSD_N:D_INNER + (g + 1) * SSD_N].astype(BF16)
            Cg = x_ref[0, :, D_INNER + GN + g * SSD_N:D_INNER + GN + (g + 1) * SSD_N].astype(BF16)
            Gm = lax.dot_general(Cg, Bg, (((1,), (1,)), ((), ())), preferred_element_type=F32)
            for pr in range(n_pairs // SSD_GROUPS):
                p = g * (n_pairs // SSD_GROUPS) + pr
                l0 = d * SSD_HEADS + 2 * p
                l1 = l0 + 1
                s0c, s1c = _lane_pick(cum, lane, l0), _lane_pick(cum, lane, l1)
                s0r, s1r = _row_pick(cumT, sub, l0), _row_pick(cumT, sub, l1)
                dtp = jnp.where(half, _lane_pick(dtv, lane, l0), _lane_pick(dtv, lane, l1))
                tot0, tot1 = _lane_pick(tot, lane1, l0), _lane_pick(tot, lane1, l1)
                sc = jnp.where(half, s0c, s1c)
                totp = jnp.where(half, tot0, tot1)
                M0 = (Gm * jnp.exp(jnp.where(maskb, s0c - s0r, NEG_BIG))).astype(BF16)
                M1 = (Gm * jnp.exp(jnp.where(maskb, s1c - s1r, NEG_BIG))).astype(BF16)
                xd = x_ref[0, :, p * LANE:(p + 1) * LANE] * dtp
                xdb = xd.astype(BF16)
                yd = jnp.where(half,
                               lax.dot_general(M0, xdb, (((1,), (0,)), ((), ())), preferred_element_type=F32),
                               lax.dot_general(M1, xdb, (((1,), (0,)), ((), ())), preferred_element_type=F32))
                Hp = H[p * LANE:(p + 1) * LANE, :]
                yo = lax.dot_general(Cg, Hp.astype(BF16), (((1,), (1,)), ((), ())), preferred_element_type=F32) * jnp.exp(sc)

                @pl.when(k >= n_cc)
                def _():
                    y_ref[0, 0, :, p * LANE:(p + 1) * LANE] = yd + yo

                xdw = (xd * jnp.exp(totp - sc)).astype(BF16)
                etot = jnp.exp(jnp.where(subc < SSD_P, tot0, tot1))
                H[p * LANE:(p + 1) * LANE, :] = Hp * etot + lax.dot_general(
                    xdw, Bg, (((0,), (0,)), ((), ())), preferred_element_type=F32)

    def ymap(b, d, k):
        return (d, b, _chunk_of(d, jnp.maximum(k, n_cc), n_cc, n_ch) - n_cc, 0)

    return pl.pallas_call(
        body, name="ssd_fwd", grid=(nb, 2, n_ch),
        in_specs=[pl.BlockSpec((1, Q, XBC), lambda b, d, k: (b, _chunk_of(d, k, n_cc, n_ch), 0)),
                  pl.BlockSpec((1, Q, LANE), lambda b, d, k: (b, _chunk_of(d, k, n_cc, n_ch), dt_cb)),
                  pl.BlockSpec((1, LANE), lambda b, d, k: (0, 0)), pl.BlockSpec((1, LANE), lambda b, d, k: (0, 0))],
        out_specs=[pl.BlockSpec((1, 1, Q, D_INNER), ymap),
                   pl.BlockSpec((1, 1, 1, D_INNER, SSD_N), lambda b, d, k: (d, b, k, 0, 0))],
        out_shape=[jax.ShapeDtypeStruct((2, nb, S, D_INNER), F32),
                   jax.ShapeDtypeStruct((2, nb, n_ch, D_INNER, SSD_N), BF16)],
        scratch_shapes=[pltpu.VMEM((D_INNER, SSD_N), F32)],
        compiler_params=_cparams("arbitrary", "arbitrary", "arbitrary"),
    )(xbc, u, alog, dtb)


def ssd_bwd(xbc, u, alog, dtb, hin, dy, n_ctx):
    nb, T, _ = xbc.shape
    n_ch, n_cc = T // CHUNK, n_ctx // CHUNK
    dt_cb = OFF_DT // LANE
    Q = CHUNK
    n_pairs = SSD_HEADS // 2
    NT = (((1,), (1,)), ((), ()))
    NN = (((1,), (0,)), ((), ()))
    TN = (((0,), (0,)), ((), ()))

    def dot(a, b, dims):
        return lax.dot_general(a.astype(BF16), b.astype(BF16), dims, preferred_element_type=F32)

    def body(x_ref, dt_ref, al_ref, db_ref, hin_ref, dy_ref, dx_ref, ddt_ref, st_ref, dH):
        d, kk = pl.program_id(1), pl.program_id(2)
        ks = n_ch - 1 - kk

        @pl.when(kk == 0)
        def _():
            dH[...] = jnp.zeros_like(dH)

        @pl.when(jnp.logical_and(jnp.logical_and(pl.program_id(0) == 0, d == 0), kk == 0))
        def _():
            st_ref[...] = jnp.zeros_like(st_ref)

        dt_raw = dt_ref[0]
        alog, dtb_v = al_ref[...], db_ref[...]
        maskb, tri, A, dtv, cum, tot = _ssd_common(d, dt_raw, alog, dtb_v)
        cumT = cum.T
        live = (ks >= n_cc).astype(F32)
        lane = lax.broadcasted_iota(jnp.int32, (Q, LANE), 1)
        lane1 = lax.broadcasted_iota(jnp.int32, (1, LANE), 1)
        sub = lax.broadcasted_iota(jnp.int32, (LANE, Q), 0)
        subc = lax.broadcasted_iota(jnp.int32, (LANE, 1), 0)
        half = lane < SSD_P
        halfc = subc < SSD_P
        dcum = jnp.zeros((Q, LANE), F32)
        dcumT = jnp.zeros((LANE, Q), F32)
        ddt = jnp.zeros((Q, LANE), F32)
        dtot = jnp.zeros((1, LANE), F32)
        for g in range(SSD_GROUPS):
            Bg = x_ref[0, :, D_INNER + g * SSD_N:D_INNER + (g + 1) * SSD_N].astype(BF16)
            Cg = x_ref[0, :, D_INNER + GN + g * SSD_N:D_INNER + GN + (g + 1) * SSD_N].astype(BF16)
            Gm = lax.dot_general(Cg, Bg, NT, preferred_element_type=F32)
            dG = jnp.zeros((Q, Q), F32)
            dC = jnp.zeros((Q, SSD_N), F32)
            dB = jnp.zeros((Q, SSD_N), F32)
            for pr in range(n_pairs // SSD_GROUPS):
                p = g * (n_pairs // SSD_GROUPS) + pr
                l0 = d * SSD_HEADS + 2 * p
                l1 = l0 + 1
                s0c, s1c = _lane_pick(cum, lane, l0), _lane_pick(cum, lane, l1)
                s0r, s1r = _row_pick(cumT, sub, l0), _row_pick(cumT, sub, l1)
                dtp = jnp.where(half, _lane_pick(dtv, lane, l0), _lane_pick(dtv, lane, l1))
                tot0, tot1 = _lane_pick(tot, lane1, l0), _lane_pick(tot, lane1, l1)
                sc = jnp.where(half, s0c, s1c)
                totp = jnp.where(half, tot0, tot1)
                L0 = jnp.exp(jnp.where(maskb, s0c - s0r, NEG_BIG))
                L1 = jnp.exp(jnp.where(maskb, s1c - s1r, NEG_BIG))
                M0, M1 = Gm * L0, Gm * L1
                xs = x_ref[0, :, p * LANE:(p + 1) * LANE]
                xd = xs * dtp
                es = jnp.exp(sc)
                dte = jnp.exp(totp - sc)
                etot = jnp.exp(jnp.where(halfc, tot0, tot1))
                dyp = dy_ref[0, :, p * LANE:(p + 1) * LANE] * live
                Hp = hin_ref[0, 0, 0, p * LANE:(p + 1) * LANE, :]
                dHp = dH[p * LANE:(p + 1) * LANE, :]
                bdh = dot(Bg, dHp, NT)
                dxd = jnp.where(half, dot(M0, dyp, TN), dot(M1, dyp, TN)) + bdh * dte
                dy0 = jnp.where(half, dyp, 0.0)
                dy1 = dyp - dy0
                dM0, dM1 = dot(dy0, xd, NT), dot(dy1, xd, NT)
                dG = dG + dM0 * L0 + dM1 * L1
                dyes = dyp * es
                xdw = xd * dte
                dC = dC + dot(dyes, Hp, NN)
                dB = dB + dot(xdw, dHp, NN)
                W0, W1 = dM0 * M0, dM1 * M1
                yoff = dot(Cg, Hp, NT) * es
                r_off = dyp * yoff
                r_st = xd * bdh * dte
                hh = jnp.sum(dHp * Hp.astype(F32), axis=1, keepdims=True) * etot
                for (l, W, hsel, hselc) in ((l0, W0, half, halfc),
                                            (l1, W1, jnp.logical_not(half), jnp.logical_not(halfc))):
                    col_g = (jnp.sum(W, axis=1, keepdims=True)
                             + jnp.sum(jnp.where(hsel, r_off - r_st, 0.0), axis=1, keepdims=True))
                    row_g = -jnp.sum(W, axis=0, keepdims=True)
                    tot_g = (jnp.sum(jnp.sum(jnp.where(hsel, r_st, 0.0), axis=1, keepdims=True), axis=0, keepdims=True)
                             + jnp.sum(jnp.where(hselc, hh, 0.0), axis=0, keepdims=True))
                    dcum = dcum + jnp.where(lane == l, col_g, 0.0)
                    dcumT = dcumT + jnp.where(sub == l, row_g, 0.0)
                    dtot = dtot + jnp.where(lane1 == l, tot_g, 0.0)
                    ddt = ddt + jnp.where(lane == l, jnp.sum(jnp.where(hsel, dxd * xs, 0.0), axis=1, keepdims=True), 0.0)
                dx_ref[0, 0, :, p * LANE:(p + 1) * LANE] = dxd * dtp
                dH[p * LANE:(p + 1) * LANE, :] = dHp * etot + dot(dyes, Cg, TN)
            dx_ref[0, 0, :, D_INNER + g * SSD_N:D_INNER + (g + 1) * SSD_N] = dB + dot(dG, Cg, TN)
            dx_ref[0, 0, :, D_INNER + GN + g * SSD_N:D_INNER + GN + (g + 1) * SSD_N] = dC + dot(dG, Bg, NN)
        dcum_all = dcum + dcumT.T
        da = lax.dot_general(tri, dcum_all, TN, precision=lax.Precision.HIGHEST, preferred_element_type=F32) + dtot
        ddtv = ddt + da * A
        ddt_raw = ddtv * jax.nn.sigmoid(dt_raw + dtb_v)
        ddt_ref[0, 0] = ddt_raw
        st_ref[0:1, :] += jnp.sum(da * dtv * A, axis=0, keepdims=True)
        st_ref[1:2, :] += jnp.sum(ddt_raw, axis=0, keepdims=True)

    def cmap(d, kk):
        return _chunk_of(d, n_ch - 1 - kk, n_cc, n_ch)

    def dymap(b, d, kk):
        return (b, _chunk_of(d, jnp.maximum(n_ch - 1 - kk, n_cc), n_cc, n_ch) - n_cc, 0)

    return pl.pallas_call(
        body, name="ssd_bwd", grid=(nb, 2, n_ch),
        in_specs=[pl.BlockSpec((1, Q, XBC), lambda b, d, kk: (b, cmap(d, kk), 0)),
                  pl.BlockSpec((1, Q, LANE), lambda b, d, kk: (b, cmap(d, kk), dt_cb)),
                  pl.BlockSpec((1, LANE), lambda b, d, kk: (0, 0)), pl.BlockSpec((1, LANE), lambda b, d, kk: (0, 0)),
                  pl.BlockSpec((1, 1, 1, D_INNER, SSD_N), lambda b, d, kk: (d, b, n_ch - 1 - kk, 0, 0)),
                  pl.BlockSpec((1, Q, D_INNER), dymap)],
        out_specs=[pl.BlockSpec((1, 1, Q, XBC), lambda b, d, kk: (d, b, cmap(d, kk), 0)),
                   pl.BlockSpec((1, 1, Q, LANE), lambda b, d, kk: (d, b, cmap(d, kk), 0)),
                   pl.BlockSpec((8, LANE), lambda b, d, kk: (0, 0))],
        out_shape=[jax.ShapeDtypeStruct((2, nb, T, XBC), F32), jax.ShapeDtypeStruct((2, nb, T, LANE), F32),
                   jax.ShapeDtypeStruct((8, LANE), F32)],
        scratch_shapes=[pltpu.VMEM((D_INNER, SSD_N), F32)],
        compiler_params=_cparams("arbitrary", "arbitrary", "arbitrary"),
    )(xbc, u, alog, dtb, hin, dy)


def _adamw(w, g, m, v):
    mn = ADAM_B1 * m + (1.0 - ADAM_B1) * g
    vn = ADAM_B2 * v + (1.0 - ADAM_B2) * jnp.square(g)
    m_hat = mn / (1.0 - ADAM_B1 ** ADAM_STEP)
    v_hat = vn / (1.0 - ADAM_B2 ** ADAM_STEP)
    return -ADAM_LR * (m_hat / (jnp.sqrt(v_hat) + ADAM_EPS) + ADAM_WD * w), mn, vn


def adamw_matrix(name, w, g_slots, m, v):
    K, n = w.shape
    s = g_slots.shape[0]
    tr = _tile(K, 256, 8)

    def body(w_ref, g_ref, m_ref, v_ref, go_ref, d_ref, mo_ref, vo_ref):
        g = g_ref[0]
        for j in range(1, s):
            g = g + g_ref[j]
        go_ref[...] = g
        d_ref[...], mo_ref[...], vo_ref[...] = _adamw(w_ref[...], g, m_ref[...], v_ref[...])

    spec = pl.BlockSpec((tr, n), lambda i: (i, 0))
    return pl.pallas_call(
        body, name=name, grid=(K // tr,),
        in_specs=[spec, pl.BlockSpec((s, tr, n), lambda i: (0, i, 0)), spec, spec], out_specs=[spec] * 4,
        out_shape=[jax.ShapeDtypeStruct((K, n), F32)] * 4,
        compiler_params=_cparams("arbitrary"),
    )(w, g_slots, m, v)


def adamw_small(ws, gs, ms, vs):
    n = len(ws)

    def body(*refs):
        for i in range(n):
            d, mn, vn = _adamw(refs[i][...], refs[n + i][...], refs[2 * n + i][...], refs[3 * n + i][...])
            refs[4 * n + i][...] = d
            refs[5 * n + i][...] = mn
            refs[6 * n + i][...] = vn

    shapes = [jax.ShapeDtypeStruct(w.shape, F32) for w in ws]
    out = pl.pallas_call(body, name="adamw_small", out_shape=shapes * 3)(*ws, *gs, *ms, *vs)
    return out[:n], out[n:2 * n], out[2 * n:]


def sum_slots(name, x):
    n = x.shape[0]

    def fn(t):
        acc = t[0]
        for j in range(1, n):
            acc = acc + t[j]
        return (acc,)

    return ew_call(name, fn, [x], [(x.shape[1:], F32)])[0]


def _pack_rows(parts):
    rows = []
    for p in parts:
        flat = p.reshape(1, -1)
        n = flat.shape[1]
        rows.append(jnp.pad(flat, ((0, 0), (0, -(-n // (8 * LANE)) * 8 * LANE - n))).reshape(-1, LANE))
    return jnp.concatenate(rows, axis=0)


def _unpack_rows(pack, shapes):
    out, r = [], 0
    for s in shapes:
        n = int(np.prod(s))
        nr = -(-n // (8 * LANE)) * 8
        out.append(pack[r:r + nr].reshape(1, -1)[:, :n].reshape(s))
        r += nr
    return out


def _mesh_pos():
    return lax.axis_index("x"), lax.axis_index("y"), lax.axis_index("c")


N_PEERS = N_DEV - 1


def all_gather(name, vs):
    n = len(vs)

    def body(*refs):
        x_refs, out_refs = refs[:n], refs[n:2 * n]
        send_sems, recv_sems, local_sems = refs[2 * n:]
        x, y, c = _mesh_pos()
        me, sibling = (x, y, c), (x, y, 1 - c)
        chips = [(1 - x, y), (x, 1 - y), (1 - x, 1 - y)]

        def slot(a, px, py, pc):
            return out_refs[a].at[4 * px + 2 * py + pc]

        def copy(a, k, block, to, src=None):
            return pltpu.make_async_remote_copy(
                src_ref=slot(a, *block) if src is None else src, dst_ref=slot(a, *block),
                send_sem=send_sems.at[N_PEERS * a + k], recv_sem=recv_sems.at[N_PEERS * a + k],
                device_id=to, device_id_type=MESH)

        mine = [pltpu.make_async_copy(x_refs[a], slot(a, *me), local_sems.at[a]) for a in range(n)]
        for cp in mine:
            cp.start()
        first = []
        for a in range(n):
            first.append(copy(a, 0, me, sibling, src=x_refs[a]))
            first += [copy(a, 1 + j, me, (*chip, c), src=x_refs[a]) for j, chip in enumerate(chips)]
        for cp in first:
            cp.start()
        passed = []
        for j, chip in enumerate(chips):
            for a in range(n):
                copy(a, 1 + j, (*chip, c), me).wait_recv()
                passed.append(copy(a, 4 + j, (*chip, c), sibling))
                passed[-1].start()
        for a in range(n):
            copy(a, 0, sibling, me).wait_recv()
            for j, chip in enumerate(chips):
                copy(a, 4 + j, (*chip, 1 - c), me).wait_recv()
        for cp in first + passed:
            cp.wait_send()
        for cp in mine:
            cp.wait()

    hbm = pl.BlockSpec(memory_space=pl.ANY)
    return pl.pallas_call(
        body, name=name, out_shape=[jax.ShapeDtypeStruct((N_DEV,) + v.shape, v.dtype) for v in vs],
        in_specs=[hbm] * n, out_specs=[hbm] * n,
        scratch_shapes=[pltpu.SemaphoreType.DMA((N_PEERS * n,)), pltpu.SemaphoreType.DMA((N_PEERS * n,)),
                        pltpu.SemaphoreType.DMA((n,))],
    )(*vs)


def all_to_all(name, vs):
    n = len(vs)

    def body(*refs):
        x_refs, out_refs = refs[:n], refs[n:2 * n]
        send_sems, recv_sems, local_sems = refs[2 * n:]
        x, y, c = _mesh_pos()
        me = 4 * x + 2 * y + c
        mine = [pltpu.make_async_copy(x_refs[a].at[me], out_refs[a].at[me], local_sems.at[a]) for a in range(n)]
        for cp in mine:
            cp.start()
        copies = []
        for k in range(1, N_DEV):
            px, py, pc = x ^ ((k >> 2) & 1), y ^ ((k >> 1) & 1), c ^ (k & 1)
            for a in range(n):
                copies.append(pltpu.make_async_remote_copy(
                    src_ref=x_refs[a].at[4 * px + 2 * py + pc], dst_ref=out_refs[a].at[me],
                    send_sem=send_sems.at[N_PEERS * a + k - 1], recv_sem=recv_sems.at[N_PEERS * a + k - 1],
                    device_id=(px, py, pc), device_id_type=MESH))
        for cp in copies:
            cp.start()
        for cp in copies:
            cp.wait_recv()
        for cp in copies:
            cp.wait_send()
        for cp in mine:
            cp.wait()

    hbm = pl.BlockSpec(memory_space=pl.ANY)
    return pl.pallas_call(
        body, name=name, out_shape=[jax.ShapeDtypeStruct(v.shape, v.dtype) for v in vs],
        in_specs=[hbm] * n, out_specs=[hbm] * n,
        scratch_shapes=[pltpu.SemaphoreType.DMA((N_PEERS * n,)), pltpu.SemaphoreType.DMA((N_PEERS * n,)),
                        pltpu.SemaphoreType.DMA((n,))],
    )(*vs)


def _taps8(w):
    return jnp.concatenate([w, jnp.zeros((8 - w.shape[0], w.shape[1]), w.dtype)], axis=0)


def _lanes128(v):
    v = v.reshape(1, -1)
    return jnp.pad(v, ((0, 0), (0, LANE - v.shape[1])))


def weights_to_internal(w_in, w_q_up, w_kv_up, w_out, w_up, w_down):
    cq, ckv, kr, z, xbc, dt = jnp.split(w_in, np.cumsum(IN_SPLITS)[:-1].tolist(), axis=1)
    K = w_in.shape[0]

    def zeros(n):
        return jnp.zeros((K, n), w_in.dtype)

    w_in_p = jnp.concatenate([cq, zeros(KR_LANE), kr, zeros(LANE - KR_LANE - ROPE), ckv, zeros(OFF_Z - OFF_CKV - KV_RANK),
                              z, xbc, dt, zeros(WIN_P - OFF_DT - 2 * SSD_HEADS)], axis=1)
    w_q_p = jnp.pad(w_q_up.reshape(Q_RANK, N_HEADS, NOPE + ROPE), ((0, 0), (0, 0), (0, HEAD_BLOCK - NOPE - ROPE))).reshape(Q_RANK, QP)
    attn_rows = w_out[:N_HEADS * V_DIM].reshape(N_HEADS, V_DIM, -1)
    w_out_p = jnp.concatenate([jnp.pad(attn_rows, ((0, 0), (HEAD_BLOCK - V_DIM, 0), (0, 0))).reshape(QP, -1),
                               w_out[N_HEADS * V_DIM:]], axis=0)
    return dict(w_in_p=w_in_p, w_q_p=w_q_p, w_kv=w_kv_up, w_out_p=w_out_p, w_up=glu_interleave(w_up), w_down=w_down)


def grads_from_internal(g_in_p, g_q_p, g_kv, g_out_p, g_up, g_down):
    g_in = jnp.concatenate([g_in_p[:, OFF_CQ:OFF_CQ + Q_RANK], g_in_p[:, OFF_CKV:OFF_CKV + KV_RANK],
                            g_in_p[:, OFF_KR + KR_LANE:OFF_KR + KR_LANE + ROPE], g_in_p[:, OFF_Z:OFF_Z + D_INNER],
                            g_in_p[:, OFF_XBC:OFF_XBC + XBC], g_in_p[:, OFF_DT:OFF_DT + 2 * SSD_HEADS]], axis=1)
    g_q = g_q_p.reshape(Q_RANK, N_HEADS, HEAD_BLOCK)[:, :, :NOPE + ROPE].reshape(Q_RANK, -1)
    g_out = jnp.concatenate([g_out_p[:QP].reshape(N_HEADS, HEAD_BLOCK, -1)[:, HEAD_BLOCK - V_DIM:].reshape(N_HEADS * V_DIM, -1),
                             g_out_p[QP:]], axis=0)
    return g_in, g_q, g_kv, g_out, glu_deinterleave(g_up), g_down


def local_step(x, ctx, target, mod_x, mod_c, W, V):
    nb, S, D = x.shape
    C = ctx.shape[1]
    T = C + S
    tr = _tile(math.gcd(C, S), 256, 8)
    tq = _tile(S, 256, 8)
    tc = 256
    cblk = C // tr
    m = [mod_x[:, i * D:(i + 1) * D][:, None, :] for i in range(N_MOD)]
    mc = [mod_c[:, i * D:(i + 1) * D] for i in range(2)]
    ssd_w8, ffn_w8 = _taps8(V["ssd_conv_w"]), _taps8(V["ffn_conv_w"])
    alog, dtb = _lanes128(V["ssd_a_log"]), _lanes128(V["ssd_dt_bias"])
    dexp = jnp.repeat(V["ssd_d"].reshape(-1), SSD_P).reshape(1, D_INNER)
    cosT, sinT = rope_tables(C, S)
    cosS, sinS = cosT[C:], sinT[C:]

    (h1x,) = rows_fwd("prenorm_x", fn_prenorm, nb, S // tr, tr, [(x, D, 0, 0)], [m[0], m[1]], [V["mix_pre_norm"]], [(D, BF16)])
    (h1c,) = rows_fwd("prenorm_c", fn_prenorm, nb, C // tr, tr, [(ctx, D, 0, 0)], [], [mc[0], mc[1], V["mix_pre_norm"]], [(D, BF16)])
    h1 = jnp.concatenate([h1c, h1x], axis=1).reshape(nb * T, D)
    u = matmul("in_proj", [(h1, W["w_in_p"])], "nn", F32).reshape(nb, T, WIN_P)
    (qn,) = rows_fwd("q_norm", fn_rms, nb, S // tr, tr, [(u, Q_RANK, OFF_CQ // Q_RANK, cblk)], [], [V["q_norm"]], [(Q_RANK, BF16)])
    (kvn,) = rows_fwd("kv_norm", fn_rms, nb, T // tr, tr, [(u, KV_RANK, OFF_CKV // KV_RANK, 0)], [], [V["kv_norm"]], [(KV_RANK, BF16)])
    qn2, kvn2 = qn.reshape(nb * S, Q_RANK), kvn.reshape(nb * T, KV_RANK)
    q_raw = matmul("q_up", [(qn2, W["w_q_p"])], "nn", F32).reshape(nb, S, QP)
    kv = matmul("kv_up", [(kvn2, W["w_kv"])], "nn", BF16).reshape(nb, T, QP)
    q = rope_call("rope_q", q_raw, QP, 0, cosS * Q_PRESCALE, sinS * Q_PRESCALE, BF16, tr)
    kr = rope_call("rope_k", u, LANE, OFF_KR // LANE, cosT, sinT, BF16, tr)
    o = attn_fwd(q, kv, kr, tq)
    xbc = ssd_conv_fwd(u, ssd_w8, V["ssd_conv_b"], C, tc)
    y2, hin = ssd_fwd(xbc, u, alog, dtb, C)
    fin_rows = [(y2[0], D_INNER, 0, 0), (y2[1], D_INNER, 0, 0), (xbc, D_INNER, 0, cblk), (u, D_INNER, OFF_Z // D_INNER, cblk)]
    fin_gl = [dexp, V["ssd_norm"]]
    (ssd,) = rows_fwd("ssd_finish", fn_ssd_finish, nb, S // tr, tr, fin_rows, [], fin_gl, [(D_INNER, BF16)])
    o2, ssd2 = o.reshape(nb * S, QP), ssd.reshape(nb * S, D_INNER)
    mix = matmul("out_proj", [(o2, W["w_out_p"][:QP]), (ssd2, W["w_out_p"][QP:])], "nn", F32).reshape(nb, S, D)
    pm_rows = [(x, D, 0, 0), (mix, D, 0, 0)]
    pm_pb = [m[2], m[4], m[3]]
    pm_gl = [V["mix_post_norm"], V["ffn_pre_norm"]]
    x1, h2 = rows_fwd("postmix", fn_postmix, nb, S // tr, tr, pm_rows, pm_pb, pm_gl, [(D, F32), (D, BF16)])
    h22 = h2.reshape(nb * S, D)
    up = matmul("up_proj", [(h22, W["w_up"])], "nn", F32).reshape(nb, S, 2 * D_FF)
    act = glu_fwd(up, ffn_w8, V["ffn_conv_b"])
    act2 = act.reshape(nb * S, D_FF)
    ffn = matmul("down_proj", [(act2, W["w_down"])], "nn", F32).reshape(nb, S, D)
    dx1, dffn, dgate2, d_ffn_post, loss = final_call(x1, ffn, target, m[5], V["ffn_post_norm"], tr)

    dffn2 = dffn.reshape(nb * S, D)
    dact = matmul("down_dgrad", [(dffn2, W["w_down"])], "nt", BF16).reshape(nb, S, D_FF)
    g_down = matmul_tn("down_wgrad", act2, dffn2)
    dup, ffn_rows = glu_bwd(up, ffn_w8, V["ffn_conv_b"], dact)
    dup2 = dup.reshape(nb * S, 2 * D_FF)
    dh2 = matmul("up_dgrad", [(dup2, W["w_up"])], "nt", BF16).reshape(nb, S, D)
    g_up = matmul_tn("up_wgrad", h22, dup2)
    dx_a, dmix, dgate1, dscale2, dshift2, d_mix_post, d_ffn_pre = rows_bwd(
        "postmix_bwd", fn_postmix, nb, S // tr, tr, pm_rows, pm_pb, pm_gl,
        [(dx1, D, 0, 0), (dh2, D, 0, 0)], [(0, F32), (1, BF16)])
    dmix2 = dmix.reshape(nb * S, D)
    dcat = matmul("out_dgrad", [(dmix2, W["w_out_p"])], "nt", BF16).reshape(nb, S, QP + D_INNER)
    g_out_p = jnp.concatenate([matmul_tn("out_wgrad_attn", o2, dmix2), matmul_tn("out_wgrad_ssd", ssd2, dmix2)], axis=0)
    dy, dxs_direct, dz, d_dexp, d_ssd_norm = rows_bwd(
        "ssd_finish_bwd", fn_ssd_finish, nb, S // tr, tr, fin_rows, [], fin_gl,
        [(dcat, D_INNER, QP // D_INNER, 0)], [(0, F32), (2, F32), (3, BF16)])
    dxbc2, ddt2, ssd_stats = ssd_bwd(xbc, u, alog, dtb, hin, dy, C)
    dxbc_raw, ssd_rows = ssd_conv_bwd(u, ssd_w8, V["ssd_conv_b"], dxbc2, dxs_direct, C, tc)
    dq, dkv, dkr = attn_bwd(q, kv, kr, dcat, tq)
    dq_pre = rope_call("rope_dq", dq, QP, 0, cosS, -sinS, BF16, tr).reshape(nb * S, QP)
    dkr_pre = rope_call("rope_dk", dkr, LANE, 0, cosT, -sinT, BF16, tr)
    dkv2 = dkv.reshape(nb * T, QP)
    dqn = matmul("q_dgrad", [(dq_pre, W["w_q_p"])], "nt", F32).reshape(nb, S, Q_RANK)
    g_q_p = matmul_tn("q_wgrad", qn2, dq_pre)
    dkvn = matmul("kv_dgrad", [(dkv2, W["w_kv"])], "nt", F32).reshape(nb, T, KV_RANK)
    g_kv = matmul_tn("kv_wgrad", kvn2, dkv2)
    dcq, d_q_norm = rows_bwd("q_norm_bwd", fn_rms, nb, S // tr, tr, [(u, Q_RANK, OFF_CQ // Q_RANK, cblk)], [], [V["q_norm"]],
                             [(dqn, Q_RANK, 0, 0)], [(0, BF16)])
    dckv, d_kv_norm = rows_bwd("kv_norm_bwd", fn_rms, nb, T // tr, tr, [(u, KV_RANK, OFF_CKV // KV_RANK, 0)], [], [V["kv_norm"]],
                               [(dkvn, KV_RANK, 0, 0)], [(0, BF16)])

    def ctx_rows(t):
        return jnp.pad(t, ((0, 0), (C, 0), (0, 0)))

    du = jnp.concatenate([ctx_rows(dcq), dkr_pre, dckv, jnp.zeros((nb, T, OFF_Z - OFF_CKV - KV_RANK), BF16), ctx_rows(dz),
                          dxbc_raw, (ddt2[0] + ddt2[1]).astype(BF16), jnp.zeros((nb, T, WIN_P - OFF_DT - LANE), BF16)],
                         axis=-1).reshape(nb * T, WIN_P)
    dh1 = matmul("in_dgrad", [(du, W["w_in_p"])], "nt", BF16).reshape(nb, T, D)
    g_in_p = matmul_tn("in_wgrad", h1, du)

    def fn_prenorm_res(xv, shift, scale, g):
        return fn_prenorm(xv, shift, scale, g) + (xv,)

    grad_x, dshift1, dscale1, d_mix_pre_x = rows_bwd(
        "prenorm_x_bwd", fn_prenorm_res, nb, S // tr, tr, [(x, D, 0, 0)], [m[0], m[1]], [V["mix_pre_norm"]],
        [(dh1, D, 0, cblk), (dx_a, D, 0, 0)], [(0, F32)])
    dshift_c, dscale_c, d_mix_pre_c = rows_bwd(
        "prenorm_c_bwd", fn_prenorm, nb, C // tr, tr, [(ctx, D, 0, 0)], [], [mc[0], mc[1], V["mix_pre_norm"]],
        [(dh1, D, 0, 0)], [])

    dmod_x = jnp.concatenate([dshift1, dscale1, dgate1, dshift2, dscale2, dgate2], axis=-1).reshape(nb, N_MOD * D)
    dmod_c = jnp.concatenate([dshift_c, dscale_c, jnp.zeros((1, (N_MOD - 2) * D), F32)], axis=-1)
    gm = dict(w_in_p=g_in_p, w_q_p=g_q_p, w_kv=g_kv, w_out_p=g_out_p, w_up=g_up, w_down=g_down)
    gv = dict(
        mix_pre_norm=d_mix_pre_x + d_mix_pre_c, mix_post_norm=d_mix_post, q_norm=d_q_norm, kv_norm=d_kv_norm,
        ssd_conv_w=ssd_rows[:SSD_K], ssd_conv_b=ssd_rows[SSD_K:SSD_K + 1],
        ssd_a_log=ssd_stats[0:1, :2 * SSD_HEADS], ssd_dt_bias=ssd_stats[1:2, :2 * SSD_HEADS],
        ssd_d=jnp.sum(d_dexp.reshape(SSD_HEADS, SSD_P), axis=1).reshape(1, SSD_HEADS), ssd_norm=d_ssd_norm,
        ffn_pre_norm=d_ffn_pre, ffn_post_norm=d_ffn_post,
        ffn_conv_w=ffn_rows[:FFN_K], ffn_conv_b=ffn_rows[FFN_K:FFN_K + 1])
    return loss, grad_x, dmod_x, dmod_c, gm, gv


WEIGHT_ORDER = ("c_ctx", "w_mod", "b_mod", "mix_pre_norm", "mix_post_norm", "w_in", "q_norm", "w_q_up", "kv_norm",
                "w_kv_up", "ssd_conv_w", "ssd_conv_b", "ssd_a_log", "ssd_dt_bias", "ssd_d", "ssd_norm", "w_out",
                "ffn_pre_norm", "ffn_post_norm", "w_up", "ffn_conv_w", "ffn_conv_b", "w_down")
MATRICES = ("w_in", "w_q_up", "w_kv_up", "w_out", "w_up", "w_down")
ROW_SHARDED = ("w_out", "w_down")
SMALL_SUMMED = ("c_ctx", "mix_pre_norm", "mix_post_norm", "q_norm", "kv_norm", "ssd_conv_w", "ssd_conv_b", "ssd_a_log",
                "ssd_dt_bias", "ssd_d", "ssd_norm", "ffn_pre_norm", "ffn_post_norm", "ffn_conv_w", "ffn_conv_b")
MOD_ROWS = 8


def _whole(shards, name):
    if name in ROW_SHARDED:
        return shards.reshape(-1, shards.shape[-1])
    return jnp.concatenate([shards[j] for j in range(N_DEV)], axis=1)


def _per_device(g, name):
    if name in ROW_SHARDED:
        return g.reshape(N_DEV, -1, g.shape[-1])
    return jnp.stack(jnp.split(g, N_DEV, axis=1))


def kernel(x, c, ctx, c_ctx, w_mod, b_mod, mix_pre_norm, mix_post_norm, w_in, q_norm, w_q_up, kv_norm, w_kv_up, ssd_conv_w, ssd_conv_b, ssd_a_log, ssd_dt_bias, ssd_d, ssd_norm, w_out, ffn_pre_norm, ffn_post_norm, w_up, ffn_conv_w, ffn_conv_b, w_down, loss_target, m_c_ctx, m_w_mod, m_b_mod, m_mix_pre_norm, m_mix_post_norm, m_w_in, m_q_norm, m_w_q_up, m_kv_norm, m_w_kv_up, m_ssd_conv_w, m_ssd_conv_b, m_ssd_a_log, m_ssd_dt_bias, m_ssd_d, m_ssd_norm, m_w_out, m_ffn_pre_norm, m_ffn_post_norm, m_w_up, m_ffn_conv_w, m_ffn_conv_b, m_w_down, v_c_ctx, v_w_mod, v_b_mod, v_mix_pre_norm, v_mix_post_norm, v_w_in, v_q_norm, v_w_q_up, v_kv_norm, v_w_kv_up, v_ssd_conv_w, v_ssd_conv_b, v_ssd_a_log, v_ssd_dt_bias, v_ssd_d, v_ssd_norm, v_w_out, v_ffn_pre_norm, v_ffn_post_norm, v_w_up, v_ffn_conv_w, v_ffn_conv_b, v_w_down):
    weights = dict(c_ctx=c_ctx, w_mod=w_mod, b_mod=b_mod, mix_pre_norm=mix_pre_norm, mix_post_norm=mix_post_norm, w_in=w_in, q_norm=q_norm, w_q_up=w_q_up, kv_norm=kv_norm, w_kv_up=w_kv_up, ssd_conv_w=ssd_conv_w, ssd_conv_b=ssd_conv_b, ssd_a_log=ssd_a_log, ssd_dt_bias=ssd_dt_bias, ssd_d=ssd_d, ssd_norm=ssd_norm, w_out=w_out, ffn_pre_norm=ffn_pre_norm, ffn_post_norm=ffn_post_norm, w_up=w_up, ffn_conv_w=ffn_conv_w, ffn_conv_b=ffn_conv_b, w_down=w_down)
    mom1 = dict(c_ctx=m_c_ctx, w_mod=m_w_mod, b_mod=m_b_mod, mix_pre_norm=m_mix_pre_norm, mix_post_norm=m_mix_post_norm, w_in=m_w_in, q_norm=m_q_norm, w_q_up=m_w_q_up, kv_norm=m_kv_norm, w_kv_up=m_w_kv_up, ssd_conv_w=m_ssd_conv_w, ssd_conv_b=m_ssd_conv_b, ssd_a_log=m_ssd_a_log, ssd_dt_bias=m_ssd_dt_bias, ssd_d=m_ssd_d, ssd_norm=m_ssd_norm, w_out=m_w_out, ffn_pre_norm=m_ffn_pre_norm, ffn_post_norm=m_ffn_post_norm, w_up=m_w_up, ffn_conv_w=m_ffn_conv_w, ffn_conv_b=m_ffn_conv_b, w_down=m_w_down)
    mom2 = dict(c_ctx=v_c_ctx, w_mod=v_w_mod, b_mod=v_b_mod, mix_pre_norm=v_mix_pre_norm, mix_post_norm=v_mix_post_norm, w_in=v_w_in, q_norm=v_q_norm, w_q_up=v_w_q_up, kv_norm=v_kv_norm, w_kv_up=v_w_kv_up, ssd_conv_w=v_ssd_conv_w, ssd_conv_b=v_ssd_conv_b, ssd_a_log=v_ssd_a_log, ssd_dt_bias=v_ssd_dt_bias, ssd_d=v_ssd_d, ssd_norm=v_ssd_norm, w_out=v_w_out, ffn_pre_norm=v_ffn_pre_norm, ffn_post_norm=v_ffn_post_norm, w_up=v_w_up, ffn_conv_w=v_ffn_conv_w, ffn_conv_b=v_ffn_conv_b, w_down=v_w_down)
    nb, S, D = x.shape
    me = 4 * lax.axis_index("x") + 2 * lax.axis_index("y") + lax.axis_index("c")

    gathered = all_gather("gather_weights", [weights[n][0].astype(BF16) for n in MATRICES])
    W = weights_to_internal(*[_whole(s, n) for n, s in zip(MATRICES, gathered)])
    c_all, ssd_w_sh, ffn_w_sh = all_gather("gather_small", [c, ssd_conv_w[0], ffn_conv_w[0]])
    V = {n: weights[n].reshape(1, -1) for n in SMALL_SUMMED if n != "c_ctx"}
    V["ssd_conv_w"] = _whole(ssd_w_sh, "ssd_conv_w")
    V["ffn_conv_w"] = _whole(ffn_w_sh, "ffn_conv_w")

    n_all = N_DEV * nb
    mod_rows = -(-(n_all + 1) // 8) * 8
    c_pad = jnp.concatenate([c_all.reshape(n_all, D), c_ctx.reshape(1, D), jnp.zeros((mod_rows - n_all - 1, D), F32)], axis=0)
    mod_cols = w_mod.shape[2]
    b_mine = lax.dynamic_slice(b_mod, (0, me * mod_cols), (1, mod_cols))
    mod_part = matmul("mod_proj", [(c_pad, w_mod[0])], "nn", F32, bias=b_mine, silu_a=True)
    mod_all = _whole(all_gather("gather_mod", [mod_part])[0], "w_mod")
    mod_x = lax.dynamic_slice(mod_all, (me * nb, 0), (nb, mod_all.shape[1]))
    mod_c = mod_all[n_all:n_all + 1]

    loss, grad_x, dmod_x, dmod_c, gm, gv = local_step(x, ctx, loss_target, mod_x, mod_c, W, V)

    dmod_mine = jnp.concatenate([dmod_x, dmod_c, jnp.zeros((MOD_ROWS - nb - 1, dmod_x.shape[1]), F32)], axis=0)
    dmod_all = all_gather("gather_dmod", [dmod_mine])[0]
    dmod_ctx = sum_slots("sum_dmod_ctx", dmod_all[:, nb:nb + 1].reshape(N_DEV, -1, LANE)).reshape(1, -1)
    dmod_full = jnp.concatenate([dmod_all[:, :nb].reshape(n_all, -1), dmod_ctx,
                                 jnp.zeros((mod_rows - n_all - 1, dmod_ctx.shape[1]), F32)], axis=0)
    (g_b_mod,) = ew_call("mod_bias_grad", lambda t: (jnp.sum(t, axis=0, keepdims=True),), [dmod_full], [((1, dmod_full.shape[1]), F32)])
    dmod_cols = lax.dynamic_slice(dmod_full, (0, me * mod_cols), (mod_rows, mod_cols))
    g_w_mod = matmul_tn("mod_wgrad", c_pad, dmod_cols, silu_a=True)
    dsilu_ctx = matmul("mod_dgrad_ctx", [(dmod_cols[n_all:n_all + 8], w_mod[0])], "nt", F32)[0:1]

    def silu_vjp(cc, ct):
        return (jax.vjp(_silu, cc)[1](ct)[0],)

    (g_c_ctx_part,) = ew_call("c_ctx_grad", silu_vjp, [c_ctx.reshape(1, D), dsilu_ctx], [((1, D), F32)])

    gv = dict(gv, c_ctx=g_c_ctx_part)
    small_parts = [loss] + [gv[n] for n in SMALL_SUMMED]
    small_sum = sum_slots("sum_small", all_gather("gather_small_grads", [_pack_rows(small_parts)])[0])
    summed = _unpack_rows(small_sum, [p.shape for p in small_parts])
    loss_out = summed[0][0, 0]
    grads = {n: g.reshape(weights[n].shape) if n not in ("ssd_conv_w", "ffn_conv_w") else g for n, g in zip(SMALL_SUMMED, summed[1:])}
    for n in ("ssd_conv_w", "ffn_conv_w"):
        cols = weights[n].shape[2]
        grads[n] = lax.dynamic_slice(grads[n], (0, me * cols), (grads[n].shape[0], cols)).reshape(weights[n].shape)
    grads["b_mod"] = g_b_mod.reshape(b_mod.shape)

    g_whole = grads_from_internal(*[gm[k] for k in ("w_in_p", "w_q_p", "w_kv", "w_out_p", "w_up", "w_down")])
    received = all_to_all("exchange_matrix_grads", [_per_device(g, n) for g, n in zip(g_whole, MATRICES)])
    slots = dict(zip(MATRICES, received), w_mod=g_w_mod[None])
    delta, new_m, new_v = {}, {}, {}
    for n in MATRICES + ("w_mod",):
        g, d, mn, vn = adamw_matrix("adamw_" + n, weights[n][0], slots[n], mom1[n][0], mom2[n][0])
        grads[n], delta[n], new_m[n], new_v[n] = [t.reshape(weights[n].shape) for t in (g, d, mn, vn)]
    small = [n for n in WEIGHT_ORDER if n not in slots]

    def two_d(t):
        return t.reshape(-1, t.shape[-1])

    ds, ms, vs = adamw_small(*[[two_d(t[n]) for n in small] for t in (weights, grads, mom1, mom2)])
    for n, d, mn, vn in zip(small, ds, ms, vs):
        delta[n], new_m[n], new_v[n] = [t.reshape(weights[n].shape) for t in (d, mn, vn)]
    return (loss_out, grad_x, *[t[n] for t in (grads, delta, new_m, new_v) for n in WEIGHT_ORDER])
```

```python
import functools
import math

import jax
import jax.numpy as jnp
import numpy as np
from jax import lax
from jax.experimental import pallas as pl
from jax.experimental.pallas import tpu as pltpu

F32 = jnp.float32
BF16 = jnp.bfloat16
MESH = pl.DeviceIdType.MESH

D_MODEL = 1024
GRID_W = 64
N_HEADS = 16
NOPE = 64
ROPE = 32
V_DIM = 64
Q_RANK = 384
KV_RANK = 256
ROPE_THETA = 10000.0
ATTN_SCALE = (NOPE + ROPE) ** -0.5
SSD_HEADS = 16
SSD_P = 64
SSD_GROUPS = 2
SSD_N = 128
SSD_K = 5
CHUNK = 128
D_INNER = SSD_HEADS * SSD_P
GN = SSD_GROUPS * SSD_N
XBC = D_INNER + 2 * GN
D_FF = 2816
FFN_K = 3
N_MOD = 6
EPS = 1e-6
IN_SPLITS = (Q_RANK, KV_RANK, ROPE, D_INNER, XBC, 2 * SSD_HEADS)
IN_WIDTH = sum(IN_SPLITS)
N_DEV = 8

ADAM_LR = 0.001
ADAM_B1 = 0.9
ADAM_B2 = 0.999
ADAM_EPS = 1e-08
ADAM_WD = 0.01
ADAM_STEP = 10

LANE = 128
HEAD_BLOCK = 128
OFF_CQ = 0
OFF_KR = 384
OFF_CKV = 512
OFF_Z = 1024
OFF_XBC = 2048
OFF_DT = 3584
WIN_P = 3840
KR_LANE = 64
QP = N_HEADS * HEAD_BLOCK

VMEM_LIMIT_V7X = 56 * 1024 * 1024
NEG_BIG = -1e30


def _cparams(*sem):
    return pltpu.CompilerParams(dimension_semantics=sem, vmem_limit_bytes=VMEM_LIMIT_V7X)


def _tile(n, target, mult=128):
    if n <= target:
        return n
    t = (target // mult) * mult
    while t >= mult:
        if n % t == 0:
            return t
        t -= mult
    return n


def _silu(x):
    return x * jax.nn.sigmoid(x)


def _rms(x, g):
    return x * lax.rsqrt(jnp.mean(x * x, axis=-1, keepdims=True) + EPS) * g


WHOLE_K_WIDE = 2048


def matmul(name, pairs, mode, out_dtype, *, bias=None, silu_a=False):
    n_pairs = len(pairs)
    M = pairs[0][0].shape[0]
    N = pairs[0][1].shape[1] if mode == "nn" else pairs[0][1].shape[0]
    k_total = sum(a.shape[1] for a, _ in pairs)
    tm = _tile(M, 1024 if k_total <= WHOLE_K_WIDE else 512, 8)
    tn = _tile(N, 1408 if k_total <= WHOLE_K_WIDE else 512)
    dims = (((1,), (0,)), ((), ())) if mode == "nn" else (((1,), (1,)), ((), ()))

    def body(*refs):
        o_ref = refs[-1]
        acc = None
        for p in range(n_pairs):
            a = refs[2 * p][...]
            if silu_a:
                a = _silu(a.astype(F32))
            d = lax.dot_general(a.astype(BF16), refs[2 * p + 1][...].astype(BF16), dims, preferred_element_type=F32)
            acc = d if acc is None else acc + d
        if bias is not None:
            acc = acc + refs[2 * n_pairs][...]
        o_ref[...] = acc.astype(o_ref.dtype)

    in_specs, args = [], []
    for a, b in pairs:
        K = a.shape[1]
        in_specs.append(pl.BlockSpec((tm, K), lambda j, i: (i, 0)))
        in_specs.append(pl.BlockSpec((K, tn), lambda j, i: (0, j)) if mode == "nn" else pl.BlockSpec((tn, K), lambda j, i: (j, 0)))
        args += [a, b]
    if bias is not None:
        in_specs.append(pl.BlockSpec((1, tn), lambda j, i: (0, j)))
        args.append(bias)
    return pl.pallas_call(
        body, name=name, grid=(N // tn, M // tm), in_specs=in_specs,
        out_specs=pl.BlockSpec((tm, tn), lambda j, i: (i, j)),
        out_shape=jax.ShapeDtypeStruct((M, N), out_dtype),
        compiler_params=_cparams("arbitrary", "arbitrary"),
    )(*args)


def matmul_tn(name, a, b, out_dtype=F32, *, silu_a=False, tm=1024, tn=512, tk=2048):
    R, M = a.shape
    N = b.shape[1]
    tm = _tile(M, tm)
    tn = _tile(N, tn)
    tk = _tile(R, tk, 8)
    nk = R // tk

    def body(a_ref, b_ref, o_ref, acc):
        k = pl.program_id(2)

        @pl.when(k == 0)
        def _():
            acc[...] = jnp.zeros_like(acc)

        x = a_ref[...]
        if silu_a:
            x = _silu(x.astype(F32))
        acc[...] += lax.dot_general(x.astype(BF16), b_ref[...].astype(BF16), (((0,), (0,)), ((), ())),
                                    preferred_element_type=F32)

        @pl.when(k == nk - 1)
        def _():
            o_ref[...] = acc[...].astype(o_ref.dtype)

    return pl.pallas_call(
        body, name=name, grid=(M // tm, N // tn, nk),
        in_specs=[pl.BlockSpec((tk, tm), lambda i, j, k: (k, i)), pl.BlockSpec((tk, tn), lambda i, j, k: (k, j))],
        out_specs=pl.BlockSpec((tm, tn), lambda i, j, k: (i, j)),
        out_shape=jax.ShapeDtypeStruct((M, N), out_dtype),
        scratch_shapes=[pltpu.VMEM((tm, tn), F32)],
        compiler_params=_cparams("arbitrary", "arbitrary", "arbitrary"),
    )(a, b)


def _row_specs(rin, pbin, glin, tr):
    specs = [pl.BlockSpec((1, tr, w), lambda b, i, cb=cb, ro=ro, bo=(e[4] if len(e) > 4 else 0): (b + bo, i + ro, cb))
             for e in rin for (_, w, cb, ro) in [e[:4]]]
    specs += [pl.BlockSpec((1, 1, a.shape[-1]), lambda b, i: (b, 0, 0)) for a in pbin]
    specs += [pl.BlockSpec((1, a.shape[-1]), lambda b, i: (0, 0)) for a in glin]
    return specs


def rows_fwd(name, fn, nb, nblk, tr, rin, pbin, glin, outs):
    nr, npb, ngl = len(rin), len(pbin), len(glin)
    n_in = nr + npb + ngl

    def body(*refs):
        args = [r[0].astype(F32) for r in refs[:nr + npb]] + [r[...] for r in refs[nr + npb:n_in]]
        res = fn(*args)
        for o, v in zip(refs[n_in:], res):
            o[0] = v.astype(o.dtype)

    return pl.pallas_call(
        body, name=name, grid=(nb, nblk), in_specs=_row_specs(rin, pbin, glin, tr),
        out_specs=[pl.BlockSpec((1, tr, w), lambda b, i: (b, i, 0)) for (w, _) in outs],
        out_shape=[jax.ShapeDtypeStruct((nb, nblk * tr, w), dt) for (w, dt) in outs],
        compiler_params=_cparams("arbitrary", "arbitrary"),
    )(*[e[0] for e in rin], *pbin, *glin)


def rows_bwd(name, fn, nb, nblk, tr, rin, pbin, glin, cts, want):
    nr, npb, ngl, nct = len(rin), len(pbin), len(glin), len(cts)
    n_in = nr + npb + ngl

    def body(*refs):
        b, i = pl.program_id(0), pl.program_id(1)
        args = [r[0].astype(F32) for r in refs[:nr + npb]] + [r[...] for r in refs[nr + npb:n_in]]
        ct = tuple(r[0].astype(F32) for r in refs[n_in:n_in + nct])
        _, vjp = jax.vjp(fn, *args)
        g = vjp(ct)
        orefs = refs[n_in + nct:]
        for o, (idx, _) in zip(orefs, want):
            o[0] = g[idx].astype(o.dtype)
        pb_refs = orefs[len(want):len(want) + npb]
        gl_refs = orefs[len(want) + npb:]

        @pl.when(i == 0)
        def _():
            for o, v in zip(pb_refs, g[nr:nr + npb]):
                o[0] = v

        @pl.when(i > 0)
        def _():
            for o, v in zip(pb_refs, g[nr:nr + npb]):
                o[0] += v

        first = jnp.logical_and(b == 0, i == 0)

        @pl.when(first)
        def _():
            for o, v in zip(gl_refs, g[nr + npb:]):
                o[...] = v

        @pl.when(jnp.logical_not(first))
        def _():
            for o, v in zip(gl_refs, g[nr + npb:]):
                o[...] += v

    out_specs = [pl.BlockSpec((1, tr, rin[idx][1]), lambda b, i: (b, i, 0)) for (idx, _) in want]
    out_shape = [jax.ShapeDtypeStruct((nb, nblk * tr, rin[idx][1]), dt) for (idx, dt) in want]
    out_specs += [pl.BlockSpec((1, 1, a.shape[-1]), lambda b, i: (b, 0, 0)) for a in pbin]
    out_shape += [jax.ShapeDtypeStruct((nb, 1, a.shape[-1]), F32) for a in pbin]
    out_specs += [pl.BlockSpec((1, a.shape[-1]), lambda b, i: (0, 0)) for a in glin]
    out_shape += [jax.ShapeDtypeStruct((1, a.shape[-1]), F32) for a in glin]
    return pl.pallas_call(
        body, name=name, grid=(nb, nblk),
        in_specs=_row_specs(rin, pbin, glin, tr) + _row_specs(cts, [], [], tr),
        out_specs=out_specs, out_shape=out_shape,
        compiler_params=_cparams("arbitrary", "arbitrary"),
    )(*[e[0] for e in rin], *pbin, *glin, *[e[0] for e in cts])


def ew_call(name, fn, ins, outs):
    def body(*refs):
        res = fn(*[r[...] for r in refs[:len(ins)]])
        for o, v in zip(refs[len(ins):], res):
            o[...] = v.astype(o.dtype)

    return pl.pallas_call(body, name=name, out_shape=[jax.ShapeDtypeStruct(s, dt) for (s, dt) in outs])(*ins)


def fn_prenorm(x, shift, scale, g):
    return (_rms(x, g) * (1.0 + scale) + shift,)


def fn_rms(x, g):
    return (_rms(x, g),)


def fn_ssd_finish(yf, yr, xs, z, dexp, nw):
    y = yf + yr + dexp * xs
    return (_rms(y * _silu(z), nw),)


def fn_postmix(x, mix, gate1, scale2, shift2, post_g, pre_g):
    x1 = x + gate1 * _rms(mix, post_g)
    h2 = _rms(x1, pre_g) * (1.0 + scale2) + shift2
    return x1, h2


def final_call(x1, ffn, target, gate2, post_g, tr):
    nb, S, D = x1.shape
    nblk = S // tr

    def body(x1_ref, f_ref, t_ref, g2_ref, pg_ref, dx1_ref, df_ref, dg2_ref, dpg_ref, loss_ref):
        b, i = pl.program_id(0), pl.program_id(1)
        tgt = t_ref[0]

        def lossfn(x1v, fv, g2, pg):
            e = x1v + g2 * _rms(fv, pg) - tgt
            return 0.5 * jnp.sum(jnp.mean(e * e, axis=-1, keepdims=True))

        val, (dx1, df, dg2, dpg) = jax.value_and_grad(lossfn, argnums=(0, 1, 2, 3))(
            x1_ref[0], f_ref[0].astype(F32), g2_ref[0], pg_ref[...])
        dx1_ref[0] = dx1
        df_ref[0] = df.astype(df_ref.dtype)
        lv = jnp.full((1, LANE), val, F32)

        @pl.when(i == 0)
        def _():
            dg2_ref[0] = dg2

        @pl.when(i > 0)
        def _():
            dg2_ref[0] += dg2

        first = jnp.logical_and(b == 0, i == 0)

        @pl.when(first)
        def _():
            dpg_ref[...] = dpg
            loss_ref[...] = lv

        @pl.when(jnp.logical_not(first))
        def _():
            dpg_ref[...] += dpg
            loss_ref[...] += lv

    row = pl.BlockSpec((1, tr, D), lambda b, i: (b, i, 0))
    pb = pl.BlockSpec((1, 1, D), lambda b, i: (b, 0, 0))
    gl = pl.BlockSpec((1, D), lambda b, i: (0, 0))
    return pl.pallas_call(
        body, name="loss_head", grid=(nb, nblk), in_specs=[row, row, row, pb, gl],
        out_specs=[row, row, pb, gl, pl.BlockSpec((1, LANE), lambda b, i: (0, 0))],
        out_shape=[jax.ShapeDtypeStruct((nb, S, D), F32), jax.ShapeDtypeStruct((nb, S, D), BF16),
                   jax.ShapeDtypeStruct((nb, 1, D), F32), jax.ShapeDtypeStruct((1, D), F32),
                   jax.ShapeDtypeStruct((1, LANE), F32)],
        compiler_params=_cparams("arbitrary", "arbitrary"),
    )(x1, ffn, target, gate2, post_g)


def _rotate_half(t):
    lane = lax.broadcasted_iota(jnp.int32, t.shape, 1)
    return jnp.where((lane & 15) < 8, -pltpu.roll(t, LANE - 8, 1), pltpu.roll(t, 8, 1))


def rope_call(name, x, width, colblk, cos, sin, out_dtype, tr):
    nb = x.shape[0]
    R = cos.shape[0]
    nblk = R // tr

    def body(x_ref, c_ref, s_ref, o_ref):
        c, s = c_ref[...], s_ref[...]
        for h in range(width // LANE):
            t = x_ref[0, :, h * LANE:(h + 1) * LANE].astype(F32)
            o_ref[0, :, h * LANE:(h + 1) * LANE] = (t * c + _rotate_half(t) * s).astype(o_ref.dtype)

    tab = pl.BlockSpec((tr, LANE), lambda b, i: (i, 0))
    return pl.pallas_call(
        body, name=name, grid=(nb, nblk),
        in_specs=[pl.BlockSpec((1, tr, width), lambda b, i: (b, i, colblk)), tab, tab],
        out_specs=pl.BlockSpec((1, tr, width), lambda b, i: (b, i, 0)),
        out_shape=jax.ShapeDtypeStruct((nb, R, width), out_dtype),
        compiler_params=_cparams("arbitrary", "arbitrary"),
    )(x, cos, sin)


def rope_tables(n_ctx, seq):
    n_rows = seq // GRID_W
    row = np.repeat(np.arange(n_rows), GRID_W).astype(np.float32)
    col = np.tile(np.arange(GRID_W), n_rows).astype(np.float32)
    axis_dim = ROPE // 2
    inv_freq = jnp.asarray(ROPE_THETA, F32) ** (-jnp.arange(0, axis_dim, 2, dtype=F32) / axis_dim)
    ang_r = jnp.asarray(row)[:, None] * inv_freq
    ang_c = jnp.asarray(col)[:, None] * inv_freq
    ang = jnp.concatenate([ang_r, ang_r, ang_c, ang_c], axis=-1)
    cos = jnp.ones((n_ctx + seq, LANE), F32).at[n_ctx:, KR_LANE:KR_LANE + ROPE].set(jnp.cos(ang))
    sin = jnp.zeros((n_ctx + seq, LANE), F32).at[n_ctx:, KR_LANE:KR_LANE + ROPE].set(jnp.sin(ang))
    return cos, sin


Q_PRESCALE = ATTN_SCALE * math.log2(math.e)


def _attn_weights(q, kc):
    s2 = lax.dot_general(q, kc, (((1,), (1,)), ((), ())), preferred_element_type=F32)
    e = jnp.exp2(s2 - jnp.max(s2, axis=1, keepdims=True))
    return e, 1.0 / jnp.sum(e, axis=1, keepdims=True)


def _key_block(kv, kr):
    lane = lax.broadcasted_iota(jnp.int32, kv.shape, 1)
    return jnp.where(lane < NOPE, kv, kr)


def attn_fwd(q, kv, kr, tq):
    nb, S, _ = q.shape
    T = kv.shape[1]

    def body(q_ref, kv_ref, kr_ref, o_ref):
        kvv = kv_ref[0]
        e, r = _attn_weights(q_ref[0], _key_block(kvv, kr_ref[0]))
        o = lax.dot_general(e.astype(BF16), kvv, (((1,), (0,)), ((), ())), preferred_element_type=F32) * r
        lane = lax.broadcasted_iota(jnp.int32, o.shape, 1)
        o_ref[0] = jnp.where(lane >= NOPE, o, 0.0).astype(o_ref.dtype)

    return pl.pallas_call(
        body, name="attn_fwd", grid=(nb, N_HEADS, S // tq),
        in_specs=[pl.BlockSpec((1, tq, HEAD_BLOCK), lambda b, h, i: (b, i, h)),
                  pl.BlockSpec((1, T, HEAD_BLOCK), lambda b, h, i: (b, 0, h)),
                  pl.BlockSpec((1, T, HEAD_BLOCK), lambda b, h, i: (b, 0, 0))],
        out_specs=pl.BlockSpec((1, tq, HEAD_BLOCK), lambda b, h, i: (b, i, h)),
        out_shape=jax.ShapeDtypeStruct((nb, S, QP), BF16),
        compiler_params=_cparams("arbitrary", "arbitrary", "arbitrary"),
    )(q, kv, kr)


def attn_bwd(q, kv, kr, do, tq):
    nb, S, _ = q.shape
    T = kv.shape[1]

    def body(q_ref, kv_ref, kr_ref, do_ref, dq_ref, dkv_ref, dkr_ref):
        h, i = pl.program_id(1), pl.program_id(2)
        qv, kvv, dov = q_ref[0], kv_ref[0], do_ref[0]
        kc = _key_block(kvv, kr_ref[0])
        e, r = _attn_weights(qv, kc)
        dor = (dov.astype(F32) * r).astype(BF16)
        dpr = lax.dot_general(dor, kvv, (((1,), (1,)), ((), ())), preferred_element_type=F32)
        ds = (e * (dpr - r * jnp.sum(dpr * e, axis=1, keepdims=True))).astype(BF16)
        dq = lax.dot_general(ds, kc, (((1,), (0,)), ((), ())), preferred_element_type=F32)
        dq_ref[0] = (dq * ATTN_SCALE).astype(dq_ref.dtype)
        dkc = lax.dot_general(ds, qv, (((0,), (0,)), ((), ())), preferred_element_type=F32) * math.log(2.0)
        dv = lax.dot_general(e.astype(BF16), dor, (((0,), (0,)), ((), ())), preferred_element_type=F32)
        lane = lax.broadcasted_iota(jnp.int32, dkc.shape, 1)
        dkv = jnp.where(lane < NOPE, dkc, dv)
        dkr = jnp.where(lane >= NOPE, dkc, 0.0)

        @pl.when(i == 0)
        def _():
            dkv_ref[0] = dkv

        @pl.when(i > 0)
        def _():
            dkv_ref[0] += dkv

        first = jnp.logical_and(h == 0, i == 0)

        @pl.when(first)
        def _():
            dkr_ref[0] = dkr

        @pl.when(jnp.logical_not(first))
        def _():
            dkr_ref[0] += dkr

    qspec = pl.BlockSpec((1, tq, HEAD_BLOCK), lambda b, h, i: (b, i, h))
    kspec = pl.BlockSpec((1, T, HEAD_BLOCK), lambda b, h, i: (b, 0, h))
    rspec = pl.BlockSpec((1, T, HEAD_BLOCK), lambda b, h, i: (b, 0, 0))
    return pl.pallas_call(
        body, name="attn_bwd", grid=(nb, N_HEADS, S // tq),
        in_specs=[qspec, kspec, rspec, qspec], out_specs=[qspec, kspec, rspec],
        out_shape=[jax.ShapeDtypeStruct((nb, S, QP), F32), jax.ShapeDtypeStruct((nb, T, QP), F32),
                   jax.ShapeDtypeStruct((nb, T, HEAD_BLOCK), F32)],
        compiler_params=_cparams("arbitrary", "arbitrary", "arbitrary"),
    )(q, kv, kr, do)


def _seg_bounds(n, n_ctx):
    t = lax.broadcasted_iota(jnp.int32, (n, 1), 0)
    if n_ctx == 0:
        return t, jnp.zeros_like(t), jnp.full_like(t, n)
    in_ctx = t < n_ctx
    return t, jnp.where(in_ctx, 0, n_ctx), jnp.where(in_ctx, n_ctx, n)


def _shift_rows(x, o, bounds):
    if o == 0:
        return x
    t, lo, hi = bounds
    n = x.shape[0]
    valid = jnp.logical_and(t + o >= lo, t + o < hi).astype(F32)
    return pltpu.roll(x, (-o) % n, 0) * valid


def _conv(x, w, bias, k, bounds):
    acc = bias
    for o in range(k):
        acc = acc + w[o:o + 1, :] * _shift_rows(x, o - k // 2, bounds)
    return acc


def _conv_bwd(x, w, dpre, k, bounds):
    dx = jnp.zeros_like(x)
    rows = []
    for o in range(k):
        dx = dx + w[o:o + 1, :] * _shift_rows(dpre, -(o - k // 2), bounds)
        rows.append(jnp.sum(dpre * _shift_rows(x, o - k // 2, bounds), axis=0, keepdims=True))
    rows.append(jnp.sum(dpre, axis=0, keepdims=True))
    sub8 = lax.broadcasted_iota(jnp.int32, (8, x.shape[1]), 0)
    out = jnp.zeros((8, x.shape[1]), F32)
    for o, r in enumerate(rows):
        out = out + jnp.where(sub8 == o, r, 0.0)
    return dx, out


def _gelu(x):
    return 0.5 * x * (1.0 + lax.erf(x * (1.0 / math.sqrt(2.0))))


def _gelu_grad(x):
    return 0.5 * (1.0 + lax.erf(x * (1.0 / math.sqrt(2.0)))) + x * jnp.exp(-0.5 * x * x) * (1.0 / math.sqrt(2.0 * math.pi))


def ssd_conv_fwd(u, w8, bias, n_ctx, tc):
    nb, T, _ = u.shape
    cb0 = OFF_XBC // tc

    def body(x_ref, w_ref, b_ref, o_ref):
        pre = _conv(x_ref[0], w_ref[...], b_ref[...], SSD_K, _seg_bounds(T, n_ctx))
        o_ref[0] = _silu(pre)

    return pl.pallas_call(
        body, name="ssd_conv_fwd", grid=(nb, XBC // tc),
        in_specs=[pl.BlockSpec((1, T, tc), lambda b, j: (b, 0, cb0 + j)),
                  pl.BlockSpec((8, tc), lambda b, j: (0, j)), pl.BlockSpec((1, tc), lambda b, j: (0, j))],
        out_specs=pl.BlockSpec((1, T, tc), lambda b, j: (b, 0, j)),
        out_shape=jax.ShapeDtypeStruct((nb, T, XBC), F32),
        compiler_params=_cparams("arbitrary", "arbitrary"),
    )(u, w8, bias)


def ssd_conv_bwd(u, w8, bias, dxbc, dxs_direct, n_ctx, tc):
    nb, T, _ = u.shape
    cb0 = OFF_XBC // tc
    n_direct = D_INNER // tc

    def body(x_ref, w_ref, b_ref, d0_ref, d1_ref, dd_ref, dx_ref, dw_ref, acc):
        j, b = pl.program_id(0), pl.program_id(1)
        acc[...] = d0_ref[0, 0] + d1_ref[0, 0]

        @pl.when(j < n_direct)
        def _():
            acc[n_ctx:, :] += dd_ref[0]

        bounds = _seg_bounds(T, n_ctx)
        x, w = x_ref[0], w_ref[...]
        pre = _conv(x, w, b_ref[...], SSD_K, bounds)
        sg = jax.nn.sigmoid(pre)
        dpre = acc[...] * (sg * (1.0 + pre * (1.0 - sg)))
        dx, rows = _conv_bwd(x, w, dpre, SSD_K, bounds)
        dx_ref[0] = dx.astype(dx_ref.dtype)

        @pl.when(b == 0)
        def _():
            dw_ref[...] = rows

        @pl.when(b > 0)
        def _():
            dw_ref[...] += rows

    dspec0 = pl.BlockSpec((1, 1, T, tc), lambda j, b: (0, b, 0, j))
    dspec1 = pl.BlockSpec((1, 1, T, tc), lambda j, b: (1, b, 0, j))
    return pl.pallas_call(
        body, name="ssd_conv_bwd", grid=(XBC // tc, nb),
        in_specs=[pl.BlockSpec((1, T, tc), lambda j, b: (b, 0, cb0 + j)),
                  pl.BlockSpec((8, tc), lambda j, b: (0, j)), pl.BlockSpec((1, tc), lambda j, b: (0, j)),
                  dspec0, dspec1,
                  pl.BlockSpec((1, T - n_ctx, tc), lambda j, b: (b, 0, jnp.minimum(j, n_direct - 1)))],
        out_specs=[pl.BlockSpec((1, T, tc), lambda j, b: (b, 0, j)), pl.BlockSpec((8, tc), lambda j, b: (0, j))],
        out_shape=[jax.ShapeDtypeStruct((nb, T, XBC), BF16), jax.ShapeDtypeStruct((8, XBC), F32)],
        scratch_shapes=[pltpu.VMEM((T, tc), F32)],
        compiler_params=_cparams("arbitrary", "arbitrary"),
    )(u, w8, bias, dxbc, dxbc, dxs_direct)


GLU_TC = 256


def glu_interleave(w_up):
    blocks = []
    for j in range(D_FF // GLU_TC):
        blocks += [w_up[:, j * GLU_TC:(j + 1) * GLU_TC], w_up[:, D_FF + j * GLU_TC:D_FF + (j + 1) * GLU_TC]]
    return jnp.concatenate(blocks, axis=1)


def glu_deinterleave(g):
    nj = D_FF // GLU_TC
    gate = [g[:, 2 * j * GLU_TC:(2 * j + 1) * GLU_TC] for j in range(nj)]
    val = [g[:, (2 * j + 1) * GLU_TC:(2 * j + 2) * GLU_TC] for j in range(nj)]
    return jnp.concatenate(gate + val, axis=1)


def glu_fwd(up, w8, bias):
    nb, S, _ = up.shape
    tc = GLU_TC

    def body(u_ref, w_ref, b_ref, o_ref):
        gc = _conv(u_ref[0, :, :tc], w_ref[...], b_ref[...], FFN_K, _seg_bounds(S, 0))
        o_ref[0] = (_gelu(gc) * u_ref[0, :, tc:]).astype(o_ref.dtype)

    return pl.pallas_call(
        body, name="glu_fwd", grid=(nb, D_FF // tc),
        in_specs=[pl.BlockSpec((1, S, 2 * tc), lambda b, j: (b, 0, j)),
                  pl.BlockSpec((8, tc), lambda b, j: (0, j)), pl.BlockSpec((1, tc), lambda b, j: (0, j))],
        out_specs=pl.BlockSpec((1, S, tc), lambda b, j: (b, 0, j)),
        out_shape=jax.ShapeDtypeStruct((nb, S, D_FF), BF16),
        compiler_params=_cparams("arbitrary", "arbitrary"),
    )(up, w8, bias)


def glu_bwd(up, w8, bias, dact):
    nb, S, _ = up.shape
    tc = GLU_TC

    def body(u_ref, w_ref, b_ref, d_ref, du_ref, dw_ref):
        b = pl.program_id(1)
        bounds = _seg_bounds(S, 0)
        x, w, val, d = u_ref[0, :, :tc], w_ref[...], u_ref[0, :, tc:], d_ref[0].astype(F32)
        gc = _conv(x, w, b_ref[...], FFN_K, bounds)
        du_ref[0, :, tc:] = (d * _gelu(gc)).astype(du_ref.dtype)
        dx, rows = _conv_bwd(x, w, d * val * _gelu_grad(gc), FFN_K, bounds)
        du_ref[0, :, :tc] = dx.astype(du_ref.dtype)

        @pl.when(b == 0)
        def _():
            dw_ref[...] = rows

        @pl.when(b > 0)
        def _():
            dw_ref[...] += rows

    pair = pl.BlockSpec((1, S, 2 * tc), lambda j, b: (b, 0, j))
    return pl.pallas_call(
        body, name="glu_bwd", grid=(D_FF // tc, nb),
        in_specs=[pair, pl.BlockSpec((8, tc), lambda j, b: (0, j)), pl.BlockSpec((1, tc), lambda j, b: (0, j)),
                  pl.BlockSpec((1, S, tc), lambda j, b: (b, 0, j))],
        out_specs=[pair, pl.BlockSpec((8, tc), lambda j, b: (0, j))],
        out_shape=[jax.ShapeDtypeStruct((nb, S, 2 * D_FF), BF16), jax.ShapeDtypeStruct((8, D_FF), F32)],
        compiler_params=_cparams("arbitrary", "arbitrary"),
    )(up, w8, bias, dact)


def _chunk_of(d, k, n_cc, n_ch):
    rev = jnp.where(k < n_cc, n_cc - 1 - k, n_cc + n_ch - 1 - k)
    return jnp.where(d == 1, rev, k)


def _lane_pick(v, lane_iota, l):
    return jnp.sum(jnp.where(lane_iota == l, v, 0.0), axis=1, keepdims=True)


def _row_pick(v, sub_iota, l):
    return jnp.sum(jnp.where(sub_iota == l, v, 0.0), axis=0, keepdims=True)


def _softplus(x):
    return jnp.maximum(x, 0.0) + jnp.log(1.0 + jnp.exp(-jnp.abs(x)))


def _ssd_common(d, dt_raw, alog, dtb):
    Q = dt_raw.shape[0]
    row = lax.broadcasted_iota(jnp.int32, (Q, Q), 0)
    col = lax.broadcasted_iota(jnp.int32, (Q, Q), 1)
    rev = d == 1
    maskb = jnp.where(rev, row, col) <= jnp.where(rev, col, row)
    tri = maskb.astype(F32)
    A = -jnp.exp(alog)
    dtv = _softplus(dt_raw + dtb)
    a = dtv * A
    cum = lax.dot_general(tri, a, (((1,), (0,)), ((), ())), precision=lax.Precision.HIGHEST, preferred_element_type=F32)
    tot = jnp.sum(a, axis=0, keepdims=True)
    return maskb, tri, A, dtv, cum, tot


def ssd_fwd(xbc, u, alog, dtb, n_ctx):
    nb, T, _ = xbc.shape
    S = T - n_ctx
    n_ch, n_cc = T // CHUNK, n_ctx // CHUNK
    dt_cb = OFF_DT // LANE
    Q = CHUNK
    n_pairs = SSD_HEADS // 2

    def body(x_ref, dt_ref, al_ref, db_ref, y_ref, hin_ref, H):
        d, k = pl.program_id(1), pl.program_id(2)

        @pl.when(k == 0)
        def _():
            H[...] = jnp.zeros_like(H)

        maskb, tri, A, dtv, cum, tot = _ssd_common(d, dt_ref[0], al_ref[...], db_ref[...])
        cumT = cum.T
        hin_ref[0, 0, 0] = H[...].astype(BF16)
        lane = lax.broadcasted_iota(jnp.int32, (Q, LANE), 1)
        lane1 = lax.broadcasted_iota(jnp.int32, (1, LANE), 1)
        sub = lax.broadcasted_iota(jnp.int32, (LANE, Q), 0)
        subc = lax.broadcasted_iota(jnp.int32, (LANE, 1), 0)
        half = lane < SSD_P
        for g in range(SSD_GROUPS):
            Bg = x_ref[0, :, D_INNER + g * SSD_N:D_INNER + (g + 1) * SSD_N].astype(BF16)
            Cg = x_ref[0, :, D_INNER + GN + g * SSD_N:D_INNER + GN + (g + 1) * SSD_N].astype(BF16)
            Gm = lax.dot_general(Cg, Bg, (((1,), (1,)), ((), ())), preferred_element_type=F32)
            for pr in range(n_pairs // SSD_GROUPS):
                p = g * (n_pairs // SSD_GROUPS) + pr
                l0 = d * SSD_HEADS + 2 * p
                l1 = l0 + 1
                s0c, s1c = _lane_pick(cum, lane, l0), _lane_pick(cum, lane, l1)
                s0r, s1r = _row_pick(cumT, sub, l0), _row_pick(cumT, sub, l1)
                dtp = jnp.where(half, _lane_pick(dtv, lane, l0), _lane_pick(dtv, lane, l1))
                tot0, tot1 = _lane_pick(tot, lane1, l0), _lane_pick(tot, lane1, l1)
                sc = jnp.where(half, s0c, s1c)
                totp = jnp.where(half, tot0, tot1)
                M0 = (Gm * jnp.exp(jnp.where(maskb, s0c - s0r, NEG_BIG))).astype(BF16)
                M1 = (Gm * jnp.exp(jnp.where(maskb, s1c - s1r, NEG_BIG))).astype(BF16)
                xd = x_ref[0, :, p * LANE:(p + 1) * LANE] * dtp
                xdb = xd.astype(BF16)
                yd = jnp.where(half,
                               lax.dot_general(M0, xdb, (((1,), (0,)), ((), ())), preferred_element_type=F32),
                               lax.dot_general(M1, xdb, (((1,), (0,)), ((), ())), preferred_element_type=F32))
                Hp = H[p * LANE:(p + 1) * LANE, :]
                yo = lax.dot_general(Cg, Hp.astype(BF16), (((1,), (1,)), ((), ())), preferred_element_type=F32) * jnp.exp(sc)

                @pl.when(k >= n_cc)
                def _():
                    y_ref[0, 0, :, p * LANE:(p + 1) * LANE] = yd + yo

                xdw = (xd * jnp.exp(totp - sc)).astype(BF16)
                etot = jnp.exp(jnp.where(subc < SSD_P, tot0, tot1))
                H[p * LANE:(p + 1) * LANE, :] = Hp * etot + lax.dot_general(
                    xdw, Bg, (((0,), (0,)), ((), ())), preferred_element_type=F32)

    def ymap(b, d, k):
        return (d, b, _chunk_of(d, jnp.maximum(k, n_cc), n_cc, n_ch) - n_cc, 0)

    return pl.pallas_call(
        body, name="ssd_fwd", grid=(nb, 2, n_ch),
        in_specs=[pl.BlockSpec((1, Q, XBC), lambda b, d, k: (b, _chunk_of(d, k, n_cc, n_ch), 0)),
                  pl.BlockSpec((1, Q, LANE), lambda b, d, k: (b, _chunk_of(d, k, n_cc, n_ch), dt_cb)),
                  pl.BlockSpec((1, LANE), lambda b, d, k: (0, 0)), pl.BlockSpec((1, LANE), lambda b, d, k: (0, 0))],
        out_specs=[pl.BlockSpec((1, 1, Q, D_INNER), ymap),
                   pl.BlockSpec((1, 1, 1, D_INNER, SSD_N), lambda b, d, k: (d, b, k, 0, 0))],
        out_shape=[jax.ShapeDtypeStruct((2, nb, S, D_INNER), F32),
                   jax.ShapeDtypeStruct((2, nb, n_ch, D_INNER, SSD_N), BF16)],
        scratch_shapes=[pltpu.VMEM((D_INNER, SSD_N), F32)],
        compiler_params=_cparams("arbitrary", "arbitrary", "arbitrary"),
    )(xbc, u, alog, dtb)


def ssd_bwd(xbc, u, alog, dtb, hin, dy, n_ctx, exchange):
    nb, T, _ = xbc.shape
    n_ex = len(exchange)
    n_ch, n_cc = T // CHUNK, n_ctx // CHUNK
    dt_cb = OFF_DT // LANE
    Q = CHUNK
    n_pairs = SSD_HEADS // 2
    NT = (((1,), (1,)), ((), ()))
    NN = (((1,), (0,)), ((), ()))
    TN = (((0,), (0,)), ((), ()))

    def dot(a, b, dims):
        return lax.dot_general(a.astype(BF16), b.astype(BF16), dims, preferred_element_type=F32)

    def body(*refs):
        x_ref, dt_ref, al_ref, db_ref, hin_ref, dy_ref = refs[:6]
        send_refs = refs[6:6 + n_ex]
        dx_ref, ddt_ref, st_ref = refs[6 + n_ex:9 + n_ex]
        recv_refs = refs[9 + n_ex:9 + 2 * n_ex]
        dH, *sems = refs[9 + 2 * n_ex:]
        d, kk = pl.program_id(1), pl.program_id(2)
        ks = n_ch - 1 - kk
        first_step = jnp.logical_and(jnp.logical_and(pl.program_id(0) == 0, d == 0), kk == 0)
        last_step = jnp.logical_and(jnp.logical_and(pl.program_id(0) == nb - 1, d == 1), kk == n_ch - 1)

        @pl.when(first_step)
        def _():
            if n_ex:
                _a2a_start(*_a2a_copies(send_refs, recv_refs, *sems))

        @pl.when(kk == 0)
        def _():
            dH[...] = jnp.zeros_like(dH)

        @pl.when(jnp.logical_and(jnp.logical_and(pl.program_id(0) == 0, d == 0), kk == 0))
        def _():
            st_ref[...] = jnp.zeros_like(st_ref)

        dt_raw = dt_ref[0]
        alog, dtb_v = al_ref[...], db_ref[...]
        maskb, tri, A, dtv, cum, tot = _ssd_common(d, dt_raw, alog, dtb_v)
        cumT = cum.T
        live = (ks >= n_cc).astype(F32)
        lane = lax.broadcasted_iota(jnp.int32, (Q, LANE), 1)
        lane1 = lax.broadcasted_iota(jnp.int32, (1, LANE), 1)
        sub = lax.broadcasted_iota(jnp.int32, (LANE, Q), 0)
        subc = lax.broadcasted_iota(jnp.int32, (LANE, 1), 0)
        half = lane < SSD_P
        halfc = subc < SSD_P
        dcum = jnp.zeros((Q, LANE), F32)
        dcumT = jnp.zeros((LANE, Q), F32)
        ddt = jnp.zeros((Q, LANE), F32)
        dtot = jnp.zeros((1, LANE), F32)
        for g in range(SSD_GROUPS):
            Bg = x_ref[0, :, D_INNER + g * SSD_N:D_INNER + (g + 1) * SSD_N].astype(BF16)
            Cg = x_ref[0, :, D_INNER + GN + g * SSD_N:D_INNER + GN + (g + 1) * SSD_N].astype(BF16)
            Gm = lax.dot_general(Cg, Bg, NT, preferred_element_type=F32)
            dG = jnp.zeros((Q, Q), F32)
            dC = jnp.zeros((Q, SSD_N), F32)
            dB = jnp.zeros((Q, SSD_N), F32)
            for pr in range(n_pairs // SSD_GROUPS):
                p = g * (n_pairs // SSD_GROUPS) + pr
                l0 = d * SSD_HEADS + 2 * p
                l1 = l0 + 1
                s0c, s1c = _lane_pick(cum, lane, l0), _lane_pick(cum, lane, l1)
                s0r, s1r = _row_pick(cumT, sub, l0), _row_pick(cumT, sub, l1)
                dtp = jnp.where(half, _lane_pick(dtv, lane, l0), _lane_pick(dtv, lane, l1))
                tot0, tot1 = _lane_pick(tot, lane1, l0), _lane_pick(tot, lane1, l1)
                sc = jnp.where(half, s0c, s1c)
                totp = jnp.where(half, tot0, tot1)
                L0 = jnp.exp(jnp.where(maskb, s0c - s0r, NEG_BIG))
                L1 = jnp.exp(jnp.where(maskb, s1c - s1r, NEG_BIG))
                M0, M1 = Gm * L0, Gm * L1
                xs = x_ref[0, :, p * LANE:(p + 1) * LANE]
                xd = xs * dtp
                es = jnp.exp(sc)
                dte = jnp.exp(totp - sc)
                etot = jnp.exp(jnp.where(halfc, tot0, tot1))
                dyp = dy_ref[0, :, p * LANE:(p + 1) * LANE] * live
                Hp = hin_ref[0, 0, 0, p * LANE:(p + 1) * LANE, :]
                dHp = dH[p * LANE:(p + 1) * LANE, :]
                bdh = dot(Bg, dHp, NT)
                dxd = jnp.where(half, dot(M0, dyp, TN), dot(M1, dyp, TN)) + bdh * dte
                dy0 = jnp.where(half, dyp, 0.0)
                dy1 = dyp - dy0
                dM0, dM1 = dot(dy0, xd, NT), dot(dy1, xd, NT)
                dG = dG + dM0 * L0 + dM1 * L1
                dyes = dyp * es
                xdw = xd * dte
                dC = dC + dot(dyes, Hp, NN)
                dB = dB + dot(xdw, dHp, NN)
                W0, W1 = dM0 * M0, dM1 * M1
                yoff = dot(Cg, Hp, NT) * es
                r_off = dyp * yoff
                r_st = xd * bdh * dte
                hh = jnp.sum(dHp * Hp.astype(F32), axis=1, keepdims=True) * etot
                for (l, W, hsel, hselc) in ((l0, W0, half, halfc),
                                            (l1, W1, jnp.logical_not(half), jnp.logical_not(halfc))):
                    col_g = (jnp.sum(W, axis=1, keepdims=True)
                             + jnp.sum(jnp.where(hsel, r_off - r_st, 0.0), axis=1, keepdims=True))
                    row_g = -jnp.sum(W, axis=0, keepdims=True)
                    tot_g = (jnp.sum(jnp.sum(jnp.where(hsel, r_st, 0.0), axis=1, keepdims=True), axis=0, keepdims=True)
                             + jnp.sum(jnp.where(hselc, hh, 0.0), axis=0, keepdims=True))
                    dcum = dcum + jnp.where(lane == l, col_g, 0.0)
                    dcumT = dcumT + jnp.where(sub == l, row_g, 0.0)
                    dtot = dtot + jnp.where(lane1 == l, tot_g, 0.0)
                    ddt = ddt + jnp.where(lane == l, jnp.sum(jnp.where(hsel, dxd * xs, 0.0), axis=1, keepdims=True), 0.0)
                dx_ref[0, 0, :, p * LANE:(p + 1) * LANE] = dxd * dtp
                dH[p * LANE:(p + 1) * LANE, :] = dHp * etot + dot(dyes, Cg, TN)
            dx_ref[0, 0, :, D_INNER + g * SSD_N:D_INNER + (g + 1) * SSD_N] = dB + dot(dG, Cg, TN)
            dx_ref[0, 0, :, D_INNER + GN + g * SSD_N:D_INNER + GN + (g + 1) * SSD_N] = dC + dot(dG, Bg, NN)
        dcum_all = dcum + dcumT.T
        da = lax.dot_general(tri, dcum_all, TN, precision=lax.Precision.HIGHEST, preferred_element_type=F32) + dtot
        ddtv = ddt + da * A
        ddt_raw = ddtv * jax.nn.sigmoid(dt_raw + dtb_v)
        ddt_ref[0, 0] = ddt_raw
        st_ref[0:1, :] += jnp.sum(da * dtv * A, axis=0, keepdims=True)
        st_ref[1:2, :] += jnp.sum(ddt_raw, axis=0, keepdims=True)

        @pl.when(last_step)
        def _():
            if n_ex:
                _a2a_wait(*_a2a_copies(send_refs, recv_refs, *sems))

    hbm = pl.BlockSpec(memory_space=pl.ANY)

    def cmap(d, kk):
        return _chunk_of(d, n_ch - 1 - kk, n_cc, n_ch)

    def dymap(b, d, kk):
        return (b, _chunk_of(d, jnp.maximum(n_ch - 1 - kk, n_cc), n_cc, n_ch) - n_cc, 0)

    return pl.pallas_call(
        body, name="ssd_bwd", grid=(nb, 2, n_ch),
        in_specs=[pl.BlockSpec((1, Q, XBC), lambda b, d, kk: (b, cmap(d, kk), 0)),
                  pl.BlockSpec((1, Q, LANE), lambda b, d, kk: (b, cmap(d, kk), dt_cb)),
                  pl.BlockSpec((1, LANE), lambda b, d, kk: (0, 0)), pl.BlockSpec((1, LANE), lambda b, d, kk: (0, 0)),
                  pl.BlockSpec((1, 1, 1, D_INNER, SSD_N), lambda b, d, kk: (d, b, n_ch - 1 - kk, 0, 0)),
                  pl.BlockSpec((1, Q, D_INNER), dymap)] + [hbm] * n_ex,
        out_specs=[pl.BlockSpec((1, 1, Q, XBC), lambda b, d, kk: (d, b, cmap(d, kk), 0)),
                   pl.BlockSpec((1, 1, Q, LANE), lambda b, d, kk: (d, b, cmap(d, kk), 0)),
                   pl.BlockSpec((8, LANE), lambda b, d, kk: (0, 0))] + [hbm] * n_ex,
        out_shape=[jax.ShapeDtypeStruct((2, nb, T, XBC), F32), jax.ShapeDtypeStruct((2, nb, T, LANE), F32),
                   jax.ShapeDtypeStruct((8, LANE), F32)] + [jax.ShapeDtypeStruct(v.shape, v.dtype) for v in exchange],
        scratch_shapes=[pltpu.VMEM((D_INNER, SSD_N), F32)] + (_a2a_scratch(n_ex) if n_ex else []),
        compiler_params=_cparams("arbitrary", "arbitrary", "arbitrary"),
    )(xbc, u, alog, dtb, hin, dy, *exchange)


def _adamw(w, g, m, v):
    mn = ADAM_B1 * m + (1.0 - ADAM_B1) * g
    vn = ADAM_B2 * v + (1.0 - ADAM_B2) * jnp.square(g)
    m_hat = mn / (1.0 - ADAM_B1 ** ADAM_STEP)
    v_hat = vn / (1.0 - ADAM_B2 ** ADAM_STEP)
    return -ADAM_LR * (m_hat / (jnp.sqrt(v_hat) + ADAM_EPS) + ADAM_WD * w), mn, vn


def adamw_matrix(name, w, g_slots, m, v):
    K, n = w.shape
    s = g_slots.shape[0]
    tr = _tile(K, 256, 8)

    def body(w_ref, g_ref, m_ref, v_ref, go_ref, d_ref, mo_ref, vo_ref):
        g = g_ref[0]
        for j in range(1, s):
            g = g + g_ref[j]
        go_ref[...] = g
        d_ref[...], mo_ref[...], vo_ref[...] = _adamw(w_ref[...], g, m_ref[...], v_ref[...])

    spec = pl.BlockSpec((tr, n), lambda i: (i, 0))
    return pl.pallas_call(
        body, name=name, grid=(K // tr,),
        in_specs=[spec, pl.BlockSpec((s, tr, n), lambda i: (0, i, 0)), spec, spec], out_specs=[spec] * 4,
        out_shape=[jax.ShapeDtypeStruct((K, n), F32)] * 4,
        compiler_params=_cparams("arbitrary"),
    )(w, g_slots, m, v)


def adamw_small(ws, gs, ms, vs):
    n = len(ws)

    def body(*refs):
        for i in range(n):
            d, mn, vn = _adamw(refs[i][...], refs[n + i][...], refs[2 * n + i][...], refs[3 * n + i][...])
            refs[4 * n + i][...] = d
            refs[5 * n + i][...] = mn
            refs[6 * n + i][...] = vn

    shapes = [jax.ShapeDtypeStruct(w.shape, F32) for w in ws]
    out = pl.pallas_call(body, name="adamw_small", out_shape=shapes * 3)(*ws, *gs, *ms, *vs)
    return out[:n], out[n:2 * n], out[2 * n:]


def sum_slots(name, x):
    n = x.shape[0]

    def fn(t):
        acc = t[0]
        for j in range(1, n):
            acc = acc + t[j]
        return (acc,)

    return ew_call(name, fn, [x], [(x.shape[1:], F32)])[0]


def _pack_rows(parts):
    rows = []
    for p in parts:
        flat = p.reshape(1, -1)
        n = flat.shape[1]
        rows.append(jnp.pad(flat, ((0, 0), (0, -(-n // (8 * LANE)) * 8 * LANE - n))).reshape(-1, LANE))
    return jnp.concatenate(rows, axis=0)


def _unpack_rows(pack, shapes):
    out, r = [], 0
    for s in shapes:
        n = int(np.prod(s))
        nr = -(-n // (8 * LANE)) * 8
        out.append(pack[r:r + nr].reshape(1, -1)[:, :n].reshape(s))
        r += nr
    return out


def _mesh_pos():
    return lax.axis_index("x"), lax.axis_index("y"), lax.axis_index("c")


N_PEERS = N_DEV - 1


def all_gather(name, vs):
    n = len(vs)

    def body(*refs):
        x_refs, out_refs = refs[:n], refs[n:2 * n]
        send_sems, recv_sems, local_sems = refs[2 * n:]
        x, y, c = _mesh_pos()
        me, sibling = (x, y, c), (x, y, 1 - c)
        chips = [(1 - x, y), (x, 1 - y), (1 - x, 1 - y)]

        def slot(a, px, py, pc):
            return out_refs[a].at[4 * px + 2 * py + pc]

        def copy(a, k, block, to, src=None):
            return pltpu.make_async_remote_copy(
                src_ref=slot(a, *block) if src is None else src, dst_ref=slot(a, *block),
                send_sem=send_sems.at[N_PEERS * a + k], recv_sem=recv_sems.at[N_PEERS * a + k],
                device_id=to, device_id_type=MESH)

        mine = [pltpu.make_async_copy(x_refs[a], slot(a, *me), local_sems.at[a]) for a in range(n)]
        for cp in mine:
            cp.start()
        first = []
        for a in range(n):
            first.append(copy(a, 0, me, sibling, src=x_refs[a]))
            first += [copy(a, 1 + j, me, (*chip, c), src=x_refs[a]) for j, chip in enumerate(chips)]
        for cp in first:
            cp.start()
        passed = []
        for j, chip in enumerate(chips):
            for a in range(n):
                copy(a, 1 + j, (*chip, c), me).wait_recv()
                passed.append(copy(a, 4 + j, (*chip, c), sibling))
                passed[-1].start()
        for a in range(n):
            copy(a, 0, sibling, me).wait_recv()
            for j, chip in enumerate(chips):
                copy(a, 4 + j, (*chip, 1 - c), me).wait_recv()
        for cp in first + passed:
            cp.wait_send()
        for cp in mine:
            cp.wait()

    hbm = pl.BlockSpec(memory_space=pl.ANY)
    return pl.pallas_call(
        body, name=name, out_shape=[jax.ShapeDtypeStruct((N_DEV,) + v.shape, v.dtype) for v in vs],
        in_specs=[hbm] * n, out_specs=[hbm] * n,
        scratch_shapes=[pltpu.SemaphoreType.DMA((N_PEERS * n,)), pltpu.SemaphoreType.DMA((N_PEERS * n,)),
                        pltpu.SemaphoreType.DMA((n,))],
    )(*vs)


def _a2a_scratch(n):
    return [pltpu.SemaphoreType.DMA((N_PEERS * n,)), pltpu.SemaphoreType.DMA((N_PEERS * n,)), pltpu.SemaphoreType.DMA((n,))]


def _a2a_copies(x_refs, out_refs, send_sems, recv_sems, local_sems):
    n = len(x_refs)
    x, y, c = _mesh_pos()
    me = 4 * x + 2 * y + c
    local = [pltpu.make_async_copy(x_refs[a].at[me], out_refs[a].at[me], local_sems.at[a]) for a in range(n)]
    remote = []
    for k in range(1, N_DEV):
        px, py, pc = x ^ ((k >> 2) & 1), y ^ ((k >> 1) & 1), c ^ (k & 1)
        for a in range(n):
            remote.append(pltpu.make_async_remote_copy(
                src_ref=x_refs[a].at[4 * px + 2 * py + pc], dst_ref=out_refs[a].at[me],
                send_sem=send_sems.at[N_PEERS * a + k - 1], recv_sem=recv_sems.at[N_PEERS * a + k - 1],
                device_id=(px, py, pc), device_id_type=MESH))
    return local, remote


def _a2a_start(local, remote):
    for cp in local + remote:
        cp.start()


def _a2a_wait(local, remote):
    for cp in remote:
        cp.wait_recv()
    for cp in remote:
        cp.wait_send()
    for cp in local:
        cp.wait()


def all_to_all(name, vs):
    n = len(vs)

    def body(*refs):
        copies = _a2a_copies(refs[:n], refs[n:2 * n], *refs[2 * n:])
        _a2a_start(*copies)
        _a2a_wait(*copies)

    hbm = pl.BlockSpec(memory_space=pl.ANY)
    return pl.pallas_call(
        body, name=name, out_shape=[jax.ShapeDtypeStruct(v.shape, v.dtype) for v in vs],
        in_specs=[hbm] * n, out_specs=[hbm] * n, scratch_shapes=_a2a_scratch(n),
    )(*vs)


def _taps8(w):
    return jnp.concatenate([w, jnp.zeros((8 - w.shape[0], w.shape[1]), w.dtype)], axis=0)


def _lanes128(v):
    v = v.reshape(1, -1)
    return jnp.pad(v, ((0, 0), (0, LANE - v.shape[1])))


def weights_to_internal(w_in, w_q_up, w_kv_up, w_out, w_up, w_down):
    cq, ckv, kr, z, xbc, dt = jnp.split(w_in, np.cumsum(IN_SPLITS)[:-1].tolist(), axis=1)
    K = w_in.shape[0]

    def zeros(n):
        return jnp.zeros((K, n), w_in.dtype)

    w_in_p = jnp.concatenate([cq, zeros(KR_LANE), kr, zeros(LANE - KR_LANE - ROPE), ckv, zeros(OFF_Z - OFF_CKV - KV_RANK),
                              z, xbc, dt, zeros(WIN_P - OFF_DT - 2 * SSD_HEADS)], axis=1)
    w_q_p = jnp.pad(w_q_up.reshape(Q_RANK, N_HEADS, NOPE + ROPE), ((0, 0), (0, 0), (0, HEAD_BLOCK - NOPE - ROPE))).reshape(Q_RANK, QP)
    attn_rows = w_out[:N_HEADS * V_DIM].reshape(N_HEADS, V_DIM, -1)
    w_out_p = jnp.concatenate([jnp.pad(attn_rows, ((0, 0), (HEAD_BLOCK - V_DIM, 0), (0, 0))).reshape(QP, -1),
                               w_out[N_HEADS * V_DIM:]], axis=0)
    return dict(w_in_p=w_in_p, w_q_p=w_q_p, w_kv=w_kv_up, w_out_p=w_out_p, w_up=glu_interleave(w_up), w_down=w_down)


def _in_grad(g_in_p):
    return jnp.concatenate([g_in_p[:, OFF_CQ:OFF_CQ + Q_RANK], g_in_p[:, OFF_CKV:OFF_CKV + KV_RANK],
                            g_in_p[:, OFF_KR + KR_LANE:OFF_KR + KR_LANE + ROPE], g_in_p[:, OFF_Z:OFF_Z + D_INNER],
                            g_in_p[:, OFF_XBC:OFF_XBC + XBC], g_in_p[:, OFF_DT:OFF_DT + 2 * SSD_HEADS]], axis=1)


def _q_grad(g_q_p):
    return g_q_p.reshape(Q_RANK, N_HEADS, HEAD_BLOCK)[:, :, :NOPE + ROPE].reshape(Q_RANK, -1)


def _out_grad(g_out_p):
    return jnp.concatenate([g_out_p[:QP].reshape(N_HEADS, HEAD_BLOCK, -1)[:, HEAD_BLOCK - V_DIM:].reshape(N_HEADS * V_DIM, -1),
                            g_out_p[QP:]], axis=0)


def grads_from_internal(g_in_p, g_q_p, g_kv, g_out_p, g_up, g_down):
    return _in_grad(g_in_p), _q_grad(g_q_p), g_kv, _out_grad(g_out_p), glu_deinterleave(g_up), g_down


EARLY = ("w_out", "w_up", "w_down")


def local_step(x, ctx, target, mod_x, mod_c, W, V):
    nb, S, D = x.shape
    C = ctx.shape[1]
    T = C + S
    tr = _tile(math.gcd(C, S), 256, 8)
    tq = _tile(S, 256, 8)
    tc = 256
    cblk = C // tr
    m = [mod_x[:, i * D:(i + 1) * D][:, None, :] for i in range(N_MOD)]
    mc = [mod_c[:, i * D:(i + 1) * D] for i in range(2)]
    ssd_w8, ffn_w8 = _taps8(V["ssd_conv_w"]), _taps8(V["ffn_conv_w"])
    alog, dtb = _lanes128(V["ssd_a_log"]), _lanes128(V["ssd_dt_bias"])
    dexp = jnp.repeat(V["ssd_d"].reshape(-1), SSD_P).reshape(1, D_INNER)
    cosT, sinT = rope_tables(C, S)
    cosS, sinS = cosT[C:], sinT[C:]

    (h1x,) = rows_fwd("prenorm_x", fn_prenorm, nb, S // tr, tr, [(x, D, 0, 0)], [m[0], m[1]], [V["mix_pre_norm"]], [(D, BF16)])
    (h1c,) = rows_fwd("prenorm_c", fn_prenorm, nb, C // tr, tr, [(ctx, D, 0, 0)], [], [mc[0], mc[1], V["mix_pre_norm"]], [(D, BF16)])
    h1 = jnp.concatenate([h1c, h1x], axis=1).reshape(nb * T, D)
    u = matmul("in_proj", [(h1, W["w_in_p"])], "nn", F32).reshape(nb, T, WIN_P)
    (qn,) = rows_fwd("q_norm", fn_rms, nb, S // tr, tr, [(u, Q_RANK, OFF_CQ // Q_RANK, cblk)], [], [V["q_norm"]], [(Q_RANK, BF16)])
    (kvn,) = rows_fwd("kv_norm", fn_rms, nb, T // tr, tr, [(u, KV_RANK, OFF_CKV // KV_RANK, 0)], [], [V["kv_norm"]], [(KV_RANK, BF16)])
    qn2, kvn2 = qn.reshape(nb * S, Q_RANK), kvn.reshape(nb * T, KV_RANK)
    q_raw = matmul("q_up", [(qn2, W["w_q_p"])], "nn", F32).reshape(nb, S, QP)
    kv = matmul("kv_up", [(kvn2, W["w_kv"])], "nn", BF16).reshape(nb, T, QP)
    q = rope_call("rope_q", q_raw, QP, 0, cosS * Q_PRESCALE, sinS * Q_PRESCALE, BF16, tr)
    kr = rope_call("rope_k", u, LANE, OFF_KR // LANE, cosT, sinT, BF16, tr)
    o = attn_fwd(q, kv, kr, tq)
    xbc = ssd_conv_fwd(u, ssd_w8, V["ssd_conv_b"], C, tc)
    y2, hin = ssd_fwd(xbc, u, alog, dtb, C)
    y2 = y2.reshape(2 * nb, S, D_INNER)
    fin_rows = [(y2, D_INNER, 0, 0, 0), (y2, D_INNER, 0, 0, nb), (xbc, D_INNER, 0, cblk), (u, D_INNER, OFF_Z // D_INNER, cblk)]
    fin_gl = [dexp, V["ssd_norm"]]
    (ssd,) = rows_fwd("ssd_finish", fn_ssd_finish, nb, S // tr, tr, fin_rows, [], fin_gl, [(D_INNER, BF16)])
    o2, ssd2 = o.reshape(nb * S, QP), ssd.reshape(nb * S, D_INNER)
    mix = matmul("out_proj", [(o2, W["w_out_p"][:QP]), (ssd2, W["w_out_p"][QP:])], "nn", F32).reshape(nb, S, D)
    pm_rows = [(x, D, 0, 0), (mix, D, 0, 0)]
    pm_pb = [m[2], m[4], m[3]]
    pm_gl = [V["mix_post_norm"], V["ffn_pre_norm"]]
    x1, h2 = rows_fwd("postmix", fn_postmix, nb, S // tr, tr, pm_rows, pm_pb, pm_gl, [(D, F32), (D, BF16)])
    h22 = h2.reshape(nb * S, D)
    up = matmul("up_proj", [(h22, W["w_up"])], "nn", F32).reshape(nb, S, 2 * D_FF)
    act = glu_fwd(up, ffn_w8, V["ffn_conv_b"])
    act2 = act.reshape(nb * S, D_FF)
    ffn = matmul("down_proj", [(act2, W["w_down"])], "nn", F32).reshape(nb, S, D)
    dx1, dffn, dgate2, d_ffn_post, loss = final_call(x1, ffn, target, m[5], V["ffn_post_norm"], tr)

    dffn2 = dffn.reshape(nb * S, D)
    dact = matmul("down_dgrad", [(dffn2, W["w_down"])], "nt", BF16).reshape(nb, S, D_FF)
    g_down = matmul_tn("down_wgrad", act2, dffn2)
    dup, ffn_rows = glu_bwd(up, ffn_w8, V["ffn_conv_b"], dact)
    dup2 = dup.reshape(nb * S, 2 * D_FF)
    dh2 = matmul("up_dgrad", [(dup2, W["w_up"])], "nt", BF16).reshape(nb, S, D)
    g_up = matmul_tn("up_wgrad", h22, dup2)
    dx_a, dmix, dgate1, dscale2, dshift2, d_mix_post, d_ffn_pre = rows_bwd(
        "postmix_bwd", fn_postmix, nb, S // tr, tr, pm_rows, pm_pb, pm_gl,
        [(dx1, D, 0, 0), (dh2, D, 0, 0)], [(0, F32), (1, BF16)])
    dmix2 = dmix.reshape(nb * S, D)
    dcat = matmul("out_dgrad", [(dmix2, W["w_out_p"])], "nt", BF16).reshape(nb, S, QP + D_INNER)
    g_out_p = jnp.concatenate([matmul_tn("out_wgrad_attn", o2, dmix2), matmul_tn("out_wgrad_ssd", ssd2, dmix2)], axis=0)
    dy, dxs_direct, dz, d_dexp, d_ssd_norm = rows_bwd(
        "ssd_finish_bwd", fn_ssd_finish, nb, S // tr, tr, fin_rows, [], fin_gl,
        [(dcat, D_INNER, QP // D_INNER, 0)], [(0, F32), (2, F32), (3, BF16)])
    early = [_per_device(g, n) for g, n in zip((_out_grad(g_out_p), glu_deinterleave(g_up), g_down), EARLY)]
    dxbc2, ddt2, ssd_stats, *received = ssd_bwd(xbc, u, alog, dtb, hin, dy, C, early)
    dxbc_raw, ssd_rows = ssd_conv_bwd(u, ssd_w8, V["ssd_conv_b"], dxbc2, dxs_direct, C, tc)
    dq, dkv, dkr = attn_bwd(q, kv, kr, dcat, tq)
    dq_pre = rope_call("rope_dq", dq, QP, 0, cosS, -sinS, BF16, tr).reshape(nb * S, QP)
    dkr_pre = rope_call("rope_dk", dkr, LANE, 0, cosT, -sinT, BF16, tr)
    dkv2 = dkv.reshape(nb * T, QP)
    dqn = matmul("q_dgrad", [(dq_pre, W["w_q_p"])], "nt", F32).reshape(nb, S, Q_RANK)
    g_q_p = matmul_tn("q_wgrad", qn2, dq_pre)
    dkvn = matmul("kv_dgrad", [(dkv2, W["w_kv"])], "nt", F32).reshape(nb, T, KV_RANK)
    g_kv = matmul_tn("kv_wgrad", kvn2, dkv2)
    dcq, d_q_norm = rows_bwd("q_norm_bwd", fn_rms, nb, S // tr, tr, [(u, Q_RANK, OFF_CQ // Q_RANK, cblk)], [], [V["q_norm"]],
                             [(dqn, Q_RANK, 0, 0)], [(0, BF16)])
    dckv, d_kv_norm = rows_bwd("kv_norm_bwd", fn_rms, nb, T // tr, tr, [(u, KV_RANK, OFF_CKV // KV_RANK, 0)], [], [V["kv_norm"]],
                               [(dkvn, KV_RANK, 0, 0)], [(0, BF16)])

    def ctx_rows(t):
        return jnp.pad(t, ((0, 0), (C, 0), (0, 0)))

    du = jnp.concatenate([ctx_rows(dcq), dkr_pre, dckv, jnp.zeros((nb, T, OFF_Z - OFF_CKV - KV_RANK), BF16), ctx_rows(dz),
                          dxbc_raw, (ddt2[0] + ddt2[1]).astype(BF16), jnp.zeros((nb, T, WIN_P - OFF_DT - LANE), BF16)],
                         axis=-1).reshape(nb * T, WIN_P)
    dh1 = matmul("in_dgrad", [(du, W["w_in_p"])], "nt", BF16).reshape(nb, T, D)
    g_in_p = matmul_tn("in_wgrad", h1, du)

    def fn_prenorm_res(xv, shift, scale, g):
        return fn_prenorm(xv, shift, scale, g) + (xv,)

    grad_x, dshift1, dscale1, d_mix_pre_x = rows_bwd(
        "prenorm_x_bwd", fn_prenorm_res, nb, S // tr, tr, [(x, D, 0, 0)], [m[0], m[1]], [V["mix_pre_norm"]],
        [(dh1, D, 0, cblk), (dx_a, D, 0, 0)], [(0, F32)])
    dshift_c, dscale_c, d_mix_pre_c = rows_bwd(
        "prenorm_c_bwd", fn_prenorm, nb, C // tr, tr, [(ctx, D, 0, 0)], [], [mc[0], mc[1], V["mix_pre_norm"]],
        [(dh1, D, 0, 0)], [])

    dmod_x = jnp.concatenate([dshift1, dscale1, dgate1, dshift2, dscale2, dgate2], axis=-1).reshape(nb, N_MOD * D)
    dmod_c = jnp.concatenate([dshift_c, dscale_c, jnp.zeros((1, (N_MOD - 2) * D), F32)], axis=-1)
    gm = dict(w_in_p=g_in_p, w_q_p=g_q_p, w_kv=g_kv, w_out_p=g_out_p, w_up=g_up, w_down=g_down)
    gv = dict(
        mix_pre_norm=d_mix_pre_x + d_mix_pre_c, mix_post_norm=d_mix_post, q_norm=d_q_norm, kv_norm=d_kv_norm,
        ssd_conv_w=ssd_rows[:SSD_K], ssd_conv_b=ssd_rows[SSD_K:SSD_K + 1],
        ssd_a_log=ssd_stats[0:1, :2 * SSD_HEADS], ssd_dt_bias=ssd_stats[1:2, :2 * SSD_HEADS],
        ssd_d=jnp.sum(d_dexp.reshape(SSD_HEADS, SSD_P), axis=1).reshape(1, SSD_HEADS), ssd_norm=d_ssd_norm,
        ffn_pre_norm=d_ffn_pre, ffn_post_norm=d_ffn_post,
        ffn_conv_w=ffn_rows[:FFN_K], ffn_conv_b=ffn_rows[FFN_K:FFN_K + 1])
    return loss, grad_x, dmod_x, dmod_c, gm, gv, dict(zip(EARLY, received))


WEIGHT_ORDER = ("c_ctx", "w_mod", "b_mod", "mix_pre_norm", "mix_post_norm", "w_in", "q_norm", "w_q_up", "kv_norm",
                "w_kv_up", "ssd_conv_w", "ssd_conv_b", "ssd_a_log", "ssd_dt_bias", "ssd_d", "ssd_norm", "w_out",
                "ffn_pre_norm", "ffn_post_norm", "w_up", "ffn_conv_w", "ffn_conv_b", "w_down")
MATRICES = ("w_in", "w_q_up", "w_kv_up", "w_out", "w_up", "w_down")
ROW_SHARDED = ("w_out", "w_down")
SMALL_SUMMED = ("c_ctx", "mix_pre_norm", "mix_post_norm", "q_norm", "kv_norm", "ssd_conv_w", "ssd_conv_b", "ssd_a_log",
                "ssd_dt_bias", "ssd_d", "ssd_norm", "ffn_pre_norm", "ffn_post_norm", "ffn_conv_w", "ffn_conv_b")
MOD_ROWS = 8


def _whole(shards, name):
    if name in ROW_SHARDED:
        return shards.reshape(-1, shards.shape[-1])
    return jnp.concatenate([shards[j] for j in range(N_DEV)], axis=1)


def _per_device(g, name):
    if name in ROW_SHARDED:
        return g.reshape(N_DEV, -1, g.shape[-1])
    return jnp.stack(jnp.split(g, N_DEV, axis=1))


def kernel(x, c, ctx, c_ctx, w_mod, b_mod, mix_pre_norm, mix_post_norm, w_in, q_norm, w_q_up, kv_norm, w_kv_up, ssd_conv_w, ssd_conv_b, ssd_a_log, ssd_dt_bias, ssd_d, ssd_norm, w_out, ffn_pre_norm, ffn_post_norm, w_up, ffn_conv_w, ffn_conv_b, w_down, loss_target, m_c_ctx, m_w_mod, m_b_mod, m_mix_pre_norm, m_mix_post_norm, m_w_in, m_q_norm, m_w_q_up, m_kv_norm, m_w_kv_up, m_ssd_conv_w, m_ssd_conv_b, m_ssd_a_log, m_ssd_dt_bias, m_ssd_d, m_ssd_norm, m_w_out, m_ffn_pre_norm, m_ffn_post_norm, m_w_up, m_ffn_conv_w, m_ffn_conv_b, m_w_down, v_c_ctx, v_w_mod, v_b_mod, v_mix_pre_norm, v_mix_post_norm, v_w_in, v_q_norm, v_w_q_up, v_kv_norm, v_w_kv_up, v_ssd_conv_w, v_ssd_conv_b, v_ssd_a_log, v_ssd_dt_bias, v_ssd_d, v_ssd_norm, v_w_out, v_ffn_pre_norm, v_ffn_post_norm, v_w_up, v_ffn_conv_w, v_ffn_conv_b, v_w_down):
    weights = dict(c_ctx=c_ctx, w_mod=w_mod, b_mod=b_mod, mix_pre_norm=mix_pre_norm, mix_post_norm=mix_post_norm, w_in=w_in, q_norm=q_norm, w_q_up=w_q_up, kv_norm=kv_norm, w_kv_up=w_kv_up, ssd_conv_w=ssd_conv_w, ssd_conv_b=ssd_conv_b, ssd_a_log=ssd_a_log, ssd_dt_bias=ssd_dt_bias, ssd_d=ssd_d, ssd_norm=ssd_norm, w_out=w_out, ffn_pre_norm=ffn_pre_norm, ffn_post_norm=ffn_post_norm, w_up=w_up, ffn_conv_w=ffn_conv_w, ffn_conv_b=ffn_conv_b, w_down=w_down)
    mom1 = dict(c_ctx=m_c_ctx, w_mod=m_w_mod, b_mod=m_b_mod, mix_pre_norm=m_mix_pre_norm, mix_post_norm=m_mix_post_norm, w_in=m_w_in, q_norm=m_q_norm, w_q_up=m_w_q_up, kv_norm=m_kv_norm, w_kv_up=m_w_kv_up, ssd_conv_w=m_ssd_conv_w, ssd_conv_b=m_ssd_conv_b, ssd_a_log=m_ssd_a_log, ssd_dt_bias=m_ssd_dt_bias, ssd_d=m_ssd_d, ssd_norm=m_ssd_norm, w_out=m_w_out, ffn_pre_norm=m_ffn_pre_norm, ffn_post_norm=m_ffn_post_norm, w_up=m_w_up, ffn_conv_w=m_ffn_conv_w, ffn_conv_b=m_ffn_conv_b, w_down=m_w_down)
    mom2 = dict(c_ctx=v_c_ctx, w_mod=v_w_mod, b_mod=v_b_mod, mix_pre_norm=v_mix_pre_norm, mix_post_norm=v_mix_post_norm, w_in=v_w_in, q_norm=v_q_norm, w_q_up=v_w_q_up, kv_norm=v_kv_norm, w_kv_up=v_w_kv_up, ssd_conv_w=v_ssd_conv_w, ssd_conv_b=v_ssd_conv_b, ssd_a_log=v_ssd_a_log, ssd_dt_bias=v_ssd_dt_bias, ssd_d=v_ssd_d, ssd_norm=v_ssd_norm, w_out=v_w_out, ffn_pre_norm=v_ffn_pre_norm, ffn_post_norm=v_ffn_post_norm, w_up=v_w_up, ffn_conv_w=v_ffn_conv_w, ffn_conv_b=v_ffn_conv_b, w_down=v_w_down)
    nb, S, D = x.shape
    me = 4 * lax.axis_index("x") + 2 * lax.axis_index("y") + lax.axis_index("c")

    gathered = all_gather("gather_weights", [weights[n][0].astype(BF16) for n in MATRICES])
    W = weights_to_internal(*[_whole(s, n) for n, s in zip(MATRICES, gathered)])
    c_all, ssd_w_sh, ffn_w_sh = all_gather("gather_small", [c, ssd_conv_w[0], ffn_conv_w[0]])
    V = {n: weights[n].reshape(1, -1) for n in SMALL_SUMMED if n != "c_ctx"}
    V["ssd_conv_w"] = _whole(ssd_w_sh, "ssd_conv_w")
    V["ffn_conv_w"] = _whole(ffn_w_sh, "ffn_conv_w")

    n_all = N_DEV * nb
    mod_rows = -(-(n_all + 1) // 8) * 8
    c_pad = jnp.concatenate([c_all.reshape(n_all, D), c_ctx.reshape(1, D), jnp.zeros((mod_rows - n_all - 1, D), F32)], axis=0)
    mod_cols = w_mod.shape[2]
    b_mine = lax.dynamic_slice(b_mod, (0, me * mod_cols), (1, mod_cols))
    mod_part = matmul("mod_proj", [(c_pad, w_mod[0])], "nn", F32, bias=b_mine, silu_a=True)
    mod_all = _whole(all_gather("gather_mod", [mod_part])[0], "w_mod")
    mod_x = lax.dynamic_slice(mod_all, (me * nb, 0), (nb, mod_all.shape[1]))
    mod_c = mod_all[n_all:n_all + 1]

    loss, grad_x, dmod_x, dmod_c, gm, gv, slots = local_step(x, ctx, loss_target, mod_x, mod_c, W, V)

    dmod_mine = jnp.concatenate([dmod_x, dmod_c, jnp.zeros((MOD_ROWS - nb - 1, dmod_x.shape[1]), F32)], axis=0)
    dmod_all = all_gather("gather_dmod", [dmod_mine])[0]
    dmod_ctx = sum_slots("sum_dmod_ctx", dmod_all[:, nb:nb + 1].reshape(N_DEV, -1, LANE)).reshape(1, -1)
    dmod_full = jnp.concatenate([dmod_all[:, :nb].reshape(n_all, -1), dmod_ctx,
                                 jnp.zeros((mod_rows - n_all - 1, dmod_ctx.shape[1]), F32)], axis=0)
    (g_b_mod,) = ew_call("mod_bias_grad", lambda t: (jnp.sum(t, axis=0, keepdims=True),), [dmod_full], [((1, dmod_full.shape[1]), F32)])
    dmod_cols = lax.dynamic_slice(dmod_full, (0, me * mod_cols), (mod_rows, mod_cols))
    g_w_mod = matmul_tn("mod_wgrad", c_pad, dmod_cols, silu_a=True)
    dsilu_ctx = matmul("mod_dgrad_ctx", [(dmod_cols[n_all:n_all + 8], w_mod[0])], "nt", F32)[0:1]

    def silu_vjp(cc, ct):
        return (jax.vjp(_silu, cc)[1](ct)[0],)

    (g_c_ctx_part,) = ew_call("c_ctx_grad", silu_vjp, [c_ctx.reshape(1, D), dsilu_ctx], [((1, D), F32)])

    gv = dict(gv, c_ctx=g_c_ctx_part)
    small_parts = [loss] + [gv[n] for n in SMALL_SUMMED]
    small_sum = sum_slots("sum_small", all_gather("gather_small_grads", [_pack_rows(small_parts)])[0])
    summed = _unpack_rows(small_sum, [p.shape for p in small_parts])
    loss_out = summed[0][0, 0]
    grads = {n: g.reshape(weights[n].shape) if n not in ("ssd_conv_w", "ffn_conv_w") else g for n, g in zip(SMALL_SUMMED, summed[1:])}
    for n in ("ssd_conv_w", "ffn_conv_w"):
        cols = weights[n].shape[2]
        grads[n] = lax.dynamic_slice(grads[n], (0, me * cols), (grads[n].shape[0], cols)).reshape(weights[n].shape)
    grads["b_mod"] = g_b_mod.reshape(b_mod.shape)

    late = ("w_in", "w_q_up", "w_kv_up")
    late_grads = (_in_grad(gm["w_in_p"]), _q_grad(gm["w_q_p"]), gm["w_kv"])
    received = all_to_all("exchange_matrix_grads", [_per_device(g, n) for g, n in zip(late_grads, late)])
    slots = dict(slots, **dict(zip(late, received)), w_mod=g_w_mod[None])
    delta, new_m, new_v = {}, {}, {}
    for n in MATRICES + ("w_mod",):
        g, d, mn, vn = adamw_matrix("adamw_" + n, weights[n][0], slots[n], mom1[n][0], mom2[n][0])
        grads[n], delta[n], new_m[n], new_v[n] = [t.reshape(weights[n].shape) for t in (g, d, mn, vn)]
    small = [n for n in WEIGHT_ORDER if n not in slots]

    def two_d(t):
        return t.reshape(-1, t.shape[-1])

    ds, ms, vs = adamw_small(*[[two_d(t[n]) for n in small] for t in (weights, grads, mom1, mom2)])
    for n, d, mn, vn in zip(small, ds, ms, vs):
        delta[n], new_m[n], new_v[n] = [t.reshape(weights[n].shape) for t in (d, mn, vn)]
    return (loss_out, grad_x, *[t[n] for t in (grads, delta, new_m, new_v) for n in WEIGHT_ORDER])
```

```python
import functools
import math

import jax
import jax.numpy as jnp
import numpy as np
from jax import lax
from jax.experimental import pallas as pl
from jax.experimental.pallas import tpu as pltpu

F32 = jnp.float32
BF16 = jnp.bfloat16
MESH = pl.DeviceIdType.MESH

D_MODEL = 1024
GRID_W = 64
N_HEADS = 16
NOPE = 64
ROPE = 32
V_DIM = 64
Q_RANK = 384
KV_RANK = 256
ROPE_THETA = 10000.0
ATTN_SCALE = (NOPE + ROPE) ** -0.5
SSD_HEADS = 16
SSD_P = 64
SSD_GROUPS = 2
SSD_N = 128
SSD_K = 5
CHUNK = 128
D_INNER = SSD_HEADS * SSD_P
GN = SSD_GROUPS * SSD_N
XBC = D_INNER + 2 * GN
D_FF = 2816
FFN_K = 3
N_MOD = 6
EPS = 1e-6
IN_SPLITS = (Q_RANK, KV_RANK, ROPE, D_INNER, XBC, 2 * SSD_HEADS)
IN_WIDTH = sum(IN_SPLITS)
N_DEV = 8

ADAM_LR = 0.001
ADAM_B1 = 0.9
ADAM_B2 = 0.999
ADAM_EPS = 1e-08
ADAM_WD = 0.01
ADAM_STEP = 10

LANE = 128
HEAD_BLOCK = 128
OFF_CQ = 0
OFF_KR = 384
OFF_CKV = 512
OFF_Z = 1024
OFF_XBC = 2048
OFF_DT = 3584
WIN_P = 3840
KR_LANE = 64
QP = N_HEADS * HEAD_BLOCK

VMEM_LIMIT_V7X = 56 * 1024 * 1024
NEG_BIG = -1e30


def _cparams(*sem):
    return pltpu.CompilerParams(dimension_semantics=sem, vmem_limit_bytes=VMEM_LIMIT_V7X)


def _tile(n, target, mult=128):
    if n <= target:
        return n
    t = (target // mult) * mult
    while t >= mult:
        if n % t == 0:
            return t
        t -= mult
    return n


def _silu(x):
    return x * jax.nn.sigmoid(x)


def _rms(x, g):
    return x * lax.rsqrt(jnp.mean(x * x, axis=-1, keepdims=True) + EPS) * g


WHOLE_K_WIDE = 2048


def matmul(name, pairs, mode, out_dtype, *, bias=None, silu_a=False, hosted=None):
    n_pairs = len(pairs)
    M = pairs[0][0].shape[0]
    N = pairs[0][1].shape[1] if mode == "nn" else pairs[0][1].shape[0]
    k_total = sum(a.shape[1] for a, _ in pairs)
    tm = _tile(M, 1024 if k_total <= WHOLE_K_WIDE else 512, 8)
    tn = _tile(N, 1408 if k_total <= WHOLE_K_WIDE else 512)
    dims = (((1,), (0,)), ((), ())) if mode == "nn" else (((1,), (1,)), ((), ()))
    n_own = 2 * n_pairs + (bias is not None)
    n_ex = hosted.n if hosted else 0

    def body(*refs):
        o_ref = refs[n_own + n_ex]
        if hosted:
            j, i = pl.program_id(0), pl.program_id(1)
            begin_exchange, end_exchange = hosted.steps(
                refs[n_own:n_own + n_ex], refs[n_own + n_ex + 1:n_own + 2 * n_ex + 1], refs[n_own + 2 * n_ex + 1:],
                jnp.logical_and(j == 0, i == 0), jnp.logical_and(j == N // tn - 1, i == M // tm - 1))
            begin_exchange()
        acc = None
        for p in range(n_pairs):
            a = refs[2 * p][...]
            if silu_a:
                a = _silu(a.astype(F32))
            d = lax.dot_general(a.astype(BF16), refs[2 * p + 1][...].astype(BF16), dims, preferred_element_type=F32)
            acc = d if acc is None else acc + d
        if bias is not None:
            acc = acc + refs[2 * n_pairs][...]
        o_ref[...] = acc.astype(o_ref.dtype)
        if hosted:
            end_exchange()

    in_specs, args = [], []
    for a, b in pairs:
        K = a.shape[1]
        in_specs.append(pl.BlockSpec((tm, K), lambda j, i: (i, 0)))
        in_specs.append(pl.BlockSpec((K, tn), lambda j, i: (0, j)) if mode == "nn" else pl.BlockSpec((tn, K), lambda j, i: (j, 0)))
        args += [a, b]
    if bias is not None:
        in_specs.append(pl.BlockSpec((1, tn), lambda j, i: (0, j)))
        args.append(bias)
    out_spec = pl.BlockSpec((tm, tn), lambda j, i: (i, j))
    out_shape = jax.ShapeDtypeStruct((M, N), out_dtype)
    if not hosted:
        return pl.pallas_call(
            body, name=name, grid=(N // tn, M // tm), in_specs=in_specs, out_specs=out_spec, out_shape=out_shape,
            compiler_params=_cparams("arbitrary", "arbitrary"),
        )(*args)
    return pl.pallas_call(
        body, name=name, grid=(N // tn, M // tm), in_specs=in_specs + hosted.specs,
        out_specs=[out_spec] + hosted.specs, out_shape=[out_shape] + hosted.out_shape, scratch_shapes=hosted.scratch,
        compiler_params=_cparams("arbitrary", "arbitrary"),
    )(*args, *hosted.arrays)


def matmul_tn(name, a, b, out_dtype=F32, *, silu_a=False, tm=1408, tn=512, tk=2048):
    R, M = a.shape
    N = b.shape[1]
    tm = _tile(M, tm)
    tn = _tile(N, tn)
    tk = _tile(R, tk, 8)
    nk = R // tk

    def body(a_ref, b_ref, o_ref, acc):
        k = pl.program_id(2)

        @pl.when(k == 0)
        def _():
            acc[...] = jnp.zeros_like(acc)

        x = a_ref[...]
        if silu_a:
            x = _silu(x.astype(F32))
        acc[...] += lax.dot_general(x.astype(BF16), b_ref[...].astype(BF16), (((0,), (0,)), ((), ())),
                                    preferred_element_type=F32)

        @pl.when(k == nk - 1)
        def _():
            o_ref[...] = acc[...].astype(o_ref.dtype)

    return pl.pallas_call(
        body, name=name, grid=(M // tm, N // tn, nk),
        in_specs=[pl.BlockSpec((tk, tm), lambda i, j, k: (k, i)), pl.BlockSpec((tk, tn), lambda i, j, k: (k, j))],
        out_specs=pl.BlockSpec((tm, tn), lambda i, j, k: (i, j)),
        out_shape=jax.ShapeDtypeStruct((M, N), out_dtype),
        scratch_shapes=[pltpu.VMEM((tm, tn), F32)],
        compiler_params=_cparams("arbitrary", "arbitrary", "arbitrary"),
    )(a, b)


def _row_specs(rin, pbin, glin, tr):
    specs = [pl.BlockSpec((1, tr, w), lambda b, i, cb=cb, ro=ro, bo=(e[4] if len(e) > 4 else 0): (b + bo, i + ro, cb))
             for e in rin for (_, w, cb, ro) in [e[:4]]]
    specs += [pl.BlockSpec((1, 1, a.shape[-1]), lambda b, i: (b, 0, 0)) for a in pbin]
    specs += [pl.BlockSpec((1, a.shape[-1]), lambda b, i: (0, 0)) for a in glin]
    return specs


def rows_fwd(name, fn, nb, nblk, tr, rin, pbin, glin, outs):
    nr, npb, ngl = len(rin), len(pbin), len(glin)
    n_in = nr + npb + ngl

    def body(*refs):
        args = [r[0].astype(F32) for r in refs[:nr + npb]] + [r[...] for r in refs[nr + npb:n_in]]
        res = fn(*args)
        for o, v in zip(refs[n_in:], res):
            o[0] = v.astype(o.dtype)

    return pl.pallas_call(
        body, name=name, grid=(nb, nblk), in_specs=_row_specs(rin, pbin, glin, tr),
        out_specs=[pl.BlockSpec((1, tr, w), lambda b, i: (b, i, 0)) for (w, _) in outs],
        out_shape=[jax.ShapeDtypeStruct((nb, nblk * tr, w), dt) for (w, dt) in outs],
        compiler_params=_cparams("arbitrary", "arbitrary"),
    )(*[e[0] for e in rin], *pbin, *glin)


def rows_bwd(name, fn, nb, nblk, tr, rin, pbin, glin, cts, want):
    nr, npb, ngl, nct = len(rin), len(pbin), len(glin), len(cts)
    n_in = nr + npb + ngl

    def body(*refs):
        b, i = pl.program_id(0), pl.program_id(1)
        args = [r[0].astype(F32) for r in refs[:nr + npb]] + [r[...] for r in refs[nr + npb:n_in]]
        ct = tuple(r[0].astype(F32) for r in refs[n_in:n_in + nct])
        _, vjp = jax.vjp(fn, *args)
        g = vjp(ct)
        orefs = refs[n_in + nct:]
        for o, (idx, _) in zip(orefs, want):
            o[0] = g[idx].astype(o.dtype)
        pb_refs = orefs[len(want):len(want) + npb]
        gl_refs = orefs[len(want) + npb:]

        @pl.when(i == 0)
        def _():
            for o, v in zip(pb_refs, g[nr:nr + npb]):
                o[0] = v

        @pl.when(i > 0)
        def _():
            for o, v in zip(pb_refs, g[nr:nr + npb]):
                o[0] += v

        first = jnp.logical_and(b == 0, i == 0)

        @pl.when(first)
        def _():
            for o, v in zip(gl_refs, g[nr + npb:]):
                o[...] = v

        @pl.when(jnp.logical_not(first))
        def _():
            for o, v in zip(gl_refs, g[nr + npb:]):
                o[...] += v

    out_specs = [pl.BlockSpec((1, tr, rin[idx][1]), lambda b, i: (b, i, 0)) for (idx, _) in want]
    out_shape = [jax.ShapeDtypeStruct((nb, nblk * tr, rin[idx][1]), dt) for (idx, dt) in want]
    out_specs += [pl.BlockSpec((1, 1, a.shape[-1]), lambda b, i: (b, 0, 0)) for a in pbin]
    out_shape += [jax.ShapeDtypeStruct((nb, 1, a.shape[-1]), F32) for a in pbin]
    out_specs += [pl.BlockSpec((1, a.shape[-1]), lambda b, i: (0, 0)) for a in glin]
    out_shape += [jax.ShapeDtypeStruct((1, a.shape[-1]), F32) for a in glin]
    return pl.pallas_call(
        body, name=name, grid=(nb, nblk),
        in_specs=_row_specs(rin, pbin, glin, tr) + _row_specs(cts, [], [], tr),
        out_specs=out_specs, out_shape=out_shape,
        compiler_params=_cparams("arbitrary", "arbitrary"),
    )(*[e[0] for e in rin], *pbin, *glin, *[e[0] for e in cts])


def ew_call(name, fn, ins, outs):
    def body(*refs):
        res = fn(*[r[...] for r in refs[:len(ins)]])
        for o, v in zip(refs[len(ins):], res):
            o[...] = v.astype(o.dtype)

    return pl.pallas_call(body, name=name, out_shape=[jax.ShapeDtypeStruct(s, dt) for (s, dt) in outs])(*ins)


def fn_prenorm(x, shift, scale, g):
    return (_rms(x, g) * (1.0 + scale) + shift,)


def fn_rms(x, g):
    return (_rms(x, g),)


def fn_ssd_finish(yf, yr, xs, z, dexp, nw):
    y = yf + yr + dexp * xs
    return (_rms(y * _silu(z), nw),)


def fn_postmix(x, mix, gate1, scale2, shift2, post_g, pre_g):
    x1 = x + gate1 * _rms(mix, post_g)
    h2 = _rms(x1, pre_g) * (1.0 + scale2) + shift2
    return x1, h2


def final_call(x1, ffn, target, gate2, post_g, tr):
    nb, S, D = x1.shape
    nblk = S // tr

    def body(x1_ref, f_ref, t_ref, g2_ref, pg_ref, dx1_ref, df_ref, dg2_ref, dpg_ref, loss_ref):
        b, i = pl.program_id(0), pl.program_id(1)
        tgt = t_ref[0]

        def lossfn(x1v, fv, g2, pg):
            e = x1v + g2 * _rms(fv, pg) - tgt
            return 0.5 * jnp.sum(jnp.mean(e * e, axis=-1, keepdims=True))

        val, (dx1, df, dg2, dpg) = jax.value_and_grad(lossfn, argnums=(0, 1, 2, 3))(
            x1_ref[0], f_ref[0].astype(F32), g2_ref[0], pg_ref[...])
        dx1_ref[0] = dx1
        df_ref[0] = df.astype(df_ref.dtype)
        lv = jnp.full((1, LANE), val, F32)

        @pl.when(i == 0)
        def _():
            dg2_ref[0] = dg2

        @pl.when(i > 0)
        def _():
            dg2_ref[0] += dg2

        first = jnp.logical_and(b == 0, i == 0)

        @pl.when(first)
        def _():
            dpg_ref[...] = dpg
            loss_ref[...] = lv

        @pl.when(jnp.logical_not(first))
        def _():
            dpg_ref[...] += dpg
            loss_ref[...] += lv

    row = pl.BlockSpec((1, tr, D), lambda b, i: (b, i, 0))
    pb = pl.BlockSpec((1, 1, D), lambda b, i: (b, 0, 0))
    gl = pl.BlockSpec((1, D), lambda b, i: (0, 0))
    return pl.pallas_call(
        body, name="loss_head", grid=(nb, nblk), in_specs=[row, row, row, pb, gl],
        out_specs=[row, row, pb, gl, pl.BlockSpec((1, LANE), lambda b, i: (0, 0))],
        out_shape=[jax.ShapeDtypeStruct((nb, S, D), F32), jax.ShapeDtypeStruct((nb, S, D), BF16),
                   jax.ShapeDtypeStruct((nb, 1, D), F32), jax.ShapeDtypeStruct((1, D), F32),
                   jax.ShapeDtypeStruct((1, LANE), F32)],
        compiler_params=_cparams("arbitrary", "arbitrary"),
    )(x1, ffn, target, gate2, post_g)


def _rotate_half(t):
    lane = lax.broadcasted_iota(jnp.int32, t.shape, 1)
    return jnp.where((lane & 15) < 8, -pltpu.roll(t, LANE - 8, 1), pltpu.roll(t, 8, 1))


def rope_call(name, x, width, colblk, cos, sin, out_dtype, tr):
    nb = x.shape[0]
    R = cos.shape[0]
    nblk = R // tr

    def body(x_ref, c_ref, s_ref, o_ref):
        c, s = c_ref[...], s_ref[...]
        for h in range(width // LANE):
            t = x_ref[0, :, h * LANE:(h + 1) * LANE].astype(F32)
            o_ref[0, :, h * LANE:(h + 1) * LANE] = (t * c + _rotate_half(t) * s).astype(o_ref.dtype)

    tab = pl.BlockSpec((tr, LANE), lambda b, i: (i, 0))
    return pl.pallas_call(
        body, name=name, grid=(nb, nblk),
        in_specs=[pl.BlockSpec((1, tr, width), lambda b, i: (b, i, colblk)), tab, tab],
        out_specs=pl.BlockSpec((1, tr, width), lambda b, i: (b, i, 0)),
        out_shape=jax.ShapeDtypeStruct((nb, R, width), out_dtype),
        compiler_params=_cparams("arbitrary", "arbitrary"),
    )(x, cos, sin)


def rope_tables(n_ctx, seq):
    n_rows = seq // GRID_W
    row = np.repeat(np.arange(n_rows), GRID_W).astype(np.float32)
    col = np.tile(np.arange(GRID_W), n_rows).astype(np.float32)
    axis_dim = ROPE // 2
    inv_freq = jnp.asarray(ROPE_THETA, F32) ** (-jnp.arange(0, axis_dim, 2, dtype=F32) / axis_dim)
    ang_r = jnp.asarray(row)[:, None] * inv_freq
    ang_c = jnp.asarray(col)[:, None] * inv_freq
    ang = jnp.concatenate([ang_r, ang_r, ang_c, ang_c], axis=-1)
    cos = jnp.ones((n_ctx + seq, LANE), F32).at[n_ctx:, KR_LANE:KR_LANE + ROPE].set(jnp.cos(ang))
    sin = jnp.zeros((n_ctx + seq, LANE), F32).at[n_ctx:, KR_LANE:KR_LANE + ROPE].set(jnp.sin(ang))
    return cos, sin


Q_PRESCALE = ATTN_SCALE * math.log2(math.e)


def _attn_weights(q, kc):
    s2 = lax.dot_general(q, kc, (((1,), (1,)), ((), ())), preferred_element_type=F32)
    e = jnp.exp2(s2 - jnp.max(s2, axis=1, keepdims=True))
    return e, 1.0 / jnp.sum(e, axis=1, keepdims=True)


def _key_block(kv, kr):
    lane = lax.broadcasted_iota(jnp.int32, kv.shape, 1)
    return jnp.where(lane < NOPE, kv, kr)


def attn_fwd(q, kv, kr, tq):
    nb, S, _ = q.shape
    T = kv.shape[1]

    def body(q_ref, kv_ref, kr_ref, o_ref):
        kvv = kv_ref[0]
        e, r = _attn_weights(q_ref[0], _key_block(kvv, kr_ref[0]))
        o = lax.dot_general(e.astype(BF16), kvv, (((1,), (0,)), ((), ())), preferred_element_type=F32) * r
        lane = lax.broadcasted_iota(jnp.int32, o.shape, 1)
        o_ref[0] = jnp.where(lane >= NOPE, o, 0.0).astype(o_ref.dtype)

    return pl.pallas_call(
        body, name="attn_fwd", grid=(nb, N_HEADS, S // tq),
        in_specs=[pl.BlockSpec((1, tq, HEAD_BLOCK), lambda b, h, i: (b, i, h)),
                  pl.BlockSpec((1, T, HEAD_BLOCK), lambda b, h, i: (b, 0, h)),
                  pl.BlockSpec((1, T, HEAD_BLOCK), lambda b, h, i: (b, 0, 0))],
        out_specs=pl.BlockSpec((1, tq, HEAD_BLOCK), lambda b, h, i: (b, i, h)),
        out_shape=jax.ShapeDtypeStruct((nb, S, QP), BF16),
        compiler_params=_cparams("arbitrary", "arbitrary", "arbitrary"),
    )(q, kv, kr)


def attn_bwd(q, kv, kr, do, tq):
    nb, S, _ = q.shape
    T = kv.shape[1]

    def body(q_ref, kv_ref, kr_ref, do_ref, dq_ref, dkv_ref, dkr_ref):
        h, i = pl.program_id(1), pl.program_id(2)
        qv, kvv, dov = q_ref[0], kv_ref[0], do_ref[0]
        kc = _key_block(kvv, kr_ref[0])
        e, r = _attn_weights(qv, kc)
        dor = (dov.astype(F32) * r).astype(BF16)
        dpr = lax.dot_general(dor, kvv, (((1,), (1,)), ((), ())), preferred_element_type=F32)
        ds = (e * (dpr - r * jnp.sum(dpr * e, axis=1, keepdims=True))).astype(BF16)
        dq = lax.dot_general(ds, kc, (((1,), (0,)), ((), ())), preferred_element_type=F32)
        dq_ref[0] = (dq * ATTN_SCALE).astype(dq_ref.dtype)
        dkc = lax.dot_general(ds, qv, (((0,), (0,)), ((), ())), preferred_element_type=F32) * math.log(2.0)
        dv = lax.dot_general(e.astype(BF16), dor, (((0,), (0,)), ((), ())), preferred_element_type=F32)
        lane = lax.broadcasted_iota(jnp.int32, dkc.shape, 1)
        dkv = jnp.where(lane < NOPE, dkc, dv)
        dkr = jnp.where(lane >= NOPE, dkc, 0.0)

        @pl.when(i == 0)
        def _():
            dkv_ref[0] = dkv

        @pl.when(i > 0)
        def _():
            dkv_ref[0] += dkv

        first = jnp.logical_and(h == 0, i == 0)

        @pl.when(first)
        def _():
            dkr_ref[0] = dkr

        @pl.when(jnp.logical_not(first))
        def _():
            dkr_ref[0] += dkr

    qspec = pl.BlockSpec((1, tq, HEAD_BLOCK), lambda b, h, i: (b, i, h))
    kspec = pl.BlockSpec((1, T, HEAD_BLOCK), lambda b, h, i: (b, 0, h))
    rspec = pl.BlockSpec((1, T, HEAD_BLOCK), lambda b, h, i: (b, 0, 0))
    return pl.pallas_call(
        body, name="attn_bwd", grid=(nb, N_HEADS, S // tq),
        in_specs=[qspec, kspec, rspec, qspec], out_specs=[qspec, kspec, rspec],
        out_shape=[jax.ShapeDtypeStruct((nb, S, QP), F32), jax.ShapeDtypeStruct((nb, T, QP), F32),
                   jax.ShapeDtypeStruct((nb, T, HEAD_BLOCK), F32)],
        compiler_params=_cparams("arbitrary", "arbitrary", "arbitrary"),
    )(q, kv, kr, do)


def _seg_bounds(n, n_ctx):
    t = lax.broadcasted_iota(jnp.int32, (n, 1), 0)
    if n_ctx == 0:
        return t, jnp.zeros_like(t), jnp.full_like(t, n)
    in_ctx = t < n_ctx
    return t, jnp.where(in_ctx, 0, n_ctx), jnp.where(in_ctx, n_ctx, n)


def _shift_rows(x, o, bounds):
    if o == 0:
        return x
    t, lo, hi = bounds
    n = x.shape[0]
    valid = jnp.logical_and(t + o >= lo, t + o < hi).astype(F32)
    return pltpu.roll(x, (-o) % n, 0) * valid


def _conv(x, w, bias, k, bounds):
    acc = bias
    for o in range(k):
        acc = acc + w[o:o + 1, :] * _shift_rows(x, o - k // 2, bounds)
    return acc


def _conv_bwd(x, w, dpre, k, bounds):
    dx = jnp.zeros_like(x)
    rows = []
    for o in range(k):
        dx = dx + w[o:o + 1, :] * _shift_rows(dpre, -(o - k // 2), bounds)
        rows.append(jnp.sum(dpre * _shift_rows(x, o - k // 2, bounds), axis=0, keepdims=True))
    rows.append(jnp.sum(dpre, axis=0, keepdims=True))
    sub8 = lax.broadcasted_iota(jnp.int32, (8, x.shape[1]), 0)
    out = jnp.zeros((8, x.shape[1]), F32)
    for o, r in enumerate(rows):
        out = out + jnp.where(sub8 == o, r, 0.0)
    return dx, out


def _gelu(x):
    return 0.5 * x * (1.0 + lax.erf(x * (1.0 / math.sqrt(2.0))))


def _gelu_grad(x):
    return 0.5 * (1.0 + lax.erf(x * (1.0 / math.sqrt(2.0)))) + x * jnp.exp(-0.5 * x * x) * (1.0 / math.sqrt(2.0 * math.pi))


def ssd_conv_fwd(u, w8, bias, n_ctx, tc):
    nb, T, _ = u.shape
    cb0 = OFF_XBC // tc

    def body(x_ref, w_ref, b_ref, o_ref):
        pre = _conv(x_ref[0], w_ref[...], b_ref[...], SSD_K, _seg_bounds(T, n_ctx))
        o_ref[0] = _silu(pre)

    return pl.pallas_call(
        body, name="ssd_conv_fwd", grid=(nb, XBC // tc),
        in_specs=[pl.BlockSpec((1, T, tc), lambda b, j: (b, 0, cb0 + j)),
                  pl.BlockSpec((8, tc), lambda b, j: (0, j)), pl.BlockSpec((1, tc), lambda b, j: (0, j))],
        out_specs=pl.BlockSpec((1, T, tc), lambda b, j: (b, 0, j)),
        out_shape=jax.ShapeDtypeStruct((nb, T, XBC), F32),
        compiler_params=_cparams("arbitrary", "arbitrary"),
    )(u, w8, bias)


def ssd_conv_bwd(u, w8, bias, dxbc, dxs_direct, n_ctx, tc):
    nb, T, _ = u.shape
    cb0 = OFF_XBC // tc
    n_direct = D_INNER // tc

    def body(x_ref, w_ref, b_ref, d0_ref, d1_ref, dd_ref, dx_ref, dw_ref, acc):
        j, b = pl.program_id(0), pl.program_id(1)
        acc[...] = d0_ref[0, 0] + d1_ref[0, 0]

        @pl.when(j < n_direct)
        def _():
            acc[n_ctx:, :] += dd_ref[0]

        bounds = _seg_bounds(T, n_ctx)
        x, w = x_ref[0], w_ref[...]
        pre = _conv(x, w, b_ref[...], SSD_K, bounds)
        sg = jax.nn.sigmoid(pre)
        dpre = acc[...] * (sg * (1.0 + pre * (1.0 - sg)))
        dx, rows = _conv_bwd(x, w, dpre, SSD_K, bounds)
        dx_ref[0] = dx.astype(dx_ref.dtype)

        @pl.when(b == 0)
        def _():
            dw_ref[...] = rows

        @pl.when(b > 0)
        def _():
            dw_ref[...] += rows

    dspec0 = pl.BlockSpec((1, 1, T, tc), lambda j, b: (0, b, 0, j))
    dspec1 = pl.BlockSpec((1, 1, T, tc), lambda j, b: (1, b, 0, j))
    return pl.pallas_call(
        body, name="ssd_conv_bwd", grid=(XBC // tc, nb),
        in_specs=[pl.BlockSpec((1, T, tc), lambda j, b: (b, 0, cb0 + j)),
                  pl.BlockSpec((8, tc), lambda j, b: (0, j)), pl.BlockSpec((1, tc), lambda j, b: (0, j)),
                  dspec0, dspec1,
                  pl.BlockSpec((1, T - n_ctx, tc), lambda j, b: (b, 0, jnp.minimum(j, n_direct - 1)))],
        out_specs=[pl.BlockSpec((1, T, tc), lambda j, b: (b, 0, j)), pl.BlockSpec((8, tc), lambda j, b: (0, j))],
        out_shape=[jax.ShapeDtypeStruct((nb, T, XBC), BF16), jax.ShapeDtypeStruct((8, XBC), F32)],
        scratch_shapes=[pltpu.VMEM((T, tc), F32)],
        compiler_params=_cparams("arbitrary", "arbitrary"),
    )(u, w8, bias, dxbc, dxbc, dxs_direct)


GLU_TC = 256


def glu_interleave(w_up):
    blocks = []
    for j in range(D_FF // GLU_TC):
        blocks += [w_up[:, j * GLU_TC:(j + 1) * GLU_TC], w_up[:, D_FF + j * GLU_TC:D_FF + (j + 1) * GLU_TC]]
    return jnp.concatenate(blocks, axis=1)


def glu_deinterleave(g):
    nj = D_FF // GLU_TC
    gate = [g[:, 2 * j * GLU_TC:(2 * j + 1) * GLU_TC] for j in range(nj)]
    val = [g[:, (2 * j + 1) * GLU_TC:(2 * j + 2) * GLU_TC] for j in range(nj)]
    return jnp.concatenate(gate + val, axis=1)


def glu_fwd(up, w8, bias):
    nb, S, _ = up.shape
    tc = GLU_TC

    def body(u_ref, w_ref, b_ref, o_ref):
        gc = _conv(u_ref[0, :, :tc], w_ref[...], b_ref[...], FFN_K, _seg_bounds(S, 0))
        o_ref[0] = (_gelu(gc) * u_ref[0, :, tc:]).astype(o_ref.dtype)

    return pl.pallas_call(
        body, name="glu_fwd", grid=(nb, D_FF // tc),
        in_specs=[pl.BlockSpec((1, S, 2 * tc), lambda b, j: (b, 0, j)),
                  pl.BlockSpec((8, tc), lambda b, j: (0, j)), pl.BlockSpec((1, tc), lambda b, j: (0, j))],
        out_specs=pl.BlockSpec((1, S, tc), lambda b, j: (b, 0, j)),
        out_shape=jax.ShapeDtypeStruct((nb, S, D_FF), BF16),
        compiler_params=_cparams("arbitrary", "arbitrary"),
    )(up, w8, bias)


def glu_bwd(up, w8, bias, dact):
    nb, S, _ = up.shape
    tc = GLU_TC

    def body(u_ref, w_ref, b_ref, d_ref, du_ref, dw_ref):
        b = pl.program_id(1)
        bounds = _seg_bounds(S, 0)
        x, w, val, d = u_ref[0, :, :tc], w_ref[...], u_ref[0, :, tc:], d_ref[0].astype(F32)
        gc = _conv(x, w, b_ref[...], FFN_K, bounds)
        du_ref[0, :, tc:] = (d * _gelu(gc)).astype(du_ref.dtype)
        dx, rows = _conv_bwd(x, w, d * val * _gelu_grad(gc), FFN_K, bounds)
        du_ref[0, :, :tc] = dx.astype(du_ref.dtype)

        @pl.when(b == 0)
        def _():
            dw_ref[...] = rows

        @pl.when(b > 0)
        def _():
            dw_ref[...] += rows

    pair = pl.BlockSpec((1, S, 2 * tc), lambda j, b: (b, 0, j))
    return pl.pallas_call(
        body, name="glu_bwd", grid=(D_FF // tc, nb),
        in_specs=[pair, pl.BlockSpec((8, tc), lambda j, b: (0, j)), pl.BlockSpec((1, tc), lambda j, b: (0, j)),
                  pl.BlockSpec((1, S, tc), lambda j, b: (b, 0, j))],
        out_specs=[pair, pl.BlockSpec((8, tc), lambda j, b: (0, j))],
        out_shape=[jax.ShapeDtypeStruct((nb, S, 2 * D_FF), BF16), jax.ShapeDtypeStruct((8, D_FF), F32)],
        compiler_params=_cparams("arbitrary", "arbitrary"),
    )(up, w8, bias, dact)


def _chunk_of(d, k, n_cc, n_ch):
    rev = jnp.where(k < n_cc, n_cc - 1 - k, n_cc + n_ch - 1 - k)
    return jnp.where(d == 1, rev, k)


def _lane_pick(v, lane_iota, l):
    return jnp.sum(jnp.where(lane_iota == l, v, 0.0), axis=1, keepdims=True)


def _row_pick(v, sub_iota, l):
    return jnp.sum(jnp.where(sub_iota == l, v, 0.0), axis=0, keepdims=True)


def _softplus(x):
    return jnp.maximum(x, 0.0) + jnp.log(1.0 + jnp.exp(-jnp.abs(x)))


def _ssd_common(d, dt_raw, alog, dtb):
    Q = dt_raw.shape[0]
    row = lax.broadcasted_iota(jnp.int32, (Q, Q), 0)
    col = lax.broadcasted_iota(jnp.int32, (Q, Q), 1)
    rev = d == 1
    maskb = jnp.where(rev, row, col) <= jnp.where(rev, col, row)
    tri = maskb.astype(F32)
    A = -jnp.exp(alog)
    dtv = _softplus(dt_raw + dtb)
    a = dtv * A
    cum = lax.dot_general(tri, a, (((1,), (0,)), ((), ())), precision=lax.Precision.HIGHEST, preferred_element_type=F32)
    tot = jnp.sum(a, axis=0, keepdims=True)
    return maskb, tri, A, dtv, cum, tot


def ssd_fwd(xbc, u, alog, dtb, n_ctx, hosted):
    nb, T, _ = xbc.shape
    S = T - n_ctx
    n_ch, n_cc = T // CHUNK, n_ctx // CHUNK
    dt_cb = OFF_DT // LANE
    Q = CHUNK
    n_pairs = SSD_HEADS // 2
    n_ex = hosted.n

    def body(*refs):
        x_ref, dt_ref, al_ref, db_ref = refs[:4]
        send_refs = refs[4:4 + n_ex]
        y_ref, hin_ref = refs[4 + n_ex:6 + n_ex]
        recv_refs = refs[6 + n_ex:6 + 2 * n_ex]
        H, *sems = refs[6 + 2 * n_ex:]
        d, k = pl.program_id(1), pl.program_id(2)
        first_step = jnp.logical_and(jnp.logical_and(pl.program_id(0) == 0, d == 0), k == 0)
        last_step = jnp.logical_and(jnp.logical_and(pl.program_id(0) == nb - 1, d == 1), k == n_ch - 1)
        begin_exchange, end_exchange = hosted.steps(send_refs, recv_refs, sems, first_step, last_step)
        begin_exchange()

        @pl.when(k == 0)
        def _():
            H[...] = jnp.zeros_like(H)

        maskb, tri, A, dtv, cum, tot = _ssd_common(d, dt_ref[0], al_ref[...], db_ref[...])
        cumT = cum.T
        hin_ref[0, 0, 0] = H[...].astype(BF16)
        lane = lax.broadcasted_iota(jnp.int32, (Q, LANE), 1)
        lane1 = lax.broadcasted_iota(jnp.int32, (1, LANE), 1)
        sub = lax.broadcasted_iota(jnp.int32, (LANE, Q), 0)
        subc = lax.broadcasted_iota(jnp.int32, (LANE, 1), 0)
        half = lane < SSD_P
        for g in range(SSD_GROUPS):
            Bg = x_ref[0, :, D_INNER + g * SSD_N:D_INNER + (g + 1) * SSD_N].astype(BF16)
            Cg = x_ref[0, :, D_INNER + GN + g * SSD_N:D_INNER + GN + (g + 1) * SSD_N].astype(BF16)
            Gm = lax.dot_general(Cg, Bg, (((1,), (1,)), ((), ())), preferred_element_type=F32)
            for pr in range(n_pairs // SSD_GROUPS):
                p = g * (n_pairs // SSD_GROUPS) + pr
                l0 = d * SSD_HEADS + 2 * p
                l1 = l0 + 1
                s0c, s1c = _lane_pick(cum, lane, l0), _lane_pick(cum, lane, l1)
                s0r, s1r = _row_pick(cumT, sub, l0), _row_pick(cumT, sub, l1)
                dtp = jnp.where(half, _lane_pick(dtv, lane, l0), _lane_pick(dtv, lane, l1))
                tot0, tot1 = _lane_pick(tot, lane1, l0), _lane_pick(tot, lane1, l1)
                sc = jnp.where(half, s0c, s1c)
                totp = jnp.where(half, tot0, tot1)
                M0 = (Gm * jnp.exp(jnp.where(maskb, s0c - s0r, NEG_BIG))).astype(BF16)
                M1 = (Gm * jnp.exp(jnp.where(maskb, s1c - s1r, NEG_BIG))).astype(BF16)
                xd = x_ref[0, :, p * LANE:(p + 1) * LANE] * dtp
                xdb = xd.astype(BF16)
                yd = jnp.where(half,
                               lax.dot_general(M0, xdb, (((1,), (0,)), ((), ())), preferred_element_type=F32),
                               lax.dot_general(M1, xdb, (((1,), (0,)), ((), ())), preferred_element_type=F32))
                Hp = H[p * LANE:(p + 1) * LANE, :]
                yo = lax.dot_general(Cg, Hp.astype(BF16), (((1,), (1,)), ((), ())), preferred_element_type=F32) * jnp.exp(sc)

                @pl.when(k >= n_cc)
                def _():
                    y_ref[0, 0, :, p * LANE:(p + 1) * LANE] = yd + yo

                xdw = (xd * jnp.exp(totp - sc)).astype(BF16)
                etot = jnp.exp(jnp.where(subc < SSD_P, tot0, tot1))
                H[p * LANE:(p + 1) * LANE, :] = Hp * etot + lax.dot_general(
                    xdw, Bg, (((0,), (0,)), ((), ())), preferred_element_type=F32)
        end_exchange()

    def ymap(b, d, k):
        return (d, b, _chunk_of(d, jnp.maximum(k, n_cc), n_cc, n_ch) - n_cc, 0)

    return pl.pallas_call(
        body, name="ssd_fwd", grid=(nb, 2, n_ch),
        in_specs=[pl.BlockSpec((1, Q, XBC), lambda b, d, k: (b, _chunk_of(d, k, n_cc, n_ch), 0)),
                  pl.BlockSpec((1, Q, LANE), lambda b, d, k: (b, _chunk_of(d, k, n_cc, n_ch), dt_cb)),
                  pl.BlockSpec((1, LANE), lambda b, d, k: (0, 0)), pl.BlockSpec((1, LANE), lambda b, d, k: (0, 0))] + hosted.specs,
        out_specs=[pl.BlockSpec((1, 1, Q, D_INNER), ymap),
                   pl.BlockSpec((1, 1, 1, D_INNER, SSD_N), lambda b, d, k: (d, b, k, 0, 0))] + hosted.specs,
        out_shape=[jax.ShapeDtypeStruct((2, nb, S, D_INNER), F32),
                   jax.ShapeDtypeStruct((2, nb, n_ch, D_INNER, SSD_N), BF16)] + hosted.out_shape,
        scratch_shapes=[pltpu.VMEM((D_INNER, SSD_N), F32)] + hosted.scratch,
        compiler_params=_cparams("arbitrary", "arbitrary", "arbitrary"),
    )(xbc, u, alog, dtb, *hosted.arrays)


def ssd_bwd(xbc, u, alog, dtb, hin, dy, n_ctx, hosted):
    nb, T, _ = xbc.shape
    n_ex = hosted.n
    n_ch, n_cc = T // CHUNK, n_ctx // CHUNK
    dt_cb = OFF_DT // LANE
    Q = CHUNK
    n_pairs = SSD_HEADS // 2
    NT = (((1,), (1,)), ((), ()))
    NN = (((1,), (0,)), ((), ()))
    TN = (((0,), (0,)), ((), ()))

    def dot(a, b, dims):
        return lax.dot_general(a.astype(BF16), b.astype(BF16), dims, preferred_element_type=F32)

    def body(*refs):
        x_ref, dt_ref, al_ref, db_ref, hin_ref, dy_ref = refs[:6]
        send_refs = refs[6:6 + n_ex]
        dx_ref, ddt_ref, st_ref = refs[6 + n_ex:9 + n_ex]
        recv_refs = refs[9 + n_ex:9 + 2 * n_ex]
        dH, *sems = refs[9 + 2 * n_ex:]
        d, kk = pl.program_id(1), pl.program_id(2)
        ks = n_ch - 1 - kk
        first_step = jnp.logical_and(jnp.logical_and(pl.program_id(0) == 0, d == 0), kk == 0)
        last_step = jnp.logical_and(jnp.logical_and(pl.program_id(0) == nb - 1, d == 1), kk == n_ch - 1)
        begin_exchange, end_exchange = hosted.steps(send_refs, recv_refs, sems, first_step, last_step)
        begin_exchange()

        @pl.when(kk == 0)
        def _():
            dH[...] = jnp.zeros_like(dH)

        @pl.when(jnp.logical_and(jnp.logical_and(pl.program_id(0) == 0, d == 0), kk == 0))
        def _():
            st_ref[...] = jnp.zeros_like(st_ref)

        dt_raw = dt_ref[0]
        alog, dtb_v = al_ref[...], db_ref[...]
        maskb, tri, A, dtv, cum, tot = _ssd_common(d, dt_raw, alog, dtb_v)
        cumT = cum.T
        live = (ks >= n_cc).astype(F32)
        lane = lax.broadcasted_iota(jnp.int32, (Q, LANE), 1)
        lane1 = lax.broadcasted_iota(jnp.int32, (1, LANE), 1)
        sub = lax.broadcasted_iota(jnp.int32, (LANE, Q), 0)
        subc = lax.broadcasted_iota(jnp.int32, (LANE, 1), 0)
        half = lane < SSD_P
        halfc = subc < SSD_P
        dcum = jnp.zeros((Q, LANE), F32)
        dcumT = jnp.zeros((LANE, Q), F32)
        ddt = jnp.zeros((Q, LANE), F32)
        dtot = jnp.zeros((1, LANE), F32)
        for g in range(SSD_GROUPS):
            Bg = x_ref[0, :, D_INNER + g * SSD_N:D_INNER + (g + 1) * SSD_N].astype(BF16)
            Cg = x_ref[0, :, D_INNER + GN + g * SSD_N:D_INNER + GN + (g + 1) * SSD_N].astype(BF16)
            Gm = lax.dot_general(Cg, Bg, NT, preferred_element_type=F32)
            dG = jnp.zeros((Q, Q), F32)
            dC = jnp.zeros((Q, SSD_N), F32)
            dB = jnp.zeros((Q, SSD_N), F32)
            for pr in range(n_pairs // SSD_GROUPS):
                p = g * (n_pairs // SSD_GROUPS) + pr
                l0 = d * SSD_HEADS + 2 * p
                l1 = l0 + 1
                s0c, s1c = _lane_pick(cum, lane, l0), _lane_pick(cum, lane, l1)
                s0r, s1r = _row_pick(cumT, sub, l0), _row_pick(cumT, sub, l1)
                dtp = jnp.where(half, _lane_pick(dtv, lane, l0), _lane_pick(dtv, lane, l1))
                tot0, tot1 = _lane_pick(tot, lane1, l0), _lane_pick(tot, lane1, l1)
                sc = jnp.where(half, s0c, s1c)
                totp = jnp.where(half, tot0, tot1)
                L0 = jnp.exp(jnp.where(maskb, s0c - s0r, NEG_BIG))
                L1 = jnp.exp(jnp.where(maskb, s1c - s1r, NEG_BIG))
                M0, M1 = Gm * L0, Gm * L1
                xs = x_ref[0, :, p * LANE:(p + 1) * LANE]
                xd = xs * dtp
                es = jnp.exp(sc)
                dte = jnp.exp(totp - sc)
                etot = jnp.exp(jnp.where(halfc, tot0, tot1))
                dyp = dy_ref[0, :, p * LANE:(p + 1) * LANE] * live
                Hp = hin_ref[0, 0, 0, p * LANE:(p + 1) * LANE, :]
                dHp = dH[p * LANE:(p + 1) * LANE, :]
                bdh = dot(Bg, dHp, NT)
                dxd = jnp.where(half, dot(M0, dyp, TN), dot(M1, dyp, TN)) + bdh * dte
                dy0 = jnp.where(half, dyp, 0.0)
                dy1 = dyp - dy0
                dM0, dM1 = dot(dy0, xd, NT), dot(dy1, xd, NT)
                dG = dG + dM0 * L0 + dM1 * L1
                dyes = dyp * es
                xdw = xd * dte
                dC = dC + dot(dyes, Hp, NN)
                dB = dB + dot(xdw, dHp, NN)
                W0, W1 = dM0 * M0, dM1 * M1
                yoff = dot(Cg, Hp, NT) * es
                r_off = dyp * yoff
                r_st = xd * bdh * dte
                hh = jnp.sum(dHp * Hp.astype(F32), axis=1, keepdims=True) * etot
                for (l, W, hsel, hselc) in ((l0, W0, half, halfc),
                                            (l1, W1, jnp.logical_not(half), jnp.logical_not(halfc))):
                    col_g = (jnp.sum(W, axis=1, keepdims=True)
                             + jnp.sum(jnp.where(hsel, r_off - r_st, 0.0), axis=1, keepdims=True))
                    row_g = -jnp.sum(W, axis=0, keepdims=True)
                    tot_g = (jnp.sum(jnp.sum(jnp.where(hsel, r_st, 0.0), axis=1, keepdims=True), axis=0, keepdims=True)
                             + jnp.sum(jnp.where(hselc, hh, 0.0), axis=0, keepdims=True))
                    dcum = dcum + jnp.where(lane == l, col_g, 0.0)
                    dcumT = dcumT + jnp.where(sub == l, row_g, 0.0)
                    dtot = dtot + jnp.where(lane1 == l, tot_g, 0.0)
                    ddt = ddt + jnp.where(lane == l, jnp.sum(jnp.where(hsel, dxd * xs, 0.0), axis=1, keepdims=True), 0.0)
                dx_ref[0, 0, :, p * LANE:(p + 1) * LANE] = dxd * dtp
                dH[p * LANE:(p + 1) * LANE, :] = dHp * etot + dot(dyes, Cg, TN)
            dx_ref[0, 0, :, D_INNER + g * SSD_N:D_INNER + (g + 1) * SSD_N] = dB + dot(dG, Cg, TN)
            dx_ref[0, 0, :, D_INNER + GN + g * SSD_N:D_INNER + GN + (g + 1) * SSD_N] = dC + dot(dG, Bg, NN)
        dcum_all = dcum + dcumT.T
        da = lax.dot_general(tri, dcum_all, TN, precision=lax.Precision.HIGHEST, preferred_element_type=F32) + dtot
        ddtv = ddt + da * A
        ddt_raw = ddtv * jax.nn.sigmoid(dt_raw + dtb_v)
        ddt_ref[0, 0] = ddt_raw
        st_ref[0:1, :] += jnp.sum(da * dtv * A, axis=0, keepdims=True)
        st_ref[1:2, :] += jnp.sum(ddt_raw, axis=0, keepdims=True)
        end_exchange()

    def cmap(d, kk):
        return _chunk_of(d, n_ch - 1 - kk, n_cc, n_ch)

    def dymap(b, d, kk):
        return (b, _chunk_of(d, jnp.maximum(n_ch - 1 - kk, n_cc), n_cc, n_ch) - n_cc, 0)

    return pl.pallas_call(
        body, name="ssd_bwd", grid=(nb, 2, n_ch),
        in_specs=[pl.BlockSpec((1, Q, XBC), lambda b, d, kk: (b, cmap(d, kk), 0)),
                  pl.BlockSpec((1, Q, LANE), lambda b, d, kk: (b, cmap(d, kk), dt_cb)),
                  pl.BlockSpec((1, LANE), lambda b, d, kk: (0, 0)), pl.BlockSpec((1, LANE), lambda b, d, kk: (0, 0)),
                  pl.BlockSpec((1, 1, 1, D_INNER, SSD_N), lambda b, d, kk: (d, b, n_ch - 1 - kk, 0, 0)),
                  pl.BlockSpec((1, Q, D_INNER), dymap)] + hosted.specs,
        out_specs=[pl.BlockSpec((1, 1, Q, XBC), lambda b, d, kk: (d, b, cmap(d, kk), 0)),
                   pl.BlockSpec((1, 1, Q, LANE), lambda b, d, kk: (d, b, cmap(d, kk), 0)),
                   pl.BlockSpec((8, LANE), lambda b, d, kk: (0, 0))] + hosted.specs,
        out_shape=[jax.ShapeDtypeStruct((2, nb, T, XBC), F32), jax.ShapeDtypeStruct((2, nb, T, LANE), F32),
                   jax.ShapeDtypeStruct((8, LANE), F32)] + hosted.out_shape,
        scratch_shapes=[pltpu.VMEM((D_INNER, SSD_N), F32)] + hosted.scratch,
        compiler_params=_cparams("arbitrary", "arbitrary", "arbitrary"),
    )(xbc, u, alog, dtb, hin, dy, *hosted.arrays)


def _adamw(w, g, m, v):
    mn = ADAM_B1 * m + (1.0 - ADAM_B1) * g
    vn = ADAM_B2 * v + (1.0 - ADAM_B2) * jnp.square(g)
    m_hat = mn / (1.0 - ADAM_B1 ** ADAM_STEP)
    v_hat = vn / (1.0 - ADAM_B2 ** ADAM_STEP)
    return -ADAM_LR * (m_hat / (jnp.sqrt(v_hat) + ADAM_EPS) + ADAM_WD * w), mn, vn


def adamw_matrix(name, w, g_slots, m, v):
    K, n = w.shape
    s = g_slots.shape[0]
    tr = _tile(K, 256, 8)

    def body(w_ref, g_ref, m_ref, v_ref, go_ref, d_ref, mo_ref, vo_ref):
        g = g_ref[0]
        for j in range(1, s):
            g = g + g_ref[j]
        go_ref[...] = g
        d_ref[...], mo_ref[...], vo_ref[...] = _adamw(w_ref[...], g, m_ref[...], v_ref[...])

    spec = pl.BlockSpec((tr, n), lambda i: (i, 0))
    return pl.pallas_call(
        body, name=name, grid=(K // tr,),
        in_specs=[spec, pl.BlockSpec((s, tr, n), lambda i: (0, i, 0)), spec, spec], out_specs=[spec] * 4,
        out_shape=[jax.ShapeDtypeStruct((K, n), F32)] * 4,
        compiler_params=_cparams("arbitrary"),
    )(w, g_slots, m, v)


def adamw_small(ws, gs, ms, vs):
    n = len(ws)

    def body(*refs):
        for i in range(n):
            d, mn, vn = _adamw(refs[i][...], refs[n + i][...], refs[2 * n + i][...], refs[3 * n + i][...])
            refs[4 * n + i][...] = d
            refs[5 * n + i][...] = mn
            refs[6 * n + i][...] = vn

    shapes = [jax.ShapeDtypeStruct(w.shape, F32) for w in ws]
    out = pl.pallas_call(body, name="adamw_small", out_shape=shapes * 3)(*ws, *gs, *ms, *vs)
    return out[:n], out[n:2 * n], out[2 * n:]


def sum_slots(name, x):
    n = x.shape[0]

    def fn(t):
        acc = t[0]
        for j in range(1, n):
            acc = acc + t[j]
        return (acc,)

    return ew_call(name, fn, [x], [(x.shape[1:], F32)])[0]


def _pack_rows(parts):
    rows = []
    for p in parts:
        flat = p.reshape(1, -1)
        n = flat.shape[1]
        rows.append(jnp.pad(flat, ((0, 0), (0, -(-n // (8 * LANE)) * 8 * LANE - n))).reshape(-1, LANE))
    return jnp.concatenate(rows, axis=0)


def _unpack_rows(pack, shapes):
    out, r = [], 0
    for s in shapes:
        n = int(np.prod(s))
        nr = -(-n // (8 * LANE)) * 8
        out.append(pack[r:r + nr].reshape(1, -1)[:, :n].reshape(s))
        r += nr
    return out


def _mesh_pos():
    return lax.axis_index("x"), lax.axis_index("y"), lax.axis_index("c")


N_PEERS = N_DEV - 1


def all_gather(name, vs):
    n = len(vs)

    def body(*refs):
        _ag_start(refs[:n], refs[n:2 * n], *refs[2 * n:])
        _ag_finish(refs[:n], refs[n:2 * n], *refs[2 * n:])

    hbm = pl.BlockSpec(memory_space=pl.ANY)
    return pl.pallas_call(
        body, name=name, out_shape=_ag_out_shape(vs), in_specs=[hbm] * n, out_specs=[hbm] * n,
        scratch_shapes=_a2a_scratch(n),
    )(*vs)


def _ag_out_shape(vs):
    return [jax.ShapeDtypeStruct((N_DEV,) + v.shape, v.dtype) for v in vs]


def _ag_copies(x_refs, out_refs, send_sems, recv_sems, local_sems):
    n = len(x_refs)
    x, y, c = _mesh_pos()
    me, sibling = (x, y, c), (x, y, 1 - c)
    chips = [(1 - x, y), (x, 1 - y), (1 - x, 1 - y)]

    def slot(a, px, py, pc):
        return out_refs[a].at[4 * px + 2 * py + pc]

    def copy(a, k, block, to, src=None):
        return pltpu.make_async_remote_copy(
            src_ref=slot(a, *block) if src is None else src, dst_ref=slot(a, *block),
            send_sem=send_sems.at[N_PEERS * a + k], recv_sem=recv_sems.at[N_PEERS * a + k],
            device_id=to, device_id_type=MESH)

    local = [pltpu.make_async_copy(x_refs[a], slot(a, *me), local_sems.at[a]) for a in range(n)]
    first = []
    for a in range(n):
        first.append(copy(a, 0, me, sibling, src=x_refs[a]))
        first += [copy(a, 1 + j, me, (*chip, c), src=x_refs[a]) for j, chip in enumerate(chips)]
    passed = [(copy(a, 1 + j, (*chip, c), me), copy(a, 4 + j, (*chip, c), sibling))
              for j, chip in enumerate(chips) for a in range(n)]
    from_sibling = []
    for a in range(n):
        from_sibling.append(copy(a, 0, sibling, me))
        from_sibling += [copy(a, 4 + j, (*chip, 1 - c), me) for j, chip in enumerate(chips)]
    return local, first, passed, from_sibling


def _ag_start(*refs):
    local, first, _, _ = _ag_copies(*refs)
    for cp in local + first:
        cp.start()


def _ag_finish(*refs):
    local, first, passed, from_sibling = _ag_copies(*refs)
    for arrived, hand_on in passed:
        arrived.wait_recv()
        hand_on.start()
    for cp in from_sibling:
        cp.wait_recv()
    for cp in first + [hand_on for _, hand_on in passed]:
        cp.wait_send()
    for cp in local:
        cp.wait()


def _a2a_scratch(n):
    return [pltpu.SemaphoreType.DMA((N_PEERS * n,)), pltpu.SemaphoreType.DMA((N_PEERS * n,)), pltpu.SemaphoreType.DMA((n,))]


def _a2a_copies(x_refs, out_refs, send_sems, recv_sems, local_sems):
    n = len(x_refs)
    x, y, c = _mesh_pos()
    me = 4 * x + 2 * y + c
    local = [pltpu.make_async_copy(x_refs[a].at[me], out_refs[a].at[me], local_sems.at[a]) for a in range(n)]
    remote = []
    for k in range(1, N_DEV):
        px, py, pc = x ^ ((k >> 2) & 1), y ^ ((k >> 1) & 1), c ^ (k & 1)
        for a in range(n):
            remote.append(pltpu.make_async_remote_copy(
                src_ref=x_refs[a].at[4 * px + 2 * py + pc], dst_ref=out_refs[a].at[me],
                send_sem=send_sems.at[N_PEERS * a + k - 1], recv_sem=recv_sems.at[N_PEERS * a + k - 1],
                device_id=(px, py, pc), device_id_type=MESH))
    return local, remote


def _a2a_start(local, remote):
    for cp in local + remote:
        cp.start()


def _a2a_wait(local, remote):
    for cp in remote:
        cp.wait_recv()
    for cp in remote:
        cp.wait_send()
    for cp in local:
        cp.wait()


class Hosted:
    def __init__(self, start=None, finish=None, arrays=(), out_shape=()):
        self.start, self.finish, self.arrays, self.out_shape = start, finish, list(arrays), list(out_shape)
        self.n = len(self.arrays)
        self.specs = [pl.BlockSpec(memory_space=pl.ANY)] * self.n
        self.scratch = _a2a_scratch(self.n) if self.n else []

    def steps(self, send_refs, recv_refs, sems, first_step, last_step):
        def begin():
            if self.n:
                pl.when(first_step)(lambda: self.start(send_refs, recv_refs, *sems))

        def end():
            if self.n:
                pl.when(last_step)(lambda: self.finish(send_refs, recv_refs, *sems))

        return begin, end


def hosted_all_to_all(vs):
    return Hosted(lambda *r: _a2a_start(*_a2a_copies(*r)), lambda *r: _a2a_wait(*_a2a_copies(*r)), vs,
                  [jax.ShapeDtypeStruct(v.shape, v.dtype) for v in vs])


def hosted_all_gather(vs):
    return Hosted(_ag_start, _ag_finish, vs, _ag_out_shape(vs))


def _taps8(w):
    return jnp.concatenate([w, jnp.zeros((8 - w.shape[0], w.shape[1]), w.dtype)], axis=0)


def _lanes128(v):
    v = v.reshape(1, -1)
    return jnp.pad(v, ((0, 0), (0, LANE - v.shape[1])))


FIRST = ("w_in", "w_q_up", "w_kv_up")
LATE_WEIGHTS = ("w_out", "w_up", "w_down")


def first_weights_to_internal(w_in, w_q_up, w_kv_up):
    cq, ckv, kr, z, xbc, dt = jnp.split(w_in, np.cumsum(IN_SPLITS)[:-1].tolist(), axis=1)
    K = w_in.shape[0]

    def zeros(n):
        return jnp.zeros((K, n), w_in.dtype)

    w_in_p = jnp.concatenate([cq, zeros(KR_LANE), kr, zeros(LANE - KR_LANE - ROPE), ckv, zeros(OFF_Z - OFF_CKV - KV_RANK),
                              z, xbc, dt, zeros(WIN_P - OFF_DT - 2 * SSD_HEADS)], axis=1)
    w_q_p = jnp.pad(w_q_up.reshape(Q_RANK, N_HEADS, NOPE + ROPE), ((0, 0), (0, 0), (0, HEAD_BLOCK - NOPE - ROPE))).reshape(Q_RANK, QP)
    return dict(w_in_p=w_in_p, w_q_p=w_q_p, w_kv=w_kv_up)


def late_weights_to_internal(w_out, w_up, w_down):
    attn_rows = w_out[:N_HEADS * V_DIM].reshape(N_HEADS, V_DIM, -1)
    w_out_p = jnp.concatenate([jnp.pad(attn_rows, ((0, 0), (HEAD_BLOCK - V_DIM, 0), (0, 0))).reshape(QP, -1),
                               w_out[N_HEADS * V_DIM:]], axis=0)
    return dict(w_out_p=w_out_p, w_up=glu_interleave(w_up), w_down=w_down)


def _in_grad(g_in_p):
    return jnp.concatenate([g_in_p[:, OFF_CQ:OFF_CQ + Q_RANK], g_in_p[:, OFF_CKV:OFF_CKV + KV_RANK],
                            g_in_p[:, OFF_KR + KR_LANE:OFF_KR + KR_LANE + ROPE], g_in_p[:, OFF_Z:OFF_Z + D_INNER],
                            g_in_p[:, OFF_XBC:OFF_XBC + XBC], g_in_p[:, OFF_DT:OFF_DT + 2 * SSD_HEADS]], axis=1)


def _q_grad(g_q_p):
    return g_q_p.reshape(Q_RANK, N_HEADS, HEAD_BLOCK)[:, :, :NOPE + ROPE].reshape(Q_RANK, -1)


def _out_grad(g_out_p):
    return jnp.concatenate([g_out_p[:QP].reshape(N_HEADS, HEAD_BLOCK, -1)[:, HEAD_BLOCK - V_DIM:].reshape(N_HEADS * V_DIM, -1),
                            g_out_p[QP:]], axis=0)


EARLY = ("w_out", "w_up", "w_down", "w_q_up", "w_kv_up")


def local_step(x, ctx, target, mod_x, mod_c, W, late_shards, V):
    nb, S, D = x.shape
    C = ctx.shape[1]
    T = C + S
    tr = _tile(math.gcd(C, S), 256, 8)
    tq = _tile(S, 256, 8)
    tc = 256
    cblk = C // tr
    m = [mod_x[:, i * D:(i + 1) * D][:, None, :] for i in range(N_MOD)]
    mc = [mod_c[:, i * D:(i + 1) * D] for i in range(2)]
    ssd_w8, ffn_w8 = _taps8(V["ssd_conv_w"]), _taps8(V["ffn_conv_w"])
    alog, dtb = _lanes128(V["ssd_a_log"]), _lanes128(V["ssd_dt_bias"])
    dexp = jnp.repeat(V["ssd_d"].reshape(-1), SSD_P).reshape(1, D_INNER)
    cosT, sinT = rope_tables(C, S)
    cosS, sinS = cosT[C:], sinT[C:]

    (h1x,) = rows_fwd("prenorm_x", fn_prenorm, nb, S // tr, tr, [(x, D, 0, 0)], [m[0], m[1]], [V["mix_pre_norm"]], [(D, BF16)])
    (h1c,) = rows_fwd("prenorm_c", fn_prenorm, nb, C // tr, tr, [(ctx, D, 0, 0)], [], [mc[0], mc[1], V["mix_pre_norm"]], [(D, BF16)])
    h1 = jnp.concatenate([h1c, h1x], axis=1).reshape(nb * T, D)
    u = matmul("in_proj", [(h1, W["w_in_p"])], "nn", F32).reshape(nb, T, WIN_P)
    (qn,) = rows_fwd("q_norm", fn_rms, nb, S // tr, tr, [(u, Q_RANK, OFF_CQ // Q_RANK, cblk)], [], [V["q_norm"]], [(Q_RANK, BF16)])
    (kvn,) = rows_fwd("kv_norm", fn_rms, nb, T // tr, tr, [(u, KV_RANK, OFF_CKV // KV_RANK, 0)], [], [V["kv_norm"]], [(KV_RANK, BF16)])
    qn2, kvn2 = qn.reshape(nb * S, Q_RANK), kvn.reshape(nb * T, KV_RANK)
    q_raw = matmul("q_up", [(qn2, W["w_q_p"])], "nn", F32).reshape(nb, S, QP)
    kv = matmul("kv_up", [(kvn2, W["w_kv"])], "nn", BF16).reshape(nb, T, QP)
    q = rope_call("rope_q", q_raw, QP, 0, cosS * Q_PRESCALE, sinS * Q_PRESCALE, BF16, tr)
    kr = rope_call("rope_k", u, LANE, OFF_KR // LANE, cosT, sinT, BF16, tr)
    o = attn_fwd(q, kv, kr, tq)
    xbc = ssd_conv_fwd(u, ssd_w8, V["ssd_conv_b"], C, tc)
    y2, hin, *late = ssd_fwd(xbc, u, alog, dtb, C, hosted_all_gather(late_shards))
    W = dict(W, **late_weights_to_internal(*[_whole(s, n) for s, n in zip(late, LATE_WEIGHTS)]))
    y2 = y2.reshape(2 * nb, S, D_INNER)
    fin_rows = [(y2, D_INNER, 0, 0, 0), (y2, D_INNER, 0, 0, nb), (xbc, D_INNER, 0, cblk), (u, D_INNER, OFF_Z // D_INNER, cblk)]
    fin_gl = [dexp, V["ssd_norm"]]
    (ssd,) = rows_fwd("ssd_finish", fn_ssd_finish, nb, S // tr, tr, fin_rows, [], fin_gl, [(D_INNER, BF16)])
    o2, ssd2 = o.reshape(nb * S, QP), ssd.reshape(nb * S, D_INNER)
    mix = matmul("out_proj", [(o2, W["w_out_p"][:QP]), (ssd2, W["w_out_p"][QP:])], "nn", F32).reshape(nb, S, D)
    pm_rows = [(x, D, 0, 0), (mix, D, 0, 0)]
    pm_pb = [m[2], m[4], m[3]]
    pm_gl = [V["mix_post_norm"], V["ffn_pre_norm"]]
    x1, h2 = rows_fwd("postmix", fn_postmix, nb, S // tr, tr, pm_rows, pm_pb, pm_gl, [(D, F32), (D, BF16)])
    h22 = h2.reshape(nb * S, D)
    up = matmul("up_proj", [(h22, W["w_up"])], "nn", F32).reshape(nb, S, 2 * D_FF)
    act = glu_fwd(up, ffn_w8, V["ffn_conv_b"])
    act2 = act.reshape(nb * S, D_FF)
    ffn = matmul("down_proj", [(act2, W["w_down"])], "nn", F32).reshape(nb, S, D)
    dx1, dffn, dgate2, d_ffn_post, loss = final_call(x1, ffn, target, m[5], V["ffn_post_norm"], tr)

    dffn2 = dffn.reshape(nb * S, D)
    dact = matmul("down_dgrad", [(dffn2, W["w_down"])], "nt", BF16).reshape(nb, S, D_FF)
    g_down = matmul_tn("down_wgrad", act2, dffn2)
    dup, ffn_rows = glu_bwd(up, ffn_w8, V["ffn_conv_b"], dact)
    dup2 = dup.reshape(nb * S, 2 * D_FF)
    dh2 = matmul("up_dgrad", [(dup2, W["w_up"])], "nt", BF16).reshape(nb, S, D)
    g_up = matmul_tn("up_wgrad", h22, dup2)
    dx_a, dmix, dgate1, dscale2, dshift2, d_mix_post, d_ffn_pre = rows_bwd(
        "postmix_bwd", fn_postmix, nb, S // tr, tr, pm_rows, pm_pb, pm_gl,
        [(dx1, D, 0, 0), (dh2, D, 0, 0)], [(0, F32), (1, BF16)])
    dmix2 = dmix.reshape(nb * S, D)
    dcat = matmul("out_dgrad", [(dmix2, W["w_out_p"])], "nt", BF16).reshape(nb, S, QP + D_INNER)
    g_out_p = jnp.concatenate([matmul_tn("out_wgrad_attn", o2, dmix2), matmul_tn("out_wgrad_ssd", ssd2, dmix2)], axis=0)
    dy, dxs_direct, dz, d_dexp, d_ssd_norm = rows_bwd(
        "ssd_finish_bwd", fn_ssd_finish, nb, S // tr, tr, fin_rows, [], fin_gl,
        [(dcat, D_INNER, QP // D_INNER, 0)], [(0, F32), (2, F32), (3, BF16)])
    dq, dkv, dkr = attn_bwd(q, kv, kr, dcat, tq)
    dq_pre = rope_call("rope_dq", dq, QP, 0, cosS, -sinS, BF16, tr).reshape(nb * S, QP)
    dkr_pre = rope_call("rope_dk", dkr, LANE, 0, cosT, -sinT, BF16, tr)
    dkv2 = dkv.reshape(nb * T, QP)
    dqn = matmul("q_dgrad", [(dq_pre, W["w_q_p"])], "nt", F32).reshape(nb, S, Q_RANK)
    g_q_p = matmul_tn("q_wgrad", qn2, dq_pre)
    dkvn = matmul("kv_dgrad", [(dkv2, W["w_kv"])], "nt", F32).reshape(nb, T, KV_RANK)
    g_kv = matmul_tn("kv_wgrad", kvn2, dkv2)
    early_grads = (_out_grad(g_out_p), glu_deinterleave(g_up), g_down, _q_grad(g_q_p), g_kv)
    early = hosted_all_to_all([_per_device(g, n) for g, n in zip(early_grads, EARLY)])
    dxbc2, ddt2, ssd_stats, *received = ssd_bwd(xbc, u, alog, dtb, hin, dy, C, early)
    dxbc_raw, ssd_rows = ssd_conv_bwd(u, ssd_w8, V["ssd_conv_b"], dxbc2, dxs_direct, C, tc)
    dcq, d_q_norm = rows_bwd("q_norm_bwd", fn_rms, nb, S // tr, tr, [(u, Q_RANK, OFF_CQ // Q_RANK, cblk)], [], [V["q_norm"]],
                             [(dqn, Q_RANK, 0, 0)], [(0, BF16)])
    dckv, d_kv_norm = rows_bwd("kv_norm_bwd", fn_rms, nb, T // tr, tr, [(u, KV_RANK, OFF_CKV // KV_RANK, 0)], [], [V["kv_norm"]],
                               [(dkvn, KV_RANK, 0, 0)], [(0, BF16)])

    def ctx_rows(t):
        return jnp.pad(t, ((0, 0), (C, 0), (0, 0)))

    du = jnp.concatenate([ctx_rows(dcq), dkr_pre, dckv, jnp.zeros((nb, T, OFF_Z - OFF_CKV - KV_RANK), BF16), ctx_rows(dz),
                          dxbc_raw, (ddt2[0] + ddt2[1]).astype(BF16), jnp.zeros((nb, T, WIN_P - OFF_DT - LANE), BF16)],
                         axis=-1).reshape(nb * T, WIN_P)
    g_in_p = matmul_tn("in_wgrad", h1, du)
    dh1, received_in = matmul("in_dgrad", [(du, W["w_in_p"])], "nt", BF16,
                              hosted=hosted_all_to_all([_per_device(_in_grad(g_in_p), "w_in")]))
    dh1 = dh1.reshape(nb, T, D)

    def fn_prenorm_res(xv, shift, scale, g):
        return fn_prenorm(xv, shift, scale, g) + (xv,)

    grad_x, dshift1, dscale1, d_mix_pre_x = rows_bwd(
        "prenorm_x_bwd", fn_prenorm_res, nb, S // tr, tr, [(x, D, 0, 0)], [m[0], m[1]], [V["mix_pre_norm"]],
        [(dh1, D, 0, cblk), (dx_a, D, 0, 0)], [(0, F32)])
    dshift_c, dscale_c, d_mix_pre_c = rows_bwd(
        "prenorm_c_bwd", fn_prenorm, nb, C // tr, tr, [(ctx, D, 0, 0)], [], [mc[0], mc[1], V["mix_pre_norm"]],
        [(dh1, D, 0, 0)], [])

    dmod_x = jnp.concatenate([dshift1, dscale1, dgate1, dshift2, dscale2, dgate2], axis=-1).reshape(nb, N_MOD * D)
    dmod_c = jnp.concatenate([dshift_c, dscale_c, jnp.zeros((1, (N_MOD - 2) * D), F32)], axis=-1)
    gv = dict(
        mix_pre_norm=d_mix_pre_x + d_mix_pre_c, mix_post_norm=d_mix_post, q_norm=d_q_norm, kv_norm=d_kv_norm,
        ssd_conv_w=ssd_rows[:SSD_K], ssd_conv_b=ssd_rows[SSD_K:SSD_K + 1],
        ssd_a_log=ssd_stats[0:1, :2 * SSD_HEADS], ssd_dt_bias=ssd_stats[1:2, :2 * SSD_HEADS],
        ssd_d=jnp.sum(d_dexp.reshape(SSD_HEADS, SSD_P), axis=1).reshape(1, SSD_HEADS), ssd_norm=d_ssd_norm,
        ffn_pre_norm=d_ffn_pre, ffn_post_norm=d_ffn_post,
        ffn_conv_w=ffn_rows[:FFN_K], ffn_conv_b=ffn_rows[FFN_K:FFN_K + 1])
    return loss, grad_x, dmod_x, dmod_c, gv, dict(zip(EARLY, received), w_in=received_in)


WEIGHT_ORDER = ("c_ctx", "w_mod", "b_mod", "mix_pre_norm", "mix_post_norm", "w_in", "q_norm", "w_q_up", "kv_norm",
                "w_kv_up", "ssd_conv_w", "ssd_conv_b", "ssd_a_log", "ssd_dt_bias", "ssd_d", "ssd_norm", "w_out",
                "ffn_pre_norm", "ffn_post_norm", "w_up", "ffn_conv_w", "ffn_conv_b", "w_down")
MATRICES = ("w_in", "w_q_up", "w_kv_up", "w_out", "w_up", "w_down")
ROW_SHARDED = ("w_out", "w_down")
SMALL_SUMMED = ("c_ctx", "mix_pre_norm", "mix_post_norm", "q_norm", "kv_norm", "ssd_conv_w", "ssd_conv_b", "ssd_a_log",
                "ssd_dt_bias", "ssd_d", "ssd_norm", "ffn_pre_norm", "ffn_post_norm", "ffn_conv_w", "ffn_conv_b")
MOD_ROWS = 8


def _whole(shards, name):
    if name in ROW_SHARDED:
        return shards.reshape(-1, shards.shape[-1])
    return jnp.concatenate([shards[j] for j in range(N_DEV)], axis=1)


def _per_device(g, name):
    if name in ROW_SHARDED:
        return g.reshape(N_DEV, -1, g.shape[-1])
    return jnp.stack(jnp.split(g, N_DEV, axis=1))


def kernel(x, c, ctx, c_ctx, w_mod, b_mod, mix_pre_norm, mix_post_norm, w_in, q_norm, w_q_up, kv_norm, w_kv_up, ssd_conv_w, ssd_conv_b, ssd_a_log, ssd_dt_bias, ssd_d, ssd_norm, w_out, ffn_pre_norm, ffn_post_norm, w_up, ffn_conv_w, ffn_conv_b, w_down, loss_target, m_c_ctx, m_w_mod, m_b_mod, m_mix_pre_norm, m_mix_post_norm, m_w_in, m_q_norm, m_w_q_up, m_kv_norm, m_w_kv_up, m_ssd_conv_w, m_ssd_conv_b, m_ssd_a_log, m_ssd_dt_bias, m_ssd_d, m_ssd_norm, m_w_out, m_ffn_pre_norm, m_ffn_post_norm, m_w_up, m_ffn_conv_w, m_ffn_conv_b, m_w_down, v_c_ctx, v_w_mod, v_b_mod, v_mix_pre_norm, v_mix_post_norm, v_w_in, v_q_norm, v_w_q_up, v_kv_norm, v_w_kv_up, v_ssd_conv_w, v_ssd_conv_b, v_ssd_a_log, v_ssd_dt_bias, v_ssd_d, v_ssd_norm, v_w_out, v_ffn_pre_norm, v_ffn_post_norm, v_w_up, v_ffn_conv_w, v_ffn_conv_b, v_w_down):
    weights = dict(c_ctx=c_ctx, w_mod=w_mod, b_mod=b_mod, mix_pre_norm=mix_pre_norm, mix_post_norm=mix_post_norm, w_in=w_in, q_norm=q_norm, w_q_up=w_q_up, kv_norm=kv_norm, w_kv_up=w_kv_up, ssd_conv_w=ssd_conv_w, ssd_conv_b=ssd_conv_b, ssd_a_log=ssd_a_log, ssd_dt_bias=ssd_dt_bias, ssd_d=ssd_d, ssd_norm=ssd_norm, w_out=w_out, ffn_pre_norm=ffn_pre_norm, ffn_post_norm=ffn_post_norm, w_up=w_up, ffn_conv_w=ffn_conv_w, ffn_conv_b=ffn_conv_b, w_down=w_down)
    mom1 = dict(c_ctx=m_c_ctx, w_mod=m_w_mod, b_mod=m_b_mod, mix_pre_norm=m_mix_pre_norm, mix_post_norm=m_mix_post_norm, w_in=m_w_in, q_norm=m_q_norm, w_q_up=m_w_q_up, kv_norm=m_kv_norm, w_kv_up=m_w_kv_up, ssd_conv_w=m_ssd_conv_w, ssd_conv_b=m_ssd_conv_b, ssd_a_log=m_ssd_a_log, ssd_dt_bias=m_ssd_dt_bias, ssd_d=m_ssd_d, ssd_norm=m_ssd_norm, w_out=m_w_out, ffn_pre_norm=m_ffn_pre_norm, ffn_post_norm=m_ffn_post_norm, w_up=m_w_up, ffn_conv_w=m_ffn_conv_w, ffn_conv_b=m_ffn_conv_b, w_down=m_w_down)
    mom2 = dict(c_ctx=v_c_ctx, w_mod=v_w_mod, b_mod=v_b_mod, mix_pre_norm=v_mix_pre_norm, mix_post_norm=v_mix_post_norm, w_in=v_w_in, q_norm=v_q_norm, w_q_up=v_w_q_up, kv_norm=v_kv_norm, w_kv_up=v_w_kv_up, ssd_conv_w=v_ssd_conv_w, ssd_conv_b=v_ssd_conv_b, ssd_a_log=v_ssd_a_log, ssd_dt_bias=v_ssd_dt_bias, ssd_d=v_ssd_d, ssd_norm=v_ssd_norm, w_out=v_w_out, ffn_pre_norm=v_ffn_pre_norm, ffn_post_norm=v_ffn_post_norm, w_up=v_w_up, ffn_conv_w=v_ffn_conv_w, ffn_conv_b=v_ffn_conv_b, w_down=v_w_down)
    nb, S, D = x.shape
    me = 4 * lax.axis_index("x") + 2 * lax.axis_index("y") + lax.axis_index("c")

    *first, c_all, ssd_w_sh, ffn_w_sh = all_gather(
        "gather_first", [weights[n][0].astype(BF16) for n in FIRST] + [c, ssd_conv_w[0], ffn_conv_w[0]])
    W = first_weights_to_internal(*[_whole(s, n) for n, s in zip(FIRST, first)])
    late_shards = [weights[n][0].astype(BF16) for n in LATE_WEIGHTS]
    V = {n: weights[n].reshape(1, -1) for n in SMALL_SUMMED if n != "c_ctx"}
    V["ssd_conv_w"] = _whole(ssd_w_sh, "ssd_conv_w")
    V["ffn_conv_w"] = _whole(ffn_w_sh, "ffn_conv_w")

    n_all = N_DEV * nb
    mod_rows = -(-(n_all + 1) // 8) * 8
    c_pad = jnp.concatenate([c_all.reshape(n_all, D), c_ctx.reshape(1, D), jnp.zeros((mod_rows - n_all - 1, D), F32)], axis=0)
    mod_cols = w_mod.shape[2]
    b_mine = lax.dynamic_slice(b_mod, (0, me * mod_cols), (1, mod_cols))
    mod_part = matmul("mod_proj", [(c_pad, w_mod[0])], "nn", F32, bias=b_mine, silu_a=True)
    mod_all = _whole(all_gather("gather_mod", [mod_part])[0], "w_mod")
    mod_x = lax.dynamic_slice(mod_all, (me * nb, 0), (nb, mod_all.shape[1]))
    mod_c = mod_all[n_all:n_all + 1]

    loss, grad_x, dmod_x, dmod_c, gv, slots = local_step(x, ctx, loss_target, mod_x, mod_c, W, late_shards, V)

    dmod_mine = jnp.concatenate([dmod_x, dmod_c, jnp.zeros((MOD_ROWS - nb - 1, dmod_x.shape[1]), F32)], axis=0)
    dmod_all = all_gather("gather_dmod", [dmod_mine])[0]
    dmod_ctx = sum_slots("sum_dmod_ctx", dmod_all[:, nb:nb + 1].reshape(N_DEV, -1, LANE)).reshape(1, -1)
    dmod_full = jnp.concatenate([dmod_all[:, :nb].reshape(n_all, -1), dmod_ctx,
                                 jnp.zeros((mod_rows - n_all - 1, dmod_ctx.shape[1]), F32)], axis=0)
    (g_b_mod,) = ew_call("mod_bias_grad", lambda t: (jnp.sum(t, axis=0, keepdims=True),), [dmod_full], [((1, dmod_full.shape[1]), F32)])
    dmod_cols = lax.dynamic_slice(dmod_full, (0, me * mod_cols), (mod_rows, mod_cols))
    g_w_mod = matmul_tn("mod_wgrad", c_pad, dmod_cols, silu_a=True)
    dsilu_ctx = matmul("mod_dgrad_ctx", [(dmod_cols[n_all:n_all + 8], w_mod[0])], "nt", F32)[0:1]

    def silu_vjp(cc, ct):
        return (jax.vjp(_silu, cc)[1](ct)[0],)

    (g_c_ctx_part,) = ew_call("c_ctx_grad", silu_vjp, [c_ctx.reshape(1, D), dsilu_ctx], [((1, D), F32)])

    gv = dict(gv, c_ctx=g_c_ctx_part)
    small_parts = [loss] + [gv[n] for n in SMALL_SUMMED]
    small_sum = sum_slots("sum_small", all_gather("gather_small_grads", [_pack_rows(small_parts)])[0])
    summed = _unpack_rows(small_sum, [p.shape for p in small_parts])
    loss_out = summed[0][0, 0]
    grads = {n: g.reshape(weights[n].shape) if n not in ("ssd_conv_w", "ffn_conv_w") else g for n, g in zip(SMALL_SUMMED, summed[1:])}
    for n in ("ssd_conv_w", "ffn_conv_w"):
        cols = weights[n].shape[2]
        grads[n] = lax.dynamic_slice(grads[n], (0, me * cols), (grads[n].shape[0], cols)).reshape(weights[n].shape)
    grads["b_mod"] = g_b_mod.reshape(b_mod.shape)

    slots = dict(slots, w_mod=g_w_mod[None])
    delta, new_m, new_v = {}, {}, {}
    for n in MATRICES + ("w_mod",):
        g, d, mn, vn = adamw_matrix("adamw_" + n, weights[n][0], slots[n], mom1[n][0], mom2[n][0])
        grads[n], delta[n], new_m[n], new_v[n] = [t.reshape(weights[n].shape) for t in (g, d, mn, vn)]
    small = [n for n in WEIGHT_ORDER if n not in slots]

    def two_d(t):
        return t.reshape(-1, t.shape[-1])

    ds, ms, vs = adamw_small(*[[two_d(t[n]) for n in small] for t in (weights, grads, mom1, mom2)])
    for n, d, mn, vn in zip(small, ds, ms, vs):
        delta[n], new_m[n], new_v[n] = [t.reshape(weights[n].shape) for t in (d, mn, vn)]
    return (loss_out, grad_x, *[t[n] for t in (grads, delta, new_m, new_v) for n in WEIGHT_ORDER])
```

```python
import functools
import math

import jax
import jax.numpy as jnp
import numpy as np
from jax import lax
from jax.experimental import pallas as pl
from jax.experimental.pallas import tpu as pltpu

F32 = jnp.float32
BF16 = jnp.bfloat16
MESH = pl.DeviceIdType.MESH

D_MODEL = 1024
GRID_W = 64
N_HEADS = 16
NOPE = 64
ROPE = 32
V_DIM = 64
Q_RANK = 384
KV_RANK = 256
ROPE_THETA = 10000.0
ATTN_SCALE = (NOPE + ROPE) ** -0.5
SSD_HEADS = 16
SSD_P = 64
SSD_GROUPS = 2
SSD_N = 128
SSD_K = 5
CHUNK = 128
D_INNER = SSD_HEADS * SSD_P
GN = SSD_GROUPS * SSD_N
XBC = D_INNER + 2 * GN
D_FF = 2816
FFN_K = 3
N_MOD = 6
EPS = 1e-6
IN_SPLITS = (Q_RANK, KV_RANK, ROPE, D_INNER, XBC, 2 * SSD_HEADS)
IN_WIDTH = sum(IN_SPLITS)
N_DEV = 8

ADAM_LR = 0.001
ADAM_B1 = 0.9
ADAM_B2 = 0.999
ADAM_EPS = 1e-08
ADAM_WD = 0.01
ADAM_STEP = 10

LANE = 128
HEAD_BLOCK = 128
OFF_CQ = 0
OFF_KR = 384
OFF_CKV = 512
OFF_Z = 1024
OFF_XBC = 2048
OFF_DT = 3584
WIN_P = 3840
KR_LANE = 64
QP = N_HEADS * HEAD_BLOCK

VMEM_LIMIT_V7X = 56 * 1024 * 1024
NEG_BIG = -1e30


def _cparams(*sem):
    return pltpu.CompilerParams(dimension_semantics=sem, vmem_limit_bytes=VMEM_LIMIT_V7X)


def _tile(n, target, mult=128):
    if n <= target:
        return n
    t = (target // mult) * mult
    while t >= mult:
        if n % t == 0:
            return t
        t -= mult
    return n


def _silu(x):
    return x * jax.nn.sigmoid(x)


def _rms(x, g):
    return x * lax.rsqrt(jnp.mean(x * x, axis=-1, keepdims=True) + EPS) * g


WHOLE_K_WIDE = 2048


def matmul(name, pairs, mode, out_dtype, *, bias=None, silu_a=False, hosted=None):
    n_pairs = len(pairs)
    M = pairs[0][0].shape[0]
    N = pairs[0][1].shape[1] if mode == "nn" else pairs[0][1].shape[0]
    k_total = sum(a.shape[1] for a, _ in pairs)
    tm = _tile(M, 1024 if k_total <= WHOLE_K_WIDE else 512, 8)
    tn = _tile(N, 1408 if k_total <= WHOLE_K_WIDE else 512)
    dims = (((1,), (0,)), ((), ())) if mode == "nn" else (((1,), (1,)), ((), ()))
    n_own = 2 * n_pairs + (bias is not None)
    n_ex = hosted.n if hosted else 0

    def body(*refs):
        o_ref = refs[n_own + n_ex]
        if hosted:
            j, i = pl.program_id(0), pl.program_id(1)
            begin_exchange, end_exchange = hosted.steps(
                refs[n_own:n_own + n_ex], refs[n_own + n_ex + 1:n_own + 2 * n_ex + 1], refs[n_own + 2 * n_ex + 1:],
                jnp.logical_and(j == 0, i == 0), jnp.logical_and(j == N // tn - 1, i == M // tm - 1))
            begin_exchange()
        acc = None
        for p in range(n_pairs):
            a = refs[2 * p][...]
            if silu_a:
                a = _silu(a.astype(F32))
            d = lax.dot_general(a.astype(BF16), refs[2 * p + 1][...].astype(BF16), dims, preferred_element_type=F32)
            acc = d if acc is None else acc + d
        if bias is not None:
            acc = acc + refs[2 * n_pairs][...]
        o_ref[...] = acc.astype(o_ref.dtype)
        if hosted:
            end_exchange()

    in_specs, args = [], []
    for a, b in pairs:
        K = a.shape[1]
        in_specs.append(pl.BlockSpec((tm, K), lambda j, i: (i, 0)))
        in_specs.append(pl.BlockSpec((K, tn), lambda j, i: (0, j)) if mode == "nn" else pl.BlockSpec((tn, K), lambda j, i: (j, 0)))
        args += [a, b]
    if bias is not None:
        in_specs.append(pl.BlockSpec((1, tn), lambda j, i: (0, j)))
        args.append(bias)
    out_spec = pl.BlockSpec((tm, tn), lambda j, i: (i, j))
    out_shape = jax.ShapeDtypeStruct((M, N), out_dtype)
    if not hosted:
        return pl.pallas_call(
            body, name=name, grid=(N // tn, M // tm), in_specs=in_specs, out_specs=out_spec, out_shape=out_shape,
            compiler_params=_cparams("arbitrary", "arbitrary"),
        )(*args)
    return pl.pallas_call(
        body, name=name, grid=(N // tn, M // tm), in_specs=in_specs + hosted.specs,
        out_specs=[out_spec] + hosted.specs, out_shape=[out_shape] + hosted.out_shape, scratch_shapes=hosted.scratch,
        compiler_params=_cparams("arbitrary", "arbitrary"),
    )(*args, *hosted.arrays)


def matmul_tn(name, a, b, out_dtype=F32, *, silu_a=False, tm=1408, tn=512, tk=2048):
    R, M = a.shape
    N = b.shape[1]
    tm = _tile(M, tm)
    tn = _tile(N, tn)
    tk = _tile(R, tk, 8)
    nk = R // tk

    def body(a_ref, b_ref, o_ref, acc):
        k = pl.program_id(2)

        @pl.when(k == 0)
        def _():
            acc[...] = jnp.zeros_like(acc)

        x = a_ref[...]
        if silu_a:
            x = _silu(x.astype(F32))
        acc[...] += lax.dot_general(x.astype(BF16), b_ref[...].astype(BF16), (((0,), (0,)), ((), ())),
                                    preferred_element_type=F32)

        @pl.when(k == nk - 1)
        def _():
            o_ref[...] = acc[...].astype(o_ref.dtype)

    return pl.pallas_call(
        body, name=name, grid=(M // tm, N // tn, nk),
        in_specs=[pl.BlockSpec((tk, tm), lambda i, j, k: (k, i)), pl.BlockSpec((tk, tn), lambda i, j, k: (k, j))],
        out_specs=pl.BlockSpec((tm, tn), lambda i, j, k: (i, j)),
        out_shape=jax.ShapeDtypeStruct((M, N), out_dtype),
        scratch_shapes=[pltpu.VMEM((tm, tn), F32)],
        compiler_params=_cparams("arbitrary", "arbitrary", "arbitrary"),
    )(a, b)


def _row_specs(rin, pbin, glin, tr):
    specs = [pl.BlockSpec((1, tr, w), lambda b, i, cb=cb, ro=ro, bo=(e[4] if len(e) > 4 else 0): (b + bo, i + ro, cb))
             for e in rin for (_, w, cb, ro) in [e[:4]]]
    specs += [pl.BlockSpec((1, 1, a.shape[-1]), lambda b, i: (b, 0, 0)) for a in pbin]
    specs += [pl.BlockSpec((1, a.shape[-1]), lambda b, i: (0, 0)) for a in glin]
    return specs


def rows_fwd(name, fn, nb, nblk, tr, rin, pbin, glin, outs):
    nr, npb, ngl = len(rin), len(pbin), len(glin)
    n_in = nr + npb + ngl

    def body(*refs):
        args = [r[0].astype(F32) for r in refs[:nr + npb]] + [r[...] for r in refs[nr + npb:n_in]]
        res = fn(*args)
        for o, v in zip(refs[n_in:], res):
            o[0] = v.astype(o.dtype)

    return pl.pallas_call(
        body, name=name, grid=(nb, nblk), in_specs=_row_specs(rin, pbin, glin, tr),
        out_specs=[pl.BlockSpec((1, tr, w), lambda b, i: (b, i, 0)) for (w, _) in outs],
        out_shape=[jax.ShapeDtypeStruct((nb, nblk * tr, w), dt) for (w, dt) in outs],
        compiler_params=_cparams("arbitrary", "arbitrary"),
    )(*[e[0] for e in rin], *pbin, *glin)


def rows_bwd(name, fn, nb, nblk, tr, rin, pbin, glin, cts, want):
    nr, npb, ngl, nct = len(rin), len(pbin), len(glin), len(cts)
    n_in = nr + npb + ngl

    def body(*refs):
        b, i = pl.program_id(0), pl.program_id(1)
        args = [r[0].astype(F32) for r in refs[:nr + npb]] + [r[...] for r in refs[nr + npb:n_in]]
        ct = tuple(r[0].astype(F32) for r in refs[n_in:n_in + nct])
        _, vjp = jax.vjp(fn, *args)
        g = vjp(ct)
        orefs = refs[n_in + nct:]
        for o, (idx, _) in zip(orefs, want):
            o[0] = g[idx].astype(o.dtype)
        pb_refs = orefs[len(want):len(want) + npb]
        gl_refs = orefs[len(want) + npb:]

        @pl.when(i == 0)
        def _():
            for o, v in zip(pb_refs, g[nr:nr + npb]):
                o[0] = v

        @pl.when(i > 0)
        def _():
            for o, v in zip(pb_refs, g[nr:nr + npb]):
                o[0] += v

        first = jnp.logical_and(b == 0, i == 0)

        @pl.when(first)
        def _():
            for o, v in zip(gl_refs, g[nr + npb:]):
                o[...] = v

        @pl.when(jnp.logical_not(first))
        def _():
            for o, v in zip(gl_refs, g[nr + npb:]):
                o[...] += v

    out_specs = [pl.BlockSpec((1, tr, rin[idx][1]), lambda b, i: (b, i, 0)) for (idx, _) in want]
    out_shape = [jax.ShapeDtypeStruct((nb, nblk * tr, rin[idx][1]), dt) for (idx, dt) in want]
    out_specs += [pl.BlockSpec((1, 1, a.shape[-1]), lambda b, i: (b, 0, 0)) for a in pbin]
    out_shape += [jax.ShapeDtypeStruct((nb, 1, a.shape[-1]), F32) for a in pbin]
    out_specs += [pl.BlockSpec((1, a.shape[-1]), lambda b, i: (0, 0)) for a in glin]
    out_shape += [jax.ShapeDtypeStruct((1, a.shape[-1]), F32) for a in glin]
    return pl.pallas_call(
        body, name=name, grid=(nb, nblk),
        in_specs=_row_specs(rin, pbin, glin, tr) + _row_specs(cts, [], [], tr),
        out_specs=out_specs, out_shape=out_shape,
        compiler_params=_cparams("arbitrary", "arbitrary"),
    )(*[e[0] for e in rin], *pbin, *glin, *[e[0] for e in cts])


def ew_call(name, fn, ins, outs):
    def body(*refs):
        res = fn(*[r[...] for r in refs[:len(ins)]])
        for o, v in zip(refs[len(ins):], res):
            o[...] = v.astype(o.dtype)

    return pl.pallas_call(body, name=name, out_shape=[jax.ShapeDtypeStruct(s, dt) for (s, dt) in outs])(*ins)


def fn_prenorm(x, shift, scale, g):
    return (_rms(x, g) * (1.0 + scale) + shift,)


def fn_rms(x, g):
    return (_rms(x, g),)


def fn_ssd_finish(yf, yr, xs, z, dexp, nw):
    y = yf + yr + dexp * xs
    return (_rms(y * _silu(z), nw),)


def fn_postmix(x, mix, gate1, scale2, shift2, post_g, pre_g):
    x1 = x + gate1 * _rms(mix, post_g)
    h2 = _rms(x1, pre_g) * (1.0 + scale2) + shift2
    return x1, h2


def final_call(x1, ffn, target, gate2, post_g, tr):
    nb, S, D = x1.shape
    nblk = S // tr

    def body(x1_ref, f_ref, t_ref, g2_ref, pg_ref, dx1_ref, df_ref, dg2_ref, dpg_ref, loss_ref):
        b, i = pl.program_id(0), pl.program_id(1)
        tgt = t_ref[0]

        def lossfn(x1v, fv, g2, pg):
            e = x1v + g2 * _rms(fv, pg) - tgt
            return 0.5 * jnp.sum(jnp.mean(e * e, axis=-1, keepdims=True))

        val, (dx1, df, dg2, dpg) = jax.value_and_grad(lossfn, argnums=(0, 1, 2, 3))(
            x1_ref[0], f_ref[0].astype(F32), g2_ref[0], pg_ref[...])
        dx1_ref[0] = dx1
        df_ref[0] = df.astype(df_ref.dtype)
        lv = jnp.full((1, LANE), val, F32)

        @pl.when(i == 0)
        def _():
            dg2_ref[0] = dg2

        @pl.when(i > 0)
        def _():
            dg2_ref[0] += dg2

        first = jnp.logical_and(b == 0, i == 0)

        @pl.when(first)
        def _():
            dpg_ref[...] = dpg
            loss_ref[...] = lv

        @pl.when(jnp.logical_not(first))
        def _():
            dpg_ref[...] += dpg
            loss_ref[...] += lv

    row = pl.BlockSpec((1, tr, D), lambda b, i: (b, i, 0))
    pb = pl.BlockSpec((1, 1, D), lambda b, i: (b, 0, 0))
    gl = pl.BlockSpec((1, D), lambda b, i: (0, 0))
    return pl.pallas_call(
        body, name="loss_head", grid=(nb, nblk), in_specs=[row, row, row, pb, gl],
        out_specs=[row, row, pb, gl, pl.BlockSpec((1, LANE), lambda b, i: (0, 0))],
        out_shape=[jax.ShapeDtypeStruct((nb, S, D), F32), jax.ShapeDtypeStruct((nb, S, D), BF16),
                   jax.ShapeDtypeStruct((nb, 1, D), F32), jax.ShapeDtypeStruct((1, D), F32),
                   jax.ShapeDtypeStruct((1, LANE), F32)],
        compiler_params=_cparams("arbitrary", "arbitrary"),
    )(x1, ffn, target, gate2, post_g)


def _rotate_half(t):
    lane = lax.broadcasted_iota(jnp.int32, t.shape, 1)
    return jnp.where((lane & 15) < 8, -pltpu.roll(t, LANE - 8, 1), pltpu.roll(t, 8, 1))


def rope_call(name, x, width, colblk, cos, sin, out_dtype, tr):
    nb = x.shape[0]
    R = cos.shape[0]
    nblk = R // tr

    def body(x_ref, c_ref, s_ref, o_ref):
        c, s = c_ref[...], s_ref[...]
        for h in range(width // LANE):
            t = x_ref[0, :, h * LANE:(h + 1) * LANE].astype(F32)
            o_ref[0, :, h * LANE:(h + 1) * LANE] = (t * c + _rotate_half(t) * s).astype(o_ref.dtype)

    tab = pl.BlockSpec((tr, LANE), lambda b, i: (i, 0))
    return pl.pallas_call(
        body, name=name, grid=(nb, nblk),
        in_specs=[pl.BlockSpec((1, tr, width), lambda b, i: (b, i, colblk)), tab, tab],
        out_specs=pl.BlockSpec((1, tr, width), lambda b, i: (b, i, 0)),
        out_shape=jax.ShapeDtypeStruct((nb, R, width), out_dtype),
        compiler_params=_cparams("arbitrary", "arbitrary"),
    )(x, cos, sin)


def rope_tables(n_ctx, seq):
    n_rows = seq // GRID_W
    row = np.repeat(np.arange(n_rows), GRID_W).astype(np.float32)
    col = np.tile(np.arange(GRID_W), n_rows).astype(np.float32)
    axis_dim = ROPE // 2
    inv_freq = jnp.asarray(ROPE_THETA, F32) ** (-jnp.arange(0, axis_dim, 2, dtype=F32) / axis_dim)
    ang_r = jnp.asarray(row)[:, None] * inv_freq
    ang_c = jnp.asarray(col)[:, None] * inv_freq
    ang = jnp.concatenate([ang_r, ang_r, ang_c, ang_c], axis=-1)
    cos = jnp.ones((n_ctx + seq, LANE), F32).at[n_ctx:, KR_LANE:KR_LANE + ROPE].set(jnp.cos(ang))
    sin = jnp.zeros((n_ctx + seq, LANE), F32).at[n_ctx:, KR_LANE:KR_LANE + ROPE].set(jnp.sin(ang))
    return cos, sin


Q_PRESCALE = ATTN_SCALE * math.log2(math.e)


def _attn_weights(q, kc):
    s2 = lax.dot_general(q, kc, (((1,), (1,)), ((), ())), preferred_element_type=F32)
    e = jnp.exp2(s2 - jnp.max(s2, axis=1, keepdims=True))
    return e, 1.0 / jnp.sum(e, axis=1, keepdims=True)


def _key_block(kv, kr):
    lane = lax.broadcasted_iota(jnp.int32, kv.shape, 1)
    return jnp.where(lane < NOPE, kv, kr)


def attn_fwd(q, kv, kr, tq):
    nb, S, _ = q.shape
    T = kv.shape[1]

    def body(q_ref, kv_ref, kr_ref, o_ref):
        kvv = kv_ref[0]
        e, r = _attn_weights(q_ref[0], _key_block(kvv, kr_ref[0]))
        o = lax.dot_general(e.astype(BF16), kvv, (((1,), (0,)), ((), ())), preferred_element_type=F32) * r
        lane = lax.broadcasted_iota(jnp.int32, o.shape, 1)
        o_ref[0] = jnp.where(lane >= NOPE, o, 0.0).astype(o_ref.dtype)

    return pl.pallas_call(
        body, name="attn_fwd", grid=(nb, N_HEADS, S // tq),
        in_specs=[pl.BlockSpec((1, tq, HEAD_BLOCK), lambda b, h, i: (b, i, h)),
                  pl.BlockSpec((1, T, HEAD_BLOCK), lambda b, h, i: (b, 0, h)),
                  pl.BlockSpec((1, T, HEAD_BLOCK), lambda b, h, i: (b, 0, 0))],
        out_specs=pl.BlockSpec((1, tq, HEAD_BLOCK), lambda b, h, i: (b, i, h)),
        out_shape=jax.ShapeDtypeStruct((nb, S, QP), BF16),
        compiler_params=_cparams("arbitrary", "arbitrary", "arbitrary"),
    )(q, kv, kr)


def attn_bwd(q, kv, kr, do, tq):
    nb, S, _ = q.shape
    T = kv.shape[1]

    def body(q_ref, kv_ref, kr_ref, do_ref, dq_ref, dkv_ref, dkr_ref):
        h, i = pl.program_id(1), pl.program_id(2)
        qv, kvv, dov = q_ref[0], kv_ref[0], do_ref[0]
        kc = _key_block(kvv, kr_ref[0])
        e, r = _attn_weights(qv, kc)
        dor = (dov.astype(F32) * r).astype(BF16)
        dpr = lax.dot_general(dor, kvv, (((1,), (1,)), ((), ())), preferred_element_type=F32)
        ds = (e * (dpr - r * jnp.sum(dpr * e, axis=1, keepdims=True))).astype(BF16)
        dq = lax.dot_general(ds, kc, (((1,), (0,)), ((), ())), preferred_element_type=F32)
        dq_ref[0] = (dq * ATTN_SCALE).astype(dq_ref.dtype)
        dkc = lax.dot_general(ds, qv, (((0,), (0,)), ((), ())), preferred_element_type=F32) * math.log(2.0)
        dv = lax.dot_general(e.astype(BF16), dor, (((0,), (0,)), ((), ())), preferred_element_type=F32)
        lane = lax.broadcasted_iota(jnp.int32, dkc.shape, 1)
        dkv = jnp.where(lane < NOPE, dkc, dv)
        dkr = jnp.where(lane >= NOPE, dkc, 0.0)

        @pl.when(i == 0)
        def _():
            dkv_ref[0] = dkv

        @pl.when(i > 0)
        def _():
            dkv_ref[0] += dkv

        first = jnp.logical_and(h == 0, i == 0)

        @pl.when(first)
        def _():
            dkr_ref[0] = dkr

        @pl.when(jnp.logical_not(first))
        def _():
            dkr_ref[0] += dkr

    qspec = pl.BlockSpec((1, tq, HEAD_BLOCK), lambda b, h, i: (b, i, h))
    kspec = pl.BlockSpec((1, T, HEAD_BLOCK), lambda b, h, i: (b, 0, h))
    rspec = pl.BlockSpec((1, T, HEAD_BLOCK), lambda b, h, i: (b, 0, 0))
    return pl.pallas_call(
        body, name="attn_bwd", grid=(nb, N_HEADS, S // tq),
        in_specs=[qspec, kspec, rspec, qspec], out_specs=[qspec, kspec, rspec],
        out_shape=[jax.ShapeDtypeStruct((nb, S, QP), F32), jax.ShapeDtypeStruct((nb, T, QP), F32),
                   jax.ShapeDtypeStruct((nb, T, HEAD_BLOCK), F32)],
        compiler_params=_cparams("arbitrary", "arbitrary", "arbitrary"),
    )(q, kv, kr, do)


def _seg_bounds(n, n_ctx):
    t = lax.broadcasted_iota(jnp.int32, (n, 1), 0)
    if n_ctx == 0:
        return t, jnp.zeros_like(t), jnp.full_like(t, n)
    in_ctx = t < n_ctx
    return t, jnp.where(in_ctx, 0, n_ctx), jnp.where(in_ctx, n_ctx, n)


def _shift_rows(x, o, bounds):
    if o == 0:
        return x
    t, lo, hi = bounds
    n = x.shape[0]
    valid = jnp.logical_and(t + o >= lo, t + o < hi).astype(F32)
    return pltpu.roll(x, (-o) % n, 0) * valid


def _conv(x, w, bias, k, bounds):
    acc = bias
    for o in range(k):
        acc = acc + w[o:o + 1, :] * _shift_rows(x, o - k // 2, bounds)
    return acc


def _conv_bwd(x, w, dpre, k, bounds):
    dx = jnp.zeros_like(x)
    rows = []
    for o in range(k):
        dx = dx + w[o:o + 1, :] * _shift_rows(dpre, -(o - k // 2), bounds)
        rows.append(jnp.sum(dpre * _shift_rows(x, o - k // 2, bounds), axis=0, keepdims=True))
    rows.append(jnp.sum(dpre, axis=0, keepdims=True))
    sub8 = lax.broadcasted_iota(jnp.int32, (8, x.shape[1]), 0)
    out = jnp.zeros((8, x.shape[1]), F32)
    for o, r in enumerate(rows):
        out = out + jnp.where(sub8 == o, r, 0.0)
    return dx, out


def _gelu(x):
    return 0.5 * x * (1.0 + lax.erf(x * (1.0 / math.sqrt(2.0))))


def _gelu_grad(x):
    return 0.5 * (1.0 + lax.erf(x * (1.0 / math.sqrt(2.0)))) + x * jnp.exp(-0.5 * x * x) * (1.0 / math.sqrt(2.0 * math.pi))


def ssd_conv_fwd(u, w8, bias, n_ctx, tc):
    nb, T, _ = u.shape
    cb0 = OFF_XBC // tc

    def body(x_ref, w_ref, b_ref, o_ref):
        pre = _conv(x_ref[0], w_ref[...], b_ref[...], SSD_K, _seg_bounds(T, n_ctx))
        o_ref[0] = _silu(pre)

    return pl.pallas_call(
        body, name="ssd_conv_fwd", grid=(nb, XBC // tc),
        in_specs=[pl.BlockSpec((1, T, tc), lambda b, j: (b, 0, cb0 + j)),
                  pl.BlockSpec((8, tc), lambda b, j: (0, j)), pl.BlockSpec((1, tc), lambda b, j: (0, j))],
        out_specs=pl.BlockSpec((1, T, tc), lambda b, j: (b, 0, j)),
        out_shape=jax.ShapeDtypeStruct((nb, T, XBC), F32),
        compiler_params=_cparams("arbitrary", "arbitrary"),
    )(u, w8, bias)


def ssd_conv_bwd(u, w8, bias, dxbc, dxs_direct, n_ctx, tc):
    nb, T, _ = u.shape
    cb0 = OFF_XBC // tc
    n_direct = D_INNER // tc

    def body(x_ref, w_ref, b_ref, d0_ref, d1_ref, dd_ref, dx_ref, dw_ref, acc):
        j, b = pl.program_id(0), pl.program_id(1)
        acc[...] = d0_ref[0, 0] + d1_ref[0, 0]

        @pl.when(j < n_direct)
        def _():
            acc[n_ctx:, :] += dd_ref[0]

        bounds = _seg_bounds(T, n_ctx)
        x, w = x_ref[0], w_ref[...]
        pre = _conv(x, w, b_ref[...], SSD_K, bounds)
        sg = jax.nn.sigmoid(pre)
        dpre = acc[...] * (sg * (1.0 + pre * (1.0 - sg)))
        dx, rows = _conv_bwd(x, w, dpre, SSD_K, bounds)
        dx_ref[0] = dx.astype(dx_ref.dtype)

        @pl.when(b == 0)
        def _():
            dw_ref[...] = rows

        @pl.when(b > 0)
        def _():
            dw_ref[...] += rows

    dspec0 = pl.BlockSpec((1, 1, T, tc), lambda j, b: (0, b, 0, j))
    dspec1 = pl.BlockSpec((1, 1, T, tc), lambda j, b: (1, b, 0, j))
    return pl.pallas_call(
        body, name="ssd_conv_bwd", grid=(XBC // tc, nb),
        in_specs=[pl.BlockSpec((1, T, tc), lambda j, b: (b, 0, cb0 + j)),
                  pl.BlockSpec((8, tc), lambda j, b: (0, j)), pl.BlockSpec((1, tc), lambda j, b: (0, j)),
                  dspec0, dspec1,
                  pl.BlockSpec((1, T - n_ctx, tc), lambda j, b: (b, 0, jnp.minimum(j, n_direct - 1)))],
        out_specs=[pl.BlockSpec((1, T, tc), lambda j, b: (b, 0, j)), pl.BlockSpec((8, tc), lambda j, b: (0, j))],
        out_shape=[jax.ShapeDtypeStruct((nb, T, XBC), BF16), jax.ShapeDtypeStruct((8, XBC), F32)],
        scratch_shapes=[pltpu.VMEM((T, tc), F32)],
        compiler_params=_cparams("arbitrary", "arbitrary"),
    )(u, w8, bias, dxbc, dxbc, dxs_direct)


GLU_TC = 256


def glu_interleave(w_up):
    blocks = []
    for j in range(D_FF // GLU_TC):
        blocks += [w_up[:, j * GLU_TC:(j + 1) * GLU_TC], w_up[:, D_FF + j * GLU_TC:D_FF + (j + 1) * GLU_TC]]
    return jnp.concatenate(blocks, axis=1)


def glu_deinterleave(g):
    nj = D_FF // GLU_TC
    gate = [g[:, 2 * j * GLU_TC:(2 * j + 1) * GLU_TC] for j in range(nj)]
    val = [g[:, (2 * j + 1) * GLU_TC:(2 * j + 2) * GLU_TC] for j in range(nj)]
    return jnp.concatenate(gate + val, axis=1)


def glu_fwd(up, w8, bias):
    nb, S, _ = up.shape
    tc = GLU_TC

    def body(u_ref, w_ref, b_ref, o_ref):
        gc = _conv(u_ref[0, :, :tc], w_ref[...], b_ref[...], FFN_K, _seg_bounds(S, 0))
        o_ref[0] = (_gelu(gc) * u_ref[0, :, tc:]).astype(o_ref.dtype)

    return pl.pallas_call(
        body, name="glu_fwd", grid=(nb, D_FF // tc),
        in_specs=[pl.BlockSpec((1, S, 2 * tc), lambda b, j: (b, 0, j)),
                  pl.BlockSpec((8, tc), lambda b, j: (0, j)), pl.BlockSpec((1, tc), lambda b, j: (0, j))],
        out_specs=pl.BlockSpec((1, S, tc), lambda b, j: (b, 0, j)),
        out_shape=jax.ShapeDtypeStruct((nb, S, D_FF), BF16),
        compiler_params=_cparams("arbitrary", "arbitrary"),
    )(up, w8, bias)


def glu_bwd(up, w8, bias, dact):
    nb, S, _ = up.shape
    tc = GLU_TC

    def body(u_ref, w_ref, b_ref, d_ref, du_ref, dw_ref):
        b = pl.program_id(1)
        bounds = _seg_bounds(S, 0)
        x, w, val, d = u_ref[0, :, :tc], w_ref[...], u_ref[0, :, tc:], d_ref[0].astype(F32)
        gc = _conv(x, w, b_ref[...], FFN_K, bounds)
        du_ref[0, :, tc:] = (d * _gelu(gc)).astype(du_ref.dtype)
        dx, rows = _conv_bwd(x, w, d * val * _gelu_grad(gc), FFN_K, bounds)
        du_ref[0, :, :tc] = dx.astype(du_ref.dtype)

        @pl.when(b == 0)
        def _():
            dw_ref[...] = rows

        @pl.when(b > 0)
        def _():
            dw_ref[...] += rows

    pair = pl.BlockSpec((1, S, 2 * tc), lambda j, b: (b, 0, j))
    return pl.pallas_call(
        body, name="glu_bwd", grid=(D_FF // tc, nb),
        in_specs=[pair, pl.BlockSpec((8, tc), lambda j, b: (0, j)), pl.BlockSpec((1, tc), lambda j, b: (0, j)),
                  pl.BlockSpec((1, S, tc), lambda j, b: (b, 0, j))],
        out_specs=[pair, pl.BlockSpec((8, tc), lambda j, b: (0, j))],
        out_shape=[jax.ShapeDtypeStruct((nb, S, 2 * D_FF), BF16), jax.ShapeDtypeStruct((8, D_FF), F32)],
        compiler_params=_cparams("arbitrary", "arbitrary"),
    )(up, w8, bias, dact)


def _chunk_of(d, k, n_cc, n_ch):
    rev = jnp.where(k < n_cc, n_cc - 1 - k, n_cc + n_ch - 1 - k)
    return jnp.where(d == 1, rev, k)


def _lane_pick(v, lane_iota, l):
    return jnp.sum(jnp.where(lane_iota == l, v, 0.0), axis=1, keepdims=True)


def head_spread_matrix():
    return (jnp.arange(LANE)[:, None] == (jnp.arange(D_INNER)[None, :] // SSD_P)).astype(BF16)


def _split_dot(x, e, dims):
    hi = x.astype(BF16)
    lo = (x - hi.astype(F32)).astype(BF16)
    return (lax.dot_general(hi, e, dims, preferred_element_type=F32)
            + lax.dot_general(lo, e, dims, preferred_element_type=F32))


def _spread(x, e):
    return _split_dot(x, e, (((1,), (0,)), ((), ())))


def _gather_heads(y, e):
    return _split_dot(y, e, (((1,), (1,)), ((), ())))


def _softplus(x):
    return jnp.maximum(x, 0.0) + jnp.log(1.0 + jnp.exp(-jnp.abs(x)))


def ssd_dt_inputs(u, a_log, dt_bias):
    pad = LANE - SSD_HEADS
    dt = u[..., OFF_DT:OFF_DT + 2 * SSD_HEADS]
    dt2 = jnp.stack([jnp.pad(dt[..., i * SSD_HEADS:(i + 1) * SSD_HEADS], ((0, 0), (0, 0), (0, pad))) for i in range(2)])

    def lanes(v):
        return jnp.pad(v.reshape(2, 1, SSD_HEADS), ((0, 0), (0, 0), (0, pad)))

    return dt2, lanes(a_log), lanes(dt_bias)


def _ssd_common(d, dt_raw, alog, dtb):
    Q = dt_raw.shape[0]
    row = lax.broadcasted_iota(jnp.int32, (Q, Q), 0)
    col = lax.broadcasted_iota(jnp.int32, (Q, Q), 1)
    rev = d == 1
    maskb = jnp.where(rev, row, col) <= jnp.where(rev, col, row)
    tri = maskb.astype(F32)
    A = -jnp.exp(alog)
    dtv = _softplus(dt_raw + dtb)
    a = dtv * A
    cum = lax.dot_general(tri, a, (((1,), (0,)), ((), ())), precision=lax.Precision.HIGHEST, preferred_element_type=F32)
    tot = jnp.sum(a, axis=0, keepdims=True)
    return maskb, tri, A, dtv, cum, tot


def ssd_fwd(xbc, dt2, alog2, dtb2, n_ctx, hosted):
    nb, T, _ = xbc.shape
    S = T - n_ctx
    n_ch, n_cc = T // CHUNK, n_ctx // CHUNK
    Q = CHUNK
    n_pairs = SSD_HEADS // 2
    n_ex = hosted.n
    n_in = 5

    def body(*refs):
        x_ref, dt_ref, al_ref, db_ref, e_ref = refs[:n_in]
        send_refs = refs[n_in:n_in + n_ex]
        y_ref, hin_ref = refs[n_in + n_ex:n_in + 2 + n_ex]
        recv_refs = refs[n_in + 2 + n_ex:n_in + 2 + 2 * n_ex]
        H, *sems = refs[n_in + 2 + 2 * n_ex:]
        d, k = pl.program_id(1), pl.program_id(2)
        first_step = jnp.logical_and(jnp.logical_and(pl.program_id(0) == 0, d == 0), k == 0)
        last_step = jnp.logical_and(jnp.logical_and(pl.program_id(0) == nb - 1, d == 1), k == n_ch - 1)
        begin_exchange, end_exchange = hosted.steps(send_refs, recv_refs, sems, first_step, last_step)
        begin_exchange()

        @pl.when(k == 0)
        def _():
            H[...] = jnp.zeros_like(H)

        maskb, tri, A, dtv, cum, tot = _ssd_common(d, dt_ref[0, 0], al_ref[0], db_ref[0])
        e = e_ref[...]
        cumT = cum.T
        cum_e, dt_e = _spread(cum, e), _spread(dtv, e)
        tot_e = _spread(jnp.broadcast_to(tot, (8, LANE)), e)[0:1]
        hin_ref[0, 0, 0] = H[...].astype(BF16)
        lane = lax.broadcasted_iota(jnp.int32, (Q, LANE), 1)
        lane1 = lax.broadcasted_iota(jnp.int32, (1, LANE), 1)
        subc = lax.broadcasted_iota(jnp.int32, (LANE, 1), 0)
        half = lane < SSD_P
        for g in range(SSD_GROUPS):
            Bg = x_ref[0, :, D_INNER + g * SSD_N:D_INNER + (g + 1) * SSD_N].astype(BF16)
            Cg = x_ref[0, :, D_INNER + GN + g * SSD_N:D_INNER + GN + (g + 1) * SSD_N].astype(BF16)
            Gm = lax.dot_general(Cg, Bg, (((1,), (1,)), ((), ())), preferred_element_type=F32)
            for pr in range(n_pairs // SSD_GROUPS):
                p = g * (n_pairs // SSD_GROUPS) + pr
                sc, dtp, totp = [t[:, p * LANE:(p + 1) * LANE] for t in (cum_e, dt_e, tot_e)]
                swapped = pltpu.roll(sc, SSD_P, 1)
                s0c, s1c = jnp.where(half, sc, swapped), jnp.where(half, swapped, sc)
                s0r, s1r = cumT[2 * p:2 * p + 1, :], cumT[2 * p + 1:2 * p + 2, :]
                tot0, tot1 = _lane_pick(tot, lane1, 2 * p), _lane_pick(tot, lane1, 2 * p + 1)
                M0 = (Gm * jnp.exp(jnp.where(maskb, s0c - s0r, NEG_BIG))).astype(BF16)
                M1 = (Gm * jnp.exp(jnp.where(maskb, s1c - s1r, NEG_BIG))).astype(BF16)
                xd = x_ref[0, :, p * LANE:(p + 1) * LANE] * dtp
                xdb = xd.astype(BF16)
                yd = jnp.where(half,
                               lax.dot_general(M0, xdb, (((1,), (0,)), ((), ())), preferred_element_type=F32),
                               lax.dot_general(M1, xdb, (((1,), (0,)), ((), ())), preferred_element_type=F32))
                Hp = H[p * LANE:(p + 1) * LANE, :]
                yo = lax.dot_general(Cg, Hp.astype(BF16), (((1,), (1,)), ((), ())), preferred_element_type=F32) * jnp.exp(sc)

                @pl.when(k >= n_cc)
                def _():
                    y_ref[0, 0, :, p * LANE:(p + 1) * LANE] = yd + yo

                xdw = (xd * jnp.exp(totp - sc)).astype(BF16)
                etot = jnp.exp(jnp.where(subc < SSD_P, tot0, tot1))
                H[p * LANE:(p + 1) * LANE, :] = Hp * etot + lax.dot_general(
                    xdw, Bg, (((0,), (0,)), ((), ())), preferred_element_type=F32)
        end_exchange()

    def ymap(b, d, k):
        return (d, b, _chunk_of(d, jnp.maximum(k, n_cc), n_cc, n_ch) - n_cc, 0)

    return pl.pallas_call(
        body, name="ssd_fwd", grid=(nb, 2, n_ch),
        in_specs=[pl.BlockSpec((1, Q, XBC), lambda b, d, k: (b, _chunk_of(d, k, n_cc, n_ch), 0)),
                  pl.BlockSpec((1, 1, Q, LANE), lambda b, d, k: (d, b, _chunk_of(d, k, n_cc, n_ch), 0)),
                  pl.BlockSpec((1, 1, LANE), lambda b, d, k: (d, 0, 0)), pl.BlockSpec((1, 1, LANE), lambda b, d, k: (d, 0, 0)),
                  pl.BlockSpec((LANE, D_INNER), lambda b, d, k: (0, 0))] + hosted.specs,
        out_specs=[pl.BlockSpec((1, 1, Q, D_INNER), ymap),
                   pl.BlockSpec((1, 1, 1, D_INNER, SSD_N), lambda b, d, k: (d, b, k, 0, 0))] + hosted.specs,
        out_shape=[jax.ShapeDtypeStruct((2, nb, S, D_INNER), F32),
                   jax.ShapeDtypeStruct((2, nb, n_ch, D_INNER, SSD_N), BF16)] + hosted.out_shape,
        scratch_shapes=[pltpu.VMEM((D_INNER, SSD_N), F32)] + hosted.scratch,
        compiler_params=_cparams("arbitrary", "arbitrary", "arbitrary"),
    )(xbc, dt2, alog2, dtb2, head_spread_matrix(), *hosted.arrays)


def ssd_bwd(xbc, dt2, alog2, dtb2, hin, dy, n_ctx, hosted):
    nb, T, _ = xbc.shape
    n_ex = hosted.n
    n_ch, n_cc = T // CHUNK, n_ctx // CHUNK
    n_in = 7
    Q = CHUNK
    n_pairs = SSD_HEADS // 2
    NT = (((1,), (1,)), ((), ()))
    NN = (((1,), (0,)), ((), ()))
    TN = (((0,), (0,)), ((), ()))

    def dot(a, b, dims):
        return lax.dot_general(a.astype(BF16), b.astype(BF16), dims, preferred_element_type=F32)

    def body(*refs):
        x_ref, dt_ref, al_ref, db_ref, e_ref, hin_ref, dy_ref = refs[:n_in]
        send_refs = refs[n_in:n_in + n_ex]
        dx_ref, ddt_ref, st_ref = refs[n_in + n_ex:n_in + 3 + n_ex]
        recv_refs = refs[n_in + 3 + n_ex:n_in + 3 + 2 * n_ex]
        dH, dce, dde, *sems = refs[n_in + 3 + 2 * n_ex:]
        d, kk = pl.program_id(1), pl.program_id(2)
        ks = n_ch - 1 - kk
        first_step = jnp.logical_and(jnp.logical_and(pl.program_id(0) == 0, d == 0), kk == 0)
        last_step = jnp.logical_and(jnp.logical_and(pl.program_id(0) == nb - 1, d == 1), kk == n_ch - 1)
        begin_exchange, end_exchange = hosted.steps(send_refs, recv_refs, sems, first_step, last_step)
        begin_exchange()

        @pl.when(kk == 0)
        def _():
            dH[...] = jnp.zeros_like(dH)

        @pl.when(jnp.logical_and(jnp.logical_and(pl.program_id(0) == 0, d == 0), kk == 0))
        def _():
            st_ref[...] = jnp.zeros_like(st_ref)

        dt_raw = dt_ref[0, 0]
        alog, dtb_v = al_ref[0], db_ref[0]
        maskb, tri, A, dtv, cum, tot = _ssd_common(d, dt_raw, alog, dtb_v)
        e = e_ref[...]
        cumT = cum.T
        cum_e, dt_e = _spread(cum, e), _spread(dtv, e)
        tot_e = _spread(jnp.broadcast_to(tot, (8, LANE)), e)[0:1]
        live = (ks >= n_cc).astype(F32)
        lane = lax.broadcasted_iota(jnp.int32, (Q, LANE), 1)
        lane1 = lax.broadcasted_iota(jnp.int32, (1, LANE), 1)
        sub = lax.broadcasted_iota(jnp.int32, (LANE, Q), 0)
        subc = lax.broadcasted_iota(jnp.int32, (LANE, 1), 0)
        half = lane < SSD_P
        halfc = subc < SSD_P
        ones = jnp.ones((LANE, LANE), BF16)
        dcum = jnp.zeros((Q, LANE), F32)
        dcumT = jnp.zeros((LANE, Q), F32)
        dtot = jnp.zeros((1, LANE), F32)
        dtot_parts = []
        for g in range(SSD_GROUPS):
            Bg = x_ref[0, :, D_INNER + g * SSD_N:D_INNER + (g + 1) * SSD_N].astype(BF16)
            Cg = x_ref[0, :, D_INNER + GN + g * SSD_N:D_INNER + GN + (g + 1) * SSD_N].astype(BF16)
            Gm = lax.dot_general(Cg, Bg, NT, preferred_element_type=F32)
            dG = jnp.zeros((Q, Q), F32)
            dC = jnp.zeros((Q, SSD_N), F32)
            dB = jnp.zeros((Q, SSD_N), F32)
            for pr in range(n_pairs // SSD_GROUPS):
                p = g * (n_pairs // SSD_GROUPS) + pr
                l0, l1 = 2 * p, 2 * p + 1
                sc, dtp, totp = [t[:, p * LANE:(p + 1) * LANE] for t in (cum_e, dt_e, tot_e)]
                swapped = pltpu.roll(sc, SSD_P, 1)
                s0c, s1c = jnp.where(half, sc, swapped), jnp.where(half, swapped, sc)
                s0r, s1r = cumT[l0:l0 + 1, :], cumT[l1:l1 + 1, :]
                tot0, tot1 = _lane_pick(tot, lane1, l0), _lane_pick(tot, lane1, l1)
                L0 = jnp.exp(jnp.where(maskb, s0c - s0r, NEG_BIG))
                L1 = jnp.exp(jnp.where(maskb, s1c - s1r, NEG_BIG))
                M0, M1 = Gm * L0, Gm * L1
                xs = x_ref[0, :, p * LANE:(p + 1) * LANE]
                xd = xs * dtp
                es = jnp.exp(sc)
                dte = jnp.exp(totp - sc)
                etot = jnp.exp(jnp.where(halfc, tot0, tot1))
                dyp = dy_ref[0, :, p * LANE:(p + 1) * LANE] * live
                Hp = hin_ref[0, 0, 0, p * LANE:(p + 1) * LANE, :]
                dHp = dH[p * LANE:(p + 1) * LANE, :]
                bdh = dot(Bg, dHp, NT)
                dxd = jnp.where(half, dot(M0, dyp, TN), dot(M1, dyp, TN)) + bdh * dte
                dy0 = jnp.where(half, dyp, 0.0)
                dy1 = dyp - dy0
                dM0, dM1 = dot(dy0, xd, NT), dot(dy1, xd, NT)
                dG = dG + dM0 * L0 + dM1 * L1
                dyes = dyp * es
                xdw = xd * dte
                dC = dC + dot(dyes, Hp, NN)
                dB = dB + dot(xdw, dHp, NN)
                W0, W1 = dM0 * M0, dM1 * M1
                yoff = dot(Cg, Hp, NT) * es
                r_off = dyp * yoff
                r_st = xd * bdh * dte
                hh = jnp.sum(dHp * Hp.astype(F32), axis=1, keepdims=True) * etot
                dce[:, p * LANE:(p + 1) * LANE] = r_off - r_st
                dde[:, p * LANE:(p + 1) * LANE] = dxd * xs
                dtot_parts.append(jnp.sum(r_st, axis=0, keepdims=True))
                for (l, W, hselc) in ((l0, W0, halfc), (l1, W1, jnp.logical_not(halfc))):
                    col_g = _split_dot(W, ones, NN)
                    row_g = -jnp.sum(W, axis=0, keepdims=True)
                    dcum = dcum + jnp.where(lane == l, col_g, 0.0)
                    dcumT = dcumT + jnp.where(sub == l, row_g, 0.0)
                    dtot = dtot + jnp.where(lane1 == l, jnp.sum(jnp.where(hselc, hh, 0.0), axis=0, keepdims=True), 0.0)
                dx_ref[0, 0, :, p * LANE:(p + 1) * LANE] = dxd * dtp
                dH[p * LANE:(p + 1) * LANE, :] = dHp * etot + dot(dyes, Cg, TN)
            dx_ref[0, 0, :, D_INNER + g * SSD_N:D_INNER + (g + 1) * SSD_N] = dB + dot(dG, Cg, TN)
            dx_ref[0, 0, :, D_INNER + GN + g * SSD_N:D_INNER + GN + (g + 1) * SSD_N] = dC + dot(dG, Bg, NN)
        dcum_all = dcum + dcumT.T + _gather_heads(dce[...], e)
        dtot_e = jnp.broadcast_to(jnp.concatenate(dtot_parts, axis=1), (8, D_INNER))
        dtot = dtot + _gather_heads(dtot_e, e)[0:1]
        da = lax.dot_general(tri, dcum_all, TN, precision=lax.Precision.HIGHEST, preferred_element_type=F32) + dtot
        ddtv = _gather_heads(dde[...], e) + da * A
        ddt_raw = ddtv * jax.nn.sigmoid(dt_raw + dtb_v)
        ddt_ref[0, 0] = ddt_raw
        sub8 = lax.broadcasted_iota(jnp.int32, (8, LANE), 0)
        st_ref[...] += (jnp.where(sub8 == 2 * d, jnp.sum(da * dtv * A, axis=0, keepdims=True), 0.0)
                        + jnp.where(sub8 == 2 * d + 1, jnp.sum(ddt_raw, axis=0, keepdims=True), 0.0))
        end_exchange()

    def cmap(d, kk):
        return _chunk_of(d, n_ch - 1 - kk, n_cc, n_ch)

    def dymap(b, d, kk):
        return (b, _chunk_of(d, jnp.maximum(n_ch - 1 - kk, n_cc), n_cc, n_ch) - n_cc, 0)

    return pl.pallas_call(
        body, name="ssd_bwd", grid=(nb, 2, n_ch),
        in_specs=[pl.BlockSpec((1, Q, XBC), lambda b, d, kk: (b, cmap(d, kk), 0)),
                  pl.BlockSpec((1, 1, Q, LANE), lambda b, d, kk: (d, b, cmap(d, kk), 0)),
                  pl.BlockSpec((1, 1, LANE), lambda b, d, kk: (d, 0, 0)), pl.BlockSpec((1, 1, LANE), lambda b, d, kk: (d, 0, 0)),
                  pl.BlockSpec((LANE, D_INNER), lambda b, d, kk: (0, 0)),
                  pl.BlockSpec((1, 1, 1, D_INNER, SSD_N), lambda b, d, kk: (d, b, n_ch - 1 - kk, 0, 0)),
                  pl.BlockSpec((1, Q, D_INNER), dymap)] + hosted.specs,
        out_specs=[pl.BlockSpec((1, 1, Q, XBC), lambda b, d, kk: (d, b, cmap(d, kk), 0)),
                   pl.BlockSpec((1, 1, Q, LANE), lambda b, d, kk: (d, b, cmap(d, kk), 0)),
                   pl.BlockSpec((8, LANE), lambda b, d, kk: (0, 0))] + hosted.specs,
        out_shape=[jax.ShapeDtypeStruct((2, nb, T, XBC), F32), jax.ShapeDtypeStruct((2, nb, T, LANE), F32),
                   jax.ShapeDtypeStruct((8, LANE), F32)] + hosted.out_shape,
        scratch_shapes=[pltpu.VMEM((D_INNER, SSD_N), F32), pltpu.VMEM((Q, D_INNER), F32), pltpu.VMEM((Q, D_INNER), F32)] + hosted.scratch,
        compiler_params=_cparams("arbitrary", "arbitrary", "arbitrary"),
    )(xbc, dt2, alog2, dtb2, head_spread_matrix(), hin, dy, *hosted.arrays)


def _adamw(w, g, m, v):
    mn = ADAM_B1 * m + (1.0 - ADAM_B1) * g
    vn = ADAM_B2 * v + (1.0 - ADAM_B2) * jnp.square(g)
    m_hat = mn / (1.0 - ADAM_B1 ** ADAM_STEP)
    v_hat = vn / (1.0 - ADAM_B2 ** ADAM_STEP)
    return -ADAM_LR * (m_hat / (jnp.sqrt(v_hat) + ADAM_EPS) + ADAM_WD * w), mn, vn


def adamw_matrix(name, w, g_slots, m, v):
    K, n = w.shape
    s = g_slots.shape[0]
    tr = _tile(K, 256, 8)

    def body(w_ref, g_ref, m_ref, v_ref, go_ref, d_ref, mo_ref, vo_ref):
        g = g_ref[0]
        for j in range(1, s):
            g = g + g_ref[j]
        go_ref[...] = g
        d_ref[...], mo_ref[...], vo_ref[...] = _adamw(w_ref[...], g, m_ref[...], v_ref[...])

    spec = pl.BlockSpec((tr, n), lambda i: (i, 0))
    return pl.pallas_call(
        body, name=name, grid=(K // tr,),
        in_specs=[spec, pl.BlockSpec((s, tr, n), lambda i: (0, i, 0)), spec, spec], out_specs=[spec] * 4,
        out_shape=[jax.ShapeDtypeStruct((K, n), F32)] * 4,
        compiler_params=_cparams("arbitrary"),
    )(w, g_slots, m, v)


def adamw_small(ws, gs, ms, vs):
    n = len(ws)

    def body(*refs):
        for i in range(n):
            d, mn, vn = _adamw(refs[i][...], refs[n + i][...], refs[2 * n + i][...], refs[3 * n + i][...])
            refs[4 * n + i][...] = d
            refs[5 * n + i][...] = mn
            refs[6 * n + i][...] = vn

    shapes = [jax.ShapeDtypeStruct(w.shape, F32) for w in ws]
    out = pl.pallas_call(body, name="adamw_small", out_shape=shapes * 3)(*ws, *gs, *ms, *vs)
    return out[:n], out[n:2 * n], out[2 * n:]


def sum_slots(name, x):
    n = x.shape[0]

    def fn(t):
        acc = t[0]
        for j in range(1, n):
            acc = acc + t[j]
        return (acc,)

    return ew_call(name, fn, [x], [(x.shape[1:], F32)])[0]


def _pack_rows(parts):
    rows = []
    for p in parts:
        flat = p.reshape(1, -1)
        n = flat.shape[1]
        rows.append(jnp.pad(flat, ((0, 0), (0, -(-n // (8 * LANE)) * 8 * LANE - n))).reshape(-1, LANE))
    return jnp.concatenate(rows, axis=0)


def _unpack_rows(pack, shapes):
    out, r = [], 0
    for s in shapes:
        n = int(np.prod(s))
        nr = -(-n // (8 * LANE)) * 8
        out.append(pack[r:r + nr].reshape(1, -1)[:, :n].reshape(s))
        r += nr
    return out


def _mesh_pos():
    return lax.axis_index("x"), lax.axis_index("y"), lax.axis_index("c")


N_PEERS = N_DEV - 1


def all_gather(name, vs):
    n = len(vs)

    def body(*refs):
        _ag_start(refs[:n], refs[n:2 * n], *refs[2 * n:])
        _ag_finish(refs[:n], refs[n:2 * n], *refs[2 * n:])

    hbm = pl.BlockSpec(memory_space=pl.ANY)
    return pl.pallas_call(
        body, name=name, out_shape=_ag_out_shape(vs), in_specs=[hbm] * n, out_specs=[hbm] * n,
        scratch_shapes=_a2a_scratch(n),
    )(*vs)


def _ag_out_shape(vs):
    return [jax.ShapeDtypeStruct((N_DEV,) + v.shape, v.dtype) for v in vs]


def _ag_copies(x_refs, out_refs, send_sems, recv_sems, local_sems):
    n = len(x_refs)
    x, y, c = _mesh_pos()
    me, sibling = (x, y, c), (x, y, 1 - c)
    chips = [(1 - x, y), (x, 1 - y), (1 - x, 1 - y)]

    def slot(a, px, py, pc):
        return out_refs[a].at[4 * px + 2 * py + pc]

    def copy(a, k, block, to, src=None):
        return pltpu.make_async_remote_copy(
            src_ref=slot(a, *block) if src is None else src, dst_ref=slot(a, *block),
            send_sem=send_sems.at[N_PEERS * a + k], recv_sem=recv_sems.at[N_PEERS * a + k],
            device_id=to, device_id_type=MESH)

    local = [pltpu.make_async_copy(x_refs[a], slot(a, *me), local_sems.at[a]) for a in range(n)]
    first = []
    for a in range(n):
        first.append(copy(a, 0, me, sibling, src=x_refs[a]))
        first += [copy(a, 1 + j, me, (*chip, c), src=x_refs[a]) for j, chip in enumerate(chips)]
    passed = [(copy(a, 1 + j, (*chip, c), me), copy(a, 4 + j, (*chip, c), sibling))
              for j, chip in enumerate(chips) for a in range(n)]
    from_sibling = []
    for a in range(n):
        from_sibling.append(copy(a, 0, sibling, me))
        from_sibling += [copy(a, 4 + j, (*chip, 1 - c), me) for j, chip in enumerate(chips)]
    return local, first, passed, from_sibling


def _ag_start(*refs):
    local, first, _, _ = _ag_copies(*refs)
    for cp in local + first:
        cp.start()


def _ag_finish(*refs):
    local, first, passed, from_sibling = _ag_copies(*refs)
    for arrived, hand_on in passed:
        arrived.wait_recv()
        hand_on.start()
    for cp in from_sibling:
        cp.wait_recv()
    for cp in first + [hand_on for _, hand_on in passed]:
        cp.wait_send()
    for cp in local:
        cp.wait()


def _a2a_scratch(n):
    return [pltpu.SemaphoreType.DMA((N_PEERS * n,)), pltpu.SemaphoreType.DMA((N_PEERS * n,)), pltpu.SemaphoreType.DMA((n,))]


def _a2a_copies(x_refs, out_refs, send_sems, recv_sems, local_sems):
    n = len(x_refs)
    x, y, c = _mesh_pos()
    me = 4 * x + 2 * y + c
    local = [pltpu.make_async_copy(x_refs[a].at[me], out_refs[a].at[me], local_sems.at[a]) for a in range(n)]
    remote = []
    for k in range(1, N_DEV):
        px, py, pc = x ^ ((k >> 2) & 1), y ^ ((k >> 1) & 1), c ^ (k & 1)
        for a in range(n):
            remote.append(pltpu.make_async_remote_copy(
                src_ref=x_refs[a].at[4 * px + 2 * py + pc], dst_ref=out_refs[a].at[me],
                send_sem=send_sems.at[N_PEERS * a + k - 1], recv_sem=recv_sems.at[N_PEERS * a + k - 1],
                device_id=(px, py, pc), device_id_type=MESH))
    return local, remote


def _a2a_start(local, remote):
    for cp in local + remote:
        cp.start()


def _a2a_wait(local, remote):
    for cp in remote:
        cp.wait_recv()
    for cp in remote:
        cp.wait_send()
    for cp in local:
        cp.wait()


class Hosted:
    def __init__(self, start=None, finish=None, arrays=(), out_shape=()):
        self.start, self.finish, self.arrays, self.out_shape = start, finish, list(arrays), list(out_shape)
        self.n = len(self.arrays)
        self.specs = [pl.BlockSpec(memory_space=pl.ANY)] * self.n
        self.scratch = _a2a_scratch(self.n) if self.n else []

    def steps(self, send_refs, recv_refs, sems, first_step, last_step):
        def begin():
            if self.n:
                pl.when(first_step)(lambda: self.start(send_refs, recv_refs, *sems))

        def end():
            if self.n:
                pl.when(last_step)(lambda: self.finish(send_refs, recv_refs, *sems))

        return begin, end


def hosted_all_to_all(vs):
    return Hosted(lambda *r: _a2a_start(*_a2a_copies(*r)), lambda *r: _a2a_wait(*_a2a_copies(*r)), vs,
                  [jax.ShapeDtypeStruct(v.shape, v.dtype) for v in vs])


def hosted_all_gather(vs):
    return Hosted(_ag_start, _ag_finish, vs, _ag_out_shape(vs))


def _taps8(w):
    return jnp.concatenate([w, jnp.zeros((8 - w.shape[0], w.shape[1]), w.dtype)], axis=0)


FIRST = ("w_in", "w_q_up", "w_kv_up")
LATE_WEIGHTS = ("w_out", "w_up", "w_down")


def first_weights_to_internal(w_in, w_q_up, w_kv_up):
    cq, ckv, kr, z, xbc, dt = jnp.split(w_in, np.cumsum(IN_SPLITS)[:-1].tolist(), axis=1)
    K = w_in.shape[0]

    def zeros(n):
        return jnp.zeros((K, n), w_in.dtype)

    w_in_p = jnp.concatenate([cq, zeros(KR_LANE), kr, zeros(LANE - KR_LANE - ROPE), ckv, zeros(OFF_Z - OFF_CKV - KV_RANK),
                              z, xbc, dt, zeros(WIN_P - OFF_DT - 2 * SSD_HEADS)], axis=1)
    w_q_p = jnp.pad(w_q_up.reshape(Q_RANK, N_HEADS, NOPE + ROPE), ((0, 0), (0, 0), (0, HEAD_BLOCK - NOPE - ROPE))).reshape(Q_RANK, QP)
    return dict(w_in_p=w_in_p, w_q_p=w_q_p, w_kv=w_kv_up)


def late_weights_to_internal(w_out, w_up, w_down):
    attn_rows = w_out[:N_HEADS * V_DIM].reshape(N_HEADS, V_DIM, -1)
    w_out_p = jnp.concatenate([jnp.pad(attn_rows, ((0, 0), (HEAD_BLOCK - V_DIM, 0), (0, 0))).reshape(QP, -1),
                               w_out[N_HEADS * V_DIM:]], axis=0)
    return dict(w_out_p=w_out_p, w_up=glu_interleave(w_up), w_down=w_down)


def _in_grad(g_in_p):
    return jnp.concatenate([g_in_p[:, OFF_CQ:OFF_CQ + Q_RANK], g_in_p[:, OFF_CKV:OFF_CKV + KV_RANK],
                            g_in_p[:, OFF_KR + KR_LANE:OFF_KR + KR_LANE + ROPE], g_in_p[:, OFF_Z:OFF_Z + D_INNER],
                            g_in_p[:, OFF_XBC:OFF_XBC + XBC], g_in_p[:, OFF_DT:OFF_DT + 2 * SSD_HEADS]], axis=1)


def _q_grad(g_q_p):
    return g_q_p.reshape(Q_RANK, N_HEADS, HEAD_BLOCK)[:, :, :NOPE + ROPE].reshape(Q_RANK, -1)


def _out_grad(g_out_p):
    return jnp.concatenate([g_out_p[:QP].reshape(N_HEADS, HEAD_BLOCK, -1)[:, HEAD_BLOCK - V_DIM:].reshape(N_HEADS * V_DIM, -1),
                            g_out_p[QP:]], axis=0)


EARLY = ("w_out", "w_up", "w_down", "w_q_up", "w_kv_up")


def local_step(x, ctx, target, mod_x, mod_c, W, late_shards, V):
    nb, S, D = x.shape
    C = ctx.shape[1]
    T = C + S
    tr = _tile(math.gcd(C, S), 256, 8)
    tq = _tile(S, 256, 8)
    tc = 256
    cblk = C // tr
    m = [mod_x[:, i * D:(i + 1) * D][:, None, :] for i in range(N_MOD)]
    mc = [mod_c[:, i * D:(i + 1) * D] for i in range(2)]
    ssd_w8, ffn_w8 = _taps8(V["ssd_conv_w"]), _taps8(V["ffn_conv_w"])
    dexp = jnp.repeat(V["ssd_d"].reshape(-1), SSD_P).reshape(1, D_INNER)
    cosT, sinT = rope_tables(C, S)
    cosS, sinS = cosT[C:], sinT[C:]

    (h1x,) = rows_fwd("prenorm_x", fn_prenorm, nb, S // tr, tr, [(x, D, 0, 0)], [m[0], m[1]], [V["mix_pre_norm"]], [(D, BF16)])
    (h1c,) = rows_fwd("prenorm_c", fn_prenorm, nb, C // tr, tr, [(ctx, D, 0, 0)], [], [mc[0], mc[1], V["mix_pre_norm"]], [(D, BF16)])
    h1 = jnp.concatenate([h1c, h1x], axis=1).reshape(nb * T, D)
    u = matmul("in_proj", [(h1, W["w_in_p"])], "nn", F32).reshape(nb, T, WIN_P)
    (qn,) = rows_fwd("q_norm", fn_rms, nb, S // tr, tr, [(u, Q_RANK, OFF_CQ // Q_RANK, cblk)], [], [V["q_norm"]], [(Q_RANK, BF16)])
    (kvn,) = rows_fwd("kv_norm", fn_rms, nb, T // tr, tr, [(u, KV_RANK, OFF_CKV // KV_RANK, 0)], [], [V["kv_norm"]], [(KV_RANK, BF16)])
    qn2, kvn2 = qn.reshape(nb * S, Q_RANK), kvn.reshape(nb * T, KV_RANK)
    q_raw = matmul("q_up", [(qn2, W["w_q_p"])], "nn", F32).reshape(nb, S, QP)
    kv = matmul("kv_up", [(kvn2, W["w_kv"])], "nn", BF16).reshape(nb, T, QP)
    q = rope_call("rope_q", q_raw, QP, 0, cosS * Q_PRESCALE, sinS * Q_PRESCALE, BF16, tr)
    kr = rope_call("rope_k", u, LANE, OFF_KR // LANE, cosT, sinT, BF16, tr)
    o = attn_fwd(q, kv, kr, tq)
    xbc = ssd_conv_fwd(u, ssd_w8, V["ssd_conv_b"], C, tc)
    dt2, alog2, dtb2 = ssd_dt_inputs(u, V["ssd_a_log"], V["ssd_dt_bias"])
    y2, hin, *late = ssd_fwd(xbc, dt2, alog2, dtb2, C, hosted_all_gather(late_shards))
    W = dict(W, **late_weights_to_internal(*[_whole(s, n) for s, n in zip(late, LATE_WEIGHTS)]))
    y2 = y2.reshape(2 * nb, S, D_INNER)
    fin_rows = [(y2, D_INNER, 0, 0, 0), (y2, D_INNER, 0, 0, nb), (xbc, D_INNER, 0, cblk), (u, D_INNER, OFF_Z // D_INNER, cblk)]
    fin_gl = [dexp, V["ssd_norm"]]
    (ssd,) = rows_fwd("ssd_finish", fn_ssd_finish, nb, S // tr, tr, fin_rows, [], fin_gl, [(D_INNER, BF16)])
    o2, ssd2 = o.reshape(nb * S, QP), ssd.reshape(nb * S, D_INNER)
    mix = matmul("out_proj", [(o2, W["w_out_p"][:QP]), (ssd2, W["w_out_p"][QP:])], "nn", F32).reshape(nb, S, D)
    pm_rows = [(x, D, 0, 0), (mix, D, 0, 0)]
    pm_pb = [m[2], m[4], m[3]]
    pm_gl = [V["mix_post_norm"], V["ffn_pre_norm"]]
    x1, h2 = rows_fwd("postmix", fn_postmix, nb, S // tr, tr, pm_rows, pm_pb, pm_gl, [(D, F32), (D, BF16)])
    h22 = h2.reshape(nb * S, D)
    up = matmul("up_proj", [(h22, W["w_up"])], "nn", F32).reshape(nb, S, 2 * D_FF)
    act = glu_fwd(up, ffn_w8, V["ffn_conv_b"])
    act2 = act.reshape(nb * S, D_FF)
    ffn = matmul("down_proj", [(act2, W["w_down"])], "nn", F32).reshape(nb, S, D)
    dx1, dffn, dgate2, d_ffn_post, loss = final_call(x1, ffn, target, m[5], V["ffn_post_norm"], tr)

    dffn2 = dffn.reshape(nb * S, D)
    dact = matmul("down_dgrad", [(dffn2, W["w_down"])], "nt", BF16).reshape(nb, S, D_FF)
    g_down = matmul_tn("down_wgrad", act2, dffn2)
    dup, ffn_rows = glu_bwd(up, ffn_w8, V["ffn_conv_b"], dact)
    dup2 = dup.reshape(nb * S, 2 * D_FF)
    dh2 = matmul("up_dgrad", [(dup2, W["w_up"])], "nt", BF16).reshape(nb, S, D)
    g_up = matmul_tn("up_wgrad", h22, dup2)
    dx_a, dmix, dgate1, dscale2, dshift2, d_mix_post, d_ffn_pre = rows_bwd(
        "postmix_bwd", fn_postmix, nb, S // tr, tr, pm_rows, pm_pb, pm_gl,
        [(dx1, D, 0, 0), (dh2, D, 0, 0)], [(0, F32), (1, BF16)])
    dmix2 = dmix.reshape(nb * S, D)
    dcat = matmul("out_dgrad", [(dmix2, W["w_out_p"])], "nt", BF16).reshape(nb, S, QP + D_INNER)
    g_out_p = jnp.concatenate([matmul_tn("out_wgrad_attn", o2, dmix2), matmul_tn("out_wgrad_ssd", ssd2, dmix2)], axis=0)
    dy, dxs_direct, dz, d_dexp, d_ssd_norm = rows_bwd(
        "ssd_finish_bwd", fn_ssd_finish, nb, S // tr, tr, fin_rows, [], fin_gl,
        [(dcat, D_INNER, QP // D_INNER, 0)], [(0, F32), (2, F32), (3, BF16)])
    dq, dkv, dkr = attn_bwd(q, kv, kr, dcat, tq)
    dq_pre = rope_call("rope_dq", dq, QP, 0, cosS, -sinS, BF16, tr).reshape(nb * S, QP)
    dkr_pre = rope_call("rope_dk", dkr, LANE, 0, cosT, -sinT, BF16, tr)
    dkv2 = dkv.reshape(nb * T, QP)
    dqn = matmul("q_dgrad", [(dq_pre, W["w_q_p"])], "nt", F32).reshape(nb, S, Q_RANK)
    g_q_p = matmul_tn("q_wgrad", qn2, dq_pre)
    dkvn = matmul("kv_dgrad", [(dkv2, W["w_kv"])], "nt", F32).reshape(nb, T, KV_RANK)
    g_kv = matmul_tn("kv_wgrad", kvn2, dkv2)
    early_grads = (_out_grad(g_out_p), glu_deinterleave(g_up), g_down, _q_grad(g_q_p), g_kv)
    early = hosted_all_to_all([_per_device(g, n) for g, n in zip(early_grads, EARLY)])
    dxbc2, ddt2, ssd_stats, *received = ssd_bwd(xbc, dt2, alog2, dtb2, hin, dy, C, early)
    ddt_block = jnp.concatenate([ddt2[0][..., :SSD_HEADS], ddt2[1][..., :SSD_HEADS],
                                 jnp.zeros((nb, T, WIN_P - OFF_DT - 2 * SSD_HEADS), F32)], axis=-1).astype(BF16)
    dxbc_raw, ssd_rows = ssd_conv_bwd(u, ssd_w8, V["ssd_conv_b"], dxbc2, dxs_direct, C, tc)
    dcq, d_q_norm = rows_bwd("q_norm_bwd", fn_rms, nb, S // tr, tr, [(u, Q_RANK, OFF_CQ // Q_RANK, cblk)], [], [V["q_norm"]],
                             [(dqn, Q_RANK, 0, 0)], [(0, BF16)])
    dckv, d_kv_norm = rows_bwd("kv_norm_bwd", fn_rms, nb, T // tr, tr, [(u, KV_RANK, OFF_CKV // KV_RANK, 0)], [], [V["kv_norm"]],
                               [(dkvn, KV_RANK, 0, 0)], [(0, BF16)])

    def ctx_rows(t):
        return jnp.pad(t, ((0, 0), (C, 0), (0, 0)))

    du = jnp.concatenate([ctx_rows(dcq), dkr_pre, dckv, jnp.zeros((nb, T, OFF_Z - OFF_CKV - KV_RANK), BF16), ctx_rows(dz),
                          dxbc_raw, ddt_block], axis=-1).reshape(nb * T, WIN_P)
    g_in_p = matmul_tn("in_wgrad", h1, du)
    dh1, received_in = matmul("in_dgrad", [(du, W["w_in_p"])], "nt", BF16,
                              hosted=hosted_all_to_all([_per_device(_in_grad(g_in_p), "w_in")]))
    dh1 = dh1.reshape(nb, T, D)

    def fn_prenorm_res(xv, shift, scale, g):
        return fn_prenorm(xv, shift, scale, g) + (xv,)

    grad_x, dshift1, dscale1, d_mix_pre_x = rows_bwd(
        "prenorm_x_bwd", fn_prenorm_res, nb, S // tr, tr, [(x, D, 0, 0)], [m[0], m[1]], [V["mix_pre_norm"]],
        [(dh1, D, 0, cblk), (dx_a, D, 0, 0)], [(0, F32)])
    dshift_c, dscale_c, d_mix_pre_c = rows_bwd(
        "prenorm_c_bwd", fn_prenorm, nb, C // tr, tr, [(ctx, D, 0, 0)], [], [mc[0], mc[1], V["mix_pre_norm"]],
        [(dh1, D, 0, 0)], [])

    dmod_x = jnp.concatenate([dshift1, dscale1, dgate1, dshift2, dscale2, dgate2], axis=-1).reshape(nb, N_MOD * D)
    dmod_c = jnp.concatenate([dshift_c, dscale_c, jnp.zeros((1, (N_MOD - 2) * D), F32)], axis=-1)
    gv = dict(
        mix_pre_norm=d_mix_pre_x + d_mix_pre_c, mix_post_norm=d_mix_post, q_norm=d_q_norm, kv_norm=d_kv_norm,
        ssd_conv_w=ssd_rows[:SSD_K], ssd_conv_b=ssd_rows[SSD_K:SSD_K + 1],
        ssd_a_log=jnp.concatenate([ssd_stats[0:1, :SSD_HEADS], ssd_stats[2:3, :SSD_HEADS]], axis=1),
        ssd_dt_bias=jnp.concatenate([ssd_stats[1:2, :SSD_HEADS], ssd_stats[3:4, :SSD_HEADS]], axis=1),
        ssd_d=jnp.sum(d_dexp.reshape(SSD_HEADS, SSD_P), axis=1).reshape(1, SSD_HEADS), ssd_norm=d_ssd_norm,
        ffn_pre_norm=d_ffn_pre, ffn_post_norm=d_ffn_post,
        ffn_conv_w=ffn_rows[:FFN_K], ffn_conv_b=ffn_rows[FFN_K:FFN_K + 1])
    return loss, grad_x, dmod_x, dmod_c, gv, dict(zip(EARLY, received), w_in=received_in)


WEIGHT_ORDER = ("c_ctx", "w_mod", "b_mod", "mix_pre_norm", "mix_post_norm", "w_in", "q_norm", "w_q_up", "kv_norm",
                "w_kv_up", "ssd_conv_w", "ssd_conv_b", "ssd_a_log", "ssd_dt_bias", "ssd_d", "ssd_norm", "w_out",
                "ffn_pre_norm", "ffn_post_norm", "w_up", "ffn_conv_w", "ffn_conv_b", "w_down")
MATRICES = ("w_in", "w_q_up", "w_kv_up", "w_out", "w_up", "w_down")
ROW_SHARDED = ("w_out", "w_down")
SMALL_SUMMED = ("c_ctx", "mix_pre_norm", "mix_post_norm", "q_norm", "kv_norm", "ssd_conv_w", "ssd_conv_b", "ssd_a_log",
                "ssd_dt_bias", "ssd_d", "ssd_norm", "ffn_pre_norm", "ffn_post_norm", "ffn_conv_w", "ffn_conv_b")
MOD_ROWS = 8


def _whole(shards, name):
    if name in ROW_SHARDED:
        return shards.reshape(-1, shards.shape[-1])
    return jnp.concatenate([shards[j] for j in range(N_DEV)], axis=1)


def _per_device(g, name):
    if name in ROW_SHARDED:
        return g.reshape(N_DEV, -1, g.shape[-1])
    return jnp.stack(jnp.split(g, N_DEV, axis=1))


def kernel(x, c, ctx, c_ctx, w_mod, b_mod, mix_pre_norm, mix_post_norm, w_in, q_norm, w_q_up, kv_norm, w_kv_up, ssd_conv_w, ssd_conv_b, ssd_a_log, ssd_dt_bias, ssd_d, ssd_norm, w_out, ffn_pre_norm, ffn_post_norm, w_up, ffn_conv_w, ffn_conv_b, w_down, loss_target, m_c_ctx, m_w_mod, m_b_mod, m_mix_pre_norm, m_mix_post_norm, m_w_in, m_q_norm, m_w_q_up, m_kv_norm, m_w_kv_up, m_ssd_conv_w, m_ssd_conv_b, m_ssd_a_log, m_ssd_dt_bias, m_ssd_d, m_ssd_norm, m_w_out, m_ffn_pre_norm, m_ffn_post_norm, m_w_up, m_ffn_conv_w, m_ffn_conv_b, m_w_down, v_c_ctx, v_w_mod, v_b_mod, v_mix_pre_norm, v_mix_post_norm, v_w_in, v_q_norm, v_w_q_up, v_kv_norm, v_w_kv_up, v_ssd_conv_w, v_ssd_conv_b, v_ssd_a_log, v_ssd_dt_bias, v_ssd_d, v_ssd_norm, v_w_out, v_ffn_pre_norm, v_ffn_post_norm, v_w_up, v_ffn_conv_w, v_ffn_conv_b, v_w_down):
    weights = dict(c_ctx=c_ctx, w_mod=w_mod, b_mod=b_mod, mix_pre_norm=mix_pre_norm, mix_post_norm=mix_post_norm, w_in=w_in, q_norm=q_norm, w_q_up=w_q_up, kv_norm=kv_norm, w_kv_up=w_kv_up, ssd_conv_w=ssd_conv_w, ssd_conv_b=ssd_conv_b, ssd_a_log=ssd_a_log, ssd_dt_bias=ssd_dt_bias, ssd_d=ssd_d, ssd_norm=ssd_norm, w_out=w_out, ffn_pre_norm=ffn_pre_norm, ffn_post_norm=ffn_post_norm, w_up=w_up, ffn_conv_w=ffn_conv_w, ffn_conv_b=ffn_conv_b, w_down=w_down)
    mom1 = dict(c_ctx=m_c_ctx, w_mod=m_w_mod, b_mod=m_b_mod, mix_pre_norm=m_mix_pre_norm, mix_post_norm=m_mix_post_norm, w_in=m_w_in, q_norm=m_q_norm, w_q_up=m_w_q_up, kv_norm=m_kv_norm, w_kv_up=m_w_kv_up, ssd_conv_w=m_ssd_conv_w, ssd_conv_b=m_ssd_conv_b, ssd_a_log=m_ssd_a_log, ssd_dt_bias=m_ssd_dt_bias, ssd_d=m_ssd_d, ssd_norm=m_ssd_norm, w_out=m_w_out, ffn_pre_norm=m_ffn_pre_norm, ffn_post_norm=m_ffn_post_norm, w_up=m_w_up, ffn_conv_w=m_ffn_conv_w, ffn_conv_b=m_ffn_conv_b, w_down=m_w_down)
    mom2 = dict(c_ctx=v_c_ctx, w_mod=v_w_mod, b_mod=v_b_mod, mix_pre_norm=v_mix_pre_norm, mix_post_norm=v_mix_post_norm, w_in=v_w_in, q_norm=v_q_norm, w_q_up=v_w_q_up, kv_norm=v_kv_norm, w_kv_up=v_w_kv_up, ssd_conv_w=v_ssd_conv_w, ssd_conv_b=v_ssd_conv_b, ssd_a_log=v_ssd_a_log, ssd_dt_bias=v_ssd_dt_bias, ssd_d=v_ssd_d, ssd_norm=v_ssd_norm, w_out=v_w_out, ffn_pre_norm=v_ffn_pre_norm, ffn_post_norm=v_ffn_post_norm, w_up=v_w_up, ffn_conv_w=v_ffn_conv_w, ffn_conv_b=v_ffn_conv_b, w_down=v_w_down)
    nb, S, D = x.shape
    me = 4 * lax.axis_index("x") + 2 * lax.axis_index("y") + lax.axis_index("c")

    *first, c_all, ssd_w_sh, ffn_w_sh = all_gather(
        "gather_first", [weights[n][0].astype(BF16) for n in FIRST] + [c, ssd_conv_w[0], ffn_conv_w[0]])
    W = first_weights_to_internal(*[_whole(s, n) for n, s in zip(FIRST, first)])
    late_shards = [weights[n][0].astype(BF16) for n in LATE_WEIGHTS]
    V = {n: weights[n].reshape(1, -1) for n in SMALL_SUMMED if n != "c_ctx"}
    V["ssd_conv_w"] = _whole(ssd_w_sh, "ssd_conv_w")
    V["ffn_conv_w"] = _whole(ffn_w_sh, "ffn_conv_w")

    n_all = N_DEV * nb
    mod_rows = -(-(n_all + 1) // 8) * 8
    c_pad = jnp.concatenate([c_all.reshape(n_all, D), c_ctx.reshape(1, D), jnp.zeros((mod_rows - n_all - 1, D), F32)], axis=0)
    mod_cols = w_mod.shape[2]
    b_mine = lax.dynamic_slice(b_mod, (0, me * mod_cols), (1, mod_cols))
    mod_part = matmul("mod_proj", [(c_pad, w_mod[0])], "nn", F32, bias=b_mine, silu_a=True)
    mod_all = _whole(all_gather("gather_mod", [mod_part])[0], "w_mod")
    mod_x = lax.dynamic_slice(mod_all, (me * nb, 0), (nb, mod_all.shape[1]))
    mod_c = mod_all[n_all:n_all + 1]

    loss, grad_x, dmod_x, dmod_c, gv, slots = local_step(x, ctx, loss_target, mod_x, mod_c, W, late_shards, V)

    dmod_mine = jnp.concatenate([dmod_x, dmod_c, jnp.zeros((MOD_ROWS - nb - 1, dmod_x.shape[1]), F32)], axis=0)
    dmod_all = all_gather("gather_dmod", [dmod_mine])[0]
    dmod_ctx = sum_slots("sum_dmod_ctx", dmod_all[:, nb:nb + 1].reshape(N_DEV, -1, LANE)).reshape(1, -1)
    dmod_full = jnp.concatenate([dmod_all[:, :nb].reshape(n_all, -1), dmod_ctx,
                                 jnp.zeros((mod_rows - n_all - 1, dmod_ctx.shape[1]), F32)], axis=0)
    (g_b_mod,) = ew_call("mod_bias_grad", lambda t: (jnp.sum(t, axis=0, keepdims=True),), [dmod_full], [((1, dmod_full.shape[1]), F32)])
    dmod_cols = lax.dynamic_slice(dmod_full, (0, me * mod_cols), (mod_rows, mod_cols))
    g_w_mod = matmul_tn("mod_wgrad", c_pad, dmod_cols, silu_a=True)
    dsilu_ctx = matmul("mod_dgrad_ctx", [(dmod_cols[n_all:n_all + 8], w_mod[0])], "nt", F32)[0:1]

    def silu_vjp(cc, ct):
        return (jax.vjp(_silu, cc)[1](ct)[0],)

    (g_c_ctx_part,) = ew_call("c_ctx_grad", silu_vjp, [c_ctx.reshape(1, D), dsilu_ctx], [((1, D), F32)])

    gv = dict(gv, c_ctx=g_c_ctx_part)
    small_parts = [loss] + [gv[n] for n in SMALL_SUMMED]
    small_sum = sum_slots("sum_small", all_gather("gather_small_grads", [_pack_rows(small_parts)])[0])
    summed = _unpack_rows(small_sum, [p.shape for p in small_parts])
    loss_out = summed[0][0, 0]
    grads = {n: g.reshape(weights[n].shape) if n not in ("ssd_conv_w", "ffn_conv_w") else g for n, g in zip(SMALL_SUMMED, summed[1:])}
    for n in ("ssd_conv_w", "ffn_conv_w"):
        cols = weights[n].shape[2]
        grads[n] = lax.dynamic_slice(grads[n], (0, me * cols), (grads[n].shape[0], cols)).reshape(weights[n].shape)
    grads["b_mod"] = g_b_mod.reshape(b_mod.shape)

    slots = dict(slots, w_mod=g_w_mod[None])
    delta, new_m, new_v = {}, {}, {}
    for n in MATRICES + ("w_mod",):
        g, d, mn, vn = adamw_matrix("adamw_" + n, weights[n][0], slots[n], mom1[n][0], mom2[n][0])
        grads[n], delta[n], new_m[n], new_v[n] = [t.reshape(weights[n].shape) for t in (g, d, mn, vn)]
    small = [n for n in WEIGHT_ORDER if n not in slots]

    def two_d(t):
        return t.reshape(-1, t.shape[-1])

    ds, ms, vs = adamw_small(*[[two_d(t[n]) for n in small] for t in (weights, grads, mom1, mom2)])
    for n, d, mn, vn in zip(small, ds, ms, vs):
        delta[n], new_m[n], new_v[n] = [t.reshape(weights[n].shape) for t in (d, mn, vn)]
    return (loss_out, grad_x, *[t[n] for t in (grads, delta, new_m, new_v) for n in WEIGHT_ORDER])
```

```python
import functools
import math

import jax
import jax.numpy as jnp
import numpy as np
from jax import lax
from jax.experimental import pallas as pl
from jax.experimental.pallas import tpu as pltpu

F32 = jnp.float32
BF16 = jnp.bfloat16
MESH = pl.DeviceIdType.MESH

D_MODEL = 1024
GRID_W = 64
N_HEADS = 16
NOPE = 64
ROPE = 32
V_DIM = 64
Q_RANK = 384
KV_RANK = 256
ROPE_THETA = 10000.0
ATTN_SCALE = (NOPE + ROPE) ** -0.5
SSD_HEADS = 16
SSD_P = 64
SSD_GROUPS = 2
SSD_N = 128
SSD_K = 5
CHUNK = 128
D_INNER = SSD_HEADS * SSD_P
GN = SSD_GROUPS * SSD_N
XBC = D_INNER + 2 * GN
D_FF = 2816
FFN_K = 3
N_MOD = 6
EPS = 1e-6
IN_SPLITS = (Q_RANK, KV_RANK, ROPE, D_INNER, XBC, 2 * SSD_HEADS)
IN_WIDTH = sum(IN_SPLITS)
N_DEV = 8

ADAM_LR = 0.001
ADAM_B1 = 0.9
ADAM_B2 = 0.999
ADAM_EPS = 1e-08
ADAM_WD = 0.01
ADAM_STEP = 10

LANE = 128
HEAD_BLOCK = 128
OFF_CQ = 0
OFF_KR = 384
OFF_CKV = 512
OFF_Z = 1024
OFF_XBC = 2048
OFF_DT = 3584
WIN_P = 3840
KR_LANE = 64
QP = N_HEADS * HEAD_BLOCK

VMEM_LIMIT_V7X = 56 * 1024 * 1024
NEG_BIG = -1e30


def _cparams(*sem):
    return pltpu.CompilerParams(dimension_semantics=sem, vmem_limit_bytes=VMEM_LIMIT_V7X)


def _tile(n, target, mult=128):
    if n <= target:
        return n
    t = (target // mult) * mult
    while t >= mult:
        if n % t == 0:
            return t
        t -= mult
    return n


def _silu(x):
    return x * jax.nn.sigmoid(x)


def _rms(x, g):
    return x * lax.rsqrt(jnp.mean(x * x, axis=-1, keepdims=True) + EPS) * g


WHOLE_K_WIDE = 2048


def matmul(name, pairs, mode, out_dtype, *, bias=None, silu_a=False, hosted=None):
    n_pairs = len(pairs)
    M = pairs[0][0].shape[0]
    N = pairs[0][1].shape[1] if mode == "nn" else pairs[0][1].shape[0]
    k_total = sum(a.shape[1] for a, _ in pairs)
    tm = _tile(M, 1024 if k_total <= WHOLE_K_WIDE else 512, 8)
    tn = _tile(N, 1408 if k_total <= WHOLE_K_WIDE else 512)
    dims = (((1,), (0,)), ((), ())) if mode == "nn" else (((1,), (1,)), ((), ()))
    n_own = 2 * n_pairs + (bias is not None)
    n_ex = hosted.n if hosted else 0

    def body(*refs):
        o_ref = refs[n_own + n_ex]
        if hosted:
            j, i = pl.program_id(0), pl.program_id(1)
            begin_exchange, end_exchange = hosted.steps(
                refs[n_own:n_own + n_ex], refs[n_own + n_ex + 1:n_own + 2 * n_ex + 1], refs[n_own + 2 * n_ex + 1:],
                jnp.logical_and(j == 0, i == 0), jnp.logical_and(j == N // tn - 1, i == M // tm - 1))
            begin_exchange()
        acc = None
        for p in range(n_pairs):
            a = refs[2 * p][...]
            if silu_a:
                a = _silu(a.astype(F32))
            d = lax.dot_general(a.astype(BF16), refs[2 * p + 1][...].astype(BF16), dims, preferred_element_type=F32)
            acc = d if acc is None else acc + d
        if bias is not None:
            acc = acc + refs[2 * n_pairs][...]
        o_ref[...] = acc.astype(o_ref.dtype)
        if hosted:
            end_exchange()

    in_specs, args = [], []
    for a, b in pairs:
        K = a.shape[1]
        in_specs.append(pl.BlockSpec((tm, K), lambda j, i: (i, 0)))
        in_specs.append(pl.BlockSpec((K, tn), lambda j, i: (0, j)) if mode == "nn" else pl.BlockSpec((tn, K), lambda j, i: (j, 0)))
        args += [a, b]
    if bias is not None:
        in_specs.append(pl.BlockSpec((1, tn), lambda j, i: (0, j)))
        args.append(bias)
    out_spec = pl.BlockSpec((tm, tn), lambda j, i: (i, j))
    out_shape = jax.ShapeDtypeStruct((M, N), out_dtype)
    if not hosted:
        return pl.pallas_call(
            body, name=name, grid=(N // tn, M // tm), in_specs=in_specs, out_specs=out_spec, out_shape=out_shape,
            compiler_params=_cparams("arbitrary", "arbitrary"),
        )(*args)
    return pl.pallas_call(
        body, name=name, grid=(N // tn, M // tm), in_specs=in_specs + hosted.specs,
        out_specs=[out_spec] + hosted.specs, out_shape=[out_shape] + hosted.out_shape, scratch_shapes=hosted.scratch,
        compiler_params=_cparams("arbitrary", "arbitrary"),
    )(*args, *hosted.arrays)


def matmul_tn(name, a, b, out_dtype=F32, *, silu_a=False, tm=1408, tn=512, tk=2048):
    R, M = a.shape
    N = b.shape[1]
    tm = _tile(M, tm)
    tn = _tile(N, tn)
    tk = _tile(R, tk, 8)
    nk = R // tk

    def body(a_ref, b_ref, o_ref, acc):
        k = pl.program_id(2)

        @pl.when(k == 0)
        def _():
            acc[...] = jnp.zeros_like(acc)

        x = a_ref[...]
        if silu_a:
            x = _silu(x.astype(F32))
        acc[...] += lax.dot_general(x.astype(BF16), b_ref[...].astype(BF16), (((0,), (0,)), ((), ())),
                                    preferred_element_type=F32)

        @pl.when(k == nk - 1)
        def _():
            o_ref[...] = acc[...].astype(o_ref.dtype)

    return pl.pallas_call(
        body, name=name, grid=(M // tm, N // tn, nk),
        in_specs=[pl.BlockSpec((tk, tm), lambda i, j, k: (k, i)), pl.BlockSpec((tk, tn), lambda i, j, k: (k, j))],
        out_specs=pl.BlockSpec((tm, tn), lambda i, j, k: (i, j)),
        out_shape=jax.ShapeDtypeStruct((M, N), out_dtype),
        scratch_shapes=[pltpu.VMEM((tm, tn), F32)],
        compiler_params=_cparams("arbitrary", "arbitrary", "arbitrary"),
    )(a, b)


def _row_specs(rin, pbin, glin, tr):
    specs = [pl.BlockSpec((1, tr, w), lambda b, i, cb=cb, ro=ro, bo=(e[4] if len(e) > 4 else 0): (b + bo, i + ro, cb))
             for e in rin for (_, w, cb, ro) in [e[:4]]]
    specs += [pl.BlockSpec((1, 1, a.shape[-1]), lambda b, i: (b, 0, 0)) for a in pbin]
    specs += [pl.BlockSpec((1, a.shape[-1]), lambda b, i: (0, 0)) for a in glin]
    return specs


def rows_fwd(name, fn, nb, nblk, tr, rin, pbin, glin, outs):
    nr, npb, ngl = len(rin), len(pbin), len(glin)
    n_in = nr + npb + ngl

    def body(*refs):
        args = [r[0].astype(F32) for r in refs[:nr + npb]] + [r[...] for r in refs[nr + npb:n_in]]
        res = fn(*args)
        for o, v in zip(refs[n_in:], res):
            o[0] = v.astype(o.dtype)

    return pl.pallas_call(
        body, name=name, grid=(nb, nblk), in_specs=_row_specs(rin, pbin, glin, tr),
        out_specs=[pl.BlockSpec((1, tr, w), lambda b, i: (b, i, 0)) for (w, _) in outs],
        out_shape=[jax.ShapeDtypeStruct((nb, nblk * tr, w), dt) for (w, dt) in outs],
        compiler_params=_cparams("arbitrary", "arbitrary"),
    )(*[e[0] for e in rin], *pbin, *glin)


def rows_bwd(name, fn, nb, nblk, tr, rin, pbin, glin, cts, want):
    nr, npb, ngl, nct = len(rin), len(pbin), len(glin), len(cts)
    n_in = nr + npb + ngl

    def body(*refs):
        b, i = pl.program_id(0), pl.program_id(1)
        args = [r[0].astype(F32) for r in refs[:nr + npb]] + [r[...] for r in refs[nr + npb:n_in]]
        ct = tuple(r[0].astype(F32) for r in refs[n_in:n_in + nct])
        _, vjp = jax.vjp(fn, *args)
        g = vjp(ct)
        orefs = refs[n_in + nct:]
        for o, (idx, _) in zip(orefs, want):
            o[0] = g[idx].astype(o.dtype)
        pb_refs = orefs[len(want):len(want) + npb]
        gl_refs = orefs[len(want) + npb:]

        @pl.when(i == 0)
        def _():
            for o, v in zip(pb_refs, g[nr:nr + npb]):
                o[0] = v

        @pl.when(i > 0)
        def _():
            for o, v in zip(pb_refs, g[nr:nr + npb]):
                o[0] += v

        first = jnp.logical_and(b == 0, i == 0)

        @pl.when(first)
        def _():
            for o, v in zip(gl_refs, g[nr + npb:]):
                o[...] = v

        @pl.when(jnp.logical_not(first))
        def _():
            for o, v in zip(gl_refs, g[nr + npb:]):
                o[...] += v

    out_specs = [pl.BlockSpec((1, tr, rin[idx][1]), lambda b, i: (b, i, 0)) for (idx, _) in want]
    out_shape = [jax.ShapeDtypeStruct((nb, nblk * tr, rin[idx][1]), dt) for (idx, dt) in want]
    out_specs += [pl.BlockSpec((1, 1, a.shape[-1]), lambda b, i: (b, 0, 0)) for a in pbin]
    out_shape += [jax.ShapeDtypeStruct((nb, 1, a.shape[-1]), F32) for a in pbin]
    out_specs += [pl.BlockSpec((1, a.shape[-1]), lambda b, i: (0, 0)) for a in glin]
    out_shape += [jax.ShapeDtypeStruct((1, a.shape[-1]), F32) for a in glin]
    return pl.pallas_call(
        body, name=name, grid=(nb, nblk),
        in_specs=_row_specs(rin, pbin, glin, tr) + _row_specs(cts, [], [], tr),
        out_specs=out_specs, out_shape=out_shape,
        compiler_params=_cparams("arbitrary", "arbitrary"),
    )(*[e[0] for e in rin], *pbin, *glin, *[e[0] for e in cts])


def ew_call(name, fn, ins, outs):
    def body(*refs):
        res = fn(*[r[...] for r in refs[:len(ins)]])
        for o, v in zip(refs[len(ins):], res):
            o[...] = v.astype(o.dtype)

    return pl.pallas_call(body, name=name, out_shape=[jax.ShapeDtypeStruct(s, dt) for (s, dt) in outs])(*ins)


def fn_prenorm(x, shift, scale, g):
    return (_rms(x, g) * (1.0 + scale) + shift,)


def fn_rms(x, g):
    return (_rms(x, g),)


def fn_ssd_finish(yf, yr, xs, z, dexp, nw):
    y = yf + yr + dexp * xs
    return (_rms(y * _silu(z), nw),)


def fn_postmix(x, mix, gate1, scale2, shift2, post_g, pre_g):
    x1 = x + gate1 * _rms(mix, post_g)
    h2 = _rms(x1, pre_g) * (1.0 + scale2) + shift2
    return x1, h2


def final_call(x1, ffn, target, gate2, post_g, tr):
    nb, S, D = x1.shape
    nblk = S // tr

    def body(x1_ref, f_ref, t_ref, g2_ref, pg_ref, dx1_ref, df_ref, dg2_ref, dpg_ref, loss_ref):
        b, i = pl.program_id(0), pl.program_id(1)
        tgt = t_ref[0]

        def lossfn(x1v, fv, g2, pg):
            e = x1v + g2 * _rms(fv, pg) - tgt
            return 0.5 * jnp.sum(jnp.mean(e * e, axis=-1, keepdims=True))

        val, (dx1, df, dg2, dpg) = jax.value_and_grad(lossfn, argnums=(0, 1, 2, 3))(
            x1_ref[0], f_ref[0].astype(F32), g2_ref[0], pg_ref[...])
        dx1_ref[0] = dx1
        df_ref[0] = df.astype(df_ref.dtype)
        lv = jnp.full((1, LANE), val, F32)

        @pl.when(i == 0)
        def _():
            dg2_ref[0] = dg2

        @pl.when(i > 0)
        def _():
            dg2_ref[0] += dg2

        first = jnp.logical_and(b == 0, i == 0)

        @pl.when(first)
        def _():
            dpg_ref[...] = dpg
            loss_ref[...] = lv

        @pl.when(jnp.logical_not(first))
        def _():
            dpg_ref[...] += dpg
            loss_ref[...] += lv

    row = pl.BlockSpec((1, tr, D), lambda b, i: (b, i, 0))
    pb = pl.BlockSpec((1, 1, D), lambda b, i: (b, 0, 0))
    gl = pl.BlockSpec((1, D), lambda b, i: (0, 0))
    return pl.pallas_call(
        body, name="loss_head", grid=(nb, nblk), in_specs=[row, row, row, pb, gl],
        out_specs=[row, row, pb, gl, pl.BlockSpec((1, LANE), lambda b, i: (0, 0))],
        out_shape=[jax.ShapeDtypeStruct((nb, S, D), F32), jax.ShapeDtypeStruct((nb, S, D), BF16),
                   jax.ShapeDtypeStruct((nb, 1, D), F32), jax.ShapeDtypeStruct((1, D), F32),
                   jax.ShapeDtypeStruct((1, LANE), F32)],
        compiler_params=_cparams("arbitrary", "arbitrary"),
    )(x1, ffn, target, gate2, post_g)


def _rotate_half(t):
    lane = lax.broadcasted_iota(jnp.int32, t.shape, 1)
    return jnp.where((lane & 15) < 8, -pltpu.roll(t, LANE - 8, 1), pltpu.roll(t, 8, 1))


def rope_call(name, x, width, colblk, cos, sin, out_dtype, tr):
    nb = x.shape[0]
    R = cos.shape[0]
    nblk = R // tr

    def body(x_ref, c_ref, s_ref, o_ref):
        c, s = c_ref[...], s_ref[...]
        for h in range(width // LANE):
            t = x_ref[0, :, h * LANE:(h + 1) * LANE].astype(F32)
            o_ref[0, :, h * LANE:(h + 1) * LANE] = (t * c + _rotate_half(t) * s).astype(o_ref.dtype)

    tab = pl.BlockSpec((tr, LANE), lambda b, i: (i, 0))
    return pl.pallas_call(
        body, name=name, grid=(nb, nblk),
        in_specs=[pl.BlockSpec((1, tr, width), lambda b, i: (b, i, colblk)), tab, tab],
        out_specs=pl.BlockSpec((1, tr, width), lambda b, i: (b, i, 0)),
        out_shape=jax.ShapeDtypeStruct((nb, R, width), out_dtype),
        compiler_params=_cparams("arbitrary", "arbitrary"),
    )(x, cos, sin)


def rope_tables(n_ctx, seq):
    n_rows = seq // GRID_W
    row = np.repeat(np.arange(n_rows), GRID_W).astype(np.float32)
    col = np.tile(np.arange(GRID_W), n_rows).astype(np.float32)
    axis_dim = ROPE // 2
    inv_freq = jnp.asarray(ROPE_THETA, F32) ** (-jnp.arange(0, axis_dim, 2, dtype=F32) / axis_dim)
    ang_r = jnp.asarray(row)[:, None] * inv_freq
    ang_c = jnp.asarray(col)[:, None] * inv_freq
    ang = jnp.concatenate([ang_r, ang_r, ang_c, ang_c], axis=-1)
    cos = jnp.ones((n_ctx + seq, LANE), F32).at[n_ctx:, KR_LANE:KR_LANE + ROPE].set(jnp.cos(ang))
    sin = jnp.zeros((n_ctx + seq, LANE), F32).at[n_ctx:, KR_LANE:KR_LANE + ROPE].set(jnp.sin(ang))
    return cos, sin


Q_PRESCALE = ATTN_SCALE * math.log2(math.e)


def _attn_weights(q, kc):
    s2 = lax.dot_general(q, kc, (((1,), (1,)), ((), ())), preferred_element_type=F32)
    e = jnp.exp2(s2 - jnp.max(s2, axis=1, keepdims=True))
    return e, 1.0 / jnp.sum(e, axis=1, keepdims=True)


def _key_block(kv, kr):
    lane = lax.broadcasted_iota(jnp.int32, kv.shape, 1)
    return jnp.where(lane < NOPE, kv, kr)


def attn_fwd(q, kv, kr, tq):
    nb, S, _ = q.shape
    T = kv.shape[1]

    def body(q_ref, kv_ref, kr_ref, o_ref):
        kvv = kv_ref[0]
        e, r = _attn_weights(q_ref[0], _key_block(kvv, kr_ref[0]))
        o = lax.dot_general(e.astype(BF16), kvv, (((1,), (0,)), ((), ())), preferred_element_type=F32) * r
        lane = lax.broadcasted_iota(jnp.int32, o.shape, 1)
        o_ref[0] = jnp.where(lane >= NOPE, o, 0.0).astype(o_ref.dtype)

    return pl.pallas_call(
        body, name="attn_fwd", grid=(nb, N_HEADS, S // tq),
        in_specs=[pl.BlockSpec((1, tq, HEAD_BLOCK), lambda b, h, i: (b, i, h)),
                  pl.BlockSpec((1, T, HEAD_BLOCK), lambda b, h, i: (b, 0, h)),
                  pl.BlockSpec((1, T, HEAD_BLOCK), lambda b, h, i: (b, 0, 0))],
        out_specs=pl.BlockSpec((1, tq, HEAD_BLOCK), lambda b, h, i: (b, i, h)),
        out_shape=jax.ShapeDtypeStruct((nb, S, QP), BF16),
        compiler_params=_cparams("arbitrary", "arbitrary", "arbitrary"),
    )(q, kv, kr)


def attn_bwd(q, kv, kr, do, tq):
    nb, S, _ = q.shape
    T = kv.shape[1]

    def body(q_ref, kv_ref, kr_ref, do_ref, dq_ref, dkv_ref, dkr_ref):
        h, i = pl.program_id(1), pl.program_id(2)
        qv, kvv, dov = q_ref[0], kv_ref[0], do_ref[0]
        kc = _key_block(kvv, kr_ref[0])
        e, r = _attn_weights(qv, kc)
        dor = (dov.astype(F32) * r).astype(BF16)
        dpr = lax.dot_general(dor, kvv, (((1,), (1,)), ((), ())), preferred_element_type=F32)
        ds = (e * (dpr - r * jnp.sum(dpr * e, axis=1, keepdims=True))).astype(BF16)
        dq = lax.dot_general(ds, kc, (((1,), (0,)), ((), ())), preferred_element_type=F32)
        dq_ref[0] = (dq * ATTN_SCALE).astype(dq_ref.dtype)
        dkc = lax.dot_general(ds, qv, (((0,), (0,)), ((), ())), preferred_element_type=F32) * math.log(2.0)
        dv = lax.dot_general(e.astype(BF16), dor, (((0,), (0,)), ((), ())), preferred_element_type=F32)
        lane = lax.broadcasted_iota(jnp.int32, dkc.shape, 1)
        dkv = jnp.where(lane < NOPE, dkc, dv)
        dkr = jnp.where(lane >= NOPE, dkc, 0.0)

        @pl.when(i == 0)
        def _():
            dkv_ref[0] = dkv

        @pl.when(i > 0)
        def _():
            dkv_ref[0] += dkv

        first = jnp.logical_and(h == 0, i == 0)

        @pl.when(first)
        def _():
            dkr_ref[0] = dkr

        @pl.when(jnp.logical_not(first))
        def _():
            dkr_ref[0] += dkr

    qspec = pl.BlockSpec((1, tq, HEAD_BLOCK), lambda b, h, i: (b, i, h))
    kspec = pl.BlockSpec((1, T, HEAD_BLOCK), lambda b, h, i: (b, 0, h))
    rspec = pl.BlockSpec((1, T, HEAD_BLOCK), lambda b, h, i: (b, 0, 0))
    return pl.pallas_call(
        body, name="attn_bwd", grid=(nb, N_HEADS, S // tq),
        in_specs=[qspec, kspec, rspec, qspec], out_specs=[qspec, kspec, rspec],
        out_shape=[jax.ShapeDtypeStruct((nb, S, QP), F32), jax.ShapeDtypeStruct((nb, T, QP), F32),
                   jax.ShapeDtypeStruct((nb, T, HEAD_BLOCK), F32)],
        compiler_params=_cparams("arbitrary", "arbitrary", "arbitrary"),
    )(q, kv, kr, do)


def _seg_bounds(n, n_ctx):
    t = lax.broadcasted_iota(jnp.int32, (n, 1), 0)
    if n_ctx == 0:
        return t, jnp.zeros_like(t), jnp.full_like(t, n)
    in_ctx = t < n_ctx
    return t, jnp.where(in_ctx, 0, n_ctx), jnp.where(in_ctx, n_ctx, n)


def _shift_rows(x, o, bounds):
    if o == 0:
        return x
    t, lo, hi = bounds
    n = x.shape[0]
    valid = jnp.logical_and(t + o >= lo, t + o < hi).astype(F32)
    return pltpu.roll(x, (-o) % n, 0) * valid


def _conv(x, w, bias, k, bounds):
    acc = bias
    for o in range(k):
        acc = acc + w[o:o + 1, :] * _shift_rows(x, o - k // 2, bounds)
    return acc


def _conv_bwd(x, w, dpre, k, bounds):
    dx = jnp.zeros_like(x)
    rows = []
    for o in range(k):
        dx = dx + w[o:o + 1, :] * _shift_rows(dpre, -(o - k // 2), bounds)
        rows.append(jnp.sum(dpre * _shift_rows(x, o - k // 2, bounds), axis=0, keepdims=True))
    rows.append(jnp.sum(dpre, axis=0, keepdims=True))
    sub8 = lax.broadcasted_iota(jnp.int32, (8, x.shape[1]), 0)
    out = jnp.zeros((8, x.shape[1]), F32)
    for o, r in enumerate(rows):
        out = out + jnp.where(sub8 == o, r, 0.0)
    return dx, out


def _gelu(x):
    return 0.5 * x * (1.0 + lax.erf(x * (1.0 / math.sqrt(2.0))))


def _gelu_grad(x):
    return 0.5 * (1.0 + lax.erf(x * (1.0 / math.sqrt(2.0)))) + x * jnp.exp(-0.5 * x * x) * (1.0 / math.sqrt(2.0 * math.pi))


def ssd_conv_fwd(u, w8, bias, n_ctx, tc):
    nb, T, _ = u.shape
    cb0 = OFF_XBC // tc

    def body(x_ref, w_ref, b_ref, o_ref):
        pre = _conv(x_ref[0], w_ref[...], b_ref[...], SSD_K, _seg_bounds(T, n_ctx))
        o_ref[0] = _silu(pre)

    return pl.pallas_call(
        body, name="ssd_conv_fwd", grid=(nb, XBC // tc),
        in_specs=[pl.BlockSpec((1, T, tc), lambda b, j: (b, 0, cb0 + j)),
                  pl.BlockSpec((8, tc), lambda b, j: (0, j)), pl.BlockSpec((1, tc), lambda b, j: (0, j))],
        out_specs=pl.BlockSpec((1, T, tc), lambda b, j: (b, 0, j)),
        out_shape=jax.ShapeDtypeStruct((nb, T, XBC), F32),
        compiler_params=_cparams("arbitrary", "arbitrary"),
    )(u, w8, bias)


def ssd_conv_bwd(u, w8, bias, dxbc, dxs_direct, n_ctx, tc):
    nb, T, _ = u.shape
    cb0 = OFF_XBC // tc
    n_direct = D_INNER // tc

    def body(x_ref, w_ref, b_ref, d0_ref, d1_ref, dd_ref, dx_ref, dw_ref, acc):
        j, b = pl.program_id(0), pl.program_id(1)
        acc[...] = d0_ref[0, 0] + d1_ref[0, 0]

        @pl.when(j < n_direct)
        def _():
            acc[n_ctx:, :] += dd_ref[0]

        bounds = _seg_bounds(T, n_ctx)
        x, w = x_ref[0], w_ref[...]
        pre = _conv(x, w, b_ref[...], SSD_K, bounds)
        sg = jax.nn.sigmoid(pre)
        dpre = acc[...] * (sg * (1.0 + pre * (1.0 - sg)))
        dx, rows = _conv_bwd(x, w, dpre, SSD_K, bounds)
        dx_ref[0] = dx.astype(dx_ref.dtype)

        @pl.when(b == 0)
        def _():
            dw_ref[...] = rows

        @pl.when(b > 0)
        def _():
            dw_ref[...] += rows

    dspec0 = pl.BlockSpec((1, 1, T, tc), lambda j, b: (0, b, 0, j))
    dspec1 = pl.BlockSpec((1, 1, T, tc), lambda j, b: (1, b, 0, j))
    return pl.pallas_call(
        body, name="ssd_conv_bwd", grid=(XBC // tc, nb),
        in_specs=[pl.BlockSpec((1, T, tc), lambda j, b: (b, 0, cb0 + j)),
                  pl.BlockSpec((8, tc), lambda j, b: (0, j)), pl.BlockSpec((1, tc), lambda j, b: (0, j)),
                  dspec0, dspec1,
                  pl.BlockSpec((1, T - n_ctx, tc), lambda j, b: (b, 0, jnp.minimum(j, n_direct - 1)))],
        out_specs=[pl.BlockSpec((1, T, tc), lambda j, b: (b, 0, j)), pl.BlockSpec((8, tc), lambda j, b: (0, j))],
        out_shape=[jax.ShapeDtypeStruct((nb, T, XBC), BF16), jax.ShapeDtypeStruct((8, XBC), F32)],
        scratch_shapes=[pltpu.VMEM((T, tc), F32)],
        compiler_params=_cparams("arbitrary", "arbitrary"),
    )(u, w8, bias, dxbc, dxbc, dxs_direct)


GLU_TC = 256


def glu_interleave(w_up):
    blocks = []
    for j in range(D_FF // GLU_TC):
        blocks += [w_up[:, j * GLU_TC:(j + 1) * GLU_TC], w_up[:, D_FF + j * GLU_TC:D_FF + (j + 1) * GLU_TC]]
    return jnp.concatenate(blocks, axis=1)


def glu_deinterleave(g):
    nj = D_FF // GLU_TC
    gate = [g[:, 2 * j * GLU_TC:(2 * j + 1) * GLU_TC] for j in range(nj)]
    val = [g[:, (2 * j + 1) * GLU_TC:(2 * j + 2) * GLU_TC] for j in range(nj)]
    return jnp.concatenate(gate + val, axis=1)


def glu_fwd(up, w8, bias):
    nb, S, _ = up.shape
    tc = GLU_TC

    def body(u_ref, w_ref, b_ref, o_ref):
        gc = _conv(u_ref[0, :, :tc], w_ref[...], b_ref[...], FFN_K, _seg_bounds(S, 0))
        o_ref[0] = (_gelu(gc) * u_ref[0, :, tc:]).astype(o_ref.dtype)

    return pl.pallas_call(
        body, name="glu_fwd", grid=(nb, D_FF // tc),
        in_specs=[pl.BlockSpec((1, S, 2 * tc), lambda b, j: (b, 0, j)),
                  pl.BlockSpec((8, tc), lambda b, j: (0, j)), pl.BlockSpec((1, tc), lambda b, j: (0, j))],
        out_specs=pl.BlockSpec((1, S, tc), lambda b, j: (b, 0, j)),
        out_shape=jax.ShapeDtypeStruct((nb, S, D_FF), BF16),
        compiler_params=_cparams("arbitrary", "arbitrary"),
    )(up, w8, bias)


def glu_bwd(up, w8, bias, dact):
    nb, S, _ = up.shape
    tc = GLU_TC

    def body(u_ref, w_ref, b_ref, d_ref, du_ref, dw_ref):
        b = pl.program_id(1)
        bounds = _seg_bounds(S, 0)
        x, w, val, d = u_ref[0, :, :tc], w_ref[...], u_ref[0, :, tc:], d_ref[0].astype(F32)
        gc = _conv(x, w, b_ref[...], FFN_K, bounds)
        du_ref[0, :, tc:] = (d * _gelu(gc)).astype(du_ref.dtype)
        dx, rows = _conv_bwd(x, w, d * val * _gelu_grad(gc), FFN_K, bounds)
        du_ref[0, :, :tc] = dx.astype(du_ref.dtype)

        @pl.when(b == 0)
        def _():
            dw_ref[...] = rows

        @pl.when(b > 0)
        def _():
            dw_ref[...] += rows

    pair = pl.BlockSpec((1, S, 2 * tc), lambda j, b: (b, 0, j))
    return pl.pallas_call(
        body, name="glu_bwd", grid=(D_FF // tc, nb),
        in_specs=[pair, pl.BlockSpec((8, tc), lambda j, b: (0, j)), pl.BlockSpec((1, tc), lambda j, b: (0, j)),
                  pl.BlockSpec((1, S, tc), lambda j, b: (b, 0, j))],
        out_specs=[pair, pl.BlockSpec((8, tc), lambda j, b: (0, j))],
        out_shape=[jax.ShapeDtypeStruct((nb, S, 2 * D_FF), BF16), jax.ShapeDtypeStruct((8, D_FF), F32)],
        compiler_params=_cparams("arbitrary", "arbitrary"),
    )(up, w8, bias, dact)


def _chunk_of(d, k, n_cc, n_ch):
    rev = jnp.where(k < n_cc, n_cc - 1 - k, n_cc + n_ch - 1 - k)
    return jnp.where(d == 1, rev, k)


def _lane_pick(v, lane_iota, l):
    return jnp.sum(jnp.where(lane_iota == l, v, 0.0), axis=1, keepdims=True)


def head_spread_matrix():
    return (jnp.arange(LANE)[:, None] == (jnp.arange(D_INNER)[None, :] // SSD_P)).astype(BF16)


def _split_dot(x, e, dims):
    hi = x.astype(BF16)
    lo = (x - hi.astype(F32)).astype(BF16)
    return (lax.dot_general(hi, e, dims, preferred_element_type=F32)
            + lax.dot_general(lo, e, dims, preferred_element_type=F32))


def _spread(x, e):
    return _split_dot(x, e, (((1,), (0,)), ((), ())))


def _gather_heads(y, e):
    return _split_dot(y, e, (((1,), (1,)), ((), ())))


def _softplus(x):
    return jnp.maximum(x, 0.0) + jnp.log(1.0 + jnp.exp(-jnp.abs(x)))


def ssd_dt_inputs(u, a_log, dt_bias):
    pad = LANE - SSD_HEADS
    dt = u[..., OFF_DT:OFF_DT + 2 * SSD_HEADS]
    dt2 = jnp.stack([jnp.pad(dt[..., i * SSD_HEADS:(i + 1) * SSD_HEADS], ((0, 0), (0, 0), (0, pad))) for i in range(2)])

    def lanes(v):
        return jnp.pad(v.reshape(2, 1, SSD_HEADS), ((0, 0), (0, 0), (0, pad)))

    return dt2, lanes(a_log), lanes(dt_bias)


def _ssd_common(d, dt_raw, alog, dtb):
    Q = dt_raw.shape[0]
    row = lax.broadcasted_iota(jnp.int32, (Q, Q), 0)
    col = lax.broadcasted_iota(jnp.int32, (Q, Q), 1)
    rev = d == 1
    maskb = jnp.where(rev, row, col) <= jnp.where(rev, col, row)
    tri = maskb.astype(F32)
    A = -jnp.exp(alog)
    dtv = _softplus(dt_raw + dtb)
    a = dtv * A
    cum = lax.dot_general(tri, a, (((1,), (0,)), ((), ())), precision=lax.Precision.HIGHEST, preferred_element_type=F32)
    tot = jnp.sum(a, axis=0, keepdims=True)
    return maskb, tri, A, dtv, cum, tot


def ssd_fwd(xbc, dt2, alog2, dtb2, n_ctx, hosted):
    nb, T, _ = xbc.shape
    S = T - n_ctx
    n_ch, n_cc = T // CHUNK, n_ctx // CHUNK
    Q = CHUNK
    n_pairs = SSD_HEADS // 2
    n_ex = hosted.n
    n_in = 5

    def body(*refs):
        x_ref, dt_ref, al_ref, db_ref, e_ref = refs[:n_in]
        send_refs = refs[n_in:n_in + n_ex]
        y_ref, hin_ref = refs[n_in + n_ex:n_in + 2 + n_ex]
        recv_refs = refs[n_in + 2 + n_ex:n_in + 2 + 2 * n_ex]
        H, *sems = refs[n_in + 2 + 2 * n_ex:]
        d, k = pl.program_id(1), pl.program_id(2)
        first_step = jnp.logical_and(jnp.logical_and(pl.program_id(0) == 0, d == 0), k == 0)
        last_step = jnp.logical_and(jnp.logical_and(pl.program_id(0) == nb - 1, d == 1), k == n_ch - 1)
        begin_exchange, end_exchange = hosted.steps(send_refs, recv_refs, sems, first_step, last_step)
        begin_exchange()

        @pl.when(k == 0)
        def _():
            H[...] = jnp.zeros_like(H)

        maskb, tri, A, dtv, cum, tot = _ssd_common(d, dt_ref[0, 0], al_ref[0], db_ref[0])
        e = e_ref[...]
        cumT = cum.T
        cum_e, dt_e = _spread(cum, e), _spread(dtv, e)
        tot_e = _spread(jnp.broadcast_to(tot, (8, LANE)), e)[0:1]
        hin_ref[0, 0, 0] = H[...].astype(BF16)
        lane = lax.broadcasted_iota(jnp.int32, (Q, LANE), 1)
        lane1 = lax.broadcasted_iota(jnp.int32, (1, LANE), 1)
        subc = lax.broadcasted_iota(jnp.int32, (LANE, 1), 0)
        half = lane < SSD_P
        for g in range(SSD_GROUPS):
            Bg = x_ref[0, :, D_INNER + g * SSD_N:D_INNER + (g + 1) * SSD_N].astype(BF16)
            Cg = x_ref[0, :, D_INNER + GN + g * SSD_N:D_INNER + GN + (g + 1) * SSD_N].astype(BF16)
            Gm = lax.dot_general(Cg, Bg, (((1,), (1,)), ((), ())), preferred_element_type=F32)
            for pr in range(n_pairs // SSD_GROUPS):
                p = g * (n_pairs // SSD_GROUPS) + pr
                sc, dtp, totp = [t[:, p * LANE:(p + 1) * LANE] for t in (cum_e, dt_e, tot_e)]
                swapped = pltpu.roll(sc, SSD_P, 1)
                s0c, s1c = jnp.where(half, sc, swapped), jnp.where(half, swapped, sc)
                s0r, s1r = cumT[2 * p:2 * p + 1, :], cumT[2 * p + 1:2 * p + 2, :]
                tot0, tot1 = _lane_pick(tot, lane1, 2 * p), _lane_pick(tot, lane1, 2 * p + 1)
                M0 = (Gm * jnp.exp(jnp.where(maskb, s0c - s0r, NEG_BIG))).astype(BF16)
                M1 = (Gm * jnp.exp(jnp.where(maskb, s1c - s1r, NEG_BIG))).astype(BF16)
                xd = x_ref[0, :, p * LANE:(p + 1) * LANE] * dtp
                xdb = xd.astype(BF16)
                yd = jnp.where(half,
                               lax.dot_general(M0, xdb, (((1,), (0,)), ((), ())), preferred_element_type=F32),
                               lax.dot_general(M1, xdb, (((1,), (0,)), ((), ())), preferred_element_type=F32))
                Hp = H[p * LANE:(p + 1) * LANE, :]
                yo = lax.dot_general(Cg, Hp.astype(BF16), (((1,), (1,)), ((), ())), preferred_element_type=F32) * jnp.exp(sc)

                y_ref[0, 0, :, p * LANE:(p + 1) * LANE] = yd + yo

                xdw = (xd * jnp.exp(totp - sc)).astype(BF16)
                etot = jnp.exp(jnp.where(subc < SSD_P, tot0, tot1))
                H[p * LANE:(p + 1) * LANE, :] = Hp * etot + lax.dot_general(
                    xdw, Bg, (((0,), (0,)), ((), ())), preferred_element_type=F32)
        end_exchange()

    def ymap(b, d, k):
        return (d, b, _chunk_of(d, jnp.maximum(k, n_cc), n_cc, n_ch) - n_cc, 0)

    return pl.pallas_call(
        body, name="ssd_fwd", grid=(nb, 2, n_ch),
        in_specs=[pl.BlockSpec((1, Q, XBC), lambda b, d, k: (b, _chunk_of(d, k, n_cc, n_ch), 0)),
                  pl.BlockSpec((1, 1, Q, LANE), lambda b, d, k: (d, b, _chunk_of(d, k, n_cc, n_ch), 0)),
                  pl.BlockSpec((1, 1, LANE), lambda b, d, k: (d, 0, 0)), pl.BlockSpec((1, 1, LANE), lambda b, d, k: (d, 0, 0)),
                  pl.BlockSpec((LANE, D_INNER), lambda b, d, k: (0, 0))] + hosted.specs,
        out_specs=[pl.BlockSpec((1, 1, Q, D_INNER), ymap),
                   pl.BlockSpec((1, 1, 1, D_INNER, SSD_N), lambda b, d, k: (d, b, k, 0, 0))] + hosted.specs,
        out_shape=[jax.ShapeDtypeStruct((2, nb, S, D_INNER), F32),
                   jax.ShapeDtypeStruct((2, nb, n_ch, D_INNER, SSD_N), BF16)] + hosted.out_shape,
        scratch_shapes=[pltpu.VMEM((D_INNER, SSD_N), F32)] + hosted.scratch,
        compiler_params=_cparams("arbitrary", "arbitrary", "arbitrary"),
    )(xbc, dt2, alog2, dtb2, head_spread_matrix(), *hosted.arrays)


def ssd_bwd(xbc, dt2, alog2, dtb2, hin, dy, n_ctx, hosted):
    nb, T, _ = xbc.shape
    n_ex = hosted.n
    n_ch, n_cc = T // CHUNK, n_ctx // CHUNK
    n_in = 7
    Q = CHUNK
    n_pairs = SSD_HEADS // 2
    NT = (((1,), (1,)), ((), ()))
    NN = (((1,), (0,)), ((), ()))
    TN = (((0,), (0,)), ((), ()))

    def dot(a, b, dims):
        return lax.dot_general(a.astype(BF16), b.astype(BF16), dims, preferred_element_type=F32)

    def body(*refs):
        x_ref, dt_ref, al_ref, db_ref, e_ref, hin_ref, dy_ref = refs[:n_in]
        send_refs = refs[n_in:n_in + n_ex]
        dx_ref, ddt_ref, st_ref = refs[n_in + n_ex:n_in + 3 + n_ex]
        recv_refs = refs[n_in + 3 + n_ex:n_in + 3 + 2 * n_ex]
        dH, dce, dde, *sems = refs[n_in + 3 + 2 * n_ex:]
        d, kk = pl.program_id(1), pl.program_id(2)
        ks = n_ch - 1 - kk
        first_step = jnp.logical_and(jnp.logical_and(pl.program_id(0) == 0, d == 0), kk == 0)
        last_step = jnp.logical_and(jnp.logical_and(pl.program_id(0) == nb - 1, d == 1), kk == n_ch - 1)
        begin_exchange, end_exchange = hosted.steps(send_refs, recv_refs, sems, first_step, last_step)
        begin_exchange()

        @pl.when(kk == 0)
        def _():
            dH[...] = jnp.zeros_like(dH)

        @pl.when(jnp.logical_and(jnp.logical_and(pl.program_id(0) == 0, d == 0), kk == 0))
        def _():
            st_ref[...] = jnp.zeros_like(st_ref)

        dt_raw = dt_ref[0, 0]
        alog, dtb_v = al_ref[0], db_ref[0]
        maskb, tri, A, dtv, cum, tot = _ssd_common(d, dt_raw, alog, dtb_v)
        e = e_ref[...]
        cumT = cum.T
        cum_e, dt_e = _spread(cum, e), _spread(dtv, e)
        tot_e = _spread(jnp.broadcast_to(tot, (8, LANE)), e)[0:1]
        live = (ks >= n_cc).astype(F32)
        lane = lax.broadcasted_iota(jnp.int32, (Q, LANE), 1)
        lane1 = lax.broadcasted_iota(jnp.int32, (1, LANE), 1)
        sub = lax.broadcasted_iota(jnp.int32, (LANE, Q), 0)
        subc = lax.broadcasted_iota(jnp.int32, (LANE, 1), 0)
        half = lane < SSD_P
        halfc = subc < SSD_P
        ones = jnp.ones((LANE, LANE), BF16)
        dcum = jnp.zeros((Q, LANE), F32)
        dcumT = jnp.zeros((LANE, Q), F32)
        dtot = jnp.zeros((1, LANE), F32)
        dtot_parts = []
        for g in range(SSD_GROUPS):
            Bg = x_ref[0, :, D_INNER + g * SSD_N:D_INNER + (g + 1) * SSD_N].astype(BF16)
            Cg = x_ref[0, :, D_INNER + GN + g * SSD_N:D_INNER + GN + (g + 1) * SSD_N].astype(BF16)
            Gm = lax.dot_general(Cg, Bg, NT, preferred_element_type=F32)
            dG = jnp.zeros((Q, Q), F32)
            dC = jnp.zeros((Q, SSD_N), F32)
            dB = jnp.zeros((Q, SSD_N), F32)
            for pr in range(n_pairs // SSD_GROUPS):
                p = g * (n_pairs // SSD_GROUPS) + pr
                l0, l1 = 2 * p, 2 * p + 1
                sc, dtp, totp = [t[:, p * LANE:(p + 1) * LANE] for t in (cum_e, dt_e, tot_e)]
                swapped = pltpu.roll(sc, SSD_P, 1)
                s0c, s1c = jnp.where(half, sc, swapped), jnp.where(half, swapped, sc)
                s0r, s1r = cumT[l0:l0 + 1, :], cumT[l1:l1 + 1, :]
                tot0, tot1 = _lane_pick(tot, lane1, l0), _lane_pick(tot, lane1, l1)
                L0 = jnp.exp(jnp.where(maskb, s0c - s0r, NEG_BIG))
                L1 = jnp.exp(jnp.where(maskb, s1c - s1r, NEG_BIG))
                M0, M1 = Gm * L0, Gm * L1
                xs = x_ref[0, :, p * LANE:(p + 1) * LANE]
                xd = xs * dtp
                es = jnp.exp(sc)
                dte = jnp.exp(totp - sc)
                etot = jnp.exp(jnp.where(halfc, tot0, tot1))
                dyp = dy_ref[0, :, p * LANE:(p + 1) * LANE] * live
                Hp = hin_ref[0, 0, 0, p * LANE:(p + 1) * LANE, :]
                dHp = dH[p * LANE:(p + 1) * LANE, :]
                bdh = dot(Bg, dHp, NT)
                dxd = jnp.where(half, dot(M0, dyp, TN), dot(M1, dyp, TN)) + bdh * dte
                dy0 = jnp.where(half, dyp, 0.0)
                dy1 = dyp - dy0
                dM0, dM1 = dot(dy0, xd, NT), dot(dy1, xd, NT)
                dG = dG + dM0 * L0 + dM1 * L1
                dyes = dyp * es
                xdw = xd * dte
                dC = dC + dot(dyes, Hp, NN)
                dB = dB + dot(xdw, dHp, NN)
                W0, W1 = dM0 * M0, dM1 * M1
                yoff = dot(Cg, Hp, NT) * es
                r_off = dyp * yoff
                r_st = xd * bdh * dte
                hh = jnp.sum(dHp * Hp.astype(F32), axis=1, keepdims=True) * etot
                dce[:, p * LANE:(p + 1) * LANE] = r_off - r_st
                dde[:, p * LANE:(p + 1) * LANE] = dxd * xs
                dtot_parts.append(jnp.sum(r_st, axis=0, keepdims=True))
                for (l, W, hselc) in ((l0, W0, halfc), (l1, W1, jnp.logical_not(halfc))):
                    col_g = _split_dot(W, ones, NN)
                    row_g = -jnp.sum(W, axis=0, keepdims=True)
                    dcum = dcum + jnp.where(lane == l, col_g, 0.0)
                    dcumT = dcumT + jnp.where(sub == l, row_g, 0.0)
                    dtot = dtot + jnp.where(lane1 == l, jnp.sum(jnp.where(hselc, hh, 0.0), axis=0, keepdims=True), 0.0)
                dx_ref[0, 0, :, p * LANE:(p + 1) * LANE] = dxd * dtp
                dH[p * LANE:(p + 1) * LANE, :] = dHp * etot + dot(dyes, Cg, TN)
            dx_ref[0, 0, :, D_INNER + g * SSD_N:D_INNER + (g + 1) * SSD_N] = dB + dot(dG, Cg, TN)
            dx_ref[0, 0, :, D_INNER + GN + g * SSD_N:D_INNER + GN + (g + 1) * SSD_N] = dC + dot(dG, Bg, NN)
        dcum_all = dcum + dcumT.T + _gather_heads(dce[...], e)
        dtot_e = jnp.broadcast_to(jnp.concatenate(dtot_parts, axis=1), (8, D_INNER))
        dtot = dtot + _gather_heads(dtot_e, e)[0:1]
        da = lax.dot_general(tri, dcum_all, TN, precision=lax.Precision.HIGHEST, preferred_element_type=F32) + dtot
        ddtv = _gather_heads(dde[...], e) + da * A
        ddt_raw = ddtv * jax.nn.sigmoid(dt_raw + dtb_v)
        ddt_ref[0, 0] = ddt_raw
        sub8 = lax.broadcasted_iota(jnp.int32, (8, LANE), 0)
        st_ref[...] += (jnp.where(sub8 == 2 * d, jnp.sum(da * dtv * A, axis=0, keepdims=True), 0.0)
                        + jnp.where(sub8 == 2 * d + 1, jnp.sum(ddt_raw, axis=0, keepdims=True), 0.0))
        end_exchange()

    def cmap(d, kk):
        return _chunk_of(d, n_ch - 1 - kk, n_cc, n_ch)

    def dymap(b, d, kk):
        return (b, _chunk_of(d, jnp.maximum(n_ch - 1 - kk, n_cc), n_cc, n_ch) - n_cc, 0)

    return pl.pallas_call(
        body, name="ssd_bwd", grid=(nb, 2, n_ch),
        in_specs=[pl.BlockSpec((1, Q, XBC), lambda b, d, kk: (b, cmap(d, kk), 0)),
                  pl.BlockSpec((1, 1, Q, LANE), lambda b, d, kk: (d, b, cmap(d, kk), 0)),
                  pl.BlockSpec((1, 1, LANE), lambda b, d, kk: (d, 0, 0)), pl.BlockSpec((1, 1, LANE), lambda b, d, kk: (d, 0, 0)),
                  pl.BlockSpec((LANE, D_INNER), lambda b, d, kk: (0, 0)),
                  pl.BlockSpec((1, 1, 1, D_INNER, SSD_N), lambda b, d, kk: (d, b, n_ch - 1 - kk, 0, 0)),
                  pl.BlockSpec((1, Q, D_INNER), dymap)] + hosted.specs,
        out_specs=[pl.BlockSpec((1, 1, Q, XBC), lambda b, d, kk: (d, b, cmap(d, kk), 0)),
                   pl.BlockSpec((1, 1, Q, LANE), lambda b, d, kk: (d, b, cmap(d, kk), 0)),
                   pl.BlockSpec((8, LANE), lambda b, d, kk: (0, 0))] + hosted.specs,
        out_shape=[jax.ShapeDtypeStruct((2, nb, T, XBC), F32), jax.ShapeDtypeStruct((2, nb, T, LANE), F32),
                   jax.ShapeDtypeStruct((8, LANE), F32)] + hosted.out_shape,
        scratch_shapes=[pltpu.VMEM((D_INNER, SSD_N), F32), pltpu.VMEM((Q, D_INNER), F32), pltpu.VMEM((Q, D_INNER), F32)] + hosted.scratch,
        compiler_params=_cparams("arbitrary", "arbitrary", "arbitrary"),
    )(xbc, dt2, alog2, dtb2, head_spread_matrix(), hin, dy, *hosted.arrays)


def _adamw(w, g, m, v):
    mn = ADAM_B1 * m + (1.0 - ADAM_B1) * g
    vn = ADAM_B2 * v + (1.0 - ADAM_B2) * jnp.square(g)
    m_hat = mn / (1.0 - ADAM_B1 ** ADAM_STEP)
    v_hat = vn / (1.0 - ADAM_B2 ** ADAM_STEP)
    return -ADAM_LR * (m_hat / (jnp.sqrt(v_hat) + ADAM_EPS) + ADAM_WD * w), mn, vn


def adamw_matrix(name, w, g_slots, m, v):
    K, n = w.shape
    s = g_slots.shape[0]
    tr = _tile(K, 256, 8)

    def body(w_ref, g_ref, m_ref, v_ref, go_ref, d_ref, mo_ref, vo_ref):
        g = g_ref[0].astype(F32)
        for j in range(1, s):
            g = g + g_ref[j].astype(F32)
        go_ref[...] = g
        d_ref[...], mo_ref[...], vo_ref[...] = _adamw(w_ref[...], g, m_ref[...], v_ref[...])

    spec = pl.BlockSpec((tr, n), lambda i: (i, 0))
    return pl.pallas_call(
        body, name=name, grid=(K // tr,),
        in_specs=[spec, pl.BlockSpec((s, tr, n), lambda i: (0, i, 0)), spec, spec], out_specs=[spec] * 4,
        out_shape=[jax.ShapeDtypeStruct((K, n), F32)] * 4,
        compiler_params=_cparams("arbitrary"),
    )(w, g_slots, m, v)


def adamw_small(ws, gs, ms, vs):
    n = len(ws)

    def body(*refs):
        for i in range(n):
            d, mn, vn = _adamw(refs[i][...], refs[n + i][...], refs[2 * n + i][...], refs[3 * n + i][...])
            refs[4 * n + i][...] = d
            refs[5 * n + i][...] = mn
            refs[6 * n + i][...] = vn

    shapes = [jax.ShapeDtypeStruct(w.shape, F32) for w in ws]
    out = pl.pallas_call(body, name="adamw_small", out_shape=shapes * 3)(*ws, *gs, *ms, *vs)
    return out[:n], out[n:2 * n], out[2 * n:]


def sum_slots(name, x):
    n = x.shape[0]

    def fn(t):
        acc = t[0]
        for j in range(1, n):
            acc = acc + t[j]
        return (acc,)

    return ew_call(name, fn, [x], [(x.shape[1:], F32)])[0]


def _pack_rows(parts):
    rows = []
    for p in parts:
        flat = p.reshape(1, -1)
        n = flat.shape[1]
        rows.append(jnp.pad(flat, ((0, 0), (0, -(-n // (8 * LANE)) * 8 * LANE - n))).reshape(-1, LANE))
    return jnp.concatenate(rows, axis=0)


def _unpack_rows(pack, shapes):
    out, r = [], 0
    for s in shapes:
        n = int(np.prod(s))
        nr = -(-n // (8 * LANE)) * 8
        out.append(pack[r:r + nr].reshape(1, -1)[:, :n].reshape(s))
        r += nr
    return out


def _mesh_pos():
    return lax.axis_index("x"), lax.axis_index("y"), lax.axis_index("c")


N_PEERS = N_DEV - 1


def all_gather(name, vs):
    n = len(vs)

    def body(*refs):
        _ag_start(refs[:n], refs[n:2 * n], *refs[2 * n:])
        _ag_finish(refs[:n], refs[n:2 * n], *refs[2 * n:])

    hbm = pl.BlockSpec(memory_space=pl.ANY)
    return pl.pallas_call(
        body, name=name, out_shape=_ag_out_shape(vs), in_specs=[hbm] * n, out_specs=[hbm] * n,
        scratch_shapes=_a2a_scratch(n),
    )(*vs)


def _ag_out_shape(vs):
    return [jax.ShapeDtypeStruct((N_DEV,) + v.shape, v.dtype) for v in vs]


def _ag_copies(x_refs, out_refs, send_sems, recv_sems, local_sems):
    n = len(x_refs)
    x, y, c = _mesh_pos()
    me, sibling = (x, y, c), (x, y, 1 - c)
    chips = [(1 - x, y), (x, 1 - y), (1 - x, 1 - y)]

    def slot(a, px, py, pc):
        return out_refs[a].at[4 * px + 2 * py + pc]

    def copy(a, k, block, to, src=None):
        return pltpu.make_async_remote_copy(
            src_ref=slot(a, *block) if src is None else src, dst_ref=slot(a, *block),
            send_sem=send_sems.at[N_PEERS * a + k], recv_sem=recv_sems.at[N_PEERS * a + k],
            device_id=to, device_id_type=MESH)

    local = [pltpu.make_async_copy(x_refs[a], slot(a, *me), local_sems.at[a]) for a in range(n)]
    first = []
    for a in range(n):
        first.append(copy(a, 0, me, sibling, src=x_refs[a]))
        first += [copy(a, 1 + j, me, (*chip, c), src=x_refs[a]) for j, chip in enumerate(chips)]
    passed = [(copy(a, 1 + j, (*chip, c), me), copy(a, 4 + j, (*chip, c), sibling))
              for j, chip in enumerate(chips) for a in range(n)]
    from_sibling = []
    for a in range(n):
        from_sibling.append(copy(a, 0, sibling, me))
        from_sibling += [copy(a, 4 + j, (*chip, 1 - c), me) for j, chip in enumerate(chips)]
    return local, first, passed, from_sibling


def _ag_start(*refs):
    local, first, _, _ = _ag_copies(*refs)
    for cp in local + first:
        cp.start()


def _ag_finish(*refs):
    local, first, passed, from_sibling = _ag_copies(*refs)
    for arrived, hand_on in passed:
        arrived.wait_recv()
        hand_on.start()
    for cp in from_sibling:
        cp.wait_recv()
    for cp in first + [hand_on for _, hand_on in passed]:
        cp.wait_send()
    for cp in local:
        cp.wait()


def _a2a_scratch(n):
    return [pltpu.SemaphoreType.DMA((N_PEERS * n,)), pltpu.SemaphoreType.DMA((N_PEERS * n,)), pltpu.SemaphoreType.DMA((n,))]


def _a2a_copies(x_refs, out_refs, send_sems, recv_sems, local_sems):
    n = len(x_refs)
    x, y, c = _mesh_pos()
    me = 4 * x + 2 * y + c
    local = [pltpu.make_async_copy(x_refs[a].at[me], out_refs[a].at[me], local_sems.at[a]) for a in range(n)]
    remote = []
    for k in range(1, N_DEV):
        px, py, pc = x ^ ((k >> 2) & 1), y ^ ((k >> 1) & 1), c ^ (k & 1)
        for a in range(n):
            remote.append(pltpu.make_async_remote_copy(
                src_ref=x_refs[a].at[4 * px + 2 * py + pc], dst_ref=out_refs[a].at[me],
                send_sem=send_sems.at[N_PEERS * a + k - 1], recv_sem=recv_sems.at[N_PEERS * a + k - 1],
                device_id=(px, py, pc), device_id_type=MESH))
    return local, remote


def _a2a_start(local, remote):
    for cp in local + remote:
        cp.start()


def _a2a_wait(local, remote):
    for cp in remote:
        cp.wait_recv()
    for cp in remote:
        cp.wait_send()
    for cp in local:
        cp.wait()


class Hosted:
    def __init__(self, start=None, finish=None, arrays=(), out_shape=()):
        self.start, self.finish, self.arrays, self.out_shape = start, finish, list(arrays), list(out_shape)
        self.n = len(self.arrays)
        self.specs = [pl.BlockSpec(memory_space=pl.ANY)] * self.n
        self.scratch = _a2a_scratch(self.n) if self.n else []

    def steps(self, send_refs, recv_refs, sems, first_step, last_step):
        def begin():
            if self.n:
                pl.when(first_step)(lambda: self.start(send_refs, recv_refs, *sems))

        def end():
            if self.n:
                pl.when(last_step)(lambda: self.finish(send_refs, recv_refs, *sems))

        return begin, end


def hosted_all_to_all(vs):
    return Hosted(lambda *r: _a2a_start(*_a2a_copies(*r)), lambda *r: _a2a_wait(*_a2a_copies(*r)), vs,
                  [jax.ShapeDtypeStruct(v.shape, v.dtype) for v in vs])


def hosted_all_gather(vs):
    return Hosted(_ag_start, _ag_finish, vs, _ag_out_shape(vs))


def _taps8(w):
    return jnp.concatenate([w, jnp.zeros((8 - w.shape[0], w.shape[1]), w.dtype)], axis=0)


FIRST = ("w_in", "w_q_up", "w_kv_up")
LATE_WEIGHTS = ("w_out", "w_up", "w_down")


def first_weights_to_internal(w_in, w_q_up, w_kv_up):
    cq, ckv, kr, z, xbc, dt = jnp.split(w_in, np.cumsum(IN_SPLITS)[:-1].tolist(), axis=1)
    K = w_in.shape[0]

    def zeros(n):
        return jnp.zeros((K, n), w_in.dtype)

    w_in_p = jnp.concatenate([cq, zeros(KR_LANE), kr, zeros(LANE - KR_LANE - ROPE), ckv, zeros(OFF_Z - OFF_CKV - KV_RANK),
                              z, xbc, dt, zeros(WIN_P - OFF_DT - 2 * SSD_HEADS)], axis=1)
    w_q_p = jnp.pad(w_q_up.reshape(Q_RANK, N_HEADS, NOPE + ROPE), ((0, 0), (0, 0), (0, HEAD_BLOCK - NOPE - ROPE))).reshape(Q_RANK, QP)
    return dict(w_in_p=w_in_p, w_q_p=w_q_p, w_kv=w_kv_up)


def late_weights_to_internal(w_out, w_up, w_down):
    attn_rows = w_out[:N_HEADS * V_DIM].reshape(N_HEADS, V_DIM, -1)
    w_out_p = jnp.concatenate([jnp.pad(attn_rows, ((0, 0), (HEAD_BLOCK - V_DIM, 0), (0, 0))).reshape(QP, -1),
                               w_out[N_HEADS * V_DIM:]], axis=0)
    return dict(w_out_p=w_out_p, w_up=glu_interleave(w_up), w_down=w_down)


def _in_grad(g_in_p):
    return jnp.concatenate([g_in_p[:, OFF_CQ:OFF_CQ + Q_RANK], g_in_p[:, OFF_CKV:OFF_CKV + KV_RANK],
                            g_in_p[:, OFF_KR + KR_LANE:OFF_KR + KR_LANE + ROPE], g_in_p[:, OFF_Z:OFF_Z + D_INNER],
                            g_in_p[:, OFF_XBC:OFF_XBC + XBC], g_in_p[:, OFF_DT:OFF_DT + 2 * SSD_HEADS]], axis=1)


def _q_grad(g_q_p):
    return g_q_p.reshape(Q_RANK, N_HEADS, HEAD_BLOCK)[:, :, :NOPE + ROPE].reshape(Q_RANK, -1)


def _out_grad(g_out_p):
    return jnp.concatenate([g_out_p[:QP].reshape(N_HEADS, HEAD_BLOCK, -1)[:, HEAD_BLOCK - V_DIM:].reshape(N_HEADS * V_DIM, -1),
                            g_out_p[QP:]], axis=0)


EARLY = ("w_out", "w_up", "w_down", "w_q_up", "w_kv_up")


def local_step(x, ctx, target, mod_x, mod_c, W, late_shards, V):
    nb, S, D = x.shape
    C = ctx.shape[1]
    T = C + S
    tr = _tile(math.gcd(C, S), 256, 8)
    tq = _tile(S, 256, 8)
    tc = 256
    cblk = C // tr
    m = [mod_x[:, i * D:(i + 1) * D][:, None, :] for i in range(N_MOD)]
    mc = [mod_c[:, i * D:(i + 1) * D] for i in range(2)]
    ssd_w8, ffn_w8 = _taps8(V["ssd_conv_w"]), _taps8(V["ffn_conv_w"])
    dexp = jnp.repeat(V["ssd_d"].reshape(-1), SSD_P).reshape(1, D_INNER)
    cosT, sinT = rope_tables(C, S)
    cosS, sinS = cosT[C:], sinT[C:]

    (h1x,) = rows_fwd("prenorm_x", fn_prenorm, nb, S // tr, tr, [(x, D, 0, 0)], [m[0], m[1]], [V["mix_pre_norm"]], [(D, BF16)])
    (h1c,) = rows_fwd("prenorm_c", fn_prenorm, nb, C // tr, tr, [(ctx, D, 0, 0)], [], [mc[0], mc[1], V["mix_pre_norm"]], [(D, BF16)])
    h1 = jnp.concatenate([h1c, h1x], axis=1).reshape(nb * T, D)
    u = matmul("in_proj", [(h1, W["w_in_p"])], "nn", F32).reshape(nb, T, WIN_P)
    (qn,) = rows_fwd("q_norm", fn_rms, nb, S // tr, tr, [(u, Q_RANK, OFF_CQ // Q_RANK, cblk)], [], [V["q_norm"]], [(Q_RANK, BF16)])
    (kvn,) = rows_fwd("kv_norm", fn_rms, nb, T // tr, tr, [(u, KV_RANK, OFF_CKV // KV_RANK, 0)], [], [V["kv_norm"]], [(KV_RANK, BF16)])
    qn2, kvn2 = qn.reshape(nb * S, Q_RANK), kvn.reshape(nb * T, KV_RANK)
    q_raw = matmul("q_up", [(qn2, W["w_q_p"])], "nn", F32).reshape(nb, S, QP)
    kv = matmul("kv_up", [(kvn2, W["w_kv"])], "nn", BF16).reshape(nb, T, QP)
    q = rope_call("rope_q", q_raw, QP, 0, cosS * Q_PRESCALE, sinS * Q_PRESCALE, BF16, tr)
    kr = rope_call("rope_k", u, LANE, OFF_KR // LANE, cosT, sinT, BF16, tr)
    o = attn_fwd(q, kv, kr, tq)
    xbc = ssd_conv_fwd(u, ssd_w8, V["ssd_conv_b"], C, tc)
    dt2, alog2, dtb2 = ssd_dt_inputs(u, V["ssd_a_log"], V["ssd_dt_bias"])
    y2, hin, *late = ssd_fwd(xbc, dt2, alog2, dtb2, C, hosted_all_gather(late_shards))
    W = dict(W, **late_weights_to_internal(*[_whole(s, n) for s, n in zip(late, LATE_WEIGHTS)]))
    y2 = y2.reshape(2 * nb, S, D_INNER)
    fin_rows = [(y2, D_INNER, 0, 0, 0), (y2, D_INNER, 0, 0, nb), (xbc, D_INNER, 0, cblk), (u, D_INNER, OFF_Z // D_INNER, cblk)]
    fin_gl = [dexp, V["ssd_norm"]]
    (ssd,) = rows_fwd("ssd_finish", fn_ssd_finish, nb, S // tr, tr, fin_rows, [], fin_gl, [(D_INNER, BF16)])
    o2, ssd2 = o.reshape(nb * S, QP), ssd.reshape(nb * S, D_INNER)
    mix = matmul("out_proj", [(o2, W["w_out_p"][:QP]), (ssd2, W["w_out_p"][QP:])], "nn", F32).reshape(nb, S, D)
    pm_rows = [(x, D, 0, 0), (mix, D, 0, 0)]
    pm_pb = [m[2], m[4], m[3]]
    pm_gl = [V["mix_post_norm"], V["ffn_pre_norm"]]
    x1, h2 = rows_fwd("postmix", fn_postmix, nb, S // tr, tr, pm_rows, pm_pb, pm_gl, [(D, F32), (D, BF16)])
    h22 = h2.reshape(nb * S, D)
    up = matmul("up_proj", [(h22, W["w_up"])], "nn", F32).reshape(nb, S, 2 * D_FF)
    act = glu_fwd(up, ffn_w8, V["ffn_conv_b"])
    act2 = act.reshape(nb * S, D_FF)
    ffn = matmul("down_proj", [(act2, W["w_down"])], "nn", F32).reshape(nb, S, D)
    dx1, dffn, dgate2, d_ffn_post, loss = final_call(x1, ffn, target, m[5], V["ffn_post_norm"], tr)

    dffn2 = dffn.reshape(nb * S, D)
    dact = matmul("down_dgrad", [(dffn2, W["w_down"])], "nt", BF16).reshape(nb, S, D_FF)
    g_down = matmul_tn("down_wgrad", act2, dffn2)
    dup, ffn_rows = glu_bwd(up, ffn_w8, V["ffn_conv_b"], dact)
    dup2 = dup.reshape(nb * S, 2 * D_FF)
    dh2 = matmul("up_dgrad", [(dup2, W["w_up"])], "nt", BF16).reshape(nb, S, D)
    g_up = matmul_tn("up_wgrad", h22, dup2)
    dx_a, dmix, dgate1, dscale2, dshift2, d_mix_post, d_ffn_pre = rows_bwd(
        "postmix_bwd", fn_postmix, nb, S // tr, tr, pm_rows, pm_pb, pm_gl,
        [(dx1, D, 0, 0), (dh2, D, 0, 0)], [(0, F32), (1, BF16)])
    dmix2 = dmix.reshape(nb * S, D)
    dcat = matmul("out_dgrad", [(dmix2, W["w_out_p"])], "nt", BF16).reshape(nb, S, QP + D_INNER)
    g_out_p = jnp.concatenate([matmul_tn("out_wgrad_attn", o2, dmix2), matmul_tn("out_wgrad_ssd", ssd2, dmix2)], axis=0)
    dy, dxs_direct, dz, d_dexp, d_ssd_norm = rows_bwd(
        "ssd_finish_bwd", fn_ssd_finish, nb, S // tr, tr, fin_rows, [], fin_gl,
        [(dcat, D_INNER, QP // D_INNER, 0)], [(0, F32), (2, F32), (3, BF16)])
    dq, dkv, dkr = attn_bwd(q, kv, kr, dcat, tq)
    dq_pre = rope_call("rope_dq", dq, QP, 0, cosS, -sinS, BF16, tr).reshape(nb * S, QP)
    dkr_pre = rope_call("rope_dk", dkr, LANE, 0, cosT, -sinT, BF16, tr)
    dkv2 = dkv.reshape(nb * T, QP)
    dqn = matmul("q_dgrad", [(dq_pre, W["w_q_p"])], "nt", F32).reshape(nb, S, Q_RANK)
    g_q_p = matmul_tn("q_wgrad", qn2, dq_pre)
    dkvn = matmul("kv_dgrad", [(dkv2, W["w_kv"])], "nt", F32).reshape(nb, T, KV_RANK)
    g_kv = matmul_tn("kv_wgrad", kvn2, dkv2)
    early_grads = (_out_grad(g_out_p), glu_deinterleave(g_up), g_down, _q_grad(g_q_p), g_kv)
    early = hosted_all_to_all([_per_device(g, n) for g, n in zip(early_grads, EARLY)])
    dxbc2, ddt2, ssd_stats, *received = ssd_bwd(xbc, dt2, alog2, dtb2, hin, dy, C, early)
    ddt_block = jnp.concatenate([ddt2[0][..., :SSD_HEADS], ddt2[1][..., :SSD_HEADS],
                                 jnp.zeros((nb, T, WIN_P - OFF_DT - 2 * SSD_HEADS), F32)], axis=-1).astype(BF16)
    dxbc_raw, ssd_rows = ssd_conv_bwd(u, ssd_w8, V["ssd_conv_b"], dxbc2, dxs_direct, C, tc)
    dcq, d_q_norm = rows_bwd("q_norm_bwd", fn_rms, nb, S // tr, tr, [(u, Q_RANK, OFF_CQ // Q_RANK, cblk)], [], [V["q_norm"]],
                             [(dqn, Q_RANK, 0, 0)], [(0, BF16)])
    dckv, d_kv_norm = rows_bwd("kv_norm_bwd", fn_rms, nb, T // tr, tr, [(u, KV_RANK, OFF_CKV // KV_RANK, 0)], [], [V["kv_norm"]],
                               [(dkvn, KV_RANK, 0, 0)], [(0, BF16)])

    def ctx_rows(t):
        return jnp.pad(t, ((0, 0), (C, 0), (0, 0)))

    du = jnp.concatenate([ctx_rows(dcq), dkr_pre, dckv, jnp.zeros((nb, T, OFF_Z - OFF_CKV - KV_RANK), BF16), ctx_rows(dz),
                          dxbc_raw, ddt_block], axis=-1).reshape(nb * T, WIN_P)
    g_in_p = matmul_tn("in_wgrad", h1, du)
    dh1, received_in = matmul("in_dgrad", [(du, W["w_in_p"])], "nt", BF16,
                              hosted=hosted_all_to_all([_per_device(_in_grad(g_in_p), "w_in").astype(BF16)]))
    dh1 = dh1.reshape(nb, T, D)

    def fn_prenorm_res(xv, shift, scale, g):
        return fn_prenorm(xv, shift, scale, g) + (xv,)

    grad_x, dshift1, dscale1, d_mix_pre_x = rows_bwd(
        "prenorm_x_bwd", fn_prenorm_res, nb, S // tr, tr, [(x, D, 0, 0)], [m[0], m[1]], [V["mix_pre_norm"]],
        [(dh1, D, 0, cblk), (dx_a, D, 0, 0)], [(0, F32)])
    dshift_c, dscale_c, d_mix_pre_c = rows_bwd(
        "prenorm_c_bwd", fn_prenorm, nb, C // tr, tr, [(ctx, D, 0, 0)], [], [mc[0], mc[1], V["mix_pre_norm"]],
        [(dh1, D, 0, 0)], [])

    dmod_x = jnp.concatenate([dshift1, dscale1, dgate1, dshift2, dscale2, dgate2], axis=-1).reshape(nb, N_MOD * D)
    dmod_c = jnp.concatenate([dshift_c, dscale_c, jnp.zeros((1, (N_MOD - 2) * D), F32)], axis=-1)
    gv = dict(
        mix_pre_norm=d_mix_pre_x + d_mix_pre_c, mix_post_norm=d_mix_post, q_norm=d_q_norm, kv_norm=d_kv_norm,
        ssd_conv_w=ssd_rows[:SSD_K], ssd_conv_b=ssd_rows[SSD_K:SSD_K + 1],
        ssd_a_log=jnp.concatenate([ssd_stats[0:1, :SSD_HEADS], ssd_stats[2:3, :SSD_HEADS]], axis=1),
        ssd_dt_bias=jnp.concatenate([ssd_stats[1:2, :SSD_HEADS], ssd_stats[3:4, :SSD_HEADS]], axis=1),
        ssd_d=jnp.sum(d_dexp.reshape(SSD_HEADS, SSD_P), axis=1).reshape(1, SSD_HEADS), ssd_norm=d_ssd_norm,
        ffn_pre_norm=d_ffn_pre, ffn_post_norm=d_ffn_post,
        ffn_conv_w=ffn_rows[:FFN_K], ffn_conv_b=ffn_rows[FFN_K:FFN_K + 1])
    return loss, grad_x, dmod_x, dmod_c, gv, dict(zip(EARLY, received), w_in=received_in)


WEIGHT_ORDER = ("c_ctx", "w_mod", "b_mod", "mix_pre_norm", "mix_post_norm", "w_in", "q_norm", "w_q_up", "kv_norm",
                "w_kv_up", "ssd_conv_w", "ssd_conv_b", "ssd_a_log", "ssd_dt_bias", "ssd_d", "ssd_norm", "w_out",
                "ffn_pre_norm", "ffn_post_norm", "w_up", "ffn_conv_w", "ffn_conv_b", "w_down")
MATRICES = ("w_in", "w_q_up", "w_kv_up", "w_out", "w_up", "w_down")
ROW_SHARDED = ("w_out", "w_down")
SMALL_SUMMED = ("c_ctx", "mix_pre_norm", "mix_post_norm", "q_norm", "kv_norm", "ssd_conv_w", "ssd_conv_b", "ssd_a_log",
                "ssd_dt_bias", "ssd_d", "ssd_norm", "ffn_pre_norm", "ffn_post_norm", "ffn_conv_w", "ffn_conv_b")
MOD_ROWS = 8


def _whole(shards, name):
    if name in ROW_SHARDED:
        return shards.reshape(-1, shards.shape[-1])
    return jnp.concatenate([shards[j] for j in range(N_DEV)], axis=1)


def _per_device(g, name):
    if name in ROW_SHARDED:
        return g.reshape(N_DEV, -1, g.shape[-1])
    return jnp.stack(jnp.split(g, N_DEV, axis=1))


def kernel(x, c, ctx, c_ctx, w_mod, b_mod, mix_pre_norm, mix_post_norm, w_in, q_norm, w_q_up, kv_norm, w_kv_up, ssd_conv_w, ssd_conv_b, ssd_a_log, ssd_dt_bias, ssd_d, ssd_norm, w_out, ffn_pre_norm, ffn_post_norm, w_up, ffn_conv_w, ffn_conv_b, w_down, loss_target, m_c_ctx, m_w_mod, m_b_mod, m_mix_pre_norm, m_mix_post_norm, m_w_in, m_q_norm, m_w_q_up, m_kv_norm, m_w_kv_up, m_ssd_conv_w, m_ssd_conv_b, m_ssd_a_log, m_ssd_dt_bias, m_ssd_d, m_ssd_norm, m_w_out, m_ffn_pre_norm, m_ffn_post_norm, m_w_up, m_ffn_conv_w, m_ffn_conv_b, m_w_down, v_c_ctx, v_w_mod, v_b_mod, v_mix_pre_norm, v_mix_post_norm, v_w_in, v_q_norm, v_w_q_up, v_kv_norm, v_w_kv_up, v_ssd_conv_w, v_ssd_conv_b, v_ssd_a_log, v_ssd_dt_bias, v_ssd_d, v_ssd_norm, v_w_out, v_ffn_pre_norm, v_ffn_post_norm, v_w_up, v_ffn_conv_w, v_ffn_conv_b, v_w_down):
    weights = dict(c_ctx=c_ctx, w_mod=w_mod, b_mod=b_mod, mix_pre_norm=mix_pre_norm, mix_post_norm=mix_post_norm, w_in=w_in, q_norm=q_norm, w_q_up=w_q_up, kv_norm=kv_norm, w_kv_up=w_kv_up, ssd_conv_w=ssd_conv_w, ssd_conv_b=ssd_conv_b, ssd_a_log=ssd_a_log, ssd_dt_bias=ssd_dt_bias, ssd_d=ssd_d, ssd_norm=ssd_norm, w_out=w_out, ffn_pre_norm=ffn_pre_norm, ffn_post_norm=ffn_post_norm, w_up=w_up, ffn_conv_w=ffn_conv_w, ffn_conv_b=ffn_conv_b, w_down=w_down)
    mom1 = dict(c_ctx=m_c_ctx, w_mod=m_w_mod, b_mod=m_b_mod, mix_pre_norm=m_mix_pre_norm, mix_post_norm=m_mix_post_norm, w_in=m_w_in, q_norm=m_q_norm, w_q_up=m_w_q_up, kv_norm=m_kv_norm, w_kv_up=m_w_kv_up, ssd_conv_w=m_ssd_conv_w, ssd_conv_b=m_ssd_conv_b, ssd_a_log=m_ssd_a_log, ssd_dt_bias=m_ssd_dt_bias, ssd_d=m_ssd_d, ssd_norm=m_ssd_norm, w_out=m_w_out, ffn_pre_norm=m_ffn_pre_norm, ffn_post_norm=m_ffn_post_norm, w_up=m_w_up, ffn_conv_w=m_ffn_conv_w, ffn_conv_b=m_ffn_conv_b, w_down=m_w_down)
    mom2 = dict(c_ctx=v_c_ctx, w_mod=v_w_mod, b_mod=v_b_mod, mix_pre_norm=v_mix_pre_norm, mix_post_norm=v_mix_post_norm, w_in=v_w_in, q_norm=v_q_norm, w_q_up=v_w_q_up, kv_norm=v_kv_norm, w_kv_up=v_w_kv_up, ssd_conv_w=v_ssd_conv_w, ssd_conv_b=v_ssd_conv_b, ssd_a_log=v_ssd_a_log, ssd_dt_bias=v_ssd_dt_bias, ssd_d=v_ssd_d, ssd_norm=v_ssd_norm, w_out=v_w_out, ffn_pre_norm=v_ffn_pre_norm, ffn_post_norm=v_ffn_post_norm, w_up=v_w_up, ffn_conv_w=v_ffn_conv_w, ffn_conv_b=v_ffn_conv_b, w_down=v_w_down)
    nb, S, D = x.shape
    me = 4 * lax.axis_index("x") + 2 * lax.axis_index("y") + lax.axis_index("c")

    *first, c_all, ssd_w_sh, ffn_w_sh = all_gather(
        "gather_first", [weights[n][0].astype(BF16) for n in FIRST] + [c, ssd_conv_w[0], ffn_conv_w[0]])
    W = first_weights_to_internal(*[_whole(s, n) for n, s in zip(FIRST, first)])
    late_shards = [weights[n][0].astype(BF16) for n in LATE_WEIGHTS]
    V = {n: weights[n].reshape(1, -1) for n in SMALL_SUMMED if n != "c_ctx"}
    V["ssd_conv_w"] = _whole(ssd_w_sh, "ssd_conv_w")
    V["ffn_conv_w"] = _whole(ffn_w_sh, "ffn_conv_w")

    n_all = N_DEV * nb
    mod_rows = -(-(n_all + 1) // 8) * 8
    c_pad = jnp.concatenate([c_all.reshape(n_all, D), c_ctx.reshape(1, D), jnp.zeros((mod_rows - n_all - 1, D), F32)], axis=0)
    mod_cols = w_mod.shape[2]
    b_mine = lax.dynamic_slice(b_mod, (0, me * mod_cols), (1, mod_cols))
    mod_part = matmul("mod_proj", [(c_pad, w_mod[0])], "nn", F32, bias=b_mine, silu_a=True)
    mod_all = _whole(all_gather("gather_mod", [mod_part])[0], "w_mod")
    mod_x = lax.dynamic_slice(mod_all, (me * nb, 0), (nb, mod_all.shape[1]))
    mod_c = mod_all[n_all:n_all + 1]

    loss, grad_x, dmod_x, dmod_c, gv, slots = local_step(x, ctx, loss_target, mod_x, mod_c, W, late_shards, V)

    dmod_mine = jnp.concatenate([dmod_x, dmod_c, jnp.zeros((MOD_ROWS - nb - 1, dmod_x.shape[1]), F32)], axis=0)
    dmod_all = all_gather("gather_dmod", [dmod_mine])[0]
    dmod_ctx = sum_slots("sum_dmod_ctx", dmod_all[:, nb:nb + 1].reshape(N_DEV, -1, LANE)).reshape(1, -1)
    dmod_full = jnp.concatenate([dmod_all[:, :nb].reshape(n_all, -1), dmod_ctx,
                                 jnp.zeros((mod_rows - n_all - 1, dmod_ctx.shape[1]), F32)], axis=0)
    (g_b_mod,) = ew_call("mod_bias_grad", lambda t: (jnp.sum(t, axis=0, keepdims=True),), [dmod_full], [((1, dmod_full.shape[1]), F32)])
    dmod_cols = lax.dynamic_slice(dmod_full, (0, me * mod_cols), (mod_rows, mod_cols))
    g_w_mod = matmul_tn("mod_wgrad", c_pad, dmod_cols, silu_a=True)
    dsilu_ctx = matmul("mod_dgrad_ctx", [(dmod_cols[n_all:n_all + 8], w_mod[0])], "nt", F32)[0:1]

    def silu_vjp(cc, ct):
        return (jax.vjp(_silu, cc)[1](ct)[0],)

    (g_c_ctx_part,) = ew_call("c_ctx_grad", silu_vjp, [c_ctx.reshape(1, D), dsilu_ctx], [((1, D), F32)])

    gv = dict(gv, c_ctx=g_c_ctx_part)
    small_parts = [loss] + [gv[n] for n in SMALL_SUMMED]
    small_sum = sum_slots("sum_small", all_gather("gather_small_grads", [_pack_rows(small_parts)])[0])
    summed = _unpack_rows(small_sum, [p.shape for p in small_parts])
    loss_out = summed[0][0, 0]
    grads = {n: g.reshape(weights[n].shape) if n not in ("ssd_conv_w", "ffn_conv_w") else g for n, g in zip(SMALL_SUMMED, summed[1:])}
    for n in ("ssd_conv_w", "ffn_conv_w"):
        cols = weights[n].shape[2]
        grads[n] = lax.dynamic_slice(grads[n], (0, me * cols), (grads[n].shape[0], cols)).reshape(weights[n].shape)
    grads["b_mod"] = g_b_mod.reshape(b_mod.shape)

    slots = dict(slots, w_mod=g_w_mod[None])
    delta, new_m, new_v = {}, {}, {}
    for n in MATRICES + ("w_mod",):
        g, d, mn, vn = adamw_matrix("adamw_" + n, weights[n][0], slots[n], mom1[n][0], mom2[n][0])
        grads[n], delta[n], new_m[n], new_v[n] = [t.reshape(weights[n].shape) for t in (g, d, mn, vn)]
    small = [n for n in WEIGHT_ORDER if n not in slots]

    def two_d(t):
        return t.reshape(-1, t.shape[-1])

    ds, ms, vs = adamw_small(*[[two_d(t[n]) for n in small] for t in (weights, grads, mom1, mom2)])
    for n, d, mn, vn in zip(small, ds, ms, vs):
        delta[n], new_m[n], new_v[n] = [t.reshape(weights[n].shape) for t in (d, mn, vn)]
    return (loss_out, grad_x, *[t[n] for t in (grads, delta, new_m, new_v) for n in WEIGHT_ORDER])
```

```python
import functools
import math

import jax
import jax.numpy as jnp
import numpy as np
from jax import lax
from jax.experimental import pallas as pl
from jax.experimental.pallas import tpu as pltpu

F32 = jnp.float32
BF16 = jnp.bfloat16
MESH = pl.DeviceIdType.MESH

D_MODEL = 1024
GRID_W = 64
N_HEADS = 16
NOPE = 64
ROPE = 32
V_DIM = 64
Q_RANK = 384
KV_RANK = 256
ROPE_THETA = 10000.0
ATTN_SCALE = (NOPE + ROPE) ** -0.5
SSD_HEADS = 16
SSD_P = 64
SSD_GROUPS = 2
SSD_N = 128
SSD_K = 5
CHUNK = 128
D_INNER = SSD_HEADS * SSD_P
GN = SSD_GROUPS * SSD_N
XBC = D_INNER + 2 * GN
D_FF = 2816
FFN_K = 3
N_MOD = 6
EPS = 1e-6
IN_SPLITS = (Q_RANK, KV_RANK, ROPE, D_INNER, XBC, 2 * SSD_HEADS)
IN_WIDTH = sum(IN_SPLITS)
N_DEV = 8

ADAM_LR = 0.001
ADAM_B1 = 0.9
ADAM_B2 = 0.999
ADAM_EPS = 1e-08
ADAM_WD = 0.01
ADAM_STEP = 10

LANE = 128
HEAD_BLOCK = 128
OFF_CQ = 0
OFF_KR = 384
OFF_CKV = 512
OFF_Z = 1024
OFF_XBC = 2048
OFF_DT = 3584
WIN_P = 3840
KR_LANE = 64
QP = N_HEADS * HEAD_BLOCK

VMEM_LIMIT_V7X = 56 * 1024 * 1024
NEG_BIG = -1e30


def _cparams(*sem):
    return pltpu.CompilerParams(dimension_semantics=sem, vmem_limit_bytes=VMEM_LIMIT_V7X)


def _tile(n, target, mult=128):
    if n <= target:
        return n
    t = (target // mult) * mult
    while t >= mult:
        if n % t == 0:
            return t
        t -= mult
    return n


def _silu(x):
    return x * jax.nn.sigmoid(x)


def _rms(x, g):
    return x * lax.rsqrt(jnp.mean(x * x, axis=-1, keepdims=True) + EPS) * g


WHOLE_K_WIDE = 2048


def matmul(name, pairs, mode, out_dtype, *, bias=None, silu_a=False, hosted=None):
    n_pairs = len(pairs)
    M = pairs[0][0].shape[0]
    N = pairs[0][1].shape[1] if mode == "nn" else pairs[0][1].shape[0]
    k_total = sum(a.shape[1] for a, _ in pairs)
    tm = _tile(M, 1024 if k_total <= WHOLE_K_WIDE else 512, 8)
    tn = _tile(N, 1408 if k_total <= WHOLE_K_WIDE else 512)
    dims = (((1,), (0,)), ((), ())) if mode == "nn" else (((1,), (1,)), ((), ()))
    n_own = 2 * n_pairs + (bias is not None)
    n_ex = hosted.n if hosted else 0

    def body(*refs):
        o_ref = refs[n_own + n_ex]
        if hosted:
            j, i = pl.program_id(0), pl.program_id(1)
            begin_exchange, end_exchange = hosted.steps(
                refs[n_own:n_own + n_ex], refs[n_own + n_ex + 1:n_own + 2 * n_ex + 1], refs[n_own + 2 * n_ex + 1:],
                jnp.logical_and(j == 0, i == 0), jnp.logical_and(j == N // tn - 1, i == M // tm - 1))
            begin_exchange()
        acc = None
        for p in range(n_pairs):
            a = refs[2 * p][...]
            if silu_a:
                a = _silu(a.astype(F32))
            d = lax.dot_general(a.astype(BF16), refs[2 * p + 1][...].astype(BF16), dims, preferred_element_type=F32)
            acc = d if acc is None else acc + d
        if bias is not None:
            acc = acc + refs[2 * n_pairs][...]
        o_ref[...] = acc.astype(o_ref.dtype)
        if hosted:
            end_exchange()

    in_specs, args = [], []
    for a, b in pairs:
        K = a.shape[1]
        in_specs.append(pl.BlockSpec((tm, K), lambda j, i: (i, 0)))
        in_specs.append(pl.BlockSpec((K, tn), lambda j, i: (0, j)) if mode == "nn" else pl.BlockSpec((tn, K), lambda j, i: (j, 0)))
        args += [a, b]
    if bias is not None:
        in_specs.append(pl.BlockSpec((1, tn), lambda j, i: (0, j)))
        args.append(bias)
    out_spec = pl.BlockSpec((tm, tn), lambda j, i: (i, j))
    out_shape = jax.ShapeDtypeStruct((M, N), out_dtype)
    if not hosted:
        return pl.pallas_call(
            body, name=name, grid=(N // tn, M // tm), in_specs=in_specs, out_specs=out_spec, out_shape=out_shape,
            compiler_params=_cparams("arbitrary", "arbitrary"),
        )(*args)
    return pl.pallas_call(
        body, name=name, grid=(N // tn, M // tm), in_specs=in_specs + hosted.specs,
        out_specs=[out_spec] + hosted.specs, out_shape=[out_shape] + hosted.out_shape, scratch_shapes=hosted.scratch,
        compiler_params=_cparams("arbitrary", "arbitrary"),
    )(*args, *hosted.arrays)


def matmul_tn(name, a, b, out_dtype=F32, *, silu_a=False, tm=1408, tn=512, tk=2048):
    R, M = a.shape
    N = b.shape[1]
    tm = _tile(M, tm)
    tn = _tile(N, tn)
    tk = _tile(R, tk, 8)
    nk = R // tk

    def body(a_ref, b_ref, o_ref, acc):
        k = pl.program_id(2)

        @pl.when(k == 0)
        def _():
            acc[...] = jnp.zeros_like(acc)

        x = a_ref[...]
        if silu_a:
            x = _silu(x.astype(F32))
        acc[...] += lax.dot_general(x.astype(BF16), b_ref[...].astype(BF16), (((0,), (0,)), ((), ())),
                                    preferred_element_type=F32)

        @pl.when(k == nk - 1)
        def _():
            o_ref[...] = acc[...].astype(o_ref.dtype)

    return pl.pallas_call(
        body, name=name, grid=(M // tm, N // tn, nk),
        in_specs=[pl.BlockSpec((tk, tm), lambda i, j, k: (k, i)), pl.BlockSpec((tk, tn), lambda i, j, k: (k, j))],
        out_specs=pl.BlockSpec((tm, tn), lambda i, j, k: (i, j)),
        out_shape=jax.ShapeDtypeStruct((M, N), out_dtype),
        scratch_shapes=[pltpu.VMEM((tm, tn), F32)],
        compiler_params=_cparams("arbitrary", "arbitrary", "arbitrary"),
    )(a, b)


def _row_specs(rin, pbin, glin, tr):
    specs = [pl.BlockSpec((1, tr, w), lambda b, i, cb=cb, ro=ro, bo=(e[4] if len(e) > 4 else 0): (b + bo, i + ro, cb))
             for e in rin for (_, w, cb, ro) in [e[:4]]]
    specs += [pl.BlockSpec((1, 1, a.shape[-1]), lambda b, i: (b, 0, 0)) for a in pbin]
    specs += [pl.BlockSpec((1, a.shape[-1]), lambda b, i: (0, 0)) for a in glin]
    return specs


def rows_fwd(name, fn, nb, nblk, tr, rin, pbin, glin, outs):
    nr, npb, ngl = len(rin), len(pbin), len(glin)
    n_in = nr + npb + ngl

    def body(*refs):
        args = [r[0].astype(F32) for r in refs[:nr + npb]] + [r[...] for r in refs[nr + npb:n_in]]
        res = fn(*args)
        for o, v in zip(refs[n_in:], res):
            o[0] = v.astype(o.dtype)

    return pl.pallas_call(
        body, name=name, grid=(nb, nblk), in_specs=_row_specs(rin, pbin, glin, tr),
        out_specs=[pl.BlockSpec((1, tr, w), lambda b, i: (b, i, 0)) for (w, _) in outs],
        out_shape=[jax.ShapeDtypeStruct((nb, nblk * tr, w), dt) for (w, dt) in outs],
        compiler_params=_cparams("arbitrary", "arbitrary"),
    )(*[e[0] for e in rin], *pbin, *glin)


def rows_bwd(name, fn, nb, nblk, tr, rin, pbin, glin, cts, want):
    nr, npb, ngl, nct = len(rin), len(pbin), len(glin), len(cts)
    n_in = nr + npb + ngl

    def body(*refs):
        b, i = pl.program_id(0), pl.program_id(1)
        args = [r[0].astype(F32) for r in refs[:nr + npb]] + [r[...] for r in refs[nr + npb:n_in]]
        ct = tuple(r[0].astype(F32) for r in refs[n_in:n_in + nct])
        _, vjp = jax.vjp(fn, *args)
        g = vjp(ct)
        orefs = refs[n_in + nct:]
        for o, (idx, _) in zip(orefs, want):
            o[0] = g[idx].astype(o.dtype)
        pb_refs = orefs[len(want):len(want) + npb]
        gl_refs = orefs[len(want) + npb:]

        @pl.when(i == 0)
        def _():
            for o, v in zip(pb_refs, g[nr:nr + npb]):
                o[0] = v

        @pl.when(i > 0)
        def _():
            for o, v in zip(pb_refs, g[nr:nr + npb]):
                o[0] += v

        first = jnp.logical_and(b == 0, i == 0)

        @pl.when(first)
        def _():
            for o, v in zip(gl_refs, g[nr + npb:]):
                o[...] = v

        @pl.when(jnp.logical_not(first))
        def _():
            for o, v in zip(gl_refs, g[nr + npb:]):
                o[...] += v

    out_specs = [pl.BlockSpec((1, tr, rin[idx][1]), lambda b, i: (b, i, 0)) for (idx, _) in want]
    out_shape = [jax.ShapeDtypeStruct((nb, nblk * tr, rin[idx][1]), dt) for (idx, dt) in want]
    out_specs += [pl.BlockSpec((1, 1, a.shape[-1]), lambda b, i: (b, 0, 0)) for a in pbin]
    out_shape += [jax.ShapeDtypeStruct((nb, 1, a.shape[-1]), F32) for a in pbin]
    out_specs += [pl.BlockSpec((1, a.shape[-1]), lambda b, i: (0, 0)) for a in glin]
    out_shape += [jax.ShapeDtypeStruct((1, a.shape[-1]), F32) for a in glin]
    return pl.pallas_call(
        body, name=name, grid=(nb, nblk),
        in_specs=_row_specs(rin, pbin, glin, tr) + _row_specs(cts, [], [], tr),
        out_specs=out_specs, out_shape=out_shape,
        compiler_params=_cparams("arbitrary", "arbitrary"),
    )(*[e[0] for e in rin], *pbin, *glin, *[e[0] for e in cts])


def ew_call(name, fn, ins, outs):
    def body(*refs):
        res = fn(*[r[...] for r in refs[:len(ins)]])
        for o, v in zip(refs[len(ins):], res):
            o[...] = v.astype(o.dtype)

    return pl.pallas_call(body, name=name, out_shape=[jax.ShapeDtypeStruct(s, dt) for (s, dt) in outs])(*ins)


def fn_prenorm(x, shift, scale, g):
    return (_rms(x, g) * (1.0 + scale) + shift,)


def fn_rms(x, g):
    return (_rms(x, g),)


def fn_ssd_finish(yf, yr, xs, z, dexp, nw):
    y = yf + yr + dexp * xs
    return (_rms(y * _silu(z), nw),)


def fn_postmix(x, mix, gate1, scale2, shift2, post_g, pre_g):
    x1 = x + gate1 * _rms(mix, post_g)
    h2 = _rms(x1, pre_g) * (1.0 + scale2) + shift2
    return x1, h2


def final_call(x1, ffn, target, gate2, post_g, tr):
    nb, S, D = x1.shape
    nblk = S // tr

    def body(x1_ref, f_ref, t_ref, g2_ref, pg_ref, dx1_ref, df_ref, dg2_ref, dpg_ref, loss_ref):
        b, i = pl.program_id(0), pl.program_id(1)
        tgt = t_ref[0]

        def lossfn(x1v, fv, g2, pg):
            e = x1v + g2 * _rms(fv, pg) - tgt
            return 0.5 * jnp.sum(jnp.mean(e * e, axis=-1, keepdims=True))

        val, (dx1, df, dg2, dpg) = jax.value_and_grad(lossfn, argnums=(0, 1, 2, 3))(
            x1_ref[0], f_ref[0].astype(F32), g2_ref[0], pg_ref[...])
        dx1_ref[0] = dx1
        df_ref[0] = df.astype(df_ref.dtype)
        lv = jnp.full((1, LANE), val, F32)

        @pl.when(i == 0)
        def _():
            dg2_ref[0] = dg2

        @pl.when(i > 0)
        def _():
            dg2_ref[0] += dg2

        first = jnp.logical_and(b == 0, i == 0)

        @pl.when(first)
        def _():
            dpg_ref[...] = dpg
            loss_ref[...] = lv

        @pl.when(jnp.logical_not(first))
        def _():
            dpg_ref[...] += dpg
            loss_ref[...] += lv

    row = pl.BlockSpec((1, tr, D), lambda b, i: (b, i, 0))
    pb = pl.BlockSpec((1, 1, D), lambda b, i: (b, 0, 0))
    gl = pl.BlockSpec((1, D), lambda b, i: (0, 0))
    return pl.pallas_call(
        body, name="loss_head", grid=(nb, nblk), in_specs=[row, row, row, pb, gl],
        out_specs=[row, row, pb, gl, pl.BlockSpec((1, LANE), lambda b, i: (0, 0))],
        out_shape=[jax.ShapeDtypeStruct((nb, S, D), F32), jax.ShapeDtypeStruct((nb, S, D), BF16),
                   jax.ShapeDtypeStruct((nb, 1, D), F32), jax.ShapeDtypeStruct((1, D), F32),
                   jax.ShapeDtypeStruct((1, LANE), F32)],
        compiler_params=_cparams("arbitrary", "arbitrary"),
    )(x1, ffn, target, gate2, post_g)


def _rotate_half(t):
    lane = lax.broadcasted_iota(jnp.int32, t.shape, 1)
    return jnp.where((lane & 15) < 8, -pltpu.roll(t, LANE - 8, 1), pltpu.roll(t, 8, 1))


def rope_call(name, x, width, colblk, cos, sin, out_dtype, tr):
    nb = x.shape[0]
    R = cos.shape[0]
    nblk = R // tr

    def body(x_ref, c_ref, s_ref, o_ref):
        c, s = c_ref[...], s_ref[...]
        for h in range(width // LANE):
            t = x_ref[0, :, h * LANE:(h + 1) * LANE].astype(F32)
            o_ref[0, :, h * LANE:(h + 1) * LANE] = (t * c + _rotate_half(t) * s).astype(o_ref.dtype)

    tab = pl.BlockSpec((tr, LANE), lambda b, i: (i, 0))
    return pl.pallas_call(
        body, name=name, grid=(nb, nblk),
        in_specs=[pl.BlockSpec((1, tr, width), lambda b, i: (b, i, colblk)), tab, tab],
        out_specs=pl.BlockSpec((1, tr, width), lambda b, i: (b, i, 0)),
        out_shape=jax.ShapeDtypeStruct((nb, R, width), out_dtype),
        compiler_params=_cparams("arbitrary", "arbitrary"),
    )(x, cos, sin)


def rope_tables(n_ctx, seq):
    n_rows = seq // GRID_W
    row = np.repeat(np.arange(n_rows), GRID_W).astype(np.float32)
    col = np.tile(np.arange(GRID_W), n_rows).astype(np.float32)
    axis_dim = ROPE // 2
    inv_freq = jnp.asarray(ROPE_THETA, F32) ** (-jnp.arange(0, axis_dim, 2, dtype=F32) / axis_dim)
    ang_r = jnp.asarray(row)[:, None] * inv_freq
    ang_c = jnp.asarray(col)[:, None] * inv_freq
    ang = jnp.concatenate([ang_r, ang_r, ang_c, ang_c], axis=-1)
    cos = jnp.ones((n_ctx + seq, LANE), F32).at[n_ctx:, KR_LANE:KR_LANE + ROPE].set(jnp.cos(ang))
    sin = jnp.zeros((n_ctx + seq, LANE), F32).at[n_ctx:, KR_LANE:KR_LANE + ROPE].set(jnp.sin(ang))
    return cos, sin


Q_PRESCALE = ATTN_SCALE * math.log2(math.e)


def _attn_weights(q, kc):
    s2 = lax.dot_general(q, kc, (((1,), (1,)), ((), ())), preferred_element_type=F32)
    e = jnp.exp2(s2 - jnp.max(s2, axis=1, keepdims=True))
    return e, 1.0 / jnp.sum(e, axis=1, keepdims=True)


def _key_block(kv, kr):
    lane = lax.broadcasted_iota(jnp.int32, kv.shape, 1)
    return jnp.where(lane < NOPE, kv, kr)


def _rotated_query(q_ref, cos_ref, sin_ref):
    t = q_ref[0].astype(F32)
    return (t * cos_ref[...] + _rotate_half(t) * sin_ref[...]).astype(BF16)


def attn_fwd(q_raw, kv, kr, cos_q, sin_q, tq):
    nb, S, _ = q_raw.shape
    T = kv.shape[1]

    def body(q_ref, kv_ref, kr_ref, c_ref, s_ref, o_ref):
        kvv = kv_ref[0]
        e, r = _attn_weights(_rotated_query(q_ref, c_ref, s_ref), _key_block(kvv, kr_ref[0]))
        o = lax.dot_general(e.astype(BF16), kvv, (((1,), (0,)), ((), ())), preferred_element_type=F32) * r
        lane = lax.broadcasted_iota(jnp.int32, o.shape, 1)
        o_ref[0] = jnp.where(lane >= NOPE, o, 0.0).astype(o_ref.dtype)

    return pl.pallas_call(
        body, name="attn_fwd", grid=(nb, N_HEADS, S // tq),
        in_specs=[pl.BlockSpec((1, tq, HEAD_BLOCK), lambda b, h, i: (b, i, h)),
                  pl.BlockSpec((1, T, HEAD_BLOCK), lambda b, h, i: (b, 0, h)),
                  pl.BlockSpec((1, T, HEAD_BLOCK), lambda b, h, i: (b, 0, 0)),
                  pl.BlockSpec((tq, LANE), lambda b, h, i: (i, 0)), pl.BlockSpec((tq, LANE), lambda b, h, i: (i, 0))],
        out_specs=pl.BlockSpec((1, tq, HEAD_BLOCK), lambda b, h, i: (b, i, h)),
        out_shape=jax.ShapeDtypeStruct((nb, S, QP), BF16),
        compiler_params=_cparams("arbitrary", "arbitrary", "arbitrary"),
    )(q_raw, kv, kr, cos_q, sin_q)


def attn_bwd(q_raw, kv, kr, do, cos_q, sin_q, cos, sin, tq):
    nb, S, _ = q_raw.shape
    T = kv.shape[1]

    def body(q_ref, kv_ref, kr_ref, do_ref, cq_ref, sq_ref, c_ref, s_ref, dq_ref, dkv_ref, dkr_ref):
        h, i = pl.program_id(1), pl.program_id(2)

        @pl.when(i == 0)
        def _():
            dkv_ref[...] = jnp.zeros_like(dkv_ref)

        @pl.when(jnp.logical_and(h == 0, i == 0))
        def _():
            dkr_ref[...] = jnp.zeros_like(dkr_ref)

        qv, kvv, dov = _rotated_query(q_ref, cq_ref, sq_ref), kv_ref[0], do_ref[0]
        kc = _key_block(kvv, kr_ref[0])
        e, r = _attn_weights(qv, kc)
        dor = (dov.astype(F32) * r).astype(BF16)
        dpr = lax.dot_general(dor, kvv, (((1,), (1,)), ((), ())), preferred_element_type=F32)
        ds = (e * (dpr - r * jnp.sum(dpr * e, axis=1, keepdims=True))).astype(BF16)
        dq = lax.dot_general(ds, kc, (((1,), (0,)), ((), ())), preferred_element_type=F32) * ATTN_SCALE
        dq_ref[0] = (dq * c_ref[...] - _rotate_half(dq) * s_ref[...]).astype(dq_ref.dtype)
        dkc = lax.dot_general(ds, qv, (((0,), (0,)), ((), ())), preferred_element_type=F32) * math.log(2.0)
        dv = lax.dot_general(e.astype(BF16), dor, (((0,), (0,)), ((), ())), preferred_element_type=F32)
        lane = lax.broadcasted_iota(jnp.int32, dkc.shape, 1)
        dkv_ref[0] += jnp.where(lane < NOPE, dkc, dv)
        dkr_ref[0] += jnp.where(lane >= NOPE, dkc, 0.0)

    qspec = pl.BlockSpec((1, tq, HEAD_BLOCK), lambda b, h, i: (b, i, h))
    kspec = pl.BlockSpec((1, T, HEAD_BLOCK), lambda b, h, i: (b, 0, h))
    rspec = pl.BlockSpec((1, T, HEAD_BLOCK), lambda b, h, i: (b, 0, 0))
    tab = pl.BlockSpec((tq, LANE), lambda b, h, i: (i, 0))
    return pl.pallas_call(
        body, name="attn_bwd", grid=(nb, N_HEADS, S // tq),
        in_specs=[qspec, kspec, rspec, qspec, tab, tab, tab, tab], out_specs=[qspec, kspec, rspec],
        out_shape=[jax.ShapeDtypeStruct((nb, S, QP), BF16), jax.ShapeDtypeStruct((nb, T, QP), F32),
                   jax.ShapeDtypeStruct((nb, T, HEAD_BLOCK), F32)],
        compiler_params=_cparams("arbitrary", "arbitrary", "arbitrary"),
    )(q_raw, kv, kr, do, cos_q, sin_q, cos, sin)


CONV_HALO = 8


def _segments(n, n_ctx):
    if n_ctx == 0:
        return [(0, n, CONV_HALO)]
    return [(0, n_ctx, CONV_HALO), (n_ctx, n - n_ctx, 2 * CONV_HALO + n_ctx)]


def _halo_scratch(n, n_ctx, tc):
    return pltpu.VMEM((n + CONV_HALO * (len(_segments(n, n_ctx)) + 1), tc), F32)


def _zero_halos(scr, segs):
    z = jnp.zeros((CONV_HALO, scr.shape[1]), scr.dtype)
    scr[0:CONV_HALO, :] = z
    for (_, rows, off) in segs:
        scr[off + rows:off + rows + CONV_HALO, :] = z


CONV_BLOCK = 64


def _window(scr, off, r0):
    return scr[pl.ds(pl.multiple_of(off - CONV_HALO + r0, 8), CONV_BLOCK + 2 * CONV_HALO), :]


def _shifted(win, s):
    v = win if s == 0 else pltpu.roll(win, (-s) % win.shape[0], 0)
    return v[CONV_HALO:CONV_HALO + CONV_BLOCK]


def _taps(win, w, k, sign):
    acc = None
    for o in range(k):
        t = w[o:o + 1, :] * _shifted(win, sign * (o - k // 2))
        acc = t if acc is None else acc + t
    return acc


def _tap_grads(xwin, dpre, k):
    sub8 = lax.broadcasted_iota(jnp.int32, (8, dpre.shape[1]), 0)
    out = jnp.where(sub8 == k, jnp.sum(dpre, axis=0, keepdims=True), 0.0)
    for o in range(k):
        out = out + jnp.where(sub8 == o, jnp.sum(dpre * _shifted(xwin, o - k // 2), axis=0, keepdims=True), 0.0)
    return out


def _row_blocks(rows, fn, init=0):
    return lax.fori_loop(0, rows // CONV_BLOCK, lambda i, c: fn(pl.multiple_of(i * CONV_BLOCK, CONV_BLOCK), c), init)


def _gelu(x):
    return 0.5 * x * (1.0 + lax.erf(x * (1.0 / math.sqrt(2.0))))


def _gelu_grad(x):
    return 0.5 * (1.0 + lax.erf(x * (1.0 / math.sqrt(2.0)))) + x * jnp.exp(-0.5 * x * x) * (1.0 / math.sqrt(2.0 * math.pi))


def ssd_conv_fwd(u, w8, bias, n_ctx, tc):
    nb, T, _ = u.shape
    cb0 = OFF_XBC // tc

    segs = _segments(T, n_ctx)

    def body(x_ref, w_ref, b_ref, o_ref, xs):
        _zero_halos(xs, segs)
        for (start, rows, off) in segs:
            xs[off:off + rows, :] = x_ref[0, start:start + rows, :]
        w, bias_v = w_ref[...], b_ref[...]
        for (start, rows, off) in segs:
            def block(r0, carry, start=start, off=off):
                o_ref[0, pl.ds(pl.multiple_of(start + r0, CONV_BLOCK), CONV_BLOCK), :] = _silu(bias_v + _taps(_window(xs, off, r0), w, SSD_K, 1))
                return carry

            _row_blocks(rows, block)

    return pl.pallas_call(
        body, name="ssd_conv_fwd", grid=(nb, XBC // tc),
        in_specs=[pl.BlockSpec((1, T, tc), lambda b, j: (b, 0, cb0 + j)),
                  pl.BlockSpec((8, tc), lambda b, j: (0, j)), pl.BlockSpec((1, tc), lambda b, j: (0, j))],
        out_specs=pl.BlockSpec((1, T, tc), lambda b, j: (b, 0, j)),
        out_shape=jax.ShapeDtypeStruct((nb, T, XBC), F32),
        scratch_shapes=[_halo_scratch(T, n_ctx, tc)],
        compiler_params=_cparams("arbitrary", "arbitrary"),
    )(u, w8, bias)


def ssd_conv_bwd(u, w8, bias, dxbc, dxs_direct, n_ctx, tc):
    nb, T, _ = u.shape
    cb0 = OFF_XBC // tc
    n_direct = D_INNER // tc

    segs = _segments(T, n_ctx)

    def body(x_ref, w_ref, b_ref, d0_ref, d1_ref, dd_ref, dx_ref, dw_ref, xs, ds):
        j, b = pl.program_id(0), pl.program_id(1)
        _zero_halos(xs, segs)
        _zero_halos(ds, segs)
        for (start, rows, off) in segs:
            xs[off:off + rows, :] = x_ref[0, start:start + rows, :]
        w, bias_v = w_ref[...], b_ref[...]
        has_direct = (j < n_direct).astype(F32)
        rows = jnp.zeros((8, tc), F32)
        for (start, n_rows, off) in segs:
            def block(r0, acc, start=start, off=off):
                xwin = _window(xs, off, r0)
                pre = bias_v + _taps(xwin, w, SSD_K, 1)
                d = d0_ref[0, 0, pl.ds(pl.multiple_of(start + r0, CONV_BLOCK), CONV_BLOCK), :] + d1_ref[0, 0, pl.ds(pl.multiple_of(start + r0, CONV_BLOCK), CONV_BLOCK), :]
                if start == n_ctx:
                    d = d + dd_ref[0, pl.ds(r0, CONV_BLOCK), :] * has_direct
                sg = jax.nn.sigmoid(pre)
                dpre = d * (sg * (1.0 + pre * (1.0 - sg)))
                ds[pl.ds(pl.multiple_of(off + r0, 8), CONV_BLOCK), :] = dpre
                return acc + _tap_grads(xwin, dpre, SSD_K)

            rows = _row_blocks(n_rows, block, rows)
        for (start, n_rows, off) in segs:
            def block_dx(r0, carry, start=start, off=off):
                dx_ref[0, pl.ds(pl.multiple_of(start + r0, CONV_BLOCK), CONV_BLOCK), :] = _taps(_window(ds, off, r0), w, SSD_K, -1).astype(dx_ref.dtype)
                return carry

            _row_blocks(n_rows, block_dx)

        @pl.when(b == 0)
        def _():
            dw_ref[...] = rows

        @pl.when(b > 0)
        def _():
            dw_ref[...] += rows

    dspec0 = pl.BlockSpec((1, 1, T, tc), lambda j, b: (0, b, 0, j))
    dspec1 = pl.BlockSpec((1, 1, T, tc), lambda j, b: (1, b, 0, j))
    return pl.pallas_call(
        body, name="ssd_conv_bwd", grid=(XBC // tc, nb),
        in_specs=[pl.BlockSpec((1, T, tc), lambda j, b: (b, 0, cb0 + j)),
                  pl.BlockSpec((8, tc), lambda j, b: (0, j)), pl.BlockSpec((1, tc), lambda j, b: (0, j)),
                  dspec0, dspec1,
                  pl.BlockSpec((1, T - n_ctx, tc), lambda j, b: (b, 0, jnp.minimum(j, n_direct - 1)))],
        out_specs=[pl.BlockSpec((1, T, tc), lambda j, b: (b, 0, j)), pl.BlockSpec((8, tc), lambda j, b: (0, j))],
        out_shape=[jax.ShapeDtypeStruct((nb, T, XBC), BF16), jax.ShapeDtypeStruct((8, XBC), F32)],
        scratch_shapes=[_halo_scratch(T, n_ctx, tc), _halo_scratch(T, n_ctx, tc)],
        compiler_params=_cparams("arbitrary", "arbitrary"),
    )(u, w8, bias, dxbc, dxbc, dxs_direct)


GLU_TC = 256


def glu_interleave(w_up):
    blocks = []
    for j in range(D_FF // GLU_TC):
        blocks += [w_up[:, j * GLU_TC:(j + 1) * GLU_TC], w_up[:, D_FF + j * GLU_TC:D_FF + (j + 1) * GLU_TC]]
    return jnp.concatenate(blocks, axis=1)


def glu_deinterleave(g):
    nj = D_FF // GLU_TC
    gate = [g[:, 2 * j * GLU_TC:(2 * j + 1) * GLU_TC] for j in range(nj)]
    val = [g[:, (2 * j + 1) * GLU_TC:(2 * j + 2) * GLU_TC] for j in range(nj)]
    return jnp.concatenate(gate + val, axis=1)


def glu_fwd(up, w8, bias):
    nb, S, _ = up.shape
    tc = GLU_TC

    segs = _segments(S, 0)
    (_, _, off), = segs

    def body(u_ref, w_ref, b_ref, o_ref, xs):
        _zero_halos(xs, segs)
        xs[off:off + S, :] = u_ref[0, :, :tc]
        w, bias_v = w_ref[...], b_ref[...]

        def block(r0, carry):
            gc = bias_v + _taps(_window(xs, off, r0), w, FFN_K, 1)
            o_ref[0, pl.ds(r0, CONV_BLOCK), :] = (_gelu(gc) * u_ref[0, pl.ds(r0, CONV_BLOCK), tc:]).astype(o_ref.dtype)
            return carry

        _row_blocks(S, block)

    return pl.pallas_call(
        body, name="glu_fwd", grid=(nb, D_FF // tc),
        in_specs=[pl.BlockSpec((1, S, 2 * tc), lambda b, j: (b, 0, j)),
                  pl.BlockSpec((8, tc), lambda b, j: (0, j)), pl.BlockSpec((1, tc), lambda b, j: (0, j))],
        out_specs=pl.BlockSpec((1, S, tc), lambda b, j: (b, 0, j)),
        out_shape=jax.ShapeDtypeStruct((nb, S, D_FF), BF16),
        scratch_shapes=[_halo_scratch(S, 0, tc)],
        compiler_params=_cparams("arbitrary", "arbitrary"),
    )(up, w8, bias)


def glu_bwd(up, w8, bias, dact):
    nb, S, _ = up.shape
    tc = GLU_TC

    segs = _segments(S, 0)
    (_, _, off), = segs

    def body(u_ref, w_ref, b_ref, d_ref, du_ref, dw_ref, xs, ds):
        b = pl.program_id(1)
        _zero_halos(xs, segs)
        _zero_halos(ds, segs)
        xs[off:off + S, :] = u_ref[0, :, :tc]
        w, bias_v = w_ref[...], b_ref[...]

        def block(r0, acc):
            here = pl.ds(r0, CONV_BLOCK)
            xwin = _window(xs, off, r0)
            gc = bias_v + _taps(xwin, w, FFN_K, 1)
            d = d_ref[0, here, :].astype(F32)
            du_ref[0, here, tc:] = (d * _gelu(gc)).astype(du_ref.dtype)
            dpre = d * u_ref[0, here, tc:] * _gelu_grad(gc)
            ds[pl.ds(pl.multiple_of(off + r0, 8), CONV_BLOCK), :] = dpre
            return acc + _tap_grads(xwin, dpre, FFN_K)

        rows = _row_blocks(S, block, jnp.zeros((8, tc), F32))

        def block_dx(r0, carry):
            du_ref[0, pl.ds(r0, CONV_BLOCK), :tc] = _taps(_window(ds, off, r0), w, FFN_K, -1).astype(du_ref.dtype)
            return carry

        _row_blocks(S, block_dx)

        @pl.when(b == 0)
        def _():
            dw_ref[...] = rows

        @pl.when(b > 0)
        def _():
            dw_ref[...] += rows

    pair = pl.BlockSpec((1, S, 2 * tc), lambda j, b: (b, 0, j))
    return pl.pallas_call(
        body, name="glu_bwd", grid=(D_FF // tc, nb),
        in_specs=[pair, pl.BlockSpec((8, tc), lambda j, b: (0, j)), pl.BlockSpec((1, tc), lambda j, b: (0, j)),
                  pl.BlockSpec((1, S, tc), lambda j, b: (b, 0, j))],
        out_specs=[pair, pl.BlockSpec((8, tc), lambda j, b: (0, j))],
        out_shape=[jax.ShapeDtypeStruct((nb, S, 2 * D_FF), BF16), jax.ShapeDtypeStruct((8, D_FF), F32)],
        scratch_shapes=[_halo_scratch(S, 0, tc), _halo_scratch(S, 0, tc)],
        compiler_params=_cparams("arbitrary", "arbitrary"),
    )(up, w8, bias, dact)


def _chunk_of(d, k, n_cc, n_ch):
    rev = jnp.where(k < n_cc, n_cc - 1 - k, n_cc + n_ch - 1 - k)
    return jnp.where(d == 1, rev, k)


def _lane_pick(v, lane_iota, l):
    return jnp.sum(jnp.where(lane_iota == l, v, 0.0), axis=1, keepdims=True)


def head_spread_matrix():
    return (jnp.arange(LANE)[:, None] == (jnp.arange(D_INNER)[None, :] // SSD_P)).astype(BF16)


def _split_dot(x, e, dims):
    hi = x.astype(BF16)
    lo = (x - hi.astype(F32)).astype(BF16)
    return (lax.dot_general(hi, e, dims, preferred_element_type=F32)
            + lax.dot_general(lo, e, dims, preferred_element_type=F32))


def _spread(x, e):
    return _split_dot(x, e, (((1,), (0,)), ((), ())))


def _gather_heads(y, e):
    return _split_dot(y, e, (((1,), (1,)), ((), ())))


def _softplus(x):
    return jnp.maximum(x, 0.0) + jnp.log(1.0 + jnp.exp(-jnp.abs(x)))


def ssd_dt_inputs(u, a_log, dt_bias):
    pad = LANE - SSD_HEADS
    dt = u[..., OFF_DT:OFF_DT + 2 * SSD_HEADS]
    dt2 = jnp.stack([jnp.pad(dt[..., i * SSD_HEADS:(i + 1) * SSD_HEADS], ((0, 0), (0, 0), (0, pad))) for i in range(2)])

    def lanes(v):
        return jnp.pad(v.reshape(2, 1, SSD_HEADS), ((0, 0), (0, 0), (0, pad)))

    return dt2, lanes(a_log), lanes(dt_bias)


def _ssd_common(d, dt_raw, alog, dtb):
    Q = dt_raw.shape[0]
    row = lax.broadcasted_iota(jnp.int32, (Q, Q), 0)
    col = lax.broadcasted_iota(jnp.int32, (Q, Q), 1)
    rev = d == 1
    maskb = jnp.where(rev, row, col) <= jnp.where(rev, col, row)
    tri = maskb.astype(F32)
    A = -jnp.exp(alog)
    dtv = _softplus(dt_raw + dtb)
    a = dtv * A
    cum = lax.dot_general(tri, a, (((1,), (0,)), ((), ())), precision=lax.Precision.HIGHEST, preferred_element_type=F32)
    tot = jnp.sum(a, axis=0, keepdims=True)
    return maskb, tri, A, dtv, cum, tot


def ssd_fwd(xbc, dt2, alog2, dtb2, n_ctx, hosted):
    nb, T, _ = xbc.shape
    S = T - n_ctx
    n_ch, n_cc = T // CHUNK, n_ctx // CHUNK
    Q = CHUNK
    n_pairs = SSD_HEADS // 2
    n_ex = hosted.n
    n_in = 5

    def body(*refs):
        x_ref, dt_ref, al_ref, db_ref, e_ref = refs[:n_in]
        send_refs = refs[n_in:n_in + n_ex]
        y_ref, hin_ref = refs[n_in + n_ex:n_in + 2 + n_ex]
        recv_refs = refs[n_in + 2 + n_ex:n_in + 2 + 2 * n_ex]
        H, *sems = refs[n_in + 2 + 2 * n_ex:]
        d, k = pl.program_id(1), pl.program_id(2)
        first_step = jnp.logical_and(jnp.logical_and(pl.program_id(0) == 0, d == 0), k == 0)
        last_step = jnp.logical_and(jnp.logical_and(pl.program_id(0) == nb - 1, d == 1), k == n_ch - 1)
        begin_exchange, end_exchange = hosted.steps(send_refs, recv_refs, sems, first_step, last_step)
        begin_exchange()

        @pl.when(k == 0)
        def _():
            H[...] = jnp.zeros_like(H)

        maskb, tri, A, dtv, cum, tot = _ssd_common(d, dt_ref[0, 0], al_ref[0], db_ref[0])
        e = e_ref[...]
        cumT = cum.T
        cum_e, dt_e = _spread(cum, e), _spread(dtv, e)
        tot_e = _spread(jnp.broadcast_to(tot, (8, LANE)), e)[0:1]
        hin_ref[0, 0, 0] = H[...].astype(BF16)
        lane = lax.broadcasted_iota(jnp.int32, (Q, LANE), 1)
        lane1 = lax.broadcasted_iota(jnp.int32, (1, LANE), 1)
        subc = lax.broadcasted_iota(jnp.int32, (LANE, 1), 0)
        half = lane < SSD_P
        for g in range(SSD_GROUPS):
            Bg = x_ref[0, :, D_INNER + g * SSD_N:D_INNER + (g + 1) * SSD_N].astype(BF16)
            Cg = x_ref[0, :, D_INNER + GN + g * SSD_N:D_INNER + GN + (g + 1) * SSD_N].astype(BF16)
            Gm = lax.dot_general(Cg, Bg, (((1,), (1,)), ((), ())), preferred_element_type=F32)
            for pr in range(n_pairs // SSD_GROUPS):
                p = g * (n_pairs // SSD_GROUPS) + pr
                sc, dtp, totp = [t[:, p * LANE:(p + 1) * LANE] for t in (cum_e, dt_e, tot_e)]
                swapped = pltpu.roll(sc, SSD_P, 1)
                s0c, s1c = jnp.where(half, sc, swapped), jnp.where(half, swapped, sc)
                s0r, s1r = cumT[2 * p:2 * p + 1, :], cumT[2 * p + 1:2 * p + 2, :]
                tot0, tot1 = _lane_pick(tot, lane1, 2 * p), _lane_pick(tot, lane1, 2 * p + 1)
                M0 = (Gm * jnp.exp(jnp.where(maskb, s0c - s0r, NEG_BIG))).astype(BF16)
                M1 = (Gm * jnp.exp(jnp.where(maskb, s1c - s1r, NEG_BIG))).astype(BF16)
                xd = x_ref[0, :, p * LANE:(p + 1) * LANE] * dtp
                xdb = xd.astype(BF16)
                yd = jnp.where(half,
                               lax.dot_general(M0, xdb, (((1,), (0,)), ((), ())), preferred_element_type=F32),
                               lax.dot_general(M1, xdb, (((1,), (0,)), ((), ())), preferred_element_type=F32))
                Hp = H[p * LANE:(p + 1) * LANE, :]
                yo = lax.dot_general(Cg, Hp.astype(BF16), (((1,), (1,)), ((), ())), preferred_element_type=F32) * jnp.exp(sc)

                y_ref[0, 0, :, p * LANE:(p + 1) * LANE] = yd + yo

                xdw = (xd * jnp.exp(totp - sc)).astype(BF16)
                etot = jnp.exp(jnp.where(subc < SSD_P, tot0, tot1))
                H[p * LANE:(p + 1) * LANE, :] = Hp * etot + lax.dot_general(
                    xdw, Bg, (((0,), (0,)), ((), ())), preferred_element_type=F32)
        end_exchange()

    def ymap(b, d, k):
        return (d, b, _chunk_of(d, jnp.maximum(k, n_cc), n_cc, n_ch) - n_cc, 0)

    return pl.pallas_call(
        body, name="ssd_fwd", grid=(nb, 2, n_ch),
        in_specs=[pl.BlockSpec((1, Q, XBC), lambda b, d, k: (b, _chunk_of(d, k, n_cc, n_ch), 0)),
                  pl.BlockSpec((1, 1, Q, LANE), lambda b, d, k: (d, b, _chunk_of(d, k, n_cc, n_ch), 0)),
                  pl.BlockSpec((1, 1, LANE), lambda b, d, k: (d, 0, 0)), pl.BlockSpec((1, 1, LANE), lambda b, d, k: (d, 0, 0)),
                  pl.BlockSpec((LANE, D_INNER), lambda b, d, k: (0, 0))] + hosted.specs,
        out_specs=[pl.BlockSpec((1, 1, Q, D_INNER), ymap),
                   pl.BlockSpec((1, 1, 1, D_INNER, SSD_N), lambda b, d, k: (d, b, k, 0, 0))] + hosted.specs,
        out_shape=[jax.ShapeDtypeStruct((2, nb, S, D_INNER), F32),
                   jax.ShapeDtypeStruct((2, nb, n_ch, D_INNER, SSD_N), BF16)] + hosted.out_shape,
        scratch_shapes=[pltpu.VMEM((D_INNER, SSD_N), F32)] + hosted.scratch,
        compiler_params=_cparams("arbitrary", "arbitrary", "arbitrary"),
    )(xbc, dt2, alog2, dtb2, head_spread_matrix(), *hosted.arrays)


def ssd_bwd(xbc, dt2, alog2, dtb2, hin, dy, n_ctx, hosted):
    nb, T, _ = xbc.shape
    n_ex = hosted.n
    n_ch, n_cc = T // CHUNK, n_ctx // CHUNK
    n_in = 7
    Q = CHUNK
    n_pairs = SSD_HEADS // 2
    NT = (((1,), (1,)), ((), ()))
    NN = (((1,), (0,)), ((), ()))
    TN = (((0,), (0,)), ((), ()))

    def dot(a, b, dims):
        return lax.dot_general(a.astype(BF16), b.astype(BF16), dims, preferred_element_type=F32)

    def body(*refs):
        x_ref, dt_ref, al_ref, db_ref, e_ref, hin_ref, dy_ref = refs[:n_in]
        send_refs = refs[n_in:n_in + n_ex]
        dx_ref, ddt_ref, st_ref = refs[n_in + n_ex:n_in + 3 + n_ex]
        recv_refs = refs[n_in + 3 + n_ex:n_in + 3 + 2 * n_ex]
        dH, dce, dde, *sems = refs[n_in + 3 + 2 * n_ex:]
        d, kk = pl.program_id(1), pl.program_id(2)
        ks = n_ch - 1 - kk
        first_step = jnp.logical_and(jnp.logical_and(pl.program_id(0) == 0, d == 0), kk == 0)
        last_step = jnp.logical_and(jnp.logical_and(pl.program_id(0) == nb - 1, d == 1), kk == n_ch - 1)
        begin_exchange, end_exchange = hosted.steps(send_refs, recv_refs, sems, first_step, last_step)
        begin_exchange()

        @pl.when(kk == 0)
        def _():
            dH[...] = jnp.zeros_like(dH)

        @pl.when(jnp.logical_and(jnp.logical_and(pl.program_id(0) == 0, d == 0), kk == 0))
        def _():
            st_ref[...] = jnp.zeros_like(st_ref)

        dt_raw = dt_ref[0, 0]
        alog, dtb_v = al_ref[0], db_ref[0]
        maskb, tri, A, dtv, cum, tot = _ssd_common(d, dt_raw, alog, dtb_v)
        e = e_ref[...]
        cumT = cum.T
        cum_e, dt_e = _spread(cum, e), _spread(dtv, e)
        tot_e = _spread(jnp.broadcast_to(tot, (8, LANE)), e)[0:1]
        live = (ks >= n_cc).astype(F32)
        lane = lax.broadcasted_iota(jnp.int32, (Q, LANE), 1)
        lane1 = lax.broadcasted_iota(jnp.int32, (1, LANE), 1)
        sub = lax.broadcasted_iota(jnp.int32, (LANE, Q), 0)
        subc = lax.broadcasted_iota(jnp.int32, (LANE, 1), 0)
        half = lane < SSD_P
        halfc = subc < SSD_P
        ones = jnp.ones((LANE, LANE), BF16)
        dcum = jnp.zeros((Q, LANE), F32)
        dcumT = jnp.zeros((LANE, Q), F32)
        dtot = jnp.zeros((1, LANE), F32)
        dtot_parts = []
        for g in range(SSD_GROUPS):
            Bg = x_ref[0, :, D_INNER + g * SSD_N:D_INNER + (g + 1) * SSD_N].astype(BF16)
            Cg = x_ref[0, :, D_INNER + GN + g * SSD_N:D_INNER + GN + (g + 1) * SSD_N].astype(BF16)
            Gm = lax.dot_general(Cg, Bg, NT, preferred_element_type=F32)
            dG = jnp.zeros((Q, Q), F32)
            dC = jnp.zeros((Q, SSD_N), F32)
            dB = jnp.zeros((Q, SSD_N), F32)
            for pr in range(n_pairs // SSD_GROUPS):
                p = g * (n_pairs // SSD_GROUPS) + pr
                l0, l1 = 2 * p, 2 * p + 1
                sc, dtp, totp = [t[:, p * LANE:(p + 1) * LANE] for t in (cum_e, dt_e, tot_e)]
                swapped = pltpu.roll(sc, SSD_P, 1)
                s0c, s1c = jnp.where(half, sc, swapped), jnp.where(half, swapped, sc)
                s0r, s1r = cumT[l0:l0 + 1, :], cumT[l1:l1 + 1, :]
                tot0, tot1 = _lane_pick(tot, lane1, l0), _lane_pick(tot, lane1, l1)
                L0 = jnp.exp(jnp.where(maskb, s0c - s0r, NEG_BIG))
                L1 = jnp.exp(jnp.where(maskb, s1c - s1r, NEG_BIG))
                M0, M1 = Gm * L0, Gm * L1
                xs = x_ref[0, :, p * LANE:(p + 1) * LANE]
                xd = xs * dtp
                es = jnp.exp(sc)
                dte = jnp.exp(totp - sc)
                etot = jnp.exp(jnp.where(halfc, tot0, tot1))
                dyp = dy_ref[0, :, p * LANE:(p + 1) * LANE] * live
                Hp = hin_ref[0, 0, 0, p * LANE:(p + 1) * LANE, :]
                dHp = dH[p * LANE:(p + 1) * LANE, :]
                bdh = dot(Bg, dHp, NT)
                dxd = jnp.where(half, dot(M0, dyp, TN), dot(M1, dyp, TN)) + bdh * dte
                dy0 = jnp.where(half, dyp, 0.0)
                dy1 = dyp - dy0
                dM0, dM1 = dot(dy0, xd, NT), dot(dy1, xd, NT)
                dG = dG + dM0 * L0 + dM1 * L1
                dyes = dyp * es
                xdw = xd * dte
                dC = dC + dot(dyes, Hp, NN)
                dB = dB + dot(xdw, dHp, NN)
                W0, W1 = dM0 * M0, dM1 * M1
                yoff = dot(Cg, Hp, NT) * es
                r_off = dyp * yoff
                r_st = xd * bdh * dte
                hh = jnp.sum(dHp * Hp.astype(F32), axis=1, keepdims=True) * etot
                dce[:, p * LANE:(p + 1) * LANE] = r_off - r_st
                dde[:, p * LANE:(p + 1) * LANE] = dxd * xs
                dtot_parts.append(jnp.sum(r_st, axis=0, keepdims=True))
                for (l, W, hselc) in ((l0, W0, halfc), (l1, W1, jnp.logical_not(halfc))):
                    col_g = _split_dot(W, ones, NN)
                    row_g = -jnp.sum(W, axis=0, keepdims=True)
                    dcum = dcum + jnp.where(lane == l, col_g, 0.0)
                    dcumT = dcumT + jnp.where(sub == l, row_g, 0.0)
                    dtot = dtot + jnp.where(lane1 == l, jnp.sum(jnp.where(hselc, hh, 0.0), axis=0, keepdims=True), 0.0)
                dx_ref[0, 0, :, p * LANE:(p + 1) * LANE] = dxd * dtp
                dH[p * LANE:(p + 1) * LANE, :] = dHp * etot + dot(dyes, Cg, TN)
            dx_ref[0, 0, :, D_INNER + g * SSD_N:D_INNER + (g + 1) * SSD_N] = dB + dot(dG, Cg, TN)
            dx_ref[0, 0, :, D_INNER + GN + g * SSD_N:D_INNER + GN + (g + 1) * SSD_N] = dC + dot(dG, Bg, NN)
        dcum_all = dcum + dcumT.T + _gather_heads(dce[...], e)
        dtot_e = jnp.broadcast_to(jnp.concatenate(dtot_parts, axis=1), (8, D_INNER))
        dtot = dtot + _gather_heads(dtot_e, e)[0:1]
        da = lax.dot_general(tri, dcum_all, TN, precision=lax.Precision.HIGHEST, preferred_element_type=F32) + dtot
        ddtv = _gather_heads(dde[...], e) + da * A
        ddt_raw = ddtv * jax.nn.sigmoid(dt_raw + dtb_v)
        ddt_ref[0, 0] = ddt_raw
        sub8 = lax.broadcasted_iota(jnp.int32, (8, LANE), 0)
        st_ref[...] += (jnp.where(sub8 == 2 * d, jnp.sum(da * dtv * A, axis=0, keepdims=True), 0.0)
                        + jnp.where(sub8 == 2 * d + 1, jnp.sum(ddt_raw, axis=0, keepdims=True), 0.0))
        end_exchange()

    def cmap(d, kk):
        return _chunk_of(d, n_ch - 1 - kk, n_cc, n_ch)

    def dymap(b, d, kk):
        return (b, _chunk_of(d, jnp.maximum(n_ch - 1 - kk, n_cc), n_cc, n_ch) - n_cc, 0)

    return pl.pallas_call(
        body, name="ssd_bwd", grid=(nb, 2, n_ch),
        in_specs=[pl.BlockSpec((1, Q, XBC), lambda b, d, kk: (b, cmap(d, kk), 0)),
                  pl.BlockSpec((1, 1, Q, LANE), lambda b, d, kk: (d, b, cmap(d, kk), 0)),
                  pl.BlockSpec((1, 1, LANE), lambda b, d, kk: (d, 0, 0)), pl.BlockSpec((1, 1, LANE), lambda b, d, kk: (d, 0, 0)),
                  pl.BlockSpec((LANE, D_INNER), lambda b, d, kk: (0, 0)),
                  pl.BlockSpec((1, 1, 1, D_INNER, SSD_N), lambda b, d, kk: (d, b, n_ch - 1 - kk, 0, 0)),
                  pl.BlockSpec((1, Q, D_INNER), dymap)] + hosted.specs,
        out_specs=[pl.BlockSpec((1, 1, Q, XBC), lambda b, d, kk: (d, b, cmap(d, kk), 0)),
                   pl.BlockSpec((1, 1, Q, LANE), lambda b, d, kk: (d, b, cmap(d, kk), 0)),
                   pl.BlockSpec((8, LANE), lambda b, d, kk: (0, 0))] + hosted.specs,
        out_shape=[jax.ShapeDtypeStruct((2, nb, T, XBC), F32), jax.ShapeDtypeStruct((2, nb, T, LANE), F32),
                   jax.ShapeDtypeStruct((8, LANE), F32)] + hosted.out_shape,
        scratch_shapes=[pltpu.VMEM((D_INNER, SSD_N), F32), pltpu.VMEM((Q, D_INNER), F32), pltpu.VMEM((Q, D_INNER), F32)] + hosted.scratch,
        compiler_params=_cparams("arbitrary", "arbitrary", "arbitrary"),
    )(xbc, dt2, alog2, dtb2, head_spread_matrix(), hin, dy, *hosted.arrays)


def _adamw(w, g, m, v):
    mn = ADAM_B1 * m + (1.0 - ADAM_B1) * g
    vn = ADAM_B2 * v + (1.0 - ADAM_B2) * jnp.square(g)
    m_hat = mn / (1.0 - ADAM_B1 ** ADAM_STEP)
    v_hat = vn / (1.0 - ADAM_B2 ** ADAM_STEP)
    return -ADAM_LR * (m_hat / (jnp.sqrt(v_hat) + ADAM_EPS) + ADAM_WD * w), mn, vn


def adamw_matrix(name, w, g_slots, m, v):
    K, n = w.shape
    s = g_slots.shape[0]
    tr = _tile(K, 256, 8)

    def body(w_ref, g_ref, m_ref, v_ref, go_ref, d_ref, mo_ref, vo_ref):
        g = g_ref[0].astype(F32)
        for j in range(1, s):
            g = g + g_ref[j].astype(F32)
        go_ref[...] = g
        d_ref[...], mo_ref[...], vo_ref[...] = _adamw(w_ref[...], g, m_ref[...], v_ref[...])

    spec = pl.BlockSpec((tr, n), lambda i: (i, 0))
    return pl.pallas_call(
        body, name=name, grid=(K // tr,),
        in_specs=[spec, pl.BlockSpec((s, tr, n), lambda i: (0, i, 0)), spec, spec], out_specs=[spec] * 4,
        out_shape=[jax.ShapeDtypeStruct((K, n), F32)] * 4,
        compiler_params=_cparams("arbitrary"),
    )(w, g_slots, m, v)


def adamw_small(ws, gs, ms, vs):
    n = len(ws)

    def body(*refs):
        for i in range(n):
            d, mn, vn = _adamw(refs[i][...], refs[n + i][...], refs[2 * n + i][...], refs[3 * n + i][...])
            refs[4 * n + i][...] = d
            refs[5 * n + i][...] = mn
            refs[6 * n + i][...] = vn

    shapes = [jax.ShapeDtypeStruct(w.shape, F32) for w in ws]
    out = pl.pallas_call(body, name="adamw_small", out_shape=shapes * 3)(*ws, *gs, *ms, *vs)
    return out[:n], out[n:2 * n], out[2 * n:]


def sum_slots(name, x):
    n = x.shape[0]

    def fn(t):
        acc = t[0]
        for j in range(1, n):
            acc = acc + t[j]
        return (acc,)

    return ew_call(name, fn, [x], [(x.shape[1:], F32)])[0]


def _pack_rows(parts):
    rows = []
    for p in parts:
        flat = p.reshape(1, -1)
        n = flat.shape[1]
        rows.append(jnp.pad(flat, ((0, 0), (0, -(-n // (8 * LANE)) * 8 * LANE - n))).reshape(-1, LANE))
    return jnp.concatenate(rows, axis=0)


def _unpack_rows(pack, shapes):
    out, r = [], 0
    for s in shapes:
        n = int(np.prod(s))
        nr = -(-n // (8 * LANE)) * 8
        out.append(pack[r:r + nr].reshape(1, -1)[:, :n].reshape(s))
        r += nr
    return out


def _mesh_pos():
    return lax.axis_index("x"), lax.axis_index("y"), lax.axis_index("c")


N_PEERS = N_DEV - 1


def all_gather(name, vs):
    n = len(vs)

    def body(*refs):
        _ag_start(refs[:n], refs[n:2 * n], *refs[2 * n:])
        _ag_finish(refs[:n], refs[n:2 * n], *refs[2 * n:])

    hbm = pl.BlockSpec(memory_space=pl.ANY)
    return pl.pallas_call(
        body, name=name, out_shape=_ag_out_shape(vs), in_specs=[hbm] * n, out_specs=[hbm] * n,
        scratch_shapes=_a2a_scratch(n),
    )(*vs)


def _ag_out_shape(vs):
    return [jax.ShapeDtypeStruct((N_DEV,) + v.shape, v.dtype) for v in vs]


def _ag_copies(x_refs, out_refs, send_sems, recv_sems, local_sems):
    n = len(x_refs)
    x, y, c = _mesh_pos()
    me, sibling = (x, y, c), (x, y, 1 - c)
    chips = [(1 - x, y), (x, 1 - y), (1 - x, 1 - y)]

    def slot(a, px, py, pc):
        return out_refs[a].at[4 * px + 2 * py + pc]

    def copy(a, k, block, to, src=None):
        return pltpu.make_async_remote_copy(
            src_ref=slot(a, *block) if src is None else src, dst_ref=slot(a, *block),
            send_sem=send_sems.at[N_PEERS * a + k], recv_sem=recv_sems.at[N_PEERS * a + k],
            device_id=to, device_id_type=MESH)

    local = [pltpu.make_async_copy(x_refs[a], slot(a, *me), local_sems.at[a]) for a in range(n)]
    first = []
    for a in range(n):
        first.append(copy(a, 0, me, sibling, src=x_refs[a]))
        first += [copy(a, 1 + j, me, (*chip, c), src=x_refs[a]) for j, chip in enumerate(chips)]
    passed = [(copy(a, 1 + j, (*chip, c), me), copy(a, 4 + j, (*chip, c), sibling))
              for j, chip in enumerate(chips) for a in range(n)]
    from_sibling = []
    for a in range(n):
        from_sibling.append(copy(a, 0, sibling, me))
        from_sibling += [copy(a, 4 + j, (*chip, 1 - c), me) for j, chip in enumerate(chips)]
    return local, first, passed, from_sibling


def _ag_start(*refs):
    local, first, _, _ = _ag_copies(*refs)
    for cp in local + first:
        cp.start()


def _ag_finish(*refs):
    local, first, passed, from_sibling = _ag_copies(*refs)
    for arrived, hand_on in passed:
        arrived.wait_recv()
        hand_on.start()
    for cp in from_sibling:
        cp.wait_recv()
    for cp in first + [hand_on for _, hand_on in passed]:
        cp.wait_send()
    for cp in local:
        cp.wait()


def _a2a_scratch(n):
    return [pltpu.SemaphoreType.DMA((N_PEERS * n,)), pltpu.SemaphoreType.DMA((N_PEERS * n,)), pltpu.SemaphoreType.DMA((n,))]


def _a2a_copies(x_refs, out_refs, send_sems, recv_sems, local_sems):
    n = len(x_refs)
    x, y, c = _mesh_pos()
    me = 4 * x + 2 * y + c
    local = [pltpu.make_async_copy(x_refs[a].at[me], out_refs[a].at[me], local_sems.at[a]) for a in range(n)]
    remote = []
    for k in range(1, N_DEV):
        px, py, pc = x ^ ((k >> 2) & 1), y ^ ((k >> 1) & 1), c ^ (k & 1)
        for a in range(n):
            remote.append(pltpu.make_async_remote_copy(
                src_ref=x_refs[a].at[4 * px + 2 * py + pc], dst_ref=out_refs[a].at[me],
                send_sem=send_sems.at[N_PEERS * a + k - 1], recv_sem=recv_sems.at[N_PEERS * a + k - 1],
                device_id=(px, py, pc), device_id_type=MESH))
    return local, remote


def _a2a_start(local, remote):
    for cp in local + remote:
        cp.start()


def _a2a_wait(local, remote):
    for cp in remote:
        cp.wait_recv()
    for cp in remote:
        cp.wait_send()
    for cp in local:
        cp.wait()


class Hosted:
    def __init__(self, start=None, finish=None, arrays=(), out_shape=()):
        self.start, self.finish, self.arrays, self.out_shape = start, finish, list(arrays), list(out_shape)
        self.n = len(self.arrays)
        self.specs = [pl.BlockSpec(memory_space=pl.ANY)] * self.n
        self.scratch = _a2a_scratch(self.n) if self.n else []

    def steps(self, send_refs, recv_refs, sems, first_step, last_step):
        def begin():
            if self.n:
                pl.when(first_step)(lambda: self.start(send_refs, recv_refs, *sems))

        def end():
            if self.n:
                pl.when(last_step)(lambda: self.finish(send_refs, recv_refs, *sems))

        return begin, end


def hosted_all_to_all(vs):
    return Hosted(lambda *r: _a2a_start(*_a2a_copies(*r)), lambda *r: _a2a_wait(*_a2a_copies(*r)), vs,
                  [jax.ShapeDtypeStruct(v.shape, v.dtype) for v in vs])


def hosted_all_gather(vs):
    return Hosted(_ag_start, _ag_finish, vs, _ag_out_shape(vs))


def _taps8(w):
    return jnp.concatenate([w, jnp.zeros((8 - w.shape[0], w.shape[1]), w.dtype)], axis=0)


FIRST = ("w_in", "w_q_up", "w_kv_up")
LATE_WEIGHTS = ("w_out", "w_up", "w_down")


def first_weights_to_internal(w_in, w_q_up, w_kv_up):
    cq, ckv, kr, z, xbc, dt = jnp.split(w_in, np.cumsum(IN_SPLITS)[:-1].tolist(), axis=1)
    K = w_in.shape[0]

    def zeros(n):
        return jnp.zeros((K, n), w_in.dtype)

    w_in_p = jnp.concatenate([cq, zeros(KR_LANE), kr, zeros(LANE - KR_LANE - ROPE), ckv, zeros(OFF_Z - OFF_CKV - KV_RANK),
                              z, xbc, dt, zeros(WIN_P - OFF_DT - 2 * SSD_HEADS)], axis=1)
    w_q_p = jnp.pad(w_q_up.reshape(Q_RANK, N_HEADS, NOPE + ROPE), ((0, 0), (0, 0), (0, HEAD_BLOCK - NOPE - ROPE))).reshape(Q_RANK, QP)
    return dict(w_in_p=w_in_p, w_q_p=w_q_p, w_kv=w_kv_up)


def late_weights_to_internal(w_out, w_up, w_down):
    attn_rows = w_out[:N_HEADS * V_DIM].reshape(N_HEADS, V_DIM, -1)
    w_out_p = jnp.concatenate([jnp.pad(attn_rows, ((0, 0), (HEAD_BLOCK - V_DIM, 0), (0, 0))).reshape(QP, -1),
                               w_out[N_HEADS * V_DIM:]], axis=0)
    return dict(w_out_p=w_out_p, w_up=glu_interleave(w_up), w_down=w_down)


def _in_grad(g_in_p):
    return jnp.concatenate([g_in_p[:, OFF_CQ:OFF_CQ + Q_RANK], g_in_p[:, OFF_CKV:OFF_CKV + KV_RANK],
                            g_in_p[:, OFF_KR + KR_LANE:OFF_KR + KR_LANE + ROPE], g_in_p[:, OFF_Z:OFF_Z + D_INNER],
                            g_in_p[:, OFF_XBC:OFF_XBC + XBC], g_in_p[:, OFF_DT:OFF_DT + 2 * SSD_HEADS]], axis=1)


def _q_grad(g_q_p):
    return g_q_p.reshape(Q_RANK, N_HEADS, HEAD_BLOCK)[:, :, :NOPE + ROPE].reshape(Q_RANK, -1)


def _out_grad(g_out_p):
    return jnp.concatenate([g_out_p[:QP].reshape(N_HEADS, HEAD_BLOCK, -1)[:, HEAD_BLOCK - V_DIM:].reshape(N_HEADS * V_DIM, -1),
                            g_out_p[QP:]], axis=0)


EARLY = ("w_out", "w_up", "w_down", "w_q_up", "w_kv_up")


def local_step(x, ctx, target, mod_x, mod_c, W, late_shards, V):
    nb, S, D = x.shape
    C = ctx.shape[1]
    T = C + S
    tr = _tile(math.gcd(C, S), 256, 8)
    tq = _tile(S, 256, 8)
    tc = 256
    cblk = C // tr
    m = [mod_x[:, i * D:(i + 1) * D][:, None, :] for i in range(N_MOD)]
    mc = [mod_c[:, i * D:(i + 1) * D] for i in range(2)]
    ssd_w8, ffn_w8 = _taps8(V["ssd_conv_w"]), _taps8(V["ffn_conv_w"])
    dexp = jnp.repeat(V["ssd_d"].reshape(-1), SSD_P).reshape(1, D_INNER)
    cosT, sinT = rope_tables(C, S)
    cosS, sinS = cosT[C:], sinT[C:]

    (h1x,) = rows_fwd("prenorm_x", fn_prenorm, nb, S // tr, tr, [(x, D, 0, 0)], [m[0], m[1]], [V["mix_pre_norm"]], [(D, BF16)])
    (h1c,) = rows_fwd("prenorm_c", fn_prenorm, nb, C // tr, tr, [(ctx, D, 0, 0)], [], [mc[0], mc[1], V["mix_pre_norm"]], [(D, BF16)])
    h1 = jnp.concatenate([h1c, h1x], axis=1).reshape(nb * T, D)
    u = matmul("in_proj", [(h1, W["w_in_p"])], "nn", F32).reshape(nb, T, WIN_P)
    (qn,) = rows_fwd("q_norm", fn_rms, nb, S // tr, tr, [(u, Q_RANK, OFF_CQ // Q_RANK, cblk)], [], [V["q_norm"]], [(Q_RANK, BF16)])
    (kvn,) = rows_fwd("kv_norm", fn_rms, nb, T // tr, tr, [(u, KV_RANK, OFF_CKV // KV_RANK, 0)], [], [V["kv_norm"]], [(KV_RANK, BF16)])
    qn2, kvn2 = qn.reshape(nb * S, Q_RANK), kvn.reshape(nb * T, KV_RANK)
    q_raw = matmul("q_up", [(qn2, W["w_q_p"])], "nn", F32).reshape(nb, S, QP)
    kv = matmul("kv_up", [(kvn2, W["w_kv"])], "nn", BF16).reshape(nb, T, QP)
    cos_q, sin_q = cosS * Q_PRESCALE, sinS * Q_PRESCALE
    kr = rope_call("rope_k", u, LANE, OFF_KR // LANE, cosT, sinT, BF16, tr)
    o = attn_fwd(q_raw, kv, kr, cos_q, sin_q, tq)
    xbc = ssd_conv_fwd(u, ssd_w8, V["ssd_conv_b"], C, tc)
    dt2, alog2, dtb2 = ssd_dt_inputs(u, V["ssd_a_log"], V["ssd_dt_bias"])
    y2, hin, *late = ssd_fwd(xbc, dt2, alog2, dtb2, C, hosted_all_gather(late_shards))
    W = dict(W, **late_weights_to_internal(*[_whole(s, n) for s, n in zip(late, LATE_WEIGHTS)]))
    y2 = y2.reshape(2 * nb, S, D_INNER)
    fin_rows = [(y2, D_INNER, 0, 0, 0), (y2, D_INNER, 0, 0, nb), (xbc, D_INNER, 0, cblk), (u, D_INNER, OFF_Z // D_INNER, cblk)]
    fin_gl = [dexp, V["ssd_norm"]]
    (ssd,) = rows_fwd("ssd_finish", fn_ssd_finish, nb, S // tr, tr, fin_rows, [], fin_gl, [(D_INNER, BF16)])
    o2, ssd2 = o.reshape(nb * S, QP), ssd.reshape(nb * S, D_INNER)
    mix = matmul("out_proj", [(o2, W["w_out_p"][:QP]), (ssd2, W["w_out_p"][QP:])], "nn", F32).reshape(nb, S, D)
    pm_rows = [(x, D, 0, 0), (mix, D, 0, 0)]
    pm_pb = [m[2], m[4], m[3]]
    pm_gl = [V["mix_post_norm"], V["ffn_pre_norm"]]
    x1, h2 = rows_fwd("postmix", fn_postmix, nb, S // tr, tr, pm_rows, pm_pb, pm_gl, [(D, F32), (D, BF16)])
    h22 = h2.reshape(nb * S, D)
    up = matmul("up_proj", [(h22, W["w_up"])], "nn", F32).reshape(nb, S, 2 * D_FF)
    act = glu_fwd(up, ffn_w8, V["ffn_conv_b"])
    act2 = act.reshape(nb * S, D_FF)
    ffn = matmul("down_proj", [(act2, W["w_down"])], "nn", F32).reshape(nb, S, D)
    dx1, dffn, dgate2, d_ffn_post, loss = final_call(x1, ffn, target, m[5], V["ffn_post_norm"], tr)

    dffn2 = dffn.reshape(nb * S, D)
    dact = matmul("down_dgrad", [(dffn2, W["w_down"])], "nt", BF16).reshape(nb, S, D_FF)
    g_down = matmul_tn("down_wgrad", act2, dffn2)
    dup, ffn_rows = glu_bwd(up, ffn_w8, V["ffn_conv_b"], dact)
    dup2 = dup.reshape(nb * S, 2 * D_FF)
    dh2 = matmul("up_dgrad", [(dup2, W["w_up"])], "nt", BF16).reshape(nb, S, D)
    g_up = matmul_tn("up_wgrad", h22, dup2)
    dx_a, dmix, dgate1, dscale2, dshift2, d_mix_post, d_ffn_pre = rows_bwd(
        "postmix_bwd", fn_postmix, nb, S // tr, tr, pm_rows, pm_pb, pm_gl,
        [(dx1, D, 0, 0), (dh2, D, 0, 0)], [(0, F32), (1, BF16)])
    dmix2 = dmix.reshape(nb * S, D)
    dcat = matmul("out_dgrad", [(dmix2, W["w_out_p"])], "nt", BF16).reshape(nb, S, QP + D_INNER)
    g_out_p = jnp.concatenate([matmul_tn("out_wgrad_attn", o2, dmix2), matmul_tn("out_wgrad_ssd", ssd2, dmix2)], axis=0)
    dy, dxs_direct, dz, d_dexp, d_ssd_norm = rows_bwd(
        "ssd_finish_bwd", fn_ssd_finish, nb, S // tr, tr, fin_rows, [], fin_gl,
        [(dcat, D_INNER, QP // D_INNER, 0)], [(0, F32), (2, F32), (3, BF16)])
    dq_pre, dkv, dkr = attn_bwd(q_raw, kv, kr, dcat, cos_q, sin_q, cosS, sinS, tq)
    dq_pre = dq_pre.reshape(nb * S, QP)
    dkr_pre = rope_call("rope_dk", dkr, LANE, 0, cosT, -sinT, BF16, tr)
    dkv2 = dkv.reshape(nb * T, QP)
    dqn = matmul("q_dgrad", [(dq_pre, W["w_q_p"])], "nt", F32).reshape(nb, S, Q_RANK)
    g_q_p = matmul_tn("q_wgrad", qn2, dq_pre)
    dkvn = matmul("kv_dgrad", [(dkv2, W["w_kv"])], "nt", F32).reshape(nb, T, KV_RANK)
    g_kv = matmul_tn("kv_wgrad", kvn2, dkv2)
    early_grads = (_out_grad(g_out_p), glu_deinterleave(g_up), g_down, _q_grad(g_q_p), g_kv)
    early = hosted_all_to_all([_per_device(g, n) for g, n in zip(early_grads, EARLY)])
    dxbc2, ddt2, ssd_stats, *received = ssd_bwd(xbc, dt2, alog2, dtb2, hin, dy, C, early)
    ddt_block = jnp.concatenate([ddt2[0][..., :SSD_HEADS], ddt2[1][..., :SSD_HEADS],
                                 jnp.zeros((nb, T, WIN_P - OFF_DT - 2 * SSD_HEADS), F32)], axis=-1).astype(BF16)
    dxbc_raw, ssd_rows = ssd_conv_bwd(u, ssd_w8, V["ssd_conv_b"], dxbc2, dxs_direct, C, tc)
    dcq, d_q_norm = rows_bwd("q_norm_bwd", fn_rms, nb, S // tr, tr, [(u, Q_RANK, OFF_CQ // Q_RANK, cblk)], [], [V["q_norm"]],
                             [(dqn, Q_RANK, 0, 0)], [(0, BF16)])
    dckv, d_kv_norm = rows_bwd("kv_norm_bwd", fn_rms, nb, T // tr, tr, [(u, KV_RANK, OFF_CKV // KV_RANK, 0)], [], [V["kv_norm"]],
                               [(dkvn, KV_RANK, 0, 0)], [(0, BF16)])

    def ctx_rows(t):
        return jnp.pad(t, ((0, 0), (C, 0), (0, 0)))

    du = jnp.concatenate([ctx_rows(dcq), dkr_pre, dckv, jnp.zeros((nb, T, OFF_Z - OFF_CKV - KV_RANK), BF16), ctx_rows(dz),
                          dxbc_raw, ddt_block], axis=-1).reshape(nb * T, WIN_P)
    g_in_p = matmul_tn("in_wgrad", h1, du)
    dh1, received_in = matmul("in_dgrad", [(du, W["w_in_p"])], "nt", BF16,
                              hosted=hosted_all_to_all([_per_device(_in_grad(g_in_p), "w_in").astype(BF16)]))
    dh1 = dh1.reshape(nb, T, D)

    def fn_prenorm_res(xv, shift, scale, g):
        return fn_prenorm(xv, shift, scale, g) + (xv,)

    grad_x, dshift1, dscale1, d_mix_pre_x = rows_bwd(
        "prenorm_x_bwd", fn_prenorm_res, nb, S // tr, tr, [(x, D, 0, 0)], [m[0], m[1]], [V["mix_pre_norm"]],
        [(dh1, D, 0, cblk), (dx_a, D, 0, 0)], [(0, F32)])
    dshift_c, dscale_c, d_mix_pre_c = rows_bwd(
        "prenorm_c_bwd", fn_prenorm, nb, C // tr, tr, [(ctx, D, 0, 0)], [], [mc[0], mc[1], V["mix_pre_norm"]],
        [(dh1, D, 0, 0)], [])

    dmod_x = jnp.concatenate([dshift1, dscale1, dgate1, dshift2, dscale2, dgate2], axis=-1).reshape(nb, N_MOD * D)
    dmod_c = jnp.concatenate([dshift_c, dscale_c, jnp.zeros((1, (N_MOD - 2) * D), F32)], axis=-1)
    gv = dict(
        mix_pre_norm=d_mix_pre_x + d_mix_pre_c, mix_post_norm=d_mix_post, q_norm=d_q_norm, kv_norm=d_kv_norm,
        ssd_conv_w=ssd_rows[:SSD_K], ssd_conv_b=ssd_rows[SSD_K:SSD_K + 1],
        ssd_a_log=jnp.concatenate([ssd_stats[0:1, :SSD_HEADS], ssd_stats[2:3, :SSD_HEADS]], axis=1),
        ssd_dt_bias=jnp.concatenate([ssd_stats[1:2, :SSD_HEADS], ssd_stats[3:4, :SSD_HEADS]], axis=1),
        ssd_d=jnp.sum(d_dexp.reshape(SSD_HEADS, SSD_P), axis=1).reshape(1, SSD_HEADS), ssd_norm=d_ssd_norm,
        ffn_pre_norm=d_ffn_pre, ffn_post_norm=d_ffn_post,
        ffn_conv_w=ffn_rows[:FFN_K], ffn_conv_b=ffn_rows[FFN_K:FFN_K + 1])
    return loss, grad_x, dmod_x, dmod_c, gv, dict(zip(EARLY, received), w_in=received_in)


WEIGHT_ORDER = ("c_ctx", "w_mod", "b_mod", "mix_pre_norm", "mix_post_norm", "w_in", "q_norm", "w_q_up", "kv_norm",
                "w_kv_up", "ssd_conv_w", "ssd_conv_b", "ssd_a_log", "ssd_dt_bias", "ssd_d", "ssd_norm", "w_out",
                "ffn_pre_norm", "ffn_post_norm", "w_up", "ffn_conv_w", "ffn_conv_b", "w_down")
MATRICES = ("w_in", "w_q_up", "w_kv_up", "w_out", "w_up", "w_down")
ROW_SHARDED = ("w_out", "w_down")
SMALL_SUMMED = ("c_ctx", "mix_pre_norm", "mix_post_norm", "q_norm", "kv_norm", "ssd_conv_w", "ssd_conv_b", "ssd_a_log",
                "ssd_dt_bias", "ssd_d", "ssd_norm", "ffn_pre_norm", "ffn_post_norm", "ffn_conv_w", "ffn_conv_b")
MOD_ROWS = 8


def _whole(shards, name):
    if name in ROW_SHARDED:
        return shards.reshape(-1, shards.shape[-1])
    return jnp.concatenate([shards[j] for j in range(N_DEV)], axis=1)


def _per_device(g, name):
    if name in ROW_SHARDED:
        return g.reshape(N_DEV, -1, g.shape[-1])
    return jnp.stack(jnp.split(g, N_DEV, axis=1))


def kernel(x, c, ctx, c_ctx, w_mod, b_mod, mix_pre_norm, mix_post_norm, w_in, q_norm, w_q_up, kv_norm, w_kv_up, ssd_conv_w, ssd_conv_b, ssd_a_log, ssd_dt_bias, ssd_d, ssd_norm, w_out, ffn_pre_norm, ffn_post_norm, w_up, ffn_conv_w, ffn_conv_b, w_down, loss_target, m_c_ctx, m_w_mod, m_b_mod, m_mix_pre_norm, m_mix_post_norm, m_w_in, m_q_norm, m_w_q_up, m_kv_norm, m_w_kv_up, m_ssd_conv_w, m_ssd_conv_b, m_ssd_a_log, m_ssd_dt_bias, m_ssd_d, m_ssd_norm, m_w_out, m_ffn_pre_norm, m_ffn_post_norm, m_w_up, m_ffn_conv_w, m_ffn_conv_b, m_w_down, v_c_ctx, v_w_mod, v_b_mod, v_mix_pre_norm, v_mix_post_norm, v_w_in, v_q_norm, v_w_q_up, v_kv_norm, v_w_kv_up, v_ssd_conv_w, v_ssd_conv_b, v_ssd_a_log, v_ssd_dt_bias, v_ssd_d, v_ssd_norm, v_w_out, v_ffn_pre_norm, v_ffn_post_norm, v_w_up, v_ffn_conv_w, v_ffn_conv_b, v_w_down):
    weights = dict(c_ctx=c_ctx, w_mod=w_mod, b_mod=b_mod, mix_pre_norm=mix_pre_norm, mix_post_norm=mix_post_norm, w_in=w_in, q_norm=q_norm, w_q_up=w_q_up, kv_norm=kv_norm, w_kv_up=w_kv_up, ssd_conv_w=ssd_conv_w, ssd_conv_b=ssd_conv_b, ssd_a_log=ssd_a_log, ssd_dt_bias=ssd_dt_bias, ssd_d=ssd_d, ssd_norm=ssd_norm, w_out=w_out, ffn_pre_norm=ffn_pre_norm, ffn_post_norm=ffn_post_norm, w_up=w_up, ffn_conv_w=ffn_conv_w, ffn_conv_b=ffn_conv_b, w_down=w_down)
    mom1 = dict(c_ctx=m_c_ctx, w_mod=m_w_mod, b_mod=m_b_mod, mix_pre_norm=m_mix_pre_norm, mix_post_norm=m_mix_post_norm, w_in=m_w_in, q_norm=m_q_norm, w_q_up=m_w_q_up, kv_norm=m_kv_norm, w_kv_up=m_w_kv_up, ssd_conv_w=m_ssd_conv_w, ssd_conv_b=m_ssd_conv_b, ssd_a_log=m_ssd_a_log, ssd_dt_bias=m_ssd_dt_bias, ssd_d=m_ssd_d, ssd_norm=m_ssd_norm, w_out=m_w_out, ffn_pre_norm=m_ffn_pre_norm, ffn_post_norm=m_ffn_post_norm, w_up=m_w_up, ffn_conv_w=m_ffn_conv_w, ffn_conv_b=m_ffn_conv_b, w_down=m_w_down)
    mom2 = dict(c_ctx=v_c_ctx, w_mod=v_w_mod, b_mod=v_b_mod, mix_pre_norm=v_mix_pre_norm, mix_post_norm=v_mix_post_norm, w_in=v_w_in, q_norm=v_q_norm, w_q_up=v_w_q_up, kv_norm=v_kv_norm, w_kv_up=v_w_kv_up, ssd_conv_w=v_ssd_conv_w, ssd_conv_b=v_ssd_conv_b, ssd_a_log=v_ssd_a_log, ssd_dt_bias=v_ssd_dt_bias, ssd_d=v_ssd_d, ssd_norm=v_ssd_norm, w_out=v_w_out, ffn_pre_norm=v_ffn_pre_norm, ffn_post_norm=v_ffn_post_norm, w_up=v_w_up, ffn_conv_w=v_ffn_conv_w, ffn_conv_b=v_ffn_conv_b, w_down=v_w_down)
    nb, S, D = x.shape
    me = 4 * lax.axis_index("x") + 2 * lax.axis_index("y") + lax.axis_index("c")

    *first, c_all, ssd_w_sh, ffn_w_sh = all_gather(
        "gather_first", [weights[n][0].astype(BF16) for n in FIRST] + [c, ssd_conv_w[0], ffn_conv_w[0]])
    W = first_weights_to_internal(*[_whole(s, n) for n, s in zip(FIRST, first)])
    late_shards = [weights[n][0].astype(BF16) for n in LATE_WEIGHTS]
    V = {n: weights[n].reshape(1, -1) for n in SMALL_SUMMED if n != "c_ctx"}
    V["ssd_conv_w"] = _whole(ssd_w_sh, "ssd_conv_w")
    V["ffn_conv_w"] = _whole(ffn_w_sh, "ffn_conv_w")

    n_all = N_DEV * nb
    mod_rows = -(-(n_all + 1) // 8) * 8
    c_pad = jnp.concatenate([c_all.reshape(n_all, D), c_ctx.reshape(1, D), jnp.zeros((mod_rows - n_all - 1, D), F32)], axis=0)
    mod_cols = w_mod.shape[2]
    b_mine = lax.dynamic_slice(b_mod, (0, me * mod_cols), (1, mod_cols))
    mod_part = matmul("mod_proj", [(c_pad, w_mod[0])], "nn", F32, bias=b_mine, silu_a=True)
    mod_all = _whole(all_gather("gather_mod", [mod_part])[0], "w_mod")
    mod_x = lax.dynamic_slice(mod_all, (me * nb, 0), (nb, mod_all.shape[1]))
    mod_c = mod_all[n_all:n_all + 1]

    loss, grad_x, dmod_x, dmod_c, gv, slots = local_step(x, ctx, loss_target, mod_x, mod_c, W, late_shards, V)

    dmod_mine = jnp.concatenate([dmod_x, dmod_c, jnp.zeros((MOD_ROWS - nb - 1, dmod_x.shape[1]), F32)], axis=0)
    dmod_all = all_gather("gather_dmod", [dmod_mine])[0]
    dmod_ctx = sum_slots("sum_dmod_ctx", dmod_all[:, nb:nb + 1].reshape(N_DEV, -1, LANE)).reshape(1, -1)
    dmod_full = jnp.concatenate([dmod_all[:, :nb].reshape(n_all, -1), dmod_ctx,
                                 jnp.zeros((mod_rows - n_all - 1, dmod_ctx.shape[1]), F32)], axis=0)
    (g_b_mod,) = ew_call("mod_bias_grad", lambda t: (jnp.sum(t, axis=0, keepdims=True),), [dmod_full], [((1, dmod_full.shape[1]), F32)])
    dmod_cols = lax.dynamic_slice(dmod_full, (0, me * mod_cols), (mod_rows, mod_cols))
    g_w_mod = matmul_tn("mod_wgrad", c_pad, dmod_cols, silu_a=True)
    dsilu_ctx = matmul("mod_dgrad_ctx", [(dmod_cols[n_all:n_all + 8], w_mod[0])], "nt", F32)[0:1]

    def silu_vjp(cc, ct):
        return (jax.vjp(_silu, cc)[1](ct)[0],)

    (g_c_ctx_part,) = ew_call("c_ctx_grad", silu_vjp, [c_ctx.reshape(1, D), dsilu_ctx], [((1, D), F32)])

    gv = dict(gv, c_ctx=g_c_ctx_part)
    small_parts = [loss] + [gv[n] for n in SMALL_SUMMED]
    small_sum = sum_slots("sum_small", all_gather("gather_small_grads", [_pack_rows(small_parts)])[0])
    summed = _unpack_rows(small_sum, [p.shape for p in small_parts])
    loss_out = summed[0][0, 0]
    grads = {n: g.reshape(weights[n].shape) if n not in ("ssd_conv_w", "ffn_conv_w") else g for n, g in zip(SMALL_SUMMED, summed[1:])}
    for n in ("ssd_conv_w", "ffn_conv_w"):
        cols = weights[n].shape[2]
        grads[n] = lax.dynamic_slice(grads[n], (0, me * cols), (grads[n].shape[0], cols)).reshape(weights[n].shape)
    grads["b_mod"] = g_b_mod.reshape(b_mod.shape)

    slots = dict(slots, w_mod=g_w_mod[None])
    delta, new_m, new_v = {}, {}, {}
    for n in MATRICES + ("w_mod",):
        g, d, mn, vn = adamw_matrix("adamw_" + n, weights[n][0], slots[n], mom1[n][0], mom2[n][0])
        grads[n], delta[n], new_m[n], new_v[n] = [t.reshape(weights[n].shape) for t in (g, d, mn, vn)]
    small = [n for n in WEIGHT_ORDER if n not in slots]

    def two_d(t):
        return t.reshape(-1, t.shape[-1])

    ds, ms, vs = adamw_small(*[[two_d(t[n]) for n in small] for t in (weights, grads, mom1, mom2)])
    for n, d, mn, vn in zip(small, ds, ms, vs):
        delta[n], new_m[n], new_v[n] = [t.reshape(weights[n].shape) for t in (d, mn, vn)]
    return (loss_out, grad_x, *[t[n] for t in (grads, delta, new_m, new_v) for n in WEIGHT_ORDER])
```

```python
import functools
import math

import jax
import jax.numpy as jnp
import numpy as np
from jax import lax
from jax.experimental import pallas as pl
from jax.experimental.pallas import tpu as pltpu

F32 = jnp.float32
BF16 = jnp.bfloat16
MESH = pl.DeviceIdType.MESH

D_MODEL = 1024
GRID_W = 64
N_HEADS = 16
NOPE = 64
ROPE = 32
V_DIM = 64
Q_RANK = 384
KV_RANK = 256
ROPE_THETA = 10000.0
ATTN_SCALE = (NOPE + ROPE) ** -0.5
SSD_HEADS = 16
SSD_P = 64
SSD_GROUPS = 2
SSD_N = 128
SSD_K = 5
CHUNK = 128
D_INNER = SSD_HEADS * SSD_P
GN = SSD_GROUPS * SSD_N
XBC = D_INNER + 2 * GN
D_FF = 2816
FFN_K = 3
N_MOD = 6
EPS = 1e-6
IN_SPLITS = (Q_RANK, KV_RANK, ROPE, D_INNER, XBC, 2 * SSD_HEADS)
IN_WIDTH = sum(IN_SPLITS)
N_DEV = 8

ADAM_LR = 0.001
ADAM_B1 = 0.9
ADAM_B2 = 0.999
ADAM_EPS = 1e-08
ADAM_WD = 0.01
ADAM_STEP = 10

LANE = 128
HEAD_BLOCK = 128
OFF_CQ = 0
OFF_KR = 384
OFF_CKV = 512
OFF_Z = 1024
OFF_XBC = 2048
OFF_DT = 3584
WIN_P = 3840
KR_LANE = 64
QP = N_HEADS * HEAD_BLOCK

VMEM_LIMIT_V7X = 56 * 1024 * 1024
NEG_BIG = -1e30


def _cparams(*sem):
    return pltpu.CompilerParams(dimension_semantics=sem, vmem_limit_bytes=VMEM_LIMIT_V7X)


def _tile(n, target, mult=128):
    if n <= target:
        return n
    t = (target // mult) * mult
    while t >= mult:
        if n % t == 0:
            return t
        t -= mult
    return n


def _silu(x):
    return x * jax.nn.sigmoid(x)


def _rms(x, g):
    return x * lax.rsqrt(jnp.mean(x * x, axis=-1, keepdims=True) + EPS) * g


WHOLE_K_WIDE = 2048


def matmul(name, pairs, mode, out_dtype, *, bias=None, silu_a=False, hosted=None):
    n_pairs = len(pairs)
    M = pairs[0][0].shape[0]
    N = pairs[0][1].shape[1] if mode == "nn" else pairs[0][1].shape[0]
    k_total = sum(a.shape[1] for a, _ in pairs)
    tm = _tile(M, 1024 if k_total <= WHOLE_K_WIDE else 512, 8)
    tn = _tile(N, 1408 if k_total <= WHOLE_K_WIDE else 512)
    dims = (((1,), (0,)), ((), ())) if mode == "nn" else (((1,), (1,)), ((), ()))
    n_own = 2 * n_pairs + (bias is not None)
    n_ex = hosted.n if hosted else 0

    def body(*refs):
        o_ref = refs[n_own + n_ex]
        if hosted:
            j, i = pl.program_id(0), pl.program_id(1)
            begin_exchange, end_exchange = hosted.steps(
                refs[n_own:n_own + n_ex], refs[n_own + n_ex + 1:n_own + 2 * n_ex + 1], refs[n_own + 2 * n_ex + 1:],
                jnp.logical_and(j == 0, i == 0), jnp.logical_and(j == N // tn - 1, i == M // tm - 1))
            begin_exchange()
        acc = None
        for p in range(n_pairs):
            a = refs[2 * p][...]
            if silu_a:
                a = _silu(a.astype(F32))
            d = lax.dot_general(a.astype(BF16), refs[2 * p + 1][...].astype(BF16), dims, preferred_element_type=F32)
            acc = d if acc is None else acc + d
        if bias is not None:
            acc = acc + refs[2 * n_pairs][...]
        o_ref[...] = acc.astype(o_ref.dtype)
        if hosted:
            end_exchange()

    in_specs, args = [], []
    for a, b in pairs:
        K = a.shape[1]
        in_specs.append(pl.BlockSpec((tm, K), lambda j, i: (i, 0)))
        in_specs.append(pl.BlockSpec((K, tn), lambda j, i: (0, j)) if mode == "nn" else pl.BlockSpec((tn, K), lambda j, i: (j, 0)))
        args += [a, b]
    if bias is not None:
        in_specs.append(pl.BlockSpec((1, tn), lambda j, i: (0, j)))
        args.append(bias)
    out_spec = pl.BlockSpec((tm, tn), lambda j, i: (i, j))
    out_shape = jax.ShapeDtypeStruct((M, N), out_dtype)
    if not hosted:
        return pl.pallas_call(
            body, name=name, grid=(N // tn, M // tm), in_specs=in_specs, out_specs=out_spec, out_shape=out_shape,
            compiler_params=_cparams("arbitrary", "arbitrary"),
        )(*args)
    return pl.pallas_call(
        body, name=name, grid=(N // tn, M // tm), in_specs=in_specs + hosted.specs,
        out_specs=[out_spec] + hosted.specs, out_shape=[out_shape] + hosted.out_shape, scratch_shapes=hosted.scratch,
        compiler_params=_cparams("arbitrary", "arbitrary"),
    )(*args, *hosted.arrays)


def matmul_tn(name, a, b, out_dtype=F32, *, silu_a=False, tm=1408, tn=512, tk=2048):
    R, M = a.shape
    N = b.shape[1]
    tm = _tile(M, tm)
    tn = _tile(N, tn)
    tk = _tile(R, tk, 8)
    nk = R // tk

    def body(a_ref, b_ref, o_ref, acc):
        k = pl.program_id(2)

        @pl.when(k == 0)
        def _():
            acc[...] = jnp.zeros_like(acc)

        x = a_ref[...]
        if silu_a:
            x = _silu(x.astype(F32))
        acc[...] += lax.dot_general(x.astype(BF16), b_ref[...].astype(BF16), (((0,), (0,)), ((), ())),
                                    preferred_element_type=F32)

        @pl.when(k == nk - 1)
        def _():
            o_ref[...] = acc[...].astype(o_ref.dtype)

    return pl.pallas_call(
        body, name=name, grid=(M // tm, N // tn, nk),
        in_specs=[pl.BlockSpec((tk, tm), lambda i, j, k: (k, i)), pl.BlockSpec((tk, tn), lambda i, j, k: (k, j))],
        out_specs=pl.BlockSpec((tm, tn), lambda i, j, k: (i, j)),
        out_shape=jax.ShapeDtypeStruct((M, N), out_dtype),
        scratch_shapes=[pltpu.VMEM((tm, tn), F32)],
        compiler_params=_cparams("arbitrary", "arbitrary", "arbitrary"),
    )(a, b)


def _row_specs(rin, pbin, glin, tr):
    specs = [pl.BlockSpec((1, tr, w), lambda b, i, cb=cb, ro=ro, bo=(e[4] if len(e) > 4 else 0): (b + bo, i + ro, cb))
             for e in rin for (_, w, cb, ro) in [e[:4]]]
    specs += [pl.BlockSpec((1, 1, a.shape[-1]), lambda b, i: (b, 0, 0)) for a in pbin]
    specs += [pl.BlockSpec((1, a.shape[-1]), lambda b, i: (0, 0)) for a in glin]
    return specs


def rows_fwd(name, fn, nb, nblk, tr, rin, pbin, glin, outs):
    nr, npb, ngl = len(rin), len(pbin), len(glin)
    n_in = nr + npb + ngl

    def body(*refs):
        args = [r[0].astype(F32) for r in refs[:nr + npb]] + [r[...] for r in refs[nr + npb:n_in]]
        res = fn(*args)
        for o, v in zip(refs[n_in:], res):
            o[0] = v.astype(o.dtype)

    return pl.pallas_call(
        body, name=name, grid=(nb, nblk), in_specs=_row_specs(rin, pbin, glin, tr),
        out_specs=[pl.BlockSpec((1, tr, w), lambda b, i: (b, i, 0)) for (w, _) in outs],
        out_shape=[jax.ShapeDtypeStruct((nb, nblk * tr, w), dt) for (w, dt) in outs],
        compiler_params=_cparams("arbitrary", "arbitrary"),
    )(*[e[0] for e in rin], *pbin, *glin)


def rows_bwd(name, fn, nb, nblk, tr, rin, pbin, glin, cts, want):
    nr, npb, ngl, nct = len(rin), len(pbin), len(glin), len(cts)
    n_in = nr + npb + ngl

    def body(*refs):
        b, i = pl.program_id(0), pl.program_id(1)
        args = [r[0].astype(F32) for r in refs[:nr + npb]] + [r[...] for r in refs[nr + npb:n_in]]
        ct = tuple(r[0].astype(F32) for r in refs[n_in:n_in + nct])
        _, vjp = jax.vjp(fn, *args)
        g = vjp(ct)
        orefs = refs[n_in + nct:]
        for o, (idx, _) in zip(orefs, want):
            o[0] = g[idx].astype(o.dtype)
        pb_refs = orefs[len(want):len(want) + npb]
        gl_refs = orefs[len(want) + npb:]

        @pl.when(i == 0)
        def _():
            for o, v in zip(pb_refs, g[nr:nr + npb]):
                o[0] = v

        @pl.when(i > 0)
        def _():
            for o, v in zip(pb_refs, g[nr:nr + npb]):
                o[0] += v

        first = jnp.logical_and(b == 0, i == 0)

        @pl.when(first)
        def _():
            for o, v in zip(gl_refs, g[nr + npb:]):
                o[...] = v

        @pl.when(jnp.logical_not(first))
        def _():
            for o, v in zip(gl_refs, g[nr + npb:]):
                o[...] += v

    out_specs = [pl.BlockSpec((1, tr, rin[idx][1]), lambda b, i: (b, i, 0)) for (idx, _) in want]
    out_shape = [jax.ShapeDtypeStruct((nb, nblk * tr, rin[idx][1]), dt) for (idx, dt) in want]
    out_specs += [pl.BlockSpec((1, 1, a.shape[-1]), lambda b, i: (b, 0, 0)) for a in pbin]
    out_shape += [jax.ShapeDtypeStruct((nb, 1, a.shape[-1]), F32) for a in pbin]
    out_specs += [pl.BlockSpec((1, a.shape[-1]), lambda b, i: (0, 0)) for a in glin]
    out_shape += [jax.ShapeDtypeStruct((1, a.shape[-1]), F32) for a in glin]
    return pl.pallas_call(
        body, name=name, grid=(nb, nblk),
        in_specs=_row_specs(rin, pbin, glin, tr) + _row_specs(cts, [], [], tr),
        out_specs=out_specs, out_shape=out_shape,
        compiler_params=_cparams("arbitrary", "arbitrary"),
    )(*[e[0] for e in rin], *pbin, *glin, *[e[0] for e in cts])


def ew_call(name, fn, ins, outs):
    def body(*refs):
        res = fn(*[r[...] for r in refs[:len(ins)]])
        for o, v in zip(refs[len(ins):], res):
            o[...] = v.astype(o.dtype)

    return pl.pallas_call(body, name=name, out_shape=[jax.ShapeDtypeStruct(s, dt) for (s, dt) in outs])(*ins)


def fn_prenorm(x, shift, scale, g):
    return (_rms(x, g) * (1.0 + scale) + shift,)


def fn_rms(x, g):
    return (_rms(x, g),)


def fn_ssd_finish(yf, yr, xs, z, dexp, nw):
    y = yf + yr + dexp * xs
    return (_rms(y * _silu(z), nw),)


def fn_postmix(x, mix, gate1, scale2, shift2, post_g, pre_g):
    x1 = x + gate1 * _rms(mix, post_g)
    h2 = _rms(x1, pre_g) * (1.0 + scale2) + shift2
    return x1, h2


def final_call(x1, ffn, target, gate2, post_g, tr):
    nb, S, D = x1.shape
    nblk = S // tr

    def body(x1_ref, f_ref, t_ref, g2_ref, pg_ref, dx1_ref, df_ref, dg2_ref, dpg_ref, loss_ref):
        b, i = pl.program_id(0), pl.program_id(1)
        tgt = t_ref[0]

        def lossfn(x1v, fv, g2, pg):
            e = x1v + g2 * _rms(fv, pg) - tgt
            return 0.5 * jnp.sum(jnp.mean(e * e, axis=-1, keepdims=True))

        val, (dx1, df, dg2, dpg) = jax.value_and_grad(lossfn, argnums=(0, 1, 2, 3))(
            x1_ref[0], f_ref[0].astype(F32), g2_ref[0], pg_ref[...])
        dx1_ref[0] = dx1
        df_ref[0] = df.astype(df_ref.dtype)
        lv = jnp.full((1, LANE), val, F32)

        @pl.when(i == 0)
        def _():
            dg2_ref[0] = dg2

        @pl.when(i > 0)
        def _():
            dg2_ref[0] += dg2

        first = jnp.logical_and(b == 0, i == 0)

        @pl.when(first)
        def _():
            dpg_ref[...] = dpg
            loss_ref[...] = lv

        @pl.when(jnp.logical_not(first))
        def _():
            dpg_ref[...] += dpg
            loss_ref[...] += lv

    row = pl.BlockSpec((1, tr, D), lambda b, i: (b, i, 0))
    pb = pl.BlockSpec((1, 1, D), lambda b, i: (b, 0, 0))
    gl = pl.BlockSpec((1, D), lambda b, i: (0, 0))
    return pl.pallas_call(
        body, name="loss_head", grid=(nb, nblk), in_specs=[row, row, row, pb, gl],
        out_specs=[row, row, pb, gl, pl.BlockSpec((1, LANE), lambda b, i: (0, 0))],
        out_shape=[jax.ShapeDtypeStruct((nb, S, D), F32), jax.ShapeDtypeStruct((nb, S, D), BF16),
                   jax.ShapeDtypeStruct((nb, 1, D), F32), jax.ShapeDtypeStruct((1, D), F32),
                   jax.ShapeDtypeStruct((1, LANE), F32)],
        compiler_params=_cparams("arbitrary", "arbitrary"),
    )(x1, ffn, target, gate2, post_g)


def _rotate_half(t):
    lane = lax.broadcasted_iota(jnp.int32, t.shape, 1)
    return jnp.where((lane & 15) < 8, -pltpu.roll(t, LANE - 8, 1), pltpu.roll(t, 8, 1))


def rope_call(name, x, width, colblk, cos, sin, out_dtype, tr):
    nb = x.shape[0]
    R = cos.shape[0]
    nblk = R // tr

    def body(x_ref, c_ref, s_ref, o_ref):
        c, s = c_ref[...], s_ref[...]
        for h in range(width // LANE):
            t = x_ref[0, :, h * LANE:(h + 1) * LANE].astype(F32)
            o_ref[0, :, h * LANE:(h + 1) * LANE] = (t * c + _rotate_half(t) * s).astype(o_ref.dtype)

    tab = pl.BlockSpec((tr, LANE), lambda b, i: (i, 0))
    return pl.pallas_call(
        body, name=name, grid=(nb, nblk),
        in_specs=[pl.BlockSpec((1, tr, width), lambda b, i: (b, i, colblk)), tab, tab],
        out_specs=pl.BlockSpec((1, tr, width), lambda b, i: (b, i, 0)),
        out_shape=jax.ShapeDtypeStruct((nb, R, width), out_dtype),
        compiler_params=_cparams("arbitrary", "arbitrary"),
    )(x, cos, sin)


def rope_tables(n_ctx, seq):
    n_rows = seq // GRID_W
    row = np.repeat(np.arange(n_rows), GRID_W).astype(np.float32)
    col = np.tile(np.arange(GRID_W), n_rows).astype(np.float32)
    axis_dim = ROPE // 2
    inv_freq = jnp.asarray(ROPE_THETA, F32) ** (-jnp.arange(0, axis_dim, 2, dtype=F32) / axis_dim)
    ang_r = jnp.asarray(row)[:, None] * inv_freq
    ang_c = jnp.asarray(col)[:, None] * inv_freq
    ang = jnp.concatenate([ang_r, ang_r, ang_c, ang_c], axis=-1)
    cos = jnp.ones((n_ctx + seq, LANE), F32).at[n_ctx:, KR_LANE:KR_LANE + ROPE].set(jnp.cos(ang))
    sin = jnp.zeros((n_ctx + seq, LANE), F32).at[n_ctx:, KR_LANE:KR_LANE + ROPE].set(jnp.sin(ang))
    return cos, sin


Q_PRESCALE = ATTN_SCALE * math.log2(math.e)


def _attn_weights(q, kc):
    s2 = lax.dot_general(q, kc, (((1,), (1,)), ((), ())), preferred_element_type=F32)
    e = jnp.exp2(s2 - jnp.max(s2, axis=1, keepdims=True))
    return e, 1.0 / jnp.sum(e, axis=1, keepdims=True)


def _key_block(kv, kr):
    lane = lax.broadcasted_iota(jnp.int32, kv.shape, 1)
    return jnp.where(lane < NOPE, kv, kr)


def _rotated_query(q_ref, cos_ref, sin_ref):
    t = q_ref[0].astype(F32)
    return (t * cos_ref[...] + _rotate_half(t) * sin_ref[...]).astype(BF16)


def attn_fwd(q_raw, kv, kr, cos_q, sin_q, tq):
    nb, S, _ = q_raw.shape
    T = kv.shape[1]

    def body(q_ref, kv_ref, kr_ref, c_ref, s_ref, o_ref):
        kvv = kv_ref[0]
        e, r = _attn_weights(_rotated_query(q_ref, c_ref, s_ref), _key_block(kvv, kr_ref[0]))
        o = lax.dot_general(e.astype(BF16), kvv, (((1,), (0,)), ((), ())), preferred_element_type=F32) * r
        lane = lax.broadcasted_iota(jnp.int32, o.shape, 1)
        o_ref[0] = jnp.where(lane >= NOPE, o, 0.0).astype(o_ref.dtype)

    return pl.pallas_call(
        body, name="attn_fwd", grid=(nb, N_HEADS, S // tq),
        in_specs=[pl.BlockSpec((1, tq, HEAD_BLOCK), lambda b, h, i: (b, i, h)),
                  pl.BlockSpec((1, T, HEAD_BLOCK), lambda b, h, i: (b, 0, h)),
                  pl.BlockSpec((1, T, HEAD_BLOCK), lambda b, h, i: (b, 0, 0)),
                  pl.BlockSpec((tq, LANE), lambda b, h, i: (i, 0)), pl.BlockSpec((tq, LANE), lambda b, h, i: (i, 0))],
        out_specs=pl.BlockSpec((1, tq, HEAD_BLOCK), lambda b, h, i: (b, i, h)),
        out_shape=jax.ShapeDtypeStruct((nb, S, QP), BF16),
        compiler_params=_cparams("arbitrary", "arbitrary", "arbitrary"),
    )(q_raw, kv, kr, cos_q, sin_q)


def attn_bwd(q_raw, kv, kr, do, cos_q, sin_q, cos, sin, tq):
    nb, S, _ = q_raw.shape
    T = kv.shape[1]

    def body(q_ref, kv_ref, kr_ref, do_ref, cq_ref, sq_ref, c_ref, s_ref, dq_ref, dkv_ref, dkr_ref):
        h, i = pl.program_id(1), pl.program_id(2)

        @pl.when(i == 0)
        def _():
            dkv_ref[...] = jnp.zeros_like(dkv_ref)

        @pl.when(jnp.logical_and(h == 0, i == 0))
        def _():
            dkr_ref[...] = jnp.zeros_like(dkr_ref)

        qv, kvv, dov = _rotated_query(q_ref, cq_ref, sq_ref), kv_ref[0], do_ref[0]
        kc = _key_block(kvv, kr_ref[0])
        e, r = _attn_weights(qv, kc)
        dor = (dov.astype(F32) * r).astype(BF16)
        dpr = lax.dot_general(dor, kvv, (((1,), (1,)), ((), ())), preferred_element_type=F32)
        ds = (e * (dpr - r * jnp.sum(dpr * e, axis=1, keepdims=True))).astype(BF16)
        dq = lax.dot_general(ds, kc, (((1,), (0,)), ((), ())), preferred_element_type=F32) * ATTN_SCALE
        dq_ref[0] = (dq * c_ref[...] - _rotate_half(dq) * s_ref[...]).astype(dq_ref.dtype)
        dkc = lax.dot_general(ds, qv, (((0,), (0,)), ((), ())), preferred_element_type=F32) * math.log(2.0)
        dv = lax.dot_general(e.astype(BF16), dor, (((0,), (0,)), ((), ())), preferred_element_type=F32)
        lane = lax.broadcasted_iota(jnp.int32, dkc.shape, 1)
        dkv_ref[0] += jnp.where(lane < NOPE, dkc, dv)
        dkr_ref[0] += jnp.where(lane >= NOPE, dkc, 0.0)

    qspec = pl.BlockSpec((1, tq, HEAD_BLOCK), lambda b, h, i: (b, i, h))
    kspec = pl.BlockSpec((1, T, HEAD_BLOCK), lambda b, h, i: (b, 0, h))
    rspec = pl.BlockSpec((1, T, HEAD_BLOCK), lambda b, h, i: (b, 0, 0))
    tab = pl.BlockSpec((tq, LANE), lambda b, h, i: (i, 0))
    return pl.pallas_call(
        body, name="attn_bwd", grid=(nb, N_HEADS, S // tq),
        in_specs=[qspec, kspec, rspec, qspec, tab, tab, tab, tab], out_specs=[qspec, kspec, rspec],
        out_shape=[jax.ShapeDtypeStruct((nb, S, QP), BF16), jax.ShapeDtypeStruct((nb, T, QP), F32),
                   jax.ShapeDtypeStruct((nb, T, HEAD_BLOCK), F32)],
        compiler_params=_cparams("arbitrary", "arbitrary", "arbitrary"),
    )(q_raw, kv, kr, do, cos_q, sin_q, cos, sin)


CONV_HALO = 8


def _segments(n, n_ctx):
    if n_ctx == 0:
        return [(0, n, CONV_HALO)]
    return [(0, n_ctx, CONV_HALO), (n_ctx, n - n_ctx, 2 * CONV_HALO + n_ctx)]


def _halo_scratch(n, n_ctx, tc):
    return pltpu.VMEM((n + CONV_HALO * (len(_segments(n, n_ctx)) + 1), tc), F32)


def _zero_halos(scr, segs):
    z = jnp.zeros((CONV_HALO, scr.shape[1]), scr.dtype)
    scr[0:CONV_HALO, :] = z
    for (_, rows, off) in segs:
        scr[off + rows:off + rows + CONV_HALO, :] = z


CONV_BLOCK = 64


def _window(scr, off, r0):
    return scr[pl.ds(pl.multiple_of(off - CONV_HALO + r0, 8), CONV_BLOCK + 2 * CONV_HALO), :]


def _shifted(win, s):
    v = win if s == 0 else pltpu.roll(win, (-s) % win.shape[0], 0)
    return v[CONV_HALO:CONV_HALO + CONV_BLOCK]


def _taps(win, w, k, sign):
    acc = None
    for o in range(k):
        t = w[o:o + 1, :] * _shifted(win, sign * (o - k // 2))
        acc = t if acc is None else acc + t
    return acc


def _tap_grads(xwin, dpre, k):
    sub8 = lax.broadcasted_iota(jnp.int32, (8, dpre.shape[1]), 0)
    out = jnp.where(sub8 == k, jnp.sum(dpre, axis=0, keepdims=True), 0.0)
    for o in range(k):
        out = out + jnp.where(sub8 == o, jnp.sum(dpre * _shifted(xwin, o - k // 2), axis=0, keepdims=True), 0.0)
    return out


def _row_blocks(rows, fn, init=0):
    return lax.fori_loop(0, rows // CONV_BLOCK, lambda i, c: fn(pl.multiple_of(i * CONV_BLOCK, CONV_BLOCK), c), init)


def _gelu(x):
    return 0.5 * x * (1.0 + lax.erf(x * (1.0 / math.sqrt(2.0))))


def _gelu_grad(x):
    return 0.5 * (1.0 + lax.erf(x * (1.0 / math.sqrt(2.0)))) + x * jnp.exp(-0.5 * x * x) * (1.0 / math.sqrt(2.0 * math.pi))


def ssd_conv_fwd(u, w8, bias, n_ctx, tc):
    nb, T, _ = u.shape
    cb0 = OFF_XBC // tc

    segs = _segments(T, n_ctx)

    def body(x_ref, w_ref, b_ref, o_ref, xs):
        _zero_halos(xs, segs)
        for (start, rows, off) in segs:
            xs[off:off + rows, :] = x_ref[0, start:start + rows, :]
        w, bias_v = w_ref[...], b_ref[...]
        for (start, rows, off) in segs:
            def block(r0, carry, start=start, off=off):
                o_ref[0, pl.ds(pl.multiple_of(start + r0, CONV_BLOCK), CONV_BLOCK), :] = _silu(bias_v + _taps(_window(xs, off, r0), w, SSD_K, 1))
                return carry

            _row_blocks(rows, block)

    return pl.pallas_call(
        body, name="ssd_conv_fwd", grid=(nb, XBC // tc),
        in_specs=[pl.BlockSpec((1, T, tc), lambda b, j: (b, 0, cb0 + j)),
                  pl.BlockSpec((8, tc), lambda b, j: (0, j)), pl.BlockSpec((1, tc), lambda b, j: (0, j))],
        out_specs=pl.BlockSpec((1, T, tc), lambda b, j: (b, 0, j)),
        out_shape=jax.ShapeDtypeStruct((nb, T, XBC), F32),
        scratch_shapes=[_halo_scratch(T, n_ctx, tc)],
        compiler_params=_cparams("arbitrary", "arbitrary"),
    )(u, w8, bias)


def ssd_conv_bwd(u, w8, bias, dxbc, dxs_direct, n_ctx, tc):
    nb, T, _ = u.shape
    cb0 = OFF_XBC // tc
    n_direct = D_INNER // tc

    segs = _segments(T, n_ctx)

    def body(x_ref, w_ref, b_ref, d0_ref, d1_ref, dd_ref, dx_ref, dw_ref, xs, ds):
        j, b = pl.program_id(0), pl.program_id(1)
        _zero_halos(xs, segs)
        _zero_halos(ds, segs)
        for (start, rows, off) in segs:
            xs[off:off + rows, :] = x_ref[0, start:start + rows, :]
        w, bias_v = w_ref[...], b_ref[...]
        has_direct = (j < n_direct).astype(F32)
        rows = jnp.zeros((8, tc), F32)
        for (start, n_rows, off) in segs:
            def block(r0, acc, start=start, off=off):
                xwin = _window(xs, off, r0)
                pre = bias_v + _taps(xwin, w, SSD_K, 1)
                d = d0_ref[0, 0, pl.ds(pl.multiple_of(start + r0, CONV_BLOCK), CONV_BLOCK), :] + d1_ref[0, 0, pl.ds(pl.multiple_of(start + r0, CONV_BLOCK), CONV_BLOCK), :]
                if start == n_ctx:
                    d = d + dd_ref[0, pl.ds(r0, CONV_BLOCK), :] * has_direct
                sg = jax.nn.sigmoid(pre)
                dpre = d * (sg * (1.0 + pre * (1.0 - sg)))
                ds[pl.ds(pl.multiple_of(off + r0, 8), CONV_BLOCK), :] = dpre
                return acc + _tap_grads(xwin, dpre, SSD_K)

            rows = _row_blocks(n_rows, block, rows)
        for (start, n_rows, off) in segs:
            def block_dx(r0, carry, start=start, off=off):
                dx_ref[0, pl.ds(pl.multiple_of(start + r0, CONV_BLOCK), CONV_BLOCK), :] = _taps(_window(ds, off, r0), w, SSD_K, -1).astype(dx_ref.dtype)
                return carry

            _row_blocks(n_rows, block_dx)

        @pl.when(b == 0)
        def _():
            dw_ref[...] = rows

        @pl.when(b > 0)
        def _():
            dw_ref[...] += rows

    dspec0 = pl.BlockSpec((1, 1, T, tc), lambda j, b: (0, b, 0, j))
    dspec1 = pl.BlockSpec((1, 1, T, tc), lambda j, b: (1, b, 0, j))
    return pl.pallas_call(
        body, name="ssd_conv_bwd", grid=(XBC // tc, nb),
        in_specs=[pl.BlockSpec((1, T, tc), lambda j, b: (b, 0, cb0 + j)),
                  pl.BlockSpec((8, tc), lambda j, b: (0, j)), pl.BlockSpec((1, tc), lambda j, b: (0, j)),
                  dspec0, dspec1,
                  pl.BlockSpec((1, T - n_ctx, tc), lambda j, b: (b, 0, jnp.minimum(j, n_direct - 1)))],
        out_specs=[pl.BlockSpec((1, T, tc), lambda j, b: (b, 0, j)), pl.BlockSpec((8, tc), lambda j, b: (0, j))],
        out_shape=[jax.ShapeDtypeStruct((nb, T, XBC), BF16), jax.ShapeDtypeStruct((8, XBC), F32)],
        scratch_shapes=[_halo_scratch(T, n_ctx, tc), _halo_scratch(T, n_ctx, tc)],
        compiler_params=_cparams("arbitrary", "arbitrary"),
    )(u, w8, bias, dxbc, dxbc, dxs_direct)


GLU_TC = 256


def glu_interleave(w_up):
    blocks = []
    for j in range(D_FF // GLU_TC):
        blocks += [w_up[:, j * GLU_TC:(j + 1) * GLU_TC], w_up[:, D_FF + j * GLU_TC:D_FF + (j + 1) * GLU_TC]]
    return jnp.concatenate(blocks, axis=1)


def glu_deinterleave(g):
    nj = D_FF // GLU_TC
    gate = [g[:, 2 * j * GLU_TC:(2 * j + 1) * GLU_TC] for j in range(nj)]
    val = [g[:, (2 * j + 1) * GLU_TC:(2 * j + 2) * GLU_TC] for j in range(nj)]
    return jnp.concatenate(gate + val, axis=1)


def glu_fwd(up, w8, bias):
    nb, S, _ = up.shape
    tc = GLU_TC

    segs = _segments(S, 0)
    (_, _, off), = segs

    def body(u_ref, w_ref, b_ref, o_ref, xs):
        _zero_halos(xs, segs)
        xs[off:off + S, :] = u_ref[0, :, :tc]
        w, bias_v = w_ref[...], b_ref[...]

        def block(r0, carry):
            gc = bias_v + _taps(_window(xs, off, r0), w, FFN_K, 1)
            o_ref[0, pl.ds(r0, CONV_BLOCK), :] = (_gelu(gc) * u_ref[0, pl.ds(r0, CONV_BLOCK), tc:]).astype(o_ref.dtype)
            return carry

        _row_blocks(S, block)

    return pl.pallas_call(
        body, name="glu_fwd", grid=(nb, D_FF // tc),
        in_specs=[pl.BlockSpec((1, S, 2 * tc), lambda b, j: (b, 0, j)),
                  pl.BlockSpec((8, tc), lambda b, j: (0, j)), pl.BlockSpec((1, tc), lambda b, j: (0, j))],
        out_specs=pl.BlockSpec((1, S, tc), lambda b, j: (b, 0, j)),
        out_shape=jax.ShapeDtypeStruct((nb, S, D_FF), BF16),
        scratch_shapes=[_halo_scratch(S, 0, tc)],
        compiler_params=_cparams("arbitrary", "arbitrary"),
    )(up, w8, bias)


def glu_bwd(up, w8, bias, dact):
    nb, S, _ = up.shape
    tc = GLU_TC

    segs = _segments(S, 0)
    (_, _, off), = segs

    def body(u_ref, w_ref, b_ref, d_ref, du_ref, dw_ref, xs, ds):
        b = pl.program_id(1)
        _zero_halos(xs, segs)
        _zero_halos(ds, segs)
        xs[off:off + S, :] = u_ref[0, :, :tc]
        w, bias_v = w_ref[...], b_ref[...]

        def block(r0, acc):
            here = pl.ds(r0, CONV_BLOCK)
            xwin = _window(xs, off, r0)
            gc = bias_v + _taps(xwin, w, FFN_K, 1)
            d = d_ref[0, here, :].astype(F32)
            du_ref[0, here, tc:] = (d * _gelu(gc)).astype(du_ref.dtype)
            dpre = d * u_ref[0, here, tc:] * _gelu_grad(gc)
            ds[pl.ds(pl.multiple_of(off + r0, 8), CONV_BLOCK), :] = dpre
            return acc + _tap_grads(xwin, dpre, FFN_K)

        rows = _row_blocks(S, block, jnp.zeros((8, tc), F32))

        def block_dx(r0, carry):
            du_ref[0, pl.ds(r0, CONV_BLOCK), :tc] = _taps(_window(ds, off, r0), w, FFN_K, -1).astype(du_ref.dtype)
            return carry

        _row_blocks(S, block_dx)

        @pl.when(b == 0)
        def _():
            dw_ref[...] = rows

        @pl.when(b > 0)
        def _():
            dw_ref[...] += rows

    pair = pl.BlockSpec((1, S, 2 * tc), lambda j, b: (b, 0, j))
    return pl.pallas_call(
        body, name="glu_bwd", grid=(D_FF // tc, nb),
        in_specs=[pair, pl.BlockSpec((8, tc), lambda j, b: (0, j)), pl.BlockSpec((1, tc), lambda j, b: (0, j)),
                  pl.BlockSpec((1, S, tc), lambda j, b: (b, 0, j))],
        out_specs=[pair, pl.BlockSpec((8, tc), lambda j, b: (0, j))],
        out_shape=[jax.ShapeDtypeStruct((nb, S, 2 * D_FF), BF16), jax.ShapeDtypeStruct((8, D_FF), F32)],
        scratch_shapes=[_halo_scratch(S, 0, tc), _halo_scratch(S, 0, tc)],
        compiler_params=_cparams("arbitrary", "arbitrary"),
    )(up, w8, bias, dact)


def _chunk_of(d, k, n_cc, n_ch):
    rev = jnp.where(k < n_cc, n_cc - 1 - k, n_cc + n_ch - 1 - k)
    return jnp.where(d == 1, rev, k)


def _lane_pick(v, lane_iota, l):
    return jnp.sum(jnp.where(lane_iota == l, v, 0.0), axis=1, keepdims=True)


def head_spread_matrix():
    return (jnp.arange(LANE)[:, None] == (jnp.arange(D_INNER)[None, :] // SSD_P)).astype(BF16)


def _split_dot(x, e, dims):
    hi = x.astype(BF16)
    lo = (x - hi.astype(F32)).astype(BF16)
    return (lax.dot_general(hi, e, dims, preferred_element_type=F32)
            + lax.dot_general(lo, e, dims, preferred_element_type=F32))


def _spread(x, e):
    return _split_dot(x, e, (((1,), (0,)), ((), ())))


def _gather_heads(y, e):
    return _split_dot(y, e, (((1,), (1,)), ((), ())))


def _softplus(x):
    return jnp.maximum(x, 0.0) + jnp.log(1.0 + jnp.exp(-jnp.abs(x)))


def ssd_dt_inputs(u, a_log, dt_bias):
    pad = LANE - SSD_HEADS
    dt = u[..., OFF_DT:OFF_DT + 2 * SSD_HEADS]
    dt2 = jnp.stack([jnp.pad(dt[..., i * SSD_HEADS:(i + 1) * SSD_HEADS], ((0, 0), (0, 0), (0, pad))) for i in range(2)])

    def lanes(v):
        return jnp.pad(v.reshape(2, 1, SSD_HEADS), ((0, 0), (0, 0), (0, pad)))

    return dt2, lanes(a_log), lanes(dt_bias)


def _ssd_common(d, dt_raw, alog, dtb):
    Q = dt_raw.shape[0]
    row = lax.broadcasted_iota(jnp.int32, (Q, Q), 0)
    col = lax.broadcasted_iota(jnp.int32, (Q, Q), 1)
    rev = d == 1
    maskb = jnp.where(rev, row, col) <= jnp.where(rev, col, row)
    tri = maskb.astype(F32)
    A = -jnp.exp(alog)
    dtv = _softplus(dt_raw + dtb)
    a = dtv * A
    cum = lax.dot_general(tri, a, (((1,), (0,)), ((), ())), precision=lax.Precision.HIGHEST, preferred_element_type=F32)
    tot = jnp.sum(a, axis=0, keepdims=True)
    return maskb, tri, A, dtv, cum, tot


def ssd_fwd(xbc, dt2, alog2, dtb2, n_ctx, hosted):
    nb, T, _ = xbc.shape
    S = T - n_ctx
    n_ch, n_cc = T // CHUNK, n_ctx // CHUNK
    Q = CHUNK
    n_pairs = SSD_HEADS // 2
    n_ex = hosted.n
    n_in = 5

    def body(*refs):
        x_ref, dt_ref, al_ref, db_ref, e_ref = refs[:n_in]
        send_refs = refs[n_in:n_in + n_ex]
        y_ref, hin_ref = refs[n_in + n_ex:n_in + 2 + n_ex]
        recv_refs = refs[n_in + 2 + n_ex:n_in + 2 + 2 * n_ex]
        H, *sems = refs[n_in + 2 + 2 * n_ex:]
        d, k = pl.program_id(1), pl.program_id(2)
        first_step = jnp.logical_and(jnp.logical_and(pl.program_id(0) == 0, d == 0), k == 0)
        last_step = jnp.logical_and(jnp.logical_and(pl.program_id(0) == nb - 1, d == 1), k == n_ch - 1)
        begin_exchange, end_exchange = hosted.steps(send_refs, recv_refs, sems, first_step, last_step)
        begin_exchange()

        @pl.when(k == 0)
        def _():
            H[...] = jnp.zeros_like(H)

        maskb, tri, A, dtv, cum, tot = _ssd_common(d, dt_ref[0, 0], al_ref[0], db_ref[0])
        e = e_ref[...]
        cumT = cum.T
        cum_e, dt_e = _spread(cum, e), _spread(dtv, e)
        tot_e = _spread(jnp.broadcast_to(tot, (8, LANE)), e)[0:1]
        hin_ref[0, 0, 0] = H[...].astype(BF16)
        lane = lax.broadcasted_iota(jnp.int32, (Q, LANE), 1)
        lane1 = lax.broadcasted_iota(jnp.int32, (1, LANE), 1)
        subc = lax.broadcasted_iota(jnp.int32, (LANE, 1), 0)
        half = lane < SSD_P
        for g in range(SSD_GROUPS):
            Bg = x_ref[0, :, D_INNER + g * SSD_N:D_INNER + (g + 1) * SSD_N].astype(BF16)
            Cg = x_ref[0, :, D_INNER + GN + g * SSD_N:D_INNER + GN + (g + 1) * SSD_N].astype(BF16)
            Gm = lax.dot_general(Cg, Bg, (((1,), (1,)), ((), ())), preferred_element_type=F32)
            for pr in range(n_pairs // SSD_GROUPS):
                p = g * (n_pairs // SSD_GROUPS) + pr
                sc, dtp, totp = [t[:, p * LANE:(p + 1) * LANE] for t in (cum_e, dt_e, tot_e)]
                swapped = pltpu.roll(sc, SSD_P, 1)
                s0c, s1c = jnp.where(half, sc, swapped), jnp.where(half, swapped, sc)
                s0r, s1r = cumT[2 * p:2 * p + 1, :], cumT[2 * p + 1:2 * p + 2, :]
                tot0, tot1 = _lane_pick(tot, lane1, 2 * p), _lane_pick(tot, lane1, 2 * p + 1)
                M0 = (Gm * jnp.exp(jnp.where(maskb, s0c - s0r, NEG_BIG))).astype(BF16)
                M1 = (Gm * jnp.exp(jnp.where(maskb, s1c - s1r, NEG_BIG))).astype(BF16)
                xd = x_ref[0, :, p * LANE:(p + 1) * LANE] * dtp
                xdb = xd.astype(BF16)
                yd = jnp.where(half,
                               lax.dot_general(M0, xdb, (((1,), (0,)), ((), ())), preferred_element_type=F32),
                               lax.dot_general(M1, xdb, (((1,), (0,)), ((), ())), preferred_element_type=F32))
                Hp = H[p * LANE:(p + 1) * LANE, :]
                yo = lax.dot_general(Cg, Hp.astype(BF16), (((1,), (1,)), ((), ())), preferred_element_type=F32) * jnp.exp(sc)

                y_ref[0, 0, :, p * LANE:(p + 1) * LANE] = yd + yo

                xdw = (xd * jnp.exp(totp - sc)).astype(BF16)
                etot = jnp.exp(jnp.where(subc < SSD_P, tot0, tot1))
                H[p * LANE:(p + 1) * LANE, :] = Hp * etot + lax.dot_general(
                    xdw, Bg, (((0,), (0,)), ((), ())), preferred_element_type=F32)
        end_exchange()

    def ymap(b, d, k):
        return (d, b, _chunk_of(d, jnp.maximum(k, n_cc), n_cc, n_ch) - n_cc, 0)

    return pl.pallas_call(
        body, name="ssd_fwd", grid=(nb, 2, n_ch),
        in_specs=[pl.BlockSpec((1, Q, XBC), lambda b, d, k: (b, _chunk_of(d, k, n_cc, n_ch), 0)),
                  pl.BlockSpec((1, 1, Q, LANE), lambda b, d, k: (d, b, _chunk_of(d, k, n_cc, n_ch), 0)),
                  pl.BlockSpec((1, 1, LANE), lambda b, d, k: (d, 0, 0)), pl.BlockSpec((1, 1, LANE), lambda b, d, k: (d, 0, 0)),
                  pl.BlockSpec((LANE, D_INNER), lambda b, d, k: (0, 0))] + hosted.specs,
        out_specs=[pl.BlockSpec((1, 1, Q, D_INNER), ymap),
                   pl.BlockSpec((1, 1, 1, D_INNER, SSD_N), lambda b, d, k: (d, b, k, 0, 0))] + hosted.specs,
        out_shape=[jax.ShapeDtypeStruct((2, nb, S, D_INNER), F32),
                   jax.ShapeDtypeStruct((2, nb, n_ch, D_INNER, SSD_N), BF16)] + hosted.out_shape,
        scratch_shapes=[pltpu.VMEM((D_INNER, SSD_N), F32)] + hosted.scratch,
        compiler_params=_cparams("arbitrary", "arbitrary", "arbitrary"),
    )(xbc, dt2, alog2, dtb2, head_spread_matrix(), *hosted.arrays)


def ssd_bwd(xbc, dt2, alog2, dtb2, hin, dy, n_ctx, hosted):
    nb, T, _ = xbc.shape
    n_ex = hosted.n
    n_ch, n_cc = T // CHUNK, n_ctx // CHUNK
    n_in = 7
    Q = CHUNK
    n_pairs = SSD_HEADS // 2
    NT = (((1,), (1,)), ((), ()))
    NN = (((1,), (0,)), ((), ()))
    TN = (((0,), (0,)), ((), ()))

    def dot(a, b, dims):
        return lax.dot_general(a.astype(BF16), b.astype(BF16), dims, preferred_element_type=F32)

    def body(*refs):
        x_ref, dt_ref, al_ref, db_ref, e_ref, hin_ref, dy_ref = refs[:n_in]
        send_refs = refs[n_in:n_in + n_ex]
        dx_ref, ddt_ref, st_ref = refs[n_in + n_ex:n_in + 3 + n_ex]
        recv_refs = refs[n_in + 3 + n_ex:n_in + 3 + 2 * n_ex]
        dH, dce, dde, *sems = refs[n_in + 3 + 2 * n_ex:]
        d, kk = pl.program_id(1), pl.program_id(2)
        ks = n_ch - 1 - kk
        first_step = jnp.logical_and(jnp.logical_and(pl.program_id(0) == 0, d == 0), kk == 0)
        last_step = jnp.logical_and(jnp.logical_and(pl.program_id(0) == nb - 1, d == 1), kk == n_ch - 1)
        begin_exchange, end_exchange = hosted.steps(send_refs, recv_refs, sems, first_step, last_step)
        begin_exchange()

        @pl.when(kk == 0)
        def _():
            dH[...] = jnp.zeros_like(dH)

        @pl.when(jnp.logical_and(jnp.logical_and(pl.program_id(0) == 0, d == 0), kk == 0))
        def _():
            st_ref[...] = jnp.zeros_like(st_ref)

        dt_raw = dt_ref[0, 0]
        alog, dtb_v = al_ref[0], db_ref[0]
        maskb, tri, A, dtv, cum, tot = _ssd_common(d, dt_raw, alog, dtb_v)
        e = e_ref[...]
        cumT = cum.T
        cum_e, dt_e = _spread(cum, e), _spread(dtv, e)
        tot_e = _spread(jnp.broadcast_to(tot, (8, LANE)), e)[0:1]
        live = (ks >= n_cc).astype(F32)
        lane = lax.broadcasted_iota(jnp.int32, (Q, LANE), 1)
        lane1 = lax.broadcasted_iota(jnp.int32, (1, LANE), 1)
        sub = lax.broadcasted_iota(jnp.int32, (LANE, Q), 0)
        subc = lax.broadcasted_iota(jnp.int32, (LANE, 1), 0)
        half = lane < SSD_P
        halfc = subc < SSD_P
        ones = jnp.ones((LANE, LANE), BF16)
        dcum = jnp.zeros((Q, LANE), F32)
        dcumT = jnp.zeros((LANE, Q), F32)
        dtot = jnp.zeros((1, LANE), F32)
        dtot_parts = []
        for g in range(SSD_GROUPS):
            Bg = x_ref[0, :, D_INNER + g * SSD_N:D_INNER + (g + 1) * SSD_N].astype(BF16)
            Cg = x_ref[0, :, D_INNER + GN + g * SSD_N:D_INNER + GN + (g + 1) * SSD_N].astype(BF16)
            Gm = lax.dot_general(Cg, Bg, NT, preferred_element_type=F32)
            dG = jnp.zeros((Q, Q), F32)
            dC = jnp.zeros((Q, SSD_N), F32)
            dB = jnp.zeros((Q, SSD_N), F32)
            for pr in range(n_pairs // SSD_GROUPS):
                p = g * (n_pairs // SSD_GROUPS) + pr
                l0, l1 = 2 * p, 2 * p + 1
                sc, dtp, totp = [t[:, p * LANE:(p + 1) * LANE] for t in (cum_e, dt_e, tot_e)]
                swapped = pltpu.roll(sc, SSD_P, 1)
                s0c, s1c = jnp.where(half, sc, swapped), jnp.where(half, swapped, sc)
                s0r, s1r = cumT[l0:l0 + 1, :], cumT[l1:l1 + 1, :]
                tot0, tot1 = _lane_pick(tot, lane1, l0), _lane_pick(tot, lane1, l1)
                L0 = jnp.exp(jnp.where(maskb, s0c - s0r, NEG_BIG))
                L1 = jnp.exp(jnp.where(maskb, s1c - s1r, NEG_BIG))
                M0, M1 = Gm * L0, Gm * L1
                xs = x_ref[0, :, p * LANE:(p + 1) * LANE]
                xd = xs * dtp
                es = jnp.exp(sc)
                dte = jnp.exp(totp - sc)
                etot = jnp.exp(jnp.where(halfc, tot0, tot1))
                dyp = dy_ref[0, :, p * LANE:(p + 1) * LANE] * live
                Hp = hin_ref[0, 0, 0, p * LANE:(p + 1) * LANE, :]
                dHp = dH[p * LANE:(p + 1) * LANE, :]
                bdh = dot(Bg, dHp, NT)
                dxd = jnp.where(half, dot(M0, dyp, TN), dot(M1, dyp, TN)) + bdh * dte
                dy0 = jnp.where(half, dyp, 0.0)
                dy1 = dyp - dy0
                dM0, dM1 = dot(dy0, xd, NT), dot(dy1, xd, NT)
                dG = dG + dM0 * L0 + dM1 * L1
                dyes = dyp * es
                xdw = xd * dte
                dC = dC + dot(dyes, Hp, NN)
                dB = dB + dot(xdw, dHp, NN)
                W0, W1 = dM0 * M0, dM1 * M1
                yoff = dot(Cg, Hp, NT) * es
                r_off = dyp * yoff
                r_st = xd * bdh * dte
                hh = jnp.sum(dHp * Hp.astype(F32), axis=1, keepdims=True) * etot
                dce[:, p * LANE:(p + 1) * LANE] = r_off - r_st
                dde[:, p * LANE:(p + 1) * LANE] = dxd * xs
                dtot_parts.append(jnp.sum(r_st, axis=0, keepdims=True))
                for (l, W, hselc) in ((l0, W0, halfc), (l1, W1, jnp.logical_not(halfc))):
                    col_g = _split_dot(W, ones, NN)
                    row_g = -jnp.sum(W, axis=0, keepdims=True)
                    dcum = dcum + jnp.where(lane == l, col_g, 0.0)
                    dcumT = dcumT + jnp.where(sub == l, row_g, 0.0)
                    dtot = dtot + jnp.where(lane1 == l, jnp.sum(jnp.where(hselc, hh, 0.0), axis=0, keepdims=True), 0.0)
                dx_ref[0, 0, :, p * LANE:(p + 1) * LANE] = dxd * dtp
                dH[p * LANE:(p + 1) * LANE, :] = dHp * etot + dot(dyes, Cg, TN)
            dx_ref[0, 0, :, D_INNER + g * SSD_N:D_INNER + (g + 1) * SSD_N] = dB + dot(dG, Cg, TN)
            dx_ref[0, 0, :, D_INNER + GN + g * SSD_N:D_INNER + GN + (g + 1) * SSD_N] = dC + dot(dG, Bg, NN)
        dcum_all = dcum + dcumT.T + _gather_heads(dce[...], e)
        dtot_e = jnp.broadcast_to(jnp.concatenate(dtot_parts, axis=1), (8, D_INNER))
        dtot = dtot + _gather_heads(dtot_e, e)[0:1]
        da = lax.dot_general(tri, dcum_all, TN, precision=lax.Precision.HIGHEST, preferred_element_type=F32) + dtot
        ddtv = _gather_heads(dde[...], e) + da * A
        ddt_raw = ddtv * jax.nn.sigmoid(dt_raw + dtb_v)
        ddt_ref[0, 0] = ddt_raw
        sub8 = lax.broadcasted_iota(jnp.int32, (8, LANE), 0)
        st_ref[...] += (jnp.where(sub8 == 2 * d, jnp.sum(da * dtv * A, axis=0, keepdims=True), 0.0)
                        + jnp.where(sub8 == 2 * d + 1, jnp.sum(ddt_raw, axis=0, keepdims=True), 0.0))
        end_exchange()

    def cmap(d, kk):
        return _chunk_of(d, n_ch - 1 - kk, n_cc, n_ch)

    def dymap(b, d, kk):
        return (b, _chunk_of(d, jnp.maximum(n_ch - 1 - kk, n_cc), n_cc, n_ch) - n_cc, 0)

    return pl.pallas_call(
        body, name="ssd_bwd", grid=(nb, 2, n_ch),
        in_specs=[pl.BlockSpec((1, Q, XBC), lambda b, d, kk: (b, cmap(d, kk), 0)),
                  pl.BlockSpec((1, 1, Q, LANE), lambda b, d, kk: (d, b, cmap(d, kk), 0)),
                  pl.BlockSpec((1, 1, LANE), lambda b, d, kk: (d, 0, 0)), pl.BlockSpec((1, 1, LANE), lambda b, d, kk: (d, 0, 0)),
                  pl.BlockSpec((LANE, D_INNER), lambda b, d, kk: (0, 0)),
                  pl.BlockSpec((1, 1, 1, D_INNER, SSD_N), lambda b, d, kk: (d, b, n_ch - 1 - kk, 0, 0)),
                  pl.BlockSpec((1, Q, D_INNER), dymap)] + hosted.specs,
        out_specs=[pl.BlockSpec((1, 1, Q, XBC), lambda b, d, kk: (d, b, cmap(d, kk), 0)),
                   pl.BlockSpec((1, 1, Q, LANE), lambda b, d, kk: (d, b, cmap(d, kk), 0)),
                   pl.BlockSpec((8, LANE), lambda b, d, kk: (0, 0))] + hosted.specs,
        out_shape=[jax.ShapeDtypeStruct((2, nb, T, XBC), F32), jax.ShapeDtypeStruct((2, nb, T, LANE), F32),
                   jax.ShapeDtypeStruct((8, LANE), F32)] + hosted.out_shape,
        scratch_shapes=[pltpu.VMEM((D_INNER, SSD_N), F32), pltpu.VMEM((Q, D_INNER), F32), pltpu.VMEM((Q, D_INNER), F32)] + hosted.scratch,
        compiler_params=_cparams("arbitrary", "arbitrary", "arbitrary"),
    )(xbc, dt2, alog2, dtb2, head_spread_matrix(), hin, dy, *hosted.arrays)


def _adamw(w, g, m, v):
    mn = ADAM_B1 * m + (1.0 - ADAM_B1) * g
    vn = ADAM_B2 * v + (1.0 - ADAM_B2) * jnp.square(g)
    m_hat = mn / (1.0 - ADAM_B1 ** ADAM_STEP)
    v_hat = vn / (1.0 - ADAM_B2 ** ADAM_STEP)
    return -ADAM_LR * (m_hat / (jnp.sqrt(v_hat) + ADAM_EPS) + ADAM_WD * w), mn, vn


def adamw_matrix(name, w, g_slots, m, v):
    K, n = w.shape
    s = g_slots.shape[0]
    tr = _tile(K, 256, 8)

    def body(w_ref, g_ref, m_ref, v_ref, go_ref, d_ref, mo_ref, vo_ref):
        g = g_ref[0].astype(F32)
        for j in range(1, s):
            g = g + g_ref[j].astype(F32)
        go_ref[...] = g
        d_ref[...], mo_ref[...], vo_ref[...] = _adamw(w_ref[...], g, m_ref[...], v_ref[...])

    spec = pl.BlockSpec((tr, n), lambda i: (i, 0))
    return pl.pallas_call(
        body, name=name, grid=(K // tr,),
        in_specs=[spec, pl.BlockSpec((s, tr, n), lambda i: (0, i, 0)), spec, spec], out_specs=[spec] * 4,
        out_shape=[jax.ShapeDtypeStruct((K, n), F32)] * 4,
        compiler_params=_cparams("arbitrary"),
    )(w, g_slots, m, v)


def adamw_small(ws, gs, ms, vs):
    n = len(ws)

    def body(*refs):
        for i in range(n):
            d, mn, vn = _adamw(refs[i][...], refs[n + i][...], refs[2 * n + i][...], refs[3 * n + i][...])
            refs[4 * n + i][...] = d
            refs[5 * n + i][...] = mn
            refs[6 * n + i][...] = vn

    shapes = [jax.ShapeDtypeStruct(w.shape, F32) for w in ws]
    out = pl.pallas_call(body, name="adamw_small", out_shape=shapes * 3)(*ws, *gs, *ms, *vs)
    return out[:n], out[n:2 * n], out[2 * n:]


def sum_slots(name, x):
    n = x.shape[0]

    def fn(t):
        acc = t[0]
        for j in range(1, n):
            acc = acc + t[j]
        return (acc,)

    return ew_call(name, fn, [x], [(x.shape[1:], F32)])[0]


def _pack_rows(parts):
    rows = []
    for p in parts:
        flat = p.reshape(1, -1)
        n = flat.shape[1]
        rows.append(jnp.pad(flat, ((0, 0), (0, -(-n // (8 * LANE)) * 8 * LANE - n))).reshape(-1, LANE))
    return jnp.concatenate(rows, axis=0)


def _unpack_rows(pack, shapes):
    out, r = [], 0
    for s in shapes:
        n = int(np.prod(s))
        nr = -(-n // (8 * LANE)) * 8
        out.append(pack[r:r + nr].reshape(1, -1)[:, :n].reshape(s))
        r += nr
    return out


def _mesh_pos():
    return lax.axis_index("x"), lax.axis_index("y"), lax.axis_index("c")


N_PEERS = N_DEV - 1


def all_gather(name, vs):
    n = len(vs)

    def body(*refs):
        _ag_start(refs[:n], refs[n:2 * n], *refs[2 * n:])
        _ag_finish(refs[:n], refs[n:2 * n], *refs[2 * n:])

    hbm = pl.BlockSpec(memory_space=pl.ANY)
    return pl.pallas_call(
        body, name=name, out_shape=_ag_out_shape(vs), in_specs=[hbm] * n, out_specs=[hbm] * n,
        scratch_shapes=_a2a_scratch(n),
    )(*vs)


def _ag_out_shape(vs):
    return [jax.ShapeDtypeStruct((N_DEV,) + v.shape, v.dtype) for v in vs]


def _ag_copies(x_refs, out_refs, send_sems, recv_sems, local_sems):
    n = len(x_refs)
    x, y, c = _mesh_pos()
    me, sibling = (x, y, c), (x, y, 1 - c)
    chips = [(1 - x, y), (x, 1 - y), (1 - x, 1 - y)]

    def slot(a, px, py, pc):
        return out_refs[a].at[4 * px + 2 * py + pc]

    def copy(a, k, block, to, src=None):
        return pltpu.make_async_remote_copy(
            src_ref=slot(a, *block) if src is None else src, dst_ref=slot(a, *block),
            send_sem=send_sems.at[N_PEERS * a + k], recv_sem=recv_sems.at[N_PEERS * a + k],
            device_id=to, device_id_type=MESH)

    local = [pltpu.make_async_copy(x_refs[a], slot(a, *me), local_sems.at[a]) for a in range(n)]
    first = []
    for a in range(n):
        first.append(copy(a, 0, me, sibling, src=x_refs[a]))
        first += [copy(a, 1 + j, me, (*chip, c), src=x_refs[a]) for j, chip in enumerate(chips)]
    passed = [(copy(a, 1 + j, (*chip, c), me), copy(a, 4 + j, (*chip, c), sibling))
              for j, chip in enumerate(chips) for a in range(n)]
    from_sibling = []
    for a in range(n):
        from_sibling.append(copy(a, 0, sibling, me))
        from_sibling += [copy(a, 4 + j, (*chip, 1 - c), me) for j, chip in enumerate(chips)]
    return local, first, passed, from_sibling


def _ag_start(*refs):
    local, first, _, _ = _ag_copies(*refs)
    for cp in local + first:
        cp.start()


def _ag_finish(*refs):
    local, first, passed, from_sibling = _ag_copies(*refs)
    for arrived, hand_on in passed:
        arrived.wait_recv()
        hand_on.start()
    for cp in from_sibling:
        cp.wait_recv()
    for cp in first + [hand_on for _, hand_on in passed]:
        cp.wait_send()
    for cp in local:
        cp.wait()


def _a2a_scratch(n):
    return [pltpu.SemaphoreType.DMA((N_PEERS * n,)), pltpu.SemaphoreType.DMA((N_PEERS * n,)), pltpu.SemaphoreType.DMA((n,))]


def _a2a_copies(x_refs, out_refs, send_sems, recv_sems, local_sems):
    n = len(x_refs)
    x, y, c = _mesh_pos()
    me = 4 * x + 2 * y + c
    local = [pltpu.make_async_copy(x_refs[a].at[me], out_refs[a].at[me], local_sems.at[a]) for a in range(n)]
    remote = []
    for k in range(1, N_DEV):
        px, py, pc = x ^ ((k >> 2) & 1), y ^ ((k >> 1) & 1), c ^ (k & 1)
        for a in range(n):
            remote.append(pltpu.make_async_remote_copy(
                src_ref=x_refs[a].at[4 * px + 2 * py + pc], dst_ref=out_refs[a].at[me],
                send_sem=send_sems.at[N_PEERS * a + k - 1], recv_sem=recv_sems.at[N_PEERS * a + k - 1],
                device_id=(px, py, pc), device_id_type=MESH))
    return local, remote


def _a2a_start(local, remote):
    for cp in local + remote:
        cp.start()


def _a2a_wait(local, remote):
    for cp in remote:
        cp.wait_recv()
    for cp in remote:
        cp.wait_send()
    for cp in local:
        cp.wait()


class Hosted:
    def __init__(self, start=None, finish=None, arrays=(), out_shape=()):
        self.start, self.finish, self.arrays, self.out_shape = start, finish, list(arrays), list(out_shape)
        self.n = len(self.arrays)
        self.specs = [pl.BlockSpec(memory_space=pl.ANY)] * self.n
        self.scratch = _a2a_scratch(self.n) if self.n else []

    def steps(self, send_refs, recv_refs, sems, first_step, last_step):
        def begin():
            if self.n:
                pl.when(first_step)(lambda: self.start(send_refs, recv_refs, *sems))

        def end():
            if self.n:
                pl.when(last_step)(lambda: self.finish(send_refs, recv_refs, *sems))

        return begin, end


def hosted_all_to_all(vs):
    return Hosted(lambda *r: _a2a_start(*_a2a_copies(*r)), lambda *r: _a2a_wait(*_a2a_copies(*r)), vs,
                  [jax.ShapeDtypeStruct(v.shape, v.dtype) for v in vs])


def hosted_all_gather(vs):
    return Hosted(_ag_start, _ag_finish, vs, _ag_out_shape(vs))


def _taps8(w):
    return jnp.concatenate([w, jnp.zeros((8 - w.shape[0], w.shape[1]), w.dtype)], axis=0)


FIRST = ("w_in",)
LATE_WEIGHTS = ("w_out", "w_up", "w_down", "w_q_up", "w_kv_up")


def first_weights_to_internal(w_in):
    cq, ckv, kr, z, xbc, dt = jnp.split(w_in, np.cumsum(IN_SPLITS)[:-1].tolist(), axis=1)
    K = w_in.shape[0]

    def zeros(n):
        return jnp.zeros((K, n), w_in.dtype)

    w_in_p = jnp.concatenate([cq, zeros(KR_LANE), kr, zeros(LANE - KR_LANE - ROPE), ckv, zeros(OFF_Z - OFF_CKV - KV_RANK),
                              z, xbc, dt, zeros(WIN_P - OFF_DT - 2 * SSD_HEADS)], axis=1)
    return dict(w_in_p=w_in_p)


def late_weights_to_internal(w_out, w_up, w_down, w_q_up, w_kv_up):
    attn_rows = w_out[:N_HEADS * V_DIM].reshape(N_HEADS, V_DIM, -1)
    w_out_p = jnp.concatenate([jnp.pad(attn_rows, ((0, 0), (HEAD_BLOCK - V_DIM, 0), (0, 0))).reshape(QP, -1),
                               w_out[N_HEADS * V_DIM:]], axis=0)
    w_q_p = jnp.pad(w_q_up.reshape(Q_RANK, N_HEADS, NOPE + ROPE), ((0, 0), (0, 0), (0, HEAD_BLOCK - NOPE - ROPE))).reshape(Q_RANK, QP)
    return dict(w_out_p=w_out_p, w_up=glu_interleave(w_up), w_down=w_down, w_q_p=w_q_p, w_kv=w_kv_up)


def _q_grad(g_q_p):
    return g_q_p.reshape(Q_RANK, N_HEADS, HEAD_BLOCK)[:, :, :NOPE + ROPE].reshape(Q_RANK, -1)


def _out_grad(g_out_p):
    return jnp.concatenate([g_out_p[:QP].reshape(N_HEADS, HEAD_BLOCK, -1)[:, HEAD_BLOCK - V_DIM:].reshape(N_HEADS * V_DIM, -1),
                            g_out_p[QP:]], axis=0)


EARLY = ("w_out", "w_up", "w_down", "w_q_up", "w_kv_up")


def local_step(x, ctx, target, mod_x, mod_c, W, late_shards, V):
    nb, S, D = x.shape
    C = ctx.shape[1]
    T = C + S
    tr = _tile(math.gcd(C, S), 256, 8)
    tq = _tile(S, 256, 8)
    tc = 256
    cblk = C // tr
    m = [mod_x[:, i * D:(i + 1) * D][:, None, :] for i in range(N_MOD)]
    mc = [mod_c[:, i * D:(i + 1) * D] for i in range(2)]
    ssd_w8, ffn_w8 = _taps8(V["ssd_conv_w"]), _taps8(V["ffn_conv_w"])
    dexp = jnp.repeat(V["ssd_d"].reshape(-1), SSD_P).reshape(1, D_INNER)
    cosT, sinT = rope_tables(C, S)
    cosS, sinS = cosT[C:], sinT[C:]

    (h1x,) = rows_fwd("prenorm_x", fn_prenorm, nb, S // tr, tr, [(x, D, 0, 0)], [m[0], m[1]], [V["mix_pre_norm"]], [(D, BF16)])
    (h1c,) = rows_fwd("prenorm_c", fn_prenorm, nb, C // tr, tr, [(ctx, D, 0, 0)], [], [mc[0], mc[1], V["mix_pre_norm"]], [(D, BF16)])
    h1 = jnp.concatenate([h1c, h1x], axis=1).reshape(nb * T, D)
    u = matmul("in_proj", [(h1, W["w_in_p"])], "nn", F32).reshape(nb, T, WIN_P)
    xbc = ssd_conv_fwd(u, ssd_w8, V["ssd_conv_b"], C, tc)
    dt2, alog2, dtb2 = ssd_dt_inputs(u, V["ssd_a_log"], V["ssd_dt_bias"])
    y2, hin, *late = ssd_fwd(xbc, dt2, alog2, dtb2, C, hosted_all_gather(late_shards))
    W = dict(W, **late_weights_to_internal(*[_whole(s, n) for s, n in zip(late, LATE_WEIGHTS)]))
    y2 = y2.reshape(2 * nb, S, D_INNER)
    (qn,) = rows_fwd("q_norm", fn_rms, nb, S // tr, tr, [(u, Q_RANK, OFF_CQ // Q_RANK, cblk)], [], [V["q_norm"]], [(Q_RANK, BF16)])
    (kvn,) = rows_fwd("kv_norm", fn_rms, nb, T // tr, tr, [(u, KV_RANK, OFF_CKV // KV_RANK, 0)], [], [V["kv_norm"]], [(KV_RANK, BF16)])
    qn2, kvn2 = qn.reshape(nb * S, Q_RANK), kvn.reshape(nb * T, KV_RANK)
    q_raw = matmul("q_up", [(qn2, W["w_q_p"])], "nn", F32).reshape(nb, S, QP)
    kv = matmul("kv_up", [(kvn2, W["w_kv"])], "nn", BF16).reshape(nb, T, QP)
    cos_q, sin_q = cosS * Q_PRESCALE, sinS * Q_PRESCALE
    kr = rope_call("rope_k", u, LANE, OFF_KR // LANE, cosT, sinT, BF16, tr)
    o = attn_fwd(q_raw, kv, kr, cos_q, sin_q, tq)
    fin_rows = [(y2, D_INNER, 0, 0, 0), (y2, D_INNER, 0, 0, nb), (xbc, D_INNER, 0, cblk), (u, D_INNER, OFF_Z // D_INNER, cblk)]
    fin_gl = [dexp, V["ssd_norm"]]
    (ssd,) = rows_fwd("ssd_finish", fn_ssd_finish, nb, S // tr, tr, fin_rows, [], fin_gl, [(D_INNER, BF16)])
    o2, ssd2 = o.reshape(nb * S, QP), ssd.reshape(nb * S, D_INNER)
    mix = matmul("out_proj", [(o2, W["w_out_p"][:QP]), (ssd2, W["w_out_p"][QP:])], "nn", F32).reshape(nb, S, D)
    pm_rows = [(x, D, 0, 0), (mix, D, 0, 0)]
    pm_pb = [m[2], m[4], m[3]]
    pm_gl = [V["mix_post_norm"], V["ffn_pre_norm"]]
    x1, h2 = rows_fwd("postmix", fn_postmix, nb, S // tr, tr, pm_rows, pm_pb, pm_gl, [(D, F32), (D, BF16)])
    h22 = h2.reshape(nb * S, D)
    up = matmul("up_proj", [(h22, W["w_up"])], "nn", F32).reshape(nb, S, 2 * D_FF)
    act = glu_fwd(up, ffn_w8, V["ffn_conv_b"])
    act2 = act.reshape(nb * S, D_FF)
    ffn = matmul("down_proj", [(act2, W["w_down"])], "nn", F32).reshape(nb, S, D)
    dx1, dffn, dgate2, d_ffn_post, loss = final_call(x1, ffn, target, m[5], V["ffn_post_norm"], tr)

    dffn2 = dffn.reshape(nb * S, D)
    dact = matmul("down_dgrad", [(dffn2, W["w_down"])], "nt", BF16).reshape(nb, S, D_FF)
    g_down = matmul_tn("down_wgrad", act2, dffn2)
    dup, ffn_rows = glu_bwd(up, ffn_w8, V["ffn_conv_b"], dact)
    dup2 = dup.reshape(nb * S, 2 * D_FF)
    dh2 = matmul("up_dgrad", [(dup2, W["w_up"])], "nt", BF16).reshape(nb, S, D)
    g_up = matmul_tn("up_wgrad", h22, dup2)
    dx_a, dmix, dgate1, dscale2, dshift2, d_mix_post, d_ffn_pre = rows_bwd(
        "postmix_bwd", fn_postmix, nb, S // tr, tr, pm_rows, pm_pb, pm_gl,
        [(dx1, D, 0, 0), (dh2, D, 0, 0)], [(0, F32), (1, BF16)])
    dmix2 = dmix.reshape(nb * S, D)
    dcat = matmul("out_dgrad", [(dmix2, W["w_out_p"])], "nt", BF16).reshape(nb, S, QP + D_INNER)
    g_out_p = jnp.concatenate([matmul_tn("out_wgrad_attn", o2, dmix2), matmul_tn("out_wgrad_ssd", ssd2, dmix2)], axis=0)
    dy, dxs_direct, dz, d_dexp, d_ssd_norm = rows_bwd(
        "ssd_finish_bwd", fn_ssd_finish, nb, S // tr, tr, fin_rows, [], fin_gl,
        [(dcat, D_INNER, QP // D_INNER, 0)], [(0, F32), (2, F32), (3, BF16)])
    dq_pre, dkv, dkr = attn_bwd(q_raw, kv, kr, dcat, cos_q, sin_q, cosS, sinS, tq)
    dq_pre = dq_pre.reshape(nb * S, QP)
    dkr_pre = rope_call("rope_dk", dkr, LANE, 0, cosT, -sinT, BF16, tr)
    dkv2 = dkv.reshape(nb * T, QP)
    dqn = matmul("q_dgrad", [(dq_pre, W["w_q_p"])], "nt", F32).reshape(nb, S, Q_RANK)
    g_q_p = matmul_tn("q_wgrad", qn2, dq_pre)
    dkvn = matmul("kv_dgrad", [(dkv2, W["w_kv"])], "nt", F32).reshape(nb, T, KV_RANK)
    g_kv = matmul_tn("kv_wgrad", kvn2, dkv2)
    early_grads = (_out_grad(g_out_p), glu_deinterleave(g_up), g_down, _q_grad(g_q_p), g_kv)
    early = hosted_all_to_all([_per_device(g, n) for g, n in zip(early_grads, EARLY)])
    dxbc2, ddt2, ssd_stats, *received = ssd_bwd(xbc, dt2, alog2, dtb2, hin, dy, C, early)
    ddt_block = jnp.concatenate([ddt2[0][..., :SSD_HEADS], ddt2[1][..., :SSD_HEADS],
                                 jnp.zeros((nb, T, LANE - 2 * SSD_HEADS), F32)], axis=-1).astype(BF16)
    dxbc_raw, ssd_rows = ssd_conv_bwd(u, ssd_w8, V["ssd_conv_b"], dxbc2, dxs_direct, C, tc)
    dcq, d_q_norm = rows_bwd("q_norm_bwd", fn_rms, nb, S // tr, tr, [(u, Q_RANK, OFF_CQ // Q_RANK, cblk)], [], [V["q_norm"]],
                             [(dqn, Q_RANK, 0, 0)], [(0, BF16)])
    dckv, d_kv_norm = rows_bwd("kv_norm_bwd", fn_rms, nb, T // tr, tr, [(u, KV_RANK, OFF_CKV // KV_RANK, 0)], [], [V["kv_norm"]],
                               [(dkvn, KV_RANK, 0, 0)], [(0, BF16)])

    def ctx_rows(t):
        return jnp.pad(t, ((0, 0), (C, 0), (0, 0)))

    du = [("cq", ctx_rows(dcq), OFF_CQ, Q_RANK), ("kr", dkr_pre, OFF_KR, LANE), ("ckv", dckv, OFF_CKV, KV_RANK),
          ("z", ctx_rows(dz), OFF_Z, D_INNER), ("xbc", dxbc_raw, OFF_XBC, XBC), ("dt", ddt_block, OFF_DT, LANE)]
    du = [(name, t.reshape(nb * T, w), off, w) for (name, t, off, w) in du]
    g = {name: matmul_tn("in_wgrad_" + name, h1, t) for (name, t, _, _) in du}
    g_in = jnp.concatenate([g["cq"], g["ckv"], g["kr"][:, KR_LANE:KR_LANE + ROPE], g["z"], g["xbc"],
                            g["dt"][:, :2 * SSD_HEADS]], axis=1)
    dh1, received_in = matmul("in_dgrad", [(t, W["w_in_p"][:, off:off + w]) for (_, t, off, w) in du], "nt", BF16,
                              hosted=hosted_all_to_all([_per_device(g_in, "w_in").astype(BF16)]))
    dh1 = dh1.reshape(nb, T, D)

    def fn_prenorm_res(xv, shift, scale, g):
        return fn_prenorm(xv, shift, scale, g) + (xv,)

    grad_x, dshift1, dscale1, d_mix_pre_x = rows_bwd(
        "prenorm_x_bwd", fn_prenorm_res, nb, S // tr, tr, [(x, D, 0, 0)], [m[0], m[1]], [V["mix_pre_norm"]],
        [(dh1, D, 0, cblk), (dx_a, D, 0, 0)], [(0, F32)])
    dshift_c, dscale_c, d_mix_pre_c = rows_bwd(
        "prenorm_c_bwd", fn_prenorm, nb, C // tr, tr, [(ctx, D, 0, 0)], [], [mc[0], mc[1], V["mix_pre_norm"]],
        [(dh1, D, 0, 0)], [])

    dmod_x = jnp.concatenate([dshift1, dscale1, dgate1, dshift2, dscale2, dgate2], axis=-1).reshape(nb, N_MOD * D)
    dmod_c = jnp.concatenate([dshift_c, dscale_c, jnp.zeros((1, (N_MOD - 2) * D), F32)], axis=-1)
    gv = dict(
        mix_pre_norm=d_mix_pre_x + d_mix_pre_c, mix_post_norm=d_mix_post, q_norm=d_q_norm, kv_norm=d_kv_norm,
        ssd_conv_w=ssd_rows[:SSD_K], ssd_conv_b=ssd_rows[SSD_K:SSD_K + 1],
        ssd_a_log=jnp.concatenate([ssd_stats[0:1, :SSD_HEADS], ssd_stats[2:3, :SSD_HEADS]], axis=1),
        ssd_dt_bias=jnp.concatenate([ssd_stats[1:2, :SSD_HEADS], ssd_stats[3:4, :SSD_HEADS]], axis=1),
        ssd_d=jnp.sum(d_dexp.reshape(SSD_HEADS, SSD_P), axis=1).reshape(1, SSD_HEADS), ssd_norm=d_ssd_norm,
        ffn_pre_norm=d_ffn_pre, ffn_post_norm=d_ffn_post,
        ffn_conv_w=ffn_rows[:FFN_K], ffn_conv_b=ffn_rows[FFN_K:FFN_K + 1])
    return loss, grad_x, dmod_x, dmod_c, gv, dict(zip(EARLY, received), w_in=received_in)


WEIGHT_ORDER = ("c_ctx", "w_mod", "b_mod", "mix_pre_norm", "mix_post_norm", "w_in", "q_norm", "w_q_up", "kv_norm",
                "w_kv_up", "ssd_conv_w", "ssd_conv_b", "ssd_a_log", "ssd_dt_bias", "ssd_d", "ssd_norm", "w_out",
                "ffn_pre_norm", "ffn_post_norm", "w_up", "ffn_conv_w", "ffn_conv_b", "w_down")
MATRICES = ("w_in", "w_q_up", "w_kv_up", "w_out", "w_up", "w_down")
ROW_SHARDED = ("w_out", "w_down")
SMALL_SUMMED = ("c_ctx", "mix_pre_norm", "mix_post_norm", "q_norm", "kv_norm", "ssd_conv_w", "ssd_conv_b", "ssd_a_log",
                "ssd_dt_bias", "ssd_d", "ssd_norm", "ffn_pre_norm", "ffn_post_norm", "ffn_conv_w", "ffn_conv_b")
MOD_ROWS = 8


def _whole(shards, name):
    if name in ROW_SHARDED:
        return shards.reshape(-1, shards.shape[-1])
    return jnp.concatenate([shards[j] for j in range(N_DEV)], axis=1)


def _per_device(g, name):
    if name in ROW_SHARDED:
        return g.reshape(N_DEV, -1, g.shape[-1])
    return jnp.stack(jnp.split(g, N_DEV, axis=1))


def kernel(x, c, ctx, c_ctx, w_mod, b_mod, mix_pre_norm, mix_post_norm, w_in, q_norm, w_q_up, kv_norm, w_kv_up, ssd_conv_w, ssd_conv_b, ssd_a_log, ssd_dt_bias, ssd_d, ssd_norm, w_out, ffn_pre_norm, ffn_post_norm, w_up, ffn_conv_w, ffn_conv_b, w_down, loss_target, m_c_ctx, m_w_mod, m_b_mod, m_mix_pre_norm, m_mix_post_norm, m_w_in, m_q_norm, m_w_q_up, m_kv_norm, m_w_kv_up, m_ssd_conv_w, m_ssd_conv_b, m_ssd_a_log, m_ssd_dt_bias, m_ssd_d, m_ssd_norm, m_w_out, m_ffn_pre_norm, m_ffn_post_norm, m_w_up, m_ffn_conv_w, m_ffn_conv_b, m_w_down, v_c_ctx, v_w_mod, v_b_mod, v_mix_pre_norm, v_mix_post_norm, v_w_in, v_q_norm, v_w_q_up, v_kv_norm, v_w_kv_up, v_ssd_conv_w, v_ssd_conv_b, v_ssd_a_log, v_ssd_dt_bias, v_ssd_d, v_ssd_norm, v_w_out, v_ffn_pre_norm, v_ffn_post_norm, v_w_up, v_ffn_conv_w, v_ffn_conv_b, v_w_down):
    weights = dict(c_ctx=c_ctx, w_mod=w_mod, b_mod=b_mod, mix_pre_norm=mix_pre_norm, mix_post_norm=mix_post_norm, w_in=w_in, q_norm=q_norm, w_q_up=w_q_up, kv_norm=kv_norm, w_kv_up=w_kv_up, ssd_conv_w=ssd_conv_w, ssd_conv_b=ssd_conv_b, ssd_a_log=ssd_a_log, ssd_dt_bias=ssd_dt_bias, ssd_d=ssd_d, ssd_norm=ssd_norm, w_out=w_out, ffn_pre_norm=ffn_pre_norm, ffn_post_norm=ffn_post_norm, w_up=w_up, ffn_conv_w=ffn_conv_w, ffn_conv_b=ffn_conv_b, w_down=w_down)
    mom1 = dict(c_ctx=m_c_ctx, w_mod=m_w_mod, b_mod=m_b_mod, mix_pre_norm=m_mix_pre_norm, mix_post_norm=m_mix_post_norm, w_in=m_w_in, q_norm=m_q_norm, w_q_up=m_w_q_up, kv_norm=m_kv_norm, w_kv_up=m_w_kv_up, ssd_conv_w=m_ssd_conv_w, ssd_conv_b=m_ssd_conv_b, ssd_a_log=m_ssd_a_log, ssd_dt_bias=m_ssd_dt_bias, ssd_d=m_ssd_d, ssd_norm=m_ssd_norm, w_out=m_w_out, ffn_pre_norm=m_ffn_pre_norm, ffn_post_norm=m_ffn_post_norm, w_up=m_w_up, ffn_conv_w=m_ffn_conv_w, ffn_conv_b=m_ffn_conv_b, w_down=m_w_down)
    mom2 = dict(c_ctx=v_c_ctx, w_mod=v_w_mod, b_mod=v_b_mod, mix_pre_norm=v_mix_pre_norm, mix_post_norm=v_mix_post_norm, w_in=v_w_in, q_norm=v_q_norm, w_q_up=v_w_q_up, kv_norm=v_kv_norm, w_kv_up=v_w_kv_up, ssd_conv_w=v_ssd_conv_w, ssd_conv_b=v_ssd_conv_b, ssd_a_log=v_ssd_a_log, ssd_dt_bias=v_ssd_dt_bias, ssd_d=v_ssd_d, ssd_norm=v_ssd_norm, w_out=v_w_out, ffn_pre_norm=v_ffn_pre_norm, ffn_post_norm=v_ffn_post_norm, w_up=v_w_up, ffn_conv_w=v_ffn_conv_w, ffn_conv_b=v_ffn_conv_b, w_down=v_w_down)
    nb, S, D = x.shape
    me = 4 * lax.axis_index("x") + 2 * lax.axis_index("y") + lax.axis_index("c")

    *first, c_all, ssd_w_sh, ffn_w_sh = all_gather(
        "gather_first", [weights[n][0].astype(BF16) for n in FIRST] + [c, ssd_conv_w[0], ffn_conv_w[0]])
    W = first_weights_to_internal(*[_whole(s, n) for n, s in zip(FIRST, first)])
    late_shards = [weights[n][0].astype(BF16) for n in LATE_WEIGHTS]
    V = {n: weights[n].reshape(1, -1) for n in SMALL_SUMMED if n != "c_ctx"}
    V["ssd_conv_w"] = _whole(ssd_w_sh, "ssd_conv_w")
    V["ffn_conv_w"] = _whole(ffn_w_sh, "ffn_conv_w")

    n_all = N_DEV * nb
    mod_rows = -(-(n_all + 1) // 8) * 8
    c_pad = jnp.concatenate([c_all.reshape(n_all, D), c_ctx.reshape(1, D), jnp.zeros((mod_rows - n_all - 1, D), F32)], axis=0)
    mod_cols = w_mod.shape[2]
    b_mine = lax.dynamic_slice(b_mod, (0, me * mod_cols), (1, mod_cols))
    mod_part = matmul("mod_proj", [(c_pad, w_mod[0])], "nn", F32, bias=b_mine, silu_a=True)
    mod_all = _whole(all_gather("gather_mod", [mod_part])[0], "w_mod")
    mod_x = lax.dynamic_slice(mod_all, (me * nb, 0), (nb, mod_all.shape[1]))
    mod_c = mod_all[n_all:n_all + 1]

    loss, grad_x, dmod_x, dmod_c, gv, slots = local_step(x, ctx, loss_target, mod_x, mod_c, W, late_shards, V)

    dmod_mine = jnp.concatenate([dmod_x, dmod_c, jnp.zeros((MOD_ROWS - nb - 1, dmod_x.shape[1]), F32)], axis=0)
    dmod_all = all_gather("gather_dmod", [dmod_mine])[0]
    dmod_ctx = sum_slots("sum_dmod_ctx", dmod_all[:, nb:nb + 1].reshape(N_DEV, -1, LANE)).reshape(1, -1)
    dmod_full = jnp.concatenate([dmod_all[:, :nb].reshape(n_all, -1), dmod_ctx,
                                 jnp.zeros((mod_rows - n_all - 1, dmod_ctx.shape[1]), F32)], axis=0)
    (g_b_mod,) = ew_call("mod_bias_grad", lambda t: (jnp.sum(t, axis=0, keepdims=True),), [dmod_full], [((1, dmod_full.shape[1]), F32)])
    dmod_cols = lax.dynamic_slice(dmod_full, (0, me * mod_cols), (mod_rows, mod_cols))
    g_w_mod = matmul_tn("mod_wgrad", c_pad, dmod_cols, silu_a=True)
    dsilu_ctx = matmul("mod_dgrad_ctx", [(dmod_cols[n_all:n_all + 8], w_mod[0])], "nt", F32)[0:1]

    def silu_vjp(cc, ct):
        return (jax.vjp(_silu, cc)[1](ct)[0],)

    (g_c_ctx_part,) = ew_call("c_ctx_grad", silu_vjp, [c_ctx.reshape(1, D), dsilu_ctx], [((1, D), F32)])

    gv = dict(gv, c_ctx=g_c_ctx_part)
    small_parts = [loss] + [gv[n] for n in SMALL_SUMMED]
    small_sum = sum_slots("sum_small", all_gather("gather_small_grads", [_pack_rows(small_parts)])[0])
    summed = _unpack_rows(small_sum, [p.shape for p in small_parts])
    loss_out = summed[0][0, 0]
    grads = {n: g.reshape(weights[n].shape) if n not in ("ssd_conv_w", "ffn_conv_w") else g for n, g in zip(SMALL_SUMMED, summed[1:])}
    for n in ("ssd_conv_w", "ffn_conv_w"):
        cols = weights[n].shape[2]
        grads[n] = lax.dynamic_slice(grads[n], (0, me * cols), (grads[n].shape[0], cols)).reshape(weights[n].shape)
    grads["b_mod"] = g_b_mod.reshape(b_mod.shape)

    slots = dict(slots, w_mod=g_w_mod[None])
    delta, new_m, new_v = {}, {}, {}
    for n in MATRICES + ("w_mod",):
        g, d, mn, vn = adamw_matrix("adamw_" + n, weights[n][0], slots[n], mom1[n][0], mom2[n][0])
        grads[n], delta[n], new_m[n], new_v[n] = [t.reshape(weights[n].shape) for t in (g, d, mn, vn)]
    small = [n for n in WEIGHT_ORDER if n not in slots]

    def two_d(t):
        return t.reshape(-1, t.shape[-1])

    ds, ms, vs = adamw_small(*[[two_d(t[n]) for n in small] for t in (weights, grads, mom1, mom2)])
    for n, d, mn, vn in zip(small, ds, ms, vs):
        delta[n], new_m[n], new_v[n] = [t.reshape(weights[n].shape) for t in (d, mn, vn)]
    return (loss_out, grad_x, *[t[n] for t in (grads, delta, new_m, new_v) for n in WEIGHT_ORDER])
```

```python
import math

import jax
import jax.numpy as jnp
import numpy as np
from jax import lax
from jax.experimental import pallas as pl
from jax.experimental.pallas import tpu as pltpu

F32 = jnp.float32
BF16 = jnp.bfloat16
MESH = pl.DeviceIdType.MESH

D_MODEL = 1024
GRID_W = 64
N_HEADS = 16
NOPE = 64
ROPE = 32
V_DIM = 64
Q_RANK = 384
KV_RANK = 256
ROPE_THETA = 10000.0
ATTN_SCALE = (NOPE + ROPE) ** -0.5
SSD_HEADS = 16
SSD_P = 64
SSD_GROUPS = 2
SSD_N = 128
SSD_K = 5
CHUNK = 128
D_INNER = SSD_HEADS * SSD_P
GN = SSD_GROUPS * SSD_N
XBC = D_INNER + 2 * GN
D_FF = 2816
FFN_K = 3
N_MOD = 6
EPS = 1e-6
IN_SPLITS = (Q_RANK, KV_RANK, ROPE, D_INNER, XBC, 2 * SSD_HEADS)
IN_WIDTH = sum(IN_SPLITS)
N_DEV = 8

ADAM_LR = 0.001
ADAM_B1 = 0.9
ADAM_B2 = 0.999
ADAM_EPS = 1e-08
ADAM_WD = 0.01
ADAM_STEP = 10

LANE = 128
HEAD_BLOCK = 128
OFF_CQ = 0
OFF_KR = 384
OFF_CKV = 512
OFF_Z = 1024
OFF_XBC = 2048
OFF_DT = 3584
WIN_P = 3840
KR_LANE = 64
QP = N_HEADS * HEAD_BLOCK

VMEM_LIMIT_V7X = 56 * 1024 * 1024
NEG_BIG = -1e30


def _cparams(*sem):
    return pltpu.CompilerParams(dimension_semantics=sem, vmem_limit_bytes=VMEM_LIMIT_V7X)


def _tile(n, target, mult=128):
    if n <= target:
        return n
    t = (target // mult) * mult
    while t >= mult:
        if n % t == 0:
            return t
        t -= mult
    return n


def _silu(x):
    return x * jax.nn.sigmoid(x)


def _rms(x, g):
    return x * lax.rsqrt(jnp.mean(x * x, axis=-1, keepdims=True) + EPS) * g


WHOLE_K_WIDE = 2048


def matmul(name, pairs, mode, out_dtype, *, bias=None, silu_a=False, hosted=None):
    n_pairs = len(pairs)
    M = pairs[0][0].shape[0]
    N = pairs[0][1].shape[1] if mode == "nn" else pairs[0][1].shape[0]
    k_total = sum(a.shape[1] for a, _ in pairs)
    tm = _tile(M, 1024 if k_total <= WHOLE_K_WIDE else 512, 8)
    tn = _tile(N, 1408 if k_total <= WHOLE_K_WIDE else 512)
    dims = (((1,), (0,)), ((), ())) if mode == "nn" else (((1,), (1,)), ((), ()))
    n_own = 2 * n_pairs + (bias is not None)
    n_ex = hosted.n if hosted else 0

    def body(*refs):
        o_ref = refs[n_own + n_ex]
        if hosted:
            j, i = pl.program_id(0), pl.program_id(1)
            begin_exchange, end_exchange = hosted.steps(
                refs[n_own:n_own + n_ex], refs[n_own + n_ex + 1:n_own + 2 * n_ex + 1], refs[n_own + 2 * n_ex + 1:],
                jnp.logical_and(j == 0, i == 0), jnp.logical_and(j == N // tn - 1, i == M // tm - 1))
            begin_exchange()
        acc = None
        for p in range(n_pairs):
            a = refs[2 * p][...]
            if silu_a:
                a = _silu(a.astype(F32))
            d = lax.dot_general(a.astype(BF16), refs[2 * p + 1][...].astype(BF16), dims, preferred_element_type=F32)
            acc = d if acc is None else acc + d
        if bias is not None:
            acc = acc + refs[2 * n_pairs][...]
        o_ref[...] = acc.astype(o_ref.dtype)
        if hosted:
            end_exchange()

    in_specs, args = [], []
    for a, b in pairs:
        K = a.shape[1]
        in_specs.append(pl.BlockSpec((tm, K), lambda j, i: (i, 0)))
        in_specs.append(pl.BlockSpec((K, tn), lambda j, i: (0, j)) if mode == "nn" else pl.BlockSpec((tn, K), lambda j, i: (j, 0)))
        args += [a, b]
    if bias is not None:
        in_specs.append(pl.BlockSpec((1, tn), lambda j, i: (0, j)))
        args.append(bias)
    out_spec = pl.BlockSpec((tm, tn), lambda j, i: (i, j))
    out_shape = jax.ShapeDtypeStruct((M, N), out_dtype)
    if not hosted:
        return pl.pallas_call(
            body, name=name, grid=(N // tn, M // tm), in_specs=in_specs, out_specs=out_spec, out_shape=out_shape,
            compiler_params=_cparams("arbitrary", "arbitrary"),
        )(*args)
    return pl.pallas_call(
        body, name=name, grid=(N // tn, M // tm), in_specs=in_specs + hosted.specs,
        out_specs=[out_spec] + hosted.specs, out_shape=[out_shape] + hosted.out_shape, scratch_shapes=hosted.scratch,
        compiler_params=_cparams("arbitrary", "arbitrary"),
    )(*args, *hosted.arrays)


def matmul_tn(name, a, b, out_dtype=F32, *, silu_a=False, tm=1408, tn=512, tk=2048):
    R, M = a.shape
    N = b.shape[1]
    tm = _tile(M, tm)
    tn = _tile(N, tn)
    tk = _tile(R, tk, 8)
    nk = R // tk

    def body(a_ref, b_ref, o_ref, acc):
        k = pl.program_id(2)

        @pl.when(k == 0)
        def _():
            acc[...] = jnp.zeros_like(acc)

        x = a_ref[...]
        if silu_a:
            x = _silu(x.astype(F32))
        acc[...] += lax.dot_general(x.astype(BF16), b_ref[...].astype(BF16), (((0,), (0,)), ((), ())),
                                    preferred_element_type=F32)

        @pl.when(k == nk - 1)
        def _():
            o_ref[...] = acc[...].astype(o_ref.dtype)

    return pl.pallas_call(
        body, name=name, grid=(M // tm, N // tn, nk),
        in_specs=[pl.BlockSpec((tk, tm), lambda i, j, k: (k, i)), pl.BlockSpec((tk, tn), lambda i, j, k: (k, j))],
        out_specs=pl.BlockSpec((tm, tn), lambda i, j, k: (i, j)),
        out_shape=jax.ShapeDtypeStruct((M, N), out_dtype),
        scratch_shapes=[pltpu.VMEM((tm, tn), F32)],
        compiler_params=_cparams("arbitrary", "arbitrary", "arbitrary"),
    )(a, b)


def _row_specs(rin, pbin, glin, tr):
    specs = [pl.BlockSpec((1, tr, w), lambda b, i, cb=cb, ro=ro, bo=(e[4] if len(e) > 4 else 0): (b + bo, i + ro, cb))
             for e in rin for (_, w, cb, ro) in [e[:4]]]
    specs += [pl.BlockSpec((1, 1, a.shape[-1]), lambda b, i: (b, 0, 0)) for a in pbin]
    specs += [pl.BlockSpec((1, a.shape[-1]), lambda b, i: (0, 0)) for a in glin]
    return specs


def rows_fwd(name, fn, nb, nblk, tr, rin, pbin, glin, outs):
    nr, npb, ngl = len(rin), len(pbin), len(glin)
    n_in = nr + npb + ngl

    def body(*refs):
        args = [r[0].astype(F32) for r in refs[:nr + npb]] + [r[...] for r in refs[nr + npb:n_in]]
        res = fn(*args)
        for o, v in zip(refs[n_in:], res):
            o[0] = v.astype(o.dtype)

    return pl.pallas_call(
        body, name=name, grid=(nb, nblk), in_specs=_row_specs(rin, pbin, glin, tr),
        out_specs=[pl.BlockSpec((1, tr, w), lambda b, i: (b, i, 0)) for (w, _) in outs],
        out_shape=[jax.ShapeDtypeStruct((nb, nblk * tr, w), dt) for (w, dt) in outs],
        compiler_params=_cparams("arbitrary", "arbitrary"),
    )(*[e[0] for e in rin], *pbin, *glin)


def rows_bwd(name, fn, nb, nblk, tr, rin, pbin, glin, cts, want):
    nr, npb, ngl, nct = len(rin), len(pbin), len(glin), len(cts)
    n_in = nr + npb + ngl

    def body(*refs):
        b, i = pl.program_id(0), pl.program_id(1)
        args = [r[0].astype(F32) for r in refs[:nr + npb]] + [r[...] for r in refs[nr + npb:n_in]]
        ct = tuple(r[0].astype(F32) for r in refs[n_in:n_in + nct])
        _, vjp = jax.vjp(fn, *args)
        g = vjp(ct)
        orefs = refs[n_in + nct:]
        for o, (idx, _) in zip(orefs, want):
            o[0] = g[idx].astype(o.dtype)
        pb_refs = orefs[len(want):len(want) + npb]
        gl_refs = orefs[len(want) + npb:]

        @pl.when(i == 0)
        def _():
            for o, v in zip(pb_refs, g[nr:nr + npb]):
                o[0] = v

        @pl.when(i > 0)
        def _():
            for o, v in zip(pb_refs, g[nr:nr + npb]):
                o[0] += v

        first = jnp.logical_and(b == 0, i == 0)

        @pl.when(first)
        def _():
            for o, v in zip(gl_refs, g[nr + npb:]):
                o[...] = v

        @pl.when(jnp.logical_not(first))
        def _():
            for o, v in zip(gl_refs, g[nr + npb:]):
                o[...] += v

    out_specs = [pl.BlockSpec((1, tr, rin[idx][1]), lambda b, i: (b, i, 0)) for (idx, _) in want]
    out_shape = [jax.ShapeDtypeStruct((nb, nblk * tr, rin[idx][1]), dt) for (idx, dt) in want]
    out_specs += [pl.BlockSpec((1, 1, a.shape[-1]), lambda b, i: (b, 0, 0)) for a in pbin]
    out_shape += [jax.ShapeDtypeStruct((nb, 1, a.shape[-1]), F32) for a in pbin]
    out_specs += [pl.BlockSpec((1, a.shape[-1]), lambda b, i: (0, 0)) for a in glin]
    out_shape += [jax.ShapeDtypeStruct((1, a.shape[-1]), F32) for a in glin]
    return pl.pallas_call(
        body, name=name, grid=(nb, nblk),
        in_specs=_row_specs(rin, pbin, glin, tr) + _row_specs(cts, [], [], tr),
        out_specs=out_specs, out_shape=out_shape,
        compiler_params=_cparams("arbitrary", "arbitrary"),
    )(*[e[0] for e in rin], *pbin, *glin, *[e[0] for e in cts])


def ew_call(name, fn, ins, outs):
    def body(*refs):
        res = fn(*[r[...] for r in refs[:len(ins)]])
        for o, v in zip(refs[len(ins):], res):
            o[...] = v.astype(o.dtype)

    return pl.pallas_call(body, name=name, out_shape=[jax.ShapeDtypeStruct(s, dt) for (s, dt) in outs])(*ins)


def fn_prenorm(x, shift, scale, g):
    return (_rms(x, g) * (1.0 + scale) + shift,)


def fn_rms(x, g):
    return (_rms(x, g),)


def fn_ssd_finish(yf, yr, xs, z, dexp, nw):
    y = yf + yr + dexp * xs
    return (_rms(y * _silu(z), nw),)


def fn_postmix(x, mix, gate1, scale2, shift2, post_g, pre_g):
    x1 = x + gate1 * _rms(mix, post_g)
    h2 = _rms(x1, pre_g) * (1.0 + scale2) + shift2
    return x1, h2


def final_call(x1, ffn, target, gate2, post_g, tr):
    nb, S, D = x1.shape
    nblk = S // tr

    def body(x1_ref, f_ref, t_ref, g2_ref, pg_ref, dx1_ref, df_ref, dg2_ref, dpg_ref, loss_ref):
        b, i = pl.program_id(0), pl.program_id(1)
        tgt = t_ref[0]

        def lossfn(x1v, fv, g2, pg):
            e = x1v + g2 * _rms(fv, pg) - tgt
            return 0.5 * jnp.sum(jnp.mean(e * e, axis=-1, keepdims=True))

        val, (dx1, df, dg2, dpg) = jax.value_and_grad(lossfn, argnums=(0, 1, 2, 3))(
            x1_ref[0], f_ref[0].astype(F32), g2_ref[0], pg_ref[...])
        dx1_ref[0] = dx1
        df_ref[0] = df.astype(df_ref.dtype)
        lv = jnp.full((1, LANE), val, F32)

        @pl.when(i == 0)
        def _():
            dg2_ref[0] = dg2

        @pl.when(i > 0)
        def _():
            dg2_ref[0] += dg2

        first = jnp.logical_and(b == 0, i == 0)

        @pl.when(first)
        def _():
            dpg_ref[...] = dpg
            loss_ref[...] = lv

        @pl.when(jnp.logical_not(first))
        def _():
            dpg_ref[...] += dpg
            loss_ref[...] += lv

    row = pl.BlockSpec((1, tr, D), lambda b, i: (b, i, 0))
    pb = pl.BlockSpec((1, 1, D), lambda b, i: (b, 0, 0))
    gl = pl.BlockSpec((1, D), lambda b, i: (0, 0))
    return pl.pallas_call(
        body, name="loss_head", grid=(nb, nblk), in_specs=[row, row, row, pb, gl],
        out_specs=[row, row, pb, gl, pl.BlockSpec((1, LANE), lambda b, i: (0, 0))],
        out_shape=[jax.ShapeDtypeStruct((nb, S, D), F32), jax.ShapeDtypeStruct((nb, S, D), BF16),
                   jax.ShapeDtypeStruct((nb, 1, D), F32), jax.ShapeDtypeStruct((1, D), F32),
                   jax.ShapeDtypeStruct((1, LANE), F32)],
        compiler_params=_cparams("arbitrary", "arbitrary"),
    )(x1, ffn, target, gate2, post_g)


def _rotate_half(t):
    lane = lax.broadcasted_iota(jnp.int32, t.shape, 1)
    return jnp.where((lane & 15) < 8, -pltpu.roll(t, LANE - 8, 1), pltpu.roll(t, 8, 1))


def rope_call(name, x, width, colblk, cos, sin, out_dtype, tr):
    nb = x.shape[0]
    R = cos.shape[0]
    nblk = R // tr

    def body(x_ref, c_ref, s_ref, o_ref):
        c, s = c_ref[...], s_ref[...]
        for h in range(width // LANE):
            t = x_ref[0, :, h * LANE:(h + 1) * LANE].astype(F32)
            o_ref[0, :, h * LANE:(h + 1) * LANE] = (t * c + _rotate_half(t) * s).astype(o_ref.dtype)

    tab = pl.BlockSpec((tr, LANE), lambda b, i: (i, 0))
    return pl.pallas_call(
        body, name=name, grid=(nb, nblk),
        in_specs=[pl.BlockSpec((1, tr, width), lambda b, i: (b, i, colblk)), tab, tab],
        out_specs=pl.BlockSpec((1, tr, width), lambda b, i: (b, i, 0)),
        out_shape=jax.ShapeDtypeStruct((nb, R, width), out_dtype),
        compiler_params=_cparams("arbitrary", "arbitrary"),
    )(x, cos, sin)


def rope_tables(n_ctx, seq):
    n_rows = seq // GRID_W
    row = np.repeat(np.arange(n_rows), GRID_W).astype(np.float32)
    col = np.tile(np.arange(GRID_W), n_rows).astype(np.float32)
    axis_dim = ROPE // 2
    inv_freq = jnp.asarray(ROPE_THETA, F32) ** (-jnp.arange(0, axis_dim, 2, dtype=F32) / axis_dim)
    ang_r = jnp.asarray(row)[:, None] * inv_freq
    ang_c = jnp.asarray(col)[:, None] * inv_freq
    ang = jnp.concatenate([ang_r, ang_r, ang_c, ang_c], axis=-1)
    cos = jnp.ones((n_ctx + seq, LANE), F32).at[n_ctx:, KR_LANE:KR_LANE + ROPE].set(jnp.cos(ang))
    sin = jnp.zeros((n_ctx + seq, LANE), F32).at[n_ctx:, KR_LANE:KR_LANE + ROPE].set(jnp.sin(ang))
    return cos, sin


Q_PRESCALE = ATTN_SCALE * math.log2(math.e)


def _attn_weights(q, kc):
    s2 = lax.dot_general(q, kc, (((1,), (1,)), ((), ())), preferred_element_type=F32)
    e = jnp.exp2(s2 - jnp.max(s2, axis=1, keepdims=True))
    return e, 1.0 / jnp.sum(e, axis=1, keepdims=True)


def _key_block(kv, kr):
    lane = lax.broadcasted_iota(jnp.int32, kv.shape, 1)
    return jnp.where(lane < NOPE, kv, kr)


def _rotated_query(q_ref, cos_ref, sin_ref):
    t = q_ref[0].astype(F32)
    return (t * cos_ref[...] + _rotate_half(t) * sin_ref[...]).astype(BF16)


def attn_fwd(q_raw, kv, kr, cos_q, sin_q, tq):
    nb, S, _ = q_raw.shape
    T = kv.shape[1]

    def body(q_ref, kv_ref, kr_ref, c_ref, s_ref, o_ref):
        kvv = kv_ref[0]
        e, r = _attn_weights(_rotated_query(q_ref, c_ref, s_ref), _key_block(kvv, kr_ref[0]))
        o = lax.dot_general(e.astype(BF16), kvv, (((1,), (0,)), ((), ())), preferred_element_type=F32) * r
        lane = lax.broadcasted_iota(jnp.int32, o.shape, 1)
        o_ref[0] = jnp.where(lane >= NOPE, o, 0.0).astype(o_ref.dtype)

    return pl.pallas_call(
        body, name="attn_fwd", grid=(nb, N_HEADS, S // tq),
        in_specs=[pl.BlockSpec((1, tq, HEAD_BLOCK), lambda b, h, i: (b, i, h)),
                  pl.BlockSpec((1, T, HEAD_BLOCK), lambda b, h, i: (b, 0, h)),
                  pl.BlockSpec((1, T, HEAD_BLOCK), lambda b, h, i: (b, 0, 0)),
                  pl.BlockSpec((tq, LANE), lambda b, h, i: (i, 0)), pl.BlockSpec((tq, LANE), lambda b, h, i: (i, 0))],
        out_specs=pl.BlockSpec((1, tq, HEAD_BLOCK), lambda b, h, i: (b, i, h)),
        out_shape=jax.ShapeDtypeStruct((nb, S, QP), BF16),
        compiler_params=_cparams("arbitrary", "arbitrary", "arbitrary"),
    )(q_raw, kv, kr, cos_q, sin_q)


def attn_bwd(q_raw, kv, kr, do, cos_q, sin_q, cos, sin, tq):
    nb, S, _ = q_raw.shape
    T = kv.shape[1]

    def body(q_ref, kv_ref, kr_ref, do_ref, cq_ref, sq_ref, c_ref, s_ref, dq_ref, dkv_ref, dkr_ref):
        h, i = pl.program_id(1), pl.program_id(2)

        @pl.when(i == 0)
        def _():
            dkv_ref[...] = jnp.zeros_like(dkv_ref)

        @pl.when(jnp.logical_and(h == 0, i == 0))
        def _():
            dkr_ref[...] = jnp.zeros_like(dkr_ref)

        qv, kvv, dov = _rotated_query(q_ref, cq_ref, sq_ref), kv_ref[0], do_ref[0]
        kc = _key_block(kvv, kr_ref[0])
        e, r = _attn_weights(qv, kc)
        dor = (dov.astype(F32) * r).astype(BF16)
        dpr = lax.dot_general(dor, kvv, (((1,), (1,)), ((), ())), preferred_element_type=F32)
        ds = (e * (dpr - r * jnp.sum(dpr * e, axis=1, keepdims=True))).astype(BF16)
        dq = lax.dot_general(ds, kc, (((1,), (0,)), ((), ())), preferred_element_type=F32) * ATTN_SCALE
        dq_ref[0] = (dq * c_ref[...] - _rotate_half(dq) * s_ref[...]).astype(dq_ref.dtype)
        dkc = lax.dot_general(ds, qv, (((0,), (0,)), ((), ())), preferred_element_type=F32) * math.log(2.0)
        dv = lax.dot_general(e.astype(BF16), dor, (((0,), (0,)), ((), ())), preferred_element_type=F32)
        lane = lax.broadcasted_iota(jnp.int32, dkc.shape, 1)
        dkv_ref[0] += jnp.where(lane < NOPE, dkc, dv)
        dkr_ref[0] += jnp.where(lane >= NOPE, dkc, 0.0)

    qspec = pl.BlockSpec((1, tq, HEAD_BLOCK), lambda b, h, i: (b, i, h))
    kspec = pl.BlockSpec((1, T, HEAD_BLOCK), lambda b, h, i: (b, 0, h))
    rspec = pl.BlockSpec((1, T, HEAD_BLOCK), lambda b, h, i: (b, 0, 0))
    tab = pl.BlockSpec((tq, LANE), lambda b, h, i: (i, 0))
    return pl.pallas_call(
        body, name="attn_bwd", grid=(nb, N_HEADS, S // tq),
        in_specs=[qspec, kspec, rspec, qspec, tab, tab, tab, tab], out_specs=[qspec, kspec, rspec],
        out_shape=[jax.ShapeDtypeStruct((nb, S, QP), BF16), jax.ShapeDtypeStruct((nb, T, QP), F32),
                   jax.ShapeDtypeStruct((nb, T, HEAD_BLOCK), F32)],
        compiler_params=_cparams("arbitrary", "arbitrary", "arbitrary"),
    )(q_raw, kv, kr, do, cos_q, sin_q, cos, sin)


CONV_HALO = 8


def _segments(n, n_ctx):
    if n_ctx == 0:
        return [(0, n, CONV_HALO)]
    return [(0, n_ctx, CONV_HALO), (n_ctx, n - n_ctx, 2 * CONV_HALO + n_ctx)]


def _halo_scratch(n, n_ctx, tc):
    return pltpu.VMEM((n + CONV_HALO * (len(_segments(n, n_ctx)) + 1), tc), F32)


def _zero_halos(scr, segs):
    z = jnp.zeros((CONV_HALO, scr.shape[1]), scr.dtype)
    scr[0:CONV_HALO, :] = z
    for (_, rows, off) in segs:
        scr[off + rows:off + rows + CONV_HALO, :] = z


CONV_BLOCK_MAX = 256


def _conv_block(n, n_ctx):
    return _tile(math.gcd(n_ctx, n - n_ctx) if n_ctx else n, CONV_BLOCK_MAX, 8)


def _window(scr, off, r0, blk):
    return scr[pl.ds(pl.multiple_of(off - CONV_HALO + r0, 8), blk + 2 * CONV_HALO), :]


def _shifted(win, s):
    v = win if s == 0 else pltpu.roll(win, (-s) % win.shape[0], 0)
    return v[CONV_HALO:win.shape[0] - CONV_HALO]


def _tap_blocks(win, k, sign):
    return [_shifted(win, sign * (o - k // 2)) for o in range(k)]


def _taps(blocks, w):
    acc = None
    for o, blk in enumerate(blocks):
        t = w[o:o + 1, :] * blk
        acc = t if acc is None else acc + t
    return acc


def _tap_grads(xblocks, dpre):
    k = len(xblocks)
    sub8 = lax.broadcasted_iota(jnp.int32, (8, dpre.shape[1]), 0)
    out = jnp.where(sub8 == k, jnp.sum(dpre, axis=0, keepdims=True), 0.0)
    for o, blk in enumerate(xblocks):
        out = out + jnp.where(sub8 == o, jnp.sum(dpre * blk, axis=0, keepdims=True), 0.0)
    return out


def _row_blocks(rows, blk, fn, init=0):
    return lax.fori_loop(0, rows // blk, lambda i, c: fn(pl.multiple_of(i * blk, blk), c), init)


def _gelu(x):
    return 0.5 * x * (1.0 + lax.erf(x * (1.0 / math.sqrt(2.0))))


def _gelu_and_grad(x):
    cdf = 0.5 * (1.0 + lax.erf(x * (1.0 / math.sqrt(2.0))))
    return x * cdf, cdf + x * jnp.exp(-0.5 * x * x) * (1.0 / math.sqrt(2.0 * math.pi))


def ssd_conv_fwd(u, w8, bias, n_ctx, tc):
    nb, T, _ = u.shape
    cb0 = OFF_XBC // tc

    segs, blk = _segments(T, n_ctx), _conv_block(T, n_ctx)

    def body(x_ref, w_ref, b_ref, o_ref, xs):
        _zero_halos(xs, segs)
        for (start, rows, off) in segs:
            xs[off:off + rows, :] = x_ref[0, start:start + rows, :]
        w, bias_v = w_ref[...], b_ref[...]
        for (start, rows, off) in segs:
            def block(r0, carry, start=start, off=off):
                pre = bias_v + _taps(_tap_blocks(_window(xs, off, r0, blk), SSD_K, 1), w)
                o_ref[0, pl.ds(pl.multiple_of(start + r0, blk), blk), :] = _silu(pre)
                return carry

            _row_blocks(rows, blk, block)

    return pl.pallas_call(
        body, name="ssd_conv_fwd", grid=(nb, XBC // tc),
        in_specs=[pl.BlockSpec((1, T, tc), lambda b, j: (b, 0, cb0 + j)),
                  pl.BlockSpec((8, tc), lambda b, j: (0, j)), pl.BlockSpec((1, tc), lambda b, j: (0, j))],
        out_specs=pl.BlockSpec((1, T, tc), lambda b, j: (b, 0, j)),
        out_shape=jax.ShapeDtypeStruct((nb, T, XBC), F32),
        scratch_shapes=[_halo_scratch(T, n_ctx, tc)],
        compiler_params=_cparams("arbitrary", "arbitrary"),
    )(u, w8, bias)


def ssd_conv_bwd(u, w8, bias, dxbc, dxs_direct, n_ctx, tc):
    nb, T, _ = u.shape
    cb0 = OFF_XBC // tc
    n_direct = D_INNER // tc

    segs, blk = _segments(T, n_ctx), _conv_block(T, n_ctx)

    def body(x_ref, w_ref, b_ref, d0_ref, d1_ref, dd_ref, dx_ref, dw_ref, xs, ds):
        j, b = pl.program_id(0), pl.program_id(1)
        _zero_halos(xs, segs)
        _zero_halos(ds, segs)
        for (start, rows, off) in segs:
            xs[off:off + rows, :] = x_ref[0, start:start + rows, :]
        w, bias_v = w_ref[...], b_ref[...]
        has_direct = (j < n_direct).astype(F32)
        rows = jnp.zeros((8, tc), F32)
        for (start, n_rows, off) in segs:
            def block(r0, acc, start=start, off=off):
                xblocks = _tap_blocks(_window(xs, off, r0, blk), SSD_K, 1)
                pre = bias_v + _taps(xblocks, w)
                d = d0_ref[0, 0, pl.ds(pl.multiple_of(start + r0, blk), blk), :] + d1_ref[0, 0, pl.ds(pl.multiple_of(start + r0, blk), blk), :]
                if start == n_ctx:
                    d = d + dd_ref[0, pl.ds(r0, blk), :] * has_direct
                sg = jax.nn.sigmoid(pre)
                dpre = d * (sg * (1.0 + pre * (1.0 - sg)))
                ds[pl.ds(pl.multiple_of(off + r0, 8), blk), :] = dpre
                return acc + _tap_grads(xblocks, dpre)

            rows = _row_blocks(n_rows, blk, block, rows)
        for (start, n_rows, off) in segs:
            def block_dx(r0, carry, start=start, off=off):
                dx_ref[0, pl.ds(pl.multiple_of(start + r0, blk), blk), :] = _taps(_tap_blocks(_window(ds, off, r0, blk), SSD_K, -1), w).astype(dx_ref.dtype)
                return carry

            _row_blocks(n_rows, blk, block_dx)

        @pl.when(b == 0)
        def _():
            dw_ref[...] = rows

        @pl.when(b > 0)
        def _():
            dw_ref[...] += rows

    dspec0 = pl.BlockSpec((1, 1, T, tc), lambda j, b: (0, b, 0, j))
    dspec1 = pl.BlockSpec((1, 1, T, tc), lambda j, b: (1, b, 0, j))
    return pl.pallas_call(
        body, name="ssd_conv_bwd", grid=(XBC // tc, nb),
        in_specs=[pl.BlockSpec((1, T, tc), lambda j, b: (b, 0, cb0 + j)),
                  pl.BlockSpec((8, tc), lambda j, b: (0, j)), pl.BlockSpec((1, tc), lambda j, b: (0, j)),
                  dspec0, dspec1,
                  pl.BlockSpec((1, T - n_ctx, tc), lambda j, b: (b, 0, jnp.minimum(j, n_direct - 1)))],
        out_specs=[pl.BlockSpec((1, T, tc), lambda j, b: (b, 0, j)), pl.BlockSpec((8, tc), lambda j, b: (0, j))],
        out_shape=[jax.ShapeDtypeStruct((nb, T, XBC), BF16), jax.ShapeDtypeStruct((8, XBC), F32)],
        scratch_shapes=[_halo_scratch(T, n_ctx, tc), _halo_scratch(T, n_ctx, tc)],
        compiler_params=_cparams("arbitrary", "arbitrary"),
    )(u, w8, bias, dxbc, dxbc, dxs_direct)


GLU_TC = 256


def glu_interleave(w_up):
    blocks = []
    for j in range(D_FF // GLU_TC):
        blocks += [w_up[:, j * GLU_TC:(j + 1) * GLU_TC], w_up[:, D_FF + j * GLU_TC:D_FF + (j + 1) * GLU_TC]]
    return jnp.concatenate(blocks, axis=1)


def glu_deinterleave(g):
    nj = D_FF // GLU_TC
    gate = [g[:, 2 * j * GLU_TC:(2 * j + 1) * GLU_TC] for j in range(nj)]
    val = [g[:, (2 * j + 1) * GLU_TC:(2 * j + 2) * GLU_TC] for j in range(nj)]
    return jnp.concatenate(gate + val, axis=1)


def glu_fwd(up, w8, bias):
    nb, S, _ = up.shape
    tc = GLU_TC

    segs, blk = _segments(S, 0), _conv_block(S, 0)
    (_, _, off), = segs

    def body(u_ref, w_ref, b_ref, o_ref, xs):
        _zero_halos(xs, segs)
        xs[off:off + S, :] = u_ref[0, :, :tc]
        w, bias_v = w_ref[...], b_ref[...]

        def block(r0, carry):
            gc = bias_v + _taps(_tap_blocks(_window(xs, off, r0, blk), FFN_K, 1), w)
            o_ref[0, pl.ds(r0, blk), :] = (_gelu(gc) * u_ref[0, pl.ds(r0, blk), tc:]).astype(o_ref.dtype)
            return carry

        _row_blocks(S, blk, block)

    return pl.pallas_call(
        body, name="glu_fwd", grid=(nb, D_FF // tc),
        in_specs=[pl.BlockSpec((1, S, 2 * tc), lambda b, j: (b, 0, j)),
                  pl.BlockSpec((8, tc), lambda b, j: (0, j)), pl.BlockSpec((1, tc), lambda b, j: (0, j))],
        out_specs=pl.BlockSpec((1, S, tc), lambda b, j: (b, 0, j)),
        out_shape=jax.ShapeDtypeStruct((nb, S, D_FF), BF16),
        scratch_shapes=[_halo_scratch(S, 0, tc)],
        compiler_params=_cparams("arbitrary", "arbitrary"),
    )(up, w8, bias)


def glu_bwd(up, w8, bias, dact):
    nb, S, _ = up.shape
    tc = GLU_TC

    segs, blk = _segments(S, 0), _conv_block(S, 0)
    (_, _, off), = segs

    def body(u_ref, w_ref, b_ref, d_ref, du_ref, dw_ref, xs, ds):
        b = pl.program_id(1)
        _zero_halos(xs, segs)
        _zero_halos(ds, segs)
        xs[off:off + S, :] = u_ref[0, :, :tc]
        w, bias_v = w_ref[...], b_ref[...]

        def block(r0, acc):
            here = pl.ds(r0, blk)
            xblocks = _tap_blocks(_window(xs, off, r0, blk), FFN_K, 1)
            act, act_grad = _gelu_and_grad(bias_v + _taps(xblocks, w))
            d = d_ref[0, here, :].astype(F32)
            du_ref[0, here, tc:] = (d * act).astype(du_ref.dtype)
            dpre = d * u_ref[0, here, tc:] * act_grad
            ds[pl.ds(pl.multiple_of(off + r0, 8), blk), :] = dpre
            return acc + _tap_grads(xblocks, dpre)

        rows = _row_blocks(S, blk, block, jnp.zeros((8, tc), F32))

        def block_dx(r0, carry):
            du_ref[0, pl.ds(r0, blk), :tc] = _taps(_tap_blocks(_window(ds, off, r0, blk), FFN_K, -1), w).astype(du_ref.dtype)
            return carry

        _row_blocks(S, blk, block_dx)

        @pl.when(b == 0)
        def _():
            dw_ref[...] = rows

        @pl.when(b > 0)
        def _():
            dw_ref[...] += rows

    pair = pl.BlockSpec((1, S, 2 * tc), lambda j, b: (b, 0, j))
    return pl.pallas_call(
        body, name="glu_bwd", grid=(D_FF // tc, nb),
        in_specs=[pair, pl.BlockSpec((8, tc), lambda j, b: (0, j)), pl.BlockSpec((1, tc), lambda j, b: (0, j)),
                  pl.BlockSpec((1, S, tc), lambda j, b: (b, 0, j))],
        out_specs=[pair, pl.BlockSpec((8, tc), lambda j, b: (0, j))],
        out_shape=[jax.ShapeDtypeStruct((nb, S, 2 * D_FF), BF16), jax.ShapeDtypeStruct((8, D_FF), F32)],
        scratch_shapes=[_halo_scratch(S, 0, tc), _halo_scratch(S, 0, tc)],
        compiler_params=_cparams("arbitrary", "arbitrary"),
    )(up, w8, bias, dact)


def _chunk_of(d, k, n_cc, n_ch):
    rev = jnp.where(k < n_cc, n_cc - 1 - k, n_cc + n_ch - 1 - k)
    return jnp.where(d == 1, rev, k)


def _lane_pick(v, lane_iota, l):
    return jnp.sum(jnp.where(lane_iota == l, v, 0.0), axis=1, keepdims=True)


def head_spread_matrix():
    return (jnp.arange(LANE)[:, None] == (jnp.arange(D_INNER)[None, :] // SSD_P)).astype(BF16)


def _split_dot(x, e, dims):
    hi = x.astype(BF16)
    lo = (x - hi.astype(F32)).astype(BF16)
    return (lax.dot_general(hi, e, dims, preferred_element_type=F32)
            + lax.dot_general(lo, e, dims, preferred_element_type=F32))


def _spread(x, e):
    return _split_dot(x, e, (((1,), (0,)), ((), ())))


def _gather_heads(y, e):
    return _split_dot(y, e, (((1,), (1,)), ((), ())))


def _softplus(x):
    return jnp.maximum(x, 0.0) + jnp.log(1.0 + jnp.exp(-jnp.abs(x)))


def ssd_dt_inputs(u, a_log, dt_bias):
    pad = LANE - SSD_HEADS
    dt = u[..., OFF_DT:OFF_DT + 2 * SSD_HEADS]
    dt2 = jnp.stack([jnp.pad(dt[..., i * SSD_HEADS:(i + 1) * SSD_HEADS], ((0, 0), (0, 0), (0, pad))) for i in range(2)])

    def lanes(v):
        return jnp.pad(v.reshape(2, 1, SSD_HEADS), ((0, 0), (0, 0), (0, pad)))

    return dt2, lanes(a_log), lanes(dt_bias)


def _ssd_common(d, dt_raw, alog, dtb):
    Q = dt_raw.shape[0]
    row = lax.broadcasted_iota(jnp.int32, (Q, Q), 0)
    col = lax.broadcasted_iota(jnp.int32, (Q, Q), 1)
    rev = d == 1
    maskb = jnp.where(rev, row, col) <= jnp.where(rev, col, row)
    tri = maskb.astype(F32)
    A = -jnp.exp(alog)
    dtv = _softplus(dt_raw + dtb)
    a = dtv * A
    cum = lax.dot_general(tri, a, (((1,), (0,)), ((), ())), precision=lax.Precision.HIGHEST, preferred_element_type=F32)
    tot = jnp.sum(a, axis=0, keepdims=True)
    return maskb, tri, A, dtv, cum, tot


def ssd_fwd(xbc, dt2, alog2, dtb2, n_ctx, hosted):
    nb, T, _ = xbc.shape
    S = T - n_ctx
    n_ch, n_cc = T // CHUNK, n_ctx // CHUNK
    Q = CHUNK
    n_pairs = SSD_HEADS // 2
    n_ex = hosted.n
    n_in = 5

    def body(*refs):
        x_ref, dt_ref, al_ref, db_ref, e_ref = refs[:n_in]
        send_refs = refs[n_in:n_in + n_ex]
        y_ref, hin_ref = refs[n_in + n_ex:n_in + 2 + n_ex]
        recv_refs = refs[n_in + 2 + n_ex:n_in + 2 + 2 * n_ex]
        H, *sems = refs[n_in + 2 + 2 * n_ex:]
        d, k = pl.program_id(1), pl.program_id(2)
        first_step = jnp.logical_and(jnp.logical_and(pl.program_id(0) == 0, d == 0), k == 0)
        last_step = jnp.logical_and(jnp.logical_and(pl.program_id(0) == nb - 1, d == 1), k == n_ch - 1)
        begin_exchange, end_exchange = hosted.steps(send_refs, recv_refs, sems, first_step, last_step)
        begin_exchange()

        @pl.when(k == 0)
        def _():
            H[...] = jnp.zeros_like(H)

        maskb, tri, A, dtv, cum, tot = _ssd_common(d, dt_ref[0, 0], al_ref[0], db_ref[0])
        e = e_ref[...]
        cumT = cum.T
        cum_e, dt_e = _spread(cum, e), _spread(dtv, e)
        tot_e = _spread(jnp.broadcast_to(tot, (8, LANE)), e)[0:1]
        hin_ref[0, 0, 0] = H[...].astype(BF16)
        lane = lax.broadcasted_iota(jnp.int32, (Q, LANE), 1)
        lane1 = lax.broadcasted_iota(jnp.int32, (1, LANE), 1)
        subc = lax.broadcasted_iota(jnp.int32, (LANE, 1), 0)
        half = lane < SSD_P
        for g in range(SSD_GROUPS):
            Bg = x_ref[0, :, D_INNER + g * SSD_N:D_INNER + (g + 1) * SSD_N].astype(BF16)
            Cg = x_ref[0, :, D_INNER + GN + g * SSD_N:D_INNER + GN + (g + 1) * SSD_N].astype(BF16)
            Gm = lax.dot_general(Cg, Bg, (((1,), (1,)), ((), ())), preferred_element_type=F32)
            for pr in range(n_pairs // SSD_GROUPS):
                p = g * (n_pairs // SSD_GROUPS) + pr
                sc, dtp, totp = [t[:, p * LANE:(p + 1) * LANE] for t in (cum_e, dt_e, tot_e)]
                swapped = pltpu.roll(sc, SSD_P, 1)
                s0c, s1c = jnp.where(half, sc, swapped), jnp.where(half, swapped, sc)
                s0r, s1r = cumT[2 * p:2 * p + 1, :], cumT[2 * p + 1:2 * p + 2, :]
                tot0, tot1 = _lane_pick(tot, lane1, 2 * p), _lane_pick(tot, lane1, 2 * p + 1)
                M0 = (Gm * jnp.exp(jnp.where(maskb, s0c - s0r, NEG_BIG))).astype(BF16)
                M1 = (Gm * jnp.exp(jnp.where(maskb, s1c - s1r, NEG_BIG))).astype(BF16)
                xd = x_ref[0, :, p * LANE:(p + 1) * LANE] * dtp
                xdb = xd.astype(BF16)
                yd = jnp.where(half,
                               lax.dot_general(M0, xdb, (((1,), (0,)), ((), ())), preferred_element_type=F32),
                               lax.dot_general(M1, xdb, (((1,), (0,)), ((), ())), preferred_element_type=F32))
                Hp = H[p * LANE:(p + 1) * LANE, :]
                yo = lax.dot_general(Cg, Hp.astype(BF16), (((1,), (1,)), ((), ())), preferred_element_type=F32) * jnp.exp(sc)

                y_ref[0, 0, :, p * LANE:(p + 1) * LANE] = yd + yo

                xdw = (xd * jnp.exp(totp - sc)).astype(BF16)
                etot = jnp.exp(jnp.where(subc < SSD_P, tot0, tot1))
                H[p * LANE:(p + 1) * LANE, :] = Hp * etot + lax.dot_general(
                    xdw, Bg, (((0,), (0,)), ((), ())), preferred_element_type=F32)
        end_exchange()

    def ymap(b, d, k):
        return (d, b, _chunk_of(d, jnp.maximum(k, n_cc), n_cc, n_ch) - n_cc, 0)

    return pl.pallas_call(
        body, name="ssd_fwd", grid=(nb, 2, n_ch),
        in_specs=[pl.BlockSpec((1, Q, XBC), lambda b, d, k: (b, _chunk_of(d, k, n_cc, n_ch), 0)),
                  pl.BlockSpec((1, 1, Q, LANE), lambda b, d, k: (d, b, _chunk_of(d, k, n_cc, n_ch), 0)),
                  pl.BlockSpec((1, 1, LANE), lambda b, d, k: (d, 0, 0)), pl.BlockSpec((1, 1, LANE), lambda b, d, k: (d, 0, 0)),
                  pl.BlockSpec((LANE, D_INNER), lambda b, d, k: (0, 0))] + hosted.specs,
        out_specs=[pl.BlockSpec((1, 1, Q, D_INNER), ymap),
                   pl.BlockSpec((1, 1, 1, D_INNER, SSD_N), lambda b, d, k: (d, b, k, 0, 0))] + hosted.specs,
        out_shape=[jax.ShapeDtypeStruct((2, nb, S, D_INNER), F32),
                   jax.ShapeDtypeStruct((2, nb, n_ch, D_INNER, SSD_N), BF16)] + hosted.out_shape,
        scratch_shapes=[pltpu.VMEM((D_INNER, SSD_N), F32)] + hosted.scratch,
        compiler_params=_cparams("arbitrary", "arbitrary", "arbitrary"),
    )(xbc, dt2, alog2, dtb2, head_spread_matrix(), *hosted.arrays)


def ssd_bwd(xbc, dt2, alog2, dtb2, hin, dy, n_ctx, hosted):
    nb, T, _ = xbc.shape
    n_ex = hosted.n
    n_ch, n_cc = T // CHUNK, n_ctx // CHUNK
    n_in = 7
    Q = CHUNK
    n_pairs = SSD_HEADS // 2
    NT = (((1,), (1,)), ((), ()))
    NN = (((1,), (0,)), ((), ()))
    TN = (((0,), (0,)), ((), ()))

    def dot(a, b, dims):
        return lax.dot_general(a.astype(BF16), b.astype(BF16), dims, preferred_element_type=F32)

    def body(*refs):
        x_ref, dt_ref, al_ref, db_ref, e_ref, hin_ref, dy_ref = refs[:n_in]
        send_refs = refs[n_in:n_in + n_ex]
        dx_ref, ddt_ref, st_ref = refs[n_in + n_ex:n_in + 3 + n_ex]
        recv_refs = refs[n_in + 3 + n_ex:n_in + 3 + 2 * n_ex]
        dH, dce, dde, *sems = refs[n_in + 3 + 2 * n_ex:]
        d, kk = pl.program_id(1), pl.program_id(2)
        ks = n_ch - 1 - kk
        first_step = jnp.logical_and(jnp.logical_and(pl.program_id(0) == 0, d == 0), kk == 0)
        last_step = jnp.logical_and(jnp.logical_and(pl.program_id(0) == nb - 1, d == 1), kk == n_ch - 1)
        begin_exchange, end_exchange = hosted.steps(send_refs, recv_refs, sems, first_step, last_step)
        begin_exchange()

        @pl.when(kk == 0)
        def _():
            dH[...] = jnp.zeros_like(dH)

        @pl.when(jnp.logical_and(jnp.logical_and(pl.program_id(0) == 0, d == 0), kk == 0))
        def _():
            st_ref[...] = jnp.zeros_like(st_ref)

        dt_raw = dt_ref[0, 0]
        alog, dtb_v = al_ref[0], db_ref[0]
        maskb, tri, A, dtv, cum, tot = _ssd_common(d, dt_raw, alog, dtb_v)
        e = e_ref[...]
        cumT = cum.T
        cum_e, dt_e = _spread(cum, e), _spread(dtv, e)
        tot_e = _spread(jnp.broadcast_to(tot, (8, LANE)), e)[0:1]
        live = (ks >= n_cc).astype(F32)
        lane = lax.broadcasted_iota(jnp.int32, (Q, LANE), 1)
        lane1 = lax.broadcasted_iota(jnp.int32, (1, LANE), 1)
        sub = lax.broadcasted_iota(jnp.int32, (LANE, Q), 0)
        subc = lax.broadcasted_iota(jnp.int32, (LANE, 1), 0)
        half = lane < SSD_P
        halfc = subc < SSD_P
        ones = jnp.ones((LANE, LANE), BF16)
        dcum = jnp.zeros((Q, LANE), F32)
        dcumT = jnp.zeros((LANE, Q), F32)
        dtot = jnp.zeros((1, LANE), F32)
        dtot_parts = []
        for g in range(SSD_GROUPS):
            Bg = x_ref[0, :, D_INNER + g * SSD_N:D_INNER + (g + 1) * SSD_N].astype(BF16)
            Cg = x_ref[0, :, D_INNER + GN + g * SSD_N:D_INNER + GN + (g + 1) * SSD_N].astype(BF16)
            Gm = lax.dot_general(Cg, Bg, NT, preferred_element_type=F32)
            dG = jnp.zeros((Q, Q), F32)
            dC = jnp.zeros((Q, SSD_N), F32)
            dB = jnp.zeros((Q, SSD_N), F32)
            for pr in range(n_pairs // SSD_GROUPS):
                p = g * (n_pairs // SSD_GROUPS) + pr
                l0, l1 = 2 * p, 2 * p + 1
                sc, dtp, totp = [t[:, p * LANE:(p + 1) * LANE] for t in (cum_e, dt_e, tot_e)]
                swapped = pltpu.roll(sc, SSD_P, 1)
                s0c, s1c = jnp.where(half, sc, swapped), jnp.where(half, swapped, sc)
                s0r, s1r = cumT[l0:l0 + 1, :], cumT[l1:l1 + 1, :]
                tot0, tot1 = _lane_pick(tot, lane1, l0), _lane_pick(tot, lane1, l1)
                L0 = jnp.exp(jnp.where(maskb, s0c - s0r, NEG_BIG))
                L1 = jnp.exp(jnp.where(maskb, s1c - s1r, NEG_BIG))
                M0, M1 = Gm * L0, Gm * L1
                xs = x_ref[0, :, p * LANE:(p + 1) * LANE]
                xd = xs * dtp
                es = jnp.exp(sc)
                dte = jnp.exp(totp - sc)
                etot = jnp.exp(jnp.where(halfc, tot0, tot1))
                dyp = dy_ref[0, :, p * LANE:(p + 1) * LANE] * live
                Hp = hin_ref[0, 0, 0, p * LANE:(p + 1) * LANE, :]
                dHp = dH[p * LANE:(p + 1) * LANE, :]
                bdh = dot(Bg, dHp, NT)
                dxd = jnp.where(half, dot(M0, dyp, TN), dot(M1, dyp, TN)) + bdh * dte
                dy0 = jnp.where(half, dyp, 0.0)
                dy1 = dyp - dy0
                dM0, dM1 = dot(dy0, xd, NT), dot(dy1, xd, NT)
                dG = dG + dM0 * L0 + dM1 * L1
                dyes = dyp * es
                xdw = xd * dte
                dC = dC + dot(dyes, Hp, NN)
                dB = dB + dot(xdw, dHp, NN)
                W0, W1 = dM0 * M0, dM1 * M1
                yoff = dot(Cg, Hp, NT) * es
                r_off = dyp * yoff
                r_st = xd * bdh * dte
                hh = jnp.sum(dHp * Hp.astype(F32), axis=1, keepdims=True) * etot
                dce[:, p * LANE:(p + 1) * LANE] = r_off - r_st
                dde[:, p * LANE:(p + 1) * LANE] = dxd * xs
                dtot_parts.append(jnp.sum(r_st, axis=0, keepdims=True))
                for (l, W, hselc) in ((l0, W0, halfc), (l1, W1, jnp.logical_not(halfc))):
                    col_g = _split_dot(W, ones, NN)
                    row_g = -jnp.sum(W, axis=0, keepdims=True)
                    dcum = dcum + jnp.where(lane == l, col_g, 0.0)
                    dcumT = dcumT + jnp.where(sub == l, row_g, 0.0)
                    dtot = dtot + jnp.where(lane1 == l, jnp.sum(jnp.where(hselc, hh, 0.0), axis=0, keepdims=True), 0.0)
                dx_ref[0, 0, :, p * LANE:(p + 1) * LANE] = dxd * dtp
                dH[p * LANE:(p + 1) * LANE, :] = dHp * etot + dot(dyes, Cg, TN)
            dx_ref[0, 0, :, D_INNER + g * SSD_N:D_INNER + (g + 1) * SSD_N] = dB + dot(dG, Cg, TN)
            dx_ref[0, 0, :, D_INNER + GN + g * SSD_N:D_INNER + GN + (g + 1) * SSD_N] = dC + dot(dG, Bg, NN)
        dcum_all = dcum + dcumT.T + _gather_heads(dce[...], e)
        dtot_e = jnp.broadcast_to(jnp.concatenate(dtot_parts, axis=1), (8, D_INNER))
        dtot = dtot + _gather_heads(dtot_e, e)[0:1]
        da = lax.dot_general(tri, dcum_all, TN, precision=lax.Precision.HIGHEST, preferred_element_type=F32) + dtot
        ddtv = _gather_heads(dde[...], e) + da * A
        ddt_raw = ddtv * jax.nn.sigmoid(dt_raw + dtb_v)
        ddt_ref[0, 0] = ddt_raw
        sub8 = lax.broadcasted_iota(jnp.int32, (8, LANE), 0)
        st_ref[...] += (jnp.where(sub8 == 2 * d, jnp.sum(da * dtv * A, axis=0, keepdims=True), 0.0)
                        + jnp.where(sub8 == 2 * d + 1, jnp.sum(ddt_raw, axis=0, keepdims=True), 0.0))
        end_exchange()

    def cmap(d, kk):
        return _chunk_of(d, n_ch - 1 - kk, n_cc, n_ch)

    def dymap(b, d, kk):
        return (b, _chunk_of(d, jnp.maximum(n_ch - 1 - kk, n_cc), n_cc, n_ch) - n_cc, 0)

    return pl.pallas_call(
        body, name="ssd_bwd", grid=(nb, 2, n_ch),
        in_specs=[pl.BlockSpec((1, Q, XBC), lambda b, d, kk: (b, cmap(d, kk), 0)),
                  pl.BlockSpec((1, 1, Q, LANE), lambda b, d, kk: (d, b, cmap(d, kk), 0)),
                  pl.BlockSpec((1, 1, LANE), lambda b, d, kk: (d, 0, 0)), pl.BlockSpec((1, 1, LANE), lambda b, d, kk: (d, 0, 0)),
                  pl.BlockSpec((LANE, D_INNER), lambda b, d, kk: (0, 0)),
                  pl.BlockSpec((1, 1, 1, D_INNER, SSD_N), lambda b, d, kk: (d, b, n_ch - 1 - kk, 0, 0)),
                  pl.BlockSpec((1, Q, D_INNER), dymap)] + hosted.specs,
        out_specs=[pl.BlockSpec((1, 1, Q, XBC), lambda b, d, kk: (d, b, cmap(d, kk), 0)),
                   pl.BlockSpec((1, 1, Q, LANE), lambda b, d, kk: (d, b, cmap(d, kk), 0)),
                   pl.BlockSpec((8, LANE), lambda b, d, kk: (0, 0))] + hosted.specs,
        out_shape=[jax.ShapeDtypeStruct((2, nb, T, XBC), F32), jax.ShapeDtypeStruct((2, nb, T, LANE), F32),
                   jax.ShapeDtypeStruct((8, LANE), F32)] + hosted.out_shape,
        scratch_shapes=[pltpu.VMEM((D_INNER, SSD_N), F32), pltpu.VMEM((Q, D_INNER), F32), pltpu.VMEM((Q, D_INNER), F32)] + hosted.scratch,
        compiler_params=_cparams("arbitrary", "arbitrary", "arbitrary"),
    )(xbc, dt2, alog2, dtb2, head_spread_matrix(), hin, dy, *hosted.arrays)


def _adamw(w, g, m, v):
    mn = ADAM_B1 * m + (1.0 - ADAM_B1) * g
    vn = ADAM_B2 * v + (1.0 - ADAM_B2) * jnp.square(g)
    m_hat = mn / (1.0 - ADAM_B1 ** ADAM_STEP)
    v_hat = vn / (1.0 - ADAM_B2 ** ADAM_STEP)
    return -ADAM_LR * (m_hat / (jnp.sqrt(v_hat) + ADAM_EPS) + ADAM_WD * w), mn, vn


def adamw_matrix(name, w, g_slots, m, v):
    K, n = w.shape
    s = g_slots.shape[0]
    tr = _tile(K, 256, 8)

    def body(w_ref, g_ref, m_ref, v_ref, go_ref, d_ref, mo_ref, vo_ref):
        g = g_ref[0].astype(F32)
        for j in range(1, s):
            g = g + g_ref[j].astype(F32)
        go_ref[...] = g
        d_ref[...], mo_ref[...], vo_ref[...] = _adamw(w_ref[...], g, m_ref[...], v_ref[...])

    spec = pl.BlockSpec((tr, n), lambda i: (i, 0))
    return pl.pallas_call(
        body, name=name, grid=(K // tr,),
        in_specs=[spec, pl.BlockSpec((s, tr, n), lambda i: (0, i, 0)), spec, spec], out_specs=[spec] * 4,
        out_shape=[jax.ShapeDtypeStruct((K, n), F32)] * 4,
        compiler_params=_cparams("arbitrary"),
    )(w, g_slots, m, v)


def adamw_small(ws, gs, ms, vs):
    n = len(ws)

    def body(*refs):
        for i in range(n):
            d, mn, vn = _adamw(refs[i][...], refs[n + i][...], refs[2 * n + i][...], refs[3 * n + i][...])
            refs[4 * n + i][...] = d
            refs[5 * n + i][...] = mn
            refs[6 * n + i][...] = vn

    shapes = [jax.ShapeDtypeStruct(w.shape, F32) for w in ws]
    out = pl.pallas_call(body, name="adamw_small", out_shape=shapes * 3)(*ws, *gs, *ms, *vs)
    return out[:n], out[n:2 * n], out[2 * n:]


def sum_slots(name, x):
    n = x.shape[0]

    def fn(t):
        acc = t[0]
        for j in range(1, n):
            acc = acc + t[j]
        return (acc,)

    return ew_call(name, fn, [x], [(x.shape[1:], F32)])[0]


def _pack_rows(parts):
    rows = []
    for p in parts:
        flat = p.reshape(1, -1)
        n = flat.shape[1]
        rows.append(jnp.pad(flat, ((0, 0), (0, -(-n // (8 * LANE)) * 8 * LANE - n))).reshape(-1, LANE))
    return jnp.concatenate(rows, axis=0)


def _unpack_rows(pack, shapes):
    out, r = [], 0
    for s in shapes:
        n = int(np.prod(s))
        nr = -(-n // (8 * LANE)) * 8
        out.append(pack[r:r + nr].reshape(1, -1)[:, :n].reshape(s))
        r += nr
    return out


def _mesh_pos():
    return lax.axis_index("x"), lax.axis_index("y"), lax.axis_index("c")


N_PEERS = N_DEV - 1


def all_gather(name, vs):
    n = len(vs)

    def body(*refs):
        _ag_start(refs[:n], refs[n:2 * n], *refs[2 * n:])
        _ag_finish(refs[:n], refs[n:2 * n], *refs[2 * n:])

    hbm = pl.BlockSpec(memory_space=pl.ANY)
    return pl.pallas_call(
        body, name=name, out_shape=_ag_out_shape(vs), in_specs=[hbm] * n, out_specs=[hbm] * n,
        scratch_shapes=_a2a_scratch(n),
    )(*vs)


def _ag_out_shape(vs):
    return [jax.ShapeDtypeStruct((N_DEV,) + v.shape, v.dtype) for v in vs]


def _ag_copies(x_refs, out_refs, send_sems, recv_sems, local_sems):
    n = len(x_refs)
    x, y, c = _mesh_pos()
    me, sibling = (x, y, c), (x, y, 1 - c)
    chips = [(1 - x, y), (x, 1 - y), (1 - x, 1 - y)]

    def slot(a, px, py, pc):
        return out_refs[a].at[4 * px + 2 * py + pc]

    def copy(a, k, block, to, src=None):
        return pltpu.make_async_remote_copy(
            src_ref=slot(a, *block) if src is None else src, dst_ref=slot(a, *block),
            send_sem=send_sems.at[N_PEERS * a + k], recv_sem=recv_sems.at[N_PEERS * a + k],
            device_id=to, device_id_type=MESH)

    local = [pltpu.make_async_copy(x_refs[a], slot(a, *me), local_sems.at[a]) for a in range(n)]
    first = []
    for a in range(n):
        first.append(copy(a, 0, me, sibling, src=x_refs[a]))
        first += [copy(a, 1 + j, me, (*chip, c), src=x_refs[a]) for j, chip in enumerate(chips)]
    passed = [(copy(a, 1 + j, (*chip, c), me), copy(a, 4 + j, (*chip, c), sibling))
              for j, chip in enumerate(chips) for a in range(n)]
    from_sibling = []
    for a in range(n):
        from_sibling.append(copy(a, 0, sibling, me))
        from_sibling += [copy(a, 4 + j, (*chip, 1 - c), me) for j, chip in enumerate(chips)]
    return local, first, passed, from_sibling


def _ag_start(*refs):
    local, first, _, _ = _ag_copies(*refs)
    for cp in local + first:
        cp.start()


def _ag_finish(*refs):
    local, first, passed, from_sibling = _ag_copies(*refs)
    for arrived, hand_on in passed:
        arrived.wait_recv()
        hand_on.start()
    for cp in from_sibling:
        cp.wait_recv()
    for cp in first + [hand_on for _, hand_on in passed]:
        cp.wait_send()
    for cp in local:
        cp.wait()


def _a2a_scratch(n):
    return [pltpu.SemaphoreType.DMA((N_PEERS * n,)), pltpu.SemaphoreType.DMA((N_PEERS * n,)), pltpu.SemaphoreType.DMA((n,))]


def _a2a_copies(x_refs, out_refs, send_sems, recv_sems, local_sems):
    n = len(x_refs)
    x, y, c = _mesh_pos()
    me = 4 * x + 2 * y + c
    local = [pltpu.make_async_copy(x_refs[a].at[me], out_refs[a].at[me], local_sems.at[a]) for a in range(n)]
    remote = []
    for k in range(1, N_DEV):
        px, py, pc = x ^ ((k >> 2) & 1), y ^ ((k >> 1) & 1), c ^ (k & 1)
        for a in range(n):
            remote.append(pltpu.make_async_remote_copy(
                src_ref=x_refs[a].at[4 * px + 2 * py + pc], dst_ref=out_refs[a].at[me],
                send_sem=send_sems.at[N_PEERS * a + k - 1], recv_sem=recv_sems.at[N_PEERS * a + k - 1],
                device_id=(px, py, pc), device_id_type=MESH))
    return local, remote


def _a2a_start(local, remote):
    for cp in local + remote:
        cp.start()


def _a2a_wait(local, remote):
    for cp in remote:
        cp.wait_recv()
    for cp in remote:
        cp.wait_send()
    for cp in local:
        cp.wait()


class Hosted:
    def __init__(self, start=None, finish=None, arrays=(), out_shape=()):
        self.start, self.finish, self.arrays, self.out_shape = start, finish, list(arrays), list(out_shape)
        self.n = len(self.arrays)
        self.specs = [pl.BlockSpec(memory_space=pl.ANY)] * self.n
        self.scratch = _a2a_scratch(self.n) if self.n else []

    def steps(self, send_refs, recv_refs, sems, first_step, last_step):
        def begin():
            if self.n:
                pl.when(first_step)(lambda: self.start(send_refs, recv_refs, *sems))

        def end():
            if self.n:
                pl.when(last_step)(lambda: self.finish(send_refs, recv_refs, *sems))

        return begin, end


def hosted_all_to_all(vs):
    return Hosted(lambda *r: _a2a_start(*_a2a_copies(*r)), lambda *r: _a2a_wait(*_a2a_copies(*r)), vs,
                  [jax.ShapeDtypeStruct(v.shape, v.dtype) for v in vs])


def hosted_all_gather(vs):
    return Hosted(_ag_start, _ag_finish, vs, _ag_out_shape(vs))


def _taps8(w):
    return jnp.concatenate([w, jnp.zeros((8 - w.shape[0], w.shape[1]), w.dtype)], axis=0)


FIRST = ("w_in",)
LATE_WEIGHTS = ("w_out", "w_up", "w_down", "w_q_up", "w_kv_up")


def first_weights_to_internal(w_in):
    cq, ckv, kr, z, xbc, dt = jnp.split(w_in, np.cumsum(IN_SPLITS)[:-1].tolist(), axis=1)
    K = w_in.shape[0]

    def zeros(n):
        return jnp.zeros((K, n), w_in.dtype)

    w_in_p = jnp.concatenate([cq, zeros(KR_LANE), kr, zeros(LANE - KR_LANE - ROPE), ckv, zeros(OFF_Z - OFF_CKV - KV_RANK),
                              z, xbc, dt, zeros(WIN_P - OFF_DT - 2 * SSD_HEADS)], axis=1)
    return dict(w_in_p=w_in_p)


def late_weights_to_internal(w_out, w_up, w_down, w_q_up, w_kv_up):
    attn_rows = w_out[:N_HEADS * V_DIM].reshape(N_HEADS, V_DIM, -1)
    w_out_p = jnp.concatenate([jnp.pad(attn_rows, ((0, 0), (HEAD_BLOCK - V_DIM, 0), (0, 0))).reshape(QP, -1),
                               w_out[N_HEADS * V_DIM:]], axis=0)
    w_q_p = jnp.pad(w_q_up.reshape(Q_RANK, N_HEADS, NOPE + ROPE), ((0, 0), (0, 0), (0, HEAD_BLOCK - NOPE - ROPE))).reshape(Q_RANK, QP)
    return dict(w_out_p=w_out_p, w_up=glu_interleave(w_up), w_down=w_down, w_q_p=w_q_p, w_kv=w_kv_up)


def _q_grad(g_q_p):
    return g_q_p.reshape(Q_RANK, N_HEADS, HEAD_BLOCK)[:, :, :NOPE + ROPE].reshape(Q_RANK, -1)


def _out_grad(g_out_p):
    return jnp.concatenate([g_out_p[:QP].reshape(N_HEADS, HEAD_BLOCK, -1)[:, HEAD_BLOCK - V_DIM:].reshape(N_HEADS * V_DIM, -1),
                            g_out_p[QP:]], axis=0)


EARLY = ("w_out", "w_up", "w_down", "w_q_up", "w_kv_up")


def local_step(x, ctx, target, mod_x, mod_c, W, late_shards, V):
    nb, S, D = x.shape
    C = ctx.shape[1]
    T = C + S
    tr = _tile(math.gcd(C, S), 256, 8)
    tq = _tile(S, 256, 8)
    tc = 256
    cblk = C // tr
    m = [mod_x[:, i * D:(i + 1) * D][:, None, :] for i in range(N_MOD)]
    mc = [mod_c[:, i * D:(i + 1) * D] for i in range(2)]
    ssd_w8, ffn_w8 = _taps8(V["ssd_conv_w"]), _taps8(V["ffn_conv_w"])
    dexp = jnp.repeat(V["ssd_d"].reshape(-1), SSD_P).reshape(1, D_INNER)
    cosT, sinT = rope_tables(C, S)
    cosS, sinS = cosT[C:], sinT[C:]

    (h1x,) = rows_fwd("prenorm_x", fn_prenorm, nb, S // tr, tr, [(x, D, 0, 0)], [m[0], m[1]], [V["mix_pre_norm"]], [(D, BF16)])
    (h1c,) = rows_fwd("prenorm_c", fn_prenorm, nb, C // tr, tr, [(ctx, D, 0, 0)], [], [mc[0], mc[1], V["mix_pre_norm"]], [(D, BF16)])
    h1 = jnp.concatenate([h1c, h1x], axis=1).reshape(nb * T, D)
    u = matmul("in_proj", [(h1, W["w_in_p"])], "nn", F32).reshape(nb, T, WIN_P)
    xbc = ssd_conv_fwd(u, ssd_w8, V["ssd_conv_b"], C, tc)
    dt2, alog2, dtb2 = ssd_dt_inputs(u, V["ssd_a_log"], V["ssd_dt_bias"])
    y2, hin, *late = ssd_fwd(xbc, dt2, alog2, dtb2, C, hosted_all_gather(late_shards))
    W = dict(W, **late_weights_to_internal(*[_whole(s, n) for s, n in zip(late, LATE_WEIGHTS)]))
    y2 = y2.reshape(2 * nb, S, D_INNER)
    (qn,) = rows_fwd("q_norm", fn_rms, nb, S // tr, tr, [(u, Q_RANK, OFF_CQ // Q_RANK, cblk)], [], [V["q_norm"]], [(Q_RANK, BF16)])
    (kvn,) = rows_fwd("kv_norm", fn_rms, nb, T // tr, tr, [(u, KV_RANK, OFF_CKV // KV_RANK, 0)], [], [V["kv_norm"]], [(KV_RANK, BF16)])
    qn2, kvn2 = qn.reshape(nb * S, Q_RANK), kvn.reshape(nb * T, KV_RANK)
    q_raw = matmul("q_up", [(qn2, W["w_q_p"])], "nn", F32).reshape(nb, S, QP)
    kv = matmul("kv_up", [(kvn2, W["w_kv"])], "nn", BF16).reshape(nb, T, QP)
    cos_q, sin_q = cosS * Q_PRESCALE, sinS * Q_PRESCALE
    kr = rope_call("rope_k", u, LANE, OFF_KR // LANE, cosT, sinT, BF16, tr)
    o = attn_fwd(q_raw, kv, kr, cos_q, sin_q, tq)
    fin_rows = [(y2, D_INNER, 0, 0, 0), (y2, D_INNER, 0, 0, nb), (xbc, D_INNER, 0, cblk), (u, D_INNER, OFF_Z // D_INNER, cblk)]
    fin_gl = [dexp, V["ssd_norm"]]
    (ssd,) = rows_fwd("ssd_finish", fn_ssd_finish, nb, S // tr, tr, fin_rows, [], fin_gl, [(D_INNER, BF16)])
    o2, ssd2 = o.reshape(nb * S, QP), ssd.reshape(nb * S, D_INNER)
    mix = matmul("out_proj", [(o2, W["w_out_p"][:QP]), (ssd2, W["w_out_p"][QP:])], "nn", F32).reshape(nb, S, D)
    pm_rows = [(x, D, 0, 0), (mix, D, 0, 0)]
    pm_pb = [m[2], m[4], m[3]]
    pm_gl = [V["mix_post_norm"], V["ffn_pre_norm"]]
    x1, h2 = rows_fwd("postmix", fn_postmix, nb, S // tr, tr, pm_rows, pm_pb, pm_gl, [(D, F32), (D, BF16)])
    h22 = h2.reshape(nb * S, D)
    up = matmul("up_proj", [(h22, W["w_up"])], "nn", F32).reshape(nb, S, 2 * D_FF)
    act = glu_fwd(up, ffn_w8, V["ffn_conv_b"])
    act2 = act.reshape(nb * S, D_FF)
    ffn = matmul("down_proj", [(act2, W["w_down"])], "nn", F32).reshape(nb, S, D)
    dx1, dffn, dgate2, d_ffn_post, loss = final_call(x1, ffn, target, m[5], V["ffn_post_norm"], tr)

    dffn2 = dffn.reshape(nb * S, D)
    dact = matmul("down_dgrad", [(dffn2, W["w_down"])], "nt", BF16).reshape(nb, S, D_FF)
    g_down = matmul_tn("down_wgrad", act2, dffn2)
    dup, ffn_rows = glu_bwd(up, ffn_w8, V["ffn_conv_b"], dact)
    dup2 = dup.reshape(nb * S, 2 * D_FF)
    dh2 = matmul("up_dgrad", [(dup2, W["w_up"])], "nt", BF16).reshape(nb, S, D)
    g_up = matmul_tn("up_wgrad", h22, dup2)
    dx_a, dmix, dgate1, dscale2, dshift2, d_mix_post, d_ffn_pre = rows_bwd(
        "postmix_bwd", fn_postmix, nb, S // tr, tr, pm_rows, pm_pb, pm_gl,
        [(dx1, D, 0, 0), (dh2, D, 0, 0)], [(0, F32), (1, BF16)])
    dmix2 = dmix.reshape(nb * S, D)
    dcat = matmul("out_dgrad", [(dmix2, W["w_out_p"])], "nt", BF16).reshape(nb, S, QP + D_INNER)
    g_out_p = jnp.concatenate([matmul_tn("out_wgrad_attn", o2, dmix2), matmul_tn("out_wgrad_ssd", ssd2, dmix2)], axis=0)
    dy, dxs_direct, dz, d_dexp, d_ssd_norm = rows_bwd(
        "ssd_finish_bwd", fn_ssd_finish, nb, S // tr, tr, fin_rows, [], fin_gl,
        [(dcat, D_INNER, QP // D_INNER, 0)], [(0, F32), (2, F32), (3, BF16)])
    dq_pre, dkv, dkr = attn_bwd(q_raw, kv, kr, dcat, cos_q, sin_q, cosS, sinS, tq)
    dq_pre = dq_pre.reshape(nb * S, QP)
    dkr_pre = rope_call("rope_dk", dkr, LANE, 0, cosT, -sinT, BF16, tr)
    dkv2 = dkv.reshape(nb * T, QP)
    dqn = matmul("q_dgrad", [(dq_pre, W["w_q_p"])], "nt", F32).reshape(nb, S, Q_RANK)
    g_q_p = matmul_tn("q_wgrad", qn2, dq_pre)
    dkvn = matmul("kv_dgrad", [(dkv2, W["w_kv"])], "nt", F32).reshape(nb, T, KV_RANK)
    g_kv = matmul_tn("kv_wgrad", kvn2, dkv2)
    early_grads = (_out_grad(g_out_p), glu_deinterleave(g_up), g_down, _q_grad(g_q_p), g_kv)
    early = hosted_all_to_all([_per_device(g, n) for g, n in zip(early_grads, EARLY)])
    dxbc2, ddt2, ssd_stats, *received = ssd_bwd(xbc, dt2, alog2, dtb2, hin, dy, C, early)
    ddt_block = jnp.concatenate([ddt2[0][..., :SSD_HEADS], ddt2[1][..., :SSD_HEADS],
                                 jnp.zeros((nb, T, LANE - 2 * SSD_HEADS), F32)], axis=-1).astype(BF16)
    dxbc_raw, ssd_rows = ssd_conv_bwd(u, ssd_w8, V["ssd_conv_b"], dxbc2, dxs_direct, C, tc)
    dcq, d_q_norm = rows_bwd("q_norm_bwd", fn_rms, nb, S // tr, tr, [(u, Q_RANK, OFF_CQ // Q_RANK, cblk)], [], [V["q_norm"]],
                             [(dqn, Q_RANK, 0, 0)], [(0, BF16)])
    dckv, d_kv_norm = rows_bwd("kv_norm_bwd", fn_rms, nb, T // tr, tr, [(u, KV_RANK, OFF_CKV // KV_RANK, 0)], [], [V["kv_norm"]],
                               [(dkvn, KV_RANK, 0, 0)], [(0, BF16)])

    def ctx_rows(t):
        return jnp.pad(t, ((0, 0), (C, 0), (0, 0)))

    du = [("cq", ctx_rows(dcq), OFF_CQ, Q_RANK), ("kr", dkr_pre, OFF_KR, LANE), ("ckv", dckv, OFF_CKV, KV_RANK),
          ("z", ctx_rows(dz), OFF_Z, D_INNER), ("xbc", dxbc_raw, OFF_XBC, XBC), ("dt", ddt_block, OFF_DT, LANE)]
    du = [(name, t.reshape(nb * T, w), off, w) for (name, t, off, w) in du]
    g = {name: matmul_tn("in_wgrad_" + name, h1, t) for (name, t, _, _) in du}
    g_in = jnp.concatenate([g["cq"], g["ckv"], g["kr"][:, KR_LANE:KR_LANE + ROPE], g["z"], g["xbc"],
                            g["dt"][:, :2 * SSD_HEADS]], axis=1)
    dh1, received_in = matmul("in_dgrad", [(t, W["w_in_p"][:, off:off + w]) for (_, t, off, w) in du], "nt", BF16,
                              hosted=hosted_all_to_all([_per_device(g_in, "w_in").astype(BF16)]))
    dh1 = dh1.reshape(nb, T, D)

    def fn_prenorm_res(xv, shift, scale, g):
        return fn_prenorm(xv, shift, scale, g) + (xv,)

    grad_x, dshift1, dscale1, d_mix_pre_x = rows_bwd(
        "prenorm_x_bwd", fn_prenorm_res, nb, S // tr, tr, [(x, D, 0, 0)], [m[0], m[1]], [V["mix_pre_norm"]],
        [(dh1, D, 0, cblk), (dx_a, D, 0, 0)], [(0, F32)])
    dshift_c, dscale_c, d_mix_pre_c = rows_bwd(
        "prenorm_c_bwd", fn_prenorm, nb, C // tr, tr, [(ctx, D, 0, 0)], [], [mc[0], mc[1], V["mix_pre_norm"]],
        [(dh1, D, 0, 0)], [])

    dmod_x = jnp.concatenate([dshift1, dscale1, dgate1, dshift2, dscale2, dgate2], axis=-1).reshape(nb, N_MOD * D)
    dmod_c = jnp.concatenate([dshift_c, dscale_c, jnp.zeros((1, (N_MOD - 2) * D), F32)], axis=-1)
    gv = dict(
        mix_pre_norm=d_mix_pre_x + d_mix_pre_c, mix_post_norm=d_mix_post, q_norm=d_q_norm, kv_norm=d_kv_norm,
        ssd_conv_w=ssd_rows[:SSD_K], ssd_conv_b=ssd_rows[SSD_K:SSD_K + 1],
        ssd_a_log=jnp.concatenate([ssd_stats[0:1, :SSD_HEADS], ssd_stats[2:3, :SSD_HEADS]], axis=1),
        ssd_dt_bias=jnp.concatenate([ssd_stats[1:2, :SSD_HEADS], ssd_stats[3:4, :SSD_HEADS]], axis=1),
        ssd_d=jnp.sum(d_dexp.reshape(SSD_HEADS, SSD_P), axis=1).reshape(1, SSD_HEADS), ssd_norm=d_ssd_norm,
        ffn_pre_norm=d_ffn_pre, ffn_post_norm=d_ffn_post,
        ffn_conv_w=ffn_rows[:FFN_K], ffn_conv_b=ffn_rows[FFN_K:FFN_K + 1])
    return loss, grad_x, dmod_x, dmod_c, gv, dict(zip(EARLY, received), w_in=received_in)


WEIGHT_ORDER = ("c_ctx", "w_mod", "b_mod", "mix_pre_norm", "mix_post_norm", "w_in", "q_norm", "w_q_up", "kv_norm",
                "w_kv_up", "ssd_conv_w", "ssd_conv_b", "ssd_a_log", "ssd_dt_bias", "ssd_d", "ssd_norm", "w_out",
                "ffn_pre_norm", "ffn_post_norm", "w_up", "ffn_conv_w", "ffn_conv_b", "w_down")
MATRICES = ("w_in", "w_q_up", "w_kv_up", "w_out", "w_up", "w_down")
ROW_SHARDED = ("w_out", "w_down")
SMALL_SUMMED = ("c_ctx", "mix_pre_norm", "mix_post_norm", "q_norm", "kv_norm", "ssd_conv_w", "ssd_conv_b", "ssd_a_log",
                "ssd_dt_bias", "ssd_d", "ssd_norm", "ffn_pre_norm", "ffn_post_norm", "ffn_conv_w", "ffn_conv_b")
MOD_ROWS = 8


def _whole(shards, name):
    if name in ROW_SHARDED:
        return shards.reshape(-1, shards.shape[-1])
    return jnp.concatenate([shards[j] for j in range(N_DEV)], axis=1)


def _per_device(g, name):
    if name in ROW_SHARDED:
        return g.reshape(N_DEV, -1, g.shape[-1])
    return jnp.stack(jnp.split(g, N_DEV, axis=1))


def kernel(x, c, ctx, c_ctx, w_mod, b_mod, mix_pre_norm, mix_post_norm, w_in, q_norm, w_q_up, kv_norm, w_kv_up, ssd_conv_w, ssd_conv_b, ssd_a_log, ssd_dt_bias, ssd_d, ssd_norm, w_out, ffn_pre_norm, ffn_post_norm, w_up, ffn_conv_w, ffn_conv_b, w_down, loss_target, m_c_ctx, m_w_mod, m_b_mod, m_mix_pre_norm, m_mix_post_norm, m_w_in, m_q_norm, m_w_q_up, m_kv_norm, m_w_kv_up, m_ssd_conv_w, m_ssd_conv_b, m_ssd_a_log, m_ssd_dt_bias, m_ssd_d, m_ssd_norm, m_w_out, m_ffn_pre_norm, m_ffn_post_norm, m_w_up, m_ffn_conv_w, m_ffn_conv_b, m_w_down, v_c_ctx, v_w_mod, v_b_mod, v_mix_pre_norm, v_mix_post_norm, v_w_in, v_q_norm, v_w_q_up, v_kv_norm, v_w_kv_up, v_ssd_conv_w, v_ssd_conv_b, v_ssd_a_log, v_ssd_dt_bias, v_ssd_d, v_ssd_norm, v_w_out, v_ffn_pre_norm, v_ffn_post_norm, v_w_up, v_ffn_conv_w, v_ffn_conv_b, v_w_down):
    weights = dict(c_ctx=c_ctx, w_mod=w_mod, b_mod=b_mod, mix_pre_norm=mix_pre_norm, mix_post_norm=mix_post_norm, w_in=w_in, q_norm=q_norm, w_q_up=w_q_up, kv_norm=kv_norm, w_kv_up=w_kv_up, ssd_conv_w=ssd_conv_w, ssd_conv_b=ssd_conv_b, ssd_a_log=ssd_a_log, ssd_dt_bias=ssd_dt_bias, ssd_d=ssd_d, ssd_norm=ssd_norm, w_out=w_out, ffn_pre_norm=ffn_pre_norm, ffn_post_norm=ffn_post_norm, w_up=w_up, ffn_conv_w=ffn_conv_w, ffn_conv_b=ffn_conv_b, w_down=w_down)
    mom1 = dict(c_ctx=m_c_ctx, w_mod=m_w_mod, b_mod=m_b_mod, mix_pre_norm=m_mix_pre_norm, mix_post_norm=m_mix_post_norm, w_in=m_w_in, q_norm=m_q_norm, w_q_up=m_w_q_up, kv_norm=m_kv_norm, w_kv_up=m_w_kv_up, ssd_conv_w=m_ssd_conv_w, ssd_conv_b=m_ssd_conv_b, ssd_a_log=m_ssd_a_log, ssd_dt_bias=m_ssd_dt_bias, ssd_d=m_ssd_d, ssd_norm=m_ssd_norm, w_out=m_w_out, ffn_pre_norm=m_ffn_pre_norm, ffn_post_norm=m_ffn_post_norm, w_up=m_w_up, ffn_conv_w=m_ffn_conv_w, ffn_conv_b=m_ffn_conv_b, w_down=m_w_down)
    mom2 = dict(c_ctx=v_c_ctx, w_mod=v_w_mod, b_mod=v_b_mod, mix_pre_norm=v_mix_pre_norm, mix_post_norm=v_mix_post_norm, w_in=v_w_in, q_norm=v_q_norm, w_q_up=v_w_q_up, kv_norm=v_kv_norm, w_kv_up=v_w_kv_up, ssd_conv_w=v_ssd_conv_w, ssd_conv_b=v_ssd_conv_b, ssd_a_log=v_ssd_a_log, ssd_dt_bias=v_ssd_dt_bias, ssd_d=v_ssd_d, ssd_norm=v_ssd_norm, w_out=v_w_out, ffn_pre_norm=v_ffn_pre_norm, ffn_post_norm=v_ffn_post_norm, w_up=v_w_up, ffn_conv_w=v_ffn_conv_w, ffn_conv_b=v_ffn_conv_b, w_down=v_w_down)
    nb, S, D = x.shape
    me = 4 * lax.axis_index("x") + 2 * lax.axis_index("y") + lax.axis_index("c")

    *first, c_all, ssd_w_sh, ffn_w_sh = all_gather(
        "gather_first", [weights[n][0].astype(BF16) for n in FIRST] + [c, ssd_conv_w[0], ffn_conv_w[0]])
    W = first_weights_to_internal(*[_whole(s, n) for n, s in zip(FIRST, first)])
    late_shards = [weights[n][0].astype(BF16) for n in LATE_WEIGHTS]
    V = {n: weights[n].reshape(1, -1) for n in SMALL_SUMMED if n != "c_ctx"}
    V["ssd_conv_w"] = _whole(ssd_w_sh, "ssd_conv_w")
    V["ffn_conv_w"] = _whole(ffn_w_sh, "ffn_conv_w")

    n_all = N_DEV * nb
    mod_rows = -(-(n_all + 1) // 8) * 8
    c_pad = jnp.concatenate([c_all.reshape(n_all, D), c_ctx.reshape(1, D), jnp.zeros((mod_rows - n_all - 1, D), F32)], axis=0)
    mod_cols = w_mod.shape[2]
    b_mine = lax.dynamic_slice(b_mod, (0, me * mod_cols), (1, mod_cols))
    mod_part = matmul("mod_proj", [(c_pad, w_mod[0])], "nn", F32, bias=b_mine, silu_a=True)
    mod_all = _whole(all_gather("gather_mod", [mod_part])[0], "w_mod")
    mod_x = lax.dynamic_slice(mod_all, (me * nb, 0), (nb, mod_all.shape[1]))
    mod_c = mod_all[n_all:n_all + 1]

    loss, grad_x, dmod_x, dmod_c, gv, slots = local_step(x, ctx, loss_target, mod_x, mod_c, W, late_shards, V)

    dmod_mine = jnp.concatenate([dmod_x, dmod_c, jnp.zeros((MOD_ROWS - nb - 1, dmod_x.shape[1]), F32)], axis=0)
    dmod_all = all_gather("gather_dmod", [dmod_mine])[0]
    dmod_ctx = sum_slots("sum_dmod_ctx", dmod_all[:, nb:nb + 1].reshape(N_DEV, -1, LANE)).reshape(1, -1)
    dmod_full = jnp.concatenate([dmod_all[:, :nb].reshape(n_all, -1), dmod_ctx,
                                 jnp.zeros((mod_rows - n_all - 1, dmod_ctx.shape[1]), F32)], axis=0)
    (g_b_mod,) = ew_call("mod_bias_grad", lambda t: (jnp.sum(t, axis=0, keepdims=True),), [dmod_full], [((1, dmod_full.shape[1]), F32)])
    dmod_cols = lax.dynamic_slice(dmod_full, (0, me * mod_cols), (mod_rows, mod_cols))
    g_w_mod = matmul_tn("mod_wgrad", c_pad, dmod_cols, silu_a=True)
    dsilu_ctx = matmul("mod_dgrad_ctx", [(dmod_cols[n_all:n_all + 8], w_mod[0])], "nt", F32)[0:1]

    def silu_vjp(cc, ct):
        return (jax.vjp(_silu, cc)[1](ct)[0],)

    (g_c_ctx_part,) = ew_call("c_ctx_grad", silu_vjp, [c_ctx.reshape(1, D), dsilu_ctx], [((1, D), F32)])

    gv = dict(gv, c_ctx=g_c_ctx_part)
    small_parts = [loss] + [gv[n] for n in SMALL_SUMMED]
    small_sum = sum_slots("sum_small", all_gather("gather_small_grads", [_pack_rows(small_parts)])[0])
    summed = _unpack_rows(small_sum, [p.shape for p in small_parts])
    loss_out = summed[0][0, 0]
    grads = {n: g.reshape(weights[n].shape) if n not in ("ssd_conv_w", "ffn_conv_w") else g for n, g in zip(SMALL_SUMMED, summed[1:])}
    for n in ("ssd_conv_w", "ffn_conv_w"):
        cols = weights[n].shape[2]
        grads[n] = lax.dynamic_slice(grads[n], (0, me * cols), (grads[n].shape[0], cols)).reshape(weights[n].shape)
    grads["b_mod"] = g_b_mod.reshape(b_mod.shape)

    slots = dict(slots, w_mod=g_w_mod[None])
    delta, new_m, new_v = {}, {}, {}
    for n in MATRICES + ("w_mod",):
        g, d, mn, vn = adamw_matrix("adamw_" + n, weights[n][0], slots[n], mom1[n][0], mom2[n][0])
        grads[n], delta[n], new_m[n], new_v[n] = [t.reshape(weights[n].shape) for t in (g, d, mn, vn)]
    small = [n for n in WEIGHT_ORDER if n not in slots]

    def two_d(t):
        return t.reshape(-1, t.shape[-1])

    ds, ms, vs = adamw_small(*[[two_d(t[n]) for n in small] for t in (weights, grads, mom1, mom2)])
    for n, d, mn, vn in zip(small, ds, ms, vs):
        delta[n], new_m[n], new_v[n] = [t.reshape(weights[n].shape) for t in (d, mn, vn)]
    return (loss_out, grad_x, *[t[n] for t in (grads, delta, new_m, new_v) for n in WEIGHT_ORDER])
```

```python
import math

import jax
import jax.numpy as jnp
import numpy as np
from jax import lax
from jax.experimental import pallas as pl
from jax.experimental.pallas import tpu as pltpu

F32 = jnp.float32
BF16 = jnp.bfloat16
MESH = pl.DeviceIdType.MESH

D_MODEL = 1024
GRID_W = 64
N_HEADS = 16
NOPE = 64
ROPE = 32
V_DIM = 64
Q_RANK = 384
KV_RANK = 256
ROPE_THETA = 10000.0
ATTN_SCALE = (NOPE + ROPE) ** -0.5
SSD_HEADS = 16
SSD_P = 64
SSD_GROUPS = 2
SSD_N = 128
SSD_K = 5
CHUNK = 128
D_INNER = SSD_HEADS * SSD_P
GN = SSD_GROUPS * SSD_N
XBC = D_INNER + 2 * GN
D_FF = 2816
FFN_K = 3
N_MOD = 6
EPS = 1e-6
IN_SPLITS = (Q_RANK, KV_RANK, ROPE, D_INNER, XBC, 2 * SSD_HEADS)
IN_WIDTH = sum(IN_SPLITS)
N_DEV = 8

ADAM_LR = 0.001
ADAM_B1 = 0.9
ADAM_B2 = 0.999
ADAM_EPS = 1e-08
ADAM_WD = 0.01
ADAM_STEP = 10

LANE = 128
HEAD_BLOCK = 128
OFF_CQ = 0
OFF_KR = 384
OFF_CKV = 512
OFF_Z = 1024
OFF_XBC = 2048
OFF_DT = 3584
WIN_P = 3840
KR_LANE = 64
QP = N_HEADS * HEAD_BLOCK

VMEM_LIMIT_V7X = 56 * 1024 * 1024
NEG_BIG = -1e30


def _cparams(*sem):
    return pltpu.CompilerParams(dimension_semantics=sem, vmem_limit_bytes=VMEM_LIMIT_V7X)


def _tile(n, target, mult=128):
    if n <= target:
        return n
    t = (target // mult) * mult
    while t >= mult:
        if n % t == 0:
            return t
        t -= mult
    return n


def _silu(x):
    return x * jax.nn.sigmoid(x)


def _rms(x, g):
    return x * lax.rsqrt(jnp.mean(x * x, axis=-1, keepdims=True) + EPS) * g


WHOLE_K_WIDE = 2048


def matmul(name, pairs, mode, out_dtype, *, bias=None, silu_a=False, hosted=None):
    n_pairs = len(pairs)
    M = pairs[0][0].shape[0]
    N = pairs[0][1].shape[1] if mode == "nn" else pairs[0][1].shape[0]
    k_total = sum(a.shape[1] for a, _ in pairs)
    tm = _tile(M, 1024 if k_total <= WHOLE_K_WIDE else 512, 8)
    tn = _tile(N, 1408 if k_total <= WHOLE_K_WIDE else 512)
    dims = (((1,), (0,)), ((), ())) if mode == "nn" else (((1,), (1,)), ((), ()))
    n_own = 2 * n_pairs + (bias is not None)
    n_ex = hosted.n if hosted else 0

    def body(*refs):
        o_ref = refs[n_own + n_ex]
        if hosted:
            j, i = pl.program_id(0), pl.program_id(1)
            begin_exchange, end_exchange = hosted.steps(
                refs[n_own:n_own + n_ex], refs[n_own + n_ex + 1:n_own + 2 * n_ex + 1], refs[n_own + 2 * n_ex + 1:],
                jnp.logical_and(j == 0, i == 0), jnp.logical_and(j == N // tn - 1, i == M // tm - 1))
            begin_exchange()
        acc = None
        for p in range(n_pairs):
            a = refs[2 * p][...]
            if silu_a:
                a = _silu(a.astype(F32))
            d = lax.dot_general(a.astype(BF16), refs[2 * p + 1][...].astype(BF16), dims, preferred_element_type=F32)
            acc = d if acc is None else acc + d
        if bias is not None:
            acc = acc + refs[2 * n_pairs][...]
        o_ref[...] = acc.astype(o_ref.dtype)
        if hosted:
            end_exchange()

    in_specs, args = [], []
    for a, b in pairs:
        K = a.shape[1]
        in_specs.append(pl.BlockSpec((tm, K), lambda j, i: (i, 0)))
        in_specs.append(pl.BlockSpec((K, tn), lambda j, i: (0, j)) if mode == "nn" else pl.BlockSpec((tn, K), lambda j, i: (j, 0)))
        args += [a, b]
    if bias is not None:
        in_specs.append(pl.BlockSpec((1, tn), lambda j, i: (0, j)))
        args.append(bias)
    out_spec = pl.BlockSpec((tm, tn), lambda j, i: (i, j))
    out_shape = jax.ShapeDtypeStruct((M, N), out_dtype)
    if not hosted:
        return pl.pallas_call(
            body, name=name, grid=(N // tn, M // tm), in_specs=in_specs, out_specs=out_spec, out_shape=out_shape,
            compiler_params=_cparams("arbitrary", "arbitrary"),
        )(*args)
    return pl.pallas_call(
        body, name=name, grid=(N // tn, M // tm), in_specs=in_specs + hosted.specs,
        out_specs=[out_spec] + hosted.specs, out_shape=[out_shape] + hosted.out_shape, scratch_shapes=hosted.scratch,
        compiler_params=_cparams("arbitrary", "arbitrary"),
    )(*args, *hosted.arrays)


def matmul_tn(name, a, b, out_dtype=F32, *, silu_a=False, tm=1408, tn=512, tk=2048):
    R, M = a.shape
    N = b.shape[1]
    tm = _tile(M, tm)
    tn = _tile(N, tn)
    tk = _tile(R, tk, 8)
    nk = R // tk

    def body(a_ref, b_ref, o_ref, acc):
        k = pl.program_id(2)

        @pl.when(k == 0)
        def _():
            acc[...] = jnp.zeros_like(acc)

        x = a_ref[...]
        if silu_a:
            x = _silu(x.astype(F32))
        acc[...] += lax.dot_general(x.astype(BF16), b_ref[...].astype(BF16), (((0,), (0,)), ((), ())),
                                    preferred_element_type=F32)

        @pl.when(k == nk - 1)
        def _():
            o_ref[...] = acc[...].astype(o_ref.dtype)

    return pl.pallas_call(
        body, name=name, grid=(M // tm, N // tn, nk),
        in_specs=[pl.BlockSpec((tk, tm), lambda i, j, k: (k, i)), pl.BlockSpec((tk, tn), lambda i, j, k: (k, j))],
        out_specs=pl.BlockSpec((tm, tn), lambda i, j, k: (i, j)),
        out_shape=jax.ShapeDtypeStruct((M, N), out_dtype),
        scratch_shapes=[pltpu.VMEM((tm, tn), F32)],
        compiler_params=_cparams("arbitrary", "arbitrary", "arbitrary"),
    )(a, b)


def _row_specs(rin, pbin, glin, tr):
    specs = [pl.BlockSpec((1, tr, w), lambda b, i, cb=cb, ro=ro, bo=(e[4] if len(e) > 4 else 0): (b + bo, i + ro, cb))
             for e in rin for (_, w, cb, ro) in [e[:4]]]
    specs += [pl.BlockSpec((1, 1, a.shape[-1]), lambda b, i: (b, 0, 0)) for a in pbin]
    specs += [pl.BlockSpec((1, a.shape[-1]), lambda b, i: (0, 0)) for a in glin]
    return specs


def rows_fwd(name, fn, nb, nblk, tr, rin, pbin, glin, outs):
    nr, npb, ngl = len(rin), len(pbin), len(glin)
    n_in = nr + npb + ngl

    def body(*refs):
        args = [r[0].astype(F32) for r in refs[:nr + npb]] + [r[...] for r in refs[nr + npb:n_in]]
        res = fn(*args)
        for o, v in zip(refs[n_in:], res):
            o[0] = v.astype(o.dtype)

    return pl.pallas_call(
        body, name=name, grid=(nb, nblk), in_specs=_row_specs(rin, pbin, glin, tr),
        out_specs=[pl.BlockSpec((1, tr, w), lambda b, i: (b, i, 0)) for (w, _) in outs],
        out_shape=[jax.ShapeDtypeStruct((nb, nblk * tr, w), dt) for (w, dt) in outs],
        compiler_params=_cparams("arbitrary", "arbitrary"),
    )(*[e[0] for e in rin], *pbin, *glin)


def rows_bwd(name, fn, nb, nblk, tr, rin, pbin, glin, cts, want):
    nr, npb, ngl, nct = len(rin), len(pbin), len(glin), len(cts)
    n_in = nr + npb + ngl

    def body(*refs):
        b, i = pl.program_id(0), pl.program_id(1)
        args = [r[0].astype(F32) for r in refs[:nr + npb]] + [r[...] for r in refs[nr + npb:n_in]]
        ct = tuple(r[0].astype(F32) for r in refs[n_in:n_in + nct])
        _, vjp = jax.vjp(fn, *args)
        g = vjp(ct)
        orefs = refs[n_in + nct:]
        for o, (idx, _) in zip(orefs, want):
            o[0] = g[idx].astype(o.dtype)
        pb_refs = orefs[len(want):len(want) + npb]
        gl_refs = orefs[len(want) + npb:]

        @pl.when(i == 0)
        def _():
            for o, v in zip(pb_refs, g[nr:nr + npb]):
                o[0] = v

        @pl.when(i > 0)
        def _():
            for o, v in zip(pb_refs, g[nr:nr + npb]):
                o[0] += v

        first = jnp.logical_and(b == 0, i == 0)

        @pl.when(first)
        def _():
            for o, v in zip(gl_refs, g[nr + npb:]):
                o[...] = v

        @pl.when(jnp.logical_not(first))
        def _():
            for o, v in zip(gl_refs, g[nr + npb:]):
                o[...] += v

    out_specs = [pl.BlockSpec((1, tr, rin[idx][1]), lambda b, i: (b, i, 0)) for (idx, _) in want]
    out_shape = [jax.ShapeDtypeStruct((nb, nblk * tr, rin[idx][1]), dt) for (idx, dt) in want]
    out_specs += [pl.BlockSpec((1, 1, a.shape[-1]), lambda b, i: (b, 0, 0)) for a in pbin]
    out_shape += [jax.ShapeDtypeStruct((nb, 1, a.shape[-1]), F32) for a in pbin]
    out_specs += [pl.BlockSpec((1, a.shape[-1]), lambda b, i: (0, 0)) for a in glin]
    out_shape += [jax.ShapeDtypeStruct((1, a.shape[-1]), F32) for a in glin]
    return pl.pallas_call(
        body, name=name, grid=(nb, nblk),
        in_specs=_row_specs(rin, pbin, glin, tr) + _row_specs(cts, [], [], tr),
        out_specs=out_specs, out_shape=out_shape,
        compiler_params=_cparams("arbitrary", "arbitrary"),
    )(*[e[0] for e in rin], *pbin, *glin, *[e[0] for e in cts])


def ew_call(name, fn, ins, outs):
    def body(*refs):
        res = fn(*[r[...] for r in refs[:len(ins)]])
        for o, v in zip(refs[len(ins):], res):
            o[...] = v.astype(o.dtype)

    return pl.pallas_call(body, name=name, out_shape=[jax.ShapeDtypeStruct(s, dt) for (s, dt) in outs])(*ins)


def fn_prenorm(x, shift, scale, g):
    return (_rms(x, g) * (1.0 + scale) + shift,)


def fn_rms(x, g):
    return (_rms(x, g),)


def fn_ssd_finish(yf, yr, xs, z, dexp, nw):
    y = yf + yr + dexp * xs
    return (_rms(y * _silu(z), nw),)


def fn_postmix(x, mix, gate1, scale2, shift2, post_g, pre_g):
    x1 = x + gate1 * _rms(mix, post_g)
    h2 = _rms(x1, pre_g) * (1.0 + scale2) + shift2
    return x1, h2


def final_call(x1, ffn, target, gate2, post_g, tr):
    nb, S, D = x1.shape
    nblk = S // tr

    def body(x1_ref, f_ref, t_ref, g2_ref, pg_ref, dx1_ref, df_ref, dg2_ref, dpg_ref, loss_ref):
        b, i = pl.program_id(0), pl.program_id(1)
        tgt = t_ref[0]

        def lossfn(x1v, fv, g2, pg):
            e = x1v + g2 * _rms(fv, pg) - tgt
            return 0.5 * jnp.sum(jnp.mean(e * e, axis=-1, keepdims=True))

        val, (dx1, df, dg2, dpg) = jax.value_and_grad(lossfn, argnums=(0, 1, 2, 3))(
            x1_ref[0], f_ref[0].astype(F32), g2_ref[0], pg_ref[...])
        dx1_ref[0] = dx1
        df_ref[0] = df.astype(df_ref.dtype)
        lv = jnp.full((1, LANE), val, F32)

        @pl.when(i == 0)
        def _():
            dg2_ref[0] = dg2

        @pl.when(i > 0)
        def _():
            dg2_ref[0] += dg2

        first = jnp.logical_and(b == 0, i == 0)

        @pl.when(first)
        def _():
            dpg_ref[...] = dpg
            loss_ref[...] = lv

        @pl.when(jnp.logical_not(first))
        def _():
            dpg_ref[...] += dpg
            loss_ref[...] += lv

    row = pl.BlockSpec((1, tr, D), lambda b, i: (b, i, 0))
    pb = pl.BlockSpec((1, 1, D), lambda b, i: (b, 0, 0))
    gl = pl.BlockSpec((1, D), lambda b, i: (0, 0))
    return pl.pallas_call(
        body, name="loss_head", grid=(nb, nblk), in_specs=[row, row, row, pb, gl],
        out_specs=[row, row, pb, gl, pl.BlockSpec((1, LANE), lambda b, i: (0, 0))],
        out_shape=[jax.ShapeDtypeStruct((nb, S, D), F32), jax.ShapeDtypeStruct((nb, S, D), BF16),
                   jax.ShapeDtypeStruct((nb, 1, D), F32), jax.ShapeDtypeStruct((1, D), F32),
                   jax.ShapeDtypeStruct((1, LANE), F32)],
        compiler_params=_cparams("arbitrary", "arbitrary"),
    )(x1, ffn, target, gate2, post_g)


def _rotate_half(t):
    lane = lax.broadcasted_iota(jnp.int32, t.shape, 1)
    return jnp.where((lane & 15) < 8, -pltpu.roll(t, LANE - 8, 1), pltpu.roll(t, 8, 1))


def rope_call(name, x, width, colblk, cos, sin, out_dtype, tr):
    nb = x.shape[0]
    R = cos.shape[0]
    nblk = R // tr

    def body(x_ref, c_ref, s_ref, o_ref):
        c, s = c_ref[...], s_ref[...]
        for h in range(width // LANE):
            t = x_ref[0, :, h * LANE:(h + 1) * LANE].astype(F32)
            o_ref[0, :, h * LANE:(h + 1) * LANE] = (t * c + _rotate_half(t) * s).astype(o_ref.dtype)

    tab = pl.BlockSpec((tr, LANE), lambda b, i: (i, 0))
    return pl.pallas_call(
        body, name=name, grid=(nb, nblk),
        in_specs=[pl.BlockSpec((1, tr, width), lambda b, i: (b, i, colblk)), tab, tab],
        out_specs=pl.BlockSpec((1, tr, width), lambda b, i: (b, i, 0)),
        out_shape=jax.ShapeDtypeStruct((nb, R, width), out_dtype),
        compiler_params=_cparams("arbitrary", "arbitrary"),
    )(x, cos, sin)


def rope_tables(n_ctx, seq):
    n_rows = seq // GRID_W
    row = np.repeat(np.arange(n_rows), GRID_W).astype(np.float32)
    col = np.tile(np.arange(GRID_W), n_rows).astype(np.float32)
    axis_dim = ROPE // 2
    inv_freq = jnp.asarray(ROPE_THETA, F32) ** (-jnp.arange(0, axis_dim, 2, dtype=F32) / axis_dim)
    ang_r = jnp.asarray(row)[:, None] * inv_freq
    ang_c = jnp.asarray(col)[:, None] * inv_freq
    ang = jnp.concatenate([ang_r, ang_r, ang_c, ang_c], axis=-1)
    cos = jnp.ones((n_ctx + seq, LANE), F32).at[n_ctx:, KR_LANE:KR_LANE + ROPE].set(jnp.cos(ang))
    sin = jnp.zeros((n_ctx + seq, LANE), F32).at[n_ctx:, KR_LANE:KR_LANE + ROPE].set(jnp.sin(ang))
    return cos, sin


Q_PRESCALE = ATTN_SCALE * math.log2(math.e)


def _attn_weights(q, kc):
    s2 = lax.dot_general(q, kc, (((1,), (1,)), ((), ())), preferred_element_type=F32)
    e = jnp.exp2(s2 - jnp.max(s2, axis=1, keepdims=True))
    return e, 1.0 / jnp.sum(e, axis=1, keepdims=True)


def _key_block(kv, kr):
    lane = lax.broadcasted_iota(jnp.int32, kv.shape, 1)
    return jnp.where(lane < NOPE, kv, kr)


def _rotated_query(q_ref, cos_ref, sin_ref):
    t = q_ref[0].astype(F32)
    return (t * cos_ref[...] + _rotate_half(t) * sin_ref[...]).astype(BF16)


def attn_fwd(q_raw, kv, kr, cos_q, sin_q, tq):
    nb, S, _ = q_raw.shape
    T = kv.shape[1]

    def body(q_ref, kv_ref, kr_ref, c_ref, s_ref, o_ref):
        kvv = kv_ref[0]
        e, r = _attn_weights(_rotated_query(q_ref, c_ref, s_ref), _key_block(kvv, kr_ref[0]))
        o = lax.dot_general(e.astype(BF16), kvv, (((1,), (0,)), ((), ())), preferred_element_type=F32) * r
        lane = lax.broadcasted_iota(jnp.int32, o.shape, 1)
        o_ref[0] = jnp.where(lane >= NOPE, o, 0.0).astype(o_ref.dtype)

    return pl.pallas_call(
        body, name="attn_fwd", grid=(nb, N_HEADS, S // tq),
        in_specs=[pl.BlockSpec((1, tq, HEAD_BLOCK), lambda b, h, i: (b, i, h)),
                  pl.BlockSpec((1, T, HEAD_BLOCK), lambda b, h, i: (b, 0, h)),
                  pl.BlockSpec((1, T, HEAD_BLOCK), lambda b, h, i: (b, 0, 0)),
                  pl.BlockSpec((tq, LANE), lambda b, h, i: (i, 0)), pl.BlockSpec((tq, LANE), lambda b, h, i: (i, 0))],
        out_specs=pl.BlockSpec((1, tq, HEAD_BLOCK), lambda b, h, i: (b, i, h)),
        out_shape=jax.ShapeDtypeStruct((nb, S, QP), BF16),
        compiler_params=_cparams("arbitrary", "arbitrary", "arbitrary"),
    )(q_raw, kv, kr, cos_q, sin_q)


def attn_bwd(q_raw, kv, kr, do, cos_q, sin_q, cos, sin, tq):
    nb, S, _ = q_raw.shape
    T = kv.shape[1]

    def body(q_ref, kv_ref, kr_ref, do_ref, cq_ref, sq_ref, c_ref, s_ref, dq_ref, dkv_ref, dkr_ref):
        h, i = pl.program_id(1), pl.program_id(2)

        @pl.when(i == 0)
        def _():
            dkv_ref[...] = jnp.zeros_like(dkv_ref)

        @pl.when(jnp.logical_and(h == 0, i == 0))
        def _():
            dkr_ref[...] = jnp.zeros_like(dkr_ref)

        qv, kvv, dov = _rotated_query(q_ref, cq_ref, sq_ref), kv_ref[0], do_ref[0]
        kc = _key_block(kvv, kr_ref[0])
        e, r = _attn_weights(qv, kc)
        dor = (dov.astype(F32) * r).astype(BF16)
        dpr = lax.dot_general(dor, kvv, (((1,), (1,)), ((), ())), preferred_element_type=F32)
        ds = (e * (dpr - r * jnp.sum(dpr * e, axis=1, keepdims=True))).astype(BF16)
        dq = lax.dot_general(ds, kc, (((1,), (0,)), ((), ())), preferred_element_type=F32) * ATTN_SCALE
        dq_ref[0] = (dq * c_ref[...] - _rotate_half(dq) * s_ref[...]).astype(dq_ref.dtype)
        dkc = lax.dot_general(ds, qv, (((0,), (0,)), ((), ())), preferred_element_type=F32) * math.log(2.0)
        dv = lax.dot_general(e.astype(BF16), dor, (((0,), (0,)), ((), ())), preferred_element_type=F32)
        lane = lax.broadcasted_iota(jnp.int32, dkc.shape, 1)
        dkv_ref[0] += jnp.where(lane < NOPE, dkc, dv)
        dkr_ref[0] += jnp.where(lane >= NOPE, dkc, 0.0)

    qspec = pl.BlockSpec((1, tq, HEAD_BLOCK), lambda b, h, i: (b, i, h))
    kspec = pl.BlockSpec((1, T, HEAD_BLOCK), lambda b, h, i: (b, 0, h))
    rspec = pl.BlockSpec((1, T, HEAD_BLOCK), lambda b, h, i: (b, 0, 0))
    tab = pl.BlockSpec((tq, LANE), lambda b, h, i: (i, 0))
    return pl.pallas_call(
        body, name="attn_bwd", grid=(nb, N_HEADS, S // tq),
        in_specs=[qspec, kspec, rspec, qspec, tab, tab, tab, tab], out_specs=[qspec, kspec, rspec],
        out_shape=[jax.ShapeDtypeStruct((nb, S, QP), BF16), jax.ShapeDtypeStruct((nb, T, QP), F32),
                   jax.ShapeDtypeStruct((nb, T, HEAD_BLOCK), F32)],
        compiler_params=_cparams("arbitrary", "arbitrary", "arbitrary"),
    )(q_raw, kv, kr, do, cos_q, sin_q, cos, sin)


CONV_HALO = 8


def _segments(n, n_ctx):
    if n_ctx == 0:
        return [(0, n, CONV_HALO)]
    return [(0, n_ctx, CONV_HALO), (n_ctx, n - n_ctx, 2 * CONV_HALO + n_ctx)]


def _halo_scratch(n, n_ctx, tc):
    return pltpu.VMEM((n + CONV_HALO * (len(_segments(n, n_ctx)) + 1), tc), F32)


def _zero_halos(scr, segs):
    z = jnp.zeros((CONV_HALO, scr.shape[1]), scr.dtype)
    scr[0:CONV_HALO, :] = z
    for (_, rows, off) in segs:
        scr[off + rows:off + rows + CONV_HALO, :] = z


CONV_BLOCK_MAX = 256


def _conv_block(n, n_ctx):
    return _tile(math.gcd(n_ctx, n - n_ctx) if n_ctx else n, CONV_BLOCK_MAX, 8)


def _window(scr, off, r0, blk):
    return scr[pl.ds(pl.multiple_of(off - CONV_HALO + r0, 8), blk + 2 * CONV_HALO), :]


def _shifted(win, s):
    v = win if s == 0 else pltpu.roll(win, (-s) % win.shape[0], 0)
    return v[CONV_HALO:win.shape[0] - CONV_HALO]


def _tap_blocks(win, k, sign):
    return [_shifted(win, sign * (o - k // 2)) for o in range(k)]


def _taps(blocks, w):
    acc = None
    for o, blk in enumerate(blocks):
        t = w[o:o + 1, :] * blk
        acc = t if acc is None else acc + t
    return acc


def _tap_grads(xblocks, dpre):
    k = len(xblocks)
    sub8 = lax.broadcasted_iota(jnp.int32, (8, dpre.shape[1]), 0)
    out = jnp.where(sub8 == k, jnp.sum(dpre, axis=0, keepdims=True), 0.0)
    for o, blk in enumerate(xblocks):
        out = out + jnp.where(sub8 == o, jnp.sum(dpre * blk, axis=0, keepdims=True), 0.0)
    return out


def _row_blocks(rows, blk, fn, init=0):
    return lax.fori_loop(0, rows // blk, lambda i, c: fn(pl.multiple_of(i * blk, blk), c), init)


def _gelu(x):
    return 0.5 * x * (1.0 + lax.erf(x * (1.0 / math.sqrt(2.0))))


def _gelu_and_grad(x):
    cdf = 0.5 * (1.0 + lax.erf(x * (1.0 / math.sqrt(2.0))))
    return x * cdf, cdf + x * jnp.exp(-0.5 * x * x) * (1.0 / math.sqrt(2.0 * math.pi))


def ssd_conv_fwd(u, w8, bias, n_ctx, tc):
    nb, T, _ = u.shape
    cb0 = OFF_XBC // tc

    segs, blk = _segments(T, n_ctx), _conv_block(T, n_ctx)

    def body(x_ref, w_ref, b_ref, o_ref, xs):
        _zero_halos(xs, segs)
        for (start, rows, off) in segs:
            xs[off:off + rows, :] = x_ref[0, start:start + rows, :]
        w, bias_v = w_ref[...], b_ref[...]
        for (start, rows, off) in segs:
            def block(r0, carry, start=start, off=off):
                pre = bias_v + _taps(_tap_blocks(_window(xs, off, r0, blk), SSD_K, 1), w)
                o_ref[0, pl.ds(pl.multiple_of(start + r0, blk), blk), :] = _silu(pre)
                return carry

            _row_blocks(rows, blk, block)

    return pl.pallas_call(
        body, name="ssd_conv_fwd", grid=(nb, XBC // tc),
        in_specs=[pl.BlockSpec((1, T, tc), lambda b, j: (b, 0, cb0 + j)),
                  pl.BlockSpec((8, tc), lambda b, j: (0, j)), pl.BlockSpec((1, tc), lambda b, j: (0, j))],
        out_specs=pl.BlockSpec((1, T, tc), lambda b, j: (b, 0, j)),
        out_shape=jax.ShapeDtypeStruct((nb, T, XBC), F32),
        scratch_shapes=[_halo_scratch(T, n_ctx, tc)],
        compiler_params=_cparams("arbitrary", "arbitrary"),
    )(u, w8, bias)


def ssd_conv_bwd(u, w8, bias, dxbc, dxs_direct, n_ctx, tc):
    nb, T, _ = u.shape
    cb0 = OFF_XBC // tc
    n_direct = D_INNER // tc

    segs, blk = _segments(T, n_ctx), _conv_block(T, n_ctx)

    def body(x_ref, w_ref, b_ref, d0_ref, d1_ref, dd_ref, dx_ref, dw_ref, xs, ds):
        j, b = pl.program_id(0), pl.program_id(1)
        _zero_halos(xs, segs)
        _zero_halos(ds, segs)
        for (start, rows, off) in segs:
            xs[off:off + rows, :] = x_ref[0, start:start + rows, :]
        w, bias_v = w_ref[...], b_ref[...]
        has_direct = (j < n_direct).astype(F32)
        rows = jnp.zeros((8, tc), F32)
        for (start, n_rows, off) in segs:
            def block(r0, acc, start=start, off=off):
                xblocks = _tap_blocks(_window(xs, off, r0, blk), SSD_K, 1)
                pre = bias_v + _taps(xblocks, w)
                d = d0_ref[0, 0, pl.ds(pl.multiple_of(start + r0, blk), blk), :] + d1_ref[0, 0, pl.ds(pl.multiple_of(start + r0, blk), blk), :]
                if start == n_ctx:
                    d = d + dd_ref[0, pl.ds(r0, blk), :] * has_direct
                sg = jax.nn.sigmoid(pre)
                dpre = d * (sg * (1.0 + pre * (1.0 - sg)))
                ds[pl.ds(pl.multiple_of(off + r0, 8), blk), :] = dpre
                return acc + _tap_grads(xblocks, dpre)

            rows = _row_blocks(n_rows, blk, block, rows)
        for (start, n_rows, off) in segs:
            def block_dx(r0, carry, start=start, off=off):
                dx_ref[0, pl.ds(pl.multiple_of(start + r0, blk), blk), :] = _taps(_tap_blocks(_window(ds, off, r0, blk), SSD_K, -1), w).astype(dx_ref.dtype)
                return carry

            _row_blocks(n_rows, blk, block_dx)

        @pl.when(b == 0)
        def _():
            dw_ref[...] = rows

        @pl.when(b > 0)
        def _():
            dw_ref[...] += rows

    dspec0 = pl.BlockSpec((1, 1, T, tc), lambda j, b: (0, b, 0, j))
    dspec1 = pl.BlockSpec((1, 1, T, tc), lambda j, b: (1, b, 0, j))
    return pl.pallas_call(
        body, name="ssd_conv_bwd", grid=(XBC // tc, nb),
        in_specs=[pl.BlockSpec((1, T, tc), lambda j, b: (b, 0, cb0 + j)),
                  pl.BlockSpec((8, tc), lambda j, b: (0, j)), pl.BlockSpec((1, tc), lambda j, b: (0, j)),
                  dspec0, dspec1,
                  pl.BlockSpec((1, T - n_ctx, tc), lambda j, b: (b, 0, jnp.minimum(j, n_direct - 1)))],
        out_specs=[pl.BlockSpec((1, T, tc), lambda j, b: (b, 0, j)), pl.BlockSpec((8, tc), lambda j, b: (0, j))],
        out_shape=[jax.ShapeDtypeStruct((nb, T, XBC), BF16), jax.ShapeDtypeStruct((8, XBC), F32)],
        scratch_shapes=[_halo_scratch(T, n_ctx, tc), _halo_scratch(T, n_ctx, tc)],
        compiler_params=_cparams("arbitrary", "arbitrary"),
    )(u, w8, bias, dxbc, dxbc, dxs_direct)


GLU_TC = 256


def glu_interleave(w_up):
    blocks = []
    for j in range(D_FF // GLU_TC):
        blocks += [w_up[:, j * GLU_TC:(j + 1) * GLU_TC], w_up[:, D_FF + j * GLU_TC:D_FF + (j + 1) * GLU_TC]]
    return jnp.concatenate(blocks, axis=1)


def glu_deinterleave(g):
    nj = D_FF // GLU_TC
    gate = [g[:, 2 * j * GLU_TC:(2 * j + 1) * GLU_TC] for j in range(nj)]
    val = [g[:, (2 * j + 1) * GLU_TC:(2 * j + 2) * GLU_TC] for j in range(nj)]
    return jnp.concatenate(gate + val, axis=1)


def glu_fwd(up, w8, bias):
    nb, S, _ = up.shape
    tc = GLU_TC

    segs, blk = _segments(S, 0), _conv_block(S, 0)
    (_, _, off), = segs

    def body(u_ref, w_ref, b_ref, o_ref, xs):
        _zero_halos(xs, segs)
        xs[off:off + S, :] = u_ref[0, :, :tc]
        w, bias_v = w_ref[...], b_ref[...]

        def block(r0, carry):
            gc = bias_v + _taps(_tap_blocks(_window(xs, off, r0, blk), FFN_K, 1), w)
            o_ref[0, pl.ds(r0, blk), :] = (_gelu(gc) * u_ref[0, pl.ds(r0, blk), tc:]).astype(o_ref.dtype)
            return carry

        _row_blocks(S, blk, block)

    return pl.pallas_call(
        body, name="glu_fwd", grid=(nb, D_FF // tc),
        in_specs=[pl.BlockSpec((1, S, 2 * tc), lambda b, j: (b, 0, j)),
                  pl.BlockSpec((8, tc), lambda b, j: (0, j)), pl.BlockSpec((1, tc), lambda b, j: (0, j))],
        out_specs=pl.BlockSpec((1, S, tc), lambda b, j: (b, 0, j)),
        out_shape=jax.ShapeDtypeStruct((nb, S, D_FF), BF16),
        scratch_shapes=[_halo_scratch(S, 0, tc)],
        compiler_params=_cparams("arbitrary", "arbitrary"),
    )(up, w8, bias)


def glu_bwd(up, w8, bias, dact):
    nb, S, _ = up.shape
    tc = GLU_TC

    segs, blk = _segments(S, 0), _conv_block(S, 0)
    (_, _, off), = segs

    def body(u_ref, w_ref, b_ref, d_ref, du_ref, dw_ref, xs, ds):
        b = pl.program_id(1)
        _zero_halos(xs, segs)
        _zero_halos(ds, segs)
        xs[off:off + S, :] = u_ref[0, :, :tc]
        w, bias_v = w_ref[...], b_ref[...]

        def block(r0, acc):
            here = pl.ds(r0, blk)
            xblocks = _tap_blocks(_window(xs, off, r0, blk), FFN_K, 1)
            act, act_grad = _gelu_and_grad(bias_v + _taps(xblocks, w))
            d = d_ref[0, here, :].astype(F32)
            du_ref[0, here, tc:] = (d * act).astype(du_ref.dtype)
            dpre = d * u_ref[0, here, tc:] * act_grad
            ds[pl.ds(pl.multiple_of(off + r0, 8), blk), :] = dpre
            return acc + _tap_grads(xblocks, dpre)

        rows = _row_blocks(S, blk, block, jnp.zeros((8, tc), F32))

        def block_dx(r0, carry):
            du_ref[0, pl.ds(r0, blk), :tc] = _taps(_tap_blocks(_window(ds, off, r0, blk), FFN_K, -1), w).astype(du_ref.dtype)
            return carry

        _row_blocks(S, blk, block_dx)

        @pl.when(b == 0)
        def _():
            dw_ref[...] = rows

        @pl.when(b > 0)
        def _():
            dw_ref[...] += rows

    pair = pl.BlockSpec((1, S, 2 * tc), lambda j, b: (b, 0, j))
    return pl.pallas_call(
        body, name="glu_bwd", grid=(D_FF // tc, nb),
        in_specs=[pair, pl.BlockSpec((8, tc), lambda j, b: (0, j)), pl.BlockSpec((1, tc), lambda j, b: (0, j)),
                  pl.BlockSpec((1, S, tc), lambda j, b: (b, 0, j))],
        out_specs=[pair, pl.BlockSpec((8, tc), lambda j, b: (0, j))],
        out_shape=[jax.ShapeDtypeStruct((nb, S, 2 * D_FF), BF16), jax.ShapeDtypeStruct((8, D_FF), F32)],
        scratch_shapes=[_halo_scratch(S, 0, tc), _halo_scratch(S, 0, tc)],
        compiler_params=_cparams("arbitrary", "arbitrary"),
    )(up, w8, bias, dact)


def _chunk_of(d, k, n_cc, n_ch):
    rev = jnp.where(k < n_cc, n_cc - 1 - k, n_cc + n_ch - 1 - k)
    return jnp.where(d == 1, rev, k)


def _lane_pick(v, lane_iota, l):
    return jnp.sum(jnp.where(lane_iota == l, v, 0.0), axis=1, keepdims=True)


def head_spread_matrix():
    return (jnp.arange(LANE)[:, None] == (jnp.arange(D_INNER)[None, :] // SSD_P)).astype(BF16)


def _split_dot(x, e, dims):
    hi = x.astype(BF16)
    lo = (x - hi.astype(F32)).astype(BF16)
    return (lax.dot_general(hi, e, dims, preferred_element_type=F32)
            + lax.dot_general(lo, e, dims, preferred_element_type=F32))


def _spread(x, e):
    return _split_dot(x, e, (((1,), (0,)), ((), ())))


def _gather_heads(y, e):
    return _split_dot(y, e, (((1,), (1,)), ((), ())))


def _softplus(x):
    return jnp.maximum(x, 0.0) + jnp.log(1.0 + jnp.exp(-jnp.abs(x)))


def ssd_dt_inputs(u, a_log, dt_bias):
    pad = LANE - SSD_HEADS
    dt = u[..., OFF_DT:OFF_DT + 2 * SSD_HEADS]
    dt2 = jnp.stack([jnp.pad(dt[..., i * SSD_HEADS:(i + 1) * SSD_HEADS], ((0, 0), (0, 0), (0, pad))) for i in range(2)])

    def lanes(v):
        return jnp.pad(v.reshape(2, 1, SSD_HEADS), ((0, 0), (0, 0), (0, pad)))

    return dt2, lanes(a_log), lanes(dt_bias)


def _ssd_common(d, dt_raw, alog, dtb):
    Q = dt_raw.shape[0]
    row = lax.broadcasted_iota(jnp.int32, (Q, Q), 0)
    col = lax.broadcasted_iota(jnp.int32, (Q, Q), 1)
    rev = d == 1
    maskb = jnp.where(rev, row, col) <= jnp.where(rev, col, row)
    tri = maskb.astype(F32)
    A = -jnp.exp(alog)
    dtv = _softplus(dt_raw + dtb)
    a = dtv * A
    cum = lax.dot_general(tri, a, (((1,), (0,)), ((), ())), precision=lax.Precision.HIGHEST, preferred_element_type=F32)
    tot = jnp.sum(a, axis=0, keepdims=True)
    return maskb, tri, A, dtv, cum, tot


def ssd_fwd(xbc, dt2, alog2, dtb2, n_ctx, hosted):
    nb, T, _ = xbc.shape
    S = T - n_ctx
    n_ch, n_cc = T // CHUNK, n_ctx // CHUNK
    Q = CHUNK
    n_pairs = SSD_HEADS // 2
    n_ex = hosted.n
    n_in = 5

    def body(*refs):
        x_ref, dt_ref, al_ref, db_ref, e_ref = refs[:n_in]
        send_refs = refs[n_in:n_in + n_ex]
        y_ref, hin_ref = refs[n_in + n_ex:n_in + 2 + n_ex]
        recv_refs = refs[n_in + 2 + n_ex:n_in + 2 + 2 * n_ex]
        H, *sems = refs[n_in + 2 + 2 * n_ex:]
        d, k = pl.program_id(1), pl.program_id(2)
        first_step = jnp.logical_and(jnp.logical_and(pl.program_id(0) == 0, d == 0), k == 0)
        last_step = jnp.logical_and(jnp.logical_and(pl.program_id(0) == nb - 1, d == 1), k == n_ch - 1)
        begin_exchange, end_exchange = hosted.steps(send_refs, recv_refs, sems, first_step, last_step)
        begin_exchange()

        @pl.when(k == 0)
        def _():
            H[...] = jnp.zeros_like(H)

        maskb, tri, A, dtv, cum, tot = _ssd_common(d, dt_ref[0, 0], al_ref[0], db_ref[0])
        e = e_ref[...]
        cumT = cum.T
        cum_e, dt_e = _spread(cum, e), _spread(dtv, e)
        tot_e = _spread(jnp.broadcast_to(tot, (8, LANE)), e)[0:1]
        hin_ref[0, 0, 0] = H[...].astype(BF16)
        lane = lax.broadcasted_iota(jnp.int32, (Q, LANE), 1)
        lane1 = lax.broadcasted_iota(jnp.int32, (1, LANE), 1)
        subc = lax.broadcasted_iota(jnp.int32, (LANE, 1), 0)
        half = lane < SSD_P
        for g in range(SSD_GROUPS):
            Bg = x_ref[0, :, D_INNER + g * SSD_N:D_INNER + (g + 1) * SSD_N].astype(BF16)
            Cg = x_ref[0, :, D_INNER + GN + g * SSD_N:D_INNER + GN + (g + 1) * SSD_N].astype(BF16)
            Gm = lax.dot_general(Cg, Bg, (((1,), (1,)), ((), ())), preferred_element_type=F32)
            for pr in range(n_pairs // SSD_GROUPS):
                p = g * (n_pairs // SSD_GROUPS) + pr
                sc, dtp, totp = [t[:, p * LANE:(p + 1) * LANE] for t in (cum_e, dt_e, tot_e)]
                swapped = pltpu.roll(sc, SSD_P, 1)
                s0c, s1c = jnp.where(half, sc, swapped), jnp.where(half, swapped, sc)
                s0r, s1r = cumT[2 * p:2 * p + 1, :], cumT[2 * p + 1:2 * p + 2, :]
                tot0, tot1 = _lane_pick(tot, lane1, 2 * p), _lane_pick(tot, lane1, 2 * p + 1)
                M0 = (Gm * jnp.exp(jnp.where(maskb, s0c - s0r, NEG_BIG))).astype(BF16)
                M1 = (Gm * jnp.exp(jnp.where(maskb, s1c - s1r, NEG_BIG))).astype(BF16)
                xd = x_ref[0, :, p * LANE:(p + 1) * LANE] * dtp
                xdb = xd.astype(BF16)
                yd = jnp.where(half,
                               lax.dot_general(M0, xdb, (((1,), (0,)), ((), ())), preferred_element_type=F32),
                               lax.dot_general(M1, xdb, (((1,), (0,)), ((), ())), preferred_element_type=F32))
                Hp = H[p * LANE:(p + 1) * LANE, :]
                yo = lax.dot_general(Cg, Hp.astype(BF16), (((1,), (1,)), ((), ())), preferred_element_type=F32) * jnp.exp(sc)
                y_ref[0, 0, :, p * LANE:(p + 1) * LANE] = yd + yo
                xdw = (xd * jnp.exp(totp - sc)).astype(BF16)
                etot = jnp.exp(jnp.where(subc < SSD_P, tot0, tot1))
                H[p * LANE:(p + 1) * LANE, :] = Hp * etot + lax.dot_general(
                    xdw, Bg, (((0,), (0,)), ((), ())), preferred_element_type=F32)
        end_exchange()

    def ymap(b, d, k):
        return (d, b, _chunk_of(d, jnp.maximum(k, n_cc), n_cc, n_ch) - n_cc, 0)

    return pl.pallas_call(
        body, name="ssd_fwd", grid=(nb, 2, n_ch),
        in_specs=[pl.BlockSpec((1, Q, XBC), lambda b, d, k: (b, _chunk_of(d, k, n_cc, n_ch), 0)),
                  pl.BlockSpec((1, 1, Q, LANE), lambda b, d, k: (d, b, _chunk_of(d, k, n_cc, n_ch), 0)),
                  pl.BlockSpec((1, 1, LANE), lambda b, d, k: (d, 0, 0)), pl.BlockSpec((1, 1, LANE), lambda b, d, k: (d, 0, 0)),
                  pl.BlockSpec((LANE, D_INNER), lambda b, d, k: (0, 0))] + hosted.specs,
        out_specs=[pl.BlockSpec((1, 1, Q, D_INNER), ymap),
                   pl.BlockSpec((1, 1, 1, D_INNER, SSD_N), lambda b, d, k: (d, b, k, 0, 0))] + hosted.specs,
        out_shape=[jax.ShapeDtypeStruct((2, nb, S, D_INNER), F32),
                   jax.ShapeDtypeStruct((2, nb, n_ch, D_INNER, SSD_N), BF16)] + hosted.out_shape,
        scratch_shapes=[pltpu.VMEM((D_INNER, SSD_N), F32)] + hosted.scratch,
        compiler_params=_cparams("arbitrary", "arbitrary", "arbitrary"),
    )(xbc, dt2, alog2, dtb2, head_spread_matrix(), *hosted.arrays)


def ssd_bwd(xbc, dt2, alog2, dtb2, hin, dy, n_ctx, hosted):
    nb, T, _ = xbc.shape
    n_ex = hosted.n
    n_ch, n_cc = T // CHUNK, n_ctx // CHUNK
    n_in = 7
    Q = CHUNK
    n_pairs = SSD_HEADS // 2
    NT = (((1,), (1,)), ((), ()))
    NN = (((1,), (0,)), ((), ()))
    TN = (((0,), (0,)), ((), ()))

    def dot(a, b, dims):
        return lax.dot_general(a.astype(BF16), b.astype(BF16), dims, preferred_element_type=F32)

    def body(*refs):
        x_ref, dt_ref, al_ref, db_ref, e_ref, hin_ref, dy_ref = refs[:n_in]
        send_refs = refs[n_in:n_in + n_ex]
        dx_ref, ddt_ref, st_ref = refs[n_in + n_ex:n_in + 3 + n_ex]
        recv_refs = refs[n_in + 3 + n_ex:n_in + 3 + 2 * n_ex]
        dH, dce, dde, *sems = refs[n_in + 3 + 2 * n_ex:]
        d, kk = pl.program_id(1), pl.program_id(2)
        ks = n_ch - 1 - kk
        first_step = jnp.logical_and(jnp.logical_and(pl.program_id(0) == 0, d == 0), kk == 0)
        last_step = jnp.logical_and(jnp.logical_and(pl.program_id(0) == nb - 1, d == 1), kk == n_ch - 1)
        begin_exchange, end_exchange = hosted.steps(send_refs, recv_refs, sems, first_step, last_step)
        begin_exchange()

        @pl.when(kk == 0)
        def _():
            dH[...] = jnp.zeros_like(dH)

        @pl.when(jnp.logical_and(jnp.logical_and(pl.program_id(0) == 0, d == 0), kk == 0))
        def _():
            st_ref[...] = jnp.zeros_like(st_ref)

        dt_raw = dt_ref[0, 0]
        alog, dtb_v = al_ref[0], db_ref[0]
        maskb, tri, A, dtv, cum, tot = _ssd_common(d, dt_raw, alog, dtb_v)
        e = e_ref[...]
        cumT = cum.T
        cum_e, dt_e = _spread(cum, e), _spread(dtv, e)
        tot_e = _spread(jnp.broadcast_to(tot, (8, LANE)), e)[0:1]
        live = (ks >= n_cc).astype(F32)
        lane = lax.broadcasted_iota(jnp.int32, (Q, LANE), 1)
        lane1 = lax.broadcasted_iota(jnp.int32, (1, LANE), 1)
        sub = lax.broadcasted_iota(jnp.int32, (LANE, Q), 0)
        subc = lax.broadcasted_iota(jnp.int32, (LANE, 1), 0)
        half = lane < SSD_P
        halfc = subc < SSD_P
        pair_ones = ((lax.broadcasted_iota(jnp.int32, (2 * Q, LANE), 0) >= Q).astype(jnp.int32)
                     == (lax.broadcasted_iota(jnp.int32, (2 * Q, LANE), 1) >= SSD_P).astype(jnp.int32)).astype(BF16)
        dcumT = jnp.zeros((LANE, Q), F32)
        dtot = jnp.zeros((1, LANE), F32)
        dtot_parts = []
        for g in range(SSD_GROUPS):
            Bg = x_ref[0, :, D_INNER + g * SSD_N:D_INNER + (g + 1) * SSD_N].astype(BF16)
            Cg = x_ref[0, :, D_INNER + GN + g * SSD_N:D_INNER + GN + (g + 1) * SSD_N].astype(BF16)
            Gm = lax.dot_general(Cg, Bg, NT, preferred_element_type=F32)
            dG = jnp.zeros((Q, Q), F32)
            dC = jnp.zeros((Q, SSD_N), F32)
            dB = jnp.zeros((Q, SSD_N), F32)
            for pr in range(n_pairs // SSD_GROUPS):
                p = g * (n_pairs // SSD_GROUPS) + pr
                l0, l1 = 2 * p, 2 * p + 1
                sc, dtp, totp = [t[:, p * LANE:(p + 1) * LANE] for t in (cum_e, dt_e, tot_e)]
                swapped = pltpu.roll(sc, SSD_P, 1)
                s0c, s1c = jnp.where(half, sc, swapped), jnp.where(half, swapped, sc)
                s0r, s1r = cumT[l0:l0 + 1, :], cumT[l1:l1 + 1, :]
                tot0, tot1 = _lane_pick(tot, lane1, l0), _lane_pick(tot, lane1, l1)
                L0 = jnp.exp(jnp.where(maskb, s0c - s0r, NEG_BIG))
                L1 = jnp.exp(jnp.where(maskb, s1c - s1r, NEG_BIG))
                M0, M1 = Gm * L0, Gm * L1
                xs = x_ref[0, :, p * LANE:(p + 1) * LANE]
                xd = xs * dtp
                es = jnp.exp(sc)
                dte = jnp.exp(totp - sc)
                etot = jnp.exp(jnp.where(halfc, tot0, tot1))
                dyp = dy_ref[0, :, p * LANE:(p + 1) * LANE] * live
                Hp = hin_ref[0, 0, 0, p * LANE:(p + 1) * LANE, :]
                dHp = dH[p * LANE:(p + 1) * LANE, :]
                bdh = dot(Bg, dHp, NT)
                mtdy = dot(jnp.concatenate([M0, M1], axis=1), dyp, TN)
                dxd = jnp.where(half, mtdy[:Q], mtdy[Q:]) + bdh * dte
                dy0 = jnp.where(half, dyp, 0.0)
                dm = dot(jnp.concatenate([dy0, dyp - dy0], axis=0), xd, NT)
                dM0, dM1 = dm[:Q], dm[Q:]
                dG = dG + dM0 * L0 + dM1 * L1
                dyes = dyp * es
                xdw = xd * dte
                dC = dC + dot(dyes, Hp, NN)
                dB = dB + dot(xdw, dHp, NN)
                W0, W1 = dM0 * M0, dM1 * M1
                yoff = dot(Cg, Hp, NT) * es
                r_off = dyp * yoff
                r_st = xd * bdh * dte
                hh = jnp.sum(dHp * Hp.astype(F32), axis=1, keepdims=True) * etot
                w_rows = _split_dot(jnp.concatenate([W0, W1], axis=1), pair_ones, NN) * (1.0 / SSD_P)
                dce[:, p * LANE:(p + 1) * LANE] = r_off - r_st + w_rows
                dde[:, p * LANE:(p + 1) * LANE] = dxd * xs
                dtot_parts.append(jnp.sum(r_st, axis=0, keepdims=True))
                for (l, W, hselc) in ((l0, W0, halfc), (l1, W1, jnp.logical_not(halfc))):
                    row_g = -jnp.sum(W, axis=0, keepdims=True)
                    dcumT = dcumT + jnp.where(sub == l, row_g, 0.0)
                    dtot = dtot + jnp.where(lane1 == l, jnp.sum(jnp.where(hselc, hh, 0.0), axis=0, keepdims=True), 0.0)
                dx_ref[0, 0, :, p * LANE:(p + 1) * LANE] = dxd * dtp
                dH[p * LANE:(p + 1) * LANE, :] = dHp * etot + dot(dyes, Cg, TN)
            dx_ref[0, 0, :, D_INNER + g * SSD_N:D_INNER + (g + 1) * SSD_N] = dB + dot(dG, Cg, TN)
            dx_ref[0, 0, :, D_INNER + GN + g * SSD_N:D_INNER + GN + (g + 1) * SSD_N] = dC + dot(dG, Bg, NN)
        dcum_all = dcumT.T + _gather_heads(dce[...], e)
        dtot_e = jnp.broadcast_to(jnp.concatenate(dtot_parts, axis=1), (8, D_INNER))
        dtot = dtot + _gather_heads(dtot_e, e)[0:1]
        da = lax.dot_general(tri, dcum_all, TN, precision=lax.Precision.HIGHEST, preferred_element_type=F32) + dtot
        ddtv = _gather_heads(dde[...], e) + da * A
        ddt_raw = ddtv * jax.nn.sigmoid(dt_raw + dtb_v)
        ddt_ref[0, 0] = ddt_raw
        sub8 = lax.broadcasted_iota(jnp.int32, (8, LANE), 0)
        st_ref[...] += (jnp.where(sub8 == 2 * d, jnp.sum(da * dtv * A, axis=0, keepdims=True), 0.0)
                        + jnp.where(sub8 == 2 * d + 1, jnp.sum(ddt_raw, axis=0, keepdims=True), 0.0))
        end_exchange()

    def cmap(d, kk):
        return _chunk_of(d, n_ch - 1 - kk, n_cc, n_ch)

    def dymap(b, d, kk):
        return (b, _chunk_of(d, jnp.maximum(n_ch - 1 - kk, n_cc), n_cc, n_ch) - n_cc, 0)

    return pl.pallas_call(
        body, name="ssd_bwd", grid=(nb, 2, n_ch),
        in_specs=[pl.BlockSpec((1, Q, XBC), lambda b, d, kk: (b, cmap(d, kk), 0)),
                  pl.BlockSpec((1, 1, Q, LANE), lambda b, d, kk: (d, b, cmap(d, kk), 0)),
                  pl.BlockSpec((1, 1, LANE), lambda b, d, kk: (d, 0, 0)), pl.BlockSpec((1, 1, LANE), lambda b, d, kk: (d, 0, 0)),
                  pl.BlockSpec((LANE, D_INNER), lambda b, d, kk: (0, 0)),
                  pl.BlockSpec((1, 1, 1, D_INNER, SSD_N), lambda b, d, kk: (d, b, n_ch - 1 - kk, 0, 0)),
                  pl.BlockSpec((1, Q, D_INNER), dymap)] + hosted.specs,
        out_specs=[pl.BlockSpec((1, 1, Q, XBC), lambda b, d, kk: (d, b, cmap(d, kk), 0)),
                   pl.BlockSpec((1, 1, Q, LANE), lambda b, d, kk: (d, b, cmap(d, kk), 0)),
                   pl.BlockSpec((8, LANE), lambda b, d, kk: (0, 0))] + hosted.specs,
        out_shape=[jax.ShapeDtypeStruct((2, nb, T, XBC), F32), jax.ShapeDtypeStruct((2, nb, T, LANE), F32),
                   jax.ShapeDtypeStruct((8, LANE), F32)] + hosted.out_shape,
        scratch_shapes=[pltpu.VMEM((D_INNER, SSD_N), F32), pltpu.VMEM((Q, D_INNER), F32), pltpu.VMEM((Q, D_INNER), F32)] + hosted.scratch,
        compiler_params=_cparams("arbitrary", "arbitrary", "arbitrary"),
    )(xbc, dt2, alog2, dtb2, head_spread_matrix(), hin, dy, *hosted.arrays)


def _adamw(w, g, m, v):
    mn = ADAM_B1 * m + (1.0 - ADAM_B1) * g
    vn = ADAM_B2 * v + (1.0 - ADAM_B2) * jnp.square(g)
    m_hat = mn / (1.0 - ADAM_B1 ** ADAM_STEP)
    v_hat = vn / (1.0 - ADAM_B2 ** ADAM_STEP)
    return -ADAM_LR * (m_hat / (jnp.sqrt(v_hat) + ADAM_EPS) + ADAM_WD * w), mn, vn


def adamw_matrix(name, w, g_slots, m, v):
    K, n = w.shape
    s = g_slots.shape[0]
    tr = _tile(K, 256, 8)

    def body(w_ref, g_ref, m_ref, v_ref, go_ref, d_ref, mo_ref, vo_ref):
        g = g_ref[0].astype(F32)
        for j in range(1, s):
            g = g + g_ref[j].astype(F32)
        go_ref[...] = g
        d_ref[...], mo_ref[...], vo_ref[...] = _adamw(w_ref[...], g, m_ref[...], v_ref[...])

    spec = pl.BlockSpec((tr, n), lambda i: (i, 0))
    return pl.pallas_call(
        body, name=name, grid=(K // tr,),
        in_specs=[spec, pl.BlockSpec((s, tr, n), lambda i: (0, i, 0)), spec, spec], out_specs=[spec] * 4,
        out_shape=[jax.ShapeDtypeStruct((K, n), F32)] * 4,
        compiler_params=_cparams("arbitrary"),
    )(w, g_slots, m, v)


def adamw_small(ws, gs, ms, vs):
    n = len(ws)

    def body(*refs):
        for i in range(n):
            d, mn, vn = _adamw(refs[i][...], refs[n + i][...], refs[2 * n + i][...], refs[3 * n + i][...])
            refs[4 * n + i][...] = d
            refs[5 * n + i][...] = mn
            refs[6 * n + i][...] = vn

    shapes = [jax.ShapeDtypeStruct(w.shape, F32) for w in ws]
    out = pl.pallas_call(body, name="adamw_small", out_shape=shapes * 3)(*ws, *gs, *ms, *vs)
    return out[:n], out[n:2 * n], out[2 * n:]


def sum_slots(name, x):
    n = x.shape[0]

    def fn(t):
        acc = t[0]
        for j in range(1, n):
            acc = acc + t[j]
        return (acc,)

    return ew_call(name, fn, [x], [(x.shape[1:], F32)])[0]


def _pack_rows(parts):
    rows = []
    for p in parts:
        flat = p.reshape(1, -1)
        n = flat.shape[1]
        rows.append(jnp.pad(flat, ((0, 0), (0, -(-n // (8 * LANE)) * 8 * LANE - n))).reshape(-1, LANE))
    return jnp.concatenate(rows, axis=0)


def _unpack_rows(pack, shapes):
    out, r = [], 0
    for s in shapes:
        n = int(np.prod(s))
        nr = -(-n // (8 * LANE)) * 8
        out.append(pack[r:r + nr].reshape(1, -1)[:, :n].reshape(s))
        r += nr
    return out


def _mesh_pos():
    return lax.axis_index("x"), lax.axis_index("y"), lax.axis_index("c")


N_PEERS = N_DEV - 1


def all_gather(name, vs):
    n = len(vs)

    def body(*refs):
        _ag_start(refs[:n], refs[n:2 * n], *refs[2 * n:])
        _ag_finish(refs[:n], refs[n:2 * n], *refs[2 * n:])

    hbm = pl.BlockSpec(memory_space=pl.ANY)
    return pl.pallas_call(
        body, name=name, out_shape=_ag_out_shape(vs), in_specs=[hbm] * n, out_specs=[hbm] * n,
        scratch_shapes=_a2a_scratch(n),
    )(*vs)


def _ag_out_shape(vs):
    return [jax.ShapeDtypeStruct((N_DEV,) + v.shape, v.dtype) for v in vs]


def _ag_copies(x_refs, out_refs, send_sems, recv_sems, local_sems):
    n = len(x_refs)
    x, y, c = _mesh_pos()
    me, sibling = (x, y, c), (x, y, 1 - c)
    chips = [(1 - x, y), (x, 1 - y), (1 - x, 1 - y)]

    def slot(a, px, py, pc):
        return out_refs[a].at[4 * px + 2 * py + pc]

    def copy(a, k, block, to, src=None):
        return pltpu.make_async_remote_copy(
            src_ref=slot(a, *block) if src is None else src, dst_ref=slot(a, *block),
            send_sem=send_sems.at[N_PEERS * a + k], recv_sem=recv_sems.at[N_PEERS * a + k],
            device_id=to, device_id_type=MESH)

    local = [pltpu.make_async_copy(x_refs[a], slot(a, *me), local_sems.at[a]) for a in range(n)]
    first = []
    for a in range(n):
        first.append(copy(a, 0, me, sibling, src=x_refs[a]))
        first += [copy(a, 1 + j, me, (*chip, c), src=x_refs[a]) for j, chip in enumerate(chips)]
    passed = [(copy(a, 1 + j, (*chip, c), me), copy(a, 4 + j, (*chip, c), sibling))
              for j, chip in enumerate(chips) for a in range(n)]
    from_sibling = []
    for a in range(n):
        from_sibling.append(copy(a, 0, sibling, me))
        from_sibling += [copy(a, 4 + j, (*chip, 1 - c), me) for j, chip in enumerate(chips)]
    return local, first, passed, from_sibling


def _ag_start(*refs):
    local, first, _, _ = _ag_copies(*refs)
    for cp in local + first:
        cp.start()


def _ag_finish(*refs):
    local, first, passed, from_sibling = _ag_copies(*refs)
    for arrived, hand_on in passed:
        arrived.wait_recv()
        hand_on.start()
    for cp in from_sibling:
        cp.wait_recv()
    for cp in first + [hand_on for _, hand_on in passed]:
        cp.wait_send()
    for cp in local:
        cp.wait()


def _a2a_scratch(n):
    return [pltpu.SemaphoreType.DMA((N_PEERS * n,)), pltpu.SemaphoreType.DMA((N_PEERS * n,)), pltpu.SemaphoreType.DMA((n,))]


def _a2a_copies(x_refs, out_refs, send_sems, recv_sems, local_sems):
    n = len(x_refs)
    x, y, c = _mesh_pos()
    me = 4 * x + 2 * y + c
    local = [pltpu.make_async_copy(x_refs[a].at[me], out_refs[a].at[me], local_sems.at[a]) for a in range(n)]
    remote = []
    for k in range(1, N_DEV):
        px, py, pc = x ^ ((k >> 2) & 1), y ^ ((k >> 1) & 1), c ^ (k & 1)
        for a in range(n):
            remote.append(pltpu.make_async_remote_copy(
                src_ref=x_refs[a].at[4 * px + 2 * py + pc], dst_ref=out_refs[a].at[me],
                send_sem=send_sems.at[N_PEERS * a + k - 1], recv_sem=recv_sems.at[N_PEERS * a + k - 1],
                device_id=(px, py, pc), device_id_type=MESH))
    return local, remote


def _a2a_start(local, remote):
    for cp in local + remote:
        cp.start()


def _a2a_wait(local, remote):
    for cp in remote:
        cp.wait_recv()
    for cp in remote:
        cp.wait_send()
    for cp in local:
        cp.wait()


class Hosted:
    def __init__(self, start=None, finish=None, arrays=(), out_shape=()):
        self.start, self.finish, self.arrays, self.out_shape = start, finish, list(arrays), list(out_shape)
        self.n = len(self.arrays)
        self.specs = [pl.BlockSpec(memory_space=pl.ANY)] * self.n
        self.scratch = _a2a_scratch(self.n) if self.n else []

    def steps(self, send_refs, recv_refs, sems, first_step, last_step):
        def begin():
            if self.n:
                pl.when(first_step)(lambda: self.start(send_refs, recv_refs, *sems))

        def end():
            if self.n:
                pl.when(last_step)(lambda: self.finish(send_refs, recv_refs, *sems))

        return begin, end


def hosted_all_to_all(vs):
    return Hosted(lambda *r: _a2a_start(*_a2a_copies(*r)), lambda *r: _a2a_wait(*_a2a_copies(*r)), vs,
                  [jax.ShapeDtypeStruct(v.shape, v.dtype) for v in vs])


def hosted_all_gather(vs):
    return Hosted(_ag_start, _ag_finish, vs, _ag_out_shape(vs))


def _taps8(w):
    return jnp.concatenate([w, jnp.zeros((8 - w.shape[0], w.shape[1]), w.dtype)], axis=0)


FIRST = ("w_in",)
LATE_WEIGHTS = ("w_out", "w_up", "w_down", "w_q_up", "w_kv_up")


def first_weights_to_internal(w_in):
    cq, ckv, kr, z, xbc, dt = jnp.split(w_in, np.cumsum(IN_SPLITS)[:-1].tolist(), axis=1)
    K = w_in.shape[0]

    def zeros(n):
        return jnp.zeros((K, n), w_in.dtype)

    w_in_p = jnp.concatenate([cq, zeros(KR_LANE), kr, zeros(LANE - KR_LANE - ROPE), ckv, zeros(OFF_Z - OFF_CKV - KV_RANK),
                              z, xbc, dt, zeros(WIN_P - OFF_DT - 2 * SSD_HEADS)], axis=1)
    return dict(w_in_p=w_in_p)


def late_weights_to_internal(w_out, w_up, w_down, w_q_up, w_kv_up):
    attn_rows = w_out[:N_HEADS * V_DIM].reshape(N_HEADS, V_DIM, -1)
    w_out_p = jnp.concatenate([jnp.pad(attn_rows, ((0, 0), (HEAD_BLOCK - V_DIM, 0), (0, 0))).reshape(QP, -1),
                               w_out[N_HEADS * V_DIM:]], axis=0)
    w_q_p = jnp.pad(w_q_up.reshape(Q_RANK, N_HEADS, NOPE + ROPE), ((0, 0), (0, 0), (0, HEAD_BLOCK - NOPE - ROPE))).reshape(Q_RANK, QP)
    return dict(w_out_p=w_out_p, w_up=glu_interleave(w_up), w_down=w_down, w_q_p=w_q_p, w_kv=w_kv_up)


def _q_grad(g_q_p):
    return g_q_p.reshape(Q_RANK, N_HEADS, HEAD_BLOCK)[:, :, :NOPE + ROPE].reshape(Q_RANK, -1)


def _out_grad(g_out_p):
    return jnp.concatenate([g_out_p[:QP].reshape(N_HEADS, HEAD_BLOCK, -1)[:, HEAD_BLOCK - V_DIM:].reshape(N_HEADS * V_DIM, -1),
                            g_out_p[QP:]], axis=0)


EARLY = ("w_out", "w_up", "w_down", "w_q_up", "w_kv_up")


def local_step(x, ctx, target, mod_x, mod_c, W, late_shards, V):
    nb, S, D = x.shape
    C = ctx.shape[1]
    T = C + S
    tr = _tile(math.gcd(C, S), 256, 8)
    tq = _tile(S, 256, 8)
    tc = 256
    cblk = C // tr
    m = [mod_x[:, i * D:(i + 1) * D][:, None, :] for i in range(N_MOD)]
    mc = [mod_c[:, i * D:(i + 1) * D] for i in range(2)]
    ssd_w8, ffn_w8 = _taps8(V["ssd_conv_w"]), _taps8(V["ffn_conv_w"])
    dexp = jnp.repeat(V["ssd_d"].reshape(-1), SSD_P).reshape(1, D_INNER)
    cosT, sinT = rope_tables(C, S)
    cosS, sinS = cosT[C:], sinT[C:]

    (h1x,) = rows_fwd("prenorm_x", fn_prenorm, nb, S // tr, tr, [(x, D, 0, 0)], [m[0], m[1]], [V["mix_pre_norm"]], [(D, BF16)])
    (h1c,) = rows_fwd("prenorm_c", fn_prenorm, nb, C // tr, tr, [(ctx, D, 0, 0)], [], [mc[0], mc[1], V["mix_pre_norm"]], [(D, BF16)])
    h1 = jnp.concatenate([h1c, h1x], axis=1).reshape(nb * T, D)
    u = matmul("in_proj", [(h1, W["w_in_p"])], "nn", F32).reshape(nb, T, WIN_P)
    xbc = ssd_conv_fwd(u, ssd_w8, V["ssd_conv_b"], C, tc)
    dt2, alog2, dtb2 = ssd_dt_inputs(u, V["ssd_a_log"], V["ssd_dt_bias"])
    y2, hin, *late = ssd_fwd(xbc, dt2, alog2, dtb2, C, hosted_all_gather(late_shards))
    W = dict(W, **late_weights_to_internal(*[_whole(s, n) for s, n in zip(late, LATE_WEIGHTS)]))
    y2 = y2.reshape(2 * nb, S, D_INNER)
    (qn,) = rows_fwd("q_norm", fn_rms, nb, S // tr, tr, [(u, Q_RANK, OFF_CQ // Q_RANK, cblk)], [], [V["q_norm"]], [(Q_RANK, BF16)])
    (kvn,) = rows_fwd("kv_norm", fn_rms, nb, T // tr, tr, [(u, KV_RANK, OFF_CKV // KV_RANK, 0)], [], [V["kv_norm"]], [(KV_RANK, BF16)])
    qn2, kvn2 = qn.reshape(nb * S, Q_RANK), kvn.reshape(nb * T, KV_RANK)
    q_raw = matmul("q_up", [(qn2, W["w_q_p"])], "nn", F32).reshape(nb, S, QP)
    kv = matmul("kv_up", [(kvn2, W["w_kv"])], "nn", BF16).reshape(nb, T, QP)
    cos_q, sin_q = cosS * Q_PRESCALE, sinS * Q_PRESCALE
    kr = rope_call("rope_k", u, LANE, OFF_KR // LANE, cosT, sinT, BF16, tr)
    o = attn_fwd(q_raw, kv, kr, cos_q, sin_q, tq)
    fin_rows = [(y2, D_INNER, 0, 0, 0), (y2, D_INNER, 0, 0, nb), (xbc, D_INNER, 0, cblk), (u, D_INNER, OFF_Z // D_INNER, cblk)]
    fin_gl = [dexp, V["ssd_norm"]]
    (ssd,) = rows_fwd("ssd_finish", fn_ssd_finish, nb, S // tr, tr, fin_rows, [], fin_gl, [(D_INNER, BF16)])
    o2, ssd2 = o.reshape(nb * S, QP), ssd.reshape(nb * S, D_INNER)
    mix = matmul("out_proj", [(o2, W["w_out_p"][:QP]), (ssd2, W["w_out_p"][QP:])], "nn", F32).reshape(nb, S, D)
    pm_rows = [(x, D, 0, 0), (mix, D, 0, 0)]
    pm_pb = [m[2], m[4], m[3]]
    pm_gl = [V["mix_post_norm"], V["ffn_pre_norm"]]
    x1, h2 = rows_fwd("postmix", fn_postmix, nb, S // tr, tr, pm_rows, pm_pb, pm_gl, [(D, F32), (D, BF16)])
    h22 = h2.reshape(nb * S, D)
    up = matmul("up_proj", [(h22, W["w_up"])], "nn", F32).reshape(nb, S, 2 * D_FF)
    act = glu_fwd(up, ffn_w8, V["ffn_conv_b"])
    act2 = act.reshape(nb * S, D_FF)
    ffn = matmul("down_proj", [(act2, W["w_down"])], "nn", F32).reshape(nb, S, D)
    dx1, dffn, dgate2, d_ffn_post, loss = final_call(x1, ffn, target, m[5], V["ffn_post_norm"], tr)

    dffn2 = dffn.reshape(nb * S, D)
    dact = matmul("down_dgrad", [(dffn2, W["w_down"])], "nt", BF16).reshape(nb, S, D_FF)
    g_down = matmul_tn("down_wgrad", act2, dffn2)
    dup, ffn_rows = glu_bwd(up, ffn_w8, V["ffn_conv_b"], dact)
    dup2 = dup.reshape(nb * S, 2 * D_FF)
    dh2 = matmul("up_dgrad", [(dup2, W["w_up"])], "nt", BF16).reshape(nb, S, D)
    g_up = matmul_tn("up_wgrad", h22, dup2)
    dx_a, dmix, dgate1, dscale2, dshift2, d_mix_post, d_ffn_pre = rows_bwd(
        "postmix_bwd", fn_postmix, nb, S // tr, tr, pm_rows, pm_pb, pm_gl,
        [(dx1, D, 0, 0), (dh2, D, 0, 0)], [(0, F32), (1, BF16)])
    dmix2 = dmix.reshape(nb * S, D)
    dcat = matmul("out_dgrad", [(dmix2, W["w_out_p"])], "nt", BF16).reshape(nb, S, QP + D_INNER)
    g_out_p = jnp.concatenate([matmul_tn("out_wgrad_attn", o2, dmix2), matmul_tn("out_wgrad_ssd", ssd2, dmix2)], axis=0)
    dy, dxs_direct, dz, d_dexp, d_ssd_norm = rows_bwd(
        "ssd_finish_bwd", fn_ssd_finish, nb, S // tr, tr, fin_rows, [], fin_gl,
        [(dcat, D_INNER, QP // D_INNER, 0)], [(0, F32), (2, F32), (3, BF16)])
    dq_pre, dkv, dkr = attn_bwd(q_raw, kv, kr, dcat, cos_q, sin_q, cosS, sinS, tq)
    dq_pre = dq_pre.reshape(nb * S, QP)
    dkr_pre = rope_call("rope_dk", dkr, LANE, 0, cosT, -sinT, BF16, tr)
    dkv2 = dkv.reshape(nb * T, QP)
    dqn = matmul("q_dgrad", [(dq_pre, W["w_q_p"])], "nt", F32).reshape(nb, S, Q_RANK)
    g_q_p = matmul_tn("q_wgrad", qn2, dq_pre)
    dkvn = matmul("kv_dgrad", [(dkv2, W["w_kv"])], "nt", F32).reshape(nb, T, KV_RANK)
    g_kv = matmul_tn("kv_wgrad", kvn2, dkv2)
    early_grads = (_out_grad(g_out_p), glu_deinterleave(g_up), g_down, _q_grad(g_q_p), g_kv)
    early = hosted_all_to_all([_per_device(g, n) for g, n in zip(early_grads, EARLY)])
    dxbc2, ddt2, ssd_stats, *received = ssd_bwd(xbc, dt2, alog2, dtb2, hin, dy, C, early)
    ddt_block = jnp.concatenate([ddt2[0][..., :SSD_HEADS], ddt2[1][..., :SSD_HEADS],
                                 jnp.zeros((nb, T, LANE - 2 * SSD_HEADS), F32)], axis=-1).astype(BF16)
    dxbc_raw, ssd_rows = ssd_conv_bwd(u, ssd_w8, V["ssd_conv_b"], dxbc2, dxs_direct, C, tc)
    dcq, d_q_norm = rows_bwd("q_norm_bwd", fn_rms, nb, S // tr, tr, [(u, Q_RANK, OFF_CQ // Q_RANK, cblk)], [], [V["q_norm"]],
                             [(dqn, Q_RANK, 0, 0)], [(0, BF16)])
    dckv, d_kv_norm = rows_bwd("kv_norm_bwd", fn_rms, nb, T // tr, tr, [(u, KV_RANK, OFF_CKV // KV_RANK, 0)], [], [V["kv_norm"]],
                               [(dkvn, KV_RANK, 0, 0)], [(0, BF16)])

    def ctx_rows(t):
        return jnp.pad(t, ((0, 0), (C, 0), (0, 0)))

    du = [("cq", ctx_rows(dcq), OFF_CQ, Q_RANK), ("kr", dkr_pre, OFF_KR, LANE), ("ckv", dckv, OFF_CKV, KV_RANK),
          ("z", ctx_rows(dz), OFF_Z, D_INNER), ("xbc", dxbc_raw, OFF_XBC, XBC), ("dt", ddt_block, OFF_DT, LANE)]
    du = [(name, t.reshape(nb * T, w), off, w) for (name, t, off, w) in du]
    g = {name: matmul_tn("in_wgrad_" + name, h1, t) for (name, t, _, _) in du}
    g_in = jnp.concatenate([g["cq"], g["ckv"], g["kr"][:, KR_LANE:KR_LANE + ROPE], g["z"], g["xbc"],
                            g["dt"][:, :2 * SSD_HEADS]], axis=1)
    dh1, received_in = matmul("in_dgrad", [(t, W["w_in_p"][:, off:off + w]) for (_, t, off, w) in du], "nt", BF16,
                              hosted=hosted_all_to_all([_per_device(g_in, "w_in").astype(BF16)]))
    dh1 = dh1.reshape(nb, T, D)

    def fn_prenorm_res(xv, shift, scale, g):
        return fn_prenorm(xv, shift, scale, g) + (xv,)

    grad_x, dshift1, dscale1, d_mix_pre_x = rows_bwd(
        "prenorm_x_bwd", fn_prenorm_res, nb, S // tr, tr, [(x, D, 0, 0)], [m[0], m[1]], [V["mix_pre_norm"]],
        [(dh1, D, 0, cblk), (dx_a, D, 0, 0)], [(0, F32)])
    dshift_c, dscale_c, d_mix_pre_c = rows_bwd(
        "prenorm_c_bwd", fn_prenorm, nb, C // tr, tr, [(ctx, D, 0, 0)], [], [mc[0], mc[1], V["mix_pre_norm"]],
        [(dh1, D, 0, 0)], [])

    dmod_x = jnp.concatenate([dshift1, dscale1, dgate1, dshift2, dscale2, dgate2], axis=-1).reshape(nb, N_MOD * D)
    dmod_c = jnp.concatenate([dshift_c, dscale_c, jnp.zeros((1, (N_MOD - 2) * D), F32)], axis=-1)
    gv = dict(
        mix_pre_norm=d_mix_pre_x + d_mix_pre_c, mix_post_norm=d_mix_post, q_norm=d_q_norm, kv_norm=d_kv_norm,
        ssd_conv_w=ssd_rows[:SSD_K], ssd_conv_b=ssd_rows[SSD_K:SSD_K + 1],
        ssd_a_log=jnp.concatenate([ssd_stats[0:1, :SSD_HEADS], ssd_stats[2:3, :SSD_HEADS]], axis=1),
        ssd_dt_bias=jnp.concatenate([ssd_stats[1:2, :SSD_HEADS], ssd_stats[3:4, :SSD_HEADS]], axis=1),
        ssd_d=jnp.sum(d_dexp.reshape(SSD_HEADS, SSD_P), axis=1).reshape(1, SSD_HEADS), ssd_norm=d_ssd_norm,
        ffn_pre_norm=d_ffn_pre, ffn_post_norm=d_ffn_post,
        ffn_conv_w=ffn_rows[:FFN_K], ffn_conv_b=ffn_rows[FFN_K:FFN_K + 1])
    return loss, grad_x, dmod_x, dmod_c, gv, dict(zip(EARLY, received), w_in=received_in)


WEIGHT_ORDER = ("c_ctx", "w_mod", "b_mod", "mix_pre_norm", "mix_post_norm", "w_in", "q_norm", "w_q_up", "kv_norm",
                "w_kv_up", "ssd_conv_w", "ssd_conv_b", "ssd_a_log", "ssd_dt_bias", "ssd_d", "ssd_norm", "w_out",
                "ffn_pre_norm", "ffn_post_norm", "w_up", "ffn_conv_w", "ffn_conv_b", "w_down")
MATRICES = ("w_in", "w_q_up", "w_kv_up", "w_out", "w_up", "w_down")
ROW_SHARDED = ("w_out", "w_down")
SMALL_SUMMED = ("c_ctx", "mix_pre_norm", "mix_post_norm", "q_norm", "kv_norm", "ssd_conv_w", "ssd_conv_b", "ssd_a_log",
                "ssd_dt_bias", "ssd_d", "ssd_norm", "ffn_pre_norm", "ffn_post_norm", "ffn_conv_w", "ffn_conv_b")
MOD_ROWS = 8


def _whole(shards, name):
    if name in ROW_SHARDED:
        return shards.reshape(-1, shards.shape[-1])
    return jnp.concatenate([shards[j] for j in range(N_DEV)], axis=1)


def _per_device(g, name):
    if name in ROW_SHARDED:
        return g.reshape(N_DEV, -1, g.shape[-1])
    return jnp.stack(jnp.split(g, N_DEV, axis=1))


def kernel(x, c, ctx, c_ctx, w_mod, b_mod, mix_pre_norm, mix_post_norm, w_in, q_norm, w_q_up, kv_norm, w_kv_up, ssd_conv_w, ssd_conv_b, ssd_a_log, ssd_dt_bias, ssd_d, ssd_norm, w_out, ffn_pre_norm, ffn_post_norm, w_up, ffn_conv_w, ffn_conv_b, w_down, loss_target, m_c_ctx, m_w_mod, m_b_mod, m_mix_pre_norm, m_mix_post_norm, m_w_in, m_q_norm, m_w_q_up, m_kv_norm, m_w_kv_up, m_ssd_conv_w, m_ssd_conv_b, m_ssd_a_log, m_ssd_dt_bias, m_ssd_d, m_ssd_norm, m_w_out, m_ffn_pre_norm, m_ffn_post_norm, m_w_up, m_ffn_conv_w, m_ffn_conv_b, m_w_down, v_c_ctx, v_w_mod, v_b_mod, v_mix_pre_norm, v_mix_post_norm, v_w_in, v_q_norm, v_w_q_up, v_kv_norm, v_w_kv_up, v_ssd_conv_w, v_ssd_conv_b, v_ssd_a_log, v_ssd_dt_bias, v_ssd_d, v_ssd_norm, v_w_out, v_ffn_pre_norm, v_ffn_post_norm, v_w_up, v_ffn_conv_w, v_ffn_conv_b, v_w_down):
    weights = dict(c_ctx=c_ctx, w_mod=w_mod, b_mod=b_mod, mix_pre_norm=mix_pre_norm, mix_post_norm=mix_post_norm, w_in=w_in, q_norm=q_norm, w_q_up=w_q_up, kv_norm=kv_norm, w_kv_up=w_kv_up, ssd_conv_w=ssd_conv_w, ssd_conv_b=ssd_conv_b, ssd_a_log=ssd_a_log, ssd_dt_bias=ssd_dt_bias, ssd_d=ssd_d, ssd_norm=ssd_norm, w_out=w_out, ffn_pre_norm=ffn_pre_norm, ffn_post_norm=ffn_post_norm, w_up=w_up, ffn_conv_w=ffn_conv_w, ffn_conv_b=ffn_conv_b, w_down=w_down)
    mom1 = dict(c_ctx=m_c_ctx, w_mod=m_w_mod, b_mod=m_b_mod, mix_pre_norm=m_mix_pre_norm, mix_post_norm=m_mix_post_norm, w_in=m_w_in, q_norm=m_q_norm, w_q_up=m_w_q_up, kv_norm=m_kv_norm, w_kv_up=m_w_kv_up, ssd_conv_w=m_ssd_conv_w, ssd_conv_b=m_ssd_conv_b, ssd_a_log=m_ssd_a_log, ssd_dt_bias=m_ssd_dt_bias, ssd_d=m_ssd_d, ssd_norm=m_ssd_norm, w_out=m_w_out, ffn_pre_norm=m_ffn_pre_norm, ffn_post_norm=m_ffn_post_norm, w_up=m_w_up, ffn_conv_w=m_ffn_conv_w, ffn_conv_b=m_ffn_conv_b, w_down=m_w_down)
    mom2 = dict(c_ctx=v_c_ctx, w_mod=v_w_mod, b_mod=v_b_mod, mix_pre_norm=v_mix_pre_norm, mix_post_norm=v_mix_post_norm, w_in=v_w_in, q_norm=v_q_norm, w_q_up=v_w_q_up, kv_norm=v_kv_norm, w_kv_up=v_w_kv_up, ssd_conv_w=v_ssd_conv_w, ssd_conv_b=v_ssd_conv_b, ssd_a_log=v_ssd_a_log, ssd_dt_bias=v_ssd_dt_bias, ssd_d=v_ssd_d, ssd_norm=v_ssd_norm, w_out=v_w_out, ffn_pre_norm=v_ffn_pre_norm, ffn_post_norm=v_ffn_post_norm, w_up=v_w_up, ffn_conv_w=v_ffn_conv_w, ffn_conv_b=v_ffn_conv_b, w_down=v_w_down)
    nb, S, D = x.shape
    me = 4 * lax.axis_index("x") + 2 * lax.axis_index("y") + lax.axis_index("c")

    *first, c_all, ssd_w_sh, ffn_w_sh = all_gather(
        "gather_first", [weights[n][0].astype(BF16) for n in FIRST] + [c, ssd_conv_w[0], ffn_conv_w[0]])
    W = first_weights_to_internal(*[_whole(s, n) for n, s in zip(FIRST, first)])
    late_shards = [weights[n][0].astype(BF16) for n in LATE_WEIGHTS]
    V = {n: weights[n].reshape(1, -1) for n in SMALL_SUMMED if n != "c_ctx"}
    V["ssd_conv_w"] = _whole(ssd_w_sh, "ssd_conv_w")
    V["ffn_conv_w"] = _whole(ffn_w_sh, "ffn_conv_w")

    n_all = N_DEV * nb
    mod_rows = -(-(n_all + 1) // 8) * 8
    c_pad = jnp.concatenate([c_all.reshape(n_all, D), c_ctx.reshape(1, D), jnp.zeros((mod_rows - n_all - 1, D), F32)], axis=0)
    mod_cols = w_mod.shape[2]
    b_mine = lax.dynamic_slice(b_mod, (0, me * mod_cols), (1, mod_cols))
    mod_part = matmul("mod_proj", [(c_pad, w_mod[0])], "nn", F32, bias=b_mine, silu_a=True)
    mod_all = _whole(all_gather("gather_mod", [mod_part])[0], "w_mod")
    mod_x = lax.dynamic_slice(mod_all, (me * nb, 0), (nb, mod_all.shape[1]))
    mod_c = mod_all[n_all:n_all + 1]

    loss, grad_x, dmod_x, dmod_c, gv, slots = local_step(x, ctx, loss_target, mod_x, mod_c, W, late_shards, V)

    dmod_mine = jnp.concatenate([dmod_x, dmod_c, jnp.zeros((MOD_ROWS - nb - 1, dmod_x.shape[1]), F32)], axis=0)
    dmod_all = all_gather("gather_dmod", [dmod_mine])[0]
    dmod_ctx = sum_slots("sum_dmod_ctx", dmod_all[:, nb:nb + 1].reshape(N_DEV, -1, LANE)).reshape(1, -1)
    dmod_full = jnp.concatenate([dmod_all[:, :nb].reshape(n_all, -1), dmod_ctx,
                                 jnp.zeros((mod_rows - n_all - 1, dmod_ctx.shape[1]), F32)], axis=0)
    (g_b_mod,) = ew_call("mod_bias_grad", lambda t: (jnp.sum(t, axis=0, keepdims=True),), [dmod_full], [((1, dmod_full.shape[1]), F32)])
    dmod_cols = lax.dynamic_slice(dmod_full, (0, me * mod_cols), (mod_rows, mod_cols))
    g_w_mod = matmul_tn("mod_wgrad", c_pad, dmod_cols, silu_a=True)
    dsilu_ctx = matmul("mod_dgrad_ctx", [(dmod_cols[n_all:n_all + 8], w_mod[0])], "nt", F32)[0:1]

    def silu_vjp(cc, ct):
        return (jax.vjp(_silu, cc)[1](ct)[0],)

    (g_c_ctx_part,) = ew_call("c_ctx_grad", silu_vjp, [c_ctx.reshape(1, D), dsilu_ctx], [((1, D), F32)])

    gv = dict(gv, c_ctx=g_c_ctx_part)
    small_parts = [loss] + [gv[n] for n in SMALL_SUMMED]
    small_sum = sum_slots("sum_small", all_gather("gather_small_grads", [_pack_rows(small_parts)])[0])
    summed = _unpack_rows(small_sum, [p.shape for p in small_parts])
    loss_out = summed[0][0, 0]
    grads = {n: g.reshape(weights[n].shape) if n not in ("ssd_conv_w", "ffn_conv_w") else g for n, g in zip(SMALL_SUMMED, summed[1:])}
    for n in ("ssd_conv_w", "ffn_conv_w"):
        cols = weights[n].shape[2]
        grads[n] = lax.dynamic_slice(grads[n], (0, me * cols), (grads[n].shape[0], cols)).reshape(weights[n].shape)
    grads["b_mod"] = g_b_mod.reshape(b_mod.shape)

    slots = dict(slots, w_mod=g_w_mod[None])
    delta, new_m, new_v = {}, {}, {}
    for n in MATRICES + ("w_mod",):
        g, d, mn, vn = adamw_matrix("adamw_" + n, weights[n][0], slots[n], mom1[n][0], mom2[n][0])
        grads[n], delta[n], new_m[n], new_v[n] = [t.reshape(weights[n].shape) for t in (g, d, mn, vn)]
    small = [n for n in WEIGHT_ORDER if n not in slots]

    def two_d(t):
        return t.reshape(-1, t.shape[-1])

    ds, ms, vs = adamw_small(*[[two_d(t[n]) for n in small] for t in (weights, grads, mom1, mom2)])
    for n, d, mn, vn in zip(small, ds, ms, vs):
        delta[n], new_m[n], new_v[n] = [t.reshape(weights[n].shape) for t in (d, mn, vn)]
    return (loss_out, grad_x, *[t[n] for t in (grads, delta, new_m, new_v) for n in WEIGHT_ORDER])
```

```python
import math

import jax
import jax.numpy as jnp
import numpy as np
from jax import lax
from jax.experimental import pallas as pl
from jax.experimental.pallas import tpu as pltpu

F32 = jnp.float32
BF16 = jnp.bfloat16
MESH = pl.DeviceIdType.MESH

D_MODEL = 1024
GRID_W = 64
N_HEADS = 16
NOPE = 64
ROPE = 32
V_DIM = 64
Q_RANK = 384
KV_RANK = 256
ROPE_THETA = 10000.0
ATTN_SCALE = (NOPE + ROPE) ** -0.5
SSD_HEADS = 16
SSD_P = 64
SSD_GROUPS = 2
SSD_N = 128
SSD_K = 5
CHUNK = 128
D_INNER = SSD_HEADS * SSD_P
GN = SSD_GROUPS * SSD_N
XBC = D_INNER + 2 * GN
D_FF = 2816
FFN_K = 3
N_MOD = 6
EPS = 1e-6
IN_SPLITS = (Q_RANK, KV_RANK, ROPE, D_INNER, XBC, 2 * SSD_HEADS)
IN_WIDTH = sum(IN_SPLITS)
N_DEV = 8

ADAM_LR = 0.001
ADAM_B1 = 0.9
ADAM_B2 = 0.999
ADAM_EPS = 1e-08
ADAM_WD = 0.01
ADAM_STEP = 10

LANE = 128
HEAD_BLOCK = 128
OFF_CQ = 0
OFF_KR = 384
OFF_CKV = 512
OFF_Z = 1024
OFF_XBC = 2048
OFF_DT = 3584
WIN_P = 3840
KR_LANE = 64
QP = N_HEADS * HEAD_BLOCK

VMEM_LIMIT_V7X = 56 * 1024 * 1024
NEG_BIG = -1e30


def _cparams(*sem):
    return pltpu.CompilerParams(dimension_semantics=sem, vmem_limit_bytes=VMEM_LIMIT_V7X)


def _tile(n, target, mult=128):
    if n <= target:
        return n
    t = (target // mult) * mult
    while t >= mult:
        if n % t == 0:
            return t
        t -= mult
    return n


def _silu(x):
    return x * jax.nn.sigmoid(x)


def _rms(x, g):
    return x * lax.rsqrt(jnp.mean(x * x, axis=-1, keepdims=True) + EPS) * g


WHOLE_K_WIDE = 2048


def matmul(name, pairs, mode, out_dtype, *, bias=None, silu_a=False, hosted=None):
    n_pairs = len(pairs)
    M = pairs[0][0].shape[0]
    N = pairs[0][1].shape[1] if mode == "nn" else pairs[0][1].shape[0]
    k_total = sum(a.shape[1] for a, _ in pairs)
    tm = _tile(M, 1024 if k_total <= WHOLE_K_WIDE else 512, 8)
    tn = _tile(N, 1408 if k_total <= WHOLE_K_WIDE else 512)
    dims = (((1,), (0,)), ((), ())) if mode == "nn" else (((1,), (1,)), ((), ()))
    n_own = 2 * n_pairs + (bias is not None)
    n_ex = hosted.n if hosted else 0

    def body(*refs):
        o_ref = refs[n_own + n_ex]
        if hosted:
            j, i = pl.program_id(0), pl.program_id(1)
            begin_exchange, end_exchange = hosted.steps(
                refs[n_own:n_own + n_ex], refs[n_own + n_ex + 1:n_own + 2 * n_ex + 1], refs[n_own + 2 * n_ex + 1:],
                jnp.logical_and(j == 0, i == 0), jnp.logical_and(j == N // tn - 1, i == M // tm - 1))
            begin_exchange()
        acc = None
        for p in range(n_pairs):
            a = refs[2 * p][...]
            if silu_a:
                a = _silu(a.astype(F32))
            d = lax.dot_general(a.astype(BF16), refs[2 * p + 1][...].astype(BF16), dims, preferred_element_type=F32)
            acc = d if acc is None else acc + d
        if bias is not None:
            acc = acc + refs[2 * n_pairs][...]
        o_ref[...] = acc.astype(o_ref.dtype)
        if hosted:
            end_exchange()

    in_specs, args = [], []
    for a, b in pairs:
        K = a.shape[1]
        in_specs.append(pl.BlockSpec((tm, K), lambda j, i: (i, 0)))
        in_specs.append(pl.BlockSpec((K, tn), lambda j, i: (0, j)) if mode == "nn" else pl.BlockSpec((tn, K), lambda j, i: (j, 0)))
        args += [a, b]
    if bias is not None:
        in_specs.append(pl.BlockSpec((1, tn), lambda j, i: (0, j)))
        args.append(bias)
    out_spec = pl.BlockSpec((tm, tn), lambda j, i: (i, j))
    out_shape = jax.ShapeDtypeStruct((M, N), out_dtype)
    if not hosted:
        return pl.pallas_call(
            body, name=name, grid=(N // tn, M // tm), in_specs=in_specs, out_specs=out_spec, out_shape=out_shape,
            compiler_params=_cparams("arbitrary", "arbitrary"),
        )(*args)
    return pl.pallas_call(
        body, name=name, grid=(N // tn, M // tm), in_specs=in_specs + hosted.specs,
        out_specs=[out_spec] + hosted.specs, out_shape=[out_shape] + hosted.out_shape, scratch_shapes=hosted.scratch,
        compiler_params=_cparams("arbitrary", "arbitrary"),
    )(*args, *hosted.arrays)


def matmul_tn(name, a, b, out_dtype=F32, *, silu_a=False, tm=1408, tn=512, tk=2048):
    R, M = a.shape
    N = b.shape[1]
    tm = _tile(M, tm)
    tn = _tile(N, tn)
    tk = _tile(R, tk, 8)
    nk = R // tk

    def body(a_ref, b_ref, o_ref, acc):
        k = pl.program_id(2)

        @pl.when(k == 0)
        def _():
            acc[...] = jnp.zeros_like(acc)

        x = a_ref[...]
        if silu_a:
            x = _silu(x.astype(F32))
        acc[...] += lax.dot_general(x.astype(BF16), b_ref[...].astype(BF16), (((0,), (0,)), ((), ())),
                                    preferred_element_type=F32)

        @pl.when(k == nk - 1)
        def _():
            o_ref[...] = acc[...].astype(o_ref.dtype)

    return pl.pallas_call(
        body, name=name, grid=(M // tm, N // tn, nk),
        in_specs=[pl.BlockSpec((tk, tm), lambda i, j, k: (k, i)), pl.BlockSpec((tk, tn), lambda i, j, k: (k, j))],
        out_specs=pl.BlockSpec((tm, tn), lambda i, j, k: (i, j)),
        out_shape=jax.ShapeDtypeStruct((M, N), out_dtype),
        scratch_shapes=[pltpu.VMEM((tm, tn), F32)],
        compiler_params=_cparams("arbitrary", "arbitrary", "arbitrary"),
    )(a, b)


def _row_specs(rin, pbin, glin, tr):
    specs = [pl.BlockSpec((1, tr, w), lambda b, i, cb=cb, ro=ro, bo=(e[4] if len(e) > 4 else 0): (b + bo, i + ro, cb))
             for e in rin for (_, w, cb, ro) in [e[:4]]]
    specs += [pl.BlockSpec((1, 1, a.shape[-1]), lambda b, i: (b, 0, 0)) for a in pbin]
    specs += [pl.BlockSpec((1, a.shape[-1]), lambda b, i: (0, 0)) for a in glin]
    return specs


def rows_fwd(name, fn, nb, nblk, tr, rin, pbin, glin, outs):
    nr, npb, ngl = len(rin), len(pbin), len(glin)
    n_in = nr + npb + ngl

    def body(*refs):
        args = [r[0].astype(F32) for r in refs[:nr + npb]] + [r[...] for r in refs[nr + npb:n_in]]
        res = fn(*args)
        for o, v in zip(refs[n_in:], res):
            o[0] = v.astype(o.dtype)

    return pl.pallas_call(
        body, name=name, grid=(nb, nblk), in_specs=_row_specs(rin, pbin, glin, tr),
        out_specs=[pl.BlockSpec((1, tr, w), lambda b, i: (b, i, 0)) for (w, _) in outs],
        out_shape=[jax.ShapeDtypeStruct((nb, nblk * tr, w), dt) for (w, dt) in outs],
        compiler_params=_cparams("arbitrary", "arbitrary"),
    )(*[e[0] for e in rin], *pbin, *glin)


def rows_bwd(name, fn, nb, nblk, tr, rin, pbin, glin, cts, want):
    nr, npb, ngl, nct = len(rin), len(pbin), len(glin), len(cts)
    n_in = nr + npb + ngl

    def body(*refs):
        b, i = pl.program_id(0), pl.program_id(1)
        args = [r[0].astype(F32) for r in refs[:nr + npb]] + [r[...] for r in refs[nr + npb:n_in]]
        ct = tuple(r[0].astype(F32) for r in refs[n_in:n_in + nct])
        _, vjp = jax.vjp(fn, *args)
        g = vjp(ct)
        orefs = refs[n_in + nct:]
        for o, (idx, _) in zip(orefs, want):
            o[0] = g[idx].astype(o.dtype)
        pb_refs = orefs[len(want):len(want) + npb]
        gl_refs = orefs[len(want) + npb:]

        @pl.when(i == 0)
        def _():
            for o, v in zip(pb_refs, g[nr:nr + npb]):
                o[0] = v

        @pl.when(i > 0)
        def _():
            for o, v in zip(pb_refs, g[nr:nr + npb]):
                o[0] += v

        first = jnp.logical_and(b == 0, i == 0)

        @pl.when(first)
        def _():
            for o, v in zip(gl_refs, g[nr + npb:]):
                o[...] = v

        @pl.when(jnp.logical_not(first))
        def _():
            for o, v in zip(gl_refs, g[nr + npb:]):
                o[...] += v

    out_specs = [pl.BlockSpec((1, tr, rin[idx][1]), lambda b, i: (b, i, 0)) for (idx, _) in want]
    out_shape = [jax.ShapeDtypeStruct((nb, nblk * tr, rin[idx][1]), dt) for (idx, dt) in want]
    out_specs += [pl.BlockSpec((1, 1, a.shape[-1]), lambda b, i: (b, 0, 0)) for a in pbin]
    out_shape += [jax.ShapeDtypeStruct((nb, 1, a.shape[-1]), F32) for a in pbin]
    out_specs += [pl.BlockSpec((1, a.shape[-1]), lambda b, i: (0, 0)) for a in glin]
    out_shape += [jax.ShapeDtypeStruct((1, a.shape[-1]), F32) for a in glin]
    return pl.pallas_call(
        body, name=name, grid=(nb, nblk),
        in_specs=_row_specs(rin, pbin, glin, tr) + _row_specs(cts, [], [], tr),
        out_specs=out_specs, out_shape=out_shape,
        compiler_params=_cparams("arbitrary", "arbitrary"),
    )(*[e[0] for e in rin], *pbin, *glin, *[e[0] for e in cts])


def ew_call(name, fn, ins, outs):
    def body(*refs):
        res = fn(*[r[...] for r in refs[:len(ins)]])
        for o, v in zip(refs[len(ins):], res):
            o[...] = v.astype(o.dtype)

    return pl.pallas_call(body, name=name, out_shape=[jax.ShapeDtypeStruct(s, dt) for (s, dt) in outs])(*ins)


def fn_prenorm(x, shift, scale, g):
    return (_rms(x, g) * (1.0 + scale) + shift,)


def fn_rms(x, g):
    return (_rms(x, g),)


def fn_ssd_finish(yf, yr, xs, z, dexp, nw):
    y = yf + yr + dexp * xs
    return (_rms(y * _silu(z), nw),)


def fn_postmix(x, mix, gate1, scale2, shift2, post_g, pre_g):
    x1 = x + gate1 * _rms(mix, post_g)
    h2 = _rms(x1, pre_g) * (1.0 + scale2) + shift2
    return x1, h2


def final_call(x1, ffn, target, gate2, post_g, tr):
    nb, S, D = x1.shape
    nblk = S // tr

    def body(x1_ref, f_ref, t_ref, g2_ref, pg_ref, dx1_ref, df_ref, dg2_ref, dpg_ref, loss_ref):
        b, i = pl.program_id(0), pl.program_id(1)
        tgt = t_ref[0]

        def lossfn(x1v, fv, g2, pg):
            e = x1v + g2 * _rms(fv, pg) - tgt
            return 0.5 * jnp.sum(jnp.mean(e * e, axis=-1, keepdims=True))

        val, (dx1, df, dg2, dpg) = jax.value_and_grad(lossfn, argnums=(0, 1, 2, 3))(
            x1_ref[0], f_ref[0].astype(F32), g2_ref[0], pg_ref[...])
        dx1_ref[0] = dx1
        df_ref[0] = df.astype(df_ref.dtype)
        lv = jnp.full((1, LANE), val, F32)

        @pl.when(i == 0)
        def _():
            dg2_ref[0] = dg2

        @pl.when(i > 0)
        def _():
            dg2_ref[0] += dg2

        first = jnp.logical_and(b == 0, i == 0)

        @pl.when(first)
        def _():
            dpg_ref[...] = dpg
            loss_ref[...] = lv

        @pl.when(jnp.logical_not(first))
        def _():
            dpg_ref[...] += dpg
            loss_ref[...] += lv

    row = pl.BlockSpec((1, tr, D), lambda b, i: (b, i, 0))
    pb = pl.BlockSpec((1, 1, D), lambda b, i: (b, 0, 0))
    gl = pl.BlockSpec((1, D), lambda b, i: (0, 0))
    return pl.pallas_call(
        body, name="loss_head", grid=(nb, nblk), in_specs=[row, row, row, pb, gl],
        out_specs=[row, row, pb, gl, pl.BlockSpec((1, LANE), lambda b, i: (0, 0))],
        out_shape=[jax.ShapeDtypeStruct((nb, S, D), F32), jax.ShapeDtypeStruct((nb, S, D), BF16),
                   jax.ShapeDtypeStruct((nb, 1, D), F32), jax.ShapeDtypeStruct((1, D), F32),
                   jax.ShapeDtypeStruct((1, LANE), F32)],
        compiler_params=_cparams("arbitrary", "arbitrary"),
    )(x1, ffn, target, gate2, post_g)


def _rotate_half(t):
    lane = lax.broadcasted_iota(jnp.int32, t.shape, 1)
    return jnp.where((lane & 15) < 8, -pltpu.roll(t, LANE - 8, 1), pltpu.roll(t, 8, 1))


def rope_call(name, x, width, colblk, cos, sin, out_dtype, tr):
    nb = x.shape[0]
    R = cos.shape[0]
    nblk = R // tr

    def body(x_ref, c_ref, s_ref, o_ref):
        c, s = c_ref[...], s_ref[...]
        for h in range(width // LANE):
            t = x_ref[0, :, h * LANE:(h + 1) * LANE].astype(F32)
            o_ref[0, :, h * LANE:(h + 1) * LANE] = (t * c + _rotate_half(t) * s).astype(o_ref.dtype)

    tab = pl.BlockSpec((tr, LANE), lambda b, i: (i, 0))
    return pl.pallas_call(
        body, name=name, grid=(nb, nblk),
        in_specs=[pl.BlockSpec((1, tr, width), lambda b, i: (b, i, colblk)), tab, tab],
        out_specs=pl.BlockSpec((1, tr, width), lambda b, i: (b, i, 0)),
        out_shape=jax.ShapeDtypeStruct((nb, R, width), out_dtype),
        compiler_params=_cparams("arbitrary", "arbitrary"),
    )(x, cos, sin)


def rope_tables(n_ctx, seq):
    n_rows = seq // GRID_W
    row = np.repeat(np.arange(n_rows), GRID_W).astype(np.float32)
    col = np.tile(np.arange(GRID_W), n_rows).astype(np.float32)
    axis_dim = ROPE // 2
    inv_freq = jnp.asarray(ROPE_THETA, F32) ** (-jnp.arange(0, axis_dim, 2, dtype=F32) / axis_dim)
    ang_r = jnp.asarray(row)[:, None] * inv_freq
    ang_c = jnp.asarray(col)[:, None] * inv_freq
    ang = jnp.concatenate([ang_r, ang_r, ang_c, ang_c], axis=-1)
    cos = jnp.ones((n_ctx + seq, LANE), F32).at[n_ctx:, KR_LANE:KR_LANE + ROPE].set(jnp.cos(ang))
    sin = jnp.zeros((n_ctx + seq, LANE), F32).at[n_ctx:, KR_LANE:KR_LANE + ROPE].set(jnp.sin(ang))
    return cos, sin


Q_PRESCALE = ATTN_SCALE * math.log2(math.e)


def _attn_weights(q, kc):
    s2 = lax.dot_general(q, kc, (((1,), (1,)), ((), ())), preferred_element_type=F32)
    e = jnp.exp2(s2 - jnp.max(s2, axis=1, keepdims=True))
    return e, 1.0 / jnp.sum(e, axis=1, keepdims=True)


def _key_block(kv, kr):
    lane = lax.broadcasted_iota(jnp.int32, kv.shape, 1)
    return jnp.where(lane < NOPE, kv, kr)


def _rotated_query(q_ref, cos_ref, sin_ref):
    t = q_ref[0].astype(F32)
    return (t * cos_ref[...] + _rotate_half(t) * sin_ref[...]).astype(BF16)


def attn_fwd(q_raw, kv, kr, cos_q, sin_q, tq):
    nb, S, _ = q_raw.shape
    T = kv.shape[1]

    def body(q_ref, kv_ref, kr_ref, c_ref, s_ref, o_ref):
        kvv = kv_ref[0]
        e, r = _attn_weights(_rotated_query(q_ref, c_ref, s_ref), _key_block(kvv, kr_ref[0]))
        o = lax.dot_general(e.astype(BF16), kvv, (((1,), (0,)), ((), ())), preferred_element_type=F32) * r
        lane = lax.broadcasted_iota(jnp.int32, o.shape, 1)
        o_ref[0] = jnp.where(lane >= NOPE, o, 0.0).astype(o_ref.dtype)

    return pl.pallas_call(
        body, name="attn_fwd", grid=(nb, N_HEADS, S // tq),
        in_specs=[pl.BlockSpec((1, tq, HEAD_BLOCK), lambda b, h, i: (b, i, h)),
                  pl.BlockSpec((1, T, HEAD_BLOCK), lambda b, h, i: (b, 0, h)),
                  pl.BlockSpec((1, T, HEAD_BLOCK), lambda b, h, i: (b, 0, 0)),
                  pl.BlockSpec((tq, LANE), lambda b, h, i: (i, 0)), pl.BlockSpec((tq, LANE), lambda b, h, i: (i, 0))],
        out_specs=pl.BlockSpec((1, tq, HEAD_BLOCK), lambda b, h, i: (b, i, h)),
        out_shape=jax.ShapeDtypeStruct((nb, S, QP), BF16),
        compiler_params=_cparams("arbitrary", "arbitrary", "arbitrary"),
    )(q_raw, kv, kr, cos_q, sin_q)


def attn_bwd(q_raw, kv, kr, do, cos_q, sin_q, cos, sin, tq):
    nb, S, _ = q_raw.shape
    T = kv.shape[1]

    def body(q_ref, kv_ref, kr_ref, do_ref, cq_ref, sq_ref, c_ref, s_ref, dq_ref, dkv_ref, dkr_ref):
        h, i = pl.program_id(1), pl.program_id(2)

        @pl.when(i == 0)
        def _():
            dkv_ref[...] = jnp.zeros_like(dkv_ref)

        @pl.when(jnp.logical_and(h == 0, i == 0))
        def _():
            dkr_ref[...] = jnp.zeros_like(dkr_ref)

        qv, kvv, dov = _rotated_query(q_ref, cq_ref, sq_ref), kv_ref[0], do_ref[0]
        kc = _key_block(kvv, kr_ref[0])
        e, r = _attn_weights(qv, kc)
        dor = (dov.astype(F32) * r).astype(BF16)
        dpr = lax.dot_general(dor, kvv, (((1,), (1,)), ((), ())), preferred_element_type=F32)
        ds = (e * (dpr - r * jnp.sum(dpr * e, axis=1, keepdims=True))).astype(BF16)
        dq = lax.dot_general(ds, kc, (((1,), (0,)), ((), ())), preferred_element_type=F32) * ATTN_SCALE
        dq_ref[0] = (dq * c_ref[...] - _rotate_half(dq) * s_ref[...]).astype(dq_ref.dtype)
        dkc = lax.dot_general(ds, qv, (((0,), (0,)), ((), ())), preferred_element_type=F32) * math.log(2.0)
        dv = lax.dot_general(e.astype(BF16), dor, (((0,), (0,)), ((), ())), preferred_element_type=F32)
        lane = lax.broadcasted_iota(jnp.int32, dkc.shape, 1)
        dkv_ref[0] += jnp.where(lane < NOPE, dkc, dv)
        dkr_ref[0] += jnp.where(lane >= NOPE, dkc, 0.0)

    qspec = pl.BlockSpec((1, tq, HEAD_BLOCK), lambda b, h, i: (b, i, h))
    kspec = pl.BlockSpec((1, T, HEAD_BLOCK), lambda b, h, i: (b, 0, h))
    rspec = pl.BlockSpec((1, T, HEAD_BLOCK), lambda b, h, i: (b, 0, 0))
    tab = pl.BlockSpec((tq, LANE), lambda b, h, i: (i, 0))
    return pl.pallas_call(
        body, name="attn_bwd", grid=(nb, N_HEADS, S // tq),
        in_specs=[qspec, kspec, rspec, qspec, tab, tab, tab, tab], out_specs=[qspec, kspec, rspec],
        out_shape=[jax.ShapeDtypeStruct((nb, S, QP), BF16), jax.ShapeDtypeStruct((nb, T, QP), F32),
                   jax.ShapeDtypeStruct((nb, T, HEAD_BLOCK), F32)],
        compiler_params=_cparams("arbitrary", "arbitrary", "arbitrary"),
    )(q_raw, kv, kr, do, cos_q, sin_q, cos, sin)


CONV_HALO = 8


def _segments(n, n_ctx):
    if n_ctx == 0:
        return [(0, n, CONV_HALO)]
    return [(0, n_ctx, CONV_HALO), (n_ctx, n - n_ctx, 2 * CONV_HALO + n_ctx)]


def _halo_scratch(n, n_ctx, tc):
    return pltpu.VMEM((n + CONV_HALO * (len(_segments(n, n_ctx)) + 1), tc), F32)


def _zero_halos(scr, segs):
    z = jnp.zeros((CONV_HALO, scr.shape[1]), scr.dtype)
    scr[0:CONV_HALO, :] = z
    for (_, rows, off) in segs:
        scr[off + rows:off + rows + CONV_HALO, :] = z


CONV_BLOCK_MAX = 256


def _conv_block(n, n_ctx):
    return _tile(math.gcd(n_ctx, n - n_ctx) if n_ctx else n, CONV_BLOCK_MAX, 8)


def _window(scr, off, r0, blk):
    return scr[pl.ds(pl.multiple_of(off - CONV_HALO + r0, 8), blk + 2 * CONV_HALO), :]


def _shifted(win, s):
    v = win if s == 0 else pltpu.roll(win, (-s) % win.shape[0], 0)
    return v[CONV_HALO:win.shape[0] - CONV_HALO]


def _tap_blocks(win, k, sign):
    return [_shifted(win, sign * (o - k // 2)) for o in range(k)]


def _taps(blocks, w):
    acc = None
    for o, blk in enumerate(blocks):
        t = w[o:o + 1, :] * blk
        acc = t if acc is None else acc + t
    return acc


def _tap_grads(xblocks, dpre):
    k = len(xblocks)
    sub8 = lax.broadcasted_iota(jnp.int32, (8, dpre.shape[1]), 0)
    out = jnp.where(sub8 == k, jnp.sum(dpre, axis=0, keepdims=True), 0.0)
    for o, blk in enumerate(xblocks):
        out = out + jnp.where(sub8 == o, jnp.sum(dpre * blk, axis=0, keepdims=True), 0.0)
    return out


def _row_blocks(rows, blk, fn, init=0):
    return lax.fori_loop(0, rows // blk, lambda i, c: fn(pl.multiple_of(i * blk, blk), c), init)


def _gelu(x):
    return 0.5 * x * (1.0 + lax.erf(x * (1.0 / math.sqrt(2.0))))


def _gelu_and_grad(x):
    cdf = 0.5 * (1.0 + lax.erf(x * (1.0 / math.sqrt(2.0))))
    return x * cdf, cdf + x * jnp.exp(-0.5 * x * x) * (1.0 / math.sqrt(2.0 * math.pi))


def ssd_conv_fwd(u, w8, bias, n_ctx, tc):
    nb, T, _ = u.shape
    cb0 = OFF_XBC // tc

    segs, blk = _segments(T, n_ctx), _conv_block(T, n_ctx)

    def body(x_ref, w_ref, b_ref, o_ref, xs):
        _zero_halos(xs, segs)
        for (start, rows, off) in segs:
            xs[off:off + rows, :] = x_ref[0, start:start + rows, :]
        w, bias_v = w_ref[...], b_ref[...]
        for (start, rows, off) in segs:
            def block(r0, carry, start=start, off=off):
                pre = bias_v + _taps(_tap_blocks(_window(xs, off, r0, blk), SSD_K, 1), w)
                o_ref[0, pl.ds(pl.multiple_of(start + r0, blk), blk), :] = _silu(pre)
                return carry

            _row_blocks(rows, blk, block)

    return pl.pallas_call(
        body, name="ssd_conv_fwd", grid=(nb, XBC // tc),
        in_specs=[pl.BlockSpec((1, T, tc), lambda b, j: (b, 0, cb0 + j)),
                  pl.BlockSpec((8, tc), lambda b, j: (0, j)), pl.BlockSpec((1, tc), lambda b, j: (0, j))],
        out_specs=pl.BlockSpec((1, T, tc), lambda b, j: (b, 0, j)),
        out_shape=jax.ShapeDtypeStruct((nb, T, XBC), F32),
        scratch_shapes=[_halo_scratch(T, n_ctx, tc)],
        compiler_params=_cparams("arbitrary", "arbitrary"),
    )(u, w8, bias)


def ssd_conv_bwd(u, w8, bias, dxbc, dxs_direct, n_ctx, tc):
    nb, T, _ = u.shape
    cb0 = OFF_XBC // tc
    n_direct = D_INNER // tc

    segs, blk = _segments(T, n_ctx), _conv_block(T, n_ctx)

    def body(x_ref, w_ref, b_ref, d0_ref, d1_ref, dd_ref, dx_ref, dw_ref, xs, ds):
        j, b = pl.program_id(0), pl.program_id(1)
        _zero_halos(xs, segs)
        _zero_halos(ds, segs)
        for (start, rows, off) in segs:
            xs[off:off + rows, :] = x_ref[0, start:start + rows, :]
        w, bias_v = w_ref[...], b_ref[...]
        has_direct = (j < n_direct).astype(F32)
        rows = jnp.zeros((8, tc), F32)
        for (start, n_rows, off) in segs:
            def block(r0, acc, start=start, off=off):
                xblocks = _tap_blocks(_window(xs, off, r0, blk), SSD_K, 1)
                pre = bias_v + _taps(xblocks, w)
                d = d0_ref[0, 0, pl.ds(pl.multiple_of(start + r0, blk), blk), :] + d1_ref[0, 0, pl.ds(pl.multiple_of(start + r0, blk), blk), :]
                if start == n_ctx:
                    d = d + dd_ref[0, pl.ds(r0, blk), :] * has_direct
                sg = jax.nn.sigmoid(pre)
                dpre = d * (sg * (1.0 + pre * (1.0 - sg)))
                ds[pl.ds(pl.multiple_of(off + r0, 8), blk), :] = dpre
                return acc + _tap_grads(xblocks, dpre)

            rows = _row_blocks(n_rows, blk, block, rows)
        for (start, n_rows, off) in segs:
            def block_dx(r0, carry, start=start, off=off):
                dx_ref[0, pl.ds(pl.multiple_of(start + r0, blk), blk), :] = _taps(_tap_blocks(_window(ds, off, r0, blk), SSD_K, -1), w).astype(dx_ref.dtype)
                return carry

            _row_blocks(n_rows, blk, block_dx)

        @pl.when(b == 0)
        def _():
            dw_ref[...] = rows

        @pl.when(b > 0)
        def _():
            dw_ref[...] += rows

    dspec0 = pl.BlockSpec((1, 1, T, tc), lambda j, b: (0, b, 0, j))
    dspec1 = pl.BlockSpec((1, 1, T, tc), lambda j, b: (1, b, 0, j))
    return pl.pallas_call(
        body, name="ssd_conv_bwd", grid=(XBC // tc, nb),
        in_specs=[pl.BlockSpec((1, T, tc), lambda j, b: (b, 0, cb0 + j)),
                  pl.BlockSpec((8, tc), lambda j, b: (0, j)), pl.BlockSpec((1, tc), lambda j, b: (0, j)),
                  dspec0, dspec1,
                  pl.BlockSpec((1, T - n_ctx, tc), lambda j, b: (b, 0, jnp.minimum(j, n_direct - 1)))],
        out_specs=[pl.BlockSpec((1, T, tc), lambda j, b: (b, 0, j)), pl.BlockSpec((8, tc), lambda j, b: (0, j))],
        out_shape=[jax.ShapeDtypeStruct((nb, T, XBC), BF16), jax.ShapeDtypeStruct((8, XBC), F32)],
        scratch_shapes=[_halo_scratch(T, n_ctx, tc), _halo_scratch(T, n_ctx, tc)],
        compiler_params=_cparams("arbitrary", "arbitrary"),
    )(u, w8, bias, dxbc, dxbc, dxs_direct)


GLU_TC = 256


def glu_interleave(w_up):
    blocks = []
    for j in range(D_FF // GLU_TC):
        blocks += [w_up[:, j * GLU_TC:(j + 1) * GLU_TC], w_up[:, D_FF + j * GLU_TC:D_FF + (j + 1) * GLU_TC]]
    return jnp.concatenate(blocks, axis=1)


def glu_deinterleave(g):
    nj = D_FF // GLU_TC
    gate = [g[:, 2 * j * GLU_TC:(2 * j + 1) * GLU_TC] for j in range(nj)]
    val = [g[:, (2 * j + 1) * GLU_TC:(2 * j + 2) * GLU_TC] for j in range(nj)]
    return jnp.concatenate(gate + val, axis=1)


def glu_fwd(up, w8, bias):
    nb, S, _ = up.shape
    tc = GLU_TC

    segs, blk = _segments(S, 0), _conv_block(S, 0)
    (_, _, off), = segs

    def body(u_ref, w_ref, b_ref, o_ref, xs):
        _zero_halos(xs, segs)
        xs[off:off + S, :] = u_ref[0, :, :tc]
        w, bias_v = w_ref[...], b_ref[...]

        def block(r0, carry):
            gc = bias_v + _taps(_tap_blocks(_window(xs, off, r0, blk), FFN_K, 1), w)
            o_ref[0, pl.ds(r0, blk), :] = (_gelu(gc) * u_ref[0, pl.ds(r0, blk), tc:]).astype(o_ref.dtype)
            return carry

        _row_blocks(S, blk, block)

    return pl.pallas_call(
        body, name="glu_fwd", grid=(nb, D_FF // tc),
        in_specs=[pl.BlockSpec((1, S, 2 * tc), lambda b, j: (b, 0, j)),
                  pl.BlockSpec((8, tc), lambda b, j: (0, j)), pl.BlockSpec((1, tc), lambda b, j: (0, j))],
        out_specs=pl.BlockSpec((1, S, tc), lambda b, j: (b, 0, j)),
        out_shape=jax.ShapeDtypeStruct((nb, S, D_FF), BF16),
        scratch_shapes=[_halo_scratch(S, 0, tc)],
        compiler_params=_cparams("arbitrary", "arbitrary"),
    )(up, w8, bias)


def glu_bwd(up, w8, bias, dact):
    nb, S, _ = up.shape
    tc = GLU_TC

    segs, blk = _segments(S, 0), _conv_block(S, 0)
    (_, _, off), = segs

    def body(u_ref, w_ref, b_ref, d_ref, du_ref, dw_ref, xs, ds):
        b = pl.program_id(1)
        _zero_halos(xs, segs)
        _zero_halos(ds, segs)
        xs[off:off + S, :] = u_ref[0, :, :tc]
        w, bias_v = w_ref[...], b_ref[...]

        def block(r0, acc):
            here = pl.ds(r0, blk)
            xblocks = _tap_blocks(_window(xs, off, r0, blk), FFN_K, 1)
            act, act_grad = _gelu_and_grad(bias_v + _taps(xblocks, w))
            d = d_ref[0, here, :].astype(F32)
            du_ref[0, here, tc:] = (d * act).astype(du_ref.dtype)
            dpre = d * u_ref[0, here, tc:] * act_grad
            ds[pl.ds(pl.multiple_of(off + r0, 8), blk), :] = dpre
            return acc + _tap_grads(xblocks, dpre)

        rows = _row_blocks(S, blk, block, jnp.zeros((8, tc), F32))

        def block_dx(r0, carry):
            du_ref[0, pl.ds(r0, blk), :tc] = _taps(_tap_blocks(_window(ds, off, r0, blk), FFN_K, -1), w).astype(du_ref.dtype)
            return carry

        _row_blocks(S, blk, block_dx)

        @pl.when(b == 0)
        def _():
            dw_ref[...] = rows

        @pl.when(b > 0)
        def _():
            dw_ref[...] += rows

    pair = pl.BlockSpec((1, S, 2 * tc), lambda j, b: (b, 0, j))
    return pl.pallas_call(
        body, name="glu_bwd", grid=(D_FF // tc, nb),
        in_specs=[pair, pl.BlockSpec((8, tc), lambda j, b: (0, j)), pl.BlockSpec((1, tc), lambda j, b: (0, j)),
                  pl.BlockSpec((1, S, tc), lambda j, b: (b, 0, j))],
        out_specs=[pair, pl.BlockSpec((8, tc), lambda j, b: (0, j))],
        out_shape=[jax.ShapeDtypeStruct((nb, S, 2 * D_FF), BF16), jax.ShapeDtypeStruct((8, D_FF), F32)],
        scratch_shapes=[_halo_scratch(S, 0, tc), _halo_scratch(S, 0, tc)],
        compiler_params=_cparams("arbitrary", "arbitrary"),
    )(up, w8, bias, dact)


def _chunk_of(d, k, n_cc, n_ch):
    rev = jnp.where(k < n_cc, n_cc - 1 - k, n_cc + n_ch - 1 - k)
    return jnp.where(d == 1, rev, k)


def _lane_pick(v, lane_iota, l):
    return jnp.sum(jnp.where(lane_iota == l, v, 0.0), axis=1, keepdims=True)


def head_spread_matrix():
    return (jnp.arange(LANE)[:, None] == (jnp.arange(D_INNER)[None, :] // SSD_P)).astype(BF16)


def _split_dot(x, e, dims):
    hi = x.astype(BF16)
    lo = (x - hi.astype(F32)).astype(BF16)
    return (lax.dot_general(hi, e, dims, preferred_element_type=F32)
            + lax.dot_general(lo, e, dims, preferred_element_type=F32))


def _spread(x, e):
    return _split_dot(x, e, (((1,), (0,)), ((), ())))


def _gather_heads(y, e):
    return _split_dot(y, e, (((1,), (1,)), ((), ())))


def _softplus(x):
    return jnp.maximum(x, 0.0) + jnp.log(1.0 + jnp.exp(-jnp.abs(x)))


def ssd_dt_inputs(u, a_log, dt_bias):
    pad = LANE - SSD_HEADS
    dt = u[..., OFF_DT:OFF_DT + 2 * SSD_HEADS]
    dt2 = jnp.stack([jnp.pad(dt[..., i * SSD_HEADS:(i + 1) * SSD_HEADS], ((0, 0), (0, 0), (0, pad))) for i in range(2)])

    def lanes(v):
        return jnp.pad(v.reshape(2, 1, SSD_HEADS), ((0, 0), (0, 0), (0, pad)))

    return dt2, lanes(a_log), lanes(dt_bias)


def _ssd_common(d, dt_raw, alog, dtb):
    Q = dt_raw.shape[0]
    row = lax.broadcasted_iota(jnp.int32, (Q, Q), 0)
    col = lax.broadcasted_iota(jnp.int32, (Q, Q), 1)
    rev = d == 1
    maskb = jnp.where(rev, row, col) <= jnp.where(rev, col, row)
    tri = maskb.astype(F32)
    A = -jnp.exp(alog)
    dtv = _softplus(dt_raw + dtb)
    a = dtv * A
    cum = lax.dot_general(tri, a, (((1,), (0,)), ((), ())), precision=lax.Precision.HIGHEST, preferred_element_type=F32)
    tot = jnp.sum(a, axis=0, keepdims=True)
    return maskb, tri, A, dtv, cum, tot


def ssd_fwd(xbc, dt2, alog2, dtb2, n_ctx, hosted):
    nb, T, _ = xbc.shape
    S = T - n_ctx
    n_ch, n_cc = T // CHUNK, n_ctx // CHUNK
    Q = CHUNK
    n_pairs = SSD_HEADS // 2
    n_ex = hosted.n
    n_in = 5

    def body(*refs):
        x_ref, dt_ref, al_ref, db_ref, e_ref = refs[:n_in]
        send_refs = refs[n_in:n_in + n_ex]
        y_ref, hin_ref = refs[n_in + n_ex:n_in + 2 + n_ex]
        recv_refs = refs[n_in + 2 + n_ex:n_in + 2 + 2 * n_ex]
        H, *sems = refs[n_in + 2 + 2 * n_ex:]
        d, k = pl.program_id(1), pl.program_id(2)
        first_step = jnp.logical_and(jnp.logical_and(pl.program_id(0) == 0, d == 0), k == 0)
        last_step = jnp.logical_and(jnp.logical_and(pl.program_id(0) == nb - 1, d == 1), k == n_ch - 1)
        begin_exchange, end_exchange = hosted.steps(send_refs, recv_refs, sems, first_step, last_step)
        begin_exchange()

        @pl.when(k == 0)
        def _():
            H[...] = jnp.zeros_like(H)

        maskb, tri, A, dtv, cum, tot = _ssd_common(d, dt_ref[0, 0], al_ref[0], db_ref[0])
        e = e_ref[...]
        cumT = cum.T
        cum_e, dt_e = _spread(cum, e), _spread(dtv, e)
        tot_e = _spread(jnp.broadcast_to(tot, (8, LANE)), e)[0:1]
        hin_ref[0, 0, 0] = H[...].astype(BF16)
        lane = lax.broadcasted_iota(jnp.int32, (Q, LANE), 1)
        lane1 = lax.broadcasted_iota(jnp.int32, (1, LANE), 1)
        subc = lax.broadcasted_iota(jnp.int32, (LANE, 1), 0)
        half = lane < SSD_P
        for g in range(SSD_GROUPS):
            Bg = x_ref[0, :, D_INNER + g * SSD_N:D_INNER + (g + 1) * SSD_N].astype(BF16)
            Cg = x_ref[0, :, D_INNER + GN + g * SSD_N:D_INNER + GN + (g + 1) * SSD_N].astype(BF16)
            Gm = lax.dot_general(Cg, Bg, (((1,), (1,)), ((), ())), preferred_element_type=F32)
            for pr in range(n_pairs // SSD_GROUPS):
                p = g * (n_pairs // SSD_GROUPS) + pr
                sc, dtp, totp = [t[:, p * LANE:(p + 1) * LANE] for t in (cum_e, dt_e, tot_e)]
                swapped = pltpu.roll(sc, SSD_P, 1)
                s0c, s1c = jnp.where(half, sc, swapped), jnp.where(half, swapped, sc)
                s0r, s1r = cumT[2 * p:2 * p + 1, :], cumT[2 * p + 1:2 * p + 2, :]
                tot0, tot1 = _lane_pick(tot, lane1, 2 * p), _lane_pick(tot, lane1, 2 * p + 1)
                M0 = (Gm * jnp.exp(jnp.where(maskb, s0c - s0r, NEG_BIG))).astype(BF16)
                M1 = (Gm * jnp.exp(jnp.where(maskb, s1c - s1r, NEG_BIG))).astype(BF16)
                xd = x_ref[0, :, p * LANE:(p + 1) * LANE] * dtp
                xdb = xd.astype(BF16)
                mx = lax.dot_general(jnp.concatenate([M0, M1], axis=0), xdb, (((1,), (0,)), ((), ())),
                                     preferred_element_type=F32)
                yd = jnp.where(half, mx[:Q], mx[Q:])
                Hp = H[p * LANE:(p + 1) * LANE, :]
                yo = lax.dot_general(Cg, Hp.astype(BF16), (((1,), (1,)), ((), ())), preferred_element_type=F32) * jnp.exp(sc)
                y_ref[0, 0, :, p * LANE:(p + 1) * LANE] = yd + yo
                xdw = (xd * jnp.exp(totp - sc)).astype(BF16)
                etot = jnp.exp(jnp.where(subc < SSD_P, tot0, tot1))
                H[p * LANE:(p + 1) * LANE, :] = Hp * etot + lax.dot_general(
                    xdw, Bg, (((0,), (0,)), ((), ())), preferred_element_type=F32)
        end_exchange()

    def ymap(b, d, k):
        return (d, b, _chunk_of(d, jnp.maximum(k, n_cc), n_cc, n_ch) - n_cc, 0)

    return pl.pallas_call(
        body, name="ssd_fwd", grid=(nb, 2, n_ch),
        in_specs=[pl.BlockSpec((1, Q, XBC), lambda b, d, k: (b, _chunk_of(d, k, n_cc, n_ch), 0)),
                  pl.BlockSpec((1, 1, Q, LANE), lambda b, d, k: (d, b, _chunk_of(d, k, n_cc, n_ch), 0)),
                  pl.BlockSpec((1, 1, LANE), lambda b, d, k: (d, 0, 0)), pl.BlockSpec((1, 1, LANE), lambda b, d, k: (d, 0, 0)),
                  pl.BlockSpec((LANE, D_INNER), lambda b, d, k: (0, 0))] + hosted.specs,
        out_specs=[pl.BlockSpec((1, 1, Q, D_INNER), ymap),
                   pl.BlockSpec((1, 1, 1, D_INNER, SSD_N), lambda b, d, k: (d, b, k, 0, 0))] + hosted.specs,
        out_shape=[jax.ShapeDtypeStruct((2, nb, S, D_INNER), F32),
                   jax.ShapeDtypeStruct((2, nb, n_ch, D_INNER, SSD_N), BF16)] + hosted.out_shape,
        scratch_shapes=[pltpu.VMEM((D_INNER, SSD_N), F32)] + hosted.scratch,
        compiler_params=_cparams("arbitrary", "arbitrary", "arbitrary"),
    )(xbc, dt2, alog2, dtb2, head_spread_matrix(), *hosted.arrays)


def ssd_bwd(xbc, dt2, alog2, dtb2, hin, dy, n_ctx, hosted):
    nb, T, _ = xbc.shape
    n_ex = hosted.n
    n_ch, n_cc = T // CHUNK, n_ctx // CHUNK
    n_in = 7
    Q = CHUNK
    n_pairs = SSD_HEADS // 2
    NT = (((1,), (1,)), ((), ()))
    NN = (((1,), (0,)), ((), ()))
    TN = (((0,), (0,)), ((), ()))

    def dot(a, b, dims):
        return lax.dot_general(a.astype(BF16), b.astype(BF16), dims, preferred_element_type=F32)

    def body(*refs):
        x_ref, dt_ref, al_ref, db_ref, e_ref, hin_ref, dy_ref = refs[:n_in]
        send_refs = refs[n_in:n_in + n_ex]
        dx_ref, ddt_ref, st_ref = refs[n_in + n_ex:n_in + 3 + n_ex]
        recv_refs = refs[n_in + 3 + n_ex:n_in + 3 + 2 * n_ex]
        dH, dce, dde, *sems = refs[n_in + 3 + 2 * n_ex:]
        d, kk = pl.program_id(1), pl.program_id(2)
        ks = n_ch - 1 - kk
        first_step = jnp.logical_and(jnp.logical_and(pl.program_id(0) == 0, d == 0), kk == 0)
        last_step = jnp.logical_and(jnp.logical_and(pl.program_id(0) == nb - 1, d == 1), kk == n_ch - 1)
        begin_exchange, end_exchange = hosted.steps(send_refs, recv_refs, sems, first_step, last_step)
        begin_exchange()

        @pl.when(kk == 0)
        def _():
            dH[...] = jnp.zeros_like(dH)

        @pl.when(jnp.logical_and(jnp.logical_and(pl.program_id(0) == 0, d == 0), kk == 0))
        def _():
            st_ref[...] = jnp.zeros_like(st_ref)

        dt_raw = dt_ref[0, 0]
        alog, dtb_v = al_ref[0], db_ref[0]
        maskb, tri, A, dtv, cum, tot = _ssd_common(d, dt_raw, alog, dtb_v)
        e = e_ref[...]
        cumT = cum.T
        cum_e, dt_e = _spread(cum, e), _spread(dtv, e)
        tot_e = _spread(jnp.broadcast_to(tot, (8, LANE)), e)[0:1]
        live = (ks >= n_cc).astype(F32)
        lane = lax.broadcasted_iota(jnp.int32, (Q, LANE), 1)
        lane1 = lax.broadcasted_iota(jnp.int32, (1, LANE), 1)
        sub = lax.broadcasted_iota(jnp.int32, (LANE, Q), 0)
        subc = lax.broadcasted_iota(jnp.int32, (LANE, 1), 0)
        half = lane < SSD_P
        halfc = subc < SSD_P
        pair_ones = ((lax.broadcasted_iota(jnp.int32, (2 * Q, LANE), 0) >= Q).astype(jnp.int32)
                     == (lax.broadcasted_iota(jnp.int32, (2 * Q, LANE), 1) >= SSD_P).astype(jnp.int32)).astype(BF16)
        dcumT = jnp.zeros((LANE, Q), F32)
        dtot = jnp.zeros((1, LANE), F32)
        dtot_parts = []
        for g in range(SSD_GROUPS):
            Bg = x_ref[0, :, D_INNER + g * SSD_N:D_INNER + (g + 1) * SSD_N].astype(BF16)
            Cg = x_ref[0, :, D_INNER + GN + g * SSD_N:D_INNER + GN + (g + 1) * SSD_N].astype(BF16)
            Gm = lax.dot_general(Cg, Bg, NT, preferred_element_type=F32)
            dG = jnp.zeros((Q, Q), F32)
            dC = jnp.zeros((Q, SSD_N), F32)
            dB = jnp.zeros((Q, SSD_N), F32)
            for pr in range(n_pairs // SSD_GROUPS):
                p = g * (n_pairs // SSD_GROUPS) + pr
                l0, l1 = 2 * p, 2 * p + 1
                sc, dtp, totp = [t[:, p * LANE:(p + 1) * LANE] for t in (cum_e, dt_e, tot_e)]
                swapped = pltpu.roll(sc, SSD_P, 1)
                s0c, s1c = jnp.where(half, sc, swapped), jnp.where(half, swapped, sc)
                s0r, s1r = cumT[l0:l0 + 1, :], cumT[l1:l1 + 1, :]
                tot0, tot1 = _lane_pick(tot, lane1, l0), _lane_pick(tot, lane1, l1)
                L0 = jnp.exp(jnp.where(maskb, s0c - s0r, NEG_BIG))
                L1 = jnp.exp(jnp.where(maskb, s1c - s1r, NEG_BIG))
                M0, M1 = Gm * L0, Gm * L1
                xs = x_ref[0, :, p * LANE:(p + 1) * LANE]
                xd = xs * dtp
                es = jnp.exp(sc)
                dte = jnp.exp(totp - sc)
                etot = jnp.exp(jnp.where(halfc, tot0, tot1))
                dyp = dy_ref[0, :, p * LANE:(p + 1) * LANE] * live
                Hp = hin_ref[0, 0, 0, p * LANE:(p + 1) * LANE, :]
                dHp = dH[p * LANE:(p + 1) * LANE, :]
                bdh = dot(Bg, dHp, NT)
                mtdy = dot(jnp.concatenate([M0, M1], axis=1), dyp, TN)
                dxd = jnp.where(half, mtdy[:Q], mtdy[Q:]) + bdh * dte
                dy0 = jnp.where(half, dyp, 0.0)
                dm = dot(jnp.concatenate([dy0, dyp - dy0], axis=0), xd, NT)
                dM0, dM1 = dm[:Q], dm[Q:]
                dG = dG + dM0 * L0 + dM1 * L1
                dyes = dyp * es
                xdw = xd * dte
                dC = dC + dot(dyes, Hp, NN)
                dB = dB + dot(xdw, dHp, NN)
                W0, W1 = dM0 * M0, dM1 * M1
                yoff = dot(Cg, Hp, NT) * es
                r_off = dyp * yoff
                r_st = xd * bdh * dte
                hh = jnp.sum(dHp * Hp.astype(F32), axis=1, keepdims=True) * etot
                w_rows = _split_dot(jnp.concatenate([W0, W1], axis=1), pair_ones, NN) * (1.0 / SSD_P)
                dce[:, p * LANE:(p + 1) * LANE] = r_off - r_st + w_rows
                dde[:, p * LANE:(p + 1) * LANE] = dxd * xs
                dtot_parts.append(jnp.sum(r_st, axis=0, keepdims=True))
                for (l, W, hselc) in ((l0, W0, halfc), (l1, W1, jnp.logical_not(halfc))):
                    row_g = -jnp.sum(W, axis=0, keepdims=True)
                    dcumT = dcumT + jnp.where(sub == l, row_g, 0.0)
                    dtot = dtot + jnp.where(lane1 == l, jnp.sum(jnp.where(hselc, hh, 0.0), axis=0, keepdims=True), 0.0)
                dx_ref[0, 0, :, p * LANE:(p + 1) * LANE] = dxd * dtp
                dH[p * LANE:(p + 1) * LANE, :] = dHp * etot + dot(dyes, Cg, TN)
            dx_ref[0, 0, :, D_INNER + g * SSD_N:D_INNER + (g + 1) * SSD_N] = dB + dot(dG, Cg, TN)
            dx_ref[0, 0, :, D_INNER + GN + g * SSD_N:D_INNER + GN + (g + 1) * SSD_N] = dC + dot(dG, Bg, NN)
        dcum_all = dcumT.T + _gather_heads(dce[...], e)
        dtot_e = jnp.broadcast_to(jnp.concatenate(dtot_parts, axis=1), (8, D_INNER))
        dtot = dtot + _gather_heads(dtot_e, e)[0:1]
        da = lax.dot_general(tri, dcum_all, TN, precision=lax.Precision.HIGHEST, preferred_element_type=F32) + dtot
        ddtv = _gather_heads(dde[...], e) + da * A
        ddt_raw = ddtv * jax.nn.sigmoid(dt_raw + dtb_v)
        ddt_ref[0, 0] = ddt_raw
        sub8 = lax.broadcasted_iota(jnp.int32, (8, LANE), 0)
        st_ref[...] += (jnp.where(sub8 == 2 * d, jnp.sum(da * dtv * A, axis=0, keepdims=True), 0.0)
                        + jnp.where(sub8 == 2 * d + 1, jnp.sum(ddt_raw, axis=0, keepdims=True), 0.0))
        end_exchange()

    def cmap(d, kk):
        return _chunk_of(d, n_ch - 1 - kk, n_cc, n_ch)

    def dymap(b, d, kk):
        return (b, _chunk_of(d, jnp.maximum(n_ch - 1 - kk, n_cc), n_cc, n_ch) - n_cc, 0)

    return pl.pallas_call(
        body, name="ssd_bwd", grid=(nb, 2, n_ch),
        in_specs=[pl.BlockSpec((1, Q, XBC), lambda b, d, kk: (b, cmap(d, kk), 0)),
                  pl.BlockSpec((1, 1, Q, LANE), lambda b, d, kk: (d, b, cmap(d, kk), 0)),
                  pl.BlockSpec((1, 1, LANE), lambda b, d, kk: (d, 0, 0)), pl.BlockSpec((1, 1, LANE), lambda b, d, kk: (d, 0, 0)),
                  pl.BlockSpec((LANE, D_INNER), lambda b, d, kk: (0, 0)),
                  pl.BlockSpec((1, 1, 1, D_INNER, SSD_N), lambda b, d, kk: (d, b, n_ch - 1 - kk, 0, 0)),
                  pl.BlockSpec((1, Q, D_INNER), dymap)] + hosted.specs,
        out_specs=[pl.BlockSpec((1, 1, Q, XBC), lambda b, d, kk: (d, b, cmap(d, kk), 0)),
                   pl.BlockSpec((1, 1, Q, LANE), lambda b, d, kk: (d, b, cmap(d, kk), 0)),
                   pl.BlockSpec((8, LANE), lambda b, d, kk: (0, 0))] + hosted.specs,
        out_shape=[jax.ShapeDtypeStruct((2, nb, T, XBC), F32), jax.ShapeDtypeStruct((2, nb, T, LANE), F32),
                   jax.ShapeDtypeStruct((8, LANE), F32)] + hosted.out_shape,
        scratch_shapes=[pltpu.VMEM((D_INNER, SSD_N), F32), pltpu.VMEM((Q, D_INNER), F32), pltpu.VMEM((Q, D_INNER), F32)] + hosted.scratch,
        compiler_params=_cparams("arbitrary", "arbitrary", "arbitrary"),
    )(xbc, dt2, alog2, dtb2, head_spread_matrix(), hin, dy, *hosted.arrays)


def _adamw(w, g, m, v):
    mn = ADAM_B1 * m + (1.0 - ADAM_B1) * g
    vn = ADAM_B2 * v + (1.0 - ADAM_B2) * jnp.square(g)
    m_hat = mn / (1.0 - ADAM_B1 ** ADAM_STEP)
    v_hat = vn / (1.0 - ADAM_B2 ** ADAM_STEP)
    return -ADAM_LR * (m_hat / (jnp.sqrt(v_hat) + ADAM_EPS) + ADAM_WD * w), mn, vn


def adamw_matrix(name, w, g_slots, m, v):
    K, n = w.shape
    s = g_slots.shape[0]
    tr = _tile(K, 256, 8)

    def body(w_ref, g_ref, m_ref, v_ref, go_ref, d_ref, mo_ref, vo_ref):
        g = g_ref[0].astype(F32)
        for j in range(1, s):
            g = g + g_ref[j].astype(F32)
        go_ref[...] = g
        d_ref[...], mo_ref[...], vo_ref[...] = _adamw(w_ref[...], g, m_ref[...], v_ref[...])

    spec = pl.BlockSpec((tr, n), lambda i: (i, 0))
    return pl.pallas_call(
        body, name=name, grid=(K // tr,),
        in_specs=[spec, pl.BlockSpec((s, tr, n), lambda i: (0, i, 0)), spec, spec], out_specs=[spec] * 4,
        out_shape=[jax.ShapeDtypeStruct((K, n), F32)] * 4,
        compiler_params=_cparams("arbitrary"),
    )(w, g_slots, m, v)


def adamw_small(ws, gs, ms, vs):
    n = len(ws)

    def body(*refs):
        for i in range(n):
            d, mn, vn = _adamw(refs[i][...], refs[n + i][...], refs[2 * n + i][...], refs[3 * n + i][...])
            refs[4 * n + i][...] = d
            refs[5 * n + i][...] = mn
            refs[6 * n + i][...] = vn

    shapes = [jax.ShapeDtypeStruct(w.shape, F32) for w in ws]
    out = pl.pallas_call(body, name="adamw_small", out_shape=shapes * 3)(*ws, *gs, *ms, *vs)
    return out[:n], out[n:2 * n], out[2 * n:]


def sum_slots(name, x):
    n = x.shape[0]

    def fn(t):
        acc = t[0]
        for j in range(1, n):
            acc = acc + t[j]
        return (acc,)

    return ew_call(name, fn, [x], [(x.shape[1:], F32)])[0]


def _pack_rows(parts):
    rows = []
    for p in parts:
        flat = p.reshape(1, -1)
        n = flat.shape[1]
        rows.append(jnp.pad(flat, ((0, 0), (0, -(-n // (8 * LANE)) * 8 * LANE - n))).reshape(-1, LANE))
    return jnp.concatenate(rows, axis=0)


def _unpack_rows(pack, shapes):
    out, r = [], 0
    for s in shapes:
        n = int(np.prod(s))
        nr = -(-n // (8 * LANE)) * 8
        out.append(pack[r:r + nr].reshape(1, -1)[:, :n].reshape(s))
        r += nr
    return out


def _mesh_pos():
    return lax.axis_index("x"), lax.axis_index("y"), lax.axis_index("c")


N_PEERS = N_DEV - 1


def all_gather(name, vs):
    n = len(vs)

    def body(*refs):
        _ag_start(refs[:n], refs[n:2 * n], *refs[2 * n:])
        _ag_finish(refs[:n], refs[n:2 * n], *refs[2 * n:])

    hbm = pl.BlockSpec(memory_space=pl.ANY)
    return pl.pallas_call(
        body, name=name, out_shape=_ag_out_shape(vs), in_specs=[hbm] * n, out_specs=[hbm] * n,
        scratch_shapes=_a2a_scratch(n),
    )(*vs)


def _ag_out_shape(vs):
    return [jax.ShapeDtypeStruct((N_DEV,) + v.shape, v.dtype) for v in vs]


def _ag_copies(x_refs, out_refs, send_sems, recv_sems, local_sems):
    n = len(x_refs)
    x, y, c = _mesh_pos()
    me, sibling = (x, y, c), (x, y, 1 - c)
    chips = [(1 - x, y), (x, 1 - y), (1 - x, 1 - y)]

    def slot(a, px, py, pc):
        return out_refs[a].at[4 * px + 2 * py + pc]

    def copy(a, k, block, to, src=None):
        return pltpu.make_async_remote_copy(
            src_ref=slot(a, *block) if src is None else src, dst_ref=slot(a, *block),
            send_sem=send_sems.at[N_PEERS * a + k], recv_sem=recv_sems.at[N_PEERS * a + k],
            device_id=to, device_id_type=MESH)

    local = [pltpu.make_async_copy(x_refs[a], slot(a, *me), local_sems.at[a]) for a in range(n)]
    first = []
    for a in range(n):
        first.append(copy(a, 0, me, sibling, src=x_refs[a]))
        first += [copy(a, 1 + j, me, (*chip, c), src=x_refs[a]) for j, chip in enumerate(chips)]
    passed = [(copy(a, 1 + j, (*chip, c), me), copy(a, 4 + j, (*chip, c), sibling))
              for j, chip in enumerate(chips) for a in range(n)]
    from_sibling = []
    for a in range(n):
        from_sibling.append(copy(a, 0, sibling, me))
        from_sibling += [copy(a, 4 + j, (*chip, 1 - c), me) for j, chip in enumerate(chips)]
    return local, first, passed, from_sibling


def _ag_start(*refs):
    local, first, _, _ = _ag_copies(*refs)
    for cp in local + first:
        cp.start()


def _ag_finish(*refs):
    local, first, passed, from_sibling = _ag_copies(*refs)
    for arrived, hand_on in passed:
        arrived.wait_recv()
        hand_on.start()
    for cp in from_sibling:
        cp.wait_recv()
    for cp in first + [hand_on for _, hand_on in passed]:
        cp.wait_send()
    for cp in local:
        cp.wait()


def _a2a_scratch(n):
    return [pltpu.SemaphoreType.DMA((N_PEERS * n,)), pltpu.SemaphoreType.DMA((N_PEERS * n,)), pltpu.SemaphoreType.DMA((n,))]


def _a2a_copies(x_refs, out_refs, send_sems, recv_sems, local_sems):
    n = len(x_refs)
    x, y, c = _mesh_pos()
    me = 4 * x + 2 * y + c
    local = [pltpu.make_async_copy(x_refs[a].at[me], out_refs[a].at[me], local_sems.at[a]) for a in range(n)]
    remote = []
    for k in range(1, N_DEV):
        px, py, pc = x ^ ((k >> 2) & 1), y ^ ((k >> 1) & 1), c ^ (k & 1)
        for a in range(n):
            remote.append(pltpu.make_async_remote_copy(
                src_ref=x_refs[a].at[4 * px + 2 * py + pc], dst_ref=out_refs[a].at[me],
                send_sem=send_sems.at[N_PEERS * a + k - 1], recv_sem=recv_sems.at[N_PEERS * a + k - 1],
                device_id=(px, py, pc), device_id_type=MESH))
    return local, remote


def _a2a_start(local, remote):
    for cp in local + remote:
        cp.start()


def _a2a_wait(local, remote):
    for cp in remote:
        cp.wait_recv()
    for cp in remote:
        cp.wait_send()
    for cp in local:
        cp.wait()


class Hosted:
    def __init__(self, start=None, finish=None, arrays=(), out_shape=()):
        self.start, self.finish, self.arrays, self.out_shape = start, finish, list(arrays), list(out_shape)
        self.n = len(self.arrays)
        self.specs = [pl.BlockSpec(memory_space=pl.ANY)] * self.n
        self.scratch = _a2a_scratch(self.n) if self.n else []

    def steps(self, send_refs, recv_refs, sems, first_step, last_step):
        def begin():
            if self.n:
                pl.when(first_step)(lambda: self.start(send_refs, recv_refs, *sems))

        def end():
            if self.n:
                pl.when(last_step)(lambda: self.finish(send_refs, recv_refs, *sems))

        return begin, end


def hosted_all_to_all(vs):
    return Hosted(lambda *r: _a2a_start(*_a2a_copies(*r)), lambda *r: _a2a_wait(*_a2a_copies(*r)), vs,
                  [jax.ShapeDtypeStruct(v.shape, v.dtype) for v in vs])


def hosted_all_gather(vs):
    return Hosted(_ag_start, _ag_finish, vs, _ag_out_shape(vs))


def _taps8(w):
    return jnp.concatenate([w, jnp.zeros((8 - w.shape[0], w.shape[1]), w.dtype)], axis=0)


FIRST = ("w_in",)
LATE_WEIGHTS = ("w_out", "w_up", "w_down", "w_q_up", "w_kv_up")


def first_weights_to_internal(w_in):
    cq, ckv, kr, z, xbc, dt = jnp.split(w_in, np.cumsum(IN_SPLITS)[:-1].tolist(), axis=1)
    K = w_in.shape[0]

    def zeros(n):
        return jnp.zeros((K, n), w_in.dtype)

    w_in_p = jnp.concatenate([cq, zeros(KR_LANE), kr, zeros(LANE - KR_LANE - ROPE), ckv, zeros(OFF_Z - OFF_CKV - KV_RANK),
                              z, xbc, dt, zeros(WIN_P - OFF_DT - 2 * SSD_HEADS)], axis=1)
    return dict(w_in_p=w_in_p)


def late_weights_to_internal(w_out, w_up, w_down, w_q_up, w_kv_up):
    attn_rows = w_out[:N_HEADS * V_DIM].reshape(N_HEADS, V_DIM, -1)
    w_out_p = jnp.concatenate([jnp.pad(attn_rows, ((0, 0), (HEAD_BLOCK - V_DIM, 0), (0, 0))).reshape(QP, -1),
                               w_out[N_HEADS * V_DIM:]], axis=0)
    w_q_p = jnp.pad(w_q_up.reshape(Q_RANK, N_HEADS, NOPE + ROPE), ((0, 0), (0, 0), (0, HEAD_BLOCK - NOPE - ROPE))).reshape(Q_RANK, QP)
    return dict(w_out_p=w_out_p, w_up=glu_interleave(w_up), w_down=w_down, w_q_p=w_q_p, w_kv=w_kv_up)


def _q_grad(g_q_p):
    return g_q_p.reshape(Q_RANK, N_HEADS, HEAD_BLOCK)[:, :, :NOPE + ROPE].reshape(Q_RANK, -1)


def _out_grad(g_out_p):
    return jnp.concatenate([g_out_p[:QP].reshape(N_HEADS, HEAD_BLOCK, -1)[:, HEAD_BLOCK - V_DIM:].reshape(N_HEADS * V_DIM, -1),
                            g_out_p[QP:]], axis=0)


EARLY = ("w_out", "w_up", "w_down", "w_q_up", "w_kv_up")


def local_step(x, ctx, target, mod_x, mod_c, W, late_shards, V):
    nb, S, D = x.shape
    C = ctx.shape[1]
    T = C + S
    tr = _tile(math.gcd(C, S), 256, 8)
    tq = _tile(S, 256, 8)
    tc = 256
    cblk = C // tr
    m = [mod_x[:, i * D:(i + 1) * D][:, None, :] for i in range(N_MOD)]
    mc = [mod_c[:, i * D:(i + 1) * D] for i in range(2)]
    ssd_w8, ffn_w8 = _taps8(V["ssd_conv_w"]), _taps8(V["ffn_conv_w"])
    dexp = jnp.repeat(V["ssd_d"].reshape(-1), SSD_P).reshape(1, D_INNER)
    cosT, sinT = rope_tables(C, S)
    cosS, sinS = cosT[C:], sinT[C:]

    (h1x,) = rows_fwd("prenorm_x", fn_prenorm, nb, S // tr, tr, [(x, D, 0, 0)], [m[0], m[1]], [V["mix_pre_norm"]], [(D, BF16)])
    (h1c,) = rows_fwd("prenorm_c", fn_prenorm, nb, C // tr, tr, [(ctx, D, 0, 0)], [], [mc[0], mc[1], V["mix_pre_norm"]], [(D, BF16)])
    h1 = jnp.concatenate([h1c, h1x], axis=1).reshape(nb * T, D)
    u = matmul("in_proj", [(h1, W["w_in_p"])], "nn", F32).reshape(nb, T, WIN_P)
    xbc = ssd_conv_fwd(u, ssd_w8, V["ssd_conv_b"], C, tc)
    dt2, alog2, dtb2 = ssd_dt_inputs(u, V["ssd_a_log"], V["ssd_dt_bias"])
    y2, hin, *late = ssd_fwd(xbc, dt2, alog2, dtb2, C, hosted_all_gather(late_shards))
    W = dict(W, **late_weights_to_internal(*[_whole(s, n) for s, n in zip(late, LATE_WEIGHTS)]))
    y2 = y2.reshape(2 * nb, S, D_INNER)
    (qn,) = rows_fwd("q_norm", fn_rms, nb, S // tr, tr, [(u, Q_RANK, OFF_CQ // Q_RANK, cblk)], [], [V["q_norm"]], [(Q_RANK, BF16)])
    (kvn,) = rows_fwd("kv_norm", fn_rms, nb, T // tr, tr, [(u, KV_RANK, OFF_CKV // KV_RANK, 0)], [], [V["kv_norm"]], [(KV_RANK, BF16)])
    qn2, kvn2 = qn.reshape(nb * S, Q_RANK), kvn.reshape(nb * T, KV_RANK)
    q_raw = matmul("q_up", [(qn2, W["w_q_p"])], "nn", F32).reshape(nb, S, QP)
    kv = matmul("kv_up", [(kvn2, W["w_kv"])], "nn", BF16).reshape(nb, T, QP)
    cos_q, sin_q = cosS * Q_PRESCALE, sinS * Q_PRESCALE
    kr = rope_call("rope_k", u, LANE, OFF_KR // LANE, cosT, sinT, BF16, tr)
    o = attn_fwd(q_raw, kv, kr, cos_q, sin_q, tq)
    fin_rows = [(y2, D_INNER, 0, 0, 0), (y2, D_INNER, 0, 0, nb), (xbc, D_INNER, 0, cblk), (u, D_INNER, OFF_Z // D_INNER, cblk)]
    fin_gl = [dexp, V["ssd_norm"]]
    (ssd,) = rows_fwd("ssd_finish", fn_ssd_finish, nb, S // tr, tr, fin_rows, [], fin_gl, [(D_INNER, BF16)])
    o2, ssd2 = o.reshape(nb * S, QP), ssd.reshape(nb * S, D_INNER)
    mix = matmul("out_proj", [(o2, W["w_out_p"][:QP]), (ssd2, W["w_out_p"][QP:])], "nn", F32).reshape(nb, S, D)
    pm_rows = [(x, D, 0, 0), (mix, D, 0, 0)]
    pm_pb = [m[2], m[4], m[3]]
    pm_gl = [V["mix_post_norm"], V["ffn_pre_norm"]]
    x1, h2 = rows_fwd("postmix", fn_postmix, nb, S // tr, tr, pm_rows, pm_pb, pm_gl, [(D, F32), (D, BF16)])
    h22 = h2.reshape(nb * S, D)
    up = matmul("up_proj", [(h22, W["w_up"])], "nn", F32).reshape(nb, S, 2 * D_FF)
    act = glu_fwd(up, ffn_w8, V["ffn_conv_b"])
    act2 = act.reshape(nb * S, D_FF)
    ffn = matmul("down_proj", [(act2, W["w_down"])], "nn", F32).reshape(nb, S, D)
    dx1, dffn, dgate2, d_ffn_post, loss = final_call(x1, ffn, target, m[5], V["ffn_post_norm"], tr)

    dffn2 = dffn.reshape(nb * S, D)
    dact = matmul("down_dgrad", [(dffn2, W["w_down"])], "nt", BF16).reshape(nb, S, D_FF)
    g_down = matmul_tn("down_wgrad", act2, dffn2)
    dup, ffn_rows = glu_bwd(up, ffn_w8, V["ffn_conv_b"], dact)
    dup2 = dup.reshape(nb * S, 2 * D_FF)
    dh2 = matmul("up_dgrad", [(dup2, W["w_up"])], "nt", BF16).reshape(nb, S, D)
    g_up = matmul_tn("up_wgrad", h22, dup2)
    dx_a, dmix, dgate1, dscale2, dshift2, d_mix_post, d_ffn_pre = rows_bwd(
        "postmix_bwd", fn_postmix, nb, S // tr, tr, pm_rows, pm_pb, pm_gl,
        [(dx1, D, 0, 0), (dh2, D, 0, 0)], [(0, F32), (1, BF16)])
    dmix2 = dmix.reshape(nb * S, D)
    dcat = matmul("out_dgrad", [(dmix2, W["w_out_p"])], "nt", BF16).reshape(nb, S, QP + D_INNER)
    g_out_p = jnp.concatenate([matmul_tn("out_wgrad_attn", o2, dmix2), matmul_tn("out_wgrad_ssd", ssd2, dmix2)], axis=0)
    dy, dxs_direct, dz, d_dexp, d_ssd_norm = rows_bwd(
        "ssd_finish_bwd", fn_ssd_finish, nb, S // tr, tr, fin_rows, [], fin_gl,
        [(dcat, D_INNER, QP // D_INNER, 0)], [(0, F32), (2, F32), (3, BF16)])
    dq_pre, dkv, dkr = attn_bwd(q_raw, kv, kr, dcat, cos_q, sin_q, cosS, sinS, tq)
    dq_pre = dq_pre.reshape(nb * S, QP)
    dkr_pre = rope_call("rope_dk", dkr, LANE, 0, cosT, -sinT, BF16, tr)
    dkv2 = dkv.reshape(nb * T, QP)
    dqn = matmul("q_dgrad", [(dq_pre, W["w_q_p"])], "nt", F32).reshape(nb, S, Q_RANK)
    g_q_p = matmul_tn("q_wgrad", qn2, dq_pre)
    dkvn = matmul("kv_dgrad", [(dkv2, W["w_kv"])], "nt", F32).reshape(nb, T, KV_RANK)
    g_kv = matmul_tn("kv_wgrad", kvn2, dkv2)
    early_grads = (_out_grad(g_out_p), glu_deinterleave(g_up), g_down, _q_grad(g_q_p), g_kv)
    early = hosted_all_to_all([_per_device(g, n) for g, n in zip(early_grads, EARLY)])
    dxbc2, ddt2, ssd_stats, *received = ssd_bwd(xbc, dt2, alog2, dtb2, hin, dy, C, early)
    ddt_block = jnp.concatenate([ddt2[0][..., :SSD_HEADS], ddt2[1][..., :SSD_HEADS],
                                 jnp.zeros((nb, T, LANE - 2 * SSD_HEADS), F32)], axis=-1).astype(BF16)
    dxbc_raw, ssd_rows = ssd_conv_bwd(u, ssd_w8, V["ssd_conv_b"], dxbc2, dxs_direct, C, tc)
    dcq, d_q_norm = rows_bwd("q_norm_bwd", fn_rms, nb, S // tr, tr, [(u, Q_RANK, OFF_CQ // Q_RANK, cblk)], [], [V["q_norm"]],
                             [(dqn, Q_RANK, 0, 0)], [(0, BF16)])
    dckv, d_kv_norm = rows_bwd("kv_norm_bwd", fn_rms, nb, T // tr, tr, [(u, KV_RANK, OFF_CKV // KV_RANK, 0)], [], [V["kv_norm"]],
                               [(dkvn, KV_RANK, 0, 0)], [(0, BF16)])

    def ctx_rows(t):
        return jnp.pad(t, ((0, 0), (C, 0), (0, 0)))

    du = [("cq", ctx_rows(dcq), OFF_CQ, Q_RANK), ("kr", dkr_pre, OFF_KR, LANE), ("ckv", dckv, OFF_CKV, KV_RANK),
          ("z", ctx_rows(dz), OFF_Z, D_INNER), ("xbc", dxbc_raw, OFF_XBC, XBC), ("dt", ddt_block, OFF_DT, LANE)]
    du = [(name, t.reshape(nb * T, w), off, w) for (name, t, off, w) in du]
    g = {name: matmul_tn("in_wgrad_" + name, h1, t) for (name, t, _, _) in du}
    g_in = jnp.concatenate([g["cq"], g["ckv"], g["kr"][:, KR_LANE:KR_LANE + ROPE], g["z"], g["xbc"],
                            g["dt"][:, :2 * SSD_HEADS]], axis=1)
    dh1, received_in = matmul("in_dgrad", [(t, W["w_in_p"][:, off:off + w]) for (_, t, off, w) in du], "nt", BF16,
                              hosted=hosted_all_to_all([_per_device(g_in, "w_in").astype(BF16)]))
    dh1 = dh1.reshape(nb, T, D)

    def fn_prenorm_res(xv, shift, scale, g):
        return fn_prenorm(xv, shift, scale, g) + (xv,)

    grad_x, dshift1, dscale1, d_mix_pre_x = rows_bwd(
        "prenorm_x_bwd", fn_prenorm_res, nb, S // tr, tr, [(x, D, 0, 0)], [m[0], m[1]], [V["mix_pre_norm"]],
        [(dh1, D, 0, cblk), (dx_a, D, 0, 0)], [(0, F32)])
    dshift_c, dscale_c, d_mix_pre_c = rows_bwd(
        "prenorm_c_bwd", fn_prenorm, nb, C // tr, tr, [(ctx, D, 0, 0)], [], [mc[0], mc[1], V["mix_pre_norm"]],
        [(dh1, D, 0, 0)], [])

    dmod_x = jnp.concatenate([dshift1, dscale1, dgate1, dshift2, dscale2, dgate2], axis=-1).reshape(nb, N_MOD * D)
    dmod_c = jnp.concatenate([dshift_c, dscale_c, jnp.zeros((1, (N_MOD - 2) * D), F32)], axis=-1)
    gv = dict(
        mix_pre_norm=d_mix_pre_x + d_mix_pre_c, mix_post_norm=d_mix_post, q_norm=d_q_norm, kv_norm=d_kv_norm,
        ssd_conv_w=ssd_rows[:SSD_K], ssd_conv_b=ssd_rows[SSD_K:SSD_K + 1],
        ssd_a_log=jnp.concatenate([ssd_stats[0:1, :SSD_HEADS], ssd_stats[2:3, :SSD_HEADS]], axis=1),
        ssd_dt_bias=jnp.concatenate([ssd_stats[1:2, :SSD_HEADS], ssd_stats[3:4, :SSD_HEADS]], axis=1),
        ssd_d=jnp.sum(d_dexp.reshape(SSD_HEADS, SSD_P), axis=1).reshape(1, SSD_HEADS), ssd_norm=d_ssd_norm,
        ffn_pre_norm=d_ffn_pre, ffn_post_norm=d_ffn_post,
        ffn_conv_w=ffn_rows[:FFN_K], ffn_conv_b=ffn_rows[FFN_K:FFN_K + 1])
    return loss, grad_x, dmod_x, dmod_c, gv, dict(zip(EARLY, received), w_in=received_in)


WEIGHT_ORDER = ("c_ctx", "w_mod", "b_mod", "mix_pre_norm", "mix_post_norm", "w_in", "q_norm", "w_q_up", "kv_norm",
                "w_kv_up", "ssd_conv_w", "ssd_conv_b", "ssd_a_log", "ssd_dt_bias", "ssd_d", "ssd_norm", "w_out",
                "ffn_pre_norm", "ffn_post_norm", "w_up", "ffn_conv_w", "ffn_conv_b", "w_down")
MATRICES = ("w_in", "w_q_up", "w_kv_up", "w_out", "w_up", "w_down")
ROW_SHARDED = ("w_out", "w_down")
SMALL_SUMMED = ("c_ctx", "mix_pre_norm", "mix_post_norm", "q_norm", "kv_norm", "ssd_conv_w", "ssd_conv_b", "ssd_a_log",
                "ssd_dt_bias", "ssd_d", "ssd_norm", "ffn_pre_norm", "ffn_post_norm", "ffn_conv_w", "ffn_conv_b")
MOD_ROWS = 8


def _whole(shards, name):
    if name in ROW_SHARDED:
        return shards.reshape(-1, shards.shape[-1])
    return jnp.concatenate([shards[j] for j in range(N_DEV)], axis=1)


def _per_device(g, name):
    if name in ROW_SHARDED:
        return g.reshape(N_DEV, -1, g.shape[-1])
    return jnp.stack(jnp.split(g, N_DEV, axis=1))


def kernel(x, c, ctx, c_ctx, w_mod, b_mod, mix_pre_norm, mix_post_norm, w_in, q_norm, w_q_up, kv_norm, w_kv_up, ssd_conv_w, ssd_conv_b, ssd_a_log, ssd_dt_bias, ssd_d, ssd_norm, w_out, ffn_pre_norm, ffn_post_norm, w_up, ffn_conv_w, ffn_conv_b, w_down, loss_target, m_c_ctx, m_w_mod, m_b_mod, m_mix_pre_norm, m_mix_post_norm, m_w_in, m_q_norm, m_w_q_up, m_kv_norm, m_w_kv_up, m_ssd_conv_w, m_ssd_conv_b, m_ssd_a_log, m_ssd_dt_bias, m_ssd_d, m_ssd_norm, m_w_out, m_ffn_pre_norm, m_ffn_post_norm, m_w_up, m_ffn_conv_w, m_ffn_conv_b, m_w_down, v_c_ctx, v_w_mod, v_b_mod, v_mix_pre_norm, v_mix_post_norm, v_w_in, v_q_norm, v_w_q_up, v_kv_norm, v_w_kv_up, v_ssd_conv_w, v_ssd_conv_b, v_ssd_a_log, v_ssd_dt_bias, v_ssd_d, v_ssd_norm, v_w_out, v_ffn_pre_norm, v_ffn_post_norm, v_w_up, v_ffn_conv_w, v_ffn_conv_b, v_w_down):
    weights = dict(c_ctx=c_ctx, w_mod=w_mod, b_mod=b_mod, mix_pre_norm=mix_pre_norm, mix_post_norm=mix_post_norm, w_in=w_in, q_norm=q_norm, w_q_up=w_q_up, kv_norm=kv_norm, w_kv_up=w_kv_up, ssd_conv_w=ssd_conv_w, ssd_conv_b=ssd_conv_b, ssd_a_log=ssd_a_log, ssd_dt_bias=ssd_dt_bias, ssd_d=ssd_d, ssd_norm=ssd_norm, w_out=w_out, ffn_pre_norm=ffn_pre_norm, ffn_post_norm=ffn_post_norm, w_up=w_up, ffn_conv_w=ffn_conv_w, ffn_conv_b=ffn_conv_b, w_down=w_down)
    mom1 = dict(c_ctx=m_c_ctx, w_mod=m_w_mod, b_mod=m_b_mod, mix_pre_norm=m_mix_pre_norm, mix_post_norm=m_mix_post_norm, w_in=m_w_in, q_norm=m_q_norm, w_q_up=m_w_q_up, kv_norm=m_kv_norm, w_kv_up=m_w_kv_up, ssd_conv_w=m_ssd_conv_w, ssd_conv_b=m_ssd_conv_b, ssd_a_log=m_ssd_a_log, ssd_dt_bias=m_ssd_dt_bias, ssd_d=m_ssd_d, ssd_norm=m_ssd_norm, w_out=m_w_out, ffn_pre_norm=m_ffn_pre_norm, ffn_post_norm=m_ffn_post_norm, w_up=m_w_up, ffn_conv_w=m_ffn_conv_w, ffn_conv_b=m_ffn_conv_b, w_down=m_w_down)
    mom2 = dict(c_ctx=v_c_ctx, w_mod=v_w_mod, b_mod=v_b_mod, mix_pre_norm=v_mix_pre_norm, mix_post_norm=v_mix_post_norm, w_in=v_w_in, q_norm=v_q_norm, w_q_up=v_w_q_up, kv_norm=v_kv_norm, w_kv_up=v_w_kv_up, ssd_conv_w=v_ssd_conv_w, ssd_conv_b=v_ssd_conv_b, ssd_a_log=v_ssd_a_log, ssd_dt_bias=v_ssd_dt_bias, ssd_d=v_ssd_d, ssd_norm=v_ssd_norm, w_out=v_w_out, ffn_pre_norm=v_ffn_pre_norm, ffn_post_norm=v_ffn_post_norm, w_up=v_w_up, ffn_conv_w=v_ffn_conv_w, ffn_conv_b=v_ffn_conv_b, w_down=v_w_down)
    nb, S, D = x.shape
    me = 4 * lax.axis_index("x") + 2 * lax.axis_index("y") + lax.axis_index("c")

    *first, c_all, ssd_w_sh, ffn_w_sh = all_gather(
        "gather_first", [weights[n][0].astype(BF16) for n in FIRST] + [c, ssd_conv_w[0], ffn_conv_w[0]])
    W = first_weights_to_internal(*[_whole(s, n) for n, s in zip(FIRST, first)])
    late_shards = [weights[n][0].astype(BF16) for n in LATE_WEIGHTS]
    V = {n: weights[n].reshape(1, -1) for n in SMALL_SUMMED if n != "c_ctx"}
    V["ssd_conv_w"] = _whole(ssd_w_sh, "ssd_conv_w")
    V["ffn_conv_w"] = _whole(ffn_w_sh, "ffn_conv_w")

    n_all = N_DEV * nb
    mod_rows = -(-(n_all + 1) // 8) * 8
    c_pad = jnp.concatenate([c_all.reshape(n_all, D), c_ctx.reshape(1, D), jnp.zeros((mod_rows - n_all - 1, D), F32)], axis=0)
    mod_cols = w_mod.shape[2]
    b_mine = lax.dynamic_slice(b_mod, (0, me * mod_cols), (1, mod_cols))
    mod_part = matmul("mod_proj", [(c_pad, w_mod[0])], "nn", F32, bias=b_mine, silu_a=True)
    mod_all = _whole(all_gather("gather_mod", [mod_part])[0], "w_mod")
    mod_x = lax.dynamic_slice(mod_all, (me * nb, 0), (nb, mod_all.shape[1]))
    mod_c = mod_all[n_all:n_all + 1]

    loss, grad_x, dmod_x, dmod_c, gv, slots = local_step(x, ctx, loss_target, mod_x, mod_c, W, late_shards, V)

    dmod_mine = jnp.concatenate([dmod_x, dmod_c, jnp.zeros((MOD_ROWS - nb - 1, dmod_x.shape[1]), F32)], axis=0)
    dmod_all = all_gather("gather_dmod", [dmod_mine])[0]
    dmod_ctx = sum_slots("sum_dmod_ctx", dmod_all[:, nb:nb + 1].reshape(N_DEV, -1, LANE)).reshape(1, -1)
    dmod_full = jnp.concatenate([dmod_all[:, :nb].reshape(n_all, -1), dmod_ctx,
                                 jnp.zeros((mod_rows - n_all - 1, dmod_ctx.shape[1]), F32)], axis=0)
    (g_b_mod,) = ew_call("mod_bias_grad", lambda t: (jnp.sum(t, axis=0, keepdims=True),), [dmod_full], [((1, dmod_full.shape[1]), F32)])
    dmod_cols = lax.dynamic_slice(dmod_full, (0, me * mod_cols), (mod_rows, mod_cols))
    g_w_mod = matmul_tn("mod_wgrad", c_pad, dmod_cols, silu_a=True)
    dsilu_ctx = matmul("mod_dgrad_ctx", [(dmod_cols[n_all:n_all + 8], w_mod[0])], "nt", F32)[0:1]

    def silu_vjp(cc, ct):
        return (jax.vjp(_silu, cc)[1](ct)[0],)

    (g_c_ctx_part,) = ew_call("c_ctx_grad", silu_vjp, [c_ctx.reshape(1, D), dsilu_ctx], [((1, D), F32)])

    gv = dict(gv, c_ctx=g_c_ctx_part)
    small_parts = [loss] + [gv[n] for n in SMALL_SUMMED]
    small_sum = sum_slots("sum_small", all_gather("gather_small_grads", [_pack_rows(small_parts)])[0])
    summed = _unpack_rows(small_sum, [p.shape for p in small_parts])
    loss_out = summed[0][0, 0]
    grads = {n: g.reshape(weights[n].shape) if n not in ("ssd_conv_w", "ffn_conv_w") else g for n, g in zip(SMALL_SUMMED, summed[1:])}
    for n in ("ssd_conv_w", "ffn_conv_w"):
        cols = weights[n].shape[2]
        grads[n] = lax.dynamic_slice(grads[n], (0, me * cols), (grads[n].shape[0], cols)).reshape(weights[n].shape)
    grads["b_mod"] = g_b_mod.reshape(b_mod.shape)

    slots = dict(slots, w_mod=g_w_mod[None])
    delta, new_m, new_v = {}, {}, {}
    for n in MATRICES + ("w_mod",):
        g, d, mn, vn = adamw_matrix("adamw_" + n, weights[n][0], slots[n], mom1[n][0], mom2[n][0])
        grads[n], delta[n], new_m[n], new_v[n] = [t.reshape(weights[n].shape) for t in (g, d, mn, vn)]
    small = [n for n in WEIGHT_ORDER if n not in slots]

    def two_d(t):
        return t.reshape(-1, t.shape[-1])

    ds, ms, vs = adamw_small(*[[two_d(t[n]) for n in small] for t in (weights, grads, mom1, mom2)])
    for n, d, mn, vn in zip(small, ds, ms, vs):
        delta[n], new_m[n], new_v[n] = [t.reshape(weights[n].shape) for t in (d, mn, vn)]
    return (loss_out, grad_x, *[t[n] for t in (grads, delta, new_m, new_v) for n in WEIGHT_ORDER])
```

```python
import math

import jax
import jax.numpy as jnp
import numpy as np
from jax import lax
from jax.experimental import pallas as pl
from jax.experimental.pallas import tpu as pltpu

F32 = jnp.float32
BF16 = jnp.bfloat16
MESH = pl.DeviceIdType.MESH

D_MODEL = 1024
GRID_W = 64
N_HEADS = 16
NOPE = 64
ROPE = 32
V_DIM = 64
Q_RANK = 384
KV_RANK = 256
ROPE_THETA = 10000.0
ATTN_SCALE = (NOPE + ROPE) ** -0.5
SSD_HEADS = 16
SSD_P = 64
SSD_GROUPS = 2
SSD_N = 128
SSD_K = 5
CHUNK = 128
D_INNER = SSD_HEADS * SSD_P
GN = SSD_GROUPS * SSD_N
XBC = D_INNER + 2 * GN
D_FF = 2816
FFN_K = 3
N_MOD = 6
EPS = 1e-6
IN_SPLITS = (Q_RANK, KV_RANK, ROPE, D_INNER, XBC, 2 * SSD_HEADS)
IN_WIDTH = sum(IN_SPLITS)
N_DEV = 8

ADAM_LR = 0.001
ADAM_B1 = 0.9
ADAM_B2 = 0.999
ADAM_EPS = 1e-08
ADAM_WD = 0.01
ADAM_STEP = 10

LANE = 128
HEAD_BLOCK = 128
OFF_CQ = 0
OFF_KR = 384
OFF_CKV = 512
OFF_Z = 1024
OFF_XBC = 2048
OFF_DT = 3584
WIN_P = 3840
KR_LANE = 64
QP = N_HEADS * HEAD_BLOCK

VMEM_LIMIT_V7X = 56 * 1024 * 1024
NEG_BIG = -1e30


def _cparams(*sem):
    return pltpu.CompilerParams(dimension_semantics=sem, vmem_limit_bytes=VMEM_LIMIT_V7X)


def _tile(n, target, mult=128):
    if n <= target:
        return n
    t = (target // mult) * mult
    while t >= mult:
        if n % t == 0:
            return t
        t -= mult
    return n


def _silu(x):
    return x * jax.nn.sigmoid(x)


def _rms(x, g):
    return x * lax.rsqrt(jnp.mean(x * x, axis=-1, keepdims=True) + EPS) * g


WHOLE_K_WIDE = 2048


def matmul(name, pairs, mode, out_dtype, *, bias=None, silu_a=False, hosted=None):
    n_pairs = len(pairs)
    M = pairs[0][0].shape[0]
    N = pairs[0][1].shape[1] if mode == "nn" else pairs[0][1].shape[0]
    k_total = sum(a.shape[1] for a, _ in pairs)
    tm = _tile(M, 1024 if k_total <= WHOLE_K_WIDE else 512, 8)
    tn = _tile(N, 1408 if k_total <= WHOLE_K_WIDE else 512)
    dims = (((1,), (0,)), ((), ())) if mode == "nn" else (((1,), (1,)), ((), ()))
    n_own = 2 * n_pairs + (bias is not None)
    n_ex = hosted.n if hosted else 0

    def body(*refs):
        o_ref = refs[n_own + n_ex]
        if hosted:
            j, i = pl.program_id(0), pl.program_id(1)
            begin_exchange, end_exchange = hosted.steps(
                refs[n_own:n_own + n_ex], refs[n_own + n_ex + 1:n_own + 2 * n_ex + 1], refs[n_own + 2 * n_ex + 1:],
                jnp.logical_and(j == 0, i == 0), jnp.logical_and(j == N // tn - 1, i == M // tm - 1))
            begin_exchange()
        acc = None
        for p in range(n_pairs):
            a = refs[2 * p][...]
            if silu_a:
                a = _silu(a.astype(F32))
            d = lax.dot_general(a.astype(BF16), refs[2 * p + 1][...].astype(BF16), dims, preferred_element_type=F32)
            acc = d if acc is None else acc + d
        if bias is not None:
            acc = acc + refs[2 * n_pairs][...]
        o_ref[...] = acc.astype(o_ref.dtype)
        if hosted:
            end_exchange()

    in_specs, args = [], []
    for a, b in pairs:
        K = a.shape[1]
        in_specs.append(pl.BlockSpec((tm, K), lambda j, i: (i, 0)))
        in_specs.append(pl.BlockSpec((K, tn), lambda j, i: (0, j)) if mode == "nn" else pl.BlockSpec((tn, K), lambda j, i: (j, 0)))
        args += [a, b]
    if bias is not None:
        in_specs.append(pl.BlockSpec((1, tn), lambda j, i: (0, j)))
        args.append(bias)
    out_spec = pl.BlockSpec((tm, tn), lambda j, i: (i, j))
    out_shape = jax.ShapeDtypeStruct((M, N), out_dtype)
    if not hosted:
        return pl.pallas_call(
            body, name=name, grid=(N // tn, M // tm), in_specs=in_specs, out_specs=out_spec, out_shape=out_shape,
            compiler_params=_cparams("arbitrary", "arbitrary"),
        )(*args)
    return pl.pallas_call(
        body, name=name, grid=(N // tn, M // tm), in_specs=in_specs + hosted.specs,
        out_specs=[out_spec] + hosted.specs, out_shape=[out_shape] + hosted.out_shape, scratch_shapes=hosted.scratch,
        compiler_params=_cparams("arbitrary", "arbitrary"),
    )(*args, *hosted.arrays)


def matmul_tn(name, a, b, out_dtype=F32, *, silu_a=False, tm=1408, tn=1408, tk=2048):
    R, M = a.shape
    N = b.shape[1]
    tm = _tile(M, tm)
    tn = _tile(N, tn)
    tk = _tile(R, tk, 8)
    nk = R // tk

    def body(a_ref, b_ref, o_ref, acc):
        k = pl.program_id(2)

        @pl.when(k == 0)
        def _():
            acc[...] = jnp.zeros_like(acc)

        x = a_ref[...]
        if silu_a:
            x = _silu(x.astype(F32))
        acc[...] += lax.dot_general(x.astype(BF16), b_ref[...].astype(BF16), (((0,), (0,)), ((), ())),
                                    preferred_element_type=F32)

        @pl.when(k == nk - 1)
        def _():
            o_ref[...] = acc[...].astype(o_ref.dtype)

    return pl.pallas_call(
        body, name=name, grid=(M // tm, N // tn, nk),
        in_specs=[pl.BlockSpec((tk, tm), lambda i, j, k: (k, i)), pl.BlockSpec((tk, tn), lambda i, j, k: (k, j))],
        out_specs=pl.BlockSpec((tm, tn), lambda i, j, k: (i, j)),
        out_shape=jax.ShapeDtypeStruct((M, N), out_dtype),
        scratch_shapes=[pltpu.VMEM((tm, tn), F32)],
        compiler_params=_cparams("arbitrary", "arbitrary", "arbitrary"),
    )(a, b)


def _row_specs(rin, pbin, glin, tr):
    specs = [pl.BlockSpec((1, tr, w), lambda b, i, cb=cb, ro=ro, bo=(e[4] if len(e) > 4 else 0): (b + bo, i + ro, cb))
             for e in rin for (_, w, cb, ro) in [e[:4]]]
    specs += [pl.BlockSpec((1, 1, a.shape[-1]), lambda b, i: (b, 0, 0)) for a in pbin]
    specs += [pl.BlockSpec((1, a.shape[-1]), lambda b, i: (0, 0)) for a in glin]
    return specs


def rows_fwd(name, fn, nb, nblk, tr, rin, pbin, glin, outs):
    nr, npb, ngl = len(rin), len(pbin), len(glin)
    n_in = nr + npb + ngl

    def body(*refs):
        args = [r[0].astype(F32) for r in refs[:nr + npb]] + [r[...] for r in refs[nr + npb:n_in]]
        res = fn(*args)
        for o, v in zip(refs[n_in:], res):
            o[0] = v.astype(o.dtype)

    return pl.pallas_call(
        body, name=name, grid=(nb, nblk), in_specs=_row_specs(rin, pbin, glin, tr),
        out_specs=[pl.BlockSpec((1, tr, w), lambda b, i: (b, i, 0)) for (w, _) in outs],
        out_shape=[jax.ShapeDtypeStruct((nb, nblk * tr, w), dt) for (w, dt) in outs],
        compiler_params=_cparams("arbitrary", "arbitrary"),
    )(*[e[0] for e in rin], *pbin, *glin)


def rows_bwd(name, fn, nb, nblk, tr, rin, pbin, glin, cts, want):
    nr, npb, ngl, nct = len(rin), len(pbin), len(glin), len(cts)
    n_in = nr + npb + ngl

    def body(*refs):
        b, i = pl.program_id(0), pl.program_id(1)
        args = [r[0].astype(F32) for r in refs[:nr + npb]] + [r[...] for r in refs[nr + npb:n_in]]
        ct = tuple(r[0].astype(F32) for r in refs[n_in:n_in + nct])
        _, vjp = jax.vjp(fn, *args)
        g = vjp(ct)
        orefs = refs[n_in + nct:]
        for o, (idx, _) in zip(orefs, want):
            o[0] = g[idx].astype(o.dtype)
        pb_refs = orefs[len(want):len(want) + npb]
        gl_refs = orefs[len(want) + npb:]

        @pl.when(i == 0)
        def _():
            for o, v in zip(pb_refs, g[nr:nr + npb]):
                o[0] = v

        @pl.when(i > 0)
        def _():
            for o, v in zip(pb_refs, g[nr:nr + npb]):
                o[0] += v

        first = jnp.logical_and(b == 0, i == 0)

        @pl.when(first)
        def _():
            for o, v in zip(gl_refs, g[nr + npb:]):
                o[...] = v

        @pl.when(jnp.logical_not(first))
        def _():
            for o, v in zip(gl_refs, g[nr + npb:]):
                o[...] += v

    out_specs = [pl.BlockSpec((1, tr, rin[idx][1]), lambda b, i: (b, i, 0)) for (idx, _) in want]
    out_shape = [jax.ShapeDtypeStruct((nb, nblk * tr, rin[idx][1]), dt) for (idx, dt) in want]
    out_specs += [pl.BlockSpec((1, 1, a.shape[-1]), lambda b, i: (b, 0, 0)) for a in pbin]
    out_shape += [jax.ShapeDtypeStruct((nb, 1, a.shape[-1]), F32) for a in pbin]
    out_specs += [pl.BlockSpec((1, a.shape[-1]), lambda b, i: (0, 0)) for a in glin]
    out_shape += [jax.ShapeDtypeStruct((1, a.shape[-1]), F32) for a in glin]
    return pl.pallas_call(
        body, name=name, grid=(nb, nblk),
        in_specs=_row_specs(rin, pbin, glin, tr) + _row_specs(cts, [], [], tr),
        out_specs=out_specs, out_shape=out_shape,
        compiler_params=_cparams("arbitrary", "arbitrary"),
    )(*[e[0] for e in rin], *pbin, *glin, *[e[0] for e in cts])


def ew_call(name, fn, ins, outs):
    def body(*refs):
        res = fn(*[r[...] for r in refs[:len(ins)]])
        for o, v in zip(refs[len(ins):], res):
            o[...] = v.astype(o.dtype)

    return pl.pallas_call(body, name=name, out_shape=[jax.ShapeDtypeStruct(s, dt) for (s, dt) in outs])(*ins)


def fn_prenorm(x, shift, scale, g):
    return (_rms(x, g) * (1.0 + scale) + shift,)


def fn_rms(x, g):
    return (_rms(x, g),)


def fn_ssd_finish(yf, yr, xs, z, dexp, nw):
    y = yf + yr + dexp * xs
    return (_rms(y * _silu(z), nw),)


def fn_postmix(x, mix, gate1, scale2, shift2, post_g, pre_g):
    x1 = x + gate1 * _rms(mix, post_g)
    h2 = _rms(x1, pre_g) * (1.0 + scale2) + shift2
    return x1, h2


def final_call(x1, ffn, target, gate2, post_g, tr):
    nb, S, D = x1.shape
    nblk = S // tr

    def body(x1_ref, f_ref, t_ref, g2_ref, pg_ref, dx1_ref, df_ref, dg2_ref, dpg_ref, loss_ref):
        b, i = pl.program_id(0), pl.program_id(1)
        tgt = t_ref[0]

        def lossfn(x1v, fv, g2, pg):
            e = x1v + g2 * _rms(fv, pg) - tgt
            return 0.5 * jnp.sum(jnp.mean(e * e, axis=-1, keepdims=True))

        val, (dx1, df, dg2, dpg) = jax.value_and_grad(lossfn, argnums=(0, 1, 2, 3))(
            x1_ref[0], f_ref[0].astype(F32), g2_ref[0], pg_ref[...])
        dx1_ref[0] = dx1
        df_ref[0] = df.astype(df_ref.dtype)
        lv = jnp.full((1, LANE), val, F32)

        @pl.when(i == 0)
        def _():
            dg2_ref[0] = dg2

        @pl.when(i > 0)
        def _():
            dg2_ref[0] += dg2

        first = jnp.logical_and(b == 0, i == 0)

        @pl.when(first)
        def _():
            dpg_ref[...] = dpg
            loss_ref[...] = lv

        @pl.when(jnp.logical_not(first))
        def _():
            dpg_ref[...] += dpg
            loss_ref[...] += lv

    row = pl.BlockSpec((1, tr, D), lambda b, i: (b, i, 0))
    pb = pl.BlockSpec((1, 1, D), lambda b, i: (b, 0, 0))
    gl = pl.BlockSpec((1, D), lambda b, i: (0, 0))
    return pl.pallas_call(
        body, name="loss_head", grid=(nb, nblk), in_specs=[row, row, row, pb, gl],
        out_specs=[row, row, pb, gl, pl.BlockSpec((1, LANE), lambda b, i: (0, 0))],
        out_shape=[jax.ShapeDtypeStruct((nb, S, D), F32), jax.ShapeDtypeStruct((nb, S, D), BF16),
                   jax.ShapeDtypeStruct((nb, 1, D), F32), jax.ShapeDtypeStruct((1, D), F32),
                   jax.ShapeDtypeStruct((1, LANE), F32)],
        compiler_params=_cparams("arbitrary", "arbitrary"),
    )(x1, ffn, target, gate2, post_g)


def _rotate_half(t):
    lane = lax.broadcasted_iota(jnp.int32, t.shape, 1)
    return jnp.where((lane & 15) < 8, -pltpu.roll(t, LANE - 8, 1), pltpu.roll(t, 8, 1))


def rope_call(name, x, width, colblk, cos, sin, out_dtype, tr):
    nb = x.shape[0]
    R = cos.shape[0]
    nblk = R // tr

    def body(x_ref, c_ref, s_ref, o_ref):
        c, s = c_ref[...], s_ref[...]
        for h in range(width // LANE):
            t = x_ref[0, :, h * LANE:(h + 1) * LANE].astype(F32)
            o_ref[0, :, h * LANE:(h + 1) * LANE] = (t * c + _rotate_half(t) * s).astype(o_ref.dtype)

    tab = pl.BlockSpec((tr, LANE), lambda b, i: (i, 0))
    return pl.pallas_call(
        body, name=name, grid=(nb, nblk),
        in_specs=[pl.BlockSpec((1, tr, width), lambda b, i: (b, i, colblk)), tab, tab],
        out_specs=pl.BlockSpec((1, tr, width), lambda b, i: (b, i, 0)),
        out_shape=jax.ShapeDtypeStruct((nb, R, width), out_dtype),
        compiler_params=_cparams("arbitrary", "arbitrary"),
    )(x, cos, sin)


def rope_tables(n_ctx, seq):
    n_rows = seq // GRID_W
    row = np.repeat(np.arange(n_rows), GRID_W).astype(np.float32)
    col = np.tile(np.arange(GRID_W), n_rows).astype(np.float32)
    axis_dim = ROPE // 2
    inv_freq = jnp.asarray(ROPE_THETA, F32) ** (-jnp.arange(0, axis_dim, 2, dtype=F32) / axis_dim)
    ang_r = jnp.asarray(row)[:, None] * inv_freq
    ang_c = jnp.asarray(col)[:, None] * inv_freq
    ang = jnp.concatenate([ang_r, ang_r, ang_c, ang_c], axis=-1)
    cos = jnp.ones((n_ctx + seq, LANE), F32).at[n_ctx:, KR_LANE:KR_LANE + ROPE].set(jnp.cos(ang))
    sin = jnp.zeros((n_ctx + seq, LANE), F32).at[n_ctx:, KR_LANE:KR_LANE + ROPE].set(jnp.sin(ang))
    return cos, sin


Q_PRESCALE = ATTN_SCALE * math.log2(math.e)


def _attn_weights(q, kc):
    s2 = lax.dot_general(q, kc, (((1,), (1,)), ((), ())), preferred_element_type=F32)
    e = jnp.exp2(s2 - jnp.max(s2, axis=1, keepdims=True))
    return e, 1.0 / jnp.sum(e, axis=1, keepdims=True)


def _key_block(kv, kr):
    lane = lax.broadcasted_iota(jnp.int32, kv.shape, 1)
    return jnp.where(lane < NOPE, kv, kr)


def _rotated_query(q_ref, cos_ref, sin_ref):
    t = q_ref[0].astype(F32)
    return (t * cos_ref[...] + _rotate_half(t) * sin_ref[...]).astype(BF16)


def attn_fwd(q_raw, kv, kr, cos_q, sin_q, tq):
    nb, S, _ = q_raw.shape
    T = kv.shape[1]

    def body(q_ref, kv_ref, kr_ref, c_ref, s_ref, o_ref):
        kvv = kv_ref[0]
        e, r = _attn_weights(_rotated_query(q_ref, c_ref, s_ref), _key_block(kvv, kr_ref[0]))
        o = lax.dot_general(e.astype(BF16), kvv, (((1,), (0,)), ((), ())), preferred_element_type=F32) * r
        lane = lax.broadcasted_iota(jnp.int32, o.shape, 1)
        o_ref[0] = jnp.where(lane >= NOPE, o, 0.0).astype(o_ref.dtype)

    return pl.pallas_call(
        body, name="attn_fwd", grid=(nb, N_HEADS, S // tq),
        in_specs=[pl.BlockSpec((1, tq, HEAD_BLOCK), lambda b, h, i: (b, i, h)),
                  pl.BlockSpec((1, T, HEAD_BLOCK), lambda b, h, i: (b, 0, h)),
                  pl.BlockSpec((1, T, HEAD_BLOCK), lambda b, h, i: (b, 0, 0)),
                  pl.BlockSpec((tq, LANE), lambda b, h, i: (i, 0)), pl.BlockSpec((tq, LANE), lambda b, h, i: (i, 0))],
        out_specs=pl.BlockSpec((1, tq, HEAD_BLOCK), lambda b, h, i: (b, i, h)),
        out_shape=jax.ShapeDtypeStruct((nb, S, QP), BF16),
        compiler_params=_cparams("arbitrary", "arbitrary", "arbitrary"),
    )(q_raw, kv, kr, cos_q, sin_q)


def attn_bwd(q_raw, kv, kr, do, cos_q, sin_q, cos, sin, tq):
    nb, S, _ = q_raw.shape
    T = kv.shape[1]

    def body(q_ref, kv_ref, kr_ref, do_ref, cq_ref, sq_ref, c_ref, s_ref, dq_ref, dkv_ref, dkr_ref):
        h, i = pl.program_id(1), pl.program_id(2)

        @pl.when(i == 0)
        def _():
            dkv_ref[...] = jnp.zeros_like(dkv_ref)

        @pl.when(jnp.logical_and(h == 0, i == 0))
        def _():
            dkr_ref[...] = jnp.zeros_like(dkr_ref)

        qv, kvv, dov = _rotated_query(q_ref, cq_ref, sq_ref), kv_ref[0], do_ref[0]
        kc = _key_block(kvv, kr_ref[0])
        e, r = _attn_weights(qv, kc)
        dor = (dov.astype(F32) * r).astype(BF16)
        dpr = lax.dot_general(dor, kvv, (((1,), (1,)), ((), ())), preferred_element_type=F32)
        ds = (e * (dpr - r * jnp.sum(dpr * e, axis=1, keepdims=True))).astype(BF16)
        dq = lax.dot_general(ds, kc, (((1,), (0,)), ((), ())), preferred_element_type=F32) * ATTN_SCALE
        dq_ref[0] = (dq * c_ref[...] - _rotate_half(dq) * s_ref[...]).astype(dq_ref.dtype)
        dkc = lax.dot_general(ds, qv, (((0,), (0,)), ((), ())), preferred_element_type=F32) * math.log(2.0)
        dv = lax.dot_general(e.astype(BF16), dor, (((0,), (0,)), ((), ())), preferred_element_type=F32)
        lane = lax.broadcasted_iota(jnp.int32, dkc.shape, 1)
        dkv_ref[0] += jnp.where(lane < NOPE, dkc, dv)
        dkr_ref[0] += jnp.where(lane >= NOPE, dkc, 0.0)

    qspec = pl.BlockSpec((1, tq, HEAD_BLOCK), lambda b, h, i: (b, i, h))
    kspec = pl.BlockSpec((1, T, HEAD_BLOCK), lambda b, h, i: (b, 0, h))
    rspec = pl.BlockSpec((1, T, HEAD_BLOCK), lambda b, h, i: (b, 0, 0))
    tab = pl.BlockSpec((tq, LANE), lambda b, h, i: (i, 0))
    return pl.pallas_call(
        body, name="attn_bwd", grid=(nb, N_HEADS, S // tq),
        in_specs=[qspec, kspec, rspec, qspec, tab, tab, tab, tab], out_specs=[qspec, kspec, rspec],
        out_shape=[jax.ShapeDtypeStruct((nb, S, QP), BF16), jax.ShapeDtypeStruct((nb, T, QP), F32),
                   jax.ShapeDtypeStruct((nb, T, HEAD_BLOCK), F32)],
        compiler_params=_cparams("arbitrary", "arbitrary", "arbitrary"),
    )(q_raw, kv, kr, do, cos_q, sin_q, cos, sin)


CONV_HALO = 8


def _segments(n, n_ctx):
    if n_ctx == 0:
        return [(0, n, CONV_HALO)]
    return [(0, n_ctx, CONV_HALO), (n_ctx, n - n_ctx, 2 * CONV_HALO + n_ctx)]


def _halo_scratch(n, n_ctx, tc):
    return pltpu.VMEM((n + CONV_HALO * (len(_segments(n, n_ctx)) + 1), tc), F32)


def _zero_halos(scr, segs):
    z = jnp.zeros((CONV_HALO, scr.shape[1]), scr.dtype)
    scr[0:CONV_HALO, :] = z
    for (_, rows, off) in segs:
        scr[off + rows:off + rows + CONV_HALO, :] = z


CONV_BLOCK_MAX = 256


def _conv_block(n, n_ctx):
    return _tile(math.gcd(n_ctx, n - n_ctx) if n_ctx else n, CONV_BLOCK_MAX, 8)


def _window(scr, off, r0, blk):
    return scr[pl.ds(pl.multiple_of(off - CONV_HALO + r0, 8), blk + 2 * CONV_HALO), :]


def _shifted(win, s):
    v = win if s == 0 else pltpu.roll(win, (-s) % win.shape[0], 0)
    return v[CONV_HALO:win.shape[0] - CONV_HALO]


def _tap_blocks(win, k, sign):
    return [_shifted(win, sign * (o - k // 2)) for o in range(k)]


def _taps(blocks, w):
    acc = None
    for o, blk in enumerate(blocks):
        t = w[o:o + 1, :] * blk
        acc = t if acc is None else acc + t
    return acc


def _tap_grads(xblocks, dpre):
    k = len(xblocks)
    sub8 = lax.broadcasted_iota(jnp.int32, (8, dpre.shape[1]), 0)
    out = jnp.where(sub8 == k, jnp.sum(dpre, axis=0, keepdims=True), 0.0)
    for o, blk in enumerate(xblocks):
        out = out + jnp.where(sub8 == o, jnp.sum(dpre * blk, axis=0, keepdims=True), 0.0)
    return out


def _row_blocks(rows, blk, fn, init=0):
    return lax.fori_loop(0, rows // blk, lambda i, c: fn(pl.multiple_of(i * blk, blk), c), init)


def _gelu(x):
    return 0.5 * x * (1.0 + lax.erf(x * (1.0 / math.sqrt(2.0))))


def _gelu_and_grad(x):
    cdf = 0.5 * (1.0 + lax.erf(x * (1.0 / math.sqrt(2.0))))
    return x * cdf, cdf + x * jnp.exp(-0.5 * x * x) * (1.0 / math.sqrt(2.0 * math.pi))


def ssd_conv_fwd(u, w8, bias, n_ctx, tc):
    nb, T, _ = u.shape
    cb0 = OFF_XBC // tc

    segs, blk = _segments(T, n_ctx), _conv_block(T, n_ctx)

    def body(x_ref, w_ref, b_ref, o_ref, xs):
        _zero_halos(xs, segs)
        for (start, rows, off) in segs:
            xs[off:off + rows, :] = x_ref[0, start:start + rows, :]
        w, bias_v = w_ref[...], b_ref[...]
        for (start, rows, off) in segs:
            def block(r0, carry, start=start, off=off):
                pre = bias_v + _taps(_tap_blocks(_window(xs, off, r0, blk), SSD_K, 1), w)
                o_ref[0, pl.ds(pl.multiple_of(start + r0, blk), blk), :] = _silu(pre)
                return carry

            _row_blocks(rows, blk, block)

    return pl.pallas_call(
        body, name="ssd_conv_fwd", grid=(nb, XBC // tc),
        in_specs=[pl.BlockSpec((1, T, tc), lambda b, j: (b, 0, cb0 + j)),
                  pl.BlockSpec((8, tc), lambda b, j: (0, j)), pl.BlockSpec((1, tc), lambda b, j: (0, j))],
        out_specs=pl.BlockSpec((1, T, tc), lambda b, j: (b, 0, j)),
        out_shape=jax.ShapeDtypeStruct((nb, T, XBC), F32),
        scratch_shapes=[_halo_scratch(T, n_ctx, tc)],
        compiler_params=_cparams("arbitrary", "arbitrary"),
    )(u, w8, bias)


def ssd_conv_bwd(u, w8, bias, dxbc, dxs_direct, n_ctx, tc):
    nb, T, _ = u.shape
    cb0 = OFF_XBC // tc
    n_direct = D_INNER // tc

    segs, blk = _segments(T, n_ctx), _conv_block(T, n_ctx)

    def body(x_ref, w_ref, b_ref, d0_ref, d1_ref, dd_ref, dx_ref, dw_ref, xs, ds):
        j, b = pl.program_id(0), pl.program_id(1)
        _zero_halos(xs, segs)
        _zero_halos(ds, segs)
        for (start, rows, off) in segs:
            xs[off:off + rows, :] = x_ref[0, start:start + rows, :]
        w, bias_v = w_ref[...], b_ref[...]
        has_direct = (j < n_direct).astype(F32)
        rows = jnp.zeros((8, tc), F32)
        for (start, n_rows, off) in segs:
            def block(r0, acc, start=start, off=off):
                xblocks = _tap_blocks(_window(xs, off, r0, blk), SSD_K, 1)
                pre = bias_v + _taps(xblocks, w)
                d = d0_ref[0, 0, pl.ds(pl.multiple_of(start + r0, blk), blk), :] + d1_ref[0, 0, pl.ds(pl.multiple_of(start + r0, blk), blk), :]
                if start == n_ctx:
                    d = d + dd_ref[0, pl.ds(r0, blk), :] * has_direct
                sg = jax.nn.sigmoid(pre)
                dpre = d * (sg * (1.0 + pre * (1.0 - sg)))
                ds[pl.ds(pl.multiple_of(off + r0, 8), blk), :] = dpre
                return acc + _tap_grads(xblocks, dpre)

            rows = _row_blocks(n_rows, blk, block, rows)
        for (start, n_rows, off) in segs:
            def block_dx(r0, carry, start=start, off=off):
                dx_ref[0, pl.ds(pl.multiple_of(start + r0, blk), blk), :] = _taps(_tap_blocks(_window(ds, off, r0, blk), SSD_K, -1), w).astype(dx_ref.dtype)
                return carry

            _row_blocks(n_rows, blk, block_dx)

        @pl.when(b == 0)
        def _():
            dw_ref[...] = rows

        @pl.when(b > 0)
        def _():
            dw_ref[...] += rows

    dspec0 = pl.BlockSpec((1, 1, T, tc), lambda j, b: (0, b, 0, j))
    dspec1 = pl.BlockSpec((1, 1, T, tc), lambda j, b: (1, b, 0, j))
    return pl.pallas_call(
        body, name="ssd_conv_bwd", grid=(XBC // tc, nb),
        in_specs=[pl.BlockSpec((1, T, tc), lambda j, b: (b, 0, cb0 + j)),
                  pl.BlockSpec((8, tc), lambda j, b: (0, j)), pl.BlockSpec((1, tc), lambda j, b: (0, j)),
                  dspec0, dspec1,
                  pl.BlockSpec((1, T - n_ctx, tc), lambda j, b: (b, 0, jnp.minimum(j, n_direct - 1)))],
        out_specs=[pl.BlockSpec((1, T, tc), lambda j, b: (b, 0, j)), pl.BlockSpec((8, tc), lambda j, b: (0, j))],
        out_shape=[jax.ShapeDtypeStruct((nb, T, XBC), BF16), jax.ShapeDtypeStruct((8, XBC), F32)],
        scratch_shapes=[_halo_scratch(T, n_ctx, tc), _halo_scratch(T, n_ctx, tc)],
        compiler_params=_cparams("arbitrary", "arbitrary"),
    )(u, w8, bias, dxbc, dxbc, dxs_direct)


GLU_TC = 256


def glu_interleave(w_up):
    blocks = []
    for j in range(D_FF // GLU_TC):
        blocks += [w_up[:, j * GLU_TC:(j + 1) * GLU_TC], w_up[:, D_FF + j * GLU_TC:D_FF + (j + 1) * GLU_TC]]
    return jnp.concatenate(blocks, axis=1)


def glu_deinterleave(g):
    nj = D_FF // GLU_TC
    gate = [g[:, 2 * j * GLU_TC:(2 * j + 1) * GLU_TC] for j in range(nj)]
    val = [g[:, (2 * j + 1) * GLU_TC:(2 * j + 2) * GLU_TC] for j in range(nj)]
    return jnp.concatenate(gate + val, axis=1)


def glu_fwd(up, w8, bias):
    nb, S, _ = up.shape
    tc = GLU_TC

    segs, blk = _segments(S, 0), _conv_block(S, 0)
    (_, _, off), = segs

    def body(u_ref, w_ref, b_ref, o_ref, xs):
        _zero_halos(xs, segs)
        xs[off:off + S, :] = u_ref[0, :, :tc]
        w, bias_v = w_ref[...], b_ref[...]

        def block(r0, carry):
            gc = bias_v + _taps(_tap_blocks(_window(xs, off, r0, blk), FFN_K, 1), w)
            o_ref[0, pl.ds(r0, blk), :] = (_gelu(gc) * u_ref[0, pl.ds(r0, blk), tc:]).astype(o_ref.dtype)
            return carry

        _row_blocks(S, blk, block)

    return pl.pallas_call(
        body, name="glu_fwd", grid=(nb, D_FF // tc),
        in_specs=[pl.BlockSpec((1, S, 2 * tc), lambda b, j: (b, 0, j)),
                  pl.BlockSpec((8, tc), lambda b, j: (0, j)), pl.BlockSpec((1, tc), lambda b, j: (0, j))],
        out_specs=pl.BlockSpec((1, S, tc), lambda b, j: (b, 0, j)),
        out_shape=jax.ShapeDtypeStruct((nb, S, D_FF), BF16),
        scratch_shapes=[_halo_scratch(S, 0, tc)],
        compiler_params=_cparams("arbitrary", "arbitrary"),
    )(up, w8, bias)


def glu_bwd(up, w8, bias, dact):
    nb, S, _ = up.shape
    tc = GLU_TC

    segs, blk = _segments(S, 0), _conv_block(S, 0)
    (_, _, off), = segs

    def body(u_ref, w_ref, b_ref, d_ref, du_ref, dw_ref, xs, ds):
        b = pl.program_id(1)
        _zero_halos(xs, segs)
        _zero_halos(ds, segs)
        xs[off:off + S, :] = u_ref[0, :, :tc]
        w, bias_v = w_ref[...], b_ref[...]

        def block(r0, acc):
            here = pl.ds(r0, blk)
            xblocks = _tap_blocks(_window(xs, off, r0, blk), FFN_K, 1)
            act, act_grad = _gelu_and_grad(bias_v + _taps(xblocks, w))
            d = d_ref[0, here, :].astype(F32)
            du_ref[0, here, tc:] = (d * act).astype(du_ref.dtype)
            dpre = d * u_ref[0, here, tc:] * act_grad
            ds[pl.ds(pl.multiple_of(off + r0, 8), blk), :] = dpre
            return acc + _tap_grads(xblocks, dpre)

        rows = _row_blocks(S, blk, block, jnp.zeros((8, tc), F32))

        def block_dx(r0, carry):
            du_ref[0, pl.ds(r0, blk), :tc] = _taps(_tap_blocks(_window(ds, off, r0, blk), FFN_K, -1), w).astype(du_ref.dtype)
            return carry

        _row_blocks(S, blk, block_dx)

        @pl.when(b == 0)
        def _():
            dw_ref[...] = rows

        @pl.when(b > 0)
        def _():
            dw_ref[...] += rows

    pair = pl.BlockSpec((1, S, 2 * tc), lambda j, b: (b, 0, j))
    return pl.pallas_call(
        body, name="glu_bwd", grid=(D_FF // tc, nb),
        in_specs=[pair, pl.BlockSpec((8, tc), lambda j, b: (0, j)), pl.BlockSpec((1, tc), lambda j, b: (0, j)),
                  pl.BlockSpec((1, S, tc), lambda j, b: (b, 0, j))],
        out_specs=[pair, pl.BlockSpec((8, tc), lambda j, b: (0, j))],
        out_shape=[jax.ShapeDtypeStruct((nb, S, 2 * D_FF), BF16), jax.ShapeDtypeStruct((8, D_FF), F32)],
        scratch_shapes=[_halo_scratch(S, 0, tc), _halo_scratch(S, 0, tc)],
        compiler_params=_cparams("arbitrary", "arbitrary"),
    )(up, w8, bias, dact)


def _chunk_of(d, k, n_cc, n_ch):
    rev = jnp.where(k < n_cc, n_cc - 1 - k, n_cc + n_ch - 1 - k)
    return jnp.where(d == 1, rev, k)


def _lane_pick(v, lane_iota, l):
    return jnp.sum(jnp.where(lane_iota == l, v, 0.0), axis=1, keepdims=True)


def head_spread_matrix():
    return (jnp.arange(LANE)[:, None] == (jnp.arange(D_INNER)[None, :] // SSD_P)).astype(BF16)


def _split_dot(x, e, dims):
    hi = x.astype(BF16)
    lo = (x - hi.astype(F32)).astype(BF16)
    return (lax.dot_general(hi, e, dims, preferred_element_type=F32)
            + lax.dot_general(lo, e, dims, preferred_element_type=F32))


def _spread(x, e):
    return _split_dot(x, e, (((1,), (0,)), ((), ())))


def _gather_heads(y, e):
    return _split_dot(y, e, (((1,), (1,)), ((), ())))


def _softplus(x):
    return jnp.maximum(x, 0.0) + jnp.log(1.0 + jnp.exp(-jnp.abs(x)))


def ssd_dt_inputs(u, a_log, dt_bias):
    pad = LANE - SSD_HEADS
    dt = u[..., OFF_DT:OFF_DT + 2 * SSD_HEADS]
    dt2 = jnp.stack([jnp.pad(dt[..., i * SSD_HEADS:(i + 1) * SSD_HEADS], ((0, 0), (0, 0), (0, pad))) for i in range(2)])

    def lanes(v):
        return jnp.pad(v.reshape(2, 1, SSD_HEADS), ((0, 0), (0, 0), (0, pad)))

    return dt2, lanes(a_log), lanes(dt_bias)


def _ssd_common(d, dt_raw, alog, dtb):
    Q = dt_raw.shape[0]
    row = lax.broadcasted_iota(jnp.int32, (Q, Q), 0)
    col = lax.broadcasted_iota(jnp.int32, (Q, Q), 1)
    rev = d == 1
    maskb = jnp.where(rev, row, col) <= jnp.where(rev, col, row)
    tri = maskb.astype(F32)
    A = -jnp.exp(alog)
    dtv = _softplus(dt_raw + dtb)
    a = dtv * A
    cum = lax.dot_general(tri, a, (((1,), (0,)), ((), ())), precision=lax.Precision.HIGHEST, preferred_element_type=F32)
    tot = jnp.sum(a, axis=0, keepdims=True)
    return maskb, tri, A, dtv, cum, tot


def ssd_fwd(xbc, dt2, alog2, dtb2, n_ctx, hosted):
    nb, T, _ = xbc.shape
    S = T - n_ctx
    n_ch, n_cc = T // CHUNK, n_ctx // CHUNK
    Q = CHUNK
    n_pairs = SSD_HEADS // 2
    n_ex = hosted.n
    n_in = 5

    def body(*refs):
        x_ref, dt_ref, al_ref, db_ref, e_ref = refs[:n_in]
        send_refs = refs[n_in:n_in + n_ex]
        y_ref, hin_ref = refs[n_in + n_ex:n_in + 2 + n_ex]
        recv_refs = refs[n_in + 2 + n_ex:n_in + 2 + 2 * n_ex]
        H, *sems = refs[n_in + 2 + 2 * n_ex:]
        d, k = pl.program_id(1), pl.program_id(2)
        first_step = jnp.logical_and(jnp.logical_and(pl.program_id(0) == 0, d == 0), k == 0)
        last_step = jnp.logical_and(jnp.logical_and(pl.program_id(0) == nb - 1, d == 1), k == n_ch - 1)
        begin_exchange, end_exchange = hosted.steps(send_refs, recv_refs, sems, first_step, last_step)
        begin_exchange()

        @pl.when(k == 0)
        def _():
            H[...] = jnp.zeros_like(H)

        maskb, tri, A, dtv, cum, tot = _ssd_common(d, dt_ref[0, 0], al_ref[0], db_ref[0])
        e = e_ref[...]
        cumT = cum.T
        cum_e, dt_e = _spread(cum, e), _spread(dtv, e)
        tot_e = _spread(jnp.broadcast_to(tot, (8, LANE)), e)[0:1]
        hin_ref[0, 0, 0] = H[...].astype(BF16)
        lane = lax.broadcasted_iota(jnp.int32, (Q, LANE), 1)
        lane1 = lax.broadcasted_iota(jnp.int32, (1, LANE), 1)
        subc = lax.broadcasted_iota(jnp.int32, (LANE, 1), 0)
        half = lane < SSD_P
        for g in range(SSD_GROUPS):
            Bg = x_ref[0, :, D_INNER + g * SSD_N:D_INNER + (g + 1) * SSD_N].astype(BF16)
            Cg = x_ref[0, :, D_INNER + GN + g * SSD_N:D_INNER + GN + (g + 1) * SSD_N].astype(BF16)
            Gm = lax.dot_general(Cg, Bg, (((1,), (1,)), ((), ())), preferred_element_type=F32)
            for pr in range(n_pairs // SSD_GROUPS):
                p = g * (n_pairs // SSD_GROUPS) + pr
                sc, dtp, totp = [t[:, p * LANE:(p + 1) * LANE] for t in (cum_e, dt_e, tot_e)]
                swapped = pltpu.roll(sc, SSD_P, 1)
                s0c, s1c = jnp.where(half, sc, swapped), jnp.where(half, swapped, sc)
                s0r, s1r = cumT[2 * p:2 * p + 1, :], cumT[2 * p + 1:2 * p + 2, :]
                tot0, tot1 = _lane_pick(tot, lane1, 2 * p), _lane_pick(tot, lane1, 2 * p + 1)
                M0 = (Gm * jnp.exp(jnp.where(maskb, s0c - s0r, NEG_BIG))).astype(BF16)
                M1 = (Gm * jnp.exp(jnp.where(maskb, s1c - s1r, NEG_BIG))).astype(BF16)
                xd = x_ref[0, :, p * LANE:(p + 1) * LANE] * dtp
                xdb = xd.astype(BF16)
                yd = jnp.where(half,
                               lax.dot_general(M0, xdb, (((1,), (0,)), ((), ())), preferred_element_type=F32),
                               lax.dot_general(M1, xdb, (((1,), (0,)), ((), ())), preferred_element_type=F32))
                Hp = H[p * LANE:(p + 1) * LANE, :]
                yo = lax.dot_general(Cg, Hp.astype(BF16), (((1,), (1,)), ((), ())), preferred_element_type=F32) * jnp.exp(sc)
                y_ref[0, 0, :, p * LANE:(p + 1) * LANE] = yd + yo
                xdw = (xd * jnp.exp(totp - sc)).astype(BF16)
                etot = jnp.exp(jnp.where(subc < SSD_P, tot0, tot1))
                H[p * LANE:(p + 1) * LANE, :] = Hp * etot + lax.dot_general(
                    xdw, Bg, (((0,), (0,)), ((), ())), preferred_element_type=F32)
        end_exchange()

    def ymap(b, d, k):
        return (d, b, _chunk_of(d, jnp.maximum(k, n_cc), n_cc, n_ch) - n_cc, 0)

    return pl.pallas_call(
        body, name="ssd_fwd", grid=(nb, 2, n_ch),
        in_specs=[pl.BlockSpec((1, Q, XBC), lambda b, d, k: (b, _chunk_of(d, k, n_cc, n_ch), 0)),
                  pl.BlockSpec((1, 1, Q, LANE), lambda b, d, k: (d, b, _chunk_of(d, k, n_cc, n_ch), 0)),
                  pl.BlockSpec((1, 1, LANE), lambda b, d, k: (d, 0, 0)), pl.BlockSpec((1, 1, LANE), lambda b, d, k: (d, 0, 0)),
                  pl.BlockSpec((LANE, D_INNER), lambda b, d, k: (0, 0))] + hosted.specs,
        out_specs=[pl.BlockSpec((1, 1, Q, D_INNER), ymap),
                   pl.BlockSpec((1, 1, 1, D_INNER, SSD_N), lambda b, d, k: (d, b, k, 0, 0))] + hosted.specs,
        out_shape=[jax.ShapeDtypeStruct((2, nb, S, D_INNER), F32),
                   jax.ShapeDtypeStruct((2, nb, n_ch, D_INNER, SSD_N), BF16)] + hosted.out_shape,
        scratch_shapes=[pltpu.VMEM((D_INNER, SSD_N), F32)] + hosted.scratch,
        compiler_params=_cparams("arbitrary", "arbitrary", "arbitrary"),
    )(xbc, dt2, alog2, dtb2, head_spread_matrix(), *hosted.arrays)


def ssd_bwd(xbc, dt2, alog2, dtb2, hin, dy, n_ctx, hosted):
    nb, T, _ = xbc.shape
    n_ex = hosted.n
    n_ch, n_cc = T // CHUNK, n_ctx // CHUNK
    n_in = 7
    Q = CHUNK
    n_pairs = SSD_HEADS // 2
    NT = (((1,), (1,)), ((), ()))
    NN = (((1,), (0,)), ((), ()))
    TN = (((0,), (0,)), ((), ()))

    def dot(a, b, dims):
        return lax.dot_general(a.astype(BF16), b.astype(BF16), dims, preferred_element_type=F32)

    def body(*refs):
        x_ref, dt_ref, al_ref, db_ref, e_ref, hin_ref, dy_ref = refs[:n_in]
        send_refs = refs[n_in:n_in + n_ex]
        dx_ref, ddt_ref, st_ref = refs[n_in + n_ex:n_in + 3 + n_ex]
        recv_refs = refs[n_in + 3 + n_ex:n_in + 3 + 2 * n_ex]
        dH, dce, dde, *sems = refs[n_in + 3 + 2 * n_ex:]
        d, kk = pl.program_id(1), pl.program_id(2)
        ks = n_ch - 1 - kk
        first_step = jnp.logical_and(jnp.logical_and(pl.program_id(0) == 0, d == 0), kk == 0)
        last_step = jnp.logical_and(jnp.logical_and(pl.program_id(0) == nb - 1, d == 1), kk == n_ch - 1)
        begin_exchange, end_exchange = hosted.steps(send_refs, recv_refs, sems, first_step, last_step)
        begin_exchange()

        @pl.when(kk == 0)
        def _():
            dH[...] = jnp.zeros_like(dH)

        @pl.when(jnp.logical_and(jnp.logical_and(pl.program_id(0) == 0, d == 0), kk == 0))
        def _():
            st_ref[...] = jnp.zeros_like(st_ref)

        dt_raw = dt_ref[0, 0]
        alog, dtb_v = al_ref[0], db_ref[0]
        maskb, tri, A, dtv, cum, tot = _ssd_common(d, dt_raw, alog, dtb_v)
        e = e_ref[...]
        cumT = cum.T
        cum_e, dt_e = _spread(cum, e), _spread(dtv, e)
        tot_e = _spread(jnp.broadcast_to(tot, (8, LANE)), e)[0:1]
        live = (ks >= n_cc).astype(F32)
        lane = lax.broadcasted_iota(jnp.int32, (Q, LANE), 1)
        lane1 = lax.broadcasted_iota(jnp.int32, (1, LANE), 1)
        sub = lax.broadcasted_iota(jnp.int32, (LANE, Q), 0)
        subc = lax.broadcasted_iota(jnp.int32, (LANE, 1), 0)
        half = lane < SSD_P
        halfc = subc < SSD_P
        pair_ones = ((lax.broadcasted_iota(jnp.int32, (2 * Q, LANE), 0) >= Q).astype(jnp.int32)
                     == (lax.broadcasted_iota(jnp.int32, (2 * Q, LANE), 1) >= SSD_P).astype(jnp.int32)).astype(BF16)
        dcumT = jnp.zeros((LANE, Q), F32)
        dtot = jnp.zeros((1, LANE), F32)
        dtot_parts = []
        for g in range(SSD_GROUPS):
            Bg = x_ref[0, :, D_INNER + g * SSD_N:D_INNER + (g + 1) * SSD_N].astype(BF16)
            Cg = x_ref[0, :, D_INNER + GN + g * SSD_N:D_INNER + GN + (g + 1) * SSD_N].astype(BF16)
            Gm = lax.dot_general(Cg, Bg, NT, preferred_element_type=F32)
            dG = jnp.zeros((Q, Q), F32)
            dC = jnp.zeros((Q, SSD_N), F32)
            dB = jnp.zeros((Q, SSD_N), F32)
            for pr in range(n_pairs // SSD_GROUPS):
                p = g * (n_pairs // SSD_GROUPS) + pr
                l0, l1 = 2 * p, 2 * p + 1
                sc, dtp, totp = [t[:, p * LANE:(p + 1) * LANE] for t in (cum_e, dt_e, tot_e)]
                swapped = pltpu.roll(sc, SSD_P, 1)
                s0c, s1c = jnp.where(half, sc, swapped), jnp.where(half, swapped, sc)
                s0r, s1r = cumT[l0:l0 + 1, :], cumT[l1:l1 + 1, :]
                tot0, tot1 = _lane_pick(tot, lane1, l0), _lane_pick(tot, lane1, l1)
                L0 = jnp.exp(jnp.where(maskb, s0c - s0r, NEG_BIG))
                L1 = jnp.exp(jnp.where(maskb, s1c - s1r, NEG_BIG))
                M0, M1 = Gm * L0, Gm * L1
                xs = x_ref[0, :, p * LANE:(p + 1) * LANE]
                xd = xs * dtp
                es = jnp.exp(sc)
                dte = jnp.exp(totp - sc)
                etot = jnp.exp(jnp.where(halfc, tot0, tot1))
                dyp = dy_ref[0, :, p * LANE:(p + 1) * LANE] * live
                Hp = hin_ref[0, 0, 0, p * LANE:(p + 1) * LANE, :]
                dHp = dH[p * LANE:(p + 1) * LANE, :]
                bdh = dot(Bg, dHp, NT)
                mtdy = dot(jnp.concatenate([M0, M1], axis=1), dyp, TN)
                dxd = jnp.where(half, mtdy[:Q], mtdy[Q:]) + bdh * dte
                dy0 = jnp.where(half, dyp, 0.0)
                dm = dot(jnp.concatenate([dy0, dyp - dy0], axis=0), xd, NT)
                dM0, dM1 = dm[:Q], dm[Q:]
                dG = dG + dM0 * L0 + dM1 * L1
                dyes = dyp * es
                xdw = xd * dte
                dC = dC + dot(dyes, Hp, NN)
                dB = dB + dot(xdw, dHp, NN)
                W0, W1 = dM0 * M0, dM1 * M1
                yoff = dot(Cg, Hp, NT) * es
                r_off = dyp * yoff
                r_st = xd * bdh * dte
                hh = jnp.sum(dHp * Hp.astype(F32), axis=1, keepdims=True) * etot
                w_rows = _split_dot(jnp.concatenate([W0, W1], axis=1), pair_ones, NN) * (1.0 / SSD_P)
                dce[:, p * LANE:(p + 1) * LANE] = r_off - r_st + w_rows
                dde[:, p * LANE:(p + 1) * LANE] = dxd * xs
                dtot_parts.append(jnp.sum(r_st, axis=0, keepdims=True))
                for (l, W, hselc) in ((l0, W0, halfc), (l1, W1, jnp.logical_not(halfc))):
                    row_g = -jnp.sum(W, axis=0, keepdims=True)
                    dcumT = dcumT + jnp.where(sub == l, row_g, 0.0)
                    dtot = dtot + jnp.where(lane1 == l, jnp.sum(jnp.where(hselc, hh, 0.0), axis=0, keepdims=True), 0.0)
                dx_ref[0, 0, :, p * LANE:(p + 1) * LANE] = dxd * dtp
                dH[p * LANE:(p + 1) * LANE, :] = dHp * etot + dot(dyes, Cg, TN)
            dx_ref[0, 0, :, D_INNER + g * SSD_N:D_INNER + (g + 1) * SSD_N] = dB + dot(dG, Cg, TN)
            dx_ref[0, 0, :, D_INNER + GN + g * SSD_N:D_INNER + GN + (g + 1) * SSD_N] = dC + dot(dG, Bg, NN)
        dcum_all = dcumT.T + _gather_heads(dce[...], e)
        dtot_e = jnp.broadcast_to(jnp.concatenate(dtot_parts, axis=1), (8, D_INNER))
        dtot = dtot + _gather_heads(dtot_e, e)[0:1]
        da = lax.dot_general(tri, dcum_all, TN, precision=lax.Precision.HIGHEST, preferred_element_type=F32) + dtot
        ddtv = _gather_heads(dde[...], e) + da * A
        ddt_raw = ddtv * jax.nn.sigmoid(dt_raw + dtb_v)
        ddt_ref[0, 0] = ddt_raw
        sub8 = lax.broadcasted_iota(jnp.int32, (8, LANE), 0)
        st_ref[...] += (jnp.where(sub8 == 2 * d, jnp.sum(da * dtv * A, axis=0, keepdims=True), 0.0)
                        + jnp.where(sub8 == 2 * d + 1, jnp.sum(ddt_raw, axis=0, keepdims=True), 0.0))
        end_exchange()

    def cmap(d, kk):
        return _chunk_of(d, n_ch - 1 - kk, n_cc, n_ch)

    def dymap(b, d, kk):
        return (b, _chunk_of(d, jnp.maximum(n_ch - 1 - kk, n_cc), n_cc, n_ch) - n_cc, 0)

    return pl.pallas_call(
        body, name="ssd_bwd", grid=(nb, 2, n_ch),
        in_specs=[pl.BlockSpec((1, Q, XBC), lambda b, d, kk: (b, cmap(d, kk), 0)),
                  pl.BlockSpec((1, 1, Q, LANE), lambda b, d, kk: (d, b, cmap(d, kk), 0)),
                  pl.BlockSpec((1, 1, LANE), lambda b, d, kk: (d, 0, 0)), pl.BlockSpec((1, 1, LANE), lambda b, d, kk: (d, 0, 0)),
                  pl.BlockSpec((LANE, D_INNER), lambda b, d, kk: (0, 0)),
                  pl.BlockSpec((1, 1, 1, D_INNER, SSD_N), lambda b, d, kk: (d, b, n_ch - 1 - kk, 0, 0)),
                  pl.BlockSpec((1, Q, D_INNER), dymap)] + hosted.specs,
        out_specs=[pl.BlockSpec((1, 1, Q, XBC), lambda b, d, kk: (d, b, cmap(d, kk), 0)),
                   pl.BlockSpec((1, 1, Q, LANE), lambda b, d, kk: (d, b, cmap(d, kk), 0)),
                   pl.BlockSpec((8, LANE), lambda b, d, kk: (0, 0))] + hosted.specs,
        out_shape=[jax.ShapeDtypeStruct((2, nb, T, XBC), F32), jax.ShapeDtypeStruct((2, nb, T, LANE), F32),
                   jax.ShapeDtypeStruct((8, LANE), F32)] + hosted.out_shape,
        scratch_shapes=[pltpu.VMEM((D_INNER, SSD_N), F32), pltpu.VMEM((Q, D_INNER), F32), pltpu.VMEM((Q, D_INNER), F32)] + hosted.scratch,
        compiler_params=_cparams("arbitrary", "arbitrary", "arbitrary"),
    )(xbc, dt2, alog2, dtb2, head_spread_matrix(), hin, dy, *hosted.arrays)


def _adamw(w, g, m, v):
    mn = ADAM_B1 * m + (1.0 - ADAM_B1) * g
    vn = ADAM_B2 * v + (1.0 - ADAM_B2) * jnp.square(g)
    m_hat = mn / (1.0 - ADAM_B1 ** ADAM_STEP)
    v_hat = vn / (1.0 - ADAM_B2 ** ADAM_STEP)
    return -ADAM_LR * (m_hat / (jnp.sqrt(v_hat) + ADAM_EPS) + ADAM_WD * w), mn, vn


def adamw_matrix(name, w, g_slots, m, v):
    K, n = w.shape
    s = g_slots.shape[0]
    tr = _tile(K, 256, 8)

    def body(w_ref, g_ref, m_ref, v_ref, go_ref, d_ref, mo_ref, vo_ref):
        g = g_ref[0].astype(F32)
        for j in range(1, s):
            g = g + g_ref[j].astype(F32)
        go_ref[...] = g
        d_ref[...], mo_ref[...], vo_ref[...] = _adamw(w_ref[...], g, m_ref[...], v_ref[...])

    spec = pl.BlockSpec((tr, n), lambda i: (i, 0))
    return pl.pallas_call(
        body, name=name, grid=(K // tr,),
        in_specs=[spec, pl.BlockSpec((s, tr, n), lambda i: (0, i, 0)), spec, spec], out_specs=[spec] * 4,
        out_shape=[jax.ShapeDtypeStruct((K, n), F32)] * 4,
        compiler_params=_cparams("arbitrary"),
    )(w, g_slots, m, v)


def adamw_small(ws, gs, ms, vs):
    n = len(ws)

    def body(*refs):
        for i in range(n):
            d, mn, vn = _adamw(refs[i][...], refs[n + i][...], refs[2 * n + i][...], refs[3 * n + i][...])
            refs[4 * n + i][...] = d
            refs[5 * n + i][...] = mn
            refs[6 * n + i][...] = vn

    shapes = [jax.ShapeDtypeStruct(w.shape, F32) for w in ws]
    out = pl.pallas_call(body, name="adamw_small", out_shape=shapes * 3)(*ws, *gs, *ms, *vs)
    return out[:n], out[n:2 * n], out[2 * n:]


def sum_slots(name, x):
    n = x.shape[0]

    def fn(t):
        acc = t[0]
        for j in range(1, n):
            acc = acc + t[j]
        return (acc,)

    return ew_call(name, fn, [x], [(x.shape[1:], F32)])[0]


def _pack_rows(parts):
    rows = []
    for p in parts:
        flat = p.reshape(1, -1)
        n = flat.shape[1]
        rows.append(jnp.pad(flat, ((0, 0), (0, -(-n // (8 * LANE)) * 8 * LANE - n))).reshape(-1, LANE))
    return jnp.concatenate(rows, axis=0)


def _unpack_rows(pack, shapes):
    out, r = [], 0
    for s in shapes:
        n = int(np.prod(s))
        nr = -(-n // (8 * LANE)) * 8
        out.append(pack[r:r + nr].reshape(1, -1)[:, :n].reshape(s))
        r += nr
    return out


def _mesh_pos():
    return lax.axis_index("x"), lax.axis_index("y"), lax.axis_index("c")


N_PEERS = N_DEV - 1


def all_gather(name, vs):
    n = len(vs)

    def body(*refs):
        _ag_start(refs[:n], refs[n:2 * n], *refs[2 * n:])
        _ag_finish(refs[:n], refs[n:2 * n], *refs[2 * n:])

    hbm = pl.BlockSpec(memory_space=pl.ANY)
    return pl.pallas_call(
        body, name=name, out_shape=_ag_out_shape(vs), in_specs=[hbm] * n, out_specs=[hbm] * n,
        scratch_shapes=_a2a_scratch(n),
    )(*vs)


def _ag_out_shape(vs):
    return [jax.ShapeDtypeStruct((N_DEV,) + v.shape, v.dtype) for v in vs]


def _ag_copies(x_refs, out_refs, send_sems, recv_sems, local_sems):
    n = len(x_refs)
    x, y, c = _mesh_pos()
    me, sibling = (x, y, c), (x, y, 1 - c)
    chips = [(1 - x, y), (x, 1 - y), (1 - x, 1 - y)]

    def slot(a, px, py, pc):
        return out_refs[a].at[4 * px + 2 * py + pc]

    def copy(a, k, block, to, src=None):
        return pltpu.make_async_remote_copy(
            src_ref=slot(a, *block) if src is None else src, dst_ref=slot(a, *block),
            send_sem=send_sems.at[N_PEERS * a + k], recv_sem=recv_sems.at[N_PEERS * a + k],
            device_id=to, device_id_type=MESH)

    local = [pltpu.make_async_copy(x_refs[a], slot(a, *me), local_sems.at[a]) for a in range(n)]
    first = []
    for a in range(n):
        first.append(copy(a, 0, me, sibling, src=x_refs[a]))
        first += [copy(a, 1 + j, me, (*chip, c), src=x_refs[a]) for j, chip in enumerate(chips)]
    passed = [(copy(a, 1 + j, (*chip, c), me), copy(a, 4 + j, (*chip, c), sibling))
              for j, chip in enumerate(chips) for a in range(n)]
    from_sibling = []
    for a in range(n):
        from_sibling.append(copy(a, 0, sibling, me))
        from_sibling += [copy(a, 4 + j, (*chip, 1 - c), me) for j, chip in enumerate(chips)]
    return local, first, passed, from_sibling


def _ag_start(*refs):
    local, first, _, _ = _ag_copies(*refs)
    for cp in local + first:
        cp.start()


def _ag_finish(*refs):
    local, first, passed, from_sibling = _ag_copies(*refs)
    for arrived, hand_on in passed:
        arrived.wait_recv()
        hand_on.start()
    for cp in from_sibling:
        cp.wait_recv()
    for cp in first + [hand_on for _, hand_on in passed]:
        cp.wait_send()
    for cp in local:
        cp.wait()


def _a2a_scratch(n):
    return [pltpu.SemaphoreType.DMA((N_PEERS * n,)), pltpu.SemaphoreType.DMA((N_PEERS * n,)), pltpu.SemaphoreType.DMA((n,))]


def _a2a_copies(x_refs, out_refs, send_sems, recv_sems, local_sems):
    n = len(x_refs)
    x, y, c = _mesh_pos()
    me = 4 * x + 2 * y + c
    local = [pltpu.make_async_copy(x_refs[a].at[me], out_refs[a].at[me], local_sems.at[a]) for a in range(n)]
    remote = []
    for k in range(1, N_DEV):
        px, py, pc = x ^ ((k >> 2) & 1), y ^ ((k >> 1) & 1), c ^ (k & 1)
        for a in range(n):
            remote.append(pltpu.make_async_remote_copy(
                src_ref=x_refs[a].at[4 * px + 2 * py + pc], dst_ref=out_refs[a].at[me],
                send_sem=send_sems.at[N_PEERS * a + k - 1], recv_sem=recv_sems.at[N_PEERS * a + k - 1],
                device_id=(px, py, pc), device_id_type=MESH))
    return local, remote


def _a2a_start(local, remote):
    for cp in local + remote:
        cp.start()


def _a2a_wait(local, remote):
    for cp in remote:
        cp.wait_recv()
    for cp in remote:
        cp.wait_send()
    for cp in local:
        cp.wait()


class Hosted:
    def __init__(self, start=None, finish=None, arrays=(), out_shape=()):
        self.start, self.finish, self.arrays, self.out_shape = start, finish, list(arrays), list(out_shape)
        self.n = len(self.arrays)
        self.specs = [pl.BlockSpec(memory_space=pl.ANY)] * self.n
        self.scratch = _a2a_scratch(self.n) if self.n else []

    def steps(self, send_refs, recv_refs, sems, first_step, last_step):
        def begin():
            if self.n:
                pl.when(first_step)(lambda: self.start(send_refs, recv_refs, *sems))

        def end():
            if self.n:
                pl.when(last_step)(lambda: self.finish(send_refs, recv_refs, *sems))

        return begin, end


def hosted_all_to_all(vs):
    return Hosted(lambda *r: _a2a_start(*_a2a_copies(*r)), lambda *r: _a2a_wait(*_a2a_copies(*r)), vs,
                  [jax.ShapeDtypeStruct(v.shape, v.dtype) for v in vs])


def hosted_all_gather(vs):
    return Hosted(_ag_start, _ag_finish, vs, _ag_out_shape(vs))


def _taps8(w):
    return jnp.concatenate([w, jnp.zeros((8 - w.shape[0], w.shape[1]), w.dtype)], axis=0)


FIRST = ("w_in",)
LATE_WEIGHTS = ("w_out", "w_up", "w_down", "w_q_up", "w_kv_up")


def first_weights_to_internal(w_in):
    cq, ckv, kr, z, xbc, dt = jnp.split(w_in, np.cumsum(IN_SPLITS)[:-1].tolist(), axis=1)
    K = w_in.shape[0]

    def zeros(n):
        return jnp.zeros((K, n), w_in.dtype)

    w_in_p = jnp.concatenate([cq, zeros(KR_LANE), kr, zeros(LANE - KR_LANE - ROPE), ckv, zeros(OFF_Z - OFF_CKV - KV_RANK),
                              z, xbc, dt, zeros(WIN_P - OFF_DT - 2 * SSD_HEADS)], axis=1)
    return dict(w_in_p=w_in_p)


def late_weights_to_internal(w_out, w_up, w_down, w_q_up, w_kv_up):
    attn_rows = w_out[:N_HEADS * V_DIM].reshape(N_HEADS, V_DIM, -1)
    w_out_p = jnp.concatenate([jnp.pad(attn_rows, ((0, 0), (HEAD_BLOCK - V_DIM, 0), (0, 0))).reshape(QP, -1),
                               w_out[N_HEADS * V_DIM:]], axis=0)
    w_q_p = jnp.pad(w_q_up.reshape(Q_RANK, N_HEADS, NOPE + ROPE), ((0, 0), (0, 0), (0, HEAD_BLOCK - NOPE - ROPE))).reshape(Q_RANK, QP)
    return dict(w_out_p=w_out_p, w_up=glu_interleave(w_up), w_down=w_down, w_q_p=w_q_p, w_kv=w_kv_up)


def _q_grad(g_q_p):
    return g_q_p.reshape(Q_RANK, N_HEADS, HEAD_BLOCK)[:, :, :NOPE + ROPE].reshape(Q_RANK, -1)


def _out_grad(g_out_p):
    return jnp.concatenate([g_out_p[:QP].reshape(N_HEADS, HEAD_BLOCK, -1)[:, HEAD_BLOCK - V_DIM:].reshape(N_HEADS * V_DIM, -1),
                            g_out_p[QP:]], axis=0)


EARLY = ("w_out", "w_up", "w_down", "w_q_up", "w_kv_up")


def local_step(x, ctx, target, mod_x, mod_c, W, late_shards, V):
    nb, S, D = x.shape
    C = ctx.shape[1]
    T = C + S
    tr = _tile(math.gcd(C, S), 256, 8)
    tq = _tile(S, 256, 8)
    tc = 256
    cblk = C // tr
    m = [mod_x[:, i * D:(i + 1) * D][:, None, :] for i in range(N_MOD)]
    mc = [mod_c[:, i * D:(i + 1) * D] for i in range(2)]
    ssd_w8, ffn_w8 = _taps8(V["ssd_conv_w"]), _taps8(V["ffn_conv_w"])
    dexp = jnp.repeat(V["ssd_d"].reshape(-1), SSD_P).reshape(1, D_INNER)
    cosT, sinT = rope_tables(C, S)
    cosS, sinS = cosT[C:], sinT[C:]

    (h1x,) = rows_fwd("prenorm_x", fn_prenorm, nb, S // tr, tr, [(x, D, 0, 0)], [m[0], m[1]], [V["mix_pre_norm"]], [(D, BF16)])
    (h1c,) = rows_fwd("prenorm_c", fn_prenorm, nb, C // tr, tr, [(ctx, D, 0, 0)], [], [mc[0], mc[1], V["mix_pre_norm"]], [(D, BF16)])
    h1 = jnp.concatenate([h1c, h1x], axis=1).reshape(nb * T, D)
    u = matmul("in_proj", [(h1, W["w_in_p"])], "nn", F32).reshape(nb, T, WIN_P)
    xbc = ssd_conv_fwd(u, ssd_w8, V["ssd_conv_b"], C, tc)
    dt2, alog2, dtb2 = ssd_dt_inputs(u, V["ssd_a_log"], V["ssd_dt_bias"])
    y2, hin, *late = ssd_fwd(xbc, dt2, alog2, dtb2, C, hosted_all_gather(late_shards))
    W = dict(W, **late_weights_to_internal(*[_whole(s, n) for s, n in zip(late, LATE_WEIGHTS)]))
    y2 = y2.reshape(2 * nb, S, D_INNER)
    (qn,) = rows_fwd("q_norm", fn_rms, nb, S // tr, tr, [(u, Q_RANK, OFF_CQ // Q_RANK, cblk)], [], [V["q_norm"]], [(Q_RANK, BF16)])
    (kvn,) = rows_fwd("kv_norm", fn_rms, nb, T // tr, tr, [(u, KV_RANK, OFF_CKV // KV_RANK, 0)], [], [V["kv_norm"]], [(KV_RANK, BF16)])
    qn2, kvn2 = qn.reshape(nb * S, Q_RANK), kvn.reshape(nb * T, KV_RANK)
    q_raw = matmul("q_up", [(qn2, W["w_q_p"])], "nn", F32).reshape(nb, S, QP)
    kv = matmul("kv_up", [(kvn2, W["w_kv"])], "nn", BF16).reshape(nb, T, QP)
    cos_q, sin_q = cosS * Q_PRESCALE, sinS * Q_PRESCALE
    kr = rope_call("rope_k", u, LANE, OFF_KR // LANE, cosT, sinT, BF16, tr)
    o = attn_fwd(q_raw, kv, kr, cos_q, sin_q, tq)
    fin_rows = [(y2, D_INNER, 0, 0, 0), (y2, D_INNER, 0, 0, nb), (xbc, D_INNER, 0, cblk), (u, D_INNER, OFF_Z // D_INNER, cblk)]
    fin_gl = [dexp, V["ssd_norm"]]
    (ssd,) = rows_fwd("ssd_finish", fn_ssd_finish, nb, S // tr, tr, fin_rows, [], fin_gl, [(D_INNER, BF16)])
    o2, ssd2 = o.reshape(nb * S, QP), ssd.reshape(nb * S, D_INNER)
    mix = matmul("out_proj", [(o2, W["w_out_p"][:QP]), (ssd2, W["w_out_p"][QP:])], "nn", F32).reshape(nb, S, D)
    pm_rows = [(x, D, 0, 0), (mix, D, 0, 0)]
    pm_pb = [m[2], m[4], m[3]]
    pm_gl = [V["mix_post_norm"], V["ffn_pre_norm"]]
    x1, h2 = rows_fwd("postmix", fn_postmix, nb, S // tr, tr, pm_rows, pm_pb, pm_gl, [(D, F32), (D, BF16)])
    h22 = h2.reshape(nb * S, D)
    up = matmul("up_proj", [(h22, W["w_up"])], "nn", F32).reshape(nb, S, 2 * D_FF)
    act = glu_fwd(up, ffn_w8, V["ffn_conv_b"])
    act2 = act.reshape(nb * S, D_FF)
    ffn = matmul("down_proj", [(act2, W["w_down"])], "nn", F32).reshape(nb, S, D)
    dx1, dffn, dgate2, d_ffn_post, loss = final_call(x1, ffn, target, m[5], V["ffn_post_norm"], tr)

    dffn2 = dffn.reshape(nb * S, D)
    dact = matmul("down_dgrad", [(dffn2, W["w_down"])], "nt", BF16).reshape(nb, S, D_FF)
    g_down = matmul_tn("down_wgrad", act2, dffn2)
    dup, ffn_rows = glu_bwd(up, ffn_w8, V["ffn_conv_b"], dact)
    dup2 = dup.reshape(nb * S, 2 * D_FF)
    dh2 = matmul("up_dgrad", [(dup2, W["w_up"])], "nt", BF16).reshape(nb, S, D)
    g_up = matmul_tn("up_wgrad", h22, dup2)
    dx_a, dmix, dgate1, dscale2, dshift2, d_mix_post, d_ffn_pre = rows_bwd(
        "postmix_bwd", fn_postmix, nb, S // tr, tr, pm_rows, pm_pb, pm_gl,
        [(dx1, D, 0, 0), (dh2, D, 0, 0)], [(0, F32), (1, BF16)])
    dmix2 = dmix.reshape(nb * S, D)
    dcat = matmul("out_dgrad", [(dmix2, W["w_out_p"])], "nt", BF16).reshape(nb, S, QP + D_INNER)
    g_out_p = jnp.concatenate([matmul_tn("out_wgrad_attn", o2, dmix2), matmul_tn("out_wgrad_ssd", ssd2, dmix2)], axis=0)
    dy, dxs_direct, dz, d_dexp, d_ssd_norm = rows_bwd(
        "ssd_finish_bwd", fn_ssd_finish, nb, S // tr, tr, fin_rows, [], fin_gl,
        [(dcat, D_INNER, QP // D_INNER, 0)], [(0, F32), (2, F32), (3, BF16)])
    dq_pre, dkv, dkr = attn_bwd(q_raw, kv, kr, dcat, cos_q, sin_q, cosS, sinS, tq)
    dq_pre = dq_pre.reshape(nb * S, QP)
    dkr_pre = rope_call("rope_dk", dkr, LANE, 0, cosT, -sinT, BF16, tr)
    dkv2 = dkv.reshape(nb * T, QP)
    dqn = matmul("q_dgrad", [(dq_pre, W["w_q_p"])], "nt", F32).reshape(nb, S, Q_RANK)
    g_q_p = matmul_tn("q_wgrad", qn2, dq_pre)
    dkvn = matmul("kv_dgrad", [(dkv2, W["w_kv"])], "nt", F32).reshape(nb, T, KV_RANK)
    g_kv = matmul_tn("kv_wgrad", kvn2, dkv2)
    early_grads = (_out_grad(g_out_p), glu_deinterleave(g_up), g_down, _q_grad(g_q_p), g_kv)
    early = hosted_all_to_all([_per_device(g, n) for g, n in zip(early_grads, EARLY)])
    dxbc2, ddt2, ssd_stats, *received = ssd_bwd(xbc, dt2, alog2, dtb2, hin, dy, C, early)
    ddt_block = jnp.concatenate([ddt2[0][..., :SSD_HEADS], ddt2[1][..., :SSD_HEADS],
                                 jnp.zeros((nb, T, LANE - 2 * SSD_HEADS), F32)], axis=-1).astype(BF16)
    dxbc_raw, ssd_rows = ssd_conv_bwd(u, ssd_w8, V["ssd_conv_b"], dxbc2, dxs_direct, C, tc)
    dcq, d_q_norm = rows_bwd("q_norm_bwd", fn_rms, nb, S // tr, tr, [(u, Q_RANK, OFF_CQ // Q_RANK, cblk)], [], [V["q_norm"]],
                             [(dqn, Q_RANK, 0, 0)], [(0, BF16)])
    dckv, d_kv_norm = rows_bwd("kv_norm_bwd", fn_rms, nb, T // tr, tr, [(u, KV_RANK, OFF_CKV // KV_RANK, 0)], [], [V["kv_norm"]],
                               [(dkvn, KV_RANK, 0, 0)], [(0, BF16)])

    def ctx_rows(t):
        return jnp.pad(t, ((0, 0), (C, 0), (0, 0)))

    du = [("cq", ctx_rows(dcq), OFF_CQ, Q_RANK), ("kr", dkr_pre, OFF_KR, LANE), ("ckv", dckv, OFF_CKV, KV_RANK),
          ("z", ctx_rows(dz), OFF_Z, D_INNER), ("xbc", dxbc_raw, OFF_XBC, XBC), ("dt", ddt_block, OFF_DT, LANE)]
    du = [(name, t.reshape(nb * T, w), off, w) for (name, t, off, w) in du]
    g = {name: matmul_tn("in_wgrad_" + name, h1, t) for (name, t, _, _) in du}
    g_in = jnp.concatenate([g["cq"], g["ckv"], g["kr"][:, KR_LANE:KR_LANE + ROPE], g["z"], g["xbc"],
                            g["dt"][:, :2 * SSD_HEADS]], axis=1)
    dh1, received_in = matmul("in_dgrad", [(t, W["w_in_p"][:, off:off + w]) for (_, t, off, w) in du], "nt", BF16,
                              hosted=hosted_all_to_all([_per_device(g_in, "w_in").astype(BF16)]))
    dh1 = dh1.reshape(nb, T, D)

    def fn_prenorm_res(xv, shift, scale, g):
        return fn_prenorm(xv, shift, scale, g) + (xv,)

    grad_x, dshift1, dscale1, d_mix_pre_x = rows_bwd(
        "prenorm_x_bwd", fn_prenorm_res, nb, S // tr, tr, [(x, D, 0, 0)], [m[0], m[1]], [V["mix_pre_norm"]],
        [(dh1, D, 0, cblk), (dx_a, D, 0, 0)], [(0, F32)])
    dshift_c, dscale_c, d_mix_pre_c = rows_bwd(
        "prenorm_c_bwd", fn_prenorm, nb, C // tr, tr, [(ctx, D, 0, 0)], [], [mc[0], mc[1], V["mix_pre_norm"]],
        [(dh1, D, 0, 0)], [])

    dmod_x = jnp.concatenate([dshift1, dscale1, dgate1, dshift2, dscale2, dgate2], axis=-1).reshape(nb, N_MOD * D)
    dmod_c = jnp.concatenate([dshift_c, dscale_c, jnp.zeros((1, (N_MOD - 2) * D), F32)], axis=-1)
    gv = dict(
        mix_pre_norm=d_mix_pre_x + d_mix_pre_c, mix_post_norm=d_mix_post, q_norm=d_q_norm, kv_norm=d_kv_norm,
        ssd_conv_w=ssd_rows[:SSD_K], ssd_conv_b=ssd_rows[SSD_K:SSD_K + 1],
        ssd_a_log=jnp.concatenate([ssd_stats[0:1, :SSD_HEADS], ssd_stats[2:3, :SSD_HEADS]], axis=1),
        ssd_dt_bias=jnp.concatenate([ssd_stats[1:2, :SSD_HEADS], ssd_stats[3:4, :SSD_HEADS]], axis=1),
        ssd_d=jnp.sum(d_dexp.reshape(SSD_HEADS, SSD_P), axis=1).reshape(1, SSD_HEADS), ssd_norm=d_ssd_norm,
        ffn_pre_norm=d_ffn_pre, ffn_post_norm=d_ffn_post,
        ffn_conv_w=ffn_rows[:FFN_K], ffn_conv_b=ffn_rows[FFN_K:FFN_K + 1])
    return loss, grad_x, dmod_x, dmod_c, gv, dict(zip(EARLY, received), w_in=received_in)


WEIGHT_ORDER = ("c_ctx", "w_mod", "b_mod", "mix_pre_norm", "mix_post_norm", "w_in", "q_norm", "w_q_up", "kv_norm",
                "w_kv_up", "ssd_conv_w", "ssd_conv_b", "ssd_a_log", "ssd_dt_bias", "ssd_d", "ssd_norm", "w_out",
                "ffn_pre_norm", "ffn_post_norm", "w_up", "ffn_conv_w", "ffn_conv_b", "w_down")
MATRICES = ("w_in", "w_q_up", "w_kv_up", "w_out", "w_up", "w_down")
ROW_SHARDED = ("w_out", "w_down")
SMALL_SUMMED = ("c_ctx", "mix_pre_norm", "mix_post_norm", "q_norm", "kv_norm", "ssd_conv_w", "ssd_conv_b", "ssd_a_log",
                "ssd_dt_bias", "ssd_d", "ssd_norm", "ffn_pre_norm", "ffn_post_norm", "ffn_conv_w", "ffn_conv_b")
MOD_ROWS = 8


def _whole(shards, name):
    if name in ROW_SHARDED:
        return shards.reshape(-1, shards.shape[-1])
    return jnp.concatenate([shards[j] for j in range(N_DEV)], axis=1)


def _per_device(g, name):
    if name in ROW_SHARDED:
        return g.reshape(N_DEV, -1, g.shape[-1])
    return jnp.stack(jnp.split(g, N_DEV, axis=1))


def kernel(x, c, ctx, c_ctx, w_mod, b_mod, mix_pre_norm, mix_post_norm, w_in, q_norm, w_q_up, kv_norm, w_kv_up, ssd_conv_w, ssd_conv_b, ssd_a_log, ssd_dt_bias, ssd_d, ssd_norm, w_out, ffn_pre_norm, ffn_post_norm, w_up, ffn_conv_w, ffn_conv_b, w_down, loss_target, m_c_ctx, m_w_mod, m_b_mod, m_mix_pre_norm, m_mix_post_norm, m_w_in, m_q_norm, m_w_q_up, m_kv_norm, m_w_kv_up, m_ssd_conv_w, m_ssd_conv_b, m_ssd_a_log, m_ssd_dt_bias, m_ssd_d, m_ssd_norm, m_w_out, m_ffn_pre_norm, m_ffn_post_norm, m_w_up, m_ffn_conv_w, m_ffn_conv_b, m_w_down, v_c_ctx, v_w_mod, v_b_mod, v_mix_pre_norm, v_mix_post_norm, v_w_in, v_q_norm, v_w_q_up, v_kv_norm, v_w_kv_up, v_ssd_conv_w, v_ssd_conv_b, v_ssd_a_log, v_ssd_dt_bias, v_ssd_d, v_ssd_norm, v_w_out, v_ffn_pre_norm, v_ffn_post_norm, v_w_up, v_ffn_conv_w, v_ffn_conv_b, v_w_down):
    weights = dict(c_ctx=c_ctx, w_mod=w_mod, b_mod=b_mod, mix_pre_norm=mix_pre_norm, mix_post_norm=mix_post_norm, w_in=w_in, q_norm=q_norm, w_q_up=w_q_up, kv_norm=kv_norm, w_kv_up=w_kv_up, ssd_conv_w=ssd_conv_w, ssd_conv_b=ssd_conv_b, ssd_a_log=ssd_a_log, ssd_dt_bias=ssd_dt_bias, ssd_d=ssd_d, ssd_norm=ssd_norm, w_out=w_out, ffn_pre_norm=ffn_pre_norm, ffn_post_norm=ffn_post_norm, w_up=w_up, ffn_conv_w=ffn_conv_w, ffn_conv_b=ffn_conv_b, w_down=w_down)
    mom1 = dict(c_ctx=m_c_ctx, w_mod=m_w_mod, b_mod=m_b_mod, mix_pre_norm=m_mix_pre_norm, mix_post_norm=m_mix_post_norm, w_in=m_w_in, q_norm=m_q_norm, w_q_up=m_w_q_up, kv_norm=m_kv_norm, w_kv_up=m_w_kv_up, ssd_conv_w=m_ssd_conv_w, ssd_conv_b=m_ssd_conv_b, ssd_a_log=m_ssd_a_log, ssd_dt_bias=m_ssd_dt_bias, ssd_d=m_ssd_d, ssd_norm=m_ssd_norm, w_out=m_w_out, ffn_pre_norm=m_ffn_pre_norm, ffn_post_norm=m_ffn_post_norm, w_up=m_w_up, ffn_conv_w=m_ffn_conv_w, ffn_conv_b=m_ffn_conv_b, w_down=m_w_down)
    mom2 = dict(c_ctx=v_c_ctx, w_mod=v_w_mod, b_mod=v_b_mod, mix_pre_norm=v_mix_pre_norm, mix_post_norm=v_mix_post_norm, w_in=v_w_in, q_norm=v_q_norm, w_q_up=v_w_q_up, kv_norm=v_kv_norm, w_kv_up=v_w_kv_up, ssd_conv_w=v_ssd_conv_w, ssd_conv_b=v_ssd_conv_b, ssd_a_log=v_ssd_a_log, ssd_dt_bias=v_ssd_dt_bias, ssd_d=v_ssd_d, ssd_norm=v_ssd_norm, w_out=v_w_out, ffn_pre_norm=v_ffn_pre_norm, ffn_post_norm=v_ffn_post_norm, w_up=v_w_up, ffn_conv_w=v_ffn_conv_w, ffn_conv_b=v_ffn_conv_b, w_down=v_w_down)
    nb, S, D = x.shape
    me = 4 * lax.axis_index("x") + 2 * lax.axis_index("y") + lax.axis_index("c")

    *first, c_all, ssd_w_sh, ffn_w_sh = all_gather(
        "gather_first", [weights[n][0].astype(BF16) for n in FIRST] + [c, ssd_conv_w[0], ffn_conv_w[0]])
    W = first_weights_to_internal(*[_whole(s, n) for n, s in zip(FIRST, first)])
    late_shards = [weights[n][0].astype(BF16) for n in LATE_WEIGHTS]
    V = {n: weights[n].reshape(1, -1) for n in SMALL_SUMMED if n != "c_ctx"}
    V["ssd_conv_w"] = _whole(ssd_w_sh, "ssd_conv_w")
    V["ffn_conv_w"] = _whole(ffn_w_sh, "ffn_conv_w")

    n_all = N_DEV * nb
    mod_rows = -(-(n_all + 1) // 8) * 8
    c_pad = jnp.concatenate([c_all.reshape(n_all, D), c_ctx.reshape(1, D), jnp.zeros((mod_rows - n_all - 1, D), F32)], axis=0)
    mod_cols = w_mod.shape[2]
    b_mine = lax.dynamic_slice(b_mod, (0, me * mod_cols), (1, mod_cols))
    mod_part = matmul("mod_proj", [(c_pad, w_mod[0])], "nn", F32, bias=b_mine, silu_a=True)
    mod_all = _whole(all_gather("gather_mod", [mod_part])[0], "w_mod")
    mod_x = lax.dynamic_slice(mod_all, (me * nb, 0), (nb, mod_all.shape[1]))
    mod_c = mod_all[n_all:n_all + 1]

    loss, grad_x, dmod_x, dmod_c, gv, slots = local_step(x, ctx, loss_target, mod_x, mod_c, W, late_shards, V)

    dmod_mine = jnp.concatenate([dmod_x, dmod_c, jnp.zeros((MOD_ROWS - nb - 1, dmod_x.shape[1]), F32)], axis=0)
    dmod_all = all_gather("gather_dmod", [dmod_mine])[0]
    dmod_ctx = sum_slots("sum_dmod_ctx", dmod_all[:, nb:nb + 1].reshape(N_DEV, -1, LANE)).reshape(1, -1)
    dmod_full = jnp.concatenate([dmod_all[:, :nb].reshape(n_all, -1), dmod_ctx,
                                 jnp.zeros((mod_rows - n_all - 1, dmod_ctx.shape[1]), F32)], axis=0)
    (g_b_mod,) = ew_call("mod_bias_grad", lambda t: (jnp.sum(t, axis=0, keepdims=True),), [dmod_full], [((1, dmod_full.shape[1]), F32)])
    dmod_cols = lax.dynamic_slice(dmod_full, (0, me * mod_cols), (mod_rows, mod_cols))
    g_w_mod = matmul_tn("mod_wgrad", c_pad, dmod_cols, silu_a=True)
    dsilu_ctx = matmul("mod_dgrad_ctx", [(dmod_cols[n_all:n_all + 8], w_mod[0])], "nt", F32)[0:1]

    def silu_vjp(cc, ct):
        return (jax.vjp(_silu, cc)[1](ct)[0],)

    (g_c_ctx_part,) = ew_call("c_ctx_grad", silu_vjp, [c_ctx.reshape(1, D), dsilu_ctx], [((1, D), F32)])

    gv = dict(gv, c_ctx=g_c_ctx_part)
    small_parts = [loss] + [gv[n] for n in SMALL_SUMMED]
    small_sum = sum_slots("sum_small", all_gather("gather_small_grads", [_pack_rows(small_parts)])[0])
    summed = _unpack_rows(small_sum, [p.shape for p in small_parts])
    loss_out = summed[0][0, 0]
    grads = {n: g.reshape(weights[n].shape) if n not in ("ssd_conv_w", "ffn_conv_w") else g for n, g in zip(SMALL_SUMMED, summed[1:])}
    for n in ("ssd_conv_w", "ffn_conv_w"):
        cols = weights[n].shape[2]
        grads[n] = lax.dynamic_slice(grads[n], (0, me * cols), (grads[n].shape[0], cols)).reshape(weights[n].shape)
    grads["b_mod"] = g_b_mod.reshape(b_mod.shape)

    slots = dict(slots, w_mod=g_w_mod[None])
    delta, new_m, new_v = {}, {}, {}
    for n in MATRICES + ("w_mod",):
        g, d, mn, vn = adamw_matrix("adamw_" + n, weights[n][0], slots[n], mom1[n][0], mom2[n][0])
        grads[n], delta[n], new_m[n], new_v[n] = [t.reshape(weights[n].shape) for t in (g, d, mn, vn)]
    small = [n for n in WEIGHT_ORDER if n not in slots]

    def two_d(t):
        return t.reshape(-1, t.shape[-1])

    ds, ms, vs = adamw_small(*[[two_d(t[n]) for n in small] for t in (weights, grads, mom1, mom2)])
    for n, d, mn, vn in zip(small, ds, ms, vs):
        delta[n], new_m[n], new_v[n] = [t.reshape(weights[n].shape) for t in (d, mn, vn)]
    return (loss_out, grad_x, *[t[n] for t in (grads, delta, new_m, new_v) for n in WEIGHT_ORDER])
```

```python
import math

import jax
import jax.numpy as jnp
import numpy as np
from jax import lax
from jax.experimental import pallas as pl
from jax.experimental.pallas import tpu as pltpu

F32 = jnp.float32
BF16 = jnp.bfloat16
MESH = pl.DeviceIdType.MESH

D_MODEL = 1024
GRID_W = 64
N_HEADS = 16
NOPE = 64
ROPE = 32
V_DIM = 64
Q_RANK = 384
KV_RANK = 256
ROPE_THETA = 10000.0
ATTN_SCALE = (NOPE + ROPE) ** -0.5
SSD_HEADS = 16
SSD_P = 64
SSD_GROUPS = 2
SSD_N = 128
SSD_K = 5
CHUNK = 128
D_INNER = SSD_HEADS * SSD_P
GN = SSD_GROUPS * SSD_N
XBC = D_INNER + 2 * GN
D_FF = 2816
FFN_K = 3
N_MOD = 6
EPS = 1e-6
IN_SPLITS = (Q_RANK, KV_RANK, ROPE, D_INNER, XBC, 2 * SSD_HEADS)
IN_WIDTH = sum(IN_SPLITS)
N_DEV = 8

ADAM_LR = 0.001
ADAM_B1 = 0.9
ADAM_B2 = 0.999
ADAM_EPS = 1e-08
ADAM_WD = 0.01
ADAM_STEP = 10

LANE = 128
HEAD_BLOCK = 128
OFF_CQ = 0
OFF_KR = 384
OFF_CKV = 512
OFF_Z = 1024
OFF_XBC = 2048
OFF_DT = 3584
WIN_P = 3840
KR_LANE = 64
QP = N_HEADS * HEAD_BLOCK

VMEM_LIMIT_V7X = 56 * 1024 * 1024
NEG_BIG = -1e30


def _cparams(*sem):
    return pltpu.CompilerParams(dimension_semantics=sem, vmem_limit_bytes=VMEM_LIMIT_V7X)


def _tile(n, target, mult=128):
    if n <= target:
        return n
    t = (target // mult) * mult
    while t >= mult:
        if n % t == 0:
            return t
        t -= mult
    return n


def _silu(x):
    return x * jax.nn.sigmoid(x)


def _rms(x, g):
    return x * lax.rsqrt(jnp.mean(x * x, axis=-1, keepdims=True) + EPS) * g


WHOLE_K_WIDE = 2048


def matmul(name, pairs, mode, out_dtype, *, bias=None, silu_a=False, hosted=None):
    n_pairs = len(pairs)
    M = pairs[0][0].shape[0]
    N = pairs[0][1].shape[1] if mode == "nn" else pairs[0][1].shape[0]
    k_total = sum(a.shape[1] for a, _ in pairs)
    tm = _tile(M, 1024 if k_total <= WHOLE_K_WIDE else 512, 8)
    tn = _tile(N, 1408 if k_total <= WHOLE_K_WIDE else 1024)
    dims = (((1,), (0,)), ((), ())) if mode == "nn" else (((1,), (1,)), ((), ()))
    n_own = 2 * n_pairs + (bias is not None)
    n_ex = hosted.n if hosted else 0

    def body(*refs):
        o_ref = refs[n_own + n_ex]
        if hosted:
            j, i = pl.program_id(0), pl.program_id(1)
            begin_exchange, end_exchange = hosted.steps(
                refs[n_own:n_own + n_ex], refs[n_own + n_ex + 1:n_own + 2 * n_ex + 1], refs[n_own + 2 * n_ex + 1:],
                jnp.logical_and(j == 0, i == 0), jnp.logical_and(j == N // tn - 1, i == M // tm - 1))
            begin_exchange()
        acc = None
        for p in range(n_pairs):
            a = refs[2 * p][...]
            if silu_a:
                a = _silu(a.astype(F32))
            d = lax.dot_general(a.astype(BF16), refs[2 * p + 1][...].astype(BF16), dims, preferred_element_type=F32)
            acc = d if acc is None else acc + d
        if bias is not None:
            acc = acc + refs[2 * n_pairs][...]
        o_ref[...] = acc.astype(o_ref.dtype)
        if hosted:
            end_exchange()

    in_specs, args = [], []
    for a, b in pairs:
        K = a.shape[1]
        in_specs.append(pl.BlockSpec((tm, K), lambda j, i: (i, 0)))
        in_specs.append(pl.BlockSpec((K, tn), lambda j, i: (0, j)) if mode == "nn" else pl.BlockSpec((tn, K), lambda j, i: (j, 0)))
        args += [a, b]
    if bias is not None:
        in_specs.append(pl.BlockSpec((1, tn), lambda j, i: (0, j)))
        args.append(bias)
    out_spec = pl.BlockSpec((tm, tn), lambda j, i: (i, j))
    out_shape = jax.ShapeDtypeStruct((M, N), out_dtype)
    if not hosted:
        return pl.pallas_call(
            body, name=name, grid=(N // tn, M // tm), in_specs=in_specs, out_specs=out_spec, out_shape=out_shape,
            compiler_params=_cparams("arbitrary", "arbitrary"),
        )(*args)
    return pl.pallas_call(
        body, name=name, grid=(N // tn, M // tm), in_specs=in_specs + hosted.specs,
        out_specs=[out_spec] + hosted.specs, out_shape=[out_shape] + hosted.out_shape, scratch_shapes=hosted.scratch,
        compiler_params=_cparams("arbitrary", "arbitrary"),
    )(*args, *hosted.arrays)


def matmul_tn(name, a, b, out_dtype=F32, *, silu_a=False, tm=1408, tn=1408, tk=2048):
    R, M = a.shape
    N = b.shape[1]
    tm = _tile(M, tm)
    tn = _tile(N, tn)
    tk = _tile(R, tk, 8)
    nk = R // tk

    def body(a_ref, b_ref, o_ref, acc):
        k = pl.program_id(2)

        @pl.when(k == 0)
        def _():
            acc[...] = jnp.zeros_like(acc)

        x = a_ref[...]
        if silu_a:
            x = _silu(x.astype(F32))
        acc[...] += lax.dot_general(x.astype(BF16), b_ref[...].astype(BF16), (((0,), (0,)), ((), ())),
                                    preferred_element_type=F32)

        @pl.when(k == nk - 1)
        def _():
            o_ref[...] = acc[...].astype(o_ref.dtype)

    return pl.pallas_call(
        body, name=name, grid=(M // tm, N // tn, nk),
        in_specs=[pl.BlockSpec((tk, tm), lambda i, j, k: (k, i)), pl.BlockSpec((tk, tn), lambda i, j, k: (k, j))],
        out_specs=pl.BlockSpec((tm, tn), lambda i, j, k: (i, j)),
        out_shape=jax.ShapeDtypeStruct((M, N), out_dtype),
        scratch_shapes=[pltpu.VMEM((tm, tn), F32)],
        compiler_params=_cparams("arbitrary", "arbitrary", "arbitrary"),
    )(a, b)


def _row_specs(rin, pbin, glin, tr):
    specs = [pl.BlockSpec((1, tr, w), lambda b, i, cb=cb, ro=ro, bo=(e[4] if len(e) > 4 else 0): (b + bo, i + ro, cb))
             for e in rin for (_, w, cb, ro) in [e[:4]]]
    specs += [pl.BlockSpec((1, 1, a.shape[-1]), lambda b, i: (b, 0, 0)) for a in pbin]
    specs += [pl.BlockSpec((1, a.shape[-1]), lambda b, i: (0, 0)) for a in glin]
    return specs


def rows_fwd(name, fn, nb, nblk, tr, rin, pbin, glin, outs):
    nr, npb, ngl = len(rin), len(pbin), len(glin)
    n_in = nr + npb + ngl

    def body(*refs):
        args = [r[0].astype(F32) for r in refs[:nr + npb]] + [r[...] for r in refs[nr + npb:n_in]]
        res = fn(*args)
        for o, v in zip(refs[n_in:], res):
            o[0] = v.astype(o.dtype)

    return pl.pallas_call(
        body, name=name, grid=(nb, nblk), in_specs=_row_specs(rin, pbin, glin, tr),
        out_specs=[pl.BlockSpec((1, tr, w), lambda b, i: (b, i, 0)) for (w, _) in outs],
        out_shape=[jax.ShapeDtypeStruct((nb, nblk * tr, w), dt) for (w, dt) in outs],
        compiler_params=_cparams("arbitrary", "arbitrary"),
    )(*[e[0] for e in rin], *pbin, *glin)


def rows_bwd(name, fn, nb, nblk, tr, rin, pbin, glin, cts, want):
    nr, npb, ngl, nct = len(rin), len(pbin), len(glin), len(cts)
    n_in = nr + npb + ngl

    def body(*refs):
        b, i = pl.program_id(0), pl.program_id(1)
        args = [r[0].astype(F32) for r in refs[:nr + npb]] + [r[...] for r in refs[nr + npb:n_in]]
        ct = tuple(r[0].astype(F32) for r in refs[n_in:n_in + nct])
        _, vjp = jax.vjp(fn, *args)
        g = vjp(ct)
        orefs = refs[n_in + nct:]
        for o, (idx, _) in zip(orefs, want):
            o[0] = g[idx].astype(o.dtype)
        pb_refs = orefs[len(want):len(want) + npb]
        gl_refs = orefs[len(want) + npb:]

        @pl.when(i == 0)
        def _():
            for o, v in zip(pb_refs, g[nr:nr + npb]):
                o[0] = v

        @pl.when(i > 0)
        def _():
            for o, v in zip(pb_refs, g[nr:nr + npb]):
                o[0] += v

        first = jnp.logical_and(b == 0, i == 0)

        @pl.when(first)
        def _():
            for o, v in zip(gl_refs, g[nr + npb:]):
                o[...] = v

        @pl.when(jnp.logical_not(first))
        def _():
            for o, v in zip(gl_refs, g[nr + npb:]):
                o[...] += v

    out_specs = [pl.BlockSpec((1, tr, rin[idx][1]), lambda b, i: (b, i, 0)) for (idx, _) in want]
    out_shape = [jax.ShapeDtypeStruct((nb, nblk * tr, rin[idx][1]), dt) for (idx, dt) in want]
    out_specs += [pl.BlockSpec((1, 1, a.shape[-1]), lambda b, i: (b, 0, 0)) for a in pbin]
    out_shape += [jax.ShapeDtypeStruct((nb, 1, a.shape[-1]), F32) for a in pbin]
    out_specs += [pl.BlockSpec((1, a.shape[-1]), lambda b, i: (0, 0)) for a in glin]
    out_shape += [jax.ShapeDtypeStruct((1, a.shape[-1]), F32) for a in glin]
    return pl.pallas_call(
        body, name=name, grid=(nb, nblk),
        in_specs=_row_specs(rin, pbin, glin, tr) + _row_specs(cts, [], [], tr),
        out_specs=out_specs, out_shape=out_shape,
        compiler_params=_cparams("arbitrary", "arbitrary"),
    )(*[e[0] for e in rin], *pbin, *glin, *[e[0] for e in cts])


def ew_call(name, fn, ins, outs):
    def body(*refs):
        res = fn(*[r[...] for r in refs[:len(ins)]])
        for o, v in zip(refs[len(ins):], res):
            o[...] = v.astype(o.dtype)

    return pl.pallas_call(body, name=name, out_shape=[jax.ShapeDtypeStruct(s, dt) for (s, dt) in outs])(*ins)


def fn_prenorm(x, shift, scale, g):
    return (_rms(x, g) * (1.0 + scale) + shift,)


def fn_rms(x, g):
    return (_rms(x, g),)


def fn_ssd_finish(yf, yr, xs, z, dexp, nw):
    y = yf + yr + dexp * xs
    return (_rms(y * _silu(z), nw),)


def fn_postmix(x, mix, gate1, scale2, shift2, post_g, pre_g):
    x1 = x + gate1 * _rms(mix, post_g)
    h2 = _rms(x1, pre_g) * (1.0 + scale2) + shift2
    return x1, h2


def final_call(x1, ffn, target, gate2, post_g, tr):
    nb, S, D = x1.shape
    nblk = S // tr

    def body(x1_ref, f_ref, t_ref, g2_ref, pg_ref, dx1_ref, df_ref, dg2_ref, dpg_ref, loss_ref):
        b, i = pl.program_id(0), pl.program_id(1)
        tgt = t_ref[0]

        def lossfn(x1v, fv, g2, pg):
            e = x1v + g2 * _rms(fv, pg) - tgt
            return 0.5 * jnp.sum(jnp.mean(e * e, axis=-1, keepdims=True))

        val, (dx1, df, dg2, dpg) = jax.value_and_grad(lossfn, argnums=(0, 1, 2, 3))(
            x1_ref[0], f_ref[0].astype(F32), g2_ref[0], pg_ref[...])
        dx1_ref[0] = dx1
        df_ref[0] = df.astype(df_ref.dtype)
        lv = jnp.full((1, LANE), val, F32)

        @pl.when(i == 0)
        def _():
            dg2_ref[0] = dg2

        @pl.when(i > 0)
        def _():
            dg2_ref[0] += dg2

        first = jnp.logical_and(b == 0, i == 0)

        @pl.when(first)
        def _():
            dpg_ref[...] = dpg
            loss_ref[...] = lv

        @pl.when(jnp.logical_not(first))
        def _():
            dpg_ref[...] += dpg
            loss_ref[...] += lv

    row = pl.BlockSpec((1, tr, D), lambda b, i: (b, i, 0))
    pb = pl.BlockSpec((1, 1, D), lambda b, i: (b, 0, 0))
    gl = pl.BlockSpec((1, D), lambda b, i: (0, 0))
    return pl.pallas_call(
        body, name="loss_head", grid=(nb, nblk), in_specs=[row, row, row, pb, gl],
        out_specs=[row, row, pb, gl, pl.BlockSpec((1, LANE), lambda b, i: (0, 0))],
        out_shape=[jax.ShapeDtypeStruct((nb, S, D), F32), jax.ShapeDtypeStruct((nb, S, D), BF16),
                   jax.ShapeDtypeStruct((nb, 1, D), F32), jax.ShapeDtypeStruct((1, D), F32),
                   jax.ShapeDtypeStruct((1, LANE), F32)],
        compiler_params=_cparams("arbitrary", "arbitrary"),
    )(x1, ffn, target, gate2, post_g)


def _rotate_half(t):
    lane = lax.broadcasted_iota(jnp.int32, t.shape, 1)
    return jnp.where((lane & 15) < 8, -pltpu.roll(t, LANE - 8, 1), pltpu.roll(t, 8, 1))


def rope_call(name, x, width, colblk, cos, sin, out_dtype, tr):
    nb = x.shape[0]
    R = cos.shape[0]
    nblk = R // tr

    def body(x_ref, c_ref, s_ref, o_ref):
        c, s = c_ref[...], s_ref[...]
        for h in range(width // LANE):
            t = x_ref[0, :, h * LANE:(h + 1) * LANE].astype(F32)
            o_ref[0, :, h * LANE:(h + 1) * LANE] = (t * c + _rotate_half(t) * s).astype(o_ref.dtype)

    tab = pl.BlockSpec((tr, LANE), lambda b, i: (i, 0))
    return pl.pallas_call(
        body, name=name, grid=(nb, nblk),
        in_specs=[pl.BlockSpec((1, tr, width), lambda b, i: (b, i, colblk)), tab, tab],
        out_specs=pl.BlockSpec((1, tr, width), lambda b, i: (b, i, 0)),
        out_shape=jax.ShapeDtypeStruct((nb, R, width), out_dtype),
        compiler_params=_cparams("arbitrary", "arbitrary"),
    )(x, cos, sin)


def rope_tables(n_ctx, seq):
    n_rows = seq // GRID_W
    row = np.repeat(np.arange(n_rows), GRID_W).astype(np.float32)
    col = np.tile(np.arange(GRID_W), n_rows).astype(np.float32)
    axis_dim = ROPE // 2
    inv_freq = jnp.asarray(ROPE_THETA, F32) ** (-jnp.arange(0, axis_dim, 2, dtype=F32) / axis_dim)
    ang_r = jnp.asarray(row)[:, None] * inv_freq
    ang_c = jnp.asarray(col)[:, None] * inv_freq
    ang = jnp.concatenate([ang_r, ang_r, ang_c, ang_c], axis=-1)
    cos = jnp.ones((n_ctx + seq, LANE), F32).at[n_ctx:, KR_LANE:KR_LANE + ROPE].set(jnp.cos(ang))
    sin = jnp.zeros((n_ctx + seq, LANE), F32).at[n_ctx:, KR_LANE:KR_LANE + ROPE].set(jnp.sin(ang))
    return cos, sin


Q_PRESCALE = ATTN_SCALE * math.log2(math.e)


def _attn_weights(q, kc):
    s2 = lax.dot_general(q, kc, (((1,), (1,)), ((), ())), preferred_element_type=F32)
    e = jnp.exp2(s2 - jnp.max(s2, axis=1, keepdims=True))
    return e, 1.0 / jnp.sum(e, axis=1, keepdims=True)


def _key_block(kv, kr):
    lane = lax.broadcasted_iota(jnp.int32, kv.shape, 1)
    return jnp.where(lane < NOPE, kv, kr)


def _rotated_query(q_ref, cos_ref, sin_ref):
    t = q_ref[0].astype(F32)
    return (t * cos_ref[...] + _rotate_half(t) * sin_ref[...]).astype(BF16)


def attn_fwd(q_raw, kv, kr, cos_q, sin_q, tq):
    nb, S, _ = q_raw.shape
    T = kv.shape[1]

    def body(q_ref, kv_ref, kr_ref, c_ref, s_ref, o_ref):
        kvv = kv_ref[0]
        e, r = _attn_weights(_rotated_query(q_ref, c_ref, s_ref), _key_block(kvv, kr_ref[0]))
        o = lax.dot_general(e.astype(BF16), kvv, (((1,), (0,)), ((), ())), preferred_element_type=F32) * r
        lane = lax.broadcasted_iota(jnp.int32, o.shape, 1)
        o_ref[0] = jnp.where(lane >= NOPE, o, 0.0).astype(o_ref.dtype)

    return pl.pallas_call(
        body, name="attn_fwd", grid=(nb, N_HEADS, S // tq),
        in_specs=[pl.BlockSpec((1, tq, HEAD_BLOCK), lambda b, h, i: (b, i, h)),
                  pl.BlockSpec((1, T, HEAD_BLOCK), lambda b, h, i: (b, 0, h)),
                  pl.BlockSpec((1, T, HEAD_BLOCK), lambda b, h, i: (b, 0, 0)),
                  pl.BlockSpec((tq, LANE), lambda b, h, i: (i, 0)), pl.BlockSpec((tq, LANE), lambda b, h, i: (i, 0))],
        out_specs=pl.BlockSpec((1, tq, HEAD_BLOCK), lambda b, h, i: (b, i, h)),
        out_shape=jax.ShapeDtypeStruct((nb, S, QP), BF16),
        compiler_params=_cparams("arbitrary", "arbitrary", "arbitrary"),
    )(q_raw, kv, kr, cos_q, sin_q)


def attn_bwd(q_raw, kv, kr, do, cos_q, sin_q, cos, sin, tq):
    nb, S, _ = q_raw.shape
    T = kv.shape[1]

    def body(q_ref, kv_ref, kr_ref, do_ref, cq_ref, sq_ref, c_ref, s_ref, dq_ref, dkv_ref, dkr_ref):
        h, i = pl.program_id(1), pl.program_id(2)

        @pl.when(i == 0)
        def _():
            dkv_ref[...] = jnp.zeros_like(dkv_ref)

        @pl.when(jnp.logical_and(h == 0, i == 0))
        def _():
            dkr_ref[...] = jnp.zeros_like(dkr_ref)

        qv, kvv, dov = _rotated_query(q_ref, cq_ref, sq_ref), kv_ref[0], do_ref[0]
        kc = _key_block(kvv, kr_ref[0])
        e, r = _attn_weights(qv, kc)
        dor = (dov.astype(F32) * r).astype(BF16)
        dpr = lax.dot_general(dor, kvv, (((1,), (1,)), ((), ())), preferred_element_type=F32)
        ds = (e * (dpr - r * jnp.sum(dpr * e, axis=1, keepdims=True))).astype(BF16)
        dq = lax.dot_general(ds, kc, (((1,), (0,)), ((), ())), preferred_element_type=F32) * ATTN_SCALE
        dq_ref[0] = (dq * c_ref[...] - _rotate_half(dq) * s_ref[...]).astype(dq_ref.dtype)
        dkc = lax.dot_general(ds, qv, (((0,), (0,)), ((), ())), preferred_element_type=F32) * math.log(2.0)
        dv = lax.dot_general(e.astype(BF16), dor, (((0,), (0,)), ((), ())), preferred_element_type=F32)
        lane = lax.broadcasted_iota(jnp.int32, dkc.shape, 1)
        dkv_ref[0] += jnp.where(lane < NOPE, dkc, dv)
        dkr_ref[0] += jnp.where(lane >= NOPE, dkc, 0.0)

    qspec = pl.BlockSpec((1, tq, HEAD_BLOCK), lambda b, h, i: (b, i, h))
    kspec = pl.BlockSpec((1, T, HEAD_BLOCK), lambda b, h, i: (b, 0, h))
    rspec = pl.BlockSpec((1, T, HEAD_BLOCK), lambda b, h, i: (b, 0, 0))
    tab = pl.BlockSpec((tq, LANE), lambda b, h, i: (i, 0))
    return pl.pallas_call(
        body, name="attn_bwd", grid=(nb, N_HEADS, S // tq),
        in_specs=[qspec, kspec, rspec, qspec, tab, tab, tab, tab], out_specs=[qspec, kspec, rspec],
        out_shape=[jax.ShapeDtypeStruct((nb, S, QP), BF16), jax.ShapeDtypeStruct((nb, T, QP), F32),
                   jax.ShapeDtypeStruct((nb, T, HEAD_BLOCK), F32)],
        compiler_params=_cparams("arbitrary", "arbitrary", "arbitrary"),
    )(q_raw, kv, kr, do, cos_q, sin_q, cos, sin)


CONV_HALO = 8


def _segments(n, n_ctx):
    if n_ctx == 0:
        return [(0, n, CONV_HALO)]
    return [(0, n_ctx, CONV_HALO), (n_ctx, n - n_ctx, 2 * CONV_HALO + n_ctx)]


def _halo_scratch(n, n_ctx, tc):
    return pltpu.VMEM((n + CONV_HALO * (len(_segments(n, n_ctx)) + 1), tc), F32)


def _zero_halos(scr, segs):
    z = jnp.zeros((CONV_HALO, scr.shape[1]), scr.dtype)
    scr[0:CONV_HALO, :] = z
    for (_, rows, off) in segs:
        scr[off + rows:off + rows + CONV_HALO, :] = z


CONV_BLOCK_MAX = 256


def _conv_block(n, n_ctx):
    return _tile(math.gcd(n_ctx, n - n_ctx) if n_ctx else n, CONV_BLOCK_MAX, 8)


def _window(scr, off, r0, blk):
    return scr[pl.ds(pl.multiple_of(off - CONV_HALO + r0, 8), blk + 2 * CONV_HALO), :]


def _shifted(win, s):
    v = win if s == 0 else pltpu.roll(win, (-s) % win.shape[0], 0)
    return v[CONV_HALO:win.shape[0] - CONV_HALO]


def _tap_blocks(win, k, sign):
    return [_shifted(win, sign * (o - k // 2)) for o in range(k)]


def _taps(blocks, w):
    acc = None
    for o, blk in enumerate(blocks):
        t = w[o:o + 1, :] * blk
        acc = t if acc is None else acc + t
    return acc


def _tap_grads(xblocks, dpre):
    k = len(xblocks)
    sub8 = lax.broadcasted_iota(jnp.int32, (8, dpre.shape[1]), 0)
    out = jnp.where(sub8 == k, jnp.sum(dpre, axis=0, keepdims=True), 0.0)
    for o, blk in enumerate(xblocks):
        out = out + jnp.where(sub8 == o, jnp.sum(dpre * blk, axis=0, keepdims=True), 0.0)
    return out


def _row_blocks(rows, blk, fn, init=0):
    return lax.fori_loop(0, rows // blk, lambda i, c: fn(pl.multiple_of(i * blk, blk), c), init)


def _gelu(x):
    return 0.5 * x * (1.0 + lax.erf(x * (1.0 / math.sqrt(2.0))))


def _gelu_and_grad(x):
    cdf = 0.5 * (1.0 + lax.erf(x * (1.0 / math.sqrt(2.0))))
    return x * cdf, cdf + x * jnp.exp(-0.5 * x * x) * (1.0 / math.sqrt(2.0 * math.pi))


def ssd_conv_fwd(u, w8, bias, n_ctx, tc):
    nb, T, _ = u.shape
    cb0 = OFF_XBC // tc

    segs, blk = _segments(T, n_ctx), _conv_block(T, n_ctx)

    def body(x_ref, w_ref, b_ref, o_ref, xs):
        _zero_halos(xs, segs)
        for (start, rows, off) in segs:
            xs[off:off + rows, :] = x_ref[0, start:start + rows, :]
        w, bias_v = w_ref[...], b_ref[...]
        for (start, rows, off) in segs:
            def block(r0, carry, start=start, off=off):
                pre = bias_v + _taps(_tap_blocks(_window(xs, off, r0, blk), SSD_K, 1), w)
                o_ref[0, pl.ds(pl.multiple_of(start + r0, blk), blk), :] = _silu(pre)
                return carry

            _row_blocks(rows, blk, block)

    return pl.pallas_call(
        body, name="ssd_conv_fwd", grid=(nb, XBC // tc),
        in_specs=[pl.BlockSpec((1, T, tc), lambda b, j: (b, 0, cb0 + j)),
                  pl.BlockSpec((8, tc), lambda b, j: (0, j)), pl.BlockSpec((1, tc), lambda b, j: (0, j))],
        out_specs=pl.BlockSpec((1, T, tc), lambda b, j: (b, 0, j)),
        out_shape=jax.ShapeDtypeStruct((nb, T, XBC), F32),
        scratch_shapes=[_halo_scratch(T, n_ctx, tc)],
        compiler_params=_cparams("arbitrary", "arbitrary"),
    )(u, w8, bias)


def ssd_conv_bwd(u, w8, bias, dxbc, dxs_direct, n_ctx, tc):
    nb, T, _ = u.shape
    cb0 = OFF_XBC // tc
    n_direct = D_INNER // tc

    segs, blk = _segments(T, n_ctx), _conv_block(T, n_ctx)

    def body(x_ref, w_ref, b_ref, d0_ref, d1_ref, dd_ref, dx_ref, dw_ref, xs, ds):
        j, b = pl.program_id(0), pl.program_id(1)
        _zero_halos(xs, segs)
        _zero_halos(ds, segs)
        for (start, rows, off) in segs:
            xs[off:off + rows, :] = x_ref[0, start:start + rows, :]
        w, bias_v = w_ref[...], b_ref[...]
        has_direct = (j < n_direct).astype(F32)
        rows = jnp.zeros((8, tc), F32)
        for (start, n_rows, off) in segs:
            def block(r0, acc, start=start, off=off):
                xblocks = _tap_blocks(_window(xs, off, r0, blk), SSD_K, 1)
                pre = bias_v + _taps(xblocks, w)
                d = d0_ref[0, 0, pl.ds(pl.multiple_of(start + r0, blk), blk), :] + d1_ref[0, 0, pl.ds(pl.multiple_of(start + r0, blk), blk), :]
                if start == n_ctx:
                    d = d + dd_ref[0, pl.ds(r0, blk), :] * has_direct
                sg = jax.nn.sigmoid(pre)
                dpre = d * (sg * (1.0 + pre * (1.0 - sg)))
                ds[pl.ds(pl.multiple_of(off + r0, 8), blk), :] = dpre
                return acc + _tap_grads(xblocks, dpre)

            rows = _row_blocks(n_rows, blk, block, rows)
        for (start, n_rows, off) in segs:
            def block_dx(r0, carry, start=start, off=off):
                dx_ref[0, pl.ds(pl.multiple_of(start + r0, blk), blk), :] = _taps(_tap_blocks(_window(ds, off, r0, blk), SSD_K, -1), w).astype(dx_ref.dtype)
                return carry

            _row_blocks(n_rows, blk, block_dx)

        @pl.when(b == 0)
        def _():
            dw_ref[...] = rows

        @pl.when(b > 0)
        def _():
            dw_ref[...] += rows

    dspec0 = pl.BlockSpec((1, 1, T, tc), lambda j, b: (0, b, 0, j))
    dspec1 = pl.BlockSpec((1, 1, T, tc), lambda j, b: (1, b, 0, j))
    return pl.pallas_call(
        body, name="ssd_conv_bwd", grid=(XBC // tc, nb),
        in_specs=[pl.BlockSpec((1, T, tc), lambda j, b: (b, 0, cb0 + j)),
                  pl.BlockSpec((8, tc), lambda j, b: (0, j)), pl.BlockSpec((1, tc), lambda j, b: (0, j)),
                  dspec0, dspec1,
                  pl.BlockSpec((1, T - n_ctx, tc), lambda j, b: (b, 0, jnp.minimum(j, n_direct - 1)))],
        out_specs=[pl.BlockSpec((1, T, tc), lambda j, b: (b, 0, j)), pl.BlockSpec((8, tc), lambda j, b: (0, j))],
        out_shape=[jax.ShapeDtypeStruct((nb, T, XBC), BF16), jax.ShapeDtypeStruct((8, XBC), F32)],
        scratch_shapes=[_halo_scratch(T, n_ctx, tc), _halo_scratch(T, n_ctx, tc)],
        compiler_params=_cparams("arbitrary", "arbitrary"),
    )(u, w8, bias, dxbc, dxbc, dxs_direct)


GLU_TC = 256


def glu_interleave(w_up):
    blocks = []
    for j in range(D_FF // GLU_TC):
        blocks += [w_up[:, j * GLU_TC:(j + 1) * GLU_TC], w_up[:, D_FF + j * GLU_TC:D_FF + (j + 1) * GLU_TC]]
    return jnp.concatenate(blocks, axis=1)


def glu_deinterleave(g):
    nj = D_FF // GLU_TC
    gate = [g[:, 2 * j * GLU_TC:(2 * j + 1) * GLU_TC] for j in range(nj)]
    val = [g[:, (2 * j + 1) * GLU_TC:(2 * j + 2) * GLU_TC] for j in range(nj)]
    return jnp.concatenate(gate + val, axis=1)


def glu_fwd(up, w8, bias):
    nb, S, _ = up.shape
    tc = GLU_TC

    segs, blk = _segments(S, 0), _conv_block(S, 0)
    (_, _, off), = segs

    def body(u_ref, w_ref, b_ref, o_ref, xs):
        _zero_halos(xs, segs)
        xs[off:off + S, :] = u_ref[0, :, :tc]
        w, bias_v = w_ref[...], b_ref[...]

        def block(r0, carry):
            gc = bias_v + _taps(_tap_blocks(_window(xs, off, r0, blk), FFN_K, 1), w)
            o_ref[0, pl.ds(r0, blk), :] = (_gelu(gc) * u_ref[0, pl.ds(r0, blk), tc:]).astype(o_ref.dtype)
            return carry

        _row_blocks(S, blk, block)

    return pl.pallas_call(
        body, name="glu_fwd", grid=(nb, D_FF // tc),
        in_specs=[pl.BlockSpec((1, S, 2 * tc), lambda b, j: (b, 0, j)),
                  pl.BlockSpec((8, tc), lambda b, j: (0, j)), pl.BlockSpec((1, tc), lambda b, j: (0, j))],
        out_specs=pl.BlockSpec((1, S, tc), lambda b, j: (b, 0, j)),
        out_shape=jax.ShapeDtypeStruct((nb, S, D_FF), BF16),
        scratch_shapes=[_halo_scratch(S, 0, tc)],
        compiler_params=_cparams("arbitrary", "arbitrary"),
    )(up, w8, bias)


def glu_bwd(up, w8, bias, dact):
    nb, S, _ = up.shape
    tc = GLU_TC

    segs, blk = _segments(S, 0), _conv_block(S, 0)
    (_, _, off), = segs

    def body(u_ref, w_ref, b_ref, d_ref, du_ref, dw_ref, xs, ds):
        b = pl.program_id(1)
        _zero_halos(xs, segs)
        _zero_halos(ds, segs)
        xs[off:off + S, :] = u_ref[0, :, :tc]
        w, bias_v = w_ref[...], b_ref[...]

        def block(r0, acc):
            here = pl.ds(r0, blk)
            xblocks = _tap_blocks(_window(xs, off, r0, blk), FFN_K, 1)
            act, act_grad = _gelu_and_grad(bias_v + _taps(xblocks, w))
            d = d_ref[0, here, :].astype(F32)
            du_ref[0, here, tc:] = (d * act).astype(du_ref.dtype)
            dpre = d * u_ref[0, here, tc:] * act_grad
            ds[pl.ds(pl.multiple_of(off + r0, 8), blk), :] = dpre
            return acc + _tap_grads(xblocks, dpre)

        rows = _row_blocks(S, blk, block, jnp.zeros((8, tc), F32))

        def block_dx(r0, carry):
            du_ref[0, pl.ds(r0, blk), :tc] = _taps(_tap_blocks(_window(ds, off, r0, blk), FFN_K, -1), w).astype(du_ref.dtype)
            return carry

        _row_blocks(S, blk, block_dx)

        @pl.when(b == 0)
        def _():
            dw_ref[...] = rows

        @pl.when(b > 0)
        def _():
            dw_ref[...] += rows

    pair = pl.BlockSpec((1, S, 2 * tc), lambda j, b: (b, 0, j))
    return pl.pallas_call(
        body, name="glu_bwd", grid=(D_FF // tc, nb),
        in_specs=[pair, pl.BlockSpec((8, tc), lambda j, b: (0, j)), pl.BlockSpec((1, tc), lambda j, b: (0, j)),
                  pl.BlockSpec((1, S, tc), lambda j, b: (b, 0, j))],
        out_specs=[pair, pl.BlockSpec((8, tc), lambda j, b: (0, j))],
        out_shape=[jax.ShapeDtypeStruct((nb, S, 2 * D_FF), BF16), jax.ShapeDtypeStruct((8, D_FF), F32)],
        scratch_shapes=[_halo_scratch(S, 0, tc), _halo_scratch(S, 0, tc)],
        compiler_params=_cparams("arbitrary", "arbitrary"),
    )(up, w8, bias, dact)


def _chunk_of(d, k, n_cc, n_ch):
    rev = jnp.where(k < n_cc, n_cc - 1 - k, n_cc + n_ch - 1 - k)
    return jnp.where(d == 1, rev, k)


def _lane_pick(v, lane_iota, l):
    return jnp.sum(jnp.where(lane_iota == l, v, 0.0), axis=1, keepdims=True)


def head_spread_matrix():
    return (jnp.arange(LANE)[:, None] == (jnp.arange(D_INNER)[None, :] // SSD_P)).astype(BF16)


def _split_dot(x, e, dims):
    hi = x.astype(BF16)
    lo = (x - hi.astype(F32)).astype(BF16)
    return (lax.dot_general(hi, e, dims, preferred_element_type=F32)
            + lax.dot_general(lo, e, dims, preferred_element_type=F32))


def _spread(x, e):
    return _split_dot(x, e, (((1,), (0,)), ((), ())))


def _gather_heads(y, e):
    return _split_dot(y, e, (((1,), (1,)), ((), ())))


def _softplus(x):
    return jnp.maximum(x, 0.0) + jnp.log(1.0 + jnp.exp(-jnp.abs(x)))


def ssd_dt_inputs(u, a_log, dt_bias):
    pad = LANE - SSD_HEADS
    dt = u[..., OFF_DT:OFF_DT + 2 * SSD_HEADS]
    dt2 = jnp.stack([jnp.pad(dt[..., i * SSD_HEADS:(i + 1) * SSD_HEADS], ((0, 0), (0, 0), (0, pad))) for i in range(2)])

    def lanes(v):
        return jnp.pad(v.reshape(2, 1, SSD_HEADS), ((0, 0), (0, 0), (0, pad)))

    return dt2, lanes(a_log), lanes(dt_bias)


def _ssd_common(d, dt_raw, alog, dtb):
    Q = dt_raw.shape[0]
    row = lax.broadcasted_iota(jnp.int32, (Q, Q), 0)
    col = lax.broadcasted_iota(jnp.int32, (Q, Q), 1)
    rev = d == 1
    maskb = jnp.where(rev, row, col) <= jnp.where(rev, col, row)
    tri = maskb.astype(F32)
    A = -jnp.exp(alog)
    dtv = _softplus(dt_raw + dtb)
    a = dtv * A
    cum = lax.dot_general(tri, a, (((1,), (0,)), ((), ())), precision=lax.Precision.HIGHEST, preferred_element_type=F32)
    tot = jnp.sum(a, axis=0, keepdims=True)
    return maskb, tri, A, dtv, cum, tot


def ssd_fwd(xbc, dt2, alog2, dtb2, n_ctx, hosted):
    nb, T, _ = xbc.shape
    S = T - n_ctx
    n_ch, n_cc = T // CHUNK, n_ctx // CHUNK
    Q = CHUNK
    n_pairs = SSD_HEADS // 2
    n_ex = hosted.n
    n_in = 5

    def body(*refs):
        x_ref, dt_ref, al_ref, db_ref, e_ref = refs[:n_in]
        send_refs = refs[n_in:n_in + n_ex]
        y_ref, hin_ref = refs[n_in + n_ex:n_in + 2 + n_ex]
        recv_refs = refs[n_in + 2 + n_ex:n_in + 2 + 2 * n_ex]
        H, *sems = refs[n_in + 2 + 2 * n_ex:]
        d, k = pl.program_id(1), pl.program_id(2)
        first_step = jnp.logical_and(jnp.logical_and(pl.program_id(0) == 0, d == 0), k == 0)
        last_step = jnp.logical_and(jnp.logical_and(pl.program_id(0) == nb - 1, d == 1), k == n_ch - 1)
        begin_exchange, end_exchange = hosted.steps(send_refs, recv_refs, sems, first_step, last_step)
        begin_exchange()

        @pl.when(k == 0)
        def _():
            H[...] = jnp.zeros_like(H)

        maskb, tri, A, dtv, cum, tot = _ssd_common(d, dt_ref[0, 0], al_ref[0], db_ref[0])
        e = e_ref[...]
        cumT = cum.T
        cum_e, dt_e = _spread(cum, e), _spread(dtv, e)
        tot_e = _spread(jnp.broadcast_to(tot, (8, LANE)), e)[0:1]
        hin_ref[0, 0, 0] = H[...].astype(BF16)
        lane = lax.broadcasted_iota(jnp.int32, (Q, LANE), 1)
        lane1 = lax.broadcasted_iota(jnp.int32, (1, LANE), 1)
        subc = lax.broadcasted_iota(jnp.int32, (LANE, 1), 0)
        half = lane < SSD_P
        for g in range(SSD_GROUPS):
            Bg = x_ref[0, :, D_INNER + g * SSD_N:D_INNER + (g + 1) * SSD_N].astype(BF16)
            Cg = x_ref[0, :, D_INNER + GN + g * SSD_N:D_INNER + GN + (g + 1) * SSD_N].astype(BF16)
            Gm = lax.dot_general(Cg, Bg, (((1,), (1,)), ((), ())), preferred_element_type=F32)
            for pr in range(n_pairs // SSD_GROUPS):
                p = g * (n_pairs // SSD_GROUPS) + pr
                sc, dtp, totp = [t[:, p * LANE:(p + 1) * LANE] for t in (cum_e, dt_e, tot_e)]
                swapped = pltpu.roll(sc, SSD_P, 1)
                s0c, s1c = jnp.where(half, sc, swapped), jnp.where(half, swapped, sc)
                s0r, s1r = cumT[2 * p:2 * p + 1, :], cumT[2 * p + 1:2 * p + 2, :]
                tot0, tot1 = _lane_pick(tot, lane1, 2 * p), _lane_pick(tot, lane1, 2 * p + 1)
                M0 = (Gm * jnp.exp(jnp.where(maskb, s0c - s0r, NEG_BIG))).astype(BF16)
                M1 = (Gm * jnp.exp(jnp.where(maskb, s1c - s1r, NEG_BIG))).astype(BF16)
                xd = x_ref[0, :, p * LANE:(p + 1) * LANE] * dtp
                xdb = xd.astype(BF16)
                yd = jnp.where(half,
                               lax.dot_general(M0, xdb, (((1,), (0,)), ((), ())), preferred_element_type=F32),
                               lax.dot_general(M1, xdb, (((1,), (0,)), ((), ())), preferred_element_type=F32))
                Hp = H[p * LANE:(p + 1) * LANE, :]
                yo = lax.dot_general(Cg, Hp.astype(BF16), (((1,), (1,)), ((), ())), preferred_element_type=F32) * jnp.exp(sc)
                y_ref[0, 0, :, p * LANE:(p + 1) * LANE] = yd + yo
                xdw = (xd * jnp.exp(totp - sc)).astype(BF16)
                etot = jnp.exp(jnp.where(subc < SSD_P, tot0, tot1))
                H[p * LANE:(p + 1) * LANE, :] = Hp * etot + lax.dot_general(
                    xdw, Bg, (((0,), (0,)), ((), ())), preferred_element_type=F32)
        end_exchange()

    def ymap(b, d, k):
        return (d, b, _chunk_of(d, jnp.maximum(k, n_cc), n_cc, n_ch) - n_cc, 0)

    return pl.pallas_call(
        body, name="ssd_fwd", grid=(nb, 2, n_ch),
        in_specs=[pl.BlockSpec((1, Q, XBC), lambda b, d, k: (b, _chunk_of(d, k, n_cc, n_ch), 0)),
                  pl.BlockSpec((1, 1, Q, LANE), lambda b, d, k: (d, b, _chunk_of(d, k, n_cc, n_ch), 0)),
                  pl.BlockSpec((1, 1, LANE), lambda b, d, k: (d, 0, 0)), pl.BlockSpec((1, 1, LANE), lambda b, d, k: (d, 0, 0)),
                  pl.BlockSpec((LANE, D_INNER), lambda b, d, k: (0, 0))] + hosted.specs,
        out_specs=[pl.BlockSpec((1, 1, Q, D_INNER), ymap),
                   pl.BlockSpec((1, 1, 1, D_INNER, SSD_N), lambda b, d, k: (d, b, k, 0, 0))] + hosted.specs,
        out_shape=[jax.ShapeDtypeStruct((2, nb, S, D_INNER), F32),
                   jax.ShapeDtypeStruct((2, nb, n_ch, D_INNER, SSD_N), BF16)] + hosted.out_shape,
        scratch_shapes=[pltpu.VMEM((D_INNER, SSD_N), F32)] + hosted.scratch,
        compiler_params=_cparams("arbitrary", "arbitrary", "arbitrary"),
    )(xbc, dt2, alog2, dtb2, head_spread_matrix(), *hosted.arrays)


def ssd_bwd(xbc, dt2, alog2, dtb2, hin, dy, n_ctx, hosted):
    nb, T, _ = xbc.shape
    n_ex = hosted.n
    n_ch, n_cc = T // CHUNK, n_ctx // CHUNK
    n_in = 7
    Q = CHUNK
    n_pairs = SSD_HEADS // 2
    NT = (((1,), (1,)), ((), ()))
    NN = (((1,), (0,)), ((), ()))
    TN = (((0,), (0,)), ((), ()))

    def dot(a, b, dims):
        return lax.dot_general(a.astype(BF16), b.astype(BF16), dims, preferred_element_type=F32)

    def body(*refs):
        x_ref, dt_ref, al_ref, db_ref, e_ref, hin_ref, dy_ref = refs[:n_in]
        send_refs = refs[n_in:n_in + n_ex]
        dx_ref, ddt_ref, st_ref = refs[n_in + n_ex:n_in + 3 + n_ex]
        recv_refs = refs[n_in + 3 + n_ex:n_in + 3 + 2 * n_ex]
        dH, dce, dde, *sems = refs[n_in + 3 + 2 * n_ex:]
        d, kk = pl.program_id(1), pl.program_id(2)
        ks = n_ch - 1 - kk
        first_step = jnp.logical_and(jnp.logical_and(pl.program_id(0) == 0, d == 0), kk == 0)
        last_step = jnp.logical_and(jnp.logical_and(pl.program_id(0) == nb - 1, d == 1), kk == n_ch - 1)
        begin_exchange, end_exchange = hosted.steps(send_refs, recv_refs, sems, first_step, last_step)
        begin_exchange()

        @pl.when(kk == 0)
        def _():
            dH[...] = jnp.zeros_like(dH)

        @pl.when(jnp.logical_and(jnp.logical_and(pl.program_id(0) == 0, d == 0), kk == 0))
        def _():
            st_ref[...] = jnp.zeros_like(st_ref)

        dt_raw = dt_ref[0, 0]
        alog, dtb_v = al_ref[0], db_ref[0]
        maskb, tri, A, dtv, cum, tot = _ssd_common(d, dt_raw, alog, dtb_v)
        e = e_ref[...]
        cumT = cum.T
        cum_e, dt_e = _spread(cum, e), _spread(dtv, e)
        tot_e = _spread(jnp.broadcast_to(tot, (8, LANE)), e)[0:1]
        live = (ks >= n_cc).astype(F32)
        lane = lax.broadcasted_iota(jnp.int32, (Q, LANE), 1)
        lane1 = lax.broadcasted_iota(jnp.int32, (1, LANE), 1)
        sub = lax.broadcasted_iota(jnp.int32, (LANE, Q), 0)
        subc = lax.broadcasted_iota(jnp.int32, (LANE, 1), 0)
        half = lane < SSD_P
        halfc = subc < SSD_P
        pair_ones = ((lax.broadcasted_iota(jnp.int32, (2 * Q, LANE), 0) >= Q).astype(jnp.int32)
                     == (lax.broadcasted_iota(jnp.int32, (2 * Q, LANE), 1) >= SSD_P).astype(jnp.int32)).astype(BF16)
        dcumT = jnp.zeros((LANE, Q), F32)
        dtot = jnp.zeros((1, LANE), F32)
        dtot_parts = []
        for g in range(SSD_GROUPS):
            Bg = x_ref[0, :, D_INNER + g * SSD_N:D_INNER + (g + 1) * SSD_N].astype(BF16)
            Cg = x_ref[0, :, D_INNER + GN + g * SSD_N:D_INNER + GN + (g + 1) * SSD_N].astype(BF16)
            Gm = lax.dot_general(Cg, Bg, NT, preferred_element_type=F32)
            dG = jnp.zeros((Q, Q), F32)
            dC = jnp.zeros((Q, SSD_N), F32)
            dB = jnp.zeros((Q, SSD_N), F32)
            for pr in range(n_pairs // SSD_GROUPS):
                p = g * (n_pairs // SSD_GROUPS) + pr
                l0, l1 = 2 * p, 2 * p + 1
                sc, dtp, totp = [t[:, p * LANE:(p + 1) * LANE] for t in (cum_e, dt_e, tot_e)]
                swapped = pltpu.roll(sc, SSD_P, 1)
                s0c, s1c = jnp.where(half, sc, swapped), jnp.where(half, swapped, sc)
                s0r, s1r = cumT[l0:l0 + 1, :], cumT[l1:l1 + 1, :]
                tot0, tot1 = _lane_pick(tot, lane1, l0), _lane_pick(tot, lane1, l1)
                L0 = jnp.exp(jnp.where(maskb, s0c - s0r, NEG_BIG))
                L1 = jnp.exp(jnp.where(maskb, s1c - s1r, NEG_BIG))
                M0, M1 = Gm * L0, Gm * L1
                xs = x_ref[0, :, p * LANE:(p + 1) * LANE]
                xd = xs * dtp
                es = jnp.exp(sc)
                dte = jnp.exp(totp - sc)
                etot = jnp.exp(jnp.where(halfc, tot0, tot1))
                dyp = dy_ref[0, :, p * LANE:(p + 1) * LANE] * live
                Hp = hin_ref[0, 0, 0, p * LANE:(p + 1) * LANE, :]
                dHp = dH[p * LANE:(p + 1) * LANE, :]
                bdh = dot(Bg, dHp, NT)
                mtdy = dot(jnp.concatenate([M0, M1], axis=1), dyp, TN)
                dxd = jnp.where(half, mtdy[:Q], mtdy[Q:]) + bdh * dte
                dy0 = jnp.where(half, dyp, 0.0)
                dm = dot(jnp.concatenate([dy0, dyp - dy0], axis=0), xd, NT)
                dM0, dM1 = dm[:Q], dm[Q:]
                dG = dG + dM0 * L0 + dM1 * L1
                dyes = dyp * es
                xdw = xd * dte
                dC = dC + dot(dyes, Hp, NN)
                dB = dB + dot(xdw, dHp, NN)
                W0, W1 = dM0 * M0, dM1 * M1
                yoff = dot(Cg, Hp, NT) * es
                r_off = dyp * yoff
                r_st = xd * bdh * dte
                hh = jnp.sum(dHp * Hp.astype(F32), axis=1, keepdims=True) * etot
                w_rows = _split_dot(jnp.concatenate([W0, W1], axis=1), pair_ones, NN) * (1.0 / SSD_P)
                dce[:, p * LANE:(p + 1) * LANE] = r_off - r_st + w_rows
                dde[:, p * LANE:(p + 1) * LANE] = dxd * xs
                dtot_parts.append(jnp.sum(r_st, axis=0, keepdims=True))
                for (l, W, hselc) in ((l0, W0, halfc), (l1, W1, jnp.logical_not(halfc))):
                    row_g = -jnp.sum(W, axis=0, keepdims=True)
                    dcumT = dcumT + jnp.where(sub == l, row_g, 0.0)
                    dtot = dtot + jnp.where(lane1 == l, jnp.sum(jnp.where(hselc, hh, 0.0), axis=0, keepdims=True), 0.0)
                dx_ref[0, 0, :, p * LANE:(p + 1) * LANE] = dxd * dtp
                dH[p * LANE:(p + 1) * LANE, :] = dHp * etot + dot(dyes, Cg, TN)
            dx_ref[0, 0, :, D_INNER + g * SSD_N:D_INNER + (g + 1) * SSD_N] = dB + dot(dG, Cg, TN)
            dx_ref[0, 0, :, D_INNER + GN + g * SSD_N:D_INNER + GN + (g + 1) * SSD_N] = dC + dot(dG, Bg, NN)
        dcum_all = dcumT.T + _gather_heads(dce[...], e)
        dtot_e = jnp.broadcast_to(jnp.concatenate(dtot_parts, axis=1), (8, D_INNER))
        dtot = dtot + _gather_heads(dtot_e, e)[0:1]
        da = lax.dot_general(tri, dcum_all, TN, precision=lax.Precision.HIGHEST, preferred_element_type=F32) + dtot
        ddtv = _gather_heads(dde[...], e) + da * A
        ddt_raw = ddtv * jax.nn.sigmoid(dt_raw + dtb_v)
        ddt_ref[0, 0] = ddt_raw
        sub8 = lax.broadcasted_iota(jnp.int32, (8, LANE), 0)
        st_ref[...] += (jnp.where(sub8 == 2 * d, jnp.sum(da * dtv * A, axis=0, keepdims=True), 0.0)
                        + jnp.where(sub8 == 2 * d + 1, jnp.sum(ddt_raw, axis=0, keepdims=True), 0.0))
        end_exchange()

    def cmap(d, kk):
        return _chunk_of(d, n_ch - 1 - kk, n_cc, n_ch)

    def dymap(b, d, kk):
        return (b, _chunk_of(d, jnp.maximum(n_ch - 1 - kk, n_cc), n_cc, n_ch) - n_cc, 0)

    return pl.pallas_call(
        body, name="ssd_bwd", grid=(nb, 2, n_ch),
        in_specs=[pl.BlockSpec((1, Q, XBC), lambda b, d, kk: (b, cmap(d, kk), 0)),
                  pl.BlockSpec((1, 1, Q, LANE), lambda b, d, kk: (d, b, cmap(d, kk), 0)),
                  pl.BlockSpec((1, 1, LANE), lambda b, d, kk: (d, 0, 0)), pl.BlockSpec((1, 1, LANE), lambda b, d, kk: (d, 0, 0)),
                  pl.BlockSpec((LANE, D_INNER), lambda b, d, kk: (0, 0)),
                  pl.BlockSpec((1, 1, 1, D_INNER, SSD_N), lambda b, d, kk: (d, b, n_ch - 1 - kk, 0, 0)),
                  pl.BlockSpec((1, Q, D_INNER), dymap)] + hosted.specs,
        out_specs=[pl.BlockSpec((1, 1, Q, XBC), lambda b, d, kk: (d, b, cmap(d, kk), 0)),
                   pl.BlockSpec((1, 1, Q, LANE), lambda b, d, kk: (d, b, cmap(d, kk), 0)),
                   pl.BlockSpec((8, LANE), lambda b, d, kk: (0, 0))] + hosted.specs,
        out_shape=[jax.ShapeDtypeStruct((2, nb, T, XBC), F32), jax.ShapeDtypeStruct((2, nb, T, LANE), F32),
                   jax.ShapeDtypeStruct((8, LANE), F32)] + hosted.out_shape,
        scratch_shapes=[pltpu.VMEM((D_INNER, SSD_N), F32), pltpu.VMEM((Q, D_INNER), F32), pltpu.VMEM((Q, D_INNER), F32)] + hosted.scratch,
        compiler_params=_cparams("arbitrary", "arbitrary", "arbitrary"),
    )(xbc, dt2, alog2, dtb2, head_spread_matrix(), hin, dy, *hosted.arrays)


def _adamw(w, g, m, v):
    mn = ADAM_B1 * m + (1.0 - ADAM_B1) * g
    vn = ADAM_B2 * v + (1.0 - ADAM_B2) * jnp.square(g)
    m_hat = mn / (1.0 - ADAM_B1 ** ADAM_STEP)
    v_hat = vn / (1.0 - ADAM_B2 ** ADAM_STEP)
    return -ADAM_LR * (m_hat / (jnp.sqrt(v_hat) + ADAM_EPS) + ADAM_WD * w), mn, vn


def adamw_matrix(name, w, g_slots, m, v):
    K, n = w.shape
    s = g_slots.shape[0]
    tr = _tile(K, 256, 8)

    def body(w_ref, g_ref, m_ref, v_ref, go_ref, d_ref, mo_ref, vo_ref):
        g = g_ref[0].astype(F32)
        for j in range(1, s):
            g = g + g_ref[j].astype(F32)
        go_ref[...] = g
        d_ref[...], mo_ref[...], vo_ref[...] = _adamw(w_ref[...], g, m_ref[...], v_ref[...])

    spec = pl.BlockSpec((tr, n), lambda i: (i, 0))
    return pl.pallas_call(
        body, name=name, grid=(K // tr,),
        in_specs=[spec, pl.BlockSpec((s, tr, n), lambda i: (0, i, 0)), spec, spec], out_specs=[spec] * 4,
        out_shape=[jax.ShapeDtypeStruct((K, n), F32)] * 4,
        compiler_params=_cparams("arbitrary"),
    )(w, g_slots, m, v)


def adamw_small(ws, gs, ms, vs):
    n = len(ws)

    def body(*refs):
        for i in range(n):
            d, mn, vn = _adamw(refs[i][...], refs[n + i][...], refs[2 * n + i][...], refs[3 * n + i][...])
            refs[4 * n + i][...] = d
            refs[5 * n + i][...] = mn
            refs[6 * n + i][...] = vn

    shapes = [jax.ShapeDtypeStruct(w.shape, F32) for w in ws]
    out = pl.pallas_call(body, name="adamw_small", out_shape=shapes * 3)(*ws, *gs, *ms, *vs)
    return out[:n], out[n:2 * n], out[2 * n:]


def sum_slots(name, x):
    n = x.shape[0]

    def fn(t):
        acc = t[0]
        for j in range(1, n):
            acc = acc + t[j]
        return (acc,)

    return ew_call(name, fn, [x], [(x.shape[1:], F32)])[0]


def _pack_rows(parts):
    rows = []
    for p in parts:
        flat = p.reshape(1, -1)
        n = flat.shape[1]
        rows.append(jnp.pad(flat, ((0, 0), (0, -(-n // (8 * LANE)) * 8 * LANE - n))).reshape(-1, LANE))
    return jnp.concatenate(rows, axis=0)


def _unpack_rows(pack, shapes):
    out, r = [], 0
    for s in shapes:
        n = int(np.prod(s))
        nr = -(-n // (8 * LANE)) * 8
        out.append(pack[r:r + nr].reshape(1, -1)[:, :n].reshape(s))
        r += nr
    return out


def _mesh_pos():
    return lax.axis_index("x"), lax.axis_index("y"), lax.axis_index("c")


N_PEERS = N_DEV - 1


def all_gather(name, vs):
    n = len(vs)

    def body(*refs):
        _ag_start(refs[:n], refs[n:2 * n], *refs[2 * n:])
        _ag_finish(refs[:n], refs[n:2 * n], *refs[2 * n:])

    hbm = pl.BlockSpec(memory_space=pl.ANY)
    return pl.pallas_call(
        body, name=name, out_shape=_ag_out_shape(vs), in_specs=[hbm] * n, out_specs=[hbm] * n,
        scratch_shapes=_a2a_scratch(n),
    )(*vs)


def _ag_out_shape(vs):
    return [jax.ShapeDtypeStruct((N_DEV,) + v.shape, v.dtype) for v in vs]


def _ag_copies(x_refs, out_refs, send_sems, recv_sems, local_sems):
    n = len(x_refs)
    x, y, c = _mesh_pos()
    me, sibling = (x, y, c), (x, y, 1 - c)
    chips = [(1 - x, y), (x, 1 - y), (1 - x, 1 - y)]

    def slot(a, px, py, pc):
        return out_refs[a].at[4 * px + 2 * py + pc]

    def copy(a, k, block, to, src=None):
        return pltpu.make_async_remote_copy(
            src_ref=slot(a, *block) if src is None else src, dst_ref=slot(a, *block),
            send_sem=send_sems.at[N_PEERS * a + k], recv_sem=recv_sems.at[N_PEERS * a + k],
            device_id=to, device_id_type=MESH)

    local = [pltpu.make_async_copy(x_refs[a], slot(a, *me), local_sems.at[a]) for a in range(n)]
    first = []
    for a in range(n):
        first.append(copy(a, 0, me, sibling, src=x_refs[a]))
        first += [copy(a, 1 + j, me, (*chip, c), src=x_refs[a]) for j, chip in enumerate(chips)]
    passed = [(copy(a, 1 + j, (*chip, c), me), copy(a, 4 + j, (*chip, c), sibling))
              for j, chip in enumerate(chips) for a in range(n)]
    from_sibling = []
    for a in range(n):
        from_sibling.append(copy(a, 0, sibling, me))
        from_sibling += [copy(a, 4 + j, (*chip, 1 - c), me) for j, chip in enumerate(chips)]
    return local, first, passed, from_sibling


def _ag_start(*refs):
    local, first, _, _ = _ag_copies(*refs)
    for cp in local + first:
        cp.start()


def _ag_finish(*refs):
    local, first, passed, from_sibling = _ag_copies(*refs)
    for arrived, hand_on in passed:
        arrived.wait_recv()
        hand_on.start()
    for cp in from_sibling:
        cp.wait_recv()
    for cp in first + [hand_on for _, hand_on in passed]:
        cp.wait_send()
    for cp in local:
        cp.wait()


def _a2a_scratch(n):
    return [pltpu.SemaphoreType.DMA((N_PEERS * n,)), pltpu.SemaphoreType.DMA((N_PEERS * n,)), pltpu.SemaphoreType.DMA((n,))]


def _a2a_copies(x_refs, out_refs, send_sems, recv_sems, local_sems):
    n = len(x_refs)
    x, y, c = _mesh_pos()
    me = 4 * x + 2 * y + c
    local = [pltpu.make_async_copy(x_refs[a].at[me], out_refs[a].at[me], local_sems.at[a]) for a in range(n)]
    remote = []
    for k in range(1, N_DEV):
        px, py, pc = x ^ ((k >> 2) & 1), y ^ ((k >> 1) & 1), c ^ (k & 1)
        for a in range(n):
            remote.append(pltpu.make_async_remote_copy(
                src_ref=x_refs[a].at[4 * px + 2 * py + pc], dst_ref=out_refs[a].at[me],
                send_sem=send_sems.at[N_PEERS * a + k - 1], recv_sem=recv_sems.at[N_PEERS * a + k - 1],
                device_id=(px, py, pc), device_id_type=MESH))
    return local, remote


def _a2a_start(local, remote):
    for cp in local + remote:
        cp.start()


def _a2a_wait(local, remote):
    for cp in remote:
        cp.wait_recv()
    for cp in remote:
        cp.wait_send()
    for cp in local:
        cp.wait()


class Hosted:
    def __init__(self, start=None, finish=None, arrays=(), out_shape=()):
        self.start, self.finish, self.arrays, self.out_shape = start, finish, list(arrays), list(out_shape)
        self.n = len(self.arrays)
        self.specs = [pl.BlockSpec(memory_space=pl.ANY)] * self.n
        self.scratch = _a2a_scratch(self.n) if self.n else []

    def steps(self, send_refs, recv_refs, sems, first_step, last_step):
        def begin():
            if self.n:
                pl.when(first_step)(lambda: self.start(send_refs, recv_refs, *sems))

        def end():
            if self.n:
                pl.when(last_step)(lambda: self.finish(send_refs, recv_refs, *sems))

        return begin, end


def hosted_all_to_all(vs):
    return Hosted(lambda *r: _a2a_start(*_a2a_copies(*r)), lambda *r: _a2a_wait(*_a2a_copies(*r)), vs,
                  [jax.ShapeDtypeStruct(v.shape, v.dtype) for v in vs])


def hosted_all_gather(vs):
    return Hosted(_ag_start, _ag_finish, vs, _ag_out_shape(vs))


def _taps8(w):
    return jnp.concatenate([w, jnp.zeros((8 - w.shape[0], w.shape[1]), w.dtype)], axis=0)


FIRST = ("w_in",)
LATE_WEIGHTS = ("w_out", "w_up", "w_down", "w_q_up", "w_kv_up")


def first_weights_to_internal(w_in):
    cq, ckv, kr, z, xbc, dt = jnp.split(w_in, np.cumsum(IN_SPLITS)[:-1].tolist(), axis=1)
    K = w_in.shape[0]

    def zeros(n):
        return jnp.zeros((K, n), w_in.dtype)

    w_in_p = jnp.concatenate([cq, zeros(KR_LANE), kr, zeros(LANE - KR_LANE - ROPE), ckv, zeros(OFF_Z - OFF_CKV - KV_RANK),
                              z, xbc, dt, zeros(WIN_P - OFF_DT - 2 * SSD_HEADS)], axis=1)
    return dict(w_in_p=w_in_p)


def late_weights_to_internal(w_out, w_up, w_down, w_q_up, w_kv_up):
    attn_rows = w_out[:N_HEADS * V_DIM].reshape(N_HEADS, V_DIM, -1)
    w_out_p = jnp.concatenate([jnp.pad(attn_rows, ((0, 0), (HEAD_BLOCK - V_DIM, 0), (0, 0))).reshape(QP, -1),
                               w_out[N_HEADS * V_DIM:]], axis=0)
    w_q_p = jnp.pad(w_q_up.reshape(Q_RANK, N_HEADS, NOPE + ROPE), ((0, 0), (0, 0), (0, HEAD_BLOCK - NOPE - ROPE))).reshape(Q_RANK, QP)
    return dict(w_out_p=w_out_p, w_up=glu_interleave(w_up), w_down=w_down, w_q_p=w_q_p, w_kv=w_kv_up)


def _q_grad(g_q_p):
    return g_q_p.reshape(Q_RANK, N_HEADS, HEAD_BLOCK)[:, :, :NOPE + ROPE].reshape(Q_RANK, -1)


def _out_grad(g_out_p):
    return jnp.concatenate([g_out_p[:QP].reshape(N_HEADS, HEAD_BLOCK, -1)[:, HEAD_BLOCK - V_DIM:].reshape(N_HEADS * V_DIM, -1),
                            g_out_p[QP:]], axis=0)


EARLY = ("w_out", "w_up", "w_down", "w_q_up", "w_kv_up")


def local_step(x, ctx, target, mod_x, mod_c, W, late_shards, V):
    nb, S, D = x.shape
    C = ctx.shape[1]
    T = C + S
    tr = _tile(math.gcd(C, S), 256, 8)
    tq = _tile(S, 256, 8)
    tc = 256
    cblk = C // tr
    m = [mod_x[:, i * D:(i + 1) * D][:, None, :] for i in range(N_MOD)]
    mc = [mod_c[:, i * D:(i + 1) * D] for i in range(2)]
    ssd_w8, ffn_w8 = _taps8(V["ssd_conv_w"]), _taps8(V["ffn_conv_w"])
    dexp = jnp.repeat(V["ssd_d"].reshape(-1), SSD_P).reshape(1, D_INNER)
    cosT, sinT = rope_tables(C, S)
    cosS, sinS = cosT[C:], sinT[C:]

    (h1x,) = rows_fwd("prenorm_x", fn_prenorm, nb, S // tr, tr, [(x, D, 0, 0)], [m[0], m[1]], [V["mix_pre_norm"]], [(D, BF16)])
    (h1c,) = rows_fwd("prenorm_c", fn_prenorm, nb, C // tr, tr, [(ctx, D, 0, 0)], [], [mc[0], mc[1], V["mix_pre_norm"]], [(D, BF16)])
    h1 = jnp.concatenate([h1c, h1x], axis=1).reshape(nb * T, D)
    u = matmul("in_proj", [(h1, W["w_in_p"])], "nn", F32).reshape(nb, T, WIN_P)
    xbc = ssd_conv_fwd(u, ssd_w8, V["ssd_conv_b"], C, tc)
    dt2, alog2, dtb2 = ssd_dt_inputs(u, V["ssd_a_log"], V["ssd_dt_bias"])
    y2, hin, *late = ssd_fwd(xbc, dt2, alog2, dtb2, C, hosted_all_gather(late_shards))
    W = dict(W, **late_weights_to_internal(*[_whole(s, n) for s, n in zip(late, LATE_WEIGHTS)]))
    y2 = y2.reshape(2 * nb, S, D_INNER)
    (qn,) = rows_fwd("q_norm", fn_rms, nb, S // tr, tr, [(u, Q_RANK, OFF_CQ // Q_RANK, cblk)], [], [V["q_norm"]], [(Q_RANK, BF16)])
    (kvn,) = rows_fwd("kv_norm", fn_rms, nb, T // tr, tr, [(u, KV_RANK, OFF_CKV // KV_RANK, 0)], [], [V["kv_norm"]], [(KV_RANK, BF16)])
    qn2, kvn2 = qn.reshape(nb * S, Q_RANK), kvn.reshape(nb * T, KV_RANK)
    q_raw = matmul("q_up", [(qn2, W["w_q_p"])], "nn", F32).reshape(nb, S, QP)
    kv = matmul("kv_up", [(kvn2, W["w_kv"])], "nn", BF16).reshape(nb, T, QP)
    cos_q, sin_q = cosS * Q_PRESCALE, sinS * Q_PRESCALE
    kr = rope_call("rope_k", u, LANE, OFF_KR // LANE, cosT, sinT, BF16, tr)
    o = attn_fwd(q_raw, kv, kr, cos_q, sin_q, tq)
    fin_rows = [(y2, D_INNER, 0, 0, 0), (y2, D_INNER, 0, 0, nb), (xbc, D_INNER, 0, cblk), (u, D_INNER, OFF_Z // D_INNER, cblk)]
    fin_gl = [dexp, V["ssd_norm"]]
    (ssd,) = rows_fwd("ssd_finish", fn_ssd_finish, nb, S // tr, tr, fin_rows, [], fin_gl, [(D_INNER, BF16)])
    o2, ssd2 = o.reshape(nb * S, QP), ssd.reshape(nb * S, D_INNER)
    mix = matmul("out_proj", [(o2, W["w_out_p"][:QP]), (ssd2, W["w_out_p"][QP:])], "nn", F32).reshape(nb, S, D)
    pm_rows = [(x, D, 0, 0), (mix, D, 0, 0)]
    pm_pb = [m[2], m[4], m[3]]
    pm_gl = [V["mix_post_norm"], V["ffn_pre_norm"]]
    x1, h2 = rows_fwd("postmix", fn_postmix, nb, S // tr, tr, pm_rows, pm_pb, pm_gl, [(D, F32), (D, BF16)])
    h22 = h2.reshape(nb * S, D)
    up = matmul("up_proj", [(h22, W["w_up"])], "nn", F32).reshape(nb, S, 2 * D_FF)
    act = glu_fwd(up, ffn_w8, V["ffn_conv_b"])
    act2 = act.reshape(nb * S, D_FF)
    ffn = matmul("down_proj", [(act2, W["w_down"])], "nn", F32).reshape(nb, S, D)
    dx1, dffn, dgate2, d_ffn_post, loss = final_call(x1, ffn, target, m[5], V["ffn_post_norm"], tr)

    dffn2 = dffn.reshape(nb * S, D)
    dact = matmul("down_dgrad", [(dffn2, W["w_down"])], "nt", BF16).reshape(nb, S, D_FF)
    g_down = matmul_tn("down_wgrad", act2, dffn2)
    dup, ffn_rows = glu_bwd(up, ffn_w8, V["ffn_conv_b"], dact)
    dup2 = dup.reshape(nb * S, 2 * D_FF)
    dh2 = matmul("up_dgrad", [(dup2, W["w_up"])], "nt", BF16).reshape(nb, S, D)
    g_up = matmul_tn("up_wgrad", h22, dup2)
    dx_a, dmix, dgate1, dscale2, dshift2, d_mix_post, d_ffn_pre = rows_bwd(
        "postmix_bwd", fn_postmix, nb, S // tr, tr, pm_rows, pm_pb, pm_gl,
        [(dx1, D, 0, 0), (dh2, D, 0, 0)], [(0, F32), (1, BF16)])
    dmix2 = dmix.reshape(nb * S, D)
    dcat = matmul("out_dgrad", [(dmix2, W["w_out_p"])], "nt", BF16).reshape(nb, S, QP + D_INNER)
    g_out_p = jnp.concatenate([matmul_tn("out_wgrad_attn", o2, dmix2), matmul_tn("out_wgrad_ssd", ssd2, dmix2)], axis=0)
    dy, dxs_direct, dz, d_dexp, d_ssd_norm = rows_bwd(
        "ssd_finish_bwd", fn_ssd_finish, nb, S // tr, tr, fin_rows, [], fin_gl,
        [(dcat, D_INNER, QP // D_INNER, 0)], [(0, F32), (2, F32), (3, BF16)])
    dq_pre, dkv, dkr = attn_bwd(q_raw, kv, kr, dcat, cos_q, sin_q, cosS, sinS, tq)
    dq_pre = dq_pre.reshape(nb * S, QP)
    dkr_pre = rope_call("rope_dk", dkr, LANE, 0, cosT, -sinT, BF16, tr)
    dkv2 = dkv.reshape(nb * T, QP)
    dqn = matmul("q_dgrad", [(dq_pre, W["w_q_p"])], "nt", F32).reshape(nb, S, Q_RANK)
    g_q_p = matmul_tn("q_wgrad", qn2, dq_pre)
    dkvn = matmul("kv_dgrad", [(dkv2, W["w_kv"])], "nt", F32).reshape(nb, T, KV_RANK)
    g_kv = matmul_tn("kv_wgrad", kvn2, dkv2)
    early_grads = (_out_grad(g_out_p), glu_deinterleave(g_up), g_down, _q_grad(g_q_p), g_kv)
    early = hosted_all_to_all([_per_device(g, n) for g, n in zip(early_grads, EARLY)])
    dxbc2, ddt2, ssd_stats, *received = ssd_bwd(xbc, dt2, alog2, dtb2, hin, dy, C, early)
    ddt_block = jnp.concatenate([ddt2[0][..., :SSD_HEADS], ddt2[1][..., :SSD_HEADS],
                                 jnp.zeros((nb, T, LANE - 2 * SSD_HEADS), F32)], axis=-1).astype(BF16)
    dxbc_raw, ssd_rows = ssd_conv_bwd(u, ssd_w8, V["ssd_conv_b"], dxbc2, dxs_direct, C, tc)
    dcq, d_q_norm = rows_bwd("q_norm_bwd", fn_rms, nb, S // tr, tr, [(u, Q_RANK, OFF_CQ // Q_RANK, cblk)], [], [V["q_norm"]],
                             [(dqn, Q_RANK, 0, 0)], [(0, BF16)])
    dckv, d_kv_norm = rows_bwd("kv_norm_bwd", fn_rms, nb, T // tr, tr, [(u, KV_RANK, OFF_CKV // KV_RANK, 0)], [], [V["kv_norm"]],
                               [(dkvn, KV_RANK, 0, 0)], [(0, BF16)])

    def ctx_rows(t):
        return jnp.pad(t, ((0, 0), (C, 0), (0, 0)))

    du = [("cq", ctx_rows(dcq), OFF_CQ, Q_RANK), ("kr", dkr_pre, OFF_KR, LANE), ("ckv", dckv, OFF_CKV, KV_RANK),
          ("z", ctx_rows(dz), OFF_Z, D_INNER), ("xbc", dxbc_raw, OFF_XBC, XBC), ("dt", ddt_block, OFF_DT, LANE)]
    du = [(name, t.reshape(nb * T, w), off, w) for (name, t, off, w) in du]
    g = {name: matmul_tn("in_wgrad_" + name, h1, t) for (name, t, _, _) in du}
    g_in = jnp.concatenate([g["cq"], g["ckv"], g["kr"][:, KR_LANE:KR_LANE + ROPE], g["z"], g["xbc"],
                            g["dt"][:, :2 * SSD_HEADS]], axis=1)
    dh1, received_in = matmul("in_dgrad", [(t, W["w_in_p"][:, off:off + w]) for (_, t, off, w) in du], "nt", BF16,
                              hosted=hosted_all_to_all([_per_device(g_in, "w_in").astype(BF16)]))
    dh1 = dh1.reshape(nb, T, D)

    def fn_prenorm_res(xv, shift, scale, g):
        return fn_prenorm(xv, shift, scale, g) + (xv,)

    grad_x, dshift1, dscale1, d_mix_pre_x = rows_bwd(
        "prenorm_x_bwd", fn_prenorm_res, nb, S // tr, tr, [(x, D, 0, 0)], [m[0], m[1]], [V["mix_pre_norm"]],
        [(dh1, D, 0, cblk), (dx_a, D, 0, 0)], [(0, F32)])
    dshift_c, dscale_c, d_mix_pre_c = rows_bwd(
        "prenorm_c_bwd", fn_prenorm, nb, C // tr, tr, [(ctx, D, 0, 0)], [], [mc[0], mc[1], V["mix_pre_norm"]],
        [(dh1, D, 0, 0)], [])

    dmod_x = jnp.concatenate([dshift1, dscale1, dgate1, dshift2, dscale2, dgate2], axis=-1).reshape(nb, N_MOD * D)
    dmod_c = jnp.concatenate([dshift_c, dscale_c, jnp.zeros((1, (N_MOD - 2) * D), F32)], axis=-1)
    gv = dict(
        mix_pre_norm=d_mix_pre_x + d_mix_pre_c, mix_post_norm=d_mix_post, q_norm=d_q_norm, kv_norm=d_kv_norm,
        ssd_conv_w=ssd_rows[:SSD_K], ssd_conv_b=ssd_rows[SSD_K:SSD_K + 1],
        ssd_a_log=jnp.concatenate([ssd_stats[0:1, :SSD_HEADS], ssd_stats[2:3, :SSD_HEADS]], axis=1),
        ssd_dt_bias=jnp.concatenate([ssd_stats[1:2, :SSD_HEADS], ssd_stats[3:4, :SSD_HEADS]], axis=1),
        ssd_d=jnp.sum(d_dexp.reshape(SSD_HEADS, SSD_P), axis=1).reshape(1, SSD_HEADS), ssd_norm=d_ssd_norm,
        ffn_pre_norm=d_ffn_pre, ffn_post_norm=d_ffn_post,
        ffn_conv_w=ffn_rows[:FFN_K], ffn_conv_b=ffn_rows[FFN_K:FFN_K + 1])
    return loss, grad_x, dmod_x, dmod_c, gv, dict(zip(EARLY, received), w_in=received_in)


WEIGHT_ORDER = ("c_ctx", "w_mod", "b_mod", "mix_pre_norm", "mix_post_norm", "w_in", "q_norm", "w_q_up", "kv_norm",
                "w_kv_up", "ssd_conv_w", "ssd_conv_b", "ssd_a_log", "ssd_dt_bias", "ssd_d", "ssd_norm", "w_out",
                "ffn_pre_norm", "ffn_post_norm", "w_up", "ffn_conv_w", "ffn_conv_b", "w_down")
MATRICES = ("w_in", "w_q_up", "w_kv_up", "w_out", "w_up", "w_down")
ROW_SHARDED = ("w_out", "w_down")
SMALL_SUMMED = ("c_ctx", "mix_pre_norm", "mix_post_norm", "q_norm", "kv_norm", "ssd_conv_w", "ssd_conv_b", "ssd_a_log",
                "ssd_dt_bias", "ssd_d", "ssd_norm", "ffn_pre_norm", "ffn_post_norm", "ffn_conv_w", "ffn_conv_b")
MOD_ROWS = 8


def _whole(shards, name):
    if name in ROW_SHARDED:
        return shards.reshape(-1, shards.shape[-1])
    return jnp.concatenate([shards[j] for j in range(N_DEV)], axis=1)


def _per_device(g, name):
    if name in ROW_SHARDED:
        return g.reshape(N_DEV, -1, g.shape[-1])
    return jnp.stack(jnp.split(g, N_DEV, axis=1))


def kernel(x, c, ctx, c_ctx, w_mod, b_mod, mix_pre_norm, mix_post_norm, w_in, q_norm, w_q_up, kv_norm, w_kv_up, ssd_conv_w, ssd_conv_b, ssd_a_log, ssd_dt_bias, ssd_d, ssd_norm, w_out, ffn_pre_norm, ffn_post_norm, w_up, ffn_conv_w, ffn_conv_b, w_down, loss_target, m_c_ctx, m_w_mod, m_b_mod, m_mix_pre_norm, m_mix_post_norm, m_w_in, m_q_norm, m_w_q_up, m_kv_norm, m_w_kv_up, m_ssd_conv_w, m_ssd_conv_b, m_ssd_a_log, m_ssd_dt_bias, m_ssd_d, m_ssd_norm, m_w_out, m_ffn_pre_norm, m_ffn_post_norm, m_w_up, m_ffn_conv_w, m_ffn_conv_b, m_w_down, v_c_ctx, v_w_mod, v_b_mod, v_mix_pre_norm, v_mix_post_norm, v_w_in, v_q_norm, v_w_q_up, v_kv_norm, v_w_kv_up, v_ssd_conv_w, v_ssd_conv_b, v_ssd_a_log, v_ssd_dt_bias, v_ssd_d, v_ssd_norm, v_w_out, v_ffn_pre_norm, v_ffn_post_norm, v_w_up, v_ffn_conv_w, v_ffn_conv_b, v_w_down):
    weights = dict(c_ctx=c_ctx, w_mod=w_mod, b_mod=b_mod, mix_pre_norm=mix_pre_norm, mix_post_norm=mix_post_norm, w_in=w_in, q_norm=q_norm, w_q_up=w_q_up, kv_norm=kv_norm, w_kv_up=w_kv_up, ssd_conv_w=ssd_conv_w, ssd_conv_b=ssd_conv_b, ssd_a_log=ssd_a_log, ssd_dt_bias=ssd_dt_bias, ssd_d=ssd_d, ssd_norm=ssd_norm, w_out=w_out, ffn_pre_norm=ffn_pre_norm, ffn_post_norm=ffn_post_norm, w_up=w_up, ffn_conv_w=ffn_conv_w, ffn_conv_b=ffn_conv_b, w_down=w_down)
    mom1 = dict(c_ctx=m_c_ctx, w_mod=m_w_mod, b_mod=m_b_mod, mix_pre_norm=m_mix_pre_norm, mix_post_norm=m_mix_post_norm, w_in=m_w_in, q_norm=m_q_norm, w_q_up=m_w_q_up, kv_norm=m_kv_norm, w_kv_up=m_w_kv_up, ssd_conv_w=m_ssd_conv_w, ssd_conv_b=m_ssd_conv_b, ssd_a_log=m_ssd_a_log, ssd_dt_bias=m_ssd_dt_bias, ssd_d=m_ssd_d, ssd_norm=m_ssd_norm, w_out=m_w_out, ffn_pre_norm=m_ffn_pre_norm, ffn_post_norm=m_ffn_post_norm, w_up=m_w_up, ffn_conv_w=m_ffn_conv_w, ffn_conv_b=m_ffn_conv_b, w_down=m_w_down)
    mom2 = dict(c_ctx=v_c_ctx, w_mod=v_w_mod, b_mod=v_b_mod, mix_pre_norm=v_mix_pre_norm, mix_post_norm=v_mix_post_norm, w_in=v_w_in, q_norm=v_q_norm, w_q_up=v_w_q_up, kv_norm=v_kv_norm, w_kv_up=v_w_kv_up, ssd_conv_w=v_ssd_conv_w, ssd_conv_b=v_ssd_conv_b, ssd_a_log=v_ssd_a_log, ssd_dt_bias=v_ssd_dt_bias, ssd_d=v_ssd_d, ssd_norm=v_ssd_norm, w_out=v_w_out, ffn_pre_norm=v_ffn_pre_norm, ffn_post_norm=v_ffn_post_norm, w_up=v_w_up, ffn_conv_w=v_ffn_conv_w, ffn_conv_b=v_ffn_conv_b, w_down=v_w_down)
    nb, S, D = x.shape
    me = 4 * lax.axis_index("x") + 2 * lax.axis_index("y") + lax.axis_index("c")

    *first, c_all, ssd_w_sh, ffn_w_sh = all_gather(
        "gather_first", [weights[n][0].astype(BF16) for n in FIRST] + [c, ssd_conv_w[0], ffn_conv_w[0]])
    W = first_weights_to_internal(*[_whole(s, n) for n, s in zip(FIRST, first)])
    late_shards = [weights[n][0].astype(BF16) for n in LATE_WEIGHTS]
    V = {n: weights[n].reshape(1, -1) for n in SMALL_SUMMED if n != "c_ctx"}
    V["ssd_conv_w"] = _whole(ssd_w_sh, "ssd_conv_w")
    V["ffn_conv_w"] = _whole(ffn_w_sh, "ffn_conv_w")

    n_all = N_DEV * nb
    mod_rows = -(-(n_all + 1) // 8) * 8
    c_pad = jnp.concatenate([c_all.reshape(n_all, D), c_ctx.reshape(1, D), jnp.zeros((mod_rows - n_all - 1, D), F32)], axis=0)
    mod_cols = w_mod.shape[2]
    b_mine = lax.dynamic_slice(b_mod, (0, me * mod_cols), (1, mod_cols))
    mod_part = matmul("mod_proj", [(c_pad, w_mod[0])], "nn", F32, bias=b_mine, silu_a=True)
    mod_all = _whole(all_gather("gather_mod", [mod_part])[0], "w_mod")
    mod_x = lax.dynamic_slice(mod_all, (me * nb, 0), (nb, mod_all.shape[1]))
    mod_c = mod_all[n_all:n_all + 1]

    loss, grad_x, dmod_x, dmod_c, gv, slots = local_step(x, ctx, loss_target, mod_x, mod_c, W, late_shards, V)

    dmod_mine = jnp.concatenate([dmod_x, dmod_c, jnp.zeros((MOD_ROWS - nb - 1, dmod_x.shape[1]), F32)], axis=0)
    dmod_all = all_gather("gather_dmod", [dmod_mine])[0]
    dmod_ctx = sum_slots("sum_dmod_ctx", dmod_all[:, nb:nb + 1].reshape(N_DEV, -1, LANE)).reshape(1, -1)
    dmod_full = jnp.concatenate([dmod_all[:, :nb].reshape(n_all, -1), dmod_ctx,
                                 jnp.zeros((mod_rows - n_all - 1, dmod_ctx.shape[1]), F32)], axis=0)
    (g_b_mod,) = ew_call("mod_bias_grad", lambda t: (jnp.sum(t, axis=0, keepdims=True),), [dmod_full], [((1, dmod_full.shape[1]), F32)])
    dmod_cols = lax.dynamic_slice(dmod_full, (0, me * mod_cols), (mod_rows, mod_cols))
    g_w_mod = matmul_tn("mod_wgrad", c_pad, dmod_cols, silu_a=True)
    dsilu_ctx = matmul("mod_dgrad_ctx", [(dmod_cols[n_all:n_all + 8], w_mod[0])], "nt", F32)[0:1]

    def silu_vjp(cc, ct):
        return (jax.vjp(_silu, cc)[1](ct)[0],)

    (g_c_ctx_part,) = ew_call("c_ctx_grad", silu_vjp, [c_ctx.reshape(1, D), dsilu_ctx], [((1, D), F32)])

    gv = dict(gv, c_ctx=g_c_ctx_part)
    small_parts = [loss] + [gv[n] for n in SMALL_SUMMED]
    small_sum = sum_slots("sum_small", all_gather("gather_small_grads", [_pack_rows(small_parts)])[0])
    summed = _unpack_rows(small_sum, [p.shape for p in small_parts])
    loss_out = summed[0][0, 0]
    grads = {n: g.reshape(weights[n].shape) if n not in ("ssd_conv_w", "ffn_conv_w") else g for n, g in zip(SMALL_SUMMED, summed[1:])}
    for n in ("ssd_conv_w", "ffn_conv_w"):
        cols = weights[n].shape[2]
        grads[n] = lax.dynamic_slice(grads[n], (0, me * cols), (grads[n].shape[0], cols)).reshape(weights[n].shape)
    grads["b_mod"] = g_b_mod.reshape(b_mod.shape)

    slots = dict(slots, w_mod=g_w_mod[None])
    delta, new_m, new_v = {}, {}, {}
    for n in MATRICES + ("w_mod",):
        g, d, mn, vn = adamw_matrix("adamw_" + n, weights[n][0], slots[n], mom1[n][0], mom2[n][0])
        grads[n], delta[n], new_m[n], new_v[n] = [t.reshape(weights[n].shape) for t in (g, d, mn, vn)]
    small = [n for n in WEIGHT_ORDER if n not in slots]

    def two_d(t):
        return t.reshape(-1, t.shape[-1])

    ds, ms, vs = adamw_small(*[[two_d(t[n]) for n in small] for t in (weights, grads, mom1, mom2)])
    for n, d, mn, vn in zip(small, ds, ms, vs):
        delta[n], new_m[n], new_v[n] = [t.reshape(weights[n].shape) for t in (d, mn, vn)]
    return (loss_out, grad_x, *[t[n] for t in (grads, delta, new_m, new_v) for n in WEIGHT_ORDER])
```

```python
import math

import jax
import jax.numpy as jnp
import numpy as np
from jax import lax
from jax.experimental import pallas as pl
from jax.experimental.pallas import tpu as pltpu

F32 = jnp.float32
BF16 = jnp.bfloat16
MESH = pl.DeviceIdType.MESH

D_MODEL = 1024
GRID_W = 64
N_HEADS = 16
NOPE = 64
ROPE = 32
V_DIM = 64
Q_RANK = 384
KV_RANK = 256
ROPE_THETA = 10000.0
ATTN_SCALE = (NOPE + ROPE) ** -0.5
SSD_HEADS = 16
SSD_P = 64
SSD_GROUPS = 2
SSD_N = 128
SSD_K = 5
CHUNK = 128
D_INNER = SSD_HEADS * SSD_P
GN = SSD_GROUPS * SSD_N
XBC = D_INNER + 2 * GN
D_FF = 2816
FFN_K = 3
N_MOD = 6
EPS = 1e-6
IN_SPLITS = (Q_RANK, KV_RANK, ROPE, D_INNER, XBC, 2 * SSD_HEADS)
IN_WIDTH = sum(IN_SPLITS)
N_DEV = 8

ADAM_LR = 0.001
ADAM_B1 = 0.9
ADAM_B2 = 0.999
ADAM_EPS = 1e-08
ADAM_WD = 0.01
ADAM_STEP = 10

LANE = 128
HEAD_BLOCK = 128
OFF_CQ = 0
OFF_KR = 384
OFF_CKV = 512
OFF_Z = 1024
OFF_XBC = 2048
OFF_DT = 3584
WIN_P = 3840
KR_LANE = 64
QP = N_HEADS * HEAD_BLOCK

VMEM_LIMIT_V7X = 56 * 1024 * 1024
NEG_BIG = -1e30


def _cparams(*sem):
    return pltpu.CompilerParams(dimension_semantics=sem, vmem_limit_bytes=VMEM_LIMIT_V7X)


def _tile(n, target, mult=128):
    if n <= target:
        return n
    t = (target // mult) * mult
    while t >= mult:
        if n % t == 0:
            return t
        t -= mult
    return n


def _silu(x):
    return x * jax.nn.sigmoid(x)


def _rms(x, g):
    return x * lax.rsqrt(jnp.mean(x * x, axis=-1, keepdims=True) + EPS) * g


WHOLE_K_WIDE = 2048


def matmul(name, pairs, mode, out_dtype, *, bias=None, silu_a=False, hosted=None):
    n_pairs = len(pairs)
    M = pairs[0][0].shape[0]
    N = pairs[0][1].shape[1] if mode == "nn" else pairs[0][1].shape[0]
    k_total = sum(a.shape[1] for a, _ in pairs)
    tm = _tile(M, 1024 if k_total <= WHOLE_K_WIDE else 512, 8)
    tn = _tile(N, 2816 if k_total <= WHOLE_K_WIDE else 1024)
    dims = (((1,), (0,)), ((), ())) if mode == "nn" else (((1,), (1,)), ((), ()))
    n_own = 2 * n_pairs + (bias is not None)
    n_ex = hosted.n if hosted else 0

    def body(*refs):
        o_ref = refs[n_own + n_ex]
        if hosted:
            j, i = pl.program_id(0), pl.program_id(1)
            begin_exchange, end_exchange = hosted.steps(
                refs[n_own:n_own + n_ex], refs[n_own + n_ex + 1:n_own + 2 * n_ex + 1], refs[n_own + 2 * n_ex + 1:],
                jnp.logical_and(j == 0, i == 0), jnp.logical_and(j == N // tn - 1, i == M // tm - 1))
            begin_exchange()
        acc = None
        for p in range(n_pairs):
            a = refs[2 * p][...]
            if silu_a:
                a = _silu(a.astype(F32))
            d = lax.dot_general(a.astype(BF16), refs[2 * p + 1][...].astype(BF16), dims, preferred_element_type=F32)
            acc = d if acc is None else acc + d
        if bias is not None:
            acc = acc + refs[2 * n_pairs][...]
        o_ref[...] = acc.astype(o_ref.dtype)
        if hosted:
            end_exchange()

    in_specs, args = [], []
    for a, b in pairs:
        K = a.shape[1]
        in_specs.append(pl.BlockSpec((tm, K), lambda j, i: (i, 0)))
        in_specs.append(pl.BlockSpec((K, tn), lambda j, i: (0, j)) if mode == "nn" else pl.BlockSpec((tn, K), lambda j, i: (j, 0)))
        args += [a, b]
    if bias is not None:
        in_specs.append(pl.BlockSpec((1, tn), lambda j, i: (0, j)))
        args.append(bias)
    out_spec = pl.BlockSpec((tm, tn), lambda j, i: (i, j))
    out_shape = jax.ShapeDtypeStruct((M, N), out_dtype)
    if not hosted:
        return pl.pallas_call(
            body, name=name, grid=(N // tn, M // tm), in_specs=in_specs, out_specs=out_spec, out_shape=out_shape,
            compiler_params=_cparams("arbitrary", "arbitrary"),
        )(*args)
    return pl.pallas_call(
        body, name=name, grid=(N // tn, M // tm), in_specs=in_specs + hosted.specs,
        out_specs=[out_spec] + hosted.specs, out_shape=[out_shape] + hosted.out_shape, scratch_shapes=hosted.scratch,
        compiler_params=_cparams("arbitrary", "arbitrary"),
    )(*args, *hosted.arrays)


def matmul_tn(name, a, b, out_dtype=F32, *, silu_a=False, tm=1408, tn=1408, tk=2048):
    R, M = a.shape
    N = b.shape[1]
    tm = _tile(M, tm)
    tn = _tile(N, tn)
    tk = _tile(R, tk, 8)
    nk = R // tk

    def body(a_ref, b_ref, o_ref, acc):
        k = pl.program_id(2)

        @pl.when(k == 0)
        def _():
            acc[...] = jnp.zeros_like(acc)

        x = a_ref[...]
        if silu_a:
            x = _silu(x.astype(F32))
        acc[...] += lax.dot_general(x.astype(BF16), b_ref[...].astype(BF16), (((0,), (0,)), ((), ())),
                                    preferred_element_type=F32)

        @pl.when(k == nk - 1)
        def _():
            o_ref[...] = acc[...].astype(o_ref.dtype)

    return pl.pallas_call(
        body, name=name, grid=(M // tm, N // tn, nk),
        in_specs=[pl.BlockSpec((tk, tm), lambda i, j, k: (k, i)), pl.BlockSpec((tk, tn), lambda i, j, k: (k, j))],
        out_specs=pl.BlockSpec((tm, tn), lambda i, j, k: (i, j)),
        out_shape=jax.ShapeDtypeStruct((M, N), out_dtype),
        scratch_shapes=[pltpu.VMEM((tm, tn), F32)],
        compiler_params=_cparams("arbitrary", "arbitrary", "arbitrary"),
    )(a, b)


def _row_specs(rin, pbin, glin, tr):
    specs = [pl.BlockSpec((1, tr, w), lambda b, i, cb=cb, ro=ro, bo=(e[4] if len(e) > 4 else 0): (b + bo, i + ro, cb))
             for e in rin for (_, w, cb, ro) in [e[:4]]]
    specs += [pl.BlockSpec((1, 1, a.shape[-1]), lambda b, i: (b, 0, 0)) for a in pbin]
    specs += [pl.BlockSpec((1, a.shape[-1]), lambda b, i: (0, 0)) for a in glin]
    return specs


def rows_fwd(name, fn, nb, nblk, tr, rin, pbin, glin, outs):
    nr, npb, ngl = len(rin), len(pbin), len(glin)
    n_in = nr + npb + ngl

    def body(*refs):
        args = [r[0].astype(F32) for r in refs[:nr + npb]] + [r[...] for r in refs[nr + npb:n_in]]
        res = fn(*args)
        for o, v in zip(refs[n_in:], res):
            o[0] = v.astype(o.dtype)

    return pl.pallas_call(
        body, name=name, grid=(nb, nblk), in_specs=_row_specs(rin, pbin, glin, tr),
        out_specs=[pl.BlockSpec((1, tr, w), lambda b, i: (b, i, 0)) for (w, _) in outs],
        out_shape=[jax.ShapeDtypeStruct((nb, nblk * tr, w), dt) for (w, dt) in outs],
        compiler_params=_cparams("arbitrary", "arbitrary"),
    )(*[e[0] for e in rin], *pbin, *glin)


def rows_bwd(name, fn, nb, nblk, tr, rin, pbin, glin, cts, want):
    nr, npb, ngl, nct = len(rin), len(pbin), len(glin), len(cts)
    n_in = nr + npb + ngl

    def body(*refs):
        b, i = pl.program_id(0), pl.program_id(1)
        args = [r[0].astype(F32) for r in refs[:nr + npb]] + [r[...] for r in refs[nr + npb:n_in]]
        ct = tuple(r[0].astype(F32) for r in refs[n_in:n_in + nct])
        _, vjp = jax.vjp(fn, *args)
        g = vjp(ct)
        orefs = refs[n_in + nct:]
        for o, (idx, _) in zip(orefs, want):
            o[0] = g[idx].astype(o.dtype)
        pb_refs = orefs[len(want):len(want) + npb]
        gl_refs = orefs[len(want) + npb:]

        @pl.when(i == 0)
        def _():
            for o, v in zip(pb_refs, g[nr:nr + npb]):
                o[0] = v

        @pl.when(i > 0)
        def _():
            for o, v in zip(pb_refs, g[nr:nr + npb]):
                o[0] += v

        first = jnp.logical_and(b == 0, i == 0)

        @pl.when(first)
        def _():
            for o, v in zip(gl_refs, g[nr + npb:]):
                o[...] = v

        @pl.when(jnp.logical_not(first))
        def _():
            for o, v in zip(gl_refs, g[nr + npb:]):
                o[...] += v

    out_specs = [pl.BlockSpec((1, tr, rin[idx][1]), lambda b, i: (b, i, 0)) for (idx, _) in want]
    out_shape = [jax.ShapeDtypeStruct((nb, nblk * tr, rin[idx][1]), dt) for (idx, dt) in want]
    out_specs += [pl.BlockSpec((1, 1, a.shape[-1]), lambda b, i: (b, 0, 0)) for a in pbin]
    out_shape += [jax.ShapeDtypeStruct((nb, 1, a.shape[-1]), F32) for a in pbin]
    out_specs += [pl.BlockSpec((1, a.shape[-1]), lambda b, i: (0, 0)) for a in glin]
    out_shape += [jax.ShapeDtypeStruct((1, a.shape[-1]), F32) for a in glin]
    return pl.pallas_call(
        body, name=name, grid=(nb, nblk),
        in_specs=_row_specs(rin, pbin, glin, tr) + _row_specs(cts, [], [], tr),
        out_specs=out_specs, out_shape=out_shape,
        compiler_params=_cparams("arbitrary", "arbitrary"),
    )(*[e[0] for e in rin], *pbin, *glin, *[e[0] for e in cts])


def ew_call(name, fn, ins, outs):
    def body(*refs):
        res = fn(*[r[...] for r in refs[:len(ins)]])
        for o, v in zip(refs[len(ins):], res):
            o[...] = v.astype(o.dtype)

    return pl.pallas_call(body, name=name, out_shape=[jax.ShapeDtypeStruct(s, dt) for (s, dt) in outs])(*ins)


def fn_prenorm(x, shift, scale, g):
    return (_rms(x, g) * (1.0 + scale) + shift,)


def fn_rms(x, g):
    return (_rms(x, g),)


def fn_ssd_finish(yf, yr, xs, z, dexp, nw):
    y = yf + yr + dexp * xs
    return (_rms(y * _silu(z), nw),)


def fn_postmix(x, mix, gate1, scale2, shift2, post_g, pre_g):
    x1 = x + gate1 * _rms(mix, post_g)
    h2 = _rms(x1, pre_g) * (1.0 + scale2) + shift2
    return x1, h2


def final_call(x1, ffn, target, gate2, post_g, tr):
    nb, S, D = x1.shape
    nblk = S // tr

    def body(x1_ref, f_ref, t_ref, g2_ref, pg_ref, dx1_ref, df_ref, dg2_ref, dpg_ref, loss_ref):
        b, i = pl.program_id(0), pl.program_id(1)
        tgt = t_ref[0]

        def lossfn(x1v, fv, g2, pg):
            e = x1v + g2 * _rms(fv, pg) - tgt
            return 0.5 * jnp.sum(jnp.mean(e * e, axis=-1, keepdims=True))

        val, (dx1, df, dg2, dpg) = jax.value_and_grad(lossfn, argnums=(0, 1, 2, 3))(
            x1_ref[0], f_ref[0].astype(F32), g2_ref[0], pg_ref[...])
        dx1_ref[0] = dx1
        df_ref[0] = df.astype(df_ref.dtype)
        lv = jnp.full((1, LANE), val, F32)

        @pl.when(i == 0)
        def _():
            dg2_ref[0] = dg2

        @pl.when(i > 0)
        def _():
            dg2_ref[0] += dg2

        first = jnp.logical_and(b == 0, i == 0)

        @pl.when(first)
        def _():
            dpg_ref[...] = dpg
            loss_ref[...] = lv

        @pl.when(jnp.logical_not(first))
        def _():
            dpg_ref[...] += dpg
            loss_ref[...] += lv

    row = pl.BlockSpec((1, tr, D), lambda b, i: (b, i, 0))
    pb = pl.BlockSpec((1, 1, D), lambda b, i: (b, 0, 0))
    gl = pl.BlockSpec((1, D), lambda b, i: (0, 0))
    return pl.pallas_call(
        body, name="loss_head", grid=(nb, nblk), in_specs=[row, row, row, pb, gl],
        out_specs=[row, row, pb, gl, pl.BlockSpec((1, LANE), lambda b, i: (0, 0))],
        out_shape=[jax.ShapeDtypeStruct((nb, S, D), F32), jax.ShapeDtypeStruct((nb, S, D), BF16),
                   jax.ShapeDtypeStruct((nb, 1, D), F32), jax.ShapeDtypeStruct((1, D), F32),
                   jax.ShapeDtypeStruct((1, LANE), F32)],
        compiler_params=_cparams("arbitrary", "arbitrary"),
    )(x1, ffn, target, gate2, post_g)


def _rotate_half(t):
    lane = lax.broadcasted_iota(jnp.int32, t.shape, 1)
    return jnp.where((lane & 15) < 8, -pltpu.roll(t, LANE - 8, 1), pltpu.roll(t, 8, 1))


def rope_call(name, x, width, colblk, cos, sin, out_dtype, tr):
    nb = x.shape[0]
    R = cos.shape[0]
    nblk = R // tr

    def body(x_ref, c_ref, s_ref, o_ref):
        c, s = c_ref[...], s_ref[...]
        for h in range(width // LANE):
            t = x_ref[0, :, h * LANE:(h + 1) * LANE].astype(F32)
            o_ref[0, :, h * LANE:(h + 1) * LANE] = (t * c + _rotate_half(t) * s).astype(o_ref.dtype)

    tab = pl.BlockSpec((tr, LANE), lambda b, i: (i, 0))
    return pl.pallas_call(
        body, name=name, grid=(nb, nblk),
        in_specs=[pl.BlockSpec((1, tr, width), lambda b, i: (b, i, colblk)), tab, tab],
        out_specs=pl.BlockSpec((1, tr, width), lambda b, i: (b, i, 0)),
        out_shape=jax.ShapeDtypeStruct((nb, R, width), out_dtype),
        compiler_params=_cparams("arbitrary", "arbitrary"),
    )(x, cos, sin)


def rope_tables(n_ctx, seq):
    n_rows = seq // GRID_W
    row = np.repeat(np.arange(n_rows), GRID_W).astype(np.float32)
    col = np.tile(np.arange(GRID_W), n_rows).astype(np.float32)
    axis_dim = ROPE // 2
    inv_freq = jnp.asarray(ROPE_THETA, F32) ** (-jnp.arange(0, axis_dim, 2, dtype=F32) / axis_dim)
    ang_r = jnp.asarray(row)[:, None] * inv_freq
    ang_c = jnp.asarray(col)[:, None] * inv_freq
    ang = jnp.concatenate([ang_r, ang_r, ang_c, ang_c], axis=-1)
    cos = jnp.ones((n_ctx + seq, LANE), F32).at[n_ctx:, KR_LANE:KR_LANE + ROPE].set(jnp.cos(ang))
    sin = jnp.zeros((n_ctx + seq, LANE), F32).at[n_ctx:, KR_LANE:KR_LANE + ROPE].set(jnp.sin(ang))
    return cos, sin


Q_PRESCALE = ATTN_SCALE * math.log2(math.e)


def _attn_weights(q, kc):
    s2 = lax.dot_general(q, kc, (((1,), (1,)), ((), ())), preferred_element_type=F32)
    e = jnp.exp2(s2 - jnp.max(s2, axis=1, keepdims=True))
    return e, 1.0 / jnp.sum(e, axis=1, keepdims=True)


def _key_block(kv, kr):
    lane = lax.broadcasted_iota(jnp.int32, kv.shape, 1)
    return jnp.where(lane < NOPE, kv, kr)


def _rotated_query(q_ref, cos_ref, sin_ref):
    t = q_ref[0].astype(F32)
    return (t * cos_ref[...] + _rotate_half(t) * sin_ref[...]).astype(BF16)


def attn_fwd(q_raw, kv, kr, cos_q, sin_q, tq):
    nb, S, _ = q_raw.shape
    T = kv.shape[1]

    def body(q_ref, kv_ref, kr_ref, c_ref, s_ref, o_ref):
        kvv = kv_ref[0]
        e, r = _attn_weights(_rotated_query(q_ref, c_ref, s_ref), _key_block(kvv, kr_ref[0]))
        o = lax.dot_general(e.astype(BF16), kvv, (((1,), (0,)), ((), ())), preferred_element_type=F32) * r
        lane = lax.broadcasted_iota(jnp.int32, o.shape, 1)
        o_ref[0] = jnp.where(lane >= NOPE, o, 0.0).astype(o_ref.dtype)

    return pl.pallas_call(
        body, name="attn_fwd", grid=(nb, N_HEADS, S // tq),
        in_specs=[pl.BlockSpec((1, tq, HEAD_BLOCK), lambda b, h, i: (b, i, h)),
                  pl.BlockSpec((1, T, HEAD_BLOCK), lambda b, h, i: (b, 0, h)),
                  pl.BlockSpec((1, T, HEAD_BLOCK), lambda b, h, i: (b, 0, 0)),
                  pl.BlockSpec((tq, LANE), lambda b, h, i: (i, 0)), pl.BlockSpec((tq, LANE), lambda b, h, i: (i, 0))],
        out_specs=pl.BlockSpec((1, tq, HEAD_BLOCK), lambda b, h, i: (b, i, h)),
        out_shape=jax.ShapeDtypeStruct((nb, S, QP), BF16),
        compiler_params=_cparams("arbitrary", "arbitrary", "arbitrary"),
    )(q_raw, kv, kr, cos_q, sin_q)


def attn_bwd(q_raw, kv, kr, do, cos_q, sin_q, cos, sin, tq):
    nb, S, _ = q_raw.shape
    T = kv.shape[1]

    def body(q_ref, kv_ref, kr_ref, do_ref, cq_ref, sq_ref, c_ref, s_ref, dq_ref, dkv_ref, dkr_ref):
        h, i = pl.program_id(1), pl.program_id(2)

        @pl.when(i == 0)
        def _():
            dkv_ref[...] = jnp.zeros_like(dkv_ref)

        @pl.when(jnp.logical_and(h == 0, i == 0))
        def _():
            dkr_ref[...] = jnp.zeros_like(dkr_ref)

        qv, kvv, dov = _rotated_query(q_ref, cq_ref, sq_ref), kv_ref[0], do_ref[0]
        kc = _key_block(kvv, kr_ref[0])
        e, r = _attn_weights(qv, kc)
        dor = (dov.astype(F32) * r).astype(BF16)
        dpr = lax.dot_general(dor, kvv, (((1,), (1,)), ((), ())), preferred_element_type=F32)
        ds = (e * (dpr - r * jnp.sum(dpr * e, axis=1, keepdims=True))).astype(BF16)
        dq = lax.dot_general(ds, kc, (((1,), (0,)), ((), ())), preferred_element_type=F32) * ATTN_SCALE
        dq_ref[0] = (dq * c_ref[...] - _rotate_half(dq) * s_ref[...]).astype(dq_ref.dtype)
        dkc = lax.dot_general(ds, qv, (((0,), (0,)), ((), ())), preferred_element_type=F32) * math.log(2.0)
        dv = lax.dot_general(e.astype(BF16), dor, (((0,), (0,)), ((), ())), preferred_element_type=F32)
        lane = lax.broadcasted_iota(jnp.int32, dkc.shape, 1)
        dkv_ref[0] += jnp.where(lane < NOPE, dkc, dv)
        dkr_ref[0] += jnp.where(lane >= NOPE, dkc, 0.0)

    qspec = pl.BlockSpec((1, tq, HEAD_BLOCK), lambda b, h, i: (b, i, h))
    kspec = pl.BlockSpec((1, T, HEAD_BLOCK), lambda b, h, i: (b, 0, h))
    rspec = pl.BlockSpec((1, T, HEAD_BLOCK), lambda b, h, i: (b, 0, 0))
    tab = pl.BlockSpec((tq, LANE), lambda b, h, i: (i, 0))
    return pl.pallas_call(
        body, name="attn_bwd", grid=(nb, N_HEADS, S // tq),
        in_specs=[qspec, kspec, rspec, qspec, tab, tab, tab, tab], out_specs=[qspec, kspec, rspec],
        out_shape=[jax.ShapeDtypeStruct((nb, S, QP), BF16), jax.ShapeDtypeStruct((nb, T, QP), F32),
                   jax.ShapeDtypeStruct((nb, T, HEAD_BLOCK), F32)],
        compiler_params=_cparams("arbitrary", "arbitrary", "arbitrary"),
    )(q_raw, kv, kr, do, cos_q, sin_q, cos, sin)


CONV_HALO = 8


def _segments(n, n_ctx):
    if n_ctx == 0:
        return [(0, n, CONV_HALO)]
    return [(0, n_ctx, CONV_HALO), (n_ctx, n - n_ctx, 2 * CONV_HALO + n_ctx)]


def _halo_scratch(n, n_ctx, tc):
    return pltpu.VMEM((n + CONV_HALO * (len(_segments(n, n_ctx)) + 1), tc), F32)


def _zero_halos(scr, segs):
    z = jnp.zeros((CONV_HALO, scr.shape[1]), scr.dtype)
    scr[0:CONV_HALO, :] = z
    for (_, rows, off) in segs:
        scr[off + rows:off + rows + CONV_HALO, :] = z


CONV_BLOCK_MAX = 256


def _conv_block(n, n_ctx):
    return _tile(math.gcd(n_ctx, n - n_ctx) if n_ctx else n, CONV_BLOCK_MAX, 8)


def _window(scr, off, r0, blk):
    return scr[pl.ds(pl.multiple_of(off - CONV_HALO + r0, 8), blk + 2 * CONV_HALO), :]


def _shifted(win, s):
    v = win if s == 0 else pltpu.roll(win, (-s) % win.shape[0], 0)
    return v[CONV_HALO:win.shape[0] - CONV_HALO]


def _tap_blocks(win, k, sign):
    return [_shifted(win, sign * (o - k // 2)) for o in range(k)]


def _taps(blocks, w):
    acc = None
    for o, blk in enumerate(blocks):
        t = w[o:o + 1, :] * blk
        acc = t if acc is None else acc + t
    return acc


def _tap_grads(xblocks, dpre):
    k = len(xblocks)
    sub8 = lax.broadcasted_iota(jnp.int32, (8, dpre.shape[1]), 0)
    out = jnp.where(sub8 == k, jnp.sum(dpre, axis=0, keepdims=True), 0.0)
    for o, blk in enumerate(xblocks):
        out = out + jnp.where(sub8 == o, jnp.sum(dpre * blk, axis=0, keepdims=True), 0.0)
    return out


def _row_blocks(rows, blk, fn, init=0):
    return lax.fori_loop(0, rows // blk, lambda i, c: fn(pl.multiple_of(i * blk, blk), c), init)


def _gelu(x):
    return 0.5 * x * (1.0 + lax.erf(x * (1.0 / math.sqrt(2.0))))


def _gelu_and_grad(x):
    cdf = 0.5 * (1.0 + lax.erf(x * (1.0 / math.sqrt(2.0))))
    return x * cdf, cdf + x * jnp.exp(-0.5 * x * x) * (1.0 / math.sqrt(2.0 * math.pi))


def ssd_conv_fwd(u, w8, bias, n_ctx, tc):
    nb, T, _ = u.shape
    cb0 = OFF_XBC // tc

    segs, blk = _segments(T, n_ctx), _conv_block(T, n_ctx)

    def body(x_ref, w_ref, b_ref, o_ref, xs):
        _zero_halos(xs, segs)
        for (start, rows, off) in segs:
            xs[off:off + rows, :] = x_ref[0, start:start + rows, :]
        w, bias_v = w_ref[...], b_ref[...]
        for (start, rows, off) in segs:
            def block(r0, carry, start=start, off=off):
                pre = bias_v + _taps(_tap_blocks(_window(xs, off, r0, blk), SSD_K, 1), w)
                o_ref[0, pl.ds(pl.multiple_of(start + r0, blk), blk), :] = _silu(pre)
                return carry

            _row_blocks(rows, blk, block)

    return pl.pallas_call(
        body, name="ssd_conv_fwd", grid=(nb, XBC // tc),
        in_specs=[pl.BlockSpec((1, T, tc), lambda b, j: (b, 0, cb0 + j)),
                  pl.BlockSpec((8, tc), lambda b, j: (0, j)), pl.BlockSpec((1, tc), lambda b, j: (0, j))],
        out_specs=pl.BlockSpec((1, T, tc), lambda b, j: (b, 0, j)),
        out_shape=jax.ShapeDtypeStruct((nb, T, XBC), F32),
        scratch_shapes=[_halo_scratch(T, n_ctx, tc)],
        compiler_params=_cparams("arbitrary", "arbitrary"),
    )(u, w8, bias)


def ssd_conv_bwd(u, w8, bias, dxbc, dxs_direct, n_ctx, tc):
    nb, T, _ = u.shape
    cb0 = OFF_XBC // tc
    n_direct = D_INNER // tc

    segs, blk = _segments(T, n_ctx), _conv_block(T, n_ctx)

    def body(x_ref, w_ref, b_ref, d0_ref, d1_ref, dd_ref, dx_ref, dw_ref, xs, ds):
        j, b = pl.program_id(0), pl.program_id(1)
        _zero_halos(xs, segs)
        _zero_halos(ds, segs)
        for (start, rows, off) in segs:
            xs[off:off + rows, :] = x_ref[0, start:start + rows, :]
        w, bias_v = w_ref[...], b_ref[...]
        has_direct = (j < n_direct).astype(F32)
        rows = jnp.zeros((8, tc), F32)
        for (start, n_rows, off) in segs:
            def block(r0, acc, start=start, off=off):
                xblocks = _tap_blocks(_window(xs, off, r0, blk), SSD_K, 1)
                pre = bias_v + _taps(xblocks, w)
                d = d0_ref[0, 0, pl.ds(pl.multiple_of(start + r0, blk), blk), :] + d1_ref[0, 0, pl.ds(pl.multiple_of(start + r0, blk), blk), :]
                if start == n_ctx:
                    d = d + dd_ref[0, pl.ds(r0, blk), :] * has_direct
                sg = jax.nn.sigmoid(pre)
                dpre = d * (sg * (1.0 + pre * (1.0 - sg)))
                ds[pl.ds(pl.multiple_of(off + r0, 8), blk), :] = dpre
                return acc + _tap_grads(xblocks, dpre)

            rows = _row_blocks(n_rows, blk, block, rows)
        for (start, n_rows, off) in segs:
            def block_dx(r0, carry, start=start, off=off):
                dx_ref[0, pl.ds(pl.multiple_of(start + r0, blk), blk), :] = _taps(_tap_blocks(_window(ds, off, r0, blk), SSD_K, -1), w).astype(dx_ref.dtype)
                return carry

            _row_blocks(n_rows, blk, block_dx)

        @pl.when(b == 0)
        def _():
            dw_ref[...] = rows

        @pl.when(b > 0)
        def _():
            dw_ref[...] += rows

    dspec0 = pl.BlockSpec((1, 1, T, tc), lambda j, b: (0, b, 0, j))
    dspec1 = pl.BlockSpec((1, 1, T, tc), lambda j, b: (1, b, 0, j))
    return pl.pallas_call(
        body, name="ssd_conv_bwd", grid=(XBC // tc, nb),
        in_specs=[pl.BlockSpec((1, T, tc), lambda j, b: (b, 0, cb0 + j)),
                  pl.BlockSpec((8, tc), lambda j, b: (0, j)), pl.BlockSpec((1, tc), lambda j, b: (0, j)),
                  dspec0, dspec1,
                  pl.BlockSpec((1, T - n_ctx, tc), lambda j, b: (b, 0, jnp.minimum(j, n_direct - 1)))],
        out_specs=[pl.BlockSpec((1, T, tc), lambda j, b: (b, 0, j)), pl.BlockSpec((8, tc), lambda j, b: (0, j))],
        out_shape=[jax.ShapeDtypeStruct((nb, T, XBC), BF16), jax.ShapeDtypeStruct((8, XBC), F32)],
        scratch_shapes=[_halo_scratch(T, n_ctx, tc), _halo_scratch(T, n_ctx, tc)],
        compiler_params=_cparams("arbitrary", "arbitrary"),
    )(u, w8, bias, dxbc, dxbc, dxs_direct)


GLU_TC = 256


def glu_interleave(w_up):
    blocks = []
    for j in range(D_FF // GLU_TC):
        blocks += [w_up[:, j * GLU_TC:(j + 1) * GLU_TC], w_up[:, D_FF + j * GLU_TC:D_FF + (j + 1) * GLU_TC]]
    return jnp.concatenate(blocks, axis=1)


def glu_deinterleave(g):
    nj = D_FF // GLU_TC
    gate = [g[:, 2 * j * GLU_TC:(2 * j + 1) * GLU_TC] for j in range(nj)]
    val = [g[:, (2 * j + 1) * GLU_TC:(2 * j + 2) * GLU_TC] for j in range(nj)]
    return jnp.concatenate(gate + val, axis=1)


def glu_fwd(up, w8, bias):
    nb, S, _ = up.shape
    tc = GLU_TC

    segs, blk = _segments(S, 0), _conv_block(S, 0)
    (_, _, off), = segs

    def body(u_ref, w_ref, b_ref, o_ref, xs):
        _zero_halos(xs, segs)
        xs[off:off + S, :] = u_ref[0, :, :tc]
        w, bias_v = w_ref[...], b_ref[...]

        def block(r0, carry):
            gc = bias_v + _taps(_tap_blocks(_window(xs, off, r0, blk), FFN_K, 1), w)
            o_ref[0, pl.ds(r0, blk), :] = (_gelu(gc) * u_ref[0, pl.ds(r0, blk), tc:]).astype(o_ref.dtype)
            return carry

        _row_blocks(S, blk, block)

    return pl.pallas_call(
        body, name="glu_fwd", grid=(nb, D_FF // tc),
        in_specs=[pl.BlockSpec((1, S, 2 * tc), lambda b, j: (b, 0, j)),
                  pl.BlockSpec((8, tc), lambda b, j: (0, j)), pl.BlockSpec((1, tc), lambda b, j: (0, j))],
        out_specs=pl.BlockSpec((1, S, tc), lambda b, j: (b, 0, j)),
        out_shape=jax.ShapeDtypeStruct((nb, S, D_FF), BF16),
        scratch_shapes=[_halo_scratch(S, 0, tc)],
        compiler_params=_cparams("arbitrary", "arbitrary"),
    )(up, w8, bias)


def glu_bwd(up, w8, bias, dact):
    nb, S, _ = up.shape
    tc = GLU_TC

    segs, blk = _segments(S, 0), _conv_block(S, 0)
    (_, _, off), = segs

    def body(u_ref, w_ref, b_ref, d_ref, du_ref, dw_ref, xs, ds):
        b = pl.program_id(1)
        _zero_halos(xs, segs)
        _zero_halos(ds, segs)
        xs[off:off + S, :] = u_ref[0, :, :tc]
        w, bias_v = w_ref[...], b_ref[...]

        def block(r0, acc):
            here = pl.ds(r0, blk)
            xblocks = _tap_blocks(_window(xs, off, r0, blk), FFN_K, 1)
            act, act_grad = _gelu_and_grad(bias_v + _taps(xblocks, w))
            d = d_ref[0, here, :].astype(F32)
            du_ref[0, here, tc:] = (d * act).astype(du_ref.dtype)
            dpre = d * u_ref[0, here, tc:] * act_grad
            ds[pl.ds(pl.multiple_of(off + r0, 8), blk), :] = dpre
            return acc + _tap_grads(xblocks, dpre)

        rows = _row_blocks(S, blk, block, jnp.zeros((8, tc), F32))

        def block_dx(r0, carry):
            du_ref[0, pl.ds(r0, blk), :tc] = _taps(_tap_blocks(_window(ds, off, r0, blk), FFN_K, -1), w).astype(du_ref.dtype)
            return carry

        _row_blocks(S, blk, block_dx)

        @pl.when(b == 0)
        def _():
            dw_ref[...] = rows

        @pl.when(b > 0)
        def _():
            dw_ref[...] += rows

    pair = pl.BlockSpec((1, S, 2 * tc), lambda j, b: (b, 0, j))
    return pl.pallas_call(
        body, name="glu_bwd", grid=(D_FF // tc, nb),
        in_specs=[pair, pl.BlockSpec((8, tc), lambda j, b: (0, j)), pl.BlockSpec((1, tc), lambda j, b: (0, j)),
                  pl.BlockSpec((1, S, tc), lambda j, b: (b, 0, j))],
        out_specs=[pair, pl.BlockSpec((8, tc), lambda j, b: (0, j))],
        out_shape=[jax.ShapeDtypeStruct((nb, S, 2 * D_FF), BF16), jax.ShapeDtypeStruct((8, D_FF), F32)],
        scratch_shapes=[_halo_scratch(S, 0, tc), _halo_scratch(S, 0, tc)],
        compiler_params=_cparams("arbitrary", "arbitrary"),
    )(up, w8, bias, dact)


def _chunk_of(d, k, n_cc, n_ch):
    rev = jnp.where(k < n_cc, n_cc - 1 - k, n_cc + n_ch - 1 - k)
    return jnp.where(d == 1, rev, k)


def _lane_pick(v, lane_iota, l):
    return jnp.sum(jnp.where(lane_iota == l, v, 0.0), axis=1, keepdims=True)


def head_spread_matrix():
    return (jnp.arange(LANE)[:, None] == (jnp.arange(D_INNER)[None, :] // SSD_P)).astype(BF16)


def _split_dot(x, e, dims):
    hi = x.astype(BF16)
    lo = (x - hi.astype(F32)).astype(BF16)
    return (lax.dot_general(hi, e, dims, preferred_element_type=F32)
            + lax.dot_general(lo, e, dims, preferred_element_type=F32))


def _spread(x, e):
    return _split_dot(x, e, (((1,), (0,)), ((), ())))


def _gather_heads(y, e):
    return _split_dot(y, e, (((1,), (1,)), ((), ())))


def _softplus(x):
    return jnp.maximum(x, 0.0) + jnp.log(1.0 + jnp.exp(-jnp.abs(x)))


def ssd_dt_inputs(u, a_log, dt_bias):
    pad = LANE - SSD_HEADS
    dt = u[..., OFF_DT:OFF_DT + 2 * SSD_HEADS]
    dt2 = jnp.stack([jnp.pad(dt[..., i * SSD_HEADS:(i + 1) * SSD_HEADS], ((0, 0), (0, 0), (0, pad))) for i in range(2)])

    def lanes(v):
        return jnp.pad(v.reshape(2, 1, SSD_HEADS), ((0, 0), (0, 0), (0, pad)))

    return dt2, lanes(a_log), lanes(dt_bias)


def _ssd_common(d, dt_raw, alog, dtb):
    Q = dt_raw.shape[0]
    row = lax.broadcasted_iota(jnp.int32, (Q, Q), 0)
    col = lax.broadcasted_iota(jnp.int32, (Q, Q), 1)
    rev = d == 1
    maskb = jnp.where(rev, row, col) <= jnp.where(rev, col, row)
    tri = maskb.astype(F32)
    A = -jnp.exp(alog)
    dtv = _softplus(dt_raw + dtb)
    a = dtv * A
    cum = lax.dot_general(tri, a, (((1,), (0,)), ((), ())), precision=lax.Precision.HIGHEST, preferred_element_type=F32)
    tot = jnp.sum(a, axis=0, keepdims=True)
    return maskb, tri, A, dtv, cum, tot


def ssd_fwd(xbc, dt2, alog2, dtb2, n_ctx, hosted):
    nb, T, _ = xbc.shape
    S = T - n_ctx
    n_ch, n_cc = T // CHUNK, n_ctx // CHUNK
    Q = CHUNK
    n_pairs = SSD_HEADS // 2
    n_ex = hosted.n
    n_in = 5

    def body(*refs):
        x_ref, dt_ref, al_ref, db_ref, e_ref = refs[:n_in]
        send_refs = refs[n_in:n_in + n_ex]
        y_ref, hin_ref = refs[n_in + n_ex:n_in + 2 + n_ex]
        recv_refs = refs[n_in + 2 + n_ex:n_in + 2 + 2 * n_ex]
        H, *sems = refs[n_in + 2 + 2 * n_ex:]
        d, k = pl.program_id(1), pl.program_id(2)
        first_step = jnp.logical_and(jnp.logical_and(pl.program_id(0) == 0, d == 0), k == 0)
        last_step = jnp.logical_and(jnp.logical_and(pl.program_id(0) == nb - 1, d == 1), k == n_ch - 1)
        begin_exchange, end_exchange = hosted.steps(send_refs, recv_refs, sems, first_step, last_step)
        begin_exchange()

        @pl.when(k == 0)
        def _():
            H[...] = jnp.zeros_like(H)

        maskb, tri, A, dtv, cum, tot = _ssd_common(d, dt_ref[0, 0], al_ref[0], db_ref[0])
        e = e_ref[...]
        cumT = cum.T
        cum_e, dt_e = _spread(cum, e), _spread(dtv, e)
        tot_e = _spread(jnp.broadcast_to(tot, (8, LANE)), e)[0:1]
        hin_ref[0, 0, 0] = H[...].astype(BF16)
        lane = lax.broadcasted_iota(jnp.int32, (Q, LANE), 1)
        lane1 = lax.broadcasted_iota(jnp.int32, (1, LANE), 1)
        subc = lax.broadcasted_iota(jnp.int32, (LANE, 1), 0)
        half = lane < SSD_P
        for g in range(SSD_GROUPS):
            Bg = x_ref[0, :, D_INNER + g * SSD_N:D_INNER + (g + 1) * SSD_N].astype(BF16)
            Cg = x_ref[0, :, D_INNER + GN + g * SSD_N:D_INNER + GN + (g + 1) * SSD_N].astype(BF16)
            Gm = lax.dot_general(Cg, Bg, (((1,), (1,)), ((), ())), preferred_element_type=F32)
            for pr in range(n_pairs // SSD_GROUPS):
                p = g * (n_pairs // SSD_GROUPS) + pr
                sc, dtp, totp = [t[:, p * LANE:(p + 1) * LANE] for t in (cum_e, dt_e, tot_e)]
                swapped = pltpu.roll(sc, SSD_P, 1)
                s0c, s1c = jnp.where(half, sc, swapped), jnp.where(half, swapped, sc)
                s0r, s1r = cumT[2 * p:2 * p + 1, :], cumT[2 * p + 1:2 * p + 2, :]
                tot0, tot1 = _lane_pick(tot, lane1, 2 * p), _lane_pick(tot, lane1, 2 * p + 1)
                M0 = (Gm * jnp.exp(jnp.where(maskb, s0c - s0r, NEG_BIG))).astype(BF16)
                M1 = (Gm * jnp.exp(jnp.where(maskb, s1c - s1r, NEG_BIG))).astype(BF16)
                xd = x_ref[0, :, p * LANE:(p + 1) * LANE] * dtp
                xdb = xd.astype(BF16)
                yd = jnp.where(half,
                               lax.dot_general(M0, xdb, (((1,), (0,)), ((), ())), preferred_element_type=F32),
                               lax.dot_general(M1, xdb, (((1,), (0,)), ((), ())), preferred_element_type=F32))
                Hp = H[p * LANE:(p + 1) * LANE, :]
                yo = lax.dot_general(Cg, Hp.astype(BF16), (((1,), (1,)), ((), ())), preferred_element_type=F32) * jnp.exp(sc)
                y_ref[0, 0, :, p * LANE:(p + 1) * LANE] = yd + yo
                xdw = (xd * jnp.exp(totp - sc)).astype(BF16)
                etot = jnp.exp(jnp.where(subc < SSD_P, tot0, tot1))
                H[p * LANE:(p + 1) * LANE, :] = Hp * etot + lax.dot_general(
                    xdw, Bg, (((0,), (0,)), ((), ())), preferred_element_type=F32)
        end_exchange()

    def ymap(b, d, k):
        return (d, b, _chunk_of(d, jnp.maximum(k, n_cc), n_cc, n_ch) - n_cc, 0)

    return pl.pallas_call(
        body, name="ssd_fwd", grid=(nb, 2, n_ch),
        in_specs=[pl.BlockSpec((1, Q, XBC), lambda b, d, k: (b, _chunk_of(d, k, n_cc, n_ch), 0)),
                  pl.BlockSpec((1, 1, Q, LANE), lambda b, d, k: (d, b, _chunk_of(d, k, n_cc, n_ch), 0)),
                  pl.BlockSpec((1, 1, LANE), lambda b, d, k: (d, 0, 0)), pl.BlockSpec((1, 1, LANE), lambda b, d, k: (d, 0, 0)),
                  pl.BlockSpec((LANE, D_INNER), lambda b, d, k: (0, 0))] + hosted.specs,
        out_specs=[pl.BlockSpec((1, 1, Q, D_INNER), ymap),
                   pl.BlockSpec((1, 1, 1, D_INNER, SSD_N), lambda b, d, k: (d, b, k, 0, 0))] + hosted.specs,
        out_shape=[jax.ShapeDtypeStruct((2, nb, S, D_INNER), F32),
                   jax.ShapeDtypeStruct((2, nb, n_ch, D_INNER, SSD_N), BF16)] + hosted.out_shape,
        scratch_shapes=[pltpu.VMEM((D_INNER, SSD_N), F32)] + hosted.scratch,
        compiler_params=_cparams("arbitrary", "arbitrary", "arbitrary"),
    )(xbc, dt2, alog2, dtb2, head_spread_matrix(), *hosted.arrays)


def ssd_bwd(xbc, dt2, alog2, dtb2, hin, dy, n_ctx, hosted):
    nb, T, _ = xbc.shape
    n_ex = hosted.n
    n_ch, n_cc = T // CHUNK, n_ctx // CHUNK
    n_in = 7
    Q = CHUNK
    n_pairs = SSD_HEADS // 2
    NT = (((1,), (1,)), ((), ()))
    NN = (((1,), (0,)), ((), ()))
    TN = (((0,), (0,)), ((), ()))

    def dot(a, b, dims):
        return lax.dot_general(a.astype(BF16), b.astype(BF16), dims, preferred_element_type=F32)

    def body(*refs):
        x_ref, dt_ref, al_ref, db_ref, e_ref, hin_ref, dy_ref = refs[:n_in]
        send_refs = refs[n_in:n_in + n_ex]
        dx_ref, ddt_ref, st_ref = refs[n_in + n_ex:n_in + 3 + n_ex]
        recv_refs = refs[n_in + 3 + n_ex:n_in + 3 + 2 * n_ex]
        dH, dce, dde, *sems = refs[n_in + 3 + 2 * n_ex:]
        d, kk = pl.program_id(1), pl.program_id(2)
        ks = n_ch - 1 - kk
        first_step = jnp.logical_and(jnp.logical_and(pl.program_id(0) == 0, d == 0), kk == 0)
        last_step = jnp.logical_and(jnp.logical_and(pl.program_id(0) == nb - 1, d == 1), kk == n_ch - 1)
        begin_exchange, end_exchange = hosted.steps(send_refs, recv_refs, sems, first_step, last_step)
        begin_exchange()

        @pl.when(kk == 0)
        def _():
            dH[...] = jnp.zeros_like(dH)

        @pl.when(jnp.logical_and(jnp.logical_and(pl.program_id(0) == 0, d == 0), kk == 0))
        def _():
            st_ref[...] = jnp.zeros_like(st_ref)

        dt_raw = dt_ref[0, 0]
        alog, dtb_v = al_ref[0], db_ref[0]
        maskb, tri, A, dtv, cum, tot = _ssd_common(d, dt_raw, alog, dtb_v)
        e = e_ref[...]
        cumT = cum.T
        cum_e, dt_e = _spread(cum, e), _spread(dtv, e)
        tot_e = _spread(jnp.broadcast_to(tot, (8, LANE)), e)[0:1]
        live = (ks >= n_cc).astype(F32)
        lane = lax.broadcasted_iota(jnp.int32, (Q, LANE), 1)
        lane1 = lax.broadcasted_iota(jnp.int32, (1, LANE), 1)
        sub = lax.broadcasted_iota(jnp.int32, (LANE, Q), 0)
        subc = lax.broadcasted_iota(jnp.int32, (LANE, 1), 0)
        half = lane < SSD_P
        halfc = subc < SSD_P
        pair_ones = ((lax.broadcasted_iota(jnp.int32, (2 * Q, LANE), 0) >= Q).astype(jnp.int32)
                     == (lax.broadcasted_iota(jnp.int32, (2 * Q, LANE), 1) >= SSD_P).astype(jnp.int32)).astype(BF16)
        dcumT = jnp.zeros((LANE, Q), F32)
        dtot = jnp.zeros((1, LANE), F32)
        dtot_parts = []
        for g in range(SSD_GROUPS):
            Bg = x_ref[0, :, D_INNER + g * SSD_N:D_INNER + (g + 1) * SSD_N].astype(BF16)
            Cg = x_ref[0, :, D_INNER + GN + g * SSD_N:D_INNER + GN + (g + 1) * SSD_N].astype(BF16)
            Gm = lax.dot_general(Cg, Bg, NT, preferred_element_type=F32)
            dG = jnp.zeros((Q, Q), F32)
            dC = jnp.zeros((Q, SSD_N), F32)
            dB = jnp.zeros((Q, SSD_N), F32)
            for pr in range(n_pairs // SSD_GROUPS):
                p = g * (n_pairs // SSD_GROUPS) + pr
                l0, l1 = 2 * p, 2 * p + 1
                sc, dtp, totp = [t[:, p * LANE:(p + 1) * LANE] for t in (cum_e, dt_e, tot_e)]
                swapped = pltpu.roll(sc, SSD_P, 1)
                s0c, s1c = jnp.where(half, sc, swapped), jnp.where(half, swapped, sc)
                s0r, s1r = cumT[l0:l0 + 1, :], cumT[l1:l1 + 1, :]
                tot0, tot1 = _lane_pick(tot, lane1, l0), _lane_pick(tot, lane1, l1)
                L0 = jnp.exp(jnp.where(maskb, s0c - s0r, NEG_BIG))
                L1 = jnp.exp(jnp.where(maskb, s1c - s1r, NEG_BIG))
                M0, M1 = Gm * L0, Gm * L1
                xs = x_ref[0, :, p * LANE:(p + 1) * LANE]
                xd = xs * dtp
                es = jnp.exp(sc)
                dte = jnp.exp(totp - sc)
                etot = jnp.exp(jnp.where(halfc, tot0, tot1))
                dyp = dy_ref[0, :, p * LANE:(p + 1) * LANE] * live
                Hp = hin_ref[0, 0, 0, p * LANE:(p + 1) * LANE, :]
                dHp = dH[p * LANE:(p + 1) * LANE, :]
                bdh = dot(Bg, dHp, NT)
                mtdy = dot(jnp.concatenate([M0, M1], axis=1), dyp, TN)
                dxd = jnp.where(half, mtdy[:Q], mtdy[Q:]) + bdh * dte
                dy0 = jnp.where(half, dyp, 0.0)
                dm = dot(jnp.concatenate([dy0, dyp - dy0], axis=0), xd, NT)
                dM0, dM1 = dm[:Q], dm[Q:]
                dG = dG + dM0 * L0 + dM1 * L1
                dyes = dyp * es
                xdw = xd * dte
                dC = dC + dot(dyes, Hp, NN)
                dB = dB + dot(xdw, dHp, NN)
                W0, W1 = dM0 * M0, dM1 * M1
                yoff = dot(Cg, Hp, NT) * es
                r_off = dyp * yoff
                r_st = xd * bdh * dte
                hh = jnp.sum(dHp * Hp.astype(F32), axis=1, keepdims=True) * etot
                w_rows = _split_dot(jnp.concatenate([W0, W1], axis=1), pair_ones, NN) * (1.0 / SSD_P)
                dce[:, p * LANE:(p + 1) * LANE] = r_off - r_st + w_rows
                dde[:, p * LANE:(p + 1) * LANE] = dxd * xs
                dtot_parts.append(jnp.sum(r_st, axis=0, keepdims=True))
                for (l, W, hselc) in ((l0, W0, halfc), (l1, W1, jnp.logical_not(halfc))):
                    row_g = -jnp.sum(W, axis=0, keepdims=True)
                    dcumT = dcumT + jnp.where(sub == l, row_g, 0.0)
                    dtot = dtot + jnp.where(lane1 == l, jnp.sum(jnp.where(hselc, hh, 0.0), axis=0, keepdims=True), 0.0)
                dx_ref[0, 0, :, p * LANE:(p + 1) * LANE] = dxd * dtp
                dH[p * LANE:(p + 1) * LANE, :] = dHp * etot + dot(dyes, Cg, TN)
            dx_ref[0, 0, :, D_INNER + g * SSD_N:D_INNER + (g + 1) * SSD_N] = dB + dot(dG, Cg, TN)
            dx_ref[0, 0, :, D_INNER + GN + g * SSD_N:D_INNER + GN + (g + 1) * SSD_N] = dC + dot(dG, Bg, NN)
        dcum_all = dcumT.T + _gather_heads(dce[...], e)
        dtot_e = jnp.broadcast_to(jnp.concatenate(dtot_parts, axis=1), (8, D_INNER))
        dtot = dtot + _gather_heads(dtot_e, e)[0:1]
        da = lax.dot_general(tri, dcum_all, TN, precision=lax.Precision.HIGHEST, preferred_element_type=F32) + dtot
        ddtv = _gather_heads(dde[...], e) + da * A
        ddt_raw = ddtv * jax.nn.sigmoid(dt_raw + dtb_v)
        ddt_ref[0, 0] = ddt_raw
        sub8 = lax.broadcasted_iota(jnp.int32, (8, LANE), 0)
        st_ref[...] += (jnp.where(sub8 == 2 * d, jnp.sum(da * dtv * A, axis=0, keepdims=True), 0.0)
                        + jnp.where(sub8 == 2 * d + 1, jnp.sum(ddt_raw, axis=0, keepdims=True), 0.0))
        end_exchange()

    def cmap(d, kk):
        return _chunk_of(d, n_ch - 1 - kk, n_cc, n_ch)

    def dymap(b, d, kk):
        return (b, _chunk_of(d, jnp.maximum(n_ch - 1 - kk, n_cc), n_cc, n_ch) - n_cc, 0)

    return pl.pallas_call(
        body, name="ssd_bwd", grid=(nb, 2, n_ch),
        in_specs=[pl.BlockSpec((1, Q, XBC), lambda b, d, kk: (b, cmap(d, kk), 0)),
                  pl.BlockSpec((1, 1, Q, LANE), lambda b, d, kk: (d, b, cmap(d, kk), 0)),
                  pl.BlockSpec((1, 1, LANE), lambda b, d, kk: (d, 0, 0)), pl.BlockSpec((1, 1, LANE), lambda b, d, kk: (d, 0, 0)),
                  pl.BlockSpec((LANE, D_INNER), lambda b, d, kk: (0, 0)),
                  pl.BlockSpec((1, 1, 1, D_INNER, SSD_N), lambda b, d, kk: (d, b, n_ch - 1 - kk, 0, 0)),
                  pl.BlockSpec((1, Q, D_INNER), dymap)] + hosted.specs,
        out_specs=[pl.BlockSpec((1, 1, Q, XBC), lambda b, d, kk: (d, b, cmap(d, kk), 0)),
                   pl.BlockSpec((1, 1, Q, LANE), lambda b, d, kk: (d, b, cmap(d, kk), 0)),
                   pl.BlockSpec((8, LANE), lambda b, d, kk: (0, 0))] + hosted.specs,
        out_shape=[jax.ShapeDtypeStruct((2, nb, T, XBC), F32), jax.ShapeDtypeStruct((2, nb, T, LANE), F32),
                   jax.ShapeDtypeStruct((8, LANE), F32)] + hosted.out_shape,
        scratch_shapes=[pltpu.VMEM((D_INNER, SSD_N), F32), pltpu.VMEM((Q, D_INNER), F32), pltpu.VMEM((Q, D_INNER), F32)] + hosted.scratch,
        compiler_params=_cparams("arbitrary", "arbitrary", "arbitrary"),
    )(xbc, dt2, alog2, dtb2, head_spread_matrix(), hin, dy, *hosted.arrays)


def _adamw(w, g, m, v):
    mn = ADAM_B1 * m + (1.0 - ADAM_B1) * g
    vn = ADAM_B2 * v + (1.0 - ADAM_B2) * jnp.square(g)
    m_hat = mn / (1.0 - ADAM_B1 ** ADAM_STEP)
    v_hat = vn / (1.0 - ADAM_B2 ** ADAM_STEP)
    return -ADAM_LR * (m_hat / (jnp.sqrt(v_hat) + ADAM_EPS) + ADAM_WD * w), mn, vn


def adamw_matrix(name, w, g_slots, m, v):
    K, n = w.shape
    s = g_slots.shape[0]
    tr = _tile(K, 256, 8)

    def body(w_ref, g_ref, m_ref, v_ref, go_ref, d_ref, mo_ref, vo_ref):
        g = g_ref[0].astype(F32)
        for j in range(1, s):
            g = g + g_ref[j].astype(F32)
        go_ref[...] = g
        d_ref[...], mo_ref[...], vo_ref[...] = _adamw(w_ref[...], g, m_ref[...], v_ref[...])

    spec = pl.BlockSpec((tr, n), lambda i: (i, 0))
    return pl.pallas_call(
        body, name=name, grid=(K // tr,),
        in_specs=[spec, pl.BlockSpec((s, tr, n), lambda i: (0, i, 0)), spec, spec], out_specs=[spec] * 4,
        out_shape=[jax.ShapeDtypeStruct((K, n), F32)] * 4,
        compiler_params=_cparams("arbitrary"),
    )(w, g_slots, m, v)


def adamw_small(ws, gs, ms, vs):
    n = len(ws)

    def body(*refs):
        for i in range(n):
            d, mn, vn = _adamw(refs[i][...], refs[n + i][...], refs[2 * n + i][...], refs[3 * n + i][...])
            refs[4 * n + i][...] = d
            refs[5 * n + i][...] = mn
            refs[6 * n + i][...] = vn

    shapes = [jax.ShapeDtypeStruct(w.shape, F32) for w in ws]
    out = pl.pallas_call(body, name="adamw_small", out_shape=shapes * 3)(*ws, *gs, *ms, *vs)
    return out[:n], out[n:2 * n], out[2 * n:]


def sum_slots(name, x):
    n = x.shape[0]

    def fn(t):
        acc = t[0]
        for j in range(1, n):
            acc = acc + t[j]
        return (acc,)

    return ew_call(name, fn, [x], [(x.shape[1:], F32)])[0]


def _pack_rows(parts):
    rows = []
    for p in parts:
        flat = p.reshape(1, -1)
        n = flat.shape[1]
        rows.append(jnp.pad(flat, ((0, 0), (0, -(-n // (8 * LANE)) * 8 * LANE - n))).reshape(-1, LANE))
    return jnp.concatenate(rows, axis=0)


def _unpack_rows(pack, shapes):
    out, r = [], 0
    for s in shapes:
        n = int(np.prod(s))
        nr = -(-n // (8 * LANE)) * 8
        out.append(pack[r:r + nr].reshape(1, -1)[:, :n].reshape(s))
        r += nr
    return out


def _mesh_pos():
    return lax.axis_index("x"), lax.axis_index("y"), lax.axis_index("c")


N_PEERS = N_DEV - 1


def all_gather(name, vs):
    n = len(vs)

    def body(*refs):
        _ag_start(refs[:n], refs[n:2 * n], *refs[2 * n:])
        _ag_finish(refs[:n], refs[n:2 * n], *refs[2 * n:])

    hbm = pl.BlockSpec(memory_space=pl.ANY)
    return pl.pallas_call(
        body, name=name, out_shape=_ag_out_shape(vs), in_specs=[hbm] * n, out_specs=[hbm] * n,
        scratch_shapes=_a2a_scratch(n),
    )(*vs)


def _ag_out_shape(vs):
    return [jax.ShapeDtypeStruct((N_DEV,) + v.shape, v.dtype) for v in vs]


def _ag_copies(x_refs, out_refs, send_sems, recv_sems, local_sems):
    n = len(x_refs)
    x, y, c = _mesh_pos()
    me, sibling = (x, y, c), (x, y, 1 - c)
    chips = [(1 - x, y), (x, 1 - y), (1 - x, 1 - y)]

    def slot(a, px, py, pc):
        return out_refs[a].at[4 * px + 2 * py + pc]

    def copy(a, k, block, to, src=None):
        return pltpu.make_async_remote_copy(
            src_ref=slot(a, *block) if src is None else src, dst_ref=slot(a, *block),
            send_sem=send_sems.at[N_PEERS * a + k], recv_sem=recv_sems.at[N_PEERS * a + k],
            device_id=to, device_id_type=MESH)

    local = [pltpu.make_async_copy(x_refs[a], slot(a, *me), local_sems.at[a]) for a in range(n)]
    first = []
    for a in range(n):
        first.append(copy(a, 0, me, sibling, src=x_refs[a]))
        first += [copy(a, 1 + j, me, (*chip, c), src=x_refs[a]) for j, chip in enumerate(chips)]
    passed = [(copy(a, 1 + j, (*chip, c), me), copy(a, 4 + j, (*chip, c), sibling))
              for j, chip in enumerate(chips) for a in range(n)]
    from_sibling = []
    for a in range(n):
        from_sibling.append(copy(a, 0, sibling, me))
        from_sibling += [copy(a, 4 + j, (*chip, 1 - c), me) for j, chip in enumerate(chips)]
    return local, first, passed, from_sibling


def _ag_start(*refs):
    local, first, _, _ = _ag_copies(*refs)
    for cp in local + first:
        cp.start()


def _ag_finish(*refs):
    local, first, passed, from_sibling = _ag_copies(*refs)
    for arrived, hand_on in passed:
        arrived.wait_recv()
        hand_on.start()
    for cp in from_sibling:
        cp.wait_recv()
    for cp in first + [hand_on for _, hand_on in passed]:
        cp.wait_send()
    for cp in local:
        cp.wait()


def _a2a_scratch(n):
    return [pltpu.SemaphoreType.DMA((N_PEERS * n,)), pltpu.SemaphoreType.DMA((N_PEERS * n,)), pltpu.SemaphoreType.DMA((n,))]


def _a2a_copies(x_refs, out_refs, send_sems, recv_sems, local_sems):
    n = len(x_refs)
    x, y, c = _mesh_pos()
    me = 4 * x + 2 * y + c
    local = [pltpu.make_async_copy(x_refs[a].at[me], out_refs[a].at[me], local_sems.at[a]) for a in range(n)]
    remote = []
    for k in range(1, N_DEV):
        px, py, pc = x ^ ((k >> 2) & 1), y ^ ((k >> 1) & 1), c ^ (k & 1)
        for a in range(n):
            remote.append(pltpu.make_async_remote_copy(
                src_ref=x_refs[a].at[4 * px + 2 * py + pc], dst_ref=out_refs[a].at[me],
                send_sem=send_sems.at[N_PEERS * a + k - 1], recv_sem=recv_sems.at[N_PEERS * a + k - 1],
                device_id=(px, py, pc), device_id_type=MESH))
    return local, remote


def _a2a_start(local, remote):
    for cp in local + remote:
        cp.start()


def _a2a_wait(local, remote):
    for cp in remote:
        cp.wait_recv()
    for cp in remote:
        cp.wait_send()
    for cp in local:
        cp.wait()


class Hosted:
    def __init__(self, start=None, finish=None, arrays=(), out_shape=()):
        self.start, self.finish, self.arrays, self.out_shape = start, finish, list(arrays), list(out_shape)
        self.n = len(self.arrays)
        self.specs = [pl.BlockSpec(memory_space=pl.ANY)] * self.n
        self.scratch = _a2a_scratch(self.n) if self.n else []

    def steps(self, send_refs, recv_refs, sems, first_step, last_step):
        def begin():
            if self.n:
                pl.when(first_step)(lambda: self.start(send_refs, recv_refs, *sems))

        def end():
            if self.n:
                pl.when(last_step)(lambda: self.finish(send_refs, recv_refs, *sems))

        return begin, end


def hosted_all_to_all(vs):
    return Hosted(lambda *r: _a2a_start(*_a2a_copies(*r)), lambda *r: _a2a_wait(*_a2a_copies(*r)), vs,
                  [jax.ShapeDtypeStruct(v.shape, v.dtype) for v in vs])


def hosted_all_gather(vs):
    return Hosted(_ag_start, _ag_finish, vs, _ag_out_shape(vs))


def _taps8(w):
    return jnp.concatenate([w, jnp.zeros((8 - w.shape[0], w.shape[1]), w.dtype)], axis=0)


FIRST = ("w_in",)
LATE_WEIGHTS = ("w_out", "w_up", "w_down", "w_q_up", "w_kv_up")


def first_weights_to_internal(w_in):
    cq, ckv, kr, z, xbc, dt = jnp.split(w_in, np.cumsum(IN_SPLITS)[:-1].tolist(), axis=1)
    K = w_in.shape[0]

    def zeros(n):
        return jnp.zeros((K, n), w_in.dtype)

    w_in_p = jnp.concatenate([cq, zeros(KR_LANE), kr, zeros(LANE - KR_LANE - ROPE), ckv, zeros(OFF_Z - OFF_CKV - KV_RANK),
                              z, xbc, dt, zeros(WIN_P - OFF_DT - 2 * SSD_HEADS)], axis=1)
    return dict(w_in_p=w_in_p)


def late_weights_to_internal(w_out, w_up, w_down, w_q_up, w_kv_up):
    attn_rows = w_out[:N_HEADS * V_DIM].reshape(N_HEADS, V_DIM, -1)
    w_out_p = jnp.concatenate([jnp.pad(attn_rows, ((0, 0), (HEAD_BLOCK - V_DIM, 0), (0, 0))).reshape(QP, -1),
                               w_out[N_HEADS * V_DIM:]], axis=0)
    w_q_p = jnp.pad(w_q_up.reshape(Q_RANK, N_HEADS, NOPE + ROPE), ((0, 0), (0, 0), (0, HEAD_BLOCK - NOPE - ROPE))).reshape(Q_RANK, QP)
    return dict(w_out_p=w_out_p, w_up=glu_interleave(w_up), w_down=w_down, w_q_p=w_q_p, w_kv=w_kv_up)


def _q_grad(g_q_p):
    return g_q_p.reshape(Q_RANK, N_HEADS, HEAD_BLOCK)[:, :, :NOPE + ROPE].reshape(Q_RANK, -1)


def _out_grad(g_out_p):
    return jnp.concatenate([g_out_p[:QP].reshape(N_HEADS, HEAD_BLOCK, -1)[:, HEAD_BLOCK - V_DIM:].reshape(N_HEADS * V_DIM, -1),
                            g_out_p[QP:]], axis=0)


EARLY = ("w_out", "w_up", "w_down", "w_q_up", "w_kv_up")


def local_step(x, ctx, target, mod_x, mod_c, W, late_shards, V):
    nb, S, D = x.shape
    C = ctx.shape[1]
    T = C + S
    tr = _tile(math.gcd(C, S), 256, 8)
    tq = _tile(S, 256, 8)
    tc = 256
    cblk = C // tr
    m = [mod_x[:, i * D:(i + 1) * D][:, None, :] for i in range(N_MOD)]
    mc = [mod_c[:, i * D:(i + 1) * D] for i in range(2)]
    ssd_w8, ffn_w8 = _taps8(V["ssd_conv_w"]), _taps8(V["ffn_conv_w"])
    dexp = jnp.repeat(V["ssd_d"].reshape(-1), SSD_P).reshape(1, D_INNER)
    cosT, sinT = rope_tables(C, S)
    cosS, sinS = cosT[C:], sinT[C:]

    (h1x,) = rows_fwd("prenorm_x", fn_prenorm, nb, S // tr, tr, [(x, D, 0, 0)], [m[0], m[1]], [V["mix_pre_norm"]], [(D, BF16)])
    (h1c,) = rows_fwd("prenorm_c", fn_prenorm, nb, C // tr, tr, [(ctx, D, 0, 0)], [], [mc[0], mc[1], V["mix_pre_norm"]], [(D, BF16)])
    h1 = jnp.concatenate([h1c, h1x], axis=1).reshape(nb * T, D)
    u = matmul("in_proj", [(h1, W["w_in_p"])], "nn", F32).reshape(nb, T, WIN_P)
    xbc = ssd_conv_fwd(u, ssd_w8, V["ssd_conv_b"], C, tc)
    dt2, alog2, dtb2 = ssd_dt_inputs(u, V["ssd_a_log"], V["ssd_dt_bias"])
    y2, hin, *late = ssd_fwd(xbc, dt2, alog2, dtb2, C, hosted_all_gather(late_shards))
    W = dict(W, **late_weights_to_internal(*[_whole(s, n) for s, n in zip(late, LATE_WEIGHTS)]))
    y2 = y2.reshape(2 * nb, S, D_INNER)
    (qn,) = rows_fwd("q_norm", fn_rms, nb, S // tr, tr, [(u, Q_RANK, OFF_CQ // Q_RANK, cblk)], [], [V["q_norm"]], [(Q_RANK, BF16)])
    (kvn,) = rows_fwd("kv_norm", fn_rms, nb, T // tr, tr, [(u, KV_RANK, OFF_CKV // KV_RANK, 0)], [], [V["kv_norm"]], [(KV_RANK, BF16)])
    qn2, kvn2 = qn.reshape(nb * S, Q_RANK), kvn.reshape(nb * T, KV_RANK)
    q_raw = matmul("q_up", [(qn2, W["w_q_p"])], "nn", F32).reshape(nb, S, QP)
    kv = matmul("kv_up", [(kvn2, W["w_kv"])], "nn", BF16).reshape(nb, T, QP)
    cos_q, sin_q = cosS * Q_PRESCALE, sinS * Q_PRESCALE
    kr = rope_call("rope_k", u, LANE, OFF_KR // LANE, cosT, sinT, BF16, tr)
    o = attn_fwd(q_raw, kv, kr, cos_q, sin_q, tq)
    fin_rows = [(y2, D_INNER, 0, 0, 0), (y2, D_INNER, 0, 0, nb), (xbc, D_INNER, 0, cblk), (u, D_INNER, OFF_Z // D_INNER, cblk)]
    fin_gl = [dexp, V["ssd_norm"]]
    (ssd,) = rows_fwd("ssd_finish", fn_ssd_finish, nb, S // tr, tr, fin_rows, [], fin_gl, [(D_INNER, BF16)])
    o2, ssd2 = o.reshape(nb * S, QP), ssd.reshape(nb * S, D_INNER)
    mix = matmul("out_proj", [(o2, W["w_out_p"][:QP]), (ssd2, W["w_out_p"][QP:])], "nn", F32).reshape(nb, S, D)
    pm_rows = [(x, D, 0, 0), (mix, D, 0, 0)]
    pm_pb = [m[2], m[4], m[3]]
    pm_gl = [V["mix_post_norm"], V["ffn_pre_norm"]]
    x1, h2 = rows_fwd("postmix", fn_postmix, nb, S // tr, tr, pm_rows, pm_pb, pm_gl, [(D, F32), (D, BF16)])
    h22 = h2.reshape(nb * S, D)
    up = matmul("up_proj", [(h22, W["w_up"])], "nn", F32).reshape(nb, S, 2 * D_FF)
    act = glu_fwd(up, ffn_w8, V["ffn_conv_b"])
    act2 = act.reshape(nb * S, D_FF)
    ffn = matmul("down_proj", [(act2, W["w_down"])], "nn", F32).reshape(nb, S, D)
    dx1, dffn, dgate2, d_ffn_post, loss = final_call(x1, ffn, target, m[5], V["ffn_post_norm"], tr)

    dffn2 = dffn.reshape(nb * S, D)
    dact = matmul("down_dgrad", [(dffn2, W["w_down"])], "nt", BF16).reshape(nb, S, D_FF)
    g_down = matmul_tn("down_wgrad", act2, dffn2)
    dup, ffn_rows = glu_bwd(up, ffn_w8, V["ffn_conv_b"], dact)
    dup2 = dup.reshape(nb * S, 2 * D_FF)
    dh2 = matmul("up_dgrad", [(dup2, W["w_up"])], "nt", BF16).reshape(nb, S, D)
    g_up = matmul_tn("up_wgrad", h22, dup2)
    dx_a, dmix, dgate1, dscale2, dshift2, d_mix_post, d_ffn_pre = rows_bwd(
        "postmix_bwd", fn_postmix, nb, S // tr, tr, pm_rows, pm_pb, pm_gl,
        [(dx1, D, 0, 0), (dh2, D, 0, 0)], [(0, F32), (1, BF16)])
    dmix2 = dmix.reshape(nb * S, D)
    dcat = matmul("out_dgrad", [(dmix2, W["w_out_p"])], "nt", BF16).reshape(nb, S, QP + D_INNER)
    g_out_p = jnp.concatenate([matmul_tn("out_wgrad_attn", o2, dmix2), matmul_tn("out_wgrad_ssd", ssd2, dmix2)], axis=0)
    dy, dxs_direct, dz, d_dexp, d_ssd_norm = rows_bwd(
        "ssd_finish_bwd", fn_ssd_finish, nb, S // tr, tr, fin_rows, [], fin_gl,
        [(dcat, D_INNER, QP // D_INNER, 0)], [(0, F32), (2, F32), (3, BF16)])
    dq_pre, dkv, dkr = attn_bwd(q_raw, kv, kr, dcat, cos_q, sin_q, cosS, sinS, tq)
    dq_pre = dq_pre.reshape(nb * S, QP)
    dkr_pre = rope_call("rope_dk", dkr, LANE, 0, cosT, -sinT, BF16, tr)
    dkv2 = dkv.reshape(nb * T, QP)
    dqn = matmul("q_dgrad", [(dq_pre, W["w_q_p"])], "nt", F32).reshape(nb, S, Q_RANK)
    g_q_p = matmul_tn("q_wgrad", qn2, dq_pre)
    dkvn = matmul("kv_dgrad", [(dkv2, W["w_kv"])], "nt", F32).reshape(nb, T, KV_RANK)
    g_kv = matmul_tn("kv_wgrad", kvn2, dkv2)
    early_grads = (_out_grad(g_out_p), glu_deinterleave(g_up), g_down, _q_grad(g_q_p), g_kv)
    early = hosted_all_to_all([_per_device(g, n) for g, n in zip(early_grads, EARLY)])
    dxbc2, ddt2, ssd_stats, *received = ssd_bwd(xbc, dt2, alog2, dtb2, hin, dy, C, early)
    ddt_block = jnp.concatenate([ddt2[0][..., :SSD_HEADS], ddt2[1][..., :SSD_HEADS],
                                 jnp.zeros((nb, T, LANE - 2 * SSD_HEADS), F32)], axis=-1).astype(BF16)
    dxbc_raw, ssd_rows = ssd_conv_bwd(u, ssd_w8, V["ssd_conv_b"], dxbc2, dxs_direct, C, tc)
    dcq, d_q_norm = rows_bwd("q_norm_bwd", fn_rms, nb, S // tr, tr, [(u, Q_RANK, OFF_CQ // Q_RANK, cblk)], [], [V["q_norm"]],
                             [(dqn, Q_RANK, 0, 0)], [(0, BF16)])
    dckv, d_kv_norm = rows_bwd("kv_norm_bwd", fn_rms, nb, T // tr, tr, [(u, KV_RANK, OFF_CKV // KV_RANK, 0)], [], [V["kv_norm"]],
                               [(dkvn, KV_RANK, 0, 0)], [(0, BF16)])

    def ctx_rows(t):
        return jnp.pad(t, ((0, 0), (C, 0), (0, 0)))

    du = [("cq", ctx_rows(dcq), OFF_CQ, Q_RANK), ("kr", dkr_pre, OFF_KR, LANE), ("ckv", dckv, OFF_CKV, KV_RANK),
          ("z", ctx_rows(dz), OFF_Z, D_INNER), ("xbc", dxbc_raw, OFF_XBC, XBC), ("dt", ddt_block, OFF_DT, LANE)]
    du = [(name, t.reshape(nb * T, w), off, w) for (name, t, off, w) in du]
    g = {name: matmul_tn("in_wgrad_" + name, h1, t) for (name, t, _, _) in du}
    g_in = jnp.concatenate([g["cq"], g["ckv"], g["kr"][:, KR_LANE:KR_LANE + ROPE], g["z"], g["xbc"],
                            g["dt"][:, :2 * SSD_HEADS]], axis=1)
    dh1, received_in = matmul("in_dgrad", [(t, W["w_in_p"][:, off:off + w]) for (_, t, off, w) in du], "nt", BF16,
                              hosted=hosted_all_to_all([_per_device(g_in, "w_in").astype(BF16)]))
    dh1 = dh1.reshape(nb, T, D)

    def fn_prenorm_res(xv, shift, scale, g):
        return fn_prenorm(xv, shift, scale, g) + (xv,)

    grad_x, dshift1, dscale1, d_mix_pre_x = rows_bwd(
        "prenorm_x_bwd", fn_prenorm_res, nb, S // tr, tr, [(x, D, 0, 0)], [m[0], m[1]], [V["mix_pre_norm"]],
        [(dh1, D, 0, cblk), (dx_a, D, 0, 0)], [(0, F32)])
    dshift_c, dscale_c, d_mix_pre_c = rows_bwd(
        "prenorm_c_bwd", fn_prenorm, nb, C // tr, tr, [(ctx, D, 0, 0)], [], [mc[0], mc[1], V["mix_pre_norm"]],
        [(dh1, D, 0, 0)], [])

    dmod_x = jnp.concatenate([dshift1, dscale1, dgate1, dshift2, dscale2, dgate2], axis=-1).reshape(nb, N_MOD * D)
    dmod_c = jnp.concatenate([dshift_c, dscale_c, jnp.zeros((1, (N_MOD - 2) * D), F32)], axis=-1)
    gv = dict(
        mix_pre_norm=d_mix_pre_x + d_mix_pre_c, mix_post_norm=d_mix_post, q_norm=d_q_norm, kv_norm=d_kv_norm,
        ssd_conv_w=ssd_rows[:SSD_K], ssd_conv_b=ssd_rows[SSD_K:SSD_K + 1],
        ssd_a_log=jnp.concatenate([ssd_stats[0:1, :SSD_HEADS], ssd_stats[2:3, :SSD_HEADS]], axis=1),
        ssd_dt_bias=jnp.concatenate([ssd_stats[1:2, :SSD_HEADS], ssd_stats[3:4, :SSD_HEADS]], axis=1),
        ssd_d=jnp.sum(d_dexp.reshape(SSD_HEADS, SSD_P), axis=1).reshape(1, SSD_HEADS), ssd_norm=d_ssd_norm,
        ffn_pre_norm=d_ffn_pre, ffn_post_norm=d_ffn_post,
        ffn_conv_w=ffn_rows[:FFN_K], ffn_conv_b=ffn_rows[FFN_K:FFN_K + 1])
    return loss, grad_x, dmod_x, dmod_c, gv, dict(zip(EARLY, received), w_in=received_in)


WEIGHT_ORDER = ("c_ctx", "w_mod", "b_mod", "mix_pre_norm", "mix_post_norm", "w_in", "q_norm", "w_q_up", "kv_norm",
                "w_kv_up", "ssd_conv_w", "ssd_conv_b", "ssd_a_log", "ssd_dt_bias", "ssd_d", "ssd_norm", "w_out",
                "ffn_pre_norm", "ffn_post_norm", "w_up", "ffn_conv_w", "ffn_conv_b", "w_down")
MATRICES = ("w_in", "w_q_up", "w_kv_up", "w_out", "w_up", "w_down")
ROW_SHARDED = ("w_out", "w_down")
SMALL_SUMMED = ("c_ctx", "mix_pre_norm", "mix_post_norm", "q_norm", "kv_norm", "ssd_conv_w", "ssd_conv_b", "ssd_a_log",
                "ssd_dt_bias", "ssd_d", "ssd_norm", "ffn_pre_norm", "ffn_post_norm", "ffn_conv_w", "ffn_conv_b")
MOD_ROWS = 8


def _whole(shards, name):
    if name in ROW_SHARDED:
        return shards.reshape(-1, shards.shape[-1])
    return jnp.concatenate([shards[j] for j in range(N_DEV)], axis=1)


def _per_device(g, name):
    if name in ROW_SHARDED:
        return g.reshape(N_DEV, -1, g.shape[-1])
    return jnp.stack(jnp.split(g, N_DEV, axis=1))


def kernel(x, c, ctx, c_ctx, w_mod, b_mod, mix_pre_norm, mix_post_norm, w_in, q_norm, w_q_up, kv_norm, w_kv_up, ssd_conv_w, ssd_conv_b, ssd_a_log, ssd_dt_bias, ssd_d, ssd_norm, w_out, ffn_pre_norm, ffn_post_norm, w_up, ffn_conv_w, ffn_conv_b, w_down, loss_target, m_c_ctx, m_w_mod, m_b_mod, m_mix_pre_norm, m_mix_post_norm, m_w_in, m_q_norm, m_w_q_up, m_kv_norm, m_w_kv_up, m_ssd_conv_w, m_ssd_conv_b, m_ssd_a_log, m_ssd_dt_bias, m_ssd_d, m_ssd_norm, m_w_out, m_ffn_pre_norm, m_ffn_post_norm, m_w_up, m_ffn_conv_w, m_ffn_conv_b, m_w_down, v_c_ctx, v_w_mod, v_b_mod, v_mix_pre_norm, v_mix_post_norm, v_w_in, v_q_norm, v_w_q_up, v_kv_norm, v_w_kv_up, v_ssd_conv_w, v_ssd_conv_b, v_ssd_a_log, v_ssd_dt_bias, v_ssd_d, v_ssd_norm, v_w_out, v_ffn_pre_norm, v_ffn_post_norm, v_w_up, v_ffn_conv_w, v_ffn_conv_b, v_w_down):
    weights = dict(c_ctx=c_ctx, w_mod=w_mod, b_mod=b_mod, mix_pre_norm=mix_pre_norm, mix_post_norm=mix_post_norm, w_in=w_in, q_norm=q_norm, w_q_up=w_q_up, kv_norm=kv_norm, w_kv_up=w_kv_up, ssd_conv_w=ssd_conv_w, ssd_conv_b=ssd_conv_b, ssd_a_log=ssd_a_log, ssd_dt_bias=ssd_dt_bias, ssd_d=ssd_d, ssd_norm=ssd_norm, w_out=w_out, ffn_pre_norm=ffn_pre_norm, ffn_post_norm=ffn_post_norm, w_up=w_up, ffn_conv_w=ffn_conv_w, ffn_conv_b=ffn_conv_b, w_down=w_down)
    mom1 = dict(c_ctx=m_c_ctx, w_mod=m_w_mod, b_mod=m_b_mod, mix_pre_norm=m_mix_pre_norm, mix_post_norm=m_mix_post_norm, w_in=m_w_in, q_norm=m_q_norm, w_q_up=m_w_q_up, kv_norm=m_kv_norm, w_kv_up=m_w_kv_up, ssd_conv_w=m_ssd_conv_w, ssd_conv_b=m_ssd_conv_b, ssd_a_log=m_ssd_a_log, ssd_dt_bias=m_ssd_dt_bias, ssd_d=m_ssd_d, ssd_norm=m_ssd_norm, w_out=m_w_out, ffn_pre_norm=m_ffn_pre_norm, ffn_post_norm=m_ffn_post_norm, w_up=m_w_up, ffn_conv_w=m_ffn_conv_w, ffn_conv_b=m_ffn_conv_b, w_down=m_w_down)
    mom2 = dict(c_ctx=v_c_ctx, w_mod=v_w_mod, b_mod=v_b_mod, mix_pre_norm=v_mix_pre_norm, mix_post_norm=v_mix_post_norm, w_in=v_w_in, q_norm=v_q_norm, w_q_up=v_w_q_up, kv_norm=v_kv_norm, w_kv_up=v_w_kv_up, ssd_conv_w=v_ssd_conv_w, ssd_conv_b=v_ssd_conv_b, ssd_a_log=v_ssd_a_log, ssd_dt_bias=v_ssd_dt_bias, ssd_d=v_ssd_d, ssd_norm=v_ssd_norm, w_out=v_w_out, ffn_pre_norm=v_ffn_pre_norm, ffn_post_norm=v_ffn_post_norm, w_up=v_w_up, ffn_conv_w=v_ffn_conv_w, ffn_conv_b=v_ffn_conv_b, w_down=v_w_down)
    nb, S, D = x.shape
    me = 4 * lax.axis_index("x") + 2 * lax.axis_index("y") + lax.axis_index("c")

    *first, c_all, ssd_w_sh, ffn_w_sh = all_gather(
        "gather_first", [weights[n][0].astype(BF16) for n in FIRST] + [c, ssd_conv_w[0], ffn_conv_w[0]])
    W = first_weights_to_internal(*[_whole(s, n) for n, s in zip(FIRST, first)])
    late_shards = [weights[n][0].astype(BF16) for n in LATE_WEIGHTS]
    V = {n: weights[n].reshape(1, -1) for n in SMALL_SUMMED if n != "c_ctx"}
    V["ssd_conv_w"] = _whole(ssd_w_sh, "ssd_conv_w")
    V["ffn_conv_w"] = _whole(ffn_w_sh, "ffn_conv_w")

    n_all = N_DEV * nb
    mod_rows = -(-(n_all + 1) // 8) * 8
    c_pad = jnp.concatenate([c_all.reshape(n_all, D), c_ctx.reshape(1, D), jnp.zeros((mod_rows - n_all - 1, D), F32)], axis=0)
    mod_cols = w_mod.shape[2]
    b_mine = lax.dynamic_slice(b_mod, (0, me * mod_cols), (1, mod_cols))
    mod_part = matmul("mod_proj", [(c_pad, w_mod[0])], "nn", F32, bias=b_mine, silu_a=True)
    mod_all = _whole(all_gather("gather_mod", [mod_part])[0], "w_mod")
    mod_x = lax.dynamic_slice(mod_all, (me * nb, 0), (nb, mod_all.shape[1]))
    mod_c = mod_all[n_all:n_all + 1]

    loss, grad_x, dmod_x, dmod_c, gv, slots = local_step(x, ctx, loss_target, mod_x, mod_c, W, late_shards, V)

    dmod_mine = jnp.concatenate([dmod_x, dmod_c, jnp.zeros((MOD_ROWS - nb - 1, dmod_x.shape[1]), F32)], axis=0)
    dmod_all = all_gather("gather_dmod", [dmod_mine])[0]
    dmod_ctx = sum_slots("sum_dmod_ctx", dmod_all[:, nb:nb + 1].reshape(N_DEV, -1, LANE)).reshape(1, -1)
    dmod_full = jnp.concatenate([dmod_all[:, :nb].reshape(n_all, -1), dmod_ctx,
                                 jnp.zeros((mod_rows - n_all - 1, dmod_ctx.shape[1]), F32)], axis=0)
    (g_b_mod,) = ew_call("mod_bias_grad", lambda t: (jnp.sum(t, axis=0, keepdims=True),), [dmod_full], [((1, dmod_full.shape[1]), F32)])
    dmod_cols = lax.dynamic_slice(dmod_full, (0, me * mod_cols), (mod_rows, mod_cols))
    g_w_mod = matmul_tn("mod_wgrad", c_pad, dmod_cols, silu_a=True)
    dsilu_ctx = matmul("mod_dgrad_ctx", [(dmod_cols[n_all:n_all + 8], w_mod[0])], "nt", F32)[0:1]

    def silu_vjp(cc, ct):
        return (jax.vjp(_silu, cc)[1](ct)[0],)

    (g_c_ctx_part,) = ew_call("c_ctx_grad", silu_vjp, [c_ctx.reshape(1, D), dsilu_ctx], [((1, D), F32)])

    gv = dict(gv, c_ctx=g_c_ctx_part)
    small_parts = [loss] + [gv[n] for n in SMALL_SUMMED]
    small_sum = sum_slots("sum_small", all_gather("gather_small_grads", [_pack_rows(small_parts)])[0])
    summed = _unpack_rows(small_sum, [p.shape for p in small_parts])
    loss_out = summed[0][0, 0]
    grads = {n: g.reshape(weights[n].shape) if n not in ("ssd_conv_w", "ffn_conv_w") else g for n, g in zip(SMALL_SUMMED, summed[1:])}
    for n in ("ssd_conv_w", "ffn_conv_w"):
        cols = weights[n].shape[2]
        grads[n] = lax.dynamic_slice(grads[n], (0, me * cols), (grads[n].shape[0], cols)).reshape(weights[n].shape)
    grads["b_mod"] = g_b_mod.reshape(b_mod.shape)

    slots = dict(slots, w_mod=g_w_mod[None])
    delta, new_m, new_v = {}, {}, {}
    for n in MATRICES + ("w_mod",):
        g, d, mn, vn = adamw_matrix("adamw_" + n, weights[n][0], slots[n], mom1[n][0], mom2[n][0])
        grads[n], delta[n], new_m[n], new_v[n] = [t.reshape(weights[n].shape) for t in (g, d, mn, vn)]
    small = [n for n in WEIGHT_ORDER if n not in slots]

    def two_d(t):
        return t.reshape(-1, t.shape[-1])

    ds, ms, vs = adamw_small(*[[two_d(t[n]) for n in small] for t in (weights, grads, mom1, mom2)])
    for n, d, mn, vn in zip(small, ds, ms, vs):
        delta[n], new_m[n], new_v[n] = [t.reshape(weights[n].shape) for t in (d, mn, vn)]
    return (loss_out, grad_x, *[t[n] for t in (grads, delta, new_m, new_v) for n in WEIGHT_ORDER])
```

```python
import math

import jax
import jax.numpy as jnp
import numpy as np
from jax import lax
from jax.experimental import pallas as pl
from jax.experimental.pallas import tpu as pltpu

F32 = jnp.float32
BF16 = jnp.bfloat16
MESH = pl.DeviceIdType.MESH

D_MODEL = 1024
GRID_W = 64
N_HEADS = 16
NOPE = 64
ROPE = 32
V_DIM = 64
Q_RANK = 384
KV_RANK = 256
ROPE_THETA = 10000.0
ATTN_SCALE = (NOPE + ROPE) ** -0.5
SSD_HEADS = 16
SSD_P = 64
SSD_GROUPS = 2
SSD_N = 128
SSD_K = 5
CHUNK = 128
D_INNER = SSD_HEADS * SSD_P
GN = SSD_GROUPS * SSD_N
XBC = D_INNER + 2 * GN
D_FF = 2816
FFN_K = 3
N_MOD = 6
EPS = 1e-6
IN_SPLITS = (Q_RANK, KV_RANK, ROPE, D_INNER, XBC, 2 * SSD_HEADS)
IN_WIDTH = sum(IN_SPLITS)
N_DEV = 8

ADAM_LR = 0.001
ADAM_B1 = 0.9
ADAM_B2 = 0.999
ADAM_EPS = 1e-08
ADAM_WD = 0.01
ADAM_STEP = 10

LANE = 128
HEAD_BLOCK = 128
OFF_CQ = 0
OFF_KR = 384
OFF_CKV = 512
OFF_Z = 1024
OFF_XBC = 2048
OFF_DT = 3584
WIN_P = 3840
KR_LANE = 64
QP = N_HEADS * HEAD_BLOCK

VMEM_LIMIT_V7X = 56 * 1024 * 1024
NEG_BIG = -1e30


def _cparams(*sem):
    return pltpu.CompilerParams(dimension_semantics=sem, vmem_limit_bytes=VMEM_LIMIT_V7X)


def _tile(n, target, mult=128):
    if n <= target:
        return n
    t = (target // mult) * mult
    while t >= mult:
        if n % t == 0:
            return t
        t -= mult
    return n


def _silu(x):
    return x * jax.nn.sigmoid(x)


def _rms(x, g):
    return x * lax.rsqrt(jnp.mean(x * x, axis=-1, keepdims=True) + EPS) * g


WHOLE_K_WIDE = 2048


def matmul(name, pairs, mode, out_dtype, *, bias=None, silu_a=False, hosted=None):
    n_pairs = len(pairs)
    M = pairs[0][0].shape[0]
    N = pairs[0][1].shape[1] if mode == "nn" else pairs[0][1].shape[0]
    k_total = sum(a.shape[1] for a, _ in pairs)
    tm = _tile(M, 1024 if k_total <= WHOLE_K_WIDE else 512, 8)
    tn = _tile(N, 2816 if k_total <= WHOLE_K_WIDE else 1024)
    dims = (((1,), (0,)), ((), ())) if mode == "nn" else (((1,), (1,)), ((), ()))
    n_own = 2 * n_pairs + (bias is not None)
    n_ex = hosted.n if hosted else 0

    def body(*refs):
        o_ref = refs[n_own + n_ex]
        if hosted:
            j, i = pl.program_id(0), pl.program_id(1)
            begin_exchange, end_exchange = hosted.steps(
                refs[n_own:n_own + n_ex], refs[n_own + n_ex + 1:n_own + 2 * n_ex + 1], refs[n_own + 2 * n_ex + 1:],
                jnp.logical_and(j == 0, i == 0), jnp.logical_and(j == N // tn - 1, i == M // tm - 1))
            begin_exchange()
        acc = None
        for p in range(n_pairs):
            a = refs[2 * p][...]
            if silu_a:
                a = _silu(a.astype(F32))
            d = lax.dot_general(a.astype(BF16), refs[2 * p + 1][...].astype(BF16), dims, preferred_element_type=F32)
            acc = d if acc is None else acc + d
        if bias is not None:
            acc = acc + refs[2 * n_pairs][...]
        o_ref[...] = acc.astype(o_ref.dtype)
        if hosted:
            end_exchange()

    in_specs, args = [], []
    for a, b in pairs:
        K = a.shape[1]
        in_specs.append(pl.BlockSpec((tm, K), lambda j, i: (i, 0)))
        in_specs.append(pl.BlockSpec((K, tn), lambda j, i: (0, j)) if mode == "nn" else pl.BlockSpec((tn, K), lambda j, i: (j, 0)))
        args += [a, b]
    if bias is not None:
        in_specs.append(pl.BlockSpec((1, tn), lambda j, i: (0, j)))
        args.append(bias)
    out_spec = pl.BlockSpec((tm, tn), lambda j, i: (i, j))
    out_shape = jax.ShapeDtypeStruct((M, N), out_dtype)
    if not hosted:
        return pl.pallas_call(
            body, name=name, grid=(N // tn, M // tm), in_specs=in_specs, out_specs=out_spec, out_shape=out_shape,
            compiler_params=_cparams("arbitrary", "arbitrary"),
        )(*args)
    return pl.pallas_call(
        body, name=name, grid=(N // tn, M // tm), in_specs=in_specs + hosted.specs,
        out_specs=[out_spec] + hosted.specs, out_shape=[out_shape] + hosted.out_shape, scratch_shapes=hosted.scratch,
        compiler_params=_cparams("arbitrary", "arbitrary"),
    )(*args, *hosted.arrays)


def matmul_tn(name, a, b, out_dtype=F32, *, silu_a=False, tm=1408, tn=2048, tk=2048):
    R, M = a.shape
    N = b.shape[1]
    tm = _tile(M, tm)
    tn = _tile(N, tn)
    tk = _tile(R, tk, 8)
    nk = R // tk

    def body(a_ref, b_ref, o_ref, acc):
        k = pl.program_id(2)

        @pl.when(k == 0)
        def _():
            acc[...] = jnp.zeros_like(acc)

        x = a_ref[...]
        if silu_a:
            x = _silu(x.astype(F32))
        acc[...] += lax.dot_general(x.astype(BF16), b_ref[...].astype(BF16), (((0,), (0,)), ((), ())),
                                    preferred_element_type=F32)

        @pl.when(k == nk - 1)
        def _():
            o_ref[...] = acc[...].astype(o_ref.dtype)

    return pl.pallas_call(
        body, name=name, grid=(M // tm, N // tn, nk),
        in_specs=[pl.BlockSpec((tk, tm), lambda i, j, k: (k, i)), pl.BlockSpec((tk, tn), lambda i, j, k: (k, j))],
        out_specs=pl.BlockSpec((tm, tn), lambda i, j, k: (i, j)),
        out_shape=jax.ShapeDtypeStruct((M, N), out_dtype),
        scratch_shapes=[pltpu.VMEM((tm, tn), F32)],
        compiler_params=_cparams("arbitrary", "arbitrary", "arbitrary"),
    )(a, b)


def _row_specs(rin, pbin, glin, tr):
    specs = [pl.BlockSpec((1, tr, w), lambda b, i, cb=cb, ro=ro, bo=(e[4] if len(e) > 4 else 0): (b + bo, i + ro, cb))
             for e in rin for (_, w, cb, ro) in [e[:4]]]
    specs += [pl.BlockSpec((1, 1, a.shape[-1]), lambda b, i: (b, 0, 0)) for a in pbin]
    specs += [pl.BlockSpec((1, a.shape[-1]), lambda b, i: (0, 0)) for a in glin]
    return specs


def rows_fwd(name, fn, nb, nblk, tr, rin, pbin, glin, outs):
    nr, npb, ngl = len(rin), len(pbin), len(glin)
    n_in = nr + npb + ngl

    def body(*refs):
        args = [r[0].astype(F32) for r in refs[:nr + npb]] + [r[...] for r in refs[nr + npb:n_in]]
        res = fn(*args)
        for o, v in zip(refs[n_in:], res):
            o[0] = v.astype(o.dtype)

    return pl.pallas_call(
        body, name=name, grid=(nb, nblk), in_specs=_row_specs(rin, pbin, glin, tr),
        out_specs=[pl.BlockSpec((1, tr, w), lambda b, i: (b, i, 0)) for (w, _) in outs],
        out_shape=[jax.ShapeDtypeStruct((nb, nblk * tr, w), dt) for (w, dt) in outs],
        compiler_params=_cparams("arbitrary", "arbitrary"),
    )(*[e[0] for e in rin], *pbin, *glin)


def rows_bwd(name, fn, nb, nblk, tr, rin, pbin, glin, cts, want):
    nr, npb, ngl, nct = len(rin), len(pbin), len(glin), len(cts)
    n_in = nr + npb + ngl

    def body(*refs):
        b, i = pl.program_id(0), pl.program_id(1)
        args = [r[0].astype(F32) for r in refs[:nr + npb]] + [r[...] for r in refs[nr + npb:n_in]]
        ct = tuple(r[0].astype(F32) for r in refs[n_in:n_in + nct])
        _, vjp = jax.vjp(fn, *args)
        g = vjp(ct)
        orefs = refs[n_in + nct:]
        for o, (idx, _) in zip(orefs, want):
            o[0] = g[idx].astype(o.dtype)
        pb_refs = orefs[len(want):len(want) + npb]
        gl_refs = orefs[len(want) + npb:]

        @pl.when(i == 0)
        def _():
            for o, v in zip(pb_refs, g[nr:nr + npb]):
                o[0] = v

        @pl.when(i > 0)
        def _():
            for o, v in zip(pb_refs, g[nr:nr + npb]):
                o[0] += v

        first = jnp.logical_and(b == 0, i == 0)

        @pl.when(first)
        def _():
            for o, v in zip(gl_refs, g[nr + npb:]):
                o[...] = v

        @pl.when(jnp.logical_not(first))
        def _():
            for o, v in zip(gl_refs, g[nr + npb:]):
                o[...] += v

    out_specs = [pl.BlockSpec((1, tr, rin[idx][1]), lambda b, i: (b, i, 0)) for (idx, _) in want]
    out_shape = [jax.ShapeDtypeStruct((nb, nblk * tr, rin[idx][1]), dt) for (idx, dt) in want]
    out_specs += [pl.BlockSpec((1, 1, a.shape[-1]), lambda b, i: (b, 0, 0)) for a in pbin]
    out_shape += [jax.ShapeDtypeStruct((nb, 1, a.shape[-1]), F32) for a in pbin]
    out_specs += [pl.BlockSpec((1, a.shape[-1]), lambda b, i: (0, 0)) for a in glin]
    out_shape += [jax.ShapeDtypeStruct((1, a.shape[-1]), F32) for a in glin]
    return pl.pallas_call(
        body, name=name, grid=(nb, nblk),
        in_specs=_row_specs(rin, pbin, glin, tr) + _row_specs(cts, [], [], tr),
        out_specs=out_specs, out_shape=out_shape,
        compiler_params=_cparams("arbitrary", "arbitrary"),
    )(*[e[0] for e in rin], *pbin, *glin, *[e[0] for e in cts])


def ew_call(name, fn, ins, outs):
    def body(*refs):
        res = fn(*[r[...] for r in refs[:len(ins)]])
        for o, v in zip(refs[len(ins):], res):
            o[...] = v.astype(o.dtype)

    return pl.pallas_call(body, name=name, out_shape=[jax.ShapeDtypeStruct(s, dt) for (s, dt) in outs])(*ins)


def fn_prenorm(x, shift, scale, g):
    return (_rms(x, g) * (1.0 + scale) + shift,)


def fn_rms(x, g):
    return (_rms(x, g),)


def fn_ssd_finish(yf, yr, xs, z, dexp, nw):
    y = yf + yr + dexp * xs
    return (_rms(y * _silu(z), nw),)


def fn_postmix(x, mix, gate1, scale2, shift2, post_g, pre_g):
    x1 = x + gate1 * _rms(mix, post_g)
    h2 = _rms(x1, pre_g) * (1.0 + scale2) + shift2
    return x1, h2


def final_call(x1, ffn, target, gate2, post_g, tr):
    nb, S, D = x1.shape
    nblk = S // tr

    def body(x1_ref, f_ref, t_ref, g2_ref, pg_ref, dx1_ref, df_ref, dg2_ref, dpg_ref, loss_ref):
        b, i = pl.program_id(0), pl.program_id(1)
        tgt = t_ref[0]

        def lossfn(x1v, fv, g2, pg):
            e = x1v + g2 * _rms(fv, pg) - tgt
            return 0.5 * jnp.sum(jnp.mean(e * e, axis=-1, keepdims=True))

        val, (dx1, df, dg2, dpg) = jax.value_and_grad(lossfn, argnums=(0, 1, 2, 3))(
            x1_ref[0], f_ref[0].astype(F32), g2_ref[0], pg_ref[...])
        dx1_ref[0] = dx1
        df_ref[0] = df.astype(df_ref.dtype)
        lv = jnp.full((1, LANE), val, F32)

        @pl.when(i == 0)
        def _():
            dg2_ref[0] = dg2

        @pl.when(i > 0)
        def _():
            dg2_ref[0] += dg2

        first = jnp.logical_and(b == 0, i == 0)

        @pl.when(first)
        def _():
            dpg_ref[...] = dpg
            loss_ref[...] = lv

        @pl.when(jnp.logical_not(first))
        def _():
            dpg_ref[...] += dpg
            loss_ref[...] += lv

    row = pl.BlockSpec((1, tr, D), lambda b, i: (b, i, 0))
    pb = pl.BlockSpec((1, 1, D), lambda b, i: (b, 0, 0))
    gl = pl.BlockSpec((1, D), lambda b, i: (0, 0))
    return pl.pallas_call(
        body, name="loss_head", grid=(nb, nblk), in_specs=[row, row, row, pb, gl],
        out_specs=[row, row, pb, gl, pl.BlockSpec((1, LANE), lambda b, i: (0, 0))],
        out_shape=[jax.ShapeDtypeStruct((nb, S, D), F32), jax.ShapeDtypeStruct((nb, S, D), BF16),
                   jax.ShapeDtypeStruct((nb, 1, D), F32), jax.ShapeDtypeStruct((1, D), F32),
                   jax.ShapeDtypeStruct((1, LANE), F32)],
        compiler_params=_cparams("arbitrary", "arbitrary"),
    )(x1, ffn, target, gate2, post_g)


def _rotate_half(t):
    lane = lax.broadcasted_iota(jnp.int32, t.shape, 1)
    return jnp.where((lane & 15) < 8, -pltpu.roll(t, LANE - 8, 1), pltpu.roll(t, 8, 1))


def rope_call(name, x, width, colblk, cos, sin, out_dtype, tr):
    nb = x.shape[0]
    R = cos.shape[0]
    nblk = R // tr

    def body(x_ref, c_ref, s_ref, o_ref):
        c, s = c_ref[...], s_ref[...]
        for h in range(width // LANE):
            t = x_ref[0, :, h * LANE:(h + 1) * LANE].astype(F32)
            o_ref[0, :, h * LANE:(h + 1) * LANE] = (t * c + _rotate_half(t) * s).astype(o_ref.dtype)

    tab = pl.BlockSpec((tr, LANE), lambda b, i: (i, 0))
    return pl.pallas_call(
        body, name=name, grid=(nb, nblk),
        in_specs=[pl.BlockSpec((1, tr, width), lambda b, i: (b, i, colblk)), tab, tab],
        out_specs=pl.BlockSpec((1, tr, width), lambda b, i: (b, i, 0)),
        out_shape=jax.ShapeDtypeStruct((nb, R, width), out_dtype),
        compiler_params=_cparams("arbitrary", "arbitrary"),
    )(x, cos, sin)


def rope_tables(n_ctx, seq):
    n_rows = seq // GRID_W
    row = np.repeat(np.arange(n_rows), GRID_W).astype(np.float32)
    col = np.tile(np.arange(GRID_W), n_rows).astype(np.float32)
    axis_dim = ROPE // 2
    inv_freq = jnp.asarray(ROPE_THETA, F32) ** (-jnp.arange(0, axis_dim, 2, dtype=F32) / axis_dim)
    ang_r = jnp.asarray(row)[:, None] * inv_freq
    ang_c = jnp.asarray(col)[:, None] * inv_freq
    ang = jnp.concatenate([ang_r, ang_r, ang_c, ang_c], axis=-1)
    cos = jnp.ones((n_ctx + seq, LANE), F32).at[n_ctx:, KR_LANE:KR_LANE + ROPE].set(jnp.cos(ang))
    sin = jnp.zeros((n_ctx + seq, LANE), F32).at[n_ctx:, KR_LANE:KR_LANE + ROPE].set(jnp.sin(ang))
    return cos, sin


Q_PRESCALE = ATTN_SCALE * math.log2(math.e)


def _attn_weights(q, kc):
    s2 = lax.dot_general(q, kc, (((1,), (1,)), ((), ())), preferred_element_type=F32)
    e = jnp.exp2(s2 - jnp.max(s2, axis=1, keepdims=True))
    return e, 1.0 / jnp.sum(e, axis=1, keepdims=True)


def _key_block(kv, kr):
    lane = lax.broadcasted_iota(jnp.int32, kv.shape, 1)
    return jnp.where(lane < NOPE, kv, kr)


def _rotated_query(q_ref, cos_ref, sin_ref):
    t = q_ref[0].astype(F32)
    return (t * cos_ref[...] + _rotate_half(t) * sin_ref[...]).astype(BF16)


def attn_fwd(q_raw, kv, kr, cos_q, sin_q, tq):
    nb, S, _ = q_raw.shape
    T = kv.shape[1]

    def body(q_ref, kv_ref, kr_ref, c_ref, s_ref, o_ref):
        kvv = kv_ref[0]
        e, r = _attn_weights(_rotated_query(q_ref, c_ref, s_ref), _key_block(kvv, kr_ref[0]))
        o = lax.dot_general(e.astype(BF16), kvv, (((1,), (0,)), ((), ())), preferred_element_type=F32) * r
        lane = lax.broadcasted_iota(jnp.int32, o.shape, 1)
        o_ref[0] = jnp.where(lane >= NOPE, o, 0.0).astype(o_ref.dtype)

    return pl.pallas_call(
        body, name="attn_fwd", grid=(nb, N_HEADS, S // tq),
        in_specs=[pl.BlockSpec((1, tq, HEAD_BLOCK), lambda b, h, i: (b, i, h)),
                  pl.BlockSpec((1, T, HEAD_BLOCK), lambda b, h, i: (b, 0, h)),
                  pl.BlockSpec((1, T, HEAD_BLOCK), lambda b, h, i: (b, 0, 0)),
                  pl.BlockSpec((tq, LANE), lambda b, h, i: (i, 0)), pl.BlockSpec((tq, LANE), lambda b, h, i: (i, 0))],
        out_specs=pl.BlockSpec((1, tq, HEAD_BLOCK), lambda b, h, i: (b, i, h)),
        out_shape=jax.ShapeDtypeStruct((nb, S, QP), BF16),
        compiler_params=_cparams("arbitrary", "arbitrary", "arbitrary"),
    )(q_raw, kv, kr, cos_q, sin_q)


def attn_bwd(q_raw, kv, kr, do, cos_q, sin_q, cos, sin, tq):
    nb, S, _ = q_raw.shape
    T = kv.shape[1]

    def body(q_ref, kv_ref, kr_ref, do_ref, cq_ref, sq_ref, c_ref, s_ref, dq_ref, dkv_ref, dkr_ref):
        h, i = pl.program_id(1), pl.program_id(2)

        @pl.when(i == 0)
        def _():
            dkv_ref[...] = jnp.zeros_like(dkv_ref)

        @pl.when(jnp.logical_and(h == 0, i == 0))
        def _():
            dkr_ref[...] = jnp.zeros_like(dkr_ref)

        qv, kvv, dov = _rotated_query(q_ref, cq_ref, sq_ref), kv_ref[0], do_ref[0]
        kc = _key_block(kvv, kr_ref[0])
        e, r = _attn_weights(qv, kc)
        dor = (dov.astype(F32) * r).astype(BF16)
        dpr = lax.dot_general(dor, kvv, (((1,), (1,)), ((), ())), preferred_element_type=F32)
        ds = (e * (dpr - r * jnp.sum(dpr * e, axis=1, keepdims=True))).astype(BF16)
        dq = lax.dot_general(ds, kc, (((1,), (0,)), ((), ())), preferred_element_type=F32) * ATTN_SCALE
        dq_ref[0] = (dq * c_ref[...] - _rotate_half(dq) * s_ref[...]).astype(dq_ref.dtype)
        dkc = lax.dot_general(ds, qv, (((0,), (0,)), ((), ())), preferred_element_type=F32) * math.log(2.0)
        dv = lax.dot_general(e.astype(BF16), dor, (((0,), (0,)), ((), ())), preferred_element_type=F32)
        lane = lax.broadcasted_iota(jnp.int32, dkc.shape, 1)
        dkv_ref[0] += jnp.where(lane < NOPE, dkc, dv)
        dkr_ref[0] += jnp.where(lane >= NOPE, dkc, 0.0)

    qspec = pl.BlockSpec((1, tq, HEAD_BLOCK), lambda b, h, i: (b, i, h))
    kspec = pl.BlockSpec((1, T, HEAD_BLOCK), lambda b, h, i: (b, 0, h))
    rspec = pl.BlockSpec((1, T, HEAD_BLOCK), lambda b, h, i: (b, 0, 0))
    tab = pl.BlockSpec((tq, LANE), lambda b, h, i: (i, 0))
    return pl.pallas_call(
        body, name="attn_bwd", grid=(nb, N_HEADS, S // tq),
        in_specs=[qspec, kspec, rspec, qspec, tab, tab, tab, tab], out_specs=[qspec, kspec, rspec],
        out_shape=[jax.ShapeDtypeStruct((nb, S, QP), BF16), jax.ShapeDtypeStruct((nb, T, QP), F32),
                   jax.ShapeDtypeStruct((nb, T, HEAD_BLOCK), F32)],
        compiler_params=_cparams("arbitrary", "arbitrary", "arbitrary"),
    )(q_raw, kv, kr, do, cos_q, sin_q, cos, sin)


CONV_HALO = 8


def _segments(n, n_ctx):
    if n_ctx == 0:
        return [(0, n, CONV_HALO)]
    return [(0, n_ctx, CONV_HALO), (n_ctx, n - n_ctx, 2 * CONV_HALO + n_ctx)]


def _halo_scratch(n, n_ctx, tc):
    return pltpu.VMEM((n + CONV_HALO * (len(_segments(n, n_ctx)) + 1), tc), F32)


def _zero_halos(scr, segs):
    z = jnp.zeros((CONV_HALO, scr.shape[1]), scr.dtype)
    scr[0:CONV_HALO, :] = z
    for (_, rows, off) in segs:
        scr[off + rows:off + rows + CONV_HALO, :] = z


CONV_BLOCK_MAX = 256


def _conv_block(n, n_ctx):
    return _tile(math.gcd(n_ctx, n - n_ctx) if n_ctx else n, CONV_BLOCK_MAX, 8)


def _window(scr, off, r0, blk):
    return scr[pl.ds(pl.multiple_of(off - CONV_HALO + r0, 8), blk + 2 * CONV_HALO), :]


def _shifted(win, s):
    v = win if s == 0 else pltpu.roll(win, (-s) % win.shape[0], 0)
    return v[CONV_HALO:win.shape[0] - CONV_HALO]


def _tap_blocks(win, k, sign):
    return [_shifted(win, sign * (o - k // 2)) for o in range(k)]


def _taps(blocks, w):
    acc = None
    for o, blk in enumerate(blocks):
        t = w[o:o + 1, :] * blk
        acc = t if acc is None else acc + t
    return acc


def _tap_grads(xblocks, dpre):
    k = len(xblocks)
    sub8 = lax.broadcasted_iota(jnp.int32, (8, dpre.shape[1]), 0)
    out = jnp.where(sub8 == k, jnp.sum(dpre, axis=0, keepdims=True), 0.0)
    for o, blk in enumerate(xblocks):
        out = out + jnp.where(sub8 == o, jnp.sum(dpre * blk, axis=0, keepdims=True), 0.0)
    return out


def _row_blocks(rows, blk, fn, init=0):
    return lax.fori_loop(0, rows // blk, lambda i, c: fn(pl.multiple_of(i * blk, blk), c), init)


def _gelu(x):
    return 0.5 * x * (1.0 + lax.erf(x * (1.0 / math.sqrt(2.0))))


def _gelu_and_grad(x):
    cdf = 0.5 * (1.0 + lax.erf(x * (1.0 / math.sqrt(2.0))))
    return x * cdf, cdf + x * jnp.exp(-0.5 * x * x) * (1.0 / math.sqrt(2.0 * math.pi))


def ssd_conv_fwd(u, w8, bias, n_ctx, tc):
    nb, T, _ = u.shape
    cb0 = OFF_XBC // tc

    segs, blk = _segments(T, n_ctx), _conv_block(T, n_ctx)

    def body(x_ref, w_ref, b_ref, o_ref, xs):
        _zero_halos(xs, segs)
        for (start, rows, off) in segs:
            xs[off:off + rows, :] = x_ref[0, start:start + rows, :]
        w, bias_v = w_ref[...], b_ref[...]
        for (start, rows, off) in segs:
            def block(r0, carry, start=start, off=off):
                pre = bias_v + _taps(_tap_blocks(_window(xs, off, r0, blk), SSD_K, 1), w)
                o_ref[0, pl.ds(pl.multiple_of(start + r0, blk), blk), :] = _silu(pre)
                return carry

            _row_blocks(rows, blk, block)

    return pl.pallas_call(
        body, name="ssd_conv_fwd", grid=(nb, XBC // tc),
        in_specs=[pl.BlockSpec((1, T, tc), lambda b, j: (b, 0, cb0 + j)),
                  pl.BlockSpec((8, tc), lambda b, j: (0, j)), pl.BlockSpec((1, tc), lambda b, j: (0, j))],
        out_specs=pl.BlockSpec((1, T, tc), lambda b, j: (b, 0, j)),
        out_shape=jax.ShapeDtypeStruct((nb, T, XBC), F32),
        scratch_shapes=[_halo_scratch(T, n_ctx, tc)],
        compiler_params=_cparams("arbitrary", "arbitrary"),
    )(u, w8, bias)


def ssd_conv_bwd(u, w8, bias, dxbc, dxs_direct, n_ctx, tc):
    nb, T, _ = u.shape
    cb0 = OFF_XBC // tc
    n_direct = D_INNER // tc

    segs, blk = _segments(T, n_ctx), _conv_block(T, n_ctx)

    def body(x_ref, w_ref, b_ref, d0_ref, d1_ref, dd_ref, dx_ref, dw_ref, xs, ds):
        j, b = pl.program_id(0), pl.program_id(1)
        _zero_halos(xs, segs)
        _zero_halos(ds, segs)
        for (start, rows, off) in segs:
            xs[off:off + rows, :] = x_ref[0, start:start + rows, :]
        w, bias_v = w_ref[...], b_ref[...]
        has_direct = (j < n_direct).astype(F32)
        rows = jnp.zeros((8, tc), F32)
        for (start, n_rows, off) in segs:
            def block(r0, acc, start=start, off=off):
                xblocks = _tap_blocks(_window(xs, off, r0, blk), SSD_K, 1)
                pre = bias_v + _taps(xblocks, w)
                d = d0_ref[0, 0, pl.ds(pl.multiple_of(start + r0, blk), blk), :] + d1_ref[0, 0, pl.ds(pl.multiple_of(start + r0, blk), blk), :]
                if start == n_ctx:
                    d = d + dd_ref[0, pl.ds(r0, blk), :] * has_direct
                sg = jax.nn.sigmoid(pre)
                dpre = d * (sg * (1.0 + pre * (1.0 - sg)))
                ds[pl.ds(pl.multiple_of(off + r0, 8), blk), :] = dpre
                return acc + _tap_grads(xblocks, dpre)

            rows = _row_blocks(n_rows, blk, block, rows)
        for (start, n_rows, off) in segs:
            def block_dx(r0, carry, start=start, off=off):
                dx_ref[0, pl.ds(pl.multiple_of(start + r0, blk), blk), :] = _taps(_tap_blocks(_window(ds, off, r0, blk), SSD_K, -1), w).astype(dx_ref.dtype)
                return carry

            _row_blocks(n_rows, blk, block_dx)

        @pl.when(b == 0)
        def _():
            dw_ref[...] = rows

        @pl.when(b > 0)
        def _():
            dw_ref[...] += rows

    dspec0 = pl.BlockSpec((1, 1, T, tc), lambda j, b: (0, b, 0, j))
    dspec1 = pl.BlockSpec((1, 1, T, tc), lambda j, b: (1, b, 0, j))
    return pl.pallas_call(
        body, name="ssd_conv_bwd", grid=(XBC // tc, nb),
        in_specs=[pl.BlockSpec((1, T, tc), lambda j, b: (b, 0, cb0 + j)),
                  pl.BlockSpec((8, tc), lambda j, b: (0, j)), pl.BlockSpec((1, tc), lambda j, b: (0, j)),
                  dspec0, dspec1,
                  pl.BlockSpec((1, T - n_ctx, tc), lambda j, b: (b, 0, jnp.minimum(j, n_direct - 1)))],
        out_specs=[pl.BlockSpec((1, T, tc), lambda j, b: (b, 0, j)), pl.BlockSpec((8, tc), lambda j, b: (0, j))],
        out_shape=[jax.ShapeDtypeStruct((nb, T, XBC), BF16), jax.ShapeDtypeStruct((8, XBC), F32)],
        scratch_shapes=[_halo_scratch(T, n_ctx, tc), _halo_scratch(T, n_ctx, tc)],
        compiler_params=_cparams("arbitrary", "arbitrary"),
    )(u, w8, bias, dxbc, dxbc, dxs_direct)


GLU_TC = 256


def glu_interleave(w_up):
    blocks = []
    for j in range(D_FF // GLU_TC):
        blocks += [w_up[:, j * GLU_TC:(j + 1) * GLU_TC], w_up[:, D_FF + j * GLU_TC:D_FF + (j + 1) * GLU_TC]]
    return jnp.concatenate(blocks, axis=1)


def glu_deinterleave(g):
    nj = D_FF // GLU_TC
    gate = [g[:, 2 * j * GLU_TC:(2 * j + 1) * GLU_TC] for j in range(nj)]
    val = [g[:, (2 * j + 1) * GLU_TC:(2 * j + 2) * GLU_TC] for j in range(nj)]
    return jnp.concatenate(gate + val, axis=1)


def glu_fwd(up, w8, bias):
    nb, S, _ = up.shape
    tc = GLU_TC

    segs, blk = _segments(S, 0), _conv_block(S, 0)
    (_, _, off), = segs

    def body(u_ref, w_ref, b_ref, o_ref, xs):
        _zero_halos(xs, segs)
        xs[off:off + S, :] = u_ref[0, :, :tc]
        w, bias_v = w_ref[...], b_ref[...]

        def block(r0, carry):
            gc = bias_v + _taps(_tap_blocks(_window(xs, off, r0, blk), FFN_K, 1), w)
            o_ref[0, pl.ds(r0, blk), :] = (_gelu(gc) * u_ref[0, pl.ds(r0, blk), tc:]).astype(o_ref.dtype)
            return carry

        _row_blocks(S, blk, block)

    return pl.pallas_call(
        body, name="glu_fwd", grid=(nb, D_FF // tc),
        in_specs=[pl.BlockSpec((1, S, 2 * tc), lambda b, j: (b, 0, j)),
                  pl.BlockSpec((8, tc), lambda b, j: (0, j)), pl.BlockSpec((1, tc), lambda b, j: (0, j))],
        out_specs=pl.BlockSpec((1, S, tc), lambda b, j: (b, 0, j)),
        out_shape=jax.ShapeDtypeStruct((nb, S, D_FF), BF16),
        scratch_shapes=[_halo_scratch(S, 0, tc)],
        compiler_params=_cparams("arbitrary", "arbitrary"),
    )(up, w8, bias)


def glu_bwd(up, w8, bias, dact):
    nb, S, _ = up.shape
    tc = GLU_TC

    segs, blk = _segments(S, 0), _conv_block(S, 0)
    (_, _, off), = segs

    def body(u_ref, w_ref, b_ref, d_ref, du_ref, dw_ref, xs, ds):
        b = pl.program_id(1)
        _zero_halos(xs, segs)
        _zero_halos(ds, segs)
        xs[off:off + S, :] = u_ref[0, :, :tc]
        w, bias_v = w_ref[...], b_ref[...]

        def block(r0, acc):
            here = pl.ds(r0, blk)
            xblocks = _tap_blocks(_window(xs, off, r0, blk), FFN_K, 1)
            act, act_grad = _gelu_and_grad(bias_v + _taps(xblocks, w))
            d = d_ref[0, here, :].astype(F32)
            du_ref[0, here, tc:] = (d * act).astype(du_ref.dtype)
            dpre = d * u_ref[0, here, tc:] * act_grad
            ds[pl.ds(pl.multiple_of(off + r0, 8), blk), :] = dpre
            return acc + _tap_grads(xblocks, dpre)

        rows = _row_blocks(S, blk, block, jnp.zeros((8, tc), F32))

        def block_dx(r0, carry):
            du_ref[0, pl.ds(r0, blk), :tc] = _taps(_tap_blocks(_window(ds, off, r0, blk), FFN_K, -1), w).astype(du_ref.dtype)
            return carry

        _row_blocks(S, blk, block_dx)

        @pl.when(b == 0)
        def _():
            dw_ref[...] = rows

        @pl.when(b > 0)
        def _():
            dw_ref[...] += rows

    pair = pl.BlockSpec((1, S, 2 * tc), lambda j, b: (b, 0, j))
    return pl.pallas_call(
        body, name="glu_bwd", grid=(D_FF // tc, nb),
        in_specs=[pair, pl.BlockSpec((8, tc), lambda j, b: (0, j)), pl.BlockSpec((1, tc), lambda j, b: (0, j)),
                  pl.BlockSpec((1, S, tc), lambda j, b: (b, 0, j))],
        out_specs=[pair, pl.BlockSpec((8, tc), lambda j, b: (0, j))],
        out_shape=[jax.ShapeDtypeStruct((nb, S, 2 * D_FF), BF16), jax.ShapeDtypeStruct((8, D_FF), F32)],
        scratch_shapes=[_halo_scratch(S, 0, tc), _halo_scratch(S, 0, tc)],
        compiler_params=_cparams("arbitrary", "arbitrary"),
    )(up, w8, bias, dact)


def _chunk_of(d, k, n_cc, n_ch):
    rev = jnp.where(k < n_cc, n_cc - 1 - k, n_cc + n_ch - 1 - k)
    return jnp.where(d == 1, rev, k)


def _lane_pick(v, lane_iota, l):
    return jnp.sum(jnp.where(lane_iota == l, v, 0.0), axis=1, keepdims=True)


def head_spread_matrix():
    return (jnp.arange(LANE)[:, None] == (jnp.arange(D_INNER)[None, :] // SSD_P)).astype(BF16)


def _split_dot(x, e, dims):
    hi = x.astype(BF16)
    lo = (x - hi.astype(F32)).astype(BF16)
    return (lax.dot_general(hi, e, dims, preferred_element_type=F32)
            + lax.dot_general(lo, e, dims, preferred_element_type=F32))


def _spread(x, e):
    return _split_dot(x, e, (((1,), (0,)), ((), ())))


def _gather_heads(y, e):
    return _split_dot(y, e, (((1,), (1,)), ((), ())))


def _softplus(x):
    return jnp.maximum(x, 0.0) + jnp.log(1.0 + jnp.exp(-jnp.abs(x)))


def ssd_dt_inputs(u, a_log, dt_bias):
    pad = LANE - SSD_HEADS
    dt = u[..., OFF_DT:OFF_DT + 2 * SSD_HEADS]
    dt2 = jnp.stack([jnp.pad(dt[..., i * SSD_HEADS:(i + 1) * SSD_HEADS], ((0, 0), (0, 0), (0, pad))) for i in range(2)])

    def lanes(v):
        return jnp.pad(v.reshape(2, 1, SSD_HEADS), ((0, 0), (0, 0), (0, pad)))

    return dt2, lanes(a_log), lanes(dt_bias)


def _ssd_common(d, dt_raw, alog, dtb):
    Q = dt_raw.shape[0]
    row = lax.broadcasted_iota(jnp.int32, (Q, Q), 0)
    col = lax.broadcasted_iota(jnp.int32, (Q, Q), 1)
    rev = d == 1
    maskb = jnp.where(rev, row, col) <= jnp.where(rev, col, row)
    tri = maskb.astype(F32)
    A = -jnp.exp(alog)
    dtv = _softplus(dt_raw + dtb)
    a = dtv * A
    cum = lax.dot_general(tri, a, (((1,), (0,)), ((), ())), precision=lax.Precision.HIGHEST, preferred_element_type=F32)
    tot = jnp.sum(a, axis=0, keepdims=True)
    return maskb, tri, A, dtv, cum, tot


def ssd_fwd(xbc, dt2, alog2, dtb2, n_ctx, hosted):
    nb, T, _ = xbc.shape
    S = T - n_ctx
    n_ch, n_cc = T // CHUNK, n_ctx // CHUNK
    Q = CHUNK
    n_pairs = SSD_HEADS // 2
    n_ex = hosted.n
    n_in = 5

    def body(*refs):
        x_ref, dt_ref, al_ref, db_ref, e_ref = refs[:n_in]
        send_refs = refs[n_in:n_in + n_ex]
        y_ref, hin_ref = refs[n_in + n_ex:n_in + 2 + n_ex]
        recv_refs = refs[n_in + 2 + n_ex:n_in + 2 + 2 * n_ex]
        H, *sems = refs[n_in + 2 + 2 * n_ex:]
        d, k = pl.program_id(1), pl.program_id(2)
        first_step = jnp.logical_and(jnp.logical_and(pl.program_id(0) == 0, d == 0), k == 0)
        last_step = jnp.logical_and(jnp.logical_and(pl.program_id(0) == nb - 1, d == 1), k == n_ch - 1)
        begin_exchange, end_exchange = hosted.steps(send_refs, recv_refs, sems, first_step, last_step)
        begin_exchange()

        @pl.when(k == 0)
        def _():
            H[...] = jnp.zeros_like(H)

        maskb, tri, A, dtv, cum, tot = _ssd_common(d, dt_ref[0, 0], al_ref[0], db_ref[0])
        e = e_ref[...]
        cumT = cum.T
        cum_e, dt_e = _spread(cum, e), _spread(dtv, e)
        tot_e = _spread(jnp.broadcast_to(tot, (8, LANE)), e)[0:1]
        hin_ref[0, 0, 0] = H[...].astype(BF16)
        lane = lax.broadcasted_iota(jnp.int32, (Q, LANE), 1)
        lane1 = lax.broadcasted_iota(jnp.int32, (1, LANE), 1)
        subc = lax.broadcasted_iota(jnp.int32, (LANE, 1), 0)
        half = lane < SSD_P
        for g in range(SSD_GROUPS):
            Bg = x_ref[0, :, D_INNER + g * SSD_N:D_INNER + (g + 1) * SSD_N].astype(BF16)
            Cg = x_ref[0, :, D_INNER + GN + g * SSD_N:D_INNER + GN + (g + 1) * SSD_N].astype(BF16)
            Gm = lax.dot_general(Cg, Bg, (((1,), (1,)), ((), ())), preferred_element_type=F32)
            for pr in range(n_pairs // SSD_GROUPS):
                p = g * (n_pairs // SSD_GROUPS) + pr
                sc, dtp, totp = [t[:, p * LANE:(p + 1) * LANE] for t in (cum_e, dt_e, tot_e)]
                swapped = pltpu.roll(sc, SSD_P, 1)
                s0c, s1c = jnp.where(half, sc, swapped), jnp.where(half, swapped, sc)
                s0r, s1r = cumT[2 * p:2 * p + 1, :], cumT[2 * p + 1:2 * p + 2, :]
                tot0, tot1 = _lane_pick(tot, lane1, 2 * p), _lane_pick(tot, lane1, 2 * p + 1)
                M0 = (Gm * jnp.exp(jnp.where(maskb, s0c - s0r, NEG_BIG))).astype(BF16)
                M1 = (Gm * jnp.exp(jnp.where(maskb, s1c - s1r, NEG_BIG))).astype(BF16)
                xd = x_ref[0, :, p * LANE:(p + 1) * LANE] * dtp
                xdb = xd.astype(BF16)
                yd = jnp.where(half,
                               lax.dot_general(M0, xdb, (((1,), (0,)), ((), ())), preferred_element_type=F32),
                               lax.dot_general(M1, xdb, (((1,), (0,)), ((), ())), preferred_element_type=F32))
                Hp = H[p * LANE:(p + 1) * LANE, :]
                yo = lax.dot_general(Cg, Hp.astype(BF16), (((1,), (1,)), ((), ())), preferred_element_type=F32) * jnp.exp(sc)
                y_ref[0, 0, :, p * LANE:(p + 1) * LANE] = yd + yo
                xdw = (xd * jnp.exp(totp - sc)).astype(BF16)
                etot = jnp.exp(jnp.where(subc < SSD_P, tot0, tot1))
                H[p * LANE:(p + 1) * LANE, :] = Hp * etot + lax.dot_general(
                    xdw, Bg, (((0,), (0,)), ((), ())), preferred_element_type=F32)
        end_exchange()

    def ymap(b, d, k):
        return (d, b, _chunk_of(d, jnp.maximum(k, n_cc), n_cc, n_ch) - n_cc, 0)

    return pl.pallas_call(
        body, name="ssd_fwd", grid=(nb, 2, n_ch),
        in_specs=[pl.BlockSpec((1, Q, XBC), lambda b, d, k: (b, _chunk_of(d, k, n_cc, n_ch), 0)),
                  pl.BlockSpec((1, 1, Q, LANE), lambda b, d, k: (d, b, _chunk_of(d, k, n_cc, n_ch), 0)),
                  pl.BlockSpec((1, 1, LANE), lambda b, d, k: (d, 0, 0)), pl.BlockSpec((1, 1, LANE), lambda b, d, k: (d, 0, 0)),
                  pl.BlockSpec((LANE, D_INNER), lambda b, d, k: (0, 0))] + hosted.specs,
        out_specs=[pl.BlockSpec((1, 1, Q, D_INNER), ymap),
                   pl.BlockSpec((1, 1, 1, D_INNER, SSD_N), lambda b, d, k: (d, b, k, 0, 0))] + hosted.specs,
        out_shape=[jax.ShapeDtypeStruct((2, nb, S, D_INNER), F32),
                   jax.ShapeDtypeStruct((2, nb, n_ch, D_INNER, SSD_N), BF16)] + hosted.out_shape,
        scratch_shapes=[pltpu.VMEM((D_INNER, SSD_N), F32)] + hosted.scratch,
        compiler_params=_cparams("arbitrary", "arbitrary", "arbitrary"),
    )(xbc, dt2, alog2, dtb2, head_spread_matrix(), *hosted.arrays)


def ssd_bwd(xbc, dt2, alog2, dtb2, hin, dy, n_ctx, hosted):
    nb, T, _ = xbc.shape
    n_ex = hosted.n
    n_ch, n_cc = T // CHUNK, n_ctx // CHUNK
    n_in = 7
    Q = CHUNK
    n_pairs = SSD_HEADS // 2
    NT = (((1,), (1,)), ((), ()))
    NN = (((1,), (0,)), ((), ()))
    TN = (((0,), (0,)), ((), ()))

    def dot(a, b, dims):
        return lax.dot_general(a.astype(BF16), b.astype(BF16), dims, preferred_element_type=F32)

    def body(*refs):
        x_ref, dt_ref, al_ref, db_ref, e_ref, hin_ref, dy_ref = refs[:n_in]
        send_refs = refs[n_in:n_in + n_ex]
        dx_ref, ddt_ref, st_ref = refs[n_in + n_ex:n_in + 3 + n_ex]
        recv_refs = refs[n_in + 3 + n_ex:n_in + 3 + 2 * n_ex]
        dH, dce, dde, *sems = refs[n_in + 3 + 2 * n_ex:]
        d, kk = pl.program_id(1), pl.program_id(2)
        ks = n_ch - 1 - kk
        first_step = jnp.logical_and(jnp.logical_and(pl.program_id(0) == 0, d == 0), kk == 0)
        last_step = jnp.logical_and(jnp.logical_and(pl.program_id(0) == nb - 1, d == 1), kk == n_ch - 1)
        begin_exchange, end_exchange = hosted.steps(send_refs, recv_refs, sems, first_step, last_step)
        begin_exchange()

        @pl.when(kk == 0)
        def _():
            dH[...] = jnp.zeros_like(dH)

        @pl.when(jnp.logical_and(jnp.logical_and(pl.program_id(0) == 0, d == 0), kk == 0))
        def _():
            st_ref[...] = jnp.zeros_like(st_ref)

        dt_raw = dt_ref[0, 0]
        alog, dtb_v = al_ref[0], db_ref[0]
        maskb, tri, A, dtv, cum, tot = _ssd_common(d, dt_raw, alog, dtb_v)
        e = e_ref[...]
        cumT = cum.T
        cum_e, dt_e = _spread(cum, e), _spread(dtv, e)
        tot_e = _spread(jnp.broadcast_to(tot, (8, LANE)), e)[0:1]
        live = (ks >= n_cc).astype(F32)
        lane = lax.broadcasted_iota(jnp.int32, (Q, LANE), 1)
        lane1 = lax.broadcasted_iota(jnp.int32, (1, LANE), 1)
        sub = lax.broadcasted_iota(jnp.int32, (LANE, Q), 0)
        subc = lax.broadcasted_iota(jnp.int32, (LANE, 1), 0)
        half = lane < SSD_P
        halfc = subc < SSD_P
        pair_ones = ((lax.broadcasted_iota(jnp.int32, (2 * Q, LANE), 0) >= Q).astype(jnp.int32)
                     == (lax.broadcasted_iota(jnp.int32, (2 * Q, LANE), 1) >= SSD_P).astype(jnp.int32)).astype(BF16)
        dcumT = jnp.zeros((LANE, Q), F32)
        dtot = jnp.zeros((1, LANE), F32)
        dtot_parts = []
        for g in range(SSD_GROUPS):
            Bg = x_ref[0, :, D_INNER + g * SSD_N:D_INNER + (g + 1) * SSD_N].astype(BF16)
            Cg = x_ref[0, :, D_INNER + GN + g * SSD_N:D_INNER + GN + (g + 1) * SSD_N].astype(BF16)
            Gm = lax.dot_general(Cg, Bg, NT, preferred_element_type=F32)
            dG = jnp.zeros((Q, Q), F32)
            dC = jnp.zeros((Q, SSD_N), F32)
            dB = jnp.zeros((Q, SSD_N), F32)
            for pr in range(n_pairs // SSD_GROUPS):
                p = g * (n_pairs // SSD_GROUPS) + pr
                l0, l1 = 2 * p, 2 * p + 1
                sc, dtp, totp = [t[:, p * LANE:(p + 1) * LANE] for t in (cum_e, dt_e, tot_e)]
                swapped = pltpu.roll(sc, SSD_P, 1)
                s0c, s1c = jnp.where(half, sc, swapped), jnp.where(half, swapped, sc)
                s0r, s1r = cumT[l0:l0 + 1, :], cumT[l1:l1 + 1, :]
                tot0, tot1 = _lane_pick(tot, lane1, l0), _lane_pick(tot, lane1, l1)
                L0 = jnp.exp(jnp.where(maskb, s0c - s0r, NEG_BIG))
                L1 = jnp.exp(jnp.where(maskb, s1c - s1r, NEG_BIG))
                M0, M1 = Gm * L0, Gm * L1
                xs = x_ref[0, :, p * LANE:(p + 1) * LANE]
                xd = xs * dtp
                es = jnp.exp(sc)
                dte = jnp.exp(totp - sc)
                etot = jnp.exp(jnp.where(halfc, tot0, tot1))
                dyp = dy_ref[0, :, p * LANE:(p + 1) * LANE] * live
                Hp = hin_ref[0, 0, 0, p * LANE:(p + 1) * LANE, :]
                dHp = dH[p * LANE:(p + 1) * LANE, :]
                bdh = dot(Bg, dHp, NT)
                mtdy = dot(jnp.concatenate([M0, M1], axis=1), dyp, TN)
                dxd = jnp.where(half, mtdy[:Q], mtdy[Q:]) + bdh * dte
                dy0 = jnp.where(half, dyp, 0.0)
                dm = dot(jnp.concatenate([dy0, dyp - dy0], axis=0), xd, NT)
                dM0, dM1 = dm[:Q], dm[Q:]
                dG = dG + dM0 * L0 + dM1 * L1
                dyes = dyp * es
                xdw = xd * dte
                dC = dC + dot(dyes, Hp, NN)
                dB = dB + dot(xdw, dHp, NN)
                W0, W1 = dM0 * M0, dM1 * M1
                yoff = dot(Cg, Hp, NT) * es
                r_off = dyp * yoff
                r_st = xd * bdh * dte
                hh = jnp.sum(dHp * Hp.astype(F32), axis=1, keepdims=True) * etot
                w_rows = _split_dot(jnp.concatenate([W0, W1], axis=1), pair_ones, NN) * (1.0 / SSD_P)
                dce[:, p * LANE:(p + 1) * LANE] = r_off - r_st + w_rows
                dde[:, p * LANE:(p + 1) * LANE] = dxd * xs
                dtot_parts.append(jnp.sum(r_st, axis=0, keepdims=True))
                for (l, W, hselc) in ((l0, W0, halfc), (l1, W1, jnp.logical_not(halfc))):
                    row_g = -jnp.sum(W, axis=0, keepdims=True)
                    dcumT = dcumT + jnp.where(sub == l, row_g, 0.0)
                    dtot = dtot + jnp.where(lane1 == l, jnp.sum(jnp.where(hselc, hh, 0.0), axis=0, keepdims=True), 0.0)
                dx_ref[0, 0, :, p * LANE:(p + 1) * LANE] = dxd * dtp
                dH[p * LANE:(p + 1) * LANE, :] = dHp * etot + dot(dyes, Cg, TN)
            dx_ref[0, 0, :, D_INNER + g * SSD_N:D_INNER + (g + 1) * SSD_N] = dB + dot(dG, Cg, TN)
            dx_ref[0, 0, :, D_INNER + GN + g * SSD_N:D_INNER + GN + (g + 1) * SSD_N] = dC + dot(dG, Bg, NN)
        dcum_all = dcumT.T + _gather_heads(dce[...], e)
        dtot_e = jnp.broadcast_to(jnp.concatenate(dtot_parts, axis=1), (8, D_INNER))
        dtot = dtot + _gather_heads(dtot_e, e)[0:1]
        da = lax.dot_general(tri, dcum_all, TN, precision=lax.Precision.HIGHEST, preferred_element_type=F32) + dtot
        ddtv = _gather_heads(dde[...], e) + da * A
        ddt_raw = ddtv * jax.nn.sigmoid(dt_raw + dtb_v)
        ddt_ref[0, 0] = ddt_raw
        sub8 = lax.broadcasted_iota(jnp.int32, (8, LANE), 0)
        st_ref[...] += (jnp.where(sub8 == 2 * d, jnp.sum(da * dtv * A, axis=0, keepdims=True), 0.0)
                        + jnp.where(sub8 == 2 * d + 1, jnp.sum(ddt_raw, axis=0, keepdims=True), 0.0))
        end_exchange()

    def cmap(d, kk):
        return _chunk_of(d, n_ch - 1 - kk, n_cc, n_ch)

    def dymap(b, d, kk):
        return (b, _chunk_of(d, jnp.maximum(n_ch - 1 - kk, n_cc), n_cc, n_ch) - n_cc, 0)

    return pl.pallas_call(
        body, name="ssd_bwd", grid=(nb, 2, n_ch),
        in_specs=[pl.BlockSpec((1, Q, XBC), lambda b, d, kk: (b, cmap(d, kk), 0)),
                  pl.BlockSpec((1, 1, Q, LANE), lambda b, d, kk: (d, b, cmap(d, kk), 0)),
                  pl.BlockSpec((1, 1, LANE), lambda b, d, kk: (d, 0, 0)), pl.BlockSpec((1, 1, LANE), lambda b, d, kk: (d, 0, 0)),
                  pl.BlockSpec((LANE, D_INNER), lambda b, d, kk: (0, 0)),
                  pl.BlockSpec((1, 1, 1, D_INNER, SSD_N), lambda b, d, kk: (d, b, n_ch - 1 - kk, 0, 0)),
                  pl.BlockSpec((1, Q, D_INNER), dymap)] + hosted.specs,
        out_specs=[pl.BlockSpec((1, 1, Q, XBC), lambda b, d, kk: (d, b, cmap(d, kk), 0)),
                   pl.BlockSpec((1, 1, Q, LANE), lambda b, d, kk: (d, b, cmap(d, kk), 0)),
                   pl.BlockSpec((8, LANE), lambda b, d, kk: (0, 0))] + hosted.specs,
        out_shape=[jax.ShapeDtypeStruct((2, nb, T, XBC), F32), jax.ShapeDtypeStruct((2, nb, T, LANE), F32),
                   jax.ShapeDtypeStruct((8, LANE), F32)] + hosted.out_shape,
        scratch_shapes=[pltpu.VMEM((D_INNER, SSD_N), F32), pltpu.VMEM((Q, D_INNER), F32), pltpu.VMEM((Q, D_INNER), F32)] + hosted.scratch,
        compiler_params=_cparams("arbitrary", "arbitrary", "arbitrary"),
    )(xbc, dt2, alog2, dtb2, head_spread_matrix(), hin, dy, *hosted.arrays)


def _adamw(w, g, m, v):
    mn = ADAM_B1 * m + (1.0 - ADAM_B1) * g
    vn = ADAM_B2 * v + (1.0 - ADAM_B2) * jnp.square(g)
    m_hat = mn / (1.0 - ADAM_B1 ** ADAM_STEP)
    v_hat = vn / (1.0 - ADAM_B2 ** ADAM_STEP)
    return -ADAM_LR * (m_hat / (jnp.sqrt(v_hat) + ADAM_EPS) + ADAM_WD * w), mn, vn


def adamw_matrix(name, w, g_slots, m, v):
    K, n = w.shape
    s = g_slots.shape[0]
    tr = _tile(K, 256, 8)

    def body(w_ref, g_ref, m_ref, v_ref, go_ref, d_ref, mo_ref, vo_ref):
        g = g_ref[0].astype(F32)
        for j in range(1, s):
            g = g + g_ref[j].astype(F32)
        go_ref[...] = g
        d_ref[...], mo_ref[...], vo_ref[...] = _adamw(w_ref[...], g, m_ref[...], v_ref[...])

    spec = pl.BlockSpec((tr, n), lambda i: (i, 0))
    return pl.pallas_call(
        body, name=name, grid=(K // tr,),
        in_specs=[spec, pl.BlockSpec((s, tr, n), lambda i: (0, i, 0)), spec, spec], out_specs=[spec] * 4,
        out_shape=[jax.ShapeDtypeStruct((K, n), F32)] * 4,
        compiler_params=_cparams("arbitrary"),
    )(w, g_slots, m, v)


def adamw_small(ws, gs, ms, vs):
    n = len(ws)

    def body(*refs):
        for i in range(n):
            d, mn, vn = _adamw(refs[i][...], refs[n + i][...], refs[2 * n + i][...], refs[3 * n + i][...])
            refs[4 * n + i][...] = d
            refs[5 * n + i][...] = mn
            refs[6 * n + i][...] = vn

    shapes = [jax.ShapeDtypeStruct(w.shape, F32) for w in ws]
    out = pl.pallas_call(body, name="adamw_small", out_shape=shapes * 3)(*ws, *gs, *ms, *vs)
    return out[:n], out[n:2 * n], out[2 * n:]


def sum_slots(name, x):
    n = x.shape[0]

    def fn(t):
        acc = t[0]
        for j in range(1, n):
            acc = acc + t[j]
        return (acc,)

    return ew_call(name, fn, [x], [(x.shape[1:], F32)])[0]


def _pack_rows(parts):
    rows = []
    for p in parts:
        flat = p.reshape(1, -1)
        n = flat.shape[1]
        rows.append(jnp.pad(flat, ((0, 0), (0, -(-n // (8 * LANE)) * 8 * LANE - n))).reshape(-1, LANE))
    return jnp.concatenate(rows, axis=0)


def _unpack_rows(pack, shapes):
    out, r = [], 0
    for s in shapes:
        n = int(np.prod(s))
        nr = -(-n // (8 * LANE)) * 8
        out.append(pack[r:r + nr].reshape(1, -1)[:, :n].reshape(s))
        r += nr
    return out


def _mesh_pos():
    return lax.axis_index("x"), lax.axis_index("y"), lax.axis_index("c")


N_PEERS = N_DEV - 1


def all_gather(name, vs):
    n = len(vs)

    def body(*refs):
        _ag_start(refs[:n], refs[n:2 * n], *refs[2 * n:])
        _ag_finish(refs[:n], refs[n:2 * n], *refs[2 * n:])

    hbm = pl.BlockSpec(memory_space=pl.ANY)
    return pl.pallas_call(
        body, name=name, out_shape=_ag_out_shape(vs), in_specs=[hbm] * n, out_specs=[hbm] * n,
        scratch_shapes=_a2a_scratch(n),
    )(*vs)


def _ag_out_shape(vs):
    return [jax.ShapeDtypeStruct((N_DEV,) + v.shape, v.dtype) for v in vs]


def _ag_copies(x_refs, out_refs, send_sems, recv_sems, local_sems):
    n = len(x_refs)
    x, y, c = _mesh_pos()
    me, sibling = (x, y, c), (x, y, 1 - c)
    chips = [(1 - x, y), (x, 1 - y), (1 - x, 1 - y)]

    def slot(a, px, py, pc):
        return out_refs[a].at[4 * px + 2 * py + pc]

    def copy(a, k, block, to, src=None):
        return pltpu.make_async_remote_copy(
            src_ref=slot(a, *block) if src is None else src, dst_ref=slot(a, *block),
            send_sem=send_sems.at[N_PEERS * a + k], recv_sem=recv_sems.at[N_PEERS * a + k],
            device_id=to, device_id_type=MESH)

    local = [pltpu.make_async_copy(x_refs[a], slot(a, *me), local_sems.at[a]) for a in range(n)]
    first = []
    for a in range(n):
        first.append(copy(a, 0, me, sibling, src=x_refs[a]))
        first += [copy(a, 1 + j, me, (*chip, c), src=x_refs[a]) for j, chip in enumerate(chips)]
    passed = [(copy(a, 1 + j, (*chip, c), me), copy(a, 4 + j, (*chip, c), sibling))
              for j, chip in enumerate(chips) for a in range(n)]
    from_sibling = []
    for a in range(n):
        from_sibling.append(copy(a, 0, sibling, me))
        from_sibling += [copy(a, 4 + j, (*chip, 1 - c), me) for j, chip in enumerate(chips)]
    return local, first, passed, from_sibling


def _ag_start(*refs):
    local, first, _, _ = _ag_copies(*refs)
    for cp in local + first:
        cp.start()


def _ag_finish(*refs):
    local, first, passed, from_sibling = _ag_copies(*refs)
    for arrived, hand_on in passed:
        arrived.wait_recv()
        hand_on.start()
    for cp in from_sibling:
        cp.wait_recv()
    for cp in first + [hand_on for _, hand_on in passed]:
        cp.wait_send()
    for cp in local:
        cp.wait()


def _a2a_scratch(n):
    return [pltpu.SemaphoreType.DMA((N_PEERS * n,)), pltpu.SemaphoreType.DMA((N_PEERS * n,)), pltpu.SemaphoreType.DMA((n,))]


def _a2a_copies(x_refs, out_refs, send_sems, recv_sems, local_sems):
    n = len(x_refs)
    x, y, c = _mesh_pos()
    me = 4 * x + 2 * y + c
    local = [pltpu.make_async_copy(x_refs[a].at[me], out_refs[a].at[me], local_sems.at[a]) for a in range(n)]
    remote = []
    for k in range(1, N_DEV):
        px, py, pc = x ^ ((k >> 2) & 1), y ^ ((k >> 1) & 1), c ^ (k & 1)
        for a in range(n):
            remote.append(pltpu.make_async_remote_copy(
                src_ref=x_refs[a].at[4 * px + 2 * py + pc], dst_ref=out_refs[a].at[me],
                send_sem=send_sems.at[N_PEERS * a + k - 1], recv_sem=recv_sems.at[N_PEERS * a + k - 1],
                device_id=(px, py, pc), device_id_type=MESH))
    return local, remote


def _a2a_start(local, remote):
    for cp in local + remote:
        cp.start()


def _a2a_wait(local, remote):
    for cp in remote:
        cp.wait_recv()
    for cp in remote:
        cp.wait_send()
    for cp in local:
        cp.wait()


class Hosted:
    def __init__(self, start=None, finish=None, arrays=(), out_shape=()):
        self.start, self.finish, self.arrays, self.out_shape = start, finish, list(arrays), list(out_shape)
        self.n = len(self.arrays)
        self.specs = [pl.BlockSpec(memory_space=pl.ANY)] * self.n
        self.scratch = _a2a_scratch(self.n) if self.n else []

    def steps(self, send_refs, recv_refs, sems, first_step, last_step):
        def begin():
            if self.n:
                pl.when(first_step)(lambda: self.start(send_refs, recv_refs, *sems))

        def end():
            if self.n:
                pl.when(last_step)(lambda: self.finish(send_refs, recv_refs, *sems))

        return begin, end


def hosted_all_to_all(vs):
    return Hosted(lambda *r: _a2a_start(*_a2a_copies(*r)), lambda *r: _a2a_wait(*_a2a_copies(*r)), vs,
                  [jax.ShapeDtypeStruct(v.shape, v.dtype) for v in vs])


def hosted_all_gather(vs):
    return Hosted(_ag_start, _ag_finish, vs, _ag_out_shape(vs))


def _taps8(w):
    return jnp.concatenate([w, jnp.zeros((8 - w.shape[0], w.shape[1]), w.dtype)], axis=0)


FIRST = ("w_in",)
LATE_WEIGHTS = ("w_out", "w_up", "w_down", "w_q_up", "w_kv_up")


def first_weights_to_internal(w_in):
    cq, ckv, kr, z, xbc, dt = jnp.split(w_in, np.cumsum(IN_SPLITS)[:-1].tolist(), axis=1)
    K = w_in.shape[0]

    def zeros(n):
        return jnp.zeros((K, n), w_in.dtype)

    w_in_p = jnp.concatenate([cq, zeros(KR_LANE), kr, zeros(LANE - KR_LANE - ROPE), ckv, zeros(OFF_Z - OFF_CKV - KV_RANK),
                              z, xbc, dt, zeros(WIN_P - OFF_DT - 2 * SSD_HEADS)], axis=1)
    return dict(w_in_p=w_in_p)


def late_weights_to_internal(w_out, w_up, w_down, w_q_up, w_kv_up):
    attn_rows = w_out[:N_HEADS * V_DIM].reshape(N_HEADS, V_DIM, -1)
    w_out_p = jnp.concatenate([jnp.pad(attn_rows, ((0, 0), (HEAD_BLOCK - V_DIM, 0), (0, 0))).reshape(QP, -1),
                               w_out[N_HEADS * V_DIM:]], axis=0)
    w_q_p = jnp.pad(w_q_up.reshape(Q_RANK, N_HEADS, NOPE + ROPE), ((0, 0), (0, 0), (0, HEAD_BLOCK - NOPE - ROPE))).reshape(Q_RANK, QP)
    return dict(w_out_p=w_out_p, w_up=glu_interleave(w_up), w_down=w_down, w_q_p=w_q_p, w_kv=w_kv_up)


def _q_grad(g_q_p):
    return g_q_p.reshape(Q_RANK, N_HEADS, HEAD_BLOCK)[:, :, :NOPE + ROPE].reshape(Q_RANK, -1)


def _out_grad(g_out_p):
    return jnp.concatenate([g_out_p[:QP].reshape(N_HEADS, HEAD_BLOCK, -1)[:, HEAD_BLOCK - V_DIM:].reshape(N_HEADS * V_DIM, -1),
                            g_out_p[QP:]], axis=0)


EARLY = ("w_out", "w_up", "w_down", "w_q_up", "w_kv_up")


def local_step(x, ctx, target, mod_x, mod_c, W, late_shards, V):
    nb, S, D = x.shape
    C = ctx.shape[1]
    T = C + S
    tr = _tile(math.gcd(C, S), 256, 8)
    tq = _tile(S, 256, 8)
    tc = 256
    cblk = C // tr
    m = [mod_x[:, i * D:(i + 1) * D][:, None, :] for i in range(N_MOD)]
    mc = [mod_c[:, i * D:(i + 1) * D] for i in range(2)]
    ssd_w8, ffn_w8 = _taps8(V["ssd_conv_w"]), _taps8(V["ffn_conv_w"])
    dexp = jnp.repeat(V["ssd_d"].reshape(-1), SSD_P).reshape(1, D_INNER)
    cosT, sinT = rope_tables(C, S)
    cosS, sinS = cosT[C:], sinT[C:]

    (h1x,) = rows_fwd("prenorm_x", fn_prenorm, nb, S // tr, tr, [(x, D, 0, 0)], [m[0], m[1]], [V["mix_pre_norm"]], [(D, BF16)])
    (h1c,) = rows_fwd("prenorm_c", fn_prenorm, nb, C // tr, tr, [(ctx, D, 0, 0)], [], [mc[0], mc[1], V["mix_pre_norm"]], [(D, BF16)])
    h1 = jnp.concatenate([h1c, h1x], axis=1).reshape(nb * T, D)
    u = matmul("in_proj", [(h1, W["w_in_p"])], "nn", F32).reshape(nb, T, WIN_P)
    xbc = ssd_conv_fwd(u, ssd_w8, V["ssd_conv_b"], C, tc)
    dt2, alog2, dtb2 = ssd_dt_inputs(u, V["ssd_a_log"], V["ssd_dt_bias"])
    y2, hin, *late = ssd_fwd(xbc, dt2, alog2, dtb2, C, hosted_all_gather(late_shards))
    W = dict(W, **late_weights_to_internal(*[_whole(s, n) for s, n in zip(late, LATE_WEIGHTS)]))
    y2 = y2.reshape(2 * nb, S, D_INNER)
    (qn,) = rows_fwd("q_norm", fn_rms, nb, S // tr, tr, [(u, Q_RANK, OFF_CQ // Q_RANK, cblk)], [], [V["q_norm"]], [(Q_RANK, BF16)])
    (kvn,) = rows_fwd("kv_norm", fn_rms, nb, T // tr, tr, [(u, KV_RANK, OFF_CKV // KV_RANK, 0)], [], [V["kv_norm"]], [(KV_RANK, BF16)])
    qn2, kvn2 = qn.reshape(nb * S, Q_RANK), kvn.reshape(nb * T, KV_RANK)
    q_raw = matmul("q_up", [(qn2, W["w_q_p"])], "nn", F32).reshape(nb, S, QP)
    kv = matmul("kv_up", [(kvn2, W["w_kv"])], "nn", BF16).reshape(nb, T, QP)
    cos_q, sin_q = cosS * Q_PRESCALE, sinS * Q_PRESCALE
    kr = rope_call("rope_k", u, LANE, OFF_KR // LANE, cosT, sinT, BF16, tr)
    o = attn_fwd(q_raw, kv, kr, cos_q, sin_q, tq)
    fin_rows = [(y2, D_INNER, 0, 0, 0), (y2, D_INNER, 0, 0, nb), (xbc, D_INNER, 0, cblk), (u, D_INNER, OFF_Z // D_INNER, cblk)]
    fin_gl = [dexp, V["ssd_norm"]]
    (ssd,) = rows_fwd("ssd_finish", fn_ssd_finish, nb, S // tr, tr, fin_rows, [], fin_gl, [(D_INNER, BF16)])
    o2, ssd2 = o.reshape(nb * S, QP), ssd.reshape(nb * S, D_INNER)
    mix = matmul("out_proj", [(o2, W["w_out_p"][:QP]), (ssd2, W["w_out_p"][QP:])], "nn", F32).reshape(nb, S, D)
    pm_rows = [(x, D, 0, 0), (mix, D, 0, 0)]
    pm_pb = [m[2], m[4], m[3]]
    pm_gl = [V["mix_post_norm"], V["ffn_pre_norm"]]
    x1, h2 = rows_fwd("postmix", fn_postmix, nb, S // tr, tr, pm_rows, pm_pb, pm_gl, [(D, F32), (D, BF16)])
    h22 = h2.reshape(nb * S, D)
    up = matmul("up_proj", [(h22, W["w_up"])], "nn", F32).reshape(nb, S, 2 * D_FF)
    act = glu_fwd(up, ffn_w8, V["ffn_conv_b"])
    act2 = act.reshape(nb * S, D_FF)
    ffn = matmul("down_proj", [(act2, W["w_down"])], "nn", F32).reshape(nb, S, D)
    dx1, dffn, dgate2, d_ffn_post, loss = final_call(x1, ffn, target, m[5], V["ffn_post_norm"], tr)

    dffn2 = dffn.reshape(nb * S, D)
    dact = matmul("down_dgrad", [(dffn2, W["w_down"])], "nt", BF16).reshape(nb, S, D_FF)
    g_down = matmul_tn("down_wgrad", act2, dffn2)
    dup, ffn_rows = glu_bwd(up, ffn_w8, V["ffn_conv_b"], dact)
    dup2 = dup.reshape(nb * S, 2 * D_FF)
    dh2 = matmul("up_dgrad", [(dup2, W["w_up"])], "nt", BF16).reshape(nb, S, D)
    g_up = matmul_tn("up_wgrad", h22, dup2)
    dx_a, dmix, dgate1, dscale2, dshift2, d_mix_post, d_ffn_pre = rows_bwd(
        "postmix_bwd", fn_postmix, nb, S // tr, tr, pm_rows, pm_pb, pm_gl,
        [(dx1, D, 0, 0), (dh2, D, 0, 0)], [(0, F32), (1, BF16)])
    dmix2 = dmix.reshape(nb * S, D)
    dcat = matmul("out_dgrad", [(dmix2, W["w_out_p"])], "nt", BF16).reshape(nb, S, QP + D_INNER)
    g_out_p = jnp.concatenate([matmul_tn("out_wgrad_attn", o2, dmix2), matmul_tn("out_wgrad_ssd", ssd2, dmix2)], axis=0)
    dy, dxs_direct, dz, d_dexp, d_ssd_norm = rows_bwd(
        "ssd_finish_bwd", fn_ssd_finish, nb, S // tr, tr, fin_rows, [], fin_gl,
        [(dcat, D_INNER, QP // D_INNER, 0)], [(0, F32), (2, F32), (3, BF16)])
    dq_pre, dkv, dkr = attn_bwd(q_raw, kv, kr, dcat, cos_q, sin_q, cosS, sinS, tq)
    dq_pre = dq_pre.reshape(nb * S, QP)
    dkr_pre = rope_call("rope_dk", dkr, LANE, 0, cosT, -sinT, BF16, tr)
    dkv2 = dkv.reshape(nb * T, QP)
    dqn = matmul("q_dgrad", [(dq_pre, W["w_q_p"])], "nt", F32).reshape(nb, S, Q_RANK)
    g_q_p = matmul_tn("q_wgrad", qn2, dq_pre)
    dkvn = matmul("kv_dgrad", [(dkv2, W["w_kv"])], "nt", F32).reshape(nb, T, KV_RANK)
    g_kv = matmul_tn("kv_wgrad", kvn2, dkv2)
    early_grads = (_out_grad(g_out_p), glu_deinterleave(g_up), g_down, _q_grad(g_q_p), g_kv)
    early = hosted_all_to_all([_per_device(g, n) for g, n in zip(early_grads, EARLY)])
    dxbc2, ddt2, ssd_stats, *received = ssd_bwd(xbc, dt2, alog2, dtb2, hin, dy, C, early)
    ddt_block = jnp.concatenate([ddt2[0][..., :SSD_HEADS], ddt2[1][..., :SSD_HEADS],
                                 jnp.zeros((nb, T, LANE - 2 * SSD_HEADS), F32)], axis=-1).astype(BF16)
    dxbc_raw, ssd_rows = ssd_conv_bwd(u, ssd_w8, V["ssd_conv_b"], dxbc2, dxs_direct, C, tc)
    dcq, d_q_norm = rows_bwd("q_norm_bwd", fn_rms, nb, S // tr, tr, [(u, Q_RANK, OFF_CQ // Q_RANK, cblk)], [], [V["q_norm"]],
                             [(dqn, Q_RANK, 0, 0)], [(0, BF16)])
    dckv, d_kv_norm = rows_bwd("kv_norm_bwd", fn_rms, nb, T // tr, tr, [(u, KV_RANK, OFF_CKV // KV_RANK, 0)], [], [V["kv_norm"]],
                               [(dkvn, KV_RANK, 0, 0)], [(0, BF16)])

    def ctx_rows(t):
        return jnp.pad(t, ((0, 0), (C, 0), (0, 0)))

    du = [("cq", ctx_rows(dcq), OFF_CQ, Q_RANK), ("kr", dkr_pre, OFF_KR, LANE), ("ckv", dckv, OFF_CKV, KV_RANK),
          ("z", ctx_rows(dz), OFF_Z, D_INNER), ("xbc", dxbc_raw, OFF_XBC, XBC), ("dt", ddt_block, OFF_DT, LANE)]
    du = [(name, t.reshape(nb * T, w), off, w) for (name, t, off, w) in du]
    g = {name: matmul_tn("in_wgrad_" + name, h1, t) for (name, t, _, _) in du}
    g_in = jnp.concatenate([g["cq"], g["ckv"], g["kr"][:, KR_LANE:KR_LANE + ROPE], g["z"], g["xbc"],
                            g["dt"][:, :2 * SSD_HEADS]], axis=1)
    dh1, received_in = matmul("in_dgrad", [(t, W["w_in_p"][:, off:off + w]) for (_, t, off, w) in du], "nt", BF16,
                              hosted=hosted_all_to_all([_per_device(g_in, "w_in").astype(BF16)]))
    dh1 = dh1.reshape(nb, T, D)

    def fn_prenorm_res(xv, shift, scale, g):
        return fn_prenorm(xv, shift, scale, g) + (xv,)

    grad_x, dshift1, dscale1, d_mix_pre_x = rows_bwd(
        "prenorm_x_bwd", fn_prenorm_res, nb, S // tr, tr, [(x, D, 0, 0)], [m[0], m[1]], [V["mix_pre_norm"]],
        [(dh1, D, 0, cblk), (dx_a, D, 0, 0)], [(0, F32)])
    dshift_c, dscale_c, d_mix_pre_c = rows_bwd(
        "prenorm_c_bwd", fn_prenorm, nb, C // tr, tr, [(ctx, D, 0, 0)], [], [mc[0], mc[1], V["mix_pre_norm"]],
        [(dh1, D, 0, 0)], [])

    dmod_x = jnp.concatenate([dshift1, dscale1, dgate1, dshift2, dscale2, dgate2], axis=-1).reshape(nb, N_MOD * D)
    dmod_c = jnp.concatenate([dshift_c, dscale_c, jnp.zeros((1, (N_MOD - 2) * D), F32)], axis=-1)
    gv = dict(
        mix_pre_norm=d_mix_pre_x + d_mix_pre_c, mix_post_norm=d_mix_post, q_norm=d_q_norm, kv_norm=d_kv_norm,
        ssd_conv_w=ssd_rows[:SSD_K], ssd_conv_b=ssd_rows[SSD_K:SSD_K + 1],
        ssd_a_log=jnp.concatenate([ssd_stats[0:1, :SSD_HEADS], ssd_stats[2:3, :SSD_HEADS]], axis=1),
        ssd_dt_bias=jnp.concatenate([ssd_stats[1:2, :SSD_HEADS], ssd_stats[3:4, :SSD_HEADS]], axis=1),
        ssd_d=jnp.sum(d_dexp.reshape(SSD_HEADS, SSD_P), axis=1).reshape(1, SSD_HEADS), ssd_norm=d_ssd_norm,
        ffn_pre_norm=d_ffn_pre, ffn_post_norm=d_ffn_post,
        ffn_conv_w=ffn_rows[:FFN_K], ffn_conv_b=ffn_rows[FFN_K:FFN_K + 1])
    return loss, grad_x, dmod_x, dmod_c, gv, dict(zip(EARLY, received), w_in=received_in)


WEIGHT_ORDER = ("c_ctx", "w_mod", "b_mod", "mix_pre_norm", "mix_post_norm", "w_in", "q_norm", "w_q_up", "kv_norm",
                "w_kv_up", "ssd_conv_w", "ssd_conv_b", "ssd_a_log", "ssd_dt_bias", "ssd_d", "ssd_norm", "w_out",
                "ffn_pre_norm", "ffn_post_norm", "w_up", "ffn_conv_w", "ffn_conv_b", "w_down")
MATRICES = ("w_in", "w_q_up", "w_kv_up", "w_out", "w_up", "w_down")
ROW_SHARDED = ("w_out", "w_down")
SMALL_SUMMED = ("c_ctx", "mix_pre_norm", "mix_post_norm", "q_norm", "kv_norm", "ssd_conv_w", "ssd_conv_b", "ssd_a_log",
                "ssd_dt_bias", "ssd_d", "ssd_norm", "ffn_pre_norm", "ffn_post_norm", "ffn_conv_w", "ffn_conv_b")
MOD_ROWS = 8


def _whole(shards, name):
    if name in ROW_SHARDED:
        return shards.reshape(-1, shards.shape[-1])
    return jnp.concatenate([shards[j] for j in range(N_DEV)], axis=1)


def _per_device(g, name):
    if name in ROW_SHARDED:
        return g.reshape(N_DEV, -1, g.shape[-1])
    return jnp.stack(jnp.split(g, N_DEV, axis=1))


def kernel(x, c, ctx, c_ctx, w_mod, b_mod, mix_pre_norm, mix_post_norm, w_in, q_norm, w_q_up, kv_norm, w_kv_up, ssd_conv_w, ssd_conv_b, ssd_a_log, ssd_dt_bias, ssd_d, ssd_norm, w_out, ffn_pre_norm, ffn_post_norm, w_up, ffn_conv_w, ffn_conv_b, w_down, loss_target, m_c_ctx, m_w_mod, m_b_mod, m_mix_pre_norm, m_mix_post_norm, m_w_in, m_q_norm, m_w_q_up, m_kv_norm, m_w_kv_up, m_ssd_conv_w, m_ssd_conv_b, m_ssd_a_log, m_ssd_dt_bias, m_ssd_d, m_ssd_norm, m_w_out, m_ffn_pre_norm, m_ffn_post_norm, m_w_up, m_ffn_conv_w, m_ffn_conv_b, m_w_down, v_c_ctx, v_w_mod, v_b_mod, v_mix_pre_norm, v_mix_post_norm, v_w_in, v_q_norm, v_w_q_up, v_kv_norm, v_w_kv_up, v_ssd_conv_w, v_ssd_conv_b, v_ssd_a_log, v_ssd_dt_bias, v_ssd_d, v_ssd_norm, v_w_out, v_ffn_pre_norm, v_ffn_post_norm, v_w_up, v_ffn_conv_w, v_ffn_conv_b, v_w_down):
    weights = dict(c_ctx=c_ctx, w_mod=w_mod, b_mod=b_mod, mix_pre_norm=mix_pre_norm, mix_post_norm=mix_post_norm, w_in=w_in, q_norm=q_norm, w_q_up=w_q_up, kv_norm=kv_norm, w_kv_up=w_kv_up, ssd_conv_w=ssd_conv_w, ssd_conv_b=ssd_conv_b, ssd_a_log=ssd_a_log, ssd_dt_bias=ssd_dt_bias, ssd_d=ssd_d, ssd_norm=ssd_norm, w_out=w_out, ffn_pre_norm=ffn_pre_norm, ffn_post_norm=ffn_post_norm, w_up=w_up, ffn_conv_w=ffn_conv_w, ffn_conv_b=ffn_conv_b, w_down=w_down)
    mom1 = dict(c_ctx=m_c_ctx, w_mod=m_w_mod, b_mod=m_b_mod, mix_pre_norm=m_mix_pre_norm, mix_post_norm=m_mix_post_norm, w_in=m_w_in, q_norm=m_q_norm, w_q_up=m_w_q_up, kv_norm=m_kv_norm, w_kv_up=m_w_kv_up, ssd_conv_w=m_ssd_conv_w, ssd_conv_b=m_ssd_conv_b, ssd_a_log=m_ssd_a_log, ssd_dt_bias=m_ssd_dt_bias, ssd_d=m_ssd_d, ssd_norm=m_ssd_norm, w_out=m_w_out, ffn_pre_norm=m_ffn_pre_norm, ffn_post_norm=m_ffn_post_norm, w_up=m_w_up, ffn_conv_w=m_ffn_conv_w, ffn_conv_b=m_ffn_conv_b, w_down=m_w_down)
    mom2 = dict(c_ctx=v_c_ctx, w_mod=v_w_mod, b_mod=v_b_mod, mix_pre_norm=v_mix_pre_norm, mix_post_norm=v_mix_post_norm, w_in=v_w_in, q_norm=v_q_norm, w_q_up=v_w_q_up, kv_norm=v_kv_norm, w_kv_up=v_w_kv_up, ssd_conv_w=v_ssd_conv_w, ssd_conv_b=v_ssd_conv_b, ssd_a_log=v_ssd_a_log, ssd_dt_bias=v_ssd_dt_bias, ssd_d=v_ssd_d, ssd_norm=v_ssd_norm, w_out=v_w_out, ffn_pre_norm=v_ffn_pre_norm, ffn_post_norm=v_ffn_post_norm, w_up=v_w_up, ffn_conv_w=v_ffn_conv_w, ffn_conv_b=v_ffn_conv_b, w_down=v_w_down)
    nb, S, D = x.shape
    me = 4 * lax.axis_index("x") + 2 * lax.axis_index("y") + lax.axis_index("c")

    *first, c_all, ssd_w_sh, ffn_w_sh = all_gather(
        "gather_first", [weights[n][0].astype(BF16) for n in FIRST] + [c, ssd_conv_w[0], ffn_conv_w[0]])
    W = first_weights_to_internal(*[_whole(s, n) for n, s in zip(FIRST, first)])
    late_shards = [weights[n][0].astype(BF16) for n in LATE_WEIGHTS]
    V = {n: weights[n].reshape(1, -1) for n in SMALL_SUMMED if n != "c_ctx"}
    V["ssd_conv_w"] = _whole(ssd_w_sh, "ssd_conv_w")
    V["ffn_conv_w"] = _whole(ffn_w_sh, "ffn_conv_w")

    n_all = N_DEV * nb
    mod_rows = -(-(n_all + 1) // 8) * 8
    c_pad = jnp.concatenate([c_all.reshape(n_all, D), c_ctx.reshape(1, D), jnp.zeros((mod_rows - n_all - 1, D), F32)], axis=0)
    mod_cols = w_mod.shape[2]
    b_mine = lax.dynamic_slice(b_mod, (0, me * mod_cols), (1, mod_cols))
    mod_part = matmul("mod_proj", [(c_pad, w_mod[0])], "nn", F32, bias=b_mine, silu_a=True)
    mod_all = _whole(all_gather("gather_mod", [mod_part])[0], "w_mod")
    mod_x = lax.dynamic_slice(mod_all, (me * nb, 0), (nb, mod_all.shape[1]))
    mod_c = mod_all[n_all:n_all + 1]

    loss, grad_x, dmod_x, dmod_c, gv, slots = local_step(x, ctx, loss_target, mod_x, mod_c, W, late_shards, V)

    dmod_mine = jnp.concatenate([dmod_x, dmod_c, jnp.zeros((MOD_ROWS - nb - 1, dmod_x.shape[1]), F32)], axis=0)
    dmod_all = all_gather("gather_dmod", [dmod_mine])[0]
    dmod_ctx = sum_slots("sum_dmod_ctx", dmod_all[:, nb:nb + 1].reshape(N_DEV, -1, LANE)).reshape(1, -1)
    dmod_full = jnp.concatenate([dmod_all[:, :nb].reshape(n_all, -1), dmod_ctx,
                                 jnp.zeros((mod_rows - n_all - 1, dmod_ctx.shape[1]), F32)], axis=0)
    (g_b_mod,) = ew_call("mod_bias_grad", lambda t: (jnp.sum(t, axis=0, keepdims=True),), [dmod_full], [((1, dmod_full.shape[1]), F32)])
    dmod_cols = lax.dynamic_slice(dmod_full, (0, me * mod_cols), (mod_rows, mod_cols))
    g_w_mod = matmul_tn("mod_wgrad", c_pad, dmod_cols, silu_a=True)
    dsilu_ctx = matmul("mod_dgrad_ctx", [(dmod_cols[n_all:n_all + 8], w_mod[0])], "nt", F32)[0:1]

    def silu_vjp(cc, ct):
        return (jax.vjp(_silu, cc)[1](ct)[0],)

    (g_c_ctx_part,) = ew_call("c_ctx_grad", silu_vjp, [c_ctx.reshape(1, D), dsilu_ctx], [((1, D), F32)])

    gv = dict(gv, c_ctx=g_c_ctx_part)
    small_parts = [loss] + [gv[n] for n in SMALL_SUMMED]
    small_sum = sum_slots("sum_small", all_gather("gather_small_grads", [_pack_rows(small_parts)])[0])
    summed = _unpack_rows(small_sum, [p.shape for p in small_parts])
    loss_out = summed[0][0, 0]
    grads = {n: g.reshape(weights[n].shape) if n not in ("ssd_conv_w", "ffn_conv_w") else g for n, g in zip(SMALL_SUMMED, summed[1:])}
    for n in ("ssd_conv_w", "ffn_conv_w"):
        cols = weights[n].shape[2]
        grads[n] = lax.dynamic_slice(grads[n], (0, me * cols), (grads[n].shape[0], cols)).reshape(weights[n].shape)
    grads["b_mod"] = g_b_mod.reshape(b_mod.shape)

    slots = dict(slots, w_mod=g_w_mod[None])
    delta, new_m, new_v = {}, {}, {}
    for n in MATRICES + ("w_mod",):
        g, d, mn, vn = adamw_matrix("adamw_" + n, weights[n][0], slots[n], mom1[n][0], mom2[n][0])
        grads[n], delta[n], new_m[n], new_v[n] = [t.reshape(weights[n].shape) for t in (g, d, mn, vn)]
    small = [n for n in WEIGHT_ORDER if n not in slots]

    def two_d(t):
        return t.reshape(-1, t.shape[-1])

    ds, ms, vs = adamw_small(*[[two_d(t[n]) for n in small] for t in (weights, grads, mom1, mom2)])
    for n, d, mn, vn in zip(small, ds, ms, vs):
        delta[n], new_m[n], new_v[n] = [t.reshape(weights[n].shape) for t in (d, mn, vn)]
    return (loss_out, grad_x, *[t[n] for t in (grads, delta, new_m, new_v) for n in WEIGHT_ORDER])
```

```python
import math

import jax
import jax.numpy as jnp
import numpy as np
from jax import lax
from jax.experimental import pallas as pl
from jax.experimental.pallas import tpu as pltpu

F32 = jnp.float32
BF16 = jnp.bfloat16
MESH = pl.DeviceIdType.MESH

D_MODEL = 1024
GRID_W = 64
N_HEADS = 16
NOPE = 64
ROPE = 32
V_DIM = 64
Q_RANK = 384
KV_RANK = 256
ROPE_THETA = 10000.0
ATTN_SCALE = (NOPE + ROPE) ** -0.5
SSD_HEADS = 16
SSD_P = 64
SSD_GROUPS = 2
SSD_N = 128
SSD_K = 5
CHUNK = 128
D_INNER = SSD_HEADS * SSD_P
GN = SSD_GROUPS * SSD_N
XBC = D_INNER + 2 * GN
D_FF = 2816
FFN_K = 3
N_MOD = 6
EPS = 1e-6
IN_SPLITS = (Q_RANK, KV_RANK, ROPE, D_INNER, XBC, 2 * SSD_HEADS)
IN_WIDTH = sum(IN_SPLITS)
N_DEV = 8

ADAM_LR = 0.001
ADAM_B1 = 0.9
ADAM_B2 = 0.999
ADAM_EPS = 1e-08
ADAM_WD = 0.01
ADAM_STEP = 10

LANE = 128
HEAD_BLOCK = 128
OFF_CQ = 0
OFF_KR = 384
OFF_CKV = 512
OFF_Z = 1024
OFF_XBC = 2048
OFF_DT = 3584
WIN_P = 3840
KR_LANE = 64
QP = N_HEADS * HEAD_BLOCK

VMEM_LIMIT_V7X = 56 * 1024 * 1024
NEG_BIG = -1e30


def _cparams(*sem):
    return pltpu.CompilerParams(dimension_semantics=sem, vmem_limit_bytes=VMEM_LIMIT_V7X)


def _tile(n, target, mult=128):
    if n <= target:
        return n
    t = (target // mult) * mult
    while t >= mult:
        if n % t == 0:
            return t
        t -= mult
    return n


def _silu(x):
    return x * jax.nn.sigmoid(x)


def _rms(x, g):
    return x * lax.rsqrt(jnp.mean(x * x, axis=-1, keepdims=True) + EPS) * g


WHOLE_K_WIDE = 2048


def matmul(name, pairs, mode, out_dtype, *, bias=None, silu_a=False, hosted=None):
    n_pairs = len(pairs)
    M = pairs[0][0].shape[0]
    N = pairs[0][1].shape[1] if mode == "nn" else pairs[0][1].shape[0]
    k_total = sum(a.shape[1] for a, _ in pairs)
    tm = _tile(M, 1024 if k_total <= 3584 else 512, 8)
    tn = _tile(N, 2816 if k_total <= WHOLE_K_WIDE else 1024)
    dims = (((1,), (0,)), ((), ())) if mode == "nn" else (((1,), (1,)), ((), ()))
    n_own = 2 * n_pairs + (bias is not None)
    n_ex = hosted.n if hosted else 0

    def body(*refs):
        o_ref = refs[n_own + n_ex]
        if hosted:
            j, i = pl.program_id(0), pl.program_id(1)
            begin_exchange, end_exchange = hosted.steps(
                refs[n_own:n_own + n_ex], refs[n_own + n_ex + 1:n_own + 2 * n_ex + 1], refs[n_own + 2 * n_ex + 1:],
                jnp.logical_and(j == 0, i == 0), jnp.logical_and(j == N // tn - 1, i == M // tm - 1))
            begin_exchange()
        acc = None
        for p in range(n_pairs):
            a = refs[2 * p][...]
            if silu_a:
                a = _silu(a.astype(F32))
            d = lax.dot_general(a.astype(BF16), refs[2 * p + 1][...].astype(BF16), dims, preferred_element_type=F32)
            acc = d if acc is None else acc + d
        if bias is not None:
            acc = acc + refs[2 * n_pairs][...]
        o_ref[...] = acc.astype(o_ref.dtype)
        if hosted:
            end_exchange()

    in_specs, args = [], []
    for a, b in pairs:
        K = a.shape[1]
        in_specs.append(pl.BlockSpec((tm, K), lambda j, i: (i, 0)))
        in_specs.append(pl.BlockSpec((K, tn), lambda j, i: (0, j)) if mode == "nn" else pl.BlockSpec((tn, K), lambda j, i: (j, 0)))
        args += [a, b]
    if bias is not None:
        in_specs.append(pl.BlockSpec((1, tn), lambda j, i: (0, j)))
        args.append(bias)
    out_spec = pl.BlockSpec((tm, tn), lambda j, i: (i, j))
    out_shape = jax.ShapeDtypeStruct((M, N), out_dtype)
    if not hosted:
        return pl.pallas_call(
            body, name=name, grid=(N // tn, M // tm), in_specs=in_specs, out_specs=out_spec, out_shape=out_shape,
            compiler_params=_cparams("arbitrary", "arbitrary"),
        )(*args)
    return pl.pallas_call(
        body, name=name, grid=(N // tn, M // tm), in_specs=in_specs + hosted.specs,
        out_specs=[out_spec] + hosted.specs, out_shape=[out_shape] + hosted.out_shape, scratch_shapes=hosted.scratch,
        compiler_params=_cparams("arbitrary", "arbitrary"),
    )(*args, *hosted.arrays)


def matmul_tn(name, a, b, out_dtype=F32, *, silu_a=False, tm=1408, tn=2048, tk=2048):
    R, M = a.shape
    N = b.shape[1]
    tm = _tile(M, tm)
    tn = _tile(N, tn)
    tk = _tile(R, tk, 8)
    nk = R // tk

    def body(a_ref, b_ref, o_ref, acc):
        k = pl.program_id(2)

        @pl.when(k == 0)
        def _():
            acc[...] = jnp.zeros_like(acc)

        x = a_ref[...]
        if silu_a:
            x = _silu(x.astype(F32))
        acc[...] += lax.dot_general(x.astype(BF16), b_ref[...].astype(BF16), (((0,), (0,)), ((), ())),
                                    preferred_element_type=F32)

        @pl.when(k == nk - 1)
        def _():
            o_ref[...] = acc[...].astype(o_ref.dtype)

    return pl.pallas_call(
        body, name=name, grid=(M // tm, N // tn, nk),
        in_specs=[pl.BlockSpec((tk, tm), lambda i, j, k: (k, i)), pl.BlockSpec((tk, tn), lambda i, j, k: (k, j))],
        out_specs=pl.BlockSpec((tm, tn), lambda i, j, k: (i, j)),
        out_shape=jax.ShapeDtypeStruct((M, N), out_dtype),
        scratch_shapes=[pltpu.VMEM((tm, tn), F32)],
        compiler_params=_cparams("arbitrary", "arbitrary", "arbitrary"),
    )(a, b)


def _row_specs(rin, pbin, glin, tr):
    specs = [pl.BlockSpec((1, tr, w), lambda b, i, cb=cb, ro=ro, bo=(e[4] if len(e) > 4 else 0): (b + bo, i + ro, cb))
             for e in rin for (_, w, cb, ro) in [e[:4]]]
    specs += [pl.BlockSpec((1, 1, a.shape[-1]), lambda b, i: (b, 0, 0)) for a in pbin]
    specs += [pl.BlockSpec((1, a.shape[-1]), lambda b, i: (0, 0)) for a in glin]
    return specs


def rows_fwd(name, fn, nb, nblk, tr, rin, pbin, glin, outs):
    nr, npb, ngl = len(rin), len(pbin), len(glin)
    n_in = nr + npb + ngl

    def body(*refs):
        args = [r[0].astype(F32) for r in refs[:nr + npb]] + [r[...] for r in refs[nr + npb:n_in]]
        res = fn(*args)
        for o, v in zip(refs[n_in:], res):
            o[0] = v.astype(o.dtype)

    return pl.pallas_call(
        body, name=name, grid=(nb, nblk), in_specs=_row_specs(rin, pbin, glin, tr),
        out_specs=[pl.BlockSpec((1, tr, w), lambda b, i: (b, i, 0)) for (w, _) in outs],
        out_shape=[jax.ShapeDtypeStruct((nb, nblk * tr, w), dt) for (w, dt) in outs],
        compiler_params=_cparams("arbitrary", "arbitrary"),
    )(*[e[0] for e in rin], *pbin, *glin)


def rows_bwd(name, fn, nb, nblk, tr, rin, pbin, glin, cts, want):
    nr, npb, ngl, nct = len(rin), len(pbin), len(glin), len(cts)
    n_in = nr + npb + ngl

    def body(*refs):
        b, i = pl.program_id(0), pl.program_id(1)
        args = [r[0].astype(F32) for r in refs[:nr + npb]] + [r[...] for r in refs[nr + npb:n_in]]
        ct = tuple(r[0].astype(F32) for r in refs[n_in:n_in + nct])
        _, vjp = jax.vjp(fn, *args)
        g = vjp(ct)
        orefs = refs[n_in + nct:]
        for o, (idx, _) in zip(orefs, want):
            o[0] = g[idx].astype(o.dtype)
        pb_refs = orefs[len(want):len(want) + npb]
        gl_refs = orefs[len(want) + npb:]

        @pl.when(i == 0)
        def _():
            for o, v in zip(pb_refs, g[nr:nr + npb]):
                o[0] = v

        @pl.when(i > 0)
        def _():
            for o, v in zip(pb_refs, g[nr:nr + npb]):
                o[0] += v

        first = jnp.logical_and(b == 0, i == 0)

        @pl.when(first)
        def _():
            for o, v in zip(gl_refs, g[nr + npb:]):
                o[...] = v

        @pl.when(jnp.logical_not(first))
        def _():
            for o, v in zip(gl_refs, g[nr + npb:]):
                o[...] += v

    out_specs = [pl.BlockSpec((1, tr, rin[idx][1]), lambda b, i: (b, i, 0)) for (idx, _) in want]
    out_shape = [jax.ShapeDtypeStruct((nb, nblk * tr, rin[idx][1]), dt) for (idx, dt) in want]
    out_specs += [pl.BlockSpec((1, 1, a.shape[-1]), lambda b, i: (b, 0, 0)) for a in pbin]
    out_shape += [jax.ShapeDtypeStruct((nb, 1, a.shape[-1]), F32) for a in pbin]
    out_specs += [pl.BlockSpec((1, a.shape[-1]), lambda b, i: (0, 0)) for a in glin]
    out_shape += [jax.ShapeDtypeStruct((1, a.shape[-1]), F32) for a in glin]
    return pl.pallas_call(
        body, name=name, grid=(nb, nblk),
        in_specs=_row_specs(rin, pbin, glin, tr) + _row_specs(cts, [], [], tr),
        out_specs=out_specs, out_shape=out_shape,
        compiler_params=_cparams("arbitrary", "arbitrary"),
    )(*[e[0] for e in rin], *pbin, *glin, *[e[0] for e in cts])


def ew_call(name, fn, ins, outs):
    def body(*refs):
        res = fn(*[r[...] for r in refs[:len(ins)]])
        for o, v in zip(refs[len(ins):], res):
            o[...] = v.astype(o.dtype)

    return pl.pallas_call(body, name=name, out_shape=[jax.ShapeDtypeStruct(s, dt) for (s, dt) in outs])(*ins)


def fn_prenorm(x, shift, scale, g):
    return (_rms(x, g) * (1.0 + scale) + shift,)


def fn_rms(x, g):
    return (_rms(x, g),)


def fn_ssd_finish(yf, yr, xs, z, dexp, nw):
    y = yf + yr + dexp * xs
    return (_rms(y * _silu(z), nw),)


def fn_postmix(x, mix, gate1, scale2, shift2, post_g, pre_g):
    x1 = x + gate1 * _rms(mix, post_g)
    h2 = _rms(x1, pre_g) * (1.0 + scale2) + shift2
    return x1, h2


def final_call(x1, ffn, target, gate2, post_g, tr):
    nb, S, D = x1.shape
    nblk = S // tr

    def body(x1_ref, f_ref, t_ref, g2_ref, pg_ref, dx1_ref, df_ref, dg2_ref, dpg_ref, loss_ref):
        b, i = pl.program_id(0), pl.program_id(1)
        tgt = t_ref[0]

        def lossfn(x1v, fv, g2, pg):
            e = x1v + g2 * _rms(fv, pg) - tgt
            return 0.5 * jnp.sum(jnp.mean(e * e, axis=-1, keepdims=True))

        val, (dx1, df, dg2, dpg) = jax.value_and_grad(lossfn, argnums=(0, 1, 2, 3))(
            x1_ref[0], f_ref[0].astype(F32), g2_ref[0], pg_ref[...])
        dx1_ref[0] = dx1
        df_ref[0] = df.astype(df_ref.dtype)
        lv = jnp.full((1, LANE), val, F32)

        @pl.when(i == 0)
        def _():
            dg2_ref[0] = dg2

        @pl.when(i > 0)
        def _():
            dg2_ref[0] += dg2

        first = jnp.logical_and(b == 0, i == 0)

        @pl.when(first)
        def _():
            dpg_ref[...] = dpg
            loss_ref[...] = lv

        @pl.when(jnp.logical_not(first))
        def _():
            dpg_ref[...] += dpg
            loss_ref[...] += lv

    row = pl.BlockSpec((1, tr, D), lambda b, i: (b, i, 0))
    pb = pl.BlockSpec((1, 1, D), lambda b, i: (b, 0, 0))
    gl = pl.BlockSpec((1, D), lambda b, i: (0, 0))
    return pl.pallas_call(
        body, name="loss_head", grid=(nb, nblk), in_specs=[row, row, row, pb, gl],
        out_specs=[row, row, pb, gl, pl.BlockSpec((1, LANE), lambda b, i: (0, 0))],
        out_shape=[jax.ShapeDtypeStruct((nb, S, D), F32), jax.ShapeDtypeStruct((nb, S, D), BF16),
                   jax.ShapeDtypeStruct((nb, 1, D), F32), jax.ShapeDtypeStruct((1, D), F32),
                   jax.ShapeDtypeStruct((1, LANE), F32)],
        compiler_params=_cparams("arbitrary", "arbitrary"),
    )(x1, ffn, target, gate2, post_g)


def _rotate_half(t):
    lane = lax.broadcasted_iota(jnp.int32, t.shape, 1)
    return jnp.where((lane & 15) < 8, -pltpu.roll(t, LANE - 8, 1), pltpu.roll(t, 8, 1))


def rope_call(name, x, width, colblk, cos, sin, out_dtype, tr):
    nb = x.shape[0]
    R = cos.shape[0]
    nblk = R // tr

    def body(x_ref, c_ref, s_ref, o_ref):
        c, s = c_ref[...], s_ref[...]
        for h in range(width // LANE):
            t = x_ref[0, :, h * LANE:(h + 1) * LANE].astype(F32)
            o_ref[0, :, h * LANE:(h + 1) * LANE] = (t * c + _rotate_half(t) * s).astype(o_ref.dtype)

    tab = pl.BlockSpec((tr, LANE), lambda b, i: (i, 0))
    return pl.pallas_call(
        body, name=name, grid=(nb, nblk),
        in_specs=[pl.BlockSpec((1, tr, width), lambda b, i: (b, i, colblk)), tab, tab],
        out_specs=pl.BlockSpec((1, tr, width), lambda b, i: (b, i, 0)),
        out_shape=jax.ShapeDtypeStruct((nb, R, width), out_dtype),
        compiler_params=_cparams("arbitrary", "arbitrary"),
    )(x, cos, sin)


def rope_tables(n_ctx, seq):
    n_rows = seq // GRID_W
    row = np.repeat(np.arange(n_rows), GRID_W).astype(np.float32)
    col = np.tile(np.arange(GRID_W), n_rows).astype(np.float32)
    axis_dim = ROPE // 2
    inv_freq = jnp.asarray(ROPE_THETA, F32) ** (-jnp.arange(0, axis_dim, 2, dtype=F32) / axis_dim)
    ang_r = jnp.asarray(row)[:, None] * inv_freq
    ang_c = jnp.asarray(col)[:, None] * inv_freq
    ang = jnp.concatenate([ang_r, ang_r, ang_c, ang_c], axis=-1)
    cos = jnp.ones((n_ctx + seq, LANE), F32).at[n_ctx:, KR_LANE:KR_LANE + ROPE].set(jnp.cos(ang))
    sin = jnp.zeros((n_ctx + seq, LANE), F32).at[n_ctx:, KR_LANE:KR_LANE + ROPE].set(jnp.sin(ang))
    return cos, sin


Q_PRESCALE = ATTN_SCALE * math.log2(math.e)


def _attn_weights(q, kc):
    s2 = lax.dot_general(q, kc, (((1,), (1,)), ((), ())), preferred_element_type=F32)
    e = jnp.exp2(s2 - jnp.max(s2, axis=1, keepdims=True))
    return e, 1.0 / jnp.sum(e, axis=1, keepdims=True)


def _key_block(kv, kr):
    lane = lax.broadcasted_iota(jnp.int32, kv.shape, 1)
    return jnp.where(lane < NOPE, kv, kr)


def _rotated_query(q_ref, cos_ref, sin_ref):
    t = q_ref[0].astype(F32)
    return (t * cos_ref[...] + _rotate_half(t) * sin_ref[...]).astype(BF16)


def attn_fwd(q_raw, kv, kr, cos_q, sin_q, tq):
    nb, S, _ = q_raw.shape
    T = kv.shape[1]

    def body(q_ref, kv_ref, kr_ref, c_ref, s_ref, o_ref):
        kvv = kv_ref[0]
        e, r = _attn_weights(_rotated_query(q_ref, c_ref, s_ref), _key_block(kvv, kr_ref[0]))
        o = lax.dot_general(e.astype(BF16), kvv, (((1,), (0,)), ((), ())), preferred_element_type=F32) * r
        lane = lax.broadcasted_iota(jnp.int32, o.shape, 1)
        o_ref[0] = jnp.where(lane >= NOPE, o, 0.0).astype(o_ref.dtype)

    return pl.pallas_call(
        body, name="attn_fwd", grid=(nb, N_HEADS, S // tq),
        in_specs=[pl.BlockSpec((1, tq, HEAD_BLOCK), lambda b, h, i: (b, i, h)),
                  pl.BlockSpec((1, T, HEAD_BLOCK), lambda b, h, i: (b, 0, h)),
                  pl.BlockSpec((1, T, HEAD_BLOCK), lambda b, h, i: (b, 0, 0)),
                  pl.BlockSpec((tq, LANE), lambda b, h, i: (i, 0)), pl.BlockSpec((tq, LANE), lambda b, h, i: (i, 0))],
        out_specs=pl.BlockSpec((1, tq, HEAD_BLOCK), lambda b, h, i: (b, i, h)),
        out_shape=jax.ShapeDtypeStruct((nb, S, QP), BF16),
        compiler_params=_cparams("arbitrary", "arbitrary", "arbitrary"),
    )(q_raw, kv, kr, cos_q, sin_q)


def attn_bwd(q_raw, kv, kr, do, cos_q, sin_q, cos, sin, tq):
    nb, S, _ = q_raw.shape
    T = kv.shape[1]

    def body(q_ref, kv_ref, kr_ref, do_ref, cq_ref, sq_ref, c_ref, s_ref, dq_ref, dkv_ref, dkr_ref):
        h, i = pl.program_id(1), pl.program_id(2)

        @pl.when(i == 0)
        def _():
            dkv_ref[...] = jnp.zeros_like(dkv_ref)

        @pl.when(jnp.logical_and(h == 0, i == 0))
        def _():
            dkr_ref[...] = jnp.zeros_like(dkr_ref)

        qv, kvv, dov = _rotated_query(q_ref, cq_ref, sq_ref), kv_ref[0], do_ref[0]
        kc = _key_block(kvv, kr_ref[0])
        e, r = _attn_weights(qv, kc)
        dor = (dov.astype(F32) * r).astype(BF16)
        dpr = lax.dot_general(dor, kvv, (((1,), (1,)), ((), ())), preferred_element_type=F32)
        ds = (e * (dpr - r * jnp.sum(dpr * e, axis=1, keepdims=True))).astype(BF16)
        dq = lax.dot_general(ds, kc, (((1,), (0,)), ((), ())), preferred_element_type=F32) * ATTN_SCALE
        dq_ref[0] = (dq * c_ref[...] - _rotate_half(dq) * s_ref[...]).astype(dq_ref.dtype)
        dkc = lax.dot_general(ds, qv, (((0,), (0,)), ((), ())), preferred_element_type=F32) * math.log(2.0)
        dv = lax.dot_general(e.astype(BF16), dor, (((0,), (0,)), ((), ())), preferred_element_type=F32)
        lane = lax.broadcasted_iota(jnp.int32, dkc.shape, 1)
        dkv_ref[0] += jnp.where(lane < NOPE, dkc, dv)
        dkr_ref[0] += jnp.where(lane >= NOPE, dkc, 0.0)

    qspec = pl.BlockSpec((1, tq, HEAD_BLOCK), lambda b, h, i: (b, i, h))
    kspec = pl.BlockSpec((1, T, HEAD_BLOCK), lambda b, h, i: (b, 0, h))
    rspec = pl.BlockSpec((1, T, HEAD_BLOCK), lambda b, h, i: (b, 0, 0))
    tab = pl.BlockSpec((tq, LANE), lambda b, h, i: (i, 0))
    return pl.pallas_call(
        body, name="attn_bwd", grid=(nb, N_HEADS, S // tq),
        in_specs=[qspec, kspec, rspec, qspec, tab, tab, tab, tab], out_specs=[qspec, kspec, rspec],
        out_shape=[jax.ShapeDtypeStruct((nb, S, QP), BF16), jax.ShapeDtypeStruct((nb, T, QP), F32),
                   jax.ShapeDtypeStruct((nb, T, HEAD_BLOCK), F32)],
        compiler_params=_cparams("arbitrary", "arbitrary", "arbitrary"),
    )(q_raw, kv, kr, do, cos_q, sin_q, cos, sin)


CONV_HALO = 8


def _segments(n, n_ctx):
    if n_ctx == 0:
        return [(0, n, CONV_HALO)]
    return [(0, n_ctx, CONV_HALO), (n_ctx, n - n_ctx, 2 * CONV_HALO + n_ctx)]


def _halo_scratch(n, n_ctx, tc):
    return pltpu.VMEM((n + CONV_HALO * (len(_segments(n, n_ctx)) + 1), tc), F32)


def _zero_halos(scr, segs):
    z = jnp.zeros((CONV_HALO, scr.shape[1]), scr.dtype)
    scr[0:CONV_HALO, :] = z
    for (_, rows, off) in segs:
        scr[off + rows:off + rows + CONV_HALO, :] = z


CONV_BLOCK_MAX = 256


def _conv_block(n, n_ctx):
    return _tile(math.gcd(n_ctx, n - n_ctx) if n_ctx else n, CONV_BLOCK_MAX, 8)


def _window(scr, off, r0, blk):
    return scr[pl.ds(pl.multiple_of(off - CONV_HALO + r0, 8), blk + 2 * CONV_HALO), :]


def _shifted(win, s):
    v = win if s == 0 else pltpu.roll(win, (-s) % win.shape[0], 0)
    return v[CONV_HALO:win.shape[0] - CONV_HALO]


def _tap_blocks(win, k, sign):
    return [_shifted(win, sign * (o - k // 2)) for o in range(k)]


def _taps(blocks, w):
    acc = None
    for o, blk in enumerate(blocks):
        t = w[o:o + 1, :] * blk
        acc = t if acc is None else acc + t
    return acc


def _tap_grads(xblocks, dpre):
    k = len(xblocks)
    sub8 = lax.broadcasted_iota(jnp.int32, (8, dpre.shape[1]), 0)
    out = jnp.where(sub8 == k, jnp.sum(dpre, axis=0, keepdims=True), 0.0)
    for o, blk in enumerate(xblocks):
        out = out + jnp.where(sub8 == o, jnp.sum(dpre * blk, axis=0, keepdims=True), 0.0)
    return out


def _row_blocks(rows, blk, fn, init=0):
    return lax.fori_loop(0, rows // blk, lambda i, c: fn(pl.multiple_of(i * blk, blk), c), init)


def _gelu(x):
    return 0.5 * x * (1.0 + lax.erf(x * (1.0 / math.sqrt(2.0))))


def _gelu_and_grad(x):
    cdf = 0.5 * (1.0 + lax.erf(x * (1.0 / math.sqrt(2.0))))
    return x * cdf, cdf + x * jnp.exp(-0.5 * x * x) * (1.0 / math.sqrt(2.0 * math.pi))


def ssd_conv_fwd(u, w8, bias, n_ctx, tc):
    nb, T, _ = u.shape
    cb0 = OFF_XBC // tc

    segs, blk = _segments(T, n_ctx), _conv_block(T, n_ctx)

    def body(x_ref, w_ref, b_ref, o_ref, xs):
        _zero_halos(xs, segs)
        for (start, rows, off) in segs:
            xs[off:off + rows, :] = x_ref[0, start:start + rows, :]
        w, bias_v = w_ref[...], b_ref[...]
        for (start, rows, off) in segs:
            def block(r0, carry, start=start, off=off):
                pre = bias_v + _taps(_tap_blocks(_window(xs, off, r0, blk), SSD_K, 1), w)
                o_ref[0, pl.ds(pl.multiple_of(start + r0, blk), blk), :] = _silu(pre)
                return carry

            _row_blocks(rows, blk, block)

    return pl.pallas_call(
        body, name="ssd_conv_fwd", grid=(nb, XBC // tc),
        in_specs=[pl.BlockSpec((1, T, tc), lambda b, j: (b, 0, cb0 + j)),
                  pl.BlockSpec((8, tc), lambda b, j: (0, j)), pl.BlockSpec((1, tc), lambda b, j: (0, j))],
        out_specs=pl.BlockSpec((1, T, tc), lambda b, j: (b, 0, j)),
        out_shape=jax.ShapeDtypeStruct((nb, T, XBC), F32),
        scratch_shapes=[_halo_scratch(T, n_ctx, tc)],
        compiler_params=_cparams("arbitrary", "arbitrary"),
    )(u, w8, bias)


def ssd_conv_bwd(u, w8, bias, dxbc, dxs_direct, n_ctx, tc):
    nb, T, _ = u.shape
    cb0 = OFF_XBC // tc
    n_direct = D_INNER // tc

    segs, blk = _segments(T, n_ctx), _conv_block(T, n_ctx)

    def body(x_ref, w_ref, b_ref, d0_ref, d1_ref, dd_ref, dx_ref, dw_ref, xs, ds):
        j, b = pl.program_id(0), pl.program_id(1)
        _zero_halos(xs, segs)
        _zero_halos(ds, segs)
        for (start, rows, off) in segs:
            xs[off:off + rows, :] = x_ref[0, start:start + rows, :]
        w, bias_v = w_ref[...], b_ref[...]
        has_direct = (j < n_direct).astype(F32)
        rows = jnp.zeros((8, tc), F32)
        for (start, n_rows, off) in segs:
            def block(r0, acc, start=start, off=off):
                xblocks = _tap_blocks(_window(xs, off, r0, blk), SSD_K, 1)
                pre = bias_v + _taps(xblocks, w)
                d = d0_ref[0, 0, pl.ds(pl.multiple_of(start + r0, blk), blk), :] + d1_ref[0, 0, pl.ds(pl.multiple_of(start + r0, blk), blk), :]
                if start == n_ctx:
                    d = d + dd_ref[0, pl.ds(r0, blk), :] * has_direct
                sg = jax.nn.sigmoid(pre)
                dpre = d * (sg * (1.0 + pre * (1.0 - sg)))
                ds[pl.ds(pl.multiple_of(off + r0, 8), blk), :] = dpre
                return acc + _tap_grads(xblocks, dpre)

            rows = _row_blocks(n_rows, blk, block, rows)
        for (start, n_rows, off) in segs:
            def block_dx(r0, carry, start=start, off=off):
                dx_ref[0, pl.ds(pl.multiple_of(start + r0, blk), blk), :] = _taps(_tap_blocks(_window(ds, off, r0, blk), SSD_K, -1), w).astype(dx_ref.dtype)
                return carry

            _row_blocks(n_rows, blk, block_dx)

        @pl.when(b == 0)
        def _():
            dw_ref[...] = rows

        @pl.when(b > 0)
        def _():
            dw_ref[...] += rows

    dspec0 = pl.BlockSpec((1, 1, T, tc), lambda j, b: (0, b, 0, j))
    dspec1 = pl.BlockSpec((1, 1, T, tc), lambda j, b: (1, b, 0, j))
    return pl.pallas_call(
        body, name="ssd_conv_bwd", grid=(XBC // tc, nb),
        in_specs=[pl.BlockSpec((1, T, tc), lambda j, b: (b, 0, cb0 + j)),
                  pl.BlockSpec((8, tc), lambda j, b: (0, j)), pl.BlockSpec((1, tc), lambda j, b: (0, j)),
                  dspec0, dspec1,
                  pl.BlockSpec((1, T - n_ctx, tc), lambda j, b: (b, 0, jnp.minimum(j, n_direct - 1)))],
        out_specs=[pl.BlockSpec((1, T, tc), lambda j, b: (b, 0, j)), pl.BlockSpec((8, tc), lambda j, b: (0, j))],
        out_shape=[jax.ShapeDtypeStruct((nb, T, XBC), BF16), jax.ShapeDtypeStruct((8, XBC), F32)],
        scratch_shapes=[_halo_scratch(T, n_ctx, tc), _halo_scratch(T, n_ctx, tc)],
        compiler_params=_cparams("arbitrary", "arbitrary"),
    )(u, w8, bias, dxbc, dxbc, dxs_direct)


GLU_TC = 256


def glu_interleave(w_up):
    blocks = []
    for j in range(D_FF // GLU_TC):
        blocks += [w_up[:, j * GLU_TC:(j + 1) * GLU_TC], w_up[:, D_FF + j * GLU_TC:D_FF + (j + 1) * GLU_TC]]
    return jnp.concatenate(blocks, axis=1)


def glu_deinterleave(g):
    nj = D_FF // GLU_TC
    gate = [g[:, 2 * j * GLU_TC:(2 * j + 1) * GLU_TC] for j in range(nj)]
    val = [g[:, (2 * j + 1) * GLU_TC:(2 * j + 2) * GLU_TC] for j in range(nj)]
    return jnp.concatenate(gate + val, axis=1)


def glu_fwd(up, w8, bias):
    nb, S, _ = up.shape
    tc = GLU_TC

    segs, blk = _segments(S, 0), _conv_block(S, 0)
    (_, _, off), = segs

    def body(u_ref, w_ref, b_ref, o_ref, xs):
        _zero_halos(xs, segs)
        xs[off:off + S, :] = u_ref[0, :, :tc]
        w, bias_v = w_ref[...], b_ref[...]

        def block(r0, carry):
            gc = bias_v + _taps(_tap_blocks(_window(xs, off, r0, blk), FFN_K, 1), w)
            o_ref[0, pl.ds(r0, blk), :] = (_gelu(gc) * u_ref[0, pl.ds(r0, blk), tc:]).astype(o_ref.dtype)
            return carry

        _row_blocks(S, blk, block)

    return pl.pallas_call(
        body, name="glu_fwd", grid=(nb, D_FF // tc),
        in_specs=[pl.BlockSpec((1, S, 2 * tc), lambda b, j: (b, 0, j)),
                  pl.BlockSpec((8, tc), lambda b, j: (0, j)), pl.BlockSpec((1, tc), lambda b, j: (0, j))],
        out_specs=pl.BlockSpec((1, S, tc), lambda b, j: (b, 0, j)),
        out_shape=jax.ShapeDtypeStruct((nb, S, D_FF), BF16),
        scratch_shapes=[_halo_scratch(S, 0, tc)],
        compiler_params=_cparams("arbitrary", "arbitrary"),
    )(up, w8, bias)


def glu_bwd(up, w8, bias, dact):
    nb, S, _ = up.shape
    tc = GLU_TC

    segs, blk = _segments(S, 0), _conv_block(S, 0)
    (_, _, off), = segs

    def body(u_ref, w_ref, b_ref, d_ref, du_ref, dw_ref, xs, ds):
        b = pl.program_id(1)
        _zero_halos(xs, segs)
        _zero_halos(ds, segs)
        xs[off:off + S, :] = u_ref[0, :, :tc]
        w, bias_v = w_ref[...], b_ref[...]

        def block(r0, acc):
            here = pl.ds(r0, blk)
            xblocks = _tap_blocks(_window(xs, off, r0, blk), FFN_K, 1)
            act, act_grad = _gelu_and_grad(bias_v + _taps(xblocks, w))
            d = d_ref[0, here, :].astype(F32)
            du_ref[0, here, tc:] = (d * act).astype(du_ref.dtype)
            dpre = d * u_ref[0, here, tc:] * act_grad
            ds[pl.ds(pl.multiple_of(off + r0, 8), blk), :] = dpre
            return acc + _tap_grads(xblocks, dpre)

        rows = _row_blocks(S, blk, block, jnp.zeros((8, tc), F32))

        def block_dx(r0, carry):
            du_ref[0, pl.ds(r0, blk), :tc] = _taps(_tap_blocks(_window(ds, off, r0, blk), FFN_K, -1), w).astype(du_ref.dtype)
            return carry

        _row_blocks(S, blk, block_dx)

        @pl.when(b == 0)
        def _():
            dw_ref[...] = rows

        @pl.when(b > 0)
        def _():
            dw_ref[...] += rows

    pair = pl.BlockSpec((1, S, 2 * tc), lambda j, b: (b, 0, j))
    return pl.pallas_call(
        body, name="glu_bwd", grid=(D_FF // tc, nb),
        in_specs=[pair, pl.BlockSpec((8, tc), lambda j, b: (0, j)), pl.BlockSpec((1, tc), lambda j, b: (0, j)),
                  pl.BlockSpec((1, S, tc), lambda j, b: (b, 0, j))],
        out_specs=[pair, pl.BlockSpec((8, tc), lambda j, b: (0, j))],
        out_shape=[jax.ShapeDtypeStruct((nb, S, 2 * D_FF), BF16), jax.ShapeDtypeStruct((8, D_FF), F32)],
        scratch_shapes=[_halo_scratch(S, 0, tc), _halo_scratch(S, 0, tc)],
        compiler_params=_cparams("arbitrary", "arbitrary"),
    )(up, w8, bias, dact)


def _chunk_of(d, k, n_cc, n_ch):
    rev = jnp.where(k < n_cc, n_cc - 1 - k, n_cc + n_ch - 1 - k)
    return jnp.where(d == 1, rev, k)


def _lane_pick(v, lane_iota, l):
    return jnp.sum(jnp.where(lane_iota == l, v, 0.0), axis=1, keepdims=True)


def head_spread_matrix():
    return (jnp.arange(LANE)[:, None] == (jnp.arange(D_INNER)[None, :] // SSD_P)).astype(BF16)


def _split_dot(x, e, dims):
    hi = x.astype(BF16)
    lo = (x - hi.astype(F32)).astype(BF16)
    return (lax.dot_general(hi, e, dims, preferred_element_type=F32)
            + lax.dot_general(lo, e, dims, preferred_element_type=F32))


def _spread(x, e):
    return _split_dot(x, e, (((1,), (0,)), ((), ())))


def _gather_heads(y, e):
    return _split_dot(y, e, (((1,), (1,)), ((), ())))


def _softplus(x):
    return jnp.maximum(x, 0.0) + jnp.log(1.0 + jnp.exp(-jnp.abs(x)))


def ssd_dt_inputs(u, a_log, dt_bias):
    pad = LANE - SSD_HEADS
    dt = u[..., OFF_DT:OFF_DT + 2 * SSD_HEADS]
    dt2 = jnp.stack([jnp.pad(dt[..., i * SSD_HEADS:(i + 1) * SSD_HEADS], ((0, 0), (0, 0), (0, pad))) for i in range(2)])

    def lanes(v):
        return jnp.pad(v.reshape(2, 1, SSD_HEADS), ((0, 0), (0, 0), (0, pad)))

    return dt2, lanes(a_log), lanes(dt_bias)


def _ssd_common(d, dt_raw, alog, dtb):
    Q = dt_raw.shape[0]
    row = lax.broadcasted_iota(jnp.int32, (Q, Q), 0)
    col = lax.broadcasted_iota(jnp.int32, (Q, Q), 1)
    rev = d == 1
    maskb = jnp.where(rev, row, col) <= jnp.where(rev, col, row)
    tri = maskb.astype(F32)
    A = -jnp.exp(alog)
    dtv = _softplus(dt_raw + dtb)
    a = dtv * A
    cum = lax.dot_general(tri, a, (((1,), (0,)), ((), ())), precision=lax.Precision.HIGHEST, preferred_element_type=F32)
    tot = jnp.sum(a, axis=0, keepdims=True)
    return maskb, tri, A, dtv, cum, tot


def ssd_fwd(xbc, dt2, alog2, dtb2, n_ctx, hosted):
    nb, T, _ = xbc.shape
    S = T - n_ctx
    n_ch, n_cc = T // CHUNK, n_ctx // CHUNK
    Q = CHUNK
    n_pairs = SSD_HEADS // 2
    n_ex = hosted.n
    n_in = 5

    def body(*refs):
        x_ref, dt_ref, al_ref, db_ref, e_ref = refs[:n_in]
        send_refs = refs[n_in:n_in + n_ex]
        y_ref, hin_ref = refs[n_in + n_ex:n_in + 2 + n_ex]
        recv_refs = refs[n_in + 2 + n_ex:n_in + 2 + 2 * n_ex]
        H, *sems = refs[n_in + 2 + 2 * n_ex:]
        d, k = pl.program_id(1), pl.program_id(2)
        first_step = jnp.logical_and(jnp.logical_and(pl.program_id(0) == 0, d == 0), k == 0)
        last_step = jnp.logical_and(jnp.logical_and(pl.program_id(0) == nb - 1, d == 1), k == n_ch - 1)
        begin_exchange, end_exchange = hosted.steps(send_refs, recv_refs, sems, first_step, last_step)
        begin_exchange()

        @pl.when(k == 0)
        def _():
            H[...] = jnp.zeros_like(H)

        maskb, tri, A, dtv, cum, tot = _ssd_common(d, dt_ref[0, 0], al_ref[0], db_ref[0])
        e = e_ref[...]
        cumT = cum.T
        cum_e, dt_e = _spread(cum, e), _spread(dtv, e)
        tot_e = _spread(jnp.broadcast_to(tot, (8, LANE)), e)[0:1]
        hin_ref[0, 0, 0] = H[...].astype(BF16)
        lane = lax.broadcasted_iota(jnp.int32, (Q, LANE), 1)
        lane1 = lax.broadcasted_iota(jnp.int32, (1, LANE), 1)
        subc = lax.broadcasted_iota(jnp.int32, (LANE, 1), 0)
        half = lane < SSD_P
        for g in range(SSD_GROUPS):
            Bg = x_ref[0, :, D_INNER + g * SSD_N:D_INNER + (g + 1) * SSD_N].astype(BF16)
            Cg = x_ref[0, :, D_INNER + GN + g * SSD_N:D_INNER + GN + (g + 1) * SSD_N].astype(BF16)
            Gm = lax.dot_general(Cg, Bg, (((1,), (1,)), ((), ())), preferred_element_type=F32)
            for pr in range(n_pairs // SSD_GROUPS):
                p = g * (n_pairs // SSD_GROUPS) + pr
                sc, dtp, totp = [t[:, p * LANE:(p + 1) * LANE] for t in (cum_e, dt_e, tot_e)]
                swapped = pltpu.roll(sc, SSD_P, 1)
                s0c, s1c = jnp.where(half, sc, swapped), jnp.where(half, swapped, sc)
                s0r, s1r = cumT[2 * p:2 * p + 1, :], cumT[2 * p + 1:2 * p + 2, :]
                tot0, tot1 = _lane_pick(tot, lane1, 2 * p), _lane_pick(tot, lane1, 2 * p + 1)
                M0 = (Gm * jnp.exp(jnp.where(maskb, s0c - s0r, NEG_BIG))).astype(BF16)
                M1 = (Gm * jnp.exp(jnp.where(maskb, s1c - s1r, NEG_BIG))).astype(BF16)
                xd = x_ref[0, :, p * LANE:(p + 1) * LANE] * dtp
                xdb = xd.astype(BF16)
                yd = jnp.where(half,
                               lax.dot_general(M0, xdb, (((1,), (0,)), ((), ())), preferred_element_type=F32),
                               lax.dot_general(M1, xdb, (((1,), (0,)), ((), ())), preferred_element_type=F32))
                Hp = H[p * LANE:(p + 1) * LANE, :]
                yo = lax.dot_general(Cg, Hp.astype(BF16), (((1,), (1,)), ((), ())), preferred_element_type=F32) * jnp.exp(sc)
                y_ref[0, 0, :, p * LANE:(p + 1) * LANE] = yd + yo
                xdw = (xd * jnp.exp(totp - sc)).astype(BF16)
                etot = jnp.exp(jnp.where(subc < SSD_P, tot0, tot1))
                H[p * LANE:(p + 1) * LANE, :] = Hp * etot + lax.dot_general(
                    xdw, Bg, (((0,), (0,)), ((), ())), preferred_element_type=F32)
        end_exchange()

    def ymap(b, d, k):
        return (d, b, _chunk_of(d, jnp.maximum(k, n_cc), n_cc, n_ch) - n_cc, 0)

    return pl.pallas_call(
        body, name="ssd_fwd", grid=(nb, 2, n_ch),
        in_specs=[pl.BlockSpec((1, Q, XBC), lambda b, d, k: (b, _chunk_of(d, k, n_cc, n_ch), 0)),
                  pl.BlockSpec((1, 1, Q, LANE), lambda b, d, k: (d, b, _chunk_of(d, k, n_cc, n_ch), 0)),
                  pl.BlockSpec((1, 1, LANE), lambda b, d, k: (d, 0, 0)), pl.BlockSpec((1, 1, LANE), lambda b, d, k: (d, 0, 0)),
                  pl.BlockSpec((LANE, D_INNER), lambda b, d, k: (0, 0))] + hosted.specs,
        out_specs=[pl.BlockSpec((1, 1, Q, D_INNER), ymap),
                   pl.BlockSpec((1, 1, 1, D_INNER, SSD_N), lambda b, d, k: (d, b, k, 0, 0))] + hosted.specs,
        out_shape=[jax.ShapeDtypeStruct((2, nb, S, D_INNER), F32),
                   jax.ShapeDtypeStruct((2, nb, n_ch, D_INNER, SSD_N), BF16)] + hosted.out_shape,
        scratch_shapes=[pltpu.VMEM((D_INNER, SSD_N), F32)] + hosted.scratch,
        compiler_params=_cparams("arbitrary", "arbitrary", "arbitrary"),
    )(xbc, dt2, alog2, dtb2, head_spread_matrix(), *hosted.arrays)


def ssd_bwd(xbc, dt2, alog2, dtb2, hin, dy, n_ctx, hosted):
    nb, T, _ = xbc.shape
    n_ex = hosted.n
    n_ch, n_cc = T // CHUNK, n_ctx // CHUNK
    n_in = 7
    Q = CHUNK
    n_pairs = SSD_HEADS // 2
    NT = (((1,), (1,)), ((), ()))
    NN = (((1,), (0,)), ((), ()))
    TN = (((0,), (0,)), ((), ()))

    def dot(a, b, dims):
        return lax.dot_general(a.astype(BF16), b.astype(BF16), dims, preferred_element_type=F32)

    def body(*refs):
        x_ref, dt_ref, al_ref, db_ref, e_ref, hin_ref, dy_ref = refs[:n_in]
        send_refs = refs[n_in:n_in + n_ex]
        dx_ref, ddt_ref, st_ref = refs[n_in + n_ex:n_in + 3 + n_ex]
        recv_refs = refs[n_in + 3 + n_ex:n_in + 3 + 2 * n_ex]
        dH, dce, dde, *sems = refs[n_in + 3 + 2 * n_ex:]
        d, kk = pl.program_id(1), pl.program_id(2)
        ks = n_ch - 1 - kk
        first_step = jnp.logical_and(jnp.logical_and(pl.program_id(0) == 0, d == 0), kk == 0)
        last_step = jnp.logical_and(jnp.logical_and(pl.program_id(0) == nb - 1, d == 1), kk == n_ch - 1)
        begin_exchange, end_exchange = hosted.steps(send_refs, recv_refs, sems, first_step, last_step)
        begin_exchange()

        @pl.when(kk == 0)
        def _():
            dH[...] = jnp.zeros_like(dH)

        @pl.when(jnp.logical_and(jnp.logical_and(pl.program_id(0) == 0, d == 0), kk == 0))
        def _():
            st_ref[...] = jnp.zeros_like(st_ref)

        dt_raw = dt_ref[0, 0]
        alog, dtb_v = al_ref[0], db_ref[0]
        maskb, tri, A, dtv, cum, tot = _ssd_common(d, dt_raw, alog, dtb_v)
        e = e_ref[...]
        cumT = cum.T
        cum_e, dt_e = _spread(cum, e), _spread(dtv, e)
        tot_e = _spread(jnp.broadcast_to(tot, (8, LANE)), e)[0:1]
        live = (ks >= n_cc).astype(F32)
        lane = lax.broadcasted_iota(jnp.int32, (Q, LANE), 1)
        lane1 = lax.broadcasted_iota(jnp.int32, (1, LANE), 1)
        sub = lax.broadcasted_iota(jnp.int32, (LANE, Q), 0)
        subc = lax.broadcasted_iota(jnp.int32, (LANE, 1), 0)
        half = lane < SSD_P
        halfc = subc < SSD_P
        pair_ones = ((lax.broadcasted_iota(jnp.int32, (2 * Q, LANE), 0) >= Q).astype(jnp.int32)
                     == (lax.broadcasted_iota(jnp.int32, (2 * Q, LANE), 1) >= SSD_P).astype(jnp.int32)).astype(BF16)
        dcumT = jnp.zeros((LANE, Q), F32)
        dtot = jnp.zeros((1, LANE), F32)
        dtot_parts = []
        for g in range(SSD_GROUPS):
            Bg = x_ref[0, :, D_INNER + g * SSD_N:D_INNER + (g + 1) * SSD_N].astype(BF16)
            Cg = x_ref[0, :, D_INNER + GN + g * SSD_N:D_INNER + GN + (g + 1) * SSD_N].astype(BF16)
            Gm = lax.dot_general(Cg, Bg, NT, preferred_element_type=F32)
            dG = jnp.zeros((Q, Q), F32)
            dC = jnp.zeros((Q, SSD_N), F32)
            dB = jnp.zeros((Q, SSD_N), F32)
            for pr in range(n_pairs // SSD_GROUPS):
                p = g * (n_pairs // SSD_GROUPS) + pr
                l0, l1 = 2 * p, 2 * p + 1
                sc, dtp, totp = [t[:, p * LANE:(p + 1) * LANE] for t in (cum_e, dt_e, tot_e)]
                swapped = pltpu.roll(sc, SSD_P, 1)
                s0c, s1c = jnp.where(half, sc, swapped), jnp.where(half, swapped, sc)
                s0r, s1r = cumT[l0:l0 + 1, :], cumT[l1:l1 + 1, :]
                tot0, tot1 = _lane_pick(tot, lane1, l0), _lane_pick(tot, lane1, l1)
                L0 = jnp.exp(jnp.where(maskb, s0c - s0r, NEG_BIG))
                L1 = jnp.exp(jnp.where(maskb, s1c - s1r, NEG_BIG))
                M0, M1 = Gm * L0, Gm * L1
                xs = x_ref[0, :, p * LANE:(p + 1) * LANE]
                xd = xs * dtp
                es = jnp.exp(sc)
                dte = jnp.exp(totp - sc)
                etot = jnp.exp(jnp.where(halfc, tot0, tot1))
                dyp = dy_ref[0, :, p * LANE:(p + 1) * LANE] * live
                Hp = hin_ref[0, 0, 0, p * LANE:(p + 1) * LANE, :]
                dHp = dH[p * LANE:(p + 1) * LANE, :]
                bdh = dot(Bg, dHp, NT)
                mtdy = dot(jnp.concatenate([M0, M1], axis=1), dyp, TN)
                dxd = jnp.where(half, mtdy[:Q], mtdy[Q:]) + bdh * dte
                dy0 = jnp.where(half, dyp, 0.0)
                dm = dot(jnp.concatenate([dy0, dyp - dy0], axis=0), xd, NT)
                dM0, dM1 = dm[:Q], dm[Q:]
                dG = dG + dM0 * L0 + dM1 * L1
                dyes = dyp * es
                xdw = xd * dte
                dC = dC + dot(dyes, Hp, NN)
                dB = dB + dot(xdw, dHp, NN)
                W0, W1 = dM0 * M0, dM1 * M1
                yoff = dot(Cg, Hp, NT) * es
                r_off = dyp * yoff
                r_st = xd * bdh * dte
                hh = jnp.sum(dHp * Hp.astype(F32), axis=1, keepdims=True) * etot
                w_rows = _split_dot(jnp.concatenate([W0, W1], axis=1), pair_ones, NN) * (1.0 / SSD_P)
                dce[:, p * LANE:(p + 1) * LANE] = r_off - r_st + w_rows
                dde[:, p * LANE:(p + 1) * LANE] = dxd * xs
                dtot_parts.append(jnp.sum(r_st, axis=0, keepdims=True))
                for (l, W, hselc) in ((l0, W0, halfc), (l1, W1, jnp.logical_not(halfc))):
                    row_g = -jnp.sum(W, axis=0, keepdims=True)
                    dcumT = dcumT + jnp.where(sub == l, row_g, 0.0)
                    dtot = dtot + jnp.where(lane1 == l, jnp.sum(jnp.where(hselc, hh, 0.0), axis=0, keepdims=True), 0.0)
                dx_ref[0, 0, :, p * LANE:(p + 1) * LANE] = dxd * dtp
                dH[p * LANE:(p + 1) * LANE, :] = dHp * etot + dot(dyes, Cg, TN)
            dx_ref[0, 0, :, D_INNER + g * SSD_N:D_INNER + (g + 1) * SSD_N] = dB + dot(dG, Cg, TN)
            dx_ref[0, 0, :, D_INNER + GN + g * SSD_N:D_INNER + GN + (g + 1) * SSD_N] = dC + dot(dG, Bg, NN)
        dcum_all = dcumT.T + _gather_heads(dce[...], e)
        dtot_e = jnp.broadcast_to(jnp.concatenate(dtot_parts, axis=1), (8, D_INNER))
        dtot = dtot + _gather_heads(dtot_e, e)[0:1]
        da = lax.dot_general(tri, dcum_all, TN, precision=lax.Precision.HIGHEST, preferred_element_type=F32) + dtot
        ddtv = _gather_heads(dde[...], e) + da * A
        ddt_raw = ddtv * jax.nn.sigmoid(dt_raw + dtb_v)
        ddt_ref[0, 0] = ddt_raw
        sub8 = lax.broadcasted_iota(jnp.int32, (8, LANE), 0)
        st_ref[...] += (jnp.where(sub8 == 2 * d, jnp.sum(da * dtv * A, axis=0, keepdims=True), 0.0)
                        + jnp.where(sub8 == 2 * d + 1, jnp.sum(ddt_raw, axis=0, keepdims=True), 0.0))
        end_exchange()

    def cmap(d, kk):
        return _chunk_of(d, n_ch - 1 - kk, n_cc, n_ch)

    def dymap(b, d, kk):
        return (b, _chunk_of(d, jnp.maximum(n_ch - 1 - kk, n_cc), n_cc, n_ch) - n_cc, 0)

    return pl.pallas_call(
        body, name="ssd_bwd", grid=(nb, 2, n_ch),
        in_specs=[pl.BlockSpec((1, Q, XBC), lambda b, d, kk: (b, cmap(d, kk), 0)),
                  pl.BlockSpec((1, 1, Q, LANE), lambda b, d, kk: (d, b, cmap(d, kk), 0)),
                  pl.BlockSpec((1, 1, LANE), lambda b, d, kk: (d, 0, 0)), pl.BlockSpec((1, 1, LANE), lambda b, d, kk: (d, 0, 0)),
                  pl.BlockSpec((LANE, D_INNER), lambda b, d, kk: (0, 0)),
                  pl.BlockSpec((1, 1, 1, D_INNER, SSD_N), lambda b, d, kk: (d, b, n_ch - 1 - kk, 0, 0)),
                  pl.BlockSpec((1, Q, D_INNER), dymap)] + hosted.specs,
        out_specs=[pl.BlockSpec((1, 1, Q, XBC), lambda b, d, kk: (d, b, cmap(d, kk), 0)),
                   pl.BlockSpec((1, 1, Q, LANE), lambda b, d, kk: (d, b, cmap(d, kk), 0)),
                   pl.BlockSpec((8, LANE), lambda b, d, kk: (0, 0))] + hosted.specs,
        out_shape=[jax.ShapeDtypeStruct((2, nb, T, XBC), F32), jax.ShapeDtypeStruct((2, nb, T, LANE), F32),
                   jax.ShapeDtypeStruct((8, LANE), F32)] + hosted.out_shape,
        scratch_shapes=[pltpu.VMEM((D_INNER, SSD_N), F32), pltpu.VMEM((Q, D_INNER), F32), pltpu.VMEM((Q, D_INNER), F32)] + hosted.scratch,
        compiler_params=_cparams("arbitrary", "arbitrary", "arbitrary"),
    )(xbc, dt2, alog2, dtb2, head_spread_matrix(), hin, dy, *hosted.arrays)


def _adamw(w, g, m, v):
    mn = ADAM_B1 * m + (1.0 - ADAM_B1) * g
    vn = ADAM_B2 * v + (1.0 - ADAM_B2) * jnp.square(g)
    m_hat = mn / (1.0 - ADAM_B1 ** ADAM_STEP)
    v_hat = vn / (1.0 - ADAM_B2 ** ADAM_STEP)
    return -ADAM_LR * (m_hat / (jnp.sqrt(v_hat) + ADAM_EPS) + ADAM_WD * w), mn, vn


def adamw_matrix(name, w, g_slots, m, v):
    K, n = w.shape
    s = g_slots.shape[0]
    tr = _tile(K, 256, 8)

    def body(w_ref, g_ref, m_ref, v_ref, go_ref, d_ref, mo_ref, vo_ref):
        g = g_ref[0].astype(F32)
        for j in range(1, s):
            g = g + g_ref[j].astype(F32)
        go_ref[...] = g
        d_ref[...], mo_ref[...], vo_ref[...] = _adamw(w_ref[...], g, m_ref[...], v_ref[...])

    spec = pl.BlockSpec((tr, n), lambda i: (i, 0))
    return pl.pallas_call(
        body, name=name, grid=(K // tr,),
        in_specs=[spec, pl.BlockSpec((s, tr, n), lambda i: (0, i, 0)), spec, spec], out_specs=[spec] * 4,
        out_shape=[jax.ShapeDtypeStruct((K, n), F32)] * 4,
        compiler_params=_cparams("arbitrary"),
    )(w, g_slots, m, v)


def adamw_small(ws, gs, ms, vs):
    n = len(ws)

    def body(*refs):
        for i in range(n):
            d, mn, vn = _adamw(refs[i][...], refs[n + i][...], refs[2 * n + i][...], refs[3 * n + i][...])
            refs[4 * n + i][...] = d
            refs[5 * n + i][...] = mn
            refs[6 * n + i][...] = vn

    shapes = [jax.ShapeDtypeStruct(w.shape, F32) for w in ws]
    out = pl.pallas_call(body, name="adamw_small", out_shape=shapes * 3)(*ws, *gs, *ms, *vs)
    return out[:n], out[n:2 * n], out[2 * n:]


def sum_slots(name, x):
    n = x.shape[0]

    def fn(t):
        acc = t[0]
        for j in range(1, n):
            acc = acc + t[j]
        return (acc,)

    return ew_call(name, fn, [x], [(x.shape[1:], F32)])[0]


def _pack_rows(parts):
    rows = []
    for p in parts:
        flat = p.reshape(1, -1)
        n = flat.shape[1]
        rows.append(jnp.pad(flat, ((0, 0), (0, -(-n // (8 * LANE)) * 8 * LANE - n))).reshape(-1, LANE))
    return jnp.concatenate(rows, axis=0)


def _unpack_rows(pack, shapes):
    out, r = [], 0
    for s in shapes:
        n = int(np.prod(s))
        nr = -(-n // (8 * LANE)) * 8
        out.append(pack[r:r + nr].reshape(1, -1)[:, :n].reshape(s))
        r += nr
    return out


def _mesh_pos():
    return lax.axis_index("x"), lax.axis_index("y"), lax.axis_index("c")


N_PEERS = N_DEV - 1


def all_gather(name, vs):
    n = len(vs)

    def body(*refs):
        _ag_start(refs[:n], refs[n:2 * n], *refs[2 * n:])
        _ag_finish(refs[:n], refs[n:2 * n], *refs[2 * n:])

    hbm = pl.BlockSpec(memory_space=pl.ANY)
    return pl.pallas_call(
        body, name=name, out_shape=_ag_out_shape(vs), in_specs=[hbm] * n, out_specs=[hbm] * n,
        scratch_shapes=_a2a_scratch(n),
    )(*vs)


def _ag_out_shape(vs):
    return [jax.ShapeDtypeStruct((N_DEV,) + v.shape, v.dtype) for v in vs]


def _ag_copies(x_refs, out_refs, send_sems, recv_sems, local_sems):
    n = len(x_refs)
    x, y, c = _mesh_pos()
    me, sibling = (x, y, c), (x, y, 1 - c)
    chips = [(1 - x, y), (x, 1 - y), (1 - x, 1 - y)]

    def slot(a, px, py, pc):
        return out_refs[a].at[4 * px + 2 * py + pc]

    def copy(a, k, block, to, src=None):
        return pltpu.make_async_remote_copy(
            src_ref=slot(a, *block) if src is None else src, dst_ref=slot(a, *block),
            send_sem=send_sems.at[N_PEERS * a + k], recv_sem=recv_sems.at[N_PEERS * a + k],
            device_id=to, device_id_type=MESH)

    local = [pltpu.make_async_copy(x_refs[a], slot(a, *me), local_sems.at[a]) for a in range(n)]
    first = []
    for a in range(n):
        first.append(copy(a, 0, me, sibling, src=x_refs[a]))
        first += [copy(a, 1 + j, me, (*chip, c), src=x_refs[a]) for j, chip in enumerate(chips)]
    passed = [(copy(a, 1 + j, (*chip, c), me), copy(a, 4 + j, (*chip, c), sibling))
              for j, chip in enumerate(chips) for a in range(n)]
    from_sibling = []
    for a in range(n):
        from_sibling.append(copy(a, 0, sibling, me))
        from_sibling += [copy(a, 4 + j, (*chip, 1 - c), me) for j, chip in enumerate(chips)]
    return local, first, passed, from_sibling


def _ag_start(*refs):
    local, first, _, _ = _ag_copies(*refs)
    for cp in local + first:
        cp.start()


def _ag_finish(*refs):
    local, first, passed, from_sibling = _ag_copies(*refs)
    for arrived, hand_on in passed:
        arrived.wait_recv()
        hand_on.start()
    for cp in from_sibling:
        cp.wait_recv()
    for cp in first + [hand_on for _, hand_on in passed]:
        cp.wait_send()
    for cp in local:
        cp.wait()


def _a2a_scratch(n):
    return [pltpu.SemaphoreType.DMA((N_PEERS * n,)), pltpu.SemaphoreType.DMA((N_PEERS * n,)), pltpu.SemaphoreType.DMA((n,))]


def _a2a_copies(x_refs, out_refs, send_sems, recv_sems, local_sems):
    n = len(x_refs)
    x, y, c = _mesh_pos()
    me = 4 * x + 2 * y + c
    local = [pltpu.make_async_copy(x_refs[a].at[me], out_refs[a].at[me], local_sems.at[a]) for a in range(n)]
    remote = []
    for k in range(1, N_DEV):
        px, py, pc = x ^ ((k >> 2) & 1), y ^ ((k >> 1) & 1), c ^ (k & 1)
        for a in range(n):
            remote.append(pltpu.make_async_remote_copy(
                src_ref=x_refs[a].at[4 * px + 2 * py + pc], dst_ref=out_refs[a].at[me],
                send_sem=send_sems.at[N_PEERS * a + k - 1], recv_sem=recv_sems.at[N_PEERS * a + k - 1],
                device_id=(px, py, pc), device_id_type=MESH))
    return local, remote


def _a2a_start(local, remote):
    for cp in local + remote:
        cp.start()


def _a2a_wait(local, remote):
    for cp in remote:
        cp.wait_recv()
    for cp in remote:
        cp.wait_send()
    for cp in local:
        cp.wait()


class Hosted:
    def __init__(self, start=None, finish=None, arrays=(), out_shape=()):
        self.start, self.finish, self.arrays, self.out_shape = start, finish, list(arrays), list(out_shape)
        self.n = len(self.arrays)
        self.specs = [pl.BlockSpec(memory_space=pl.ANY)] * self.n
        self.scratch = _a2a_scratch(self.n) if self.n else []

    def steps(self, send_refs, recv_refs, sems, first_step, last_step):
        def begin():
            if self.n:
                pl.when(first_step)(lambda: self.start(send_refs, recv_refs, *sems))

        def end():
            if self.n:
                pl.when(last_step)(lambda: self.finish(send_refs, recv_refs, *sems))

        return begin, end


def hosted_all_to_all(vs):
    return Hosted(lambda *r: _a2a_start(*_a2a_copies(*r)), lambda *r: _a2a_wait(*_a2a_copies(*r)), vs,
                  [jax.ShapeDtypeStruct(v.shape, v.dtype) for v in vs])


def hosted_all_gather(vs):
    return Hosted(_ag_start, _ag_finish, vs, _ag_out_shape(vs))


def _taps8(w):
    return jnp.concatenate([w, jnp.zeros((8 - w.shape[0], w.shape[1]), w.dtype)], axis=0)


FIRST = ("w_in",)
LATE_WEIGHTS = ("w_out", "w_up", "w_down", "w_q_up", "w_kv_up")


def first_weights_to_internal(w_in):
    cq, ckv, kr, z, xbc, dt = jnp.split(w_in, np.cumsum(IN_SPLITS)[:-1].tolist(), axis=1)
    K = w_in.shape[0]

    def zeros(n):
        return jnp.zeros((K, n), w_in.dtype)

    w_in_p = jnp.concatenate([cq, zeros(KR_LANE), kr, zeros(LANE - KR_LANE - ROPE), ckv, zeros(OFF_Z - OFF_CKV - KV_RANK),
                              z, xbc, dt, zeros(WIN_P - OFF_DT - 2 * SSD_HEADS)], axis=1)
    return dict(w_in_p=w_in_p)


def late_weights_to_internal(w_out, w_up, w_down, w_q_up, w_kv_up):
    attn_rows = w_out[:N_HEADS * V_DIM].reshape(N_HEADS, V_DIM, -1)
    w_out_p = jnp.concatenate([jnp.pad(attn_rows, ((0, 0), (HEAD_BLOCK - V_DIM, 0), (0, 0))).reshape(QP, -1),
                               w_out[N_HEADS * V_DIM:]], axis=0)
    w_q_p = jnp.pad(w_q_up.reshape(Q_RANK, N_HEADS, NOPE + ROPE), ((0, 0), (0, 0), (0, HEAD_BLOCK - NOPE - ROPE))).reshape(Q_RANK, QP)
    return dict(w_out_p=w_out_p, w_up=glu_interleave(w_up), w_down=w_down, w_q_p=w_q_p, w_kv=w_kv_up)


def _q_grad(g_q_p):
    return g_q_p.reshape(Q_RANK, N_HEADS, HEAD_BLOCK)[:, :, :NOPE + ROPE].reshape(Q_RANK, -1)


def _out_grad(g_out_p):
    return jnp.concatenate([g_out_p[:QP].reshape(N_HEADS, HEAD_BLOCK, -1)[:, HEAD_BLOCK - V_DIM:].reshape(N_HEADS * V_DIM, -1),
                            g_out_p[QP:]], axis=0)


EARLY = ("w_out", "w_up", "w_down", "w_q_up", "w_kv_up")


def local_step(x, ctx, target, mod_x, mod_c, W, late_shards, V):
    nb, S, D = x.shape
    C = ctx.shape[1]
    T = C + S
    tr = _tile(math.gcd(C, S), 256, 8)
    tq = _tile(S, 256, 8)
    tc = 256
    cblk = C // tr
    m = [mod_x[:, i * D:(i + 1) * D][:, None, :] for i in range(N_MOD)]
    mc = [mod_c[:, i * D:(i + 1) * D] for i in range(2)]
    ssd_w8, ffn_w8 = _taps8(V["ssd_conv_w"]), _taps8(V["ffn_conv_w"])
    dexp = jnp.repeat(V["ssd_d"].reshape(-1), SSD_P).reshape(1, D_INNER)
    cosT, sinT = rope_tables(C, S)
    cosS, sinS = cosT[C:], sinT[C:]

    (h1x,) = rows_fwd("prenorm_x", fn_prenorm, nb, S // tr, tr, [(x, D, 0, 0)], [m[0], m[1]], [V["mix_pre_norm"]], [(D, BF16)])
    (h1c,) = rows_fwd("prenorm_c", fn_prenorm, nb, C // tr, tr, [(ctx, D, 0, 0)], [], [mc[0], mc[1], V["mix_pre_norm"]], [(D, BF16)])
    h1 = jnp.concatenate([h1c, h1x], axis=1).reshape(nb * T, D)
    u = matmul("in_proj", [(h1, W["w_in_p"])], "nn", F32).reshape(nb, T, WIN_P)
    xbc = ssd_conv_fwd(u, ssd_w8, V["ssd_conv_b"], C, tc)
    dt2, alog2, dtb2 = ssd_dt_inputs(u, V["ssd_a_log"], V["ssd_dt_bias"])
    y2, hin, *late = ssd_fwd(xbc, dt2, alog2, dtb2, C, hosted_all_gather(late_shards))
    W = dict(W, **late_weights_to_internal(*[_whole(s, n) for s, n in zip(late, LATE_WEIGHTS)]))
    y2 = y2.reshape(2 * nb, S, D_INNER)
    (qn,) = rows_fwd("q_norm", fn_rms, nb, S // tr, tr, [(u, Q_RANK, OFF_CQ // Q_RANK, cblk)], [], [V["q_norm"]], [(Q_RANK, BF16)])
    (kvn,) = rows_fwd("kv_norm", fn_rms, nb, T // tr, tr, [(u, KV_RANK, OFF_CKV // KV_RANK, 0)], [], [V["kv_norm"]], [(KV_RANK, BF16)])
    qn2, kvn2 = qn.reshape(nb * S, Q_RANK), kvn.reshape(nb * T, KV_RANK)
    q_raw = matmul("q_up", [(qn2, W["w_q_p"])], "nn", F32).reshape(nb, S, QP)
    kv = matmul("kv_up", [(kvn2, W["w_kv"])], "nn", BF16).reshape(nb, T, QP)
    cos_q, sin_q = cosS * Q_PRESCALE, sinS * Q_PRESCALE
    kr = rope_call("rope_k", u, LANE, OFF_KR // LANE, cosT, sinT, BF16, tr)
    o = attn_fwd(q_raw, kv, kr, cos_q, sin_q, tq)
    fin_rows = [(y2, D_INNER, 0, 0, 0), (y2, D_INNER, 0, 0, nb), (xbc, D_INNER, 0, cblk), (u, D_INNER, OFF_Z // D_INNER, cblk)]
    fin_gl = [dexp, V["ssd_norm"]]
    (ssd,) = rows_fwd("ssd_finish", fn_ssd_finish, nb, S // tr, tr, fin_rows, [], fin_gl, [(D_INNER, BF16)])
    o2, ssd2 = o.reshape(nb * S, QP), ssd.reshape(nb * S, D_INNER)
    mix = matmul("out_proj", [(o2, W["w_out_p"][:QP]), (ssd2, W["w_out_p"][QP:])], "nn", F32).reshape(nb, S, D)
    pm_rows = [(x, D, 0, 0), (mix, D, 0, 0)]
    pm_pb = [m[2], m[4], m[3]]
    pm_gl = [V["mix_post_norm"], V["ffn_pre_norm"]]
    x1, h2 = rows_fwd("postmix", fn_postmix, nb, S // tr, tr, pm_rows, pm_pb, pm_gl, [(D, F32), (D, BF16)])
    h22 = h2.reshape(nb * S, D)
    up = matmul("up_proj", [(h22, W["w_up"])], "nn", F32).reshape(nb, S, 2 * D_FF)
    act = glu_fwd(up, ffn_w8, V["ffn_conv_b"])
    act2 = act.reshape(nb * S, D_FF)
    ffn = matmul("down_proj", [(act2, W["w_down"])], "nn", F32).reshape(nb, S, D)
    dx1, dffn, dgate2, d_ffn_post, loss = final_call(x1, ffn, target, m[5], V["ffn_post_norm"], tr)

    dffn2 = dffn.reshape(nb * S, D)
    dact = matmul("down_dgrad", [(dffn2, W["w_down"])], "nt", BF16).reshape(nb, S, D_FF)
    g_down = matmul_tn("down_wgrad", act2, dffn2)
    dup, ffn_rows = glu_bwd(up, ffn_w8, V["ffn_conv_b"], dact)
    dup2 = dup.reshape(nb * S, 2 * D_FF)
    dh2 = matmul("up_dgrad", [(dup2, W["w_up"])], "nt", BF16).reshape(nb, S, D)
    g_up = matmul_tn("up_wgrad", h22, dup2)
    dx_a, dmix, dgate1, dscale2, dshift2, d_mix_post, d_ffn_pre = rows_bwd(
        "postmix_bwd", fn_postmix, nb, S // tr, tr, pm_rows, pm_pb, pm_gl,
        [(dx1, D, 0, 0), (dh2, D, 0, 0)], [(0, F32), (1, BF16)])
    dmix2 = dmix.reshape(nb * S, D)
    dcat = matmul("out_dgrad", [(dmix2, W["w_out_p"])], "nt", BF16).reshape(nb, S, QP + D_INNER)
    g_out_p = jnp.concatenate([matmul_tn("out_wgrad_attn", o2, dmix2), matmul_tn("out_wgrad_ssd", ssd2, dmix2)], axis=0)
    dy, dxs_direct, dz, d_dexp, d_ssd_norm = rows_bwd(
        "ssd_finish_bwd", fn_ssd_finish, nb, S // tr, tr, fin_rows, [], fin_gl,
        [(dcat, D_INNER, QP // D_INNER, 0)], [(0, F32), (2, F32), (3, BF16)])
    dq_pre, dkv, dkr = attn_bwd(q_raw, kv, kr, dcat, cos_q, sin_q, cosS, sinS, tq)
    dq_pre = dq_pre.reshape(nb * S, QP)
    dkr_pre = rope_call("rope_dk", dkr, LANE, 0, cosT, -sinT, BF16, tr)
    dkv2 = dkv.reshape(nb * T, QP)
    dqn = matmul("q_dgrad", [(dq_pre, W["w_q_p"])], "nt", F32).reshape(nb, S, Q_RANK)
    g_q_p = matmul_tn("q_wgrad", qn2, dq_pre)
    dkvn = matmul("kv_dgrad", [(dkv2, W["w_kv"])], "nt", F32).reshape(nb, T, KV_RANK)
    g_kv = matmul_tn("kv_wgrad", kvn2, dkv2)
    early_grads = (_out_grad(g_out_p), glu_deinterleave(g_up), g_down, _q_grad(g_q_p), g_kv)
    early = hosted_all_to_all([_per_device(g, n) for g, n in zip(early_grads, EARLY)])
    dxbc2, ddt2, ssd_stats, *received = ssd_bwd(xbc, dt2, alog2, dtb2, hin, dy, C, early)
    ddt_block = jnp.concatenate([ddt2[0][..., :SSD_HEADS], ddt2[1][..., :SSD_HEADS],
                                 jnp.zeros((nb, T, LANE - 2 * SSD_HEADS), F32)], axis=-1).astype(BF16)
    dxbc_raw, ssd_rows = ssd_conv_bwd(u, ssd_w8, V["ssd_conv_b"], dxbc2, dxs_direct, C, tc)
    dcq, d_q_norm = rows_bwd("q_norm_bwd", fn_rms, nb, S // tr, tr, [(u, Q_RANK, OFF_CQ // Q_RANK, cblk)], [], [V["q_norm"]],
                             [(dqn, Q_RANK, 0, 0)], [(0, BF16)])
    dckv, d_kv_norm = rows_bwd("kv_norm_bwd", fn_rms, nb, T // tr, tr, [(u, KV_RANK, OFF_CKV // KV_RANK, 0)], [], [V["kv_norm"]],
                               [(dkvn, KV_RANK, 0, 0)], [(0, BF16)])

    def ctx_rows(t):
        return jnp.pad(t, ((0, 0), (C, 0), (0, 0)))

    du = [("cq", ctx_rows(dcq), OFF_CQ, Q_RANK), ("kr", dkr_pre, OFF_KR, LANE), ("ckv", dckv, OFF_CKV, KV_RANK),
          ("z", ctx_rows(dz), OFF_Z, D_INNER), ("xbc", dxbc_raw, OFF_XBC, XBC), ("dt", ddt_block, OFF_DT, LANE)]
    du = [(name, t.reshape(nb * T, w), off, w) for (name, t, off, w) in du]
    g = {name: matmul_tn("in_wgrad_" + name, h1, t) for (name, t, _, _) in du}
    g_in = jnp.concatenate([g["cq"], g["ckv"], g["kr"][:, KR_LANE:KR_LANE + ROPE], g["z"], g["xbc"],
                            g["dt"][:, :2 * SSD_HEADS]], axis=1)
    dh1, received_in = matmul("in_dgrad", [(t, W["w_in_p"][:, off:off + w]) for (_, t, off, w) in du], "nt", BF16,
                              hosted=hosted_all_to_all([_per_device(g_in, "w_in").astype(BF16)]))
    dh1 = dh1.reshape(nb, T, D)

    def fn_prenorm_res(xv, shift, scale, g):
        return fn_prenorm(xv, shift, scale, g) + (xv,)

    grad_x, dshift1, dscale1, d_mix_pre_x = rows_bwd(
        "prenorm_x_bwd", fn_prenorm_res, nb, S // tr, tr, [(x, D, 0, 0)], [m[0], m[1]], [V["mix_pre_norm"]],
        [(dh1, D, 0, cblk), (dx_a, D, 0, 0)], [(0, F32)])
    dshift_c, dscale_c, d_mix_pre_c = rows_bwd(
        "prenorm_c_bwd", fn_prenorm, nb, C // tr, tr, [(ctx, D, 0, 0)], [], [mc[0], mc[1], V["mix_pre_norm"]],
        [(dh1, D, 0, 0)], [])

    dmod_x = jnp.concatenate([dshift1, dscale1, dgate1, dshift2, dscale2, dgate2], axis=-1).reshape(nb, N_MOD * D)
    dmod_c = jnp.concatenate([dshift_c, dscale_c, jnp.zeros((1, (N_MOD - 2) * D), F32)], axis=-1)
    gv = dict(
        mix_pre_norm=d_mix_pre_x + d_mix_pre_c, mix_post_norm=d_mix_post, q_norm=d_q_norm, kv_norm=d_kv_norm,
        ssd_conv_w=ssd_rows[:SSD_K], ssd_conv_b=ssd_rows[SSD_K:SSD_K + 1],
        ssd_a_log=jnp.concatenate([ssd_stats[0:1, :SSD_HEADS], ssd_stats[2:3, :SSD_HEADS]], axis=1),
        ssd_dt_bias=jnp.concatenate([ssd_stats[1:2, :SSD_HEADS], ssd_stats[3:4, :SSD_HEADS]], axis=1),
        ssd_d=jnp.sum(d_dexp.reshape(SSD_HEADS, SSD_P), axis=1).reshape(1, SSD_HEADS), ssd_norm=d_ssd_norm,
        ffn_pre_norm=d_ffn_pre, ffn_post_norm=d_ffn_post,
        ffn_conv_w=ffn_rows[:FFN_K], ffn_conv_b=ffn_rows[FFN_K:FFN_K + 1])
    return loss, grad_x, dmod_x, dmod_c, gv, dict(zip(EARLY, received), w_in=received_in)


WEIGHT_ORDER = ("c_ctx", "w_mod", "b_mod", "mix_pre_norm", "mix_post_norm", "w_in", "q_norm", "w_q_up", "kv_norm",
                "w_kv_up", "ssd_conv_w", "ssd_conv_b", "ssd_a_log", "ssd_dt_bias", "ssd_d", "ssd_norm", "w_out",
                "ffn_pre_norm", "ffn_post_norm", "w_up", "ffn_conv_w", "ffn_conv_b", "w_down")
MATRICES = ("w_in", "w_q_up", "w_kv_up", "w_out", "w_up", "w_down")
ROW_SHARDED = ("w_out", "w_down")
SMALL_SUMMED = ("c_ctx", "mix_pre_norm", "mix_post_norm", "q_norm", "kv_norm", "ssd_conv_w", "ssd_conv_b", "ssd_a_log",
                "ssd_dt_bias", "ssd_d", "ssd_norm", "ffn_pre_norm", "ffn_post_norm", "ffn_conv_w", "ffn_conv_b")
MOD_ROWS = 8


def _whole(shards, name):
    if name in ROW_SHARDED:
        return shards.reshape(-1, shards.shape[-1])
    return jnp.concatenate([shards[j] for j in range(N_DEV)], axis=1)


def _per_device(g, name):
    if name in ROW_SHARDED:
        return g.reshape(N_DEV, -1, g.shape[-1])
    return jnp.stack(jnp.split(g, N_DEV, axis=1))


def kernel(x, c, ctx, c_ctx, w_mod, b_mod, mix_pre_norm, mix_post_norm, w_in, q_norm, w_q_up, kv_norm, w_kv_up, ssd_conv_w, ssd_conv_b, ssd_a_log, ssd_dt_bias, ssd_d, ssd_norm, w_out, ffn_pre_norm, ffn_post_norm, w_up, ffn_conv_w, ffn_conv_b, w_down, loss_target, m_c_ctx, m_w_mod, m_b_mod, m_mix_pre_norm, m_mix_post_norm, m_w_in, m_q_norm, m_w_q_up, m_kv_norm, m_w_kv_up, m_ssd_conv_w, m_ssd_conv_b, m_ssd_a_log, m_ssd_dt_bias, m_ssd_d, m_ssd_norm, m_w_out, m_ffn_pre_norm, m_ffn_post_norm, m_w_up, m_ffn_conv_w, m_ffn_conv_b, m_w_down, v_c_ctx, v_w_mod, v_b_mod, v_mix_pre_norm, v_mix_post_norm, v_w_in, v_q_norm, v_w_q_up, v_kv_norm, v_w_kv_up, v_ssd_conv_w, v_ssd_conv_b, v_ssd_a_log, v_ssd_dt_bias, v_ssd_d, v_ssd_norm, v_w_out, v_ffn_pre_norm, v_ffn_post_norm, v_w_up, v_ffn_conv_w, v_ffn_conv_b, v_w_down):
    weights = dict(c_ctx=c_ctx, w_mod=w_mod, b_mod=b_mod, mix_pre_norm=mix_pre_norm, mix_post_norm=mix_post_norm, w_in=w_in, q_norm=q_norm, w_q_up=w_q_up, kv_norm=kv_norm, w_kv_up=w_kv_up, ssd_conv_w=ssd_conv_w, ssd_conv_b=ssd_conv_b, ssd_a_log=ssd_a_log, ssd_dt_bias=ssd_dt_bias, ssd_d=ssd_d, ssd_norm=ssd_norm, w_out=w_out, ffn_pre_norm=ffn_pre_norm, ffn_post_norm=ffn_post_norm, w_up=w_up, ffn_conv_w=ffn_conv_w, ffn_conv_b=ffn_conv_b, w_down=w_down)
    mom1 = dict(c_ctx=m_c_ctx, w_mod=m_w_mod, b_mod=m_b_mod, mix_pre_norm=m_mix_pre_norm, mix_post_norm=m_mix_post_norm, w_in=m_w_in, q_norm=m_q_norm, w_q_up=m_w_q_up, kv_norm=m_kv_norm, w_kv_up=m_w_kv_up, ssd_conv_w=m_ssd_conv_w, ssd_conv_b=m_ssd_conv_b, ssd_a_log=m_ssd_a_log, ssd_dt_bias=m_ssd_dt_bias, ssd_d=m_ssd_d, ssd_norm=m_ssd_norm, w_out=m_w_out, ffn_pre_norm=m_ffn_pre_norm, ffn_post_norm=m_ffn_post_norm, w_up=m_w_up, ffn_conv_w=m_ffn_conv_w, ffn_conv_b=m_ffn_conv_b, w_down=m_w_down)
    mom2 = dict(c_ctx=v_c_ctx, w_mod=v_w_mod, b_mod=v_b_mod, mix_pre_norm=v_mix_pre_norm, mix_post_norm=v_mix_post_norm, w_in=v_w_in, q_norm=v_q_norm, w_q_up=v_w_q_up, kv_norm=v_kv_norm, w_kv_up=v_w_kv_up, ssd_conv_w=v_ssd_conv_w, ssd_conv_b=v_ssd_conv_b, ssd_a_log=v_ssd_a_log, ssd_dt_bias=v_ssd_dt_bias, ssd_d=v_ssd_d, ssd_norm=v_ssd_norm, w_out=v_w_out, ffn_pre_norm=v_ffn_pre_norm, ffn_post_norm=v_ffn_post_norm, w_up=v_w_up, ffn_conv_w=v_ffn_conv_w, ffn_conv_b=v_ffn_conv_b, w_down=v_w_down)
    nb, S, D = x.shape
    me = 4 * lax.axis_index("x") + 2 * lax.axis_index("y") + lax.axis_index("c")

    *first, c_all, ssd_w_sh, ffn_w_sh = all_gather(
        "gather_first", [weights[n][0].astype(BF16) for n in FIRST] + [c, ssd_conv_w[0], ffn_conv_w[0]])
    W = first_weights_to_internal(*[_whole(s, n) for n, s in zip(FIRST, first)])
    late_shards = [weights[n][0].astype(BF16) for n in LATE_WEIGHTS]
    V = {n: weights[n].reshape(1, -1) for n in SMALL_SUMMED if n != "c_ctx"}
    V["ssd_conv_w"] = _whole(ssd_w_sh, "ssd_conv_w")
    V["ffn_conv_w"] = _whole(ffn_w_sh, "ffn_conv_w")

    n_all = N_DEV * nb
    mod_rows = -(-(n_all + 1) // 8) * 8
    c_pad = jnp.concatenate([c_all.reshape(n_all, D), c_ctx.reshape(1, D), jnp.zeros((mod_rows - n_all - 1, D), F32)], axis=0)
    mod_cols = w_mod.shape[2]
    b_mine = lax.dynamic_slice(b_mod, (0, me * mod_cols), (1, mod_cols))
    mod_part = matmul("mod_proj", [(c_pad, w_mod[0])], "nn", F32, bias=b_mine, silu_a=True)
    mod_all = _whole(all_gather("gather_mod", [mod_part])[0], "w_mod")
    mod_x = lax.dynamic_slice(mod_all, (me * nb, 0), (nb, mod_all.shape[1]))
    mod_c = mod_all[n_all:n_all + 1]

    loss, grad_x, dmod_x, dmod_c, gv, slots = local_step(x, ctx, loss_target, mod_x, mod_c, W, late_shards, V)

    dmod_mine = jnp.concatenate([dmod_x, dmod_c, jnp.zeros((MOD_ROWS - nb - 1, dmod_x.shape[1]), F32)], axis=0)
    dmod_all = all_gather("gather_dmod", [dmod_mine])[0]
    dmod_ctx = sum_slots("sum_dmod_ctx", dmod_all[:, nb:nb + 1].reshape(N_DEV, -1, LANE)).reshape(1, -1)
    dmod_full = jnp.concatenate([dmod_all[:, :nb].reshape(n_all, -1), dmod_ctx,
                                 jnp.zeros((mod_rows - n_all - 1, dmod_ctx.shape[1]), F32)], axis=0)
    (g_b_mod,) = ew_call("mod_bias_grad", lambda t: (jnp.sum(t, axis=0, keepdims=True),), [dmod_full], [((1, dmod_full.shape[1]), F32)])
    dmod_cols = lax.dynamic_slice(dmod_full, (0, me * mod_cols), (mod_rows, mod_cols))
    g_w_mod = matmul_tn("mod_wgrad", c_pad, dmod_cols, silu_a=True)
    dsilu_ctx = matmul("mod_dgrad_ctx", [(dmod_cols[n_all:n_all + 8], w_mod[0])], "nt", F32)[0:1]

    def silu_vjp(cc, ct):
        return (jax.vjp(_silu, cc)[1](ct)[0],)

    (g_c_ctx_part,) = ew_call("c_ctx_grad", silu_vjp, [c_ctx.reshape(1, D), dsilu_ctx], [((1, D), F32)])

    gv = dict(gv, c_ctx=g_c_ctx_part)
    small_parts = [loss] + [gv[n] for n in SMALL_SUMMED]
    small_sum = sum_slots("sum_small", all_gather("gather_small_grads", [_pack_rows(small_parts)])[0])
    summed = _unpack_rows(small_sum, [p.shape for p in small_parts])
    loss_out = summed[0][0, 0]
    grads = {n: g.reshape(weights[n].shape) if n not in ("ssd_conv_w", "ffn_conv_w") else g for n, g in zip(SMALL_SUMMED, summed[1:])}
    for n in ("ssd_conv_w", "ffn_conv_w"):
        cols = weights[n].shape[2]
        grads[n] = lax.dynamic_slice(grads[n], (0, me * cols), (grads[n].shape[0], cols)).reshape(weights[n].shape)
    grads["b_mod"] = g_b_mod.reshape(b_mod.shape)

    slots = dict(slots, w_mod=g_w_mod[None])
    delta, new_m, new_v = {}, {}, {}
    for n in MATRICES + ("w_mod",):
        g, d, mn, vn = adamw_matrix("adamw_" + n, weights[n][0], slots[n], mom1[n][0], mom2[n][0])
        grads[n], delta[n], new_m[n], new_v[n] = [t.reshape(weights[n].shape) for t in (g, d, mn, vn)]
    small = [n for n in WEIGHT_ORDER if n not in slots]

    def two_d(t):
        return t.reshape(-1, t.shape[-1])

    ds, ms, vs = adamw_small(*[[two_d(t[n]) for n in small] for t in (weights, grads, mom1, mom2)])
    for n, d, mn, vn in zip(small, ds, ms, vs):
        delta[n], new_m[n], new_v[n] = [t.reshape(weights[n].shape) for t in (d, mn, vn)]
    return (loss_out, grad_x, *[t[n] for t in (grads, delta, new_m, new_v) for n in WEIGHT_ORDER])
```

```python
import math

import jax
import jax.numpy as jnp
import numpy as np
from jax import lax
from jax.experimental import pallas as pl
from jax.experimental.pallas import tpu as pltpu

F32 = jnp.float32
BF16 = jnp.bfloat16
MESH = pl.DeviceIdType.MESH

D_MODEL = 1024
GRID_W = 64
N_HEADS = 16
NOPE = 64
ROPE = 32
V_DIM = 64
Q_RANK = 384
KV_RANK = 256
ROPE_THETA = 10000.0
ATTN_SCALE = (NOPE + ROPE) ** -0.5
SSD_HEADS = 16
SSD_P = 64
SSD_GROUPS = 2
SSD_N = 128
SSD_K = 5
CHUNK = 128
D_INNER = SSD_HEADS * SSD_P
GN = SSD_GROUPS * SSD_N
XBC = D_INNER + 2 * GN
D_FF = 2816
FFN_K = 3
N_MOD = 6
EPS = 1e-6
IN_SPLITS = (Q_RANK, KV_RANK, ROPE, D_INNER, XBC, 2 * SSD_HEADS)
IN_WIDTH = sum(IN_SPLITS)
N_DEV = 8

ADAM_LR = 0.001
ADAM_B1 = 0.9
ADAM_B2 = 0.999
ADAM_EPS = 1e-08
ADAM_WD = 0.01
ADAM_STEP = 10

LANE = 128
HEAD_BLOCK = 128
OFF_CQ = 0
OFF_KR = 384
OFF_CKV = 512
OFF_Z = 1024
OFF_XBC = 2048
OFF_DT = 3584
WIN_P = 3840
KR_LANE = 64
QP = N_HEADS * HEAD_BLOCK

VMEM_LIMIT_V7X = 56 * 1024 * 1024
NEG_BIG = -1e30


def _cparams(*sem):
    return pltpu.CompilerParams(dimension_semantics=sem, vmem_limit_bytes=VMEM_LIMIT_V7X)


def _tile(n, target, mult=128):
    if n <= target:
        return n
    t = (target // mult) * mult
    while t >= mult:
        if n % t == 0:
            return t
        t -= mult
    return n


def _silu(x):
    return x * jax.nn.sigmoid(x)


def _rms(x, g):
    return x * lax.rsqrt(jnp.mean(x * x, axis=-1, keepdims=True) + EPS) * g


WHOLE_K_WIDE = 2048


def matmul(name, pairs, mode, out_dtype, *, bias=None, silu_a=False, hosted=None):
    n_pairs = len(pairs)
    M = pairs[0][0].shape[0]
    N = pairs[0][1].shape[1] if mode == "nn" else pairs[0][1].shape[0]
    k_total = sum(a.shape[1] for a, _ in pairs)
    tm = _tile(M, 1024 if k_total <= WHOLE_K_WIDE else 512, 8)
    tn = _tile(N, 2816 if k_total <= WHOLE_K_WIDE else 1024)
    dims = (((1,), (0,)), ((), ())) if mode == "nn" else (((1,), (1,)), ((), ()))
    n_own = 2 * n_pairs + (bias is not None)
    n_ex = hosted.n if hosted else 0

    def body(*refs):
        o_ref = refs[n_own + n_ex]
        if hosted:
            j, i = pl.program_id(0), pl.program_id(1)
            begin_exchange, end_exchange = hosted.steps(
                refs[n_own:n_own + n_ex], refs[n_own + n_ex + 1:n_own + 2 * n_ex + 1], refs[n_own + 2 * n_ex + 1:],
                jnp.logical_and(j == 0, i == 0), jnp.logical_and(j == N // tn - 1, i == M // tm - 1))
            begin_exchange()
        acc = None
        for p in range(n_pairs):
            a = refs[2 * p][...]
            if silu_a:
                a = _silu(a.astype(F32))
            d = lax.dot_general(a.astype(BF16), refs[2 * p + 1][...].astype(BF16), dims, preferred_element_type=F32)
            acc = d if acc is None else acc + d
        if bias is not None:
            acc = acc + refs[2 * n_pairs][...]
        o_ref[...] = acc.astype(o_ref.dtype)
        if hosted:
            end_exchange()

    in_specs, args = [], []
    for a, b in pairs:
        K = a.shape[1]
        in_specs.append(pl.BlockSpec((tm, K), lambda j, i: (i, 0)))
        in_specs.append(pl.BlockSpec((K, tn), lambda j, i: (0, j)) if mode == "nn" else pl.BlockSpec((tn, K), lambda j, i: (j, 0)))
        args += [a, b]
    if bias is not None:
        in_specs.append(pl.BlockSpec((1, tn), lambda j, i: (0, j)))
        args.append(bias)
    out_spec = pl.BlockSpec((tm, tn), lambda j, i: (i, j))
    out_shape = jax.ShapeDtypeStruct((M, N), out_dtype)
    if not hosted:
        return pl.pallas_call(
            body, name=name, grid=(N // tn, M // tm), in_specs=in_specs, out_specs=out_spec, out_shape=out_shape,
            compiler_params=_cparams("arbitrary", "arbitrary"),
        )(*args)
    return pl.pallas_call(
        body, name=name, grid=(N // tn, M // tm), in_specs=in_specs + hosted.specs,
        out_specs=[out_spec] + hosted.specs, out_shape=[out_shape] + hosted.out_shape, scratch_shapes=hosted.scratch,
        compiler_params=_cparams("arbitrary", "arbitrary"),
    )(*args, *hosted.arrays)


def matmul_tn(name, a, b, out_dtype=F32, *, silu_a=False, tm=1408, tn=2048, tk=2048):
    R, M = a.shape
    N = b.shape[1]
    tm = _tile(M, tm)
    tn = _tile(N, tn)
    tk = _tile(R, tk, 8)
    nk = R // tk

    def body(a_ref, b_ref, o_ref, acc):
        k = pl.program_id(2)

        @pl.when(k == 0)
        def _():
            acc[...] = jnp.zeros_like(acc)

        x = a_ref[...]
        if silu_a:
            x = _silu(x.astype(F32))
        acc[...] += lax.dot_general(x.astype(BF16), b_ref[...].astype(BF16), (((0,), (0,)), ((), ())),
                                    preferred_element_type=F32)

        @pl.when(k == nk - 1)
        def _():
            o_ref[...] = acc[...].astype(o_ref.dtype)

    return pl.pallas_call(
        body, name=name, grid=(M // tm, N // tn, nk),
        in_specs=[pl.BlockSpec((tk, tm), lambda i, j, k: (k, i)), pl.BlockSpec((tk, tn), lambda i, j, k: (k, j))],
        out_specs=pl.BlockSpec((tm, tn), lambda i, j, k: (i, j)),
        out_shape=jax.ShapeDtypeStruct((M, N), out_dtype),
        scratch_shapes=[pltpu.VMEM((tm, tn), F32)],
        compiler_params=_cparams("arbitrary", "arbitrary", "arbitrary"),
    )(a, b)


ROW_SUB = 32


def _row_specs(rin, pbin, glin, tr):
    specs = [pl.BlockSpec((1, tr, w), lambda b, i, cb=cb, ro=ro, bo=(e[4] if len(e) > 4 else 0): (b + bo, i + ro, cb))
             for e in rin for (_, w, cb, ro) in [e[:4]]]
    specs += [pl.BlockSpec((1, 1, a.shape[-1]), lambda b, i: (b, 0, 0)) for a in pbin]
    specs += [pl.BlockSpec((1, a.shape[-1]), lambda b, i: (0, 0)) for a in glin]
    return specs


def rows_fwd(name, fn, nb, nblk, tr, rin, pbin, glin, outs):
    nr, npb, ngl = len(rin), len(pbin), len(glin)
    n_in = nr + npb + ngl

    def body(*refs):
        shared = [r[0].astype(F32) for r in refs[nr:nr + npb]] + [r[...] for r in refs[nr + npb:n_in]]

        def sub_block(s, carry):
            rows = pl.ds(pl.multiple_of(s * ROW_SUB, ROW_SUB), ROW_SUB)
            res = fn(*[r[0, rows, :].astype(F32) for r in refs[:nr]], *shared)
            for o, v in zip(refs[n_in:], res):
                o[0, rows, :] = v.astype(o.dtype)
            return carry

        lax.fori_loop(0, tr // ROW_SUB, sub_block, 0)

    return pl.pallas_call(
        body, name=name, grid=(nb, nblk), in_specs=_row_specs(rin, pbin, glin, tr),
        out_specs=[pl.BlockSpec((1, tr, w), lambda b, i: (b, i, 0)) for (w, _) in outs],
        out_shape=[jax.ShapeDtypeStruct((nb, nblk * tr, w), dt) for (w, dt) in outs],
        compiler_params=_cparams("arbitrary", "arbitrary"),
    )(*[e[0] for e in rin], *pbin, *glin)


def rows_bwd(name, fn, nb, nblk, tr, rin, pbin, glin, cts, want):
    nr, npb, ngl, nct = len(rin), len(pbin), len(glin), len(cts)
    n_in = nr + npb + ngl

    def body(*refs):
        b, i = pl.program_id(0), pl.program_id(1)
        shared = [r[0].astype(F32) for r in refs[nr:nr + npb]] + [r[...] for r in refs[nr + npb:n_in]]
        orefs = refs[n_in + nct:]

        def sub_block(s, acc):
            rows = pl.ds(pl.multiple_of(s * ROW_SUB, ROW_SUB), ROW_SUB)
            _, vjp = jax.vjp(fn, *[r[0, rows, :].astype(F32) for r in refs[:nr]], *shared)
            gs = vjp(tuple(r[0, rows, :].astype(F32) for r in refs[n_in:n_in + nct]))
            for o, (idx, _) in zip(orefs, want):
                o[0, rows, :] = gs[idx].astype(o.dtype)
            return tuple(a + v for a, v in zip(acc, gs[nr:]))

        zeros = tuple(jnp.zeros((1, a.shape[-1]), F32) for a in list(pbin) + list(glin))
        g = (None,) * nr + tuple(lax.fori_loop(0, tr // ROW_SUB, sub_block, zeros))
        pb_refs = orefs[len(want):len(want) + npb]
        gl_refs = orefs[len(want) + npb:]

        @pl.when(i == 0)
        def _():
            for o, v in zip(pb_refs, g[nr:nr + npb]):
                o[0] = v

        @pl.when(i > 0)
        def _():
            for o, v in zip(pb_refs, g[nr:nr + npb]):
                o[0] += v

        first = jnp.logical_and(b == 0, i == 0)

        @pl.when(first)
        def _():
            for o, v in zip(gl_refs, g[nr + npb:]):
                o[...] = v

        @pl.when(jnp.logical_not(first))
        def _():
            for o, v in zip(gl_refs, g[nr + npb:]):
                o[...] += v

    out_specs = [pl.BlockSpec((1, tr, rin[idx][1]), lambda b, i: (b, i, 0)) for (idx, _) in want]
    out_shape = [jax.ShapeDtypeStruct((nb, nblk * tr, rin[idx][1]), dt) for (idx, dt) in want]
    out_specs += [pl.BlockSpec((1, 1, a.shape[-1]), lambda b, i: (b, 0, 0)) for a in pbin]
    out_shape += [jax.ShapeDtypeStruct((nb, 1, a.shape[-1]), F32) for a in pbin]
    out_specs += [pl.BlockSpec((1, a.shape[-1]), lambda b, i: (0, 0)) for a in glin]
    out_shape += [jax.ShapeDtypeStruct((1, a.shape[-1]), F32) for a in glin]
    return pl.pallas_call(
        body, name=name, grid=(nb, nblk),
        in_specs=_row_specs(rin, pbin, glin, tr) + _row_specs(cts, [], [], tr),
        out_specs=out_specs, out_shape=out_shape,
        compiler_params=_cparams("arbitrary", "arbitrary"),
    )(*[e[0] for e in rin], *pbin, *glin, *[e[0] for e in cts])


def ew_call(name, fn, ins, outs):
    def body(*refs):
        res = fn(*[r[...] for r in refs[:len(ins)]])
        for o, v in zip(refs[len(ins):], res):
            o[...] = v.astype(o.dtype)

    return pl.pallas_call(body, name=name, out_shape=[jax.ShapeDtypeStruct(s, dt) for (s, dt) in outs])(*ins)


def fn_prenorm(x, shift, scale, g):
    return (_rms(x, g) * (1.0 + scale) + shift,)


def fn_rms(x, g):
    return (_rms(x, g),)


def fn_ssd_finish(yf, yr, xs, z, dexp, nw):
    y = yf + yr + dexp * xs
    return (_rms(y * _silu(z), nw),)


def fn_postmix(x, mix, gate1, scale2, shift2, post_g, pre_g):
    x1 = x + gate1 * _rms(mix, post_g)
    h2 = _rms(x1, pre_g) * (1.0 + scale2) + shift2
    return x1, h2


def final_call(x1, ffn, target, gate2, post_g, tr):
    nb, S, D = x1.shape
    nblk = S // tr

    def body(x1_ref, f_ref, t_ref, g2_ref, pg_ref, dx1_ref, df_ref, dg2_ref, dpg_ref, loss_ref):
        b, i = pl.program_id(0), pl.program_id(1)
        tgt = t_ref[0]

        def lossfn(x1v, fv, g2, pg):
            e = x1v + g2 * _rms(fv, pg) - tgt
            return 0.5 * jnp.sum(jnp.mean(e * e, axis=-1, keepdims=True))

        val, (dx1, df, dg2, dpg) = jax.value_and_grad(lossfn, argnums=(0, 1, 2, 3))(
            x1_ref[0], f_ref[0].astype(F32), g2_ref[0], pg_ref[...])
        dx1_ref[0] = dx1
        df_ref[0] = df.astype(df_ref.dtype)
        lv = jnp.full((1, LANE), val, F32)

        @pl.when(i == 0)
        def _():
            dg2_ref[0] = dg2

        @pl.when(i > 0)
        def _():
            dg2_ref[0] += dg2

        first = jnp.logical_and(b == 0, i == 0)

        @pl.when(first)
        def _():
            dpg_ref[...] = dpg
            loss_ref[...] = lv

        @pl.when(jnp.logical_not(first))
        def _():
            dpg_ref[...] += dpg
            loss_ref[...] += lv

    row = pl.BlockSpec((1, tr, D), lambda b, i: (b, i, 0))
    pb = pl.BlockSpec((1, 1, D), lambda b, i: (b, 0, 0))
    gl = pl.BlockSpec((1, D), lambda b, i: (0, 0))
    return pl.pallas_call(
        body, name="loss_head", grid=(nb, nblk), in_specs=[row, row, row, pb, gl],
        out_specs=[row, row, pb, gl, pl.BlockSpec((1, LANE), lambda b, i: (0, 0))],
        out_shape=[jax.ShapeDtypeStruct((nb, S, D), F32), jax.ShapeDtypeStruct((nb, S, D), BF16),
                   jax.ShapeDtypeStruct((nb, 1, D), F32), jax.ShapeDtypeStruct((1, D), F32),
                   jax.ShapeDtypeStruct((1, LANE), F32)],
        compiler_params=_cparams("arbitrary", "arbitrary"),
    )(x1, ffn, target, gate2, post_g)


def _rotate_half(t):
    lane = lax.broadcasted_iota(jnp.int32, t.shape, 1)
    return jnp.where((lane & 15) < 8, -pltpu.roll(t, LANE - 8, 1), pltpu.roll(t, 8, 1))


def rope_call(name, x, width, colblk, cos, sin, out_dtype, tr):
    nb = x.shape[0]
    R = cos.shape[0]
    nblk = R // tr

    def body(x_ref, c_ref, s_ref, o_ref):
        c, s = c_ref[...], s_ref[...]
        for h in range(width // LANE):
            t = x_ref[0, :, h * LANE:(h + 1) * LANE].astype(F32)
            o_ref[0, :, h * LANE:(h + 1) * LANE] = (t * c + _rotate_half(t) * s).astype(o_ref.dtype)

    tab = pl.BlockSpec((tr, LANE), lambda b, i: (i, 0))
    return pl.pallas_call(
        body, name=name, grid=(nb, nblk),
        in_specs=[pl.BlockSpec((1, tr, width), lambda b, i: (b, i, colblk)), tab, tab],
        out_specs=pl.BlockSpec((1, tr, width), lambda b, i: (b, i, 0)),
        out_shape=jax.ShapeDtypeStruct((nb, R, width), out_dtype),
        compiler_params=_cparams("arbitrary", "arbitrary"),
    )(x, cos, sin)


def rope_tables(n_ctx, seq):
    n_rows = seq // GRID_W
    row = np.repeat(np.arange(n_rows), GRID_W).astype(np.float32)
    col = np.tile(np.arange(GRID_W), n_rows).astype(np.float32)
    axis_dim = ROPE // 2
    inv_freq = jnp.asarray(ROPE_THETA, F32) ** (-jnp.arange(0, axis_dim, 2, dtype=F32) / axis_dim)
    ang_r = jnp.asarray(row)[:, None] * inv_freq
    ang_c = jnp.asarray(col)[:, None] * inv_freq
    ang = jnp.concatenate([ang_r, ang_r, ang_c, ang_c], axis=-1)
    cos = jnp.ones((n_ctx + seq, LANE), F32).at[n_ctx:, KR_LANE:KR_LANE + ROPE].set(jnp.cos(ang))
    sin = jnp.zeros((n_ctx + seq, LANE), F32).at[n_ctx:, KR_LANE:KR_LANE + ROPE].set(jnp.sin(ang))
    return cos, sin


Q_PRESCALE = ATTN_SCALE * math.log2(math.e)


def _attn_weights(q, kc):
    s2 = lax.dot_general(q, kc, (((1,), (1,)), ((), ())), preferred_element_type=F32)
    e = jnp.exp2(s2 - jnp.max(s2, axis=1, keepdims=True))
    return e, 1.0 / jnp.sum(e, axis=1, keepdims=True)


def _key_block(kv, kr):
    lane = lax.broadcasted_iota(jnp.int32, kv.shape, 1)
    return jnp.where(lane < NOPE, kv, kr)


def _rotated_query(q_ref, cos_ref, sin_ref):
    t = q_ref[0].astype(F32)
    return (t * cos_ref[...] + _rotate_half(t) * sin_ref[...]).astype(BF16)


def attn_fwd(q_raw, kv, kr, cos_q, sin_q, tq):
    nb, S, _ = q_raw.shape
    T = kv.shape[1]

    def body(q_ref, kv_ref, kr_ref, c_ref, s_ref, o_ref):
        kvv = kv_ref[0]
        e, r = _attn_weights(_rotated_query(q_ref, c_ref, s_ref), _key_block(kvv, kr_ref[0]))
        o = lax.dot_general(e.astype(BF16), kvv, (((1,), (0,)), ((), ())), preferred_element_type=F32) * r
        lane = lax.broadcasted_iota(jnp.int32, o.shape, 1)
        o_ref[0] = jnp.where(lane >= NOPE, o, 0.0).astype(o_ref.dtype)

    return pl.pallas_call(
        body, name="attn_fwd", grid=(nb, N_HEADS, S // tq),
        in_specs=[pl.BlockSpec((1, tq, HEAD_BLOCK), lambda b, h, i: (b, i, h)),
                  pl.BlockSpec((1, T, HEAD_BLOCK), lambda b, h, i: (b, 0, h)),
                  pl.BlockSpec((1, T, HEAD_BLOCK), lambda b, h, i: (b, 0, 0)),
                  pl.BlockSpec((tq, LANE), lambda b, h, i: (i, 0)), pl.BlockSpec((tq, LANE), lambda b, h, i: (i, 0))],
        out_specs=pl.BlockSpec((1, tq, HEAD_BLOCK), lambda b, h, i: (b, i, h)),
        out_shape=jax.ShapeDtypeStruct((nb, S, QP), BF16),
        compiler_params=_cparams("arbitrary", "arbitrary", "arbitrary"),
    )(q_raw, kv, kr, cos_q, sin_q)


def attn_bwd(q_raw, kv, kr, do, cos_q, sin_q, cos, sin, tq):
    nb, S, _ = q_raw.shape
    T = kv.shape[1]

    def body(q_ref, kv_ref, kr_ref, do_ref, cq_ref, sq_ref, c_ref, s_ref, dq_ref, dkv_ref, dkr_ref):
        h, i = pl.program_id(1), pl.program_id(2)

        @pl.when(i == 0)
        def _():
            dkv_ref[...] = jnp.zeros_like(dkv_ref)

        @pl.when(jnp.logical_and(h == 0, i == 0))
        def _():
            dkr_ref[...] = jnp.zeros_like(dkr_ref)

        qv, kvv, dov = _rotated_query(q_ref, cq_ref, sq_ref), kv_ref[0], do_ref[0]
        kc = _key_block(kvv, kr_ref[0])
        e, r = _attn_weights(qv, kc)
        dor = (dov.astype(F32) * r).astype(BF16)
        dpr = lax.dot_general(dor, kvv, (((1,), (1,)), ((), ())), preferred_element_type=F32)
        ds = (e * (dpr - r * jnp.sum(dpr * e, axis=1, keepdims=True))).astype(BF16)
        dq = lax.dot_general(ds, kc, (((1,), (0,)), ((), ())), preferred_element_type=F32) * ATTN_SCALE
        dq_ref[0] = (dq * c_ref[...] - _rotate_half(dq) * s_ref[...]).astype(dq_ref.dtype)
        dkc = lax.dot_general(ds, qv, (((0,), (0,)), ((), ())), preferred_element_type=F32) * math.log(2.0)
        dv = lax.dot_general(e.astype(BF16), dor, (((0,), (0,)), ((), ())), preferred_element_type=F32)
        lane = lax.broadcasted_iota(jnp.int32, dkc.shape, 1)
        dkv_ref[0] += jnp.where(lane < NOPE, dkc, dv)
        dkr_ref[0] += jnp.where(lane >= NOPE, dkc, 0.0)

    qspec = pl.BlockSpec((1, tq, HEAD_BLOCK), lambda b, h, i: (b, i, h))
    kspec = pl.BlockSpec((1, T, HEAD_BLOCK), lambda b, h, i: (b, 0, h))
    rspec = pl.BlockSpec((1, T, HEAD_BLOCK), lambda b, h, i: (b, 0, 0))
    tab = pl.BlockSpec((tq, LANE), lambda b, h, i: (i, 0))
    return pl.pallas_call(
        body, name="attn_bwd", grid=(nb, N_HEADS, S // tq),
        in_specs=[qspec, kspec, rspec, qspec, tab, tab, tab, tab], out_specs=[qspec, kspec, rspec],
        out_shape=[jax.ShapeDtypeStruct((nb, S, QP), BF16), jax.ShapeDtypeStruct((nb, T, QP), F32),
                   jax.ShapeDtypeStruct((nb, T, HEAD_BLOCK), F32)],
        compiler_params=_cparams("arbitrary", "arbitrary", "arbitrary"),
    )(q_raw, kv, kr, do, cos_q, sin_q, cos, sin)


CONV_HALO = 8


def _segments(n, n_ctx):
    if n_ctx == 0:
        return [(0, n, CONV_HALO)]
    return [(0, n_ctx, CONV_HALO), (n_ctx, n - n_ctx, 2 * CONV_HALO + n_ctx)]


def _halo_scratch(n, n_ctx, tc):
    return pltpu.VMEM((n + CONV_HALO * (len(_segments(n, n_ctx)) + 1), tc), F32)


def _zero_halos(scr, segs):
    z = jnp.zeros((CONV_HALO, scr.shape[1]), scr.dtype)
    scr[0:CONV_HALO, :] = z
    for (_, rows, off) in segs:
        scr[off + rows:off + rows + CONV_HALO, :] = z


CONV_BLOCK_MAX = 256


def _conv_block(n, n_ctx):
    return _tile(math.gcd(n_ctx, n - n_ctx) if n_ctx else n, CONV_BLOCK_MAX, 8)


def _window(scr, off, r0, blk):
    return scr[pl.ds(pl.multiple_of(off - CONV_HALO + r0, 8), blk + 2 * CONV_HALO), :]


def _shifted(win, s):
    v = win if s == 0 else pltpu.roll(win, (-s) % win.shape[0], 0)
    return v[CONV_HALO:win.shape[0] - CONV_HALO]


def _tap_blocks(win, k, sign):
    return [_shifted(win, sign * (o - k // 2)) for o in range(k)]


def _taps(blocks, w):
    acc = None
    for o, blk in enumerate(blocks):
        t = w[o:o + 1, :] * blk
        acc = t if acc is None else acc + t
    return acc


def _tap_grads(xblocks, dpre):
    k = len(xblocks)
    sub8 = lax.broadcasted_iota(jnp.int32, (8, dpre.shape[1]), 0)
    out = jnp.where(sub8 == k, jnp.sum(dpre, axis=0, keepdims=True), 0.0)
    for o, blk in enumerate(xblocks):
        out = out + jnp.where(sub8 == o, jnp.sum(dpre * blk, axis=0, keepdims=True), 0.0)
    return out


def _row_blocks(rows, blk, fn, init=0):
    return lax.fori_loop(0, rows // blk, lambda i, c: fn(pl.multiple_of(i * blk, blk), c), init)


def _gelu(x):
    return 0.5 * x * (1.0 + lax.erf(x * (1.0 / math.sqrt(2.0))))


def _gelu_and_grad(x):
    cdf = 0.5 * (1.0 + lax.erf(x * (1.0 / math.sqrt(2.0))))
    return x * cdf, cdf + x * jnp.exp(-0.5 * x * x) * (1.0 / math.sqrt(2.0 * math.pi))


def ssd_conv_fwd(u, w8, bias, n_ctx, tc):
    nb, T, _ = u.shape
    cb0 = OFF_XBC // tc

    segs, blk = _segments(T, n_ctx), _conv_block(T, n_ctx)

    def body(x_ref, w_ref, b_ref, o_ref, xs):
        _zero_halos(xs, segs)
        for (start, rows, off) in segs:
            xs[off:off + rows, :] = x_ref[0, start:start + rows, :]
        w, bias_v = w_ref[...], b_ref[...]
        for (start, rows, off) in segs:
            def block(r0, carry, start=start, off=off):
                pre = bias_v + _taps(_tap_blocks(_window(xs, off, r0, blk), SSD_K, 1), w)
                o_ref[0, pl.ds(pl.multiple_of(start + r0, blk), blk), :] = _silu(pre)
                return carry

            _row_blocks(rows, blk, block)

    return pl.pallas_call(
        body, name="ssd_conv_fwd", grid=(nb, XBC // tc),
        in_specs=[pl.BlockSpec((1, T, tc), lambda b, j: (b, 0, cb0 + j)),
                  pl.BlockSpec((8, tc), lambda b, j: (0, j)), pl.BlockSpec((1, tc), lambda b, j: (0, j))],
        out_specs=pl.BlockSpec((1, T, tc), lambda b, j: (b, 0, j)),
        out_shape=jax.ShapeDtypeStruct((nb, T, XBC), F32),
        scratch_shapes=[_halo_scratch(T, n_ctx, tc)],
        compiler_params=_cparams("arbitrary", "arbitrary"),
    )(u, w8, bias)


def ssd_conv_bwd(u, w8, bias, dxbc, dxs_direct, n_ctx, tc):
    nb, T, _ = u.shape
    cb0 = OFF_XBC // tc
    n_direct = D_INNER // tc

    segs, blk = _segments(T, n_ctx), _conv_block(T, n_ctx)

    def body(x_ref, w_ref, b_ref, d0_ref, d1_ref, dd_ref, dx_ref, dw_ref, xs, ds):
        j, b = pl.program_id(0), pl.program_id(1)
        _zero_halos(xs, segs)
        _zero_halos(ds, segs)
        for (start, rows, off) in segs:
            xs[off:off + rows, :] = x_ref[0, start:start + rows, :]
        w, bias_v = w_ref[...], b_ref[...]
        has_direct = (j < n_direct).astype(F32)
        rows = jnp.zeros((8, tc), F32)
        for (start, n_rows, off) in segs:
            def block(r0, acc, start=start, off=off):
                xblocks = _tap_blocks(_window(xs, off, r0, blk), SSD_K, 1)
                pre = bias_v + _taps(xblocks, w)
                d = d0_ref[0, 0, pl.ds(pl.multiple_of(start + r0, blk), blk), :] + d1_ref[0, 0, pl.ds(pl.multiple_of(start + r0, blk), blk), :]
                if start == n_ctx:
                    d = d + dd_ref[0, pl.ds(r0, blk), :] * has_direct
                sg = jax.nn.sigmoid(pre)
                dpre = d * (sg * (1.0 + pre * (1.0 - sg)))
                ds[pl.ds(pl.multiple_of(off + r0, 8), blk), :] = dpre
                return acc + _tap_grads(xblocks, dpre)

            rows = _row_blocks(n_rows, blk, block, rows)
        for (start, n_rows, off) in segs:
            def block_dx(r0, carry, start=start, off=off):
                dx_ref[0, pl.ds(pl.multiple_of(start + r0, blk), blk), :] = _taps(_tap_blocks(_window(ds, off, r0, blk), SSD_K, -1), w).astype(dx_ref.dtype)
                return carry

            _row_blocks(n_rows, blk, block_dx)

        @pl.when(b == 0)
        def _():
            dw_ref[...] = rows

        @pl.when(b > 0)
        def _():
            dw_ref[...] += rows

    dspec0 = pl.BlockSpec((1, 1, T, tc), lambda j, b: (0, b, 0, j))
    dspec1 = pl.BlockSpec((1, 1, T, tc), lambda j, b: (1, b, 0, j))
    return pl.pallas_call(
        body, name="ssd_conv_bwd", grid=(XBC // tc, nb),
        in_specs=[pl.BlockSpec((1, T, tc), lambda j, b: (b, 0, cb0 + j)),
                  pl.BlockSpec((8, tc), lambda j, b: (0, j)), pl.BlockSpec((1, tc), lambda j, b: (0, j)),
                  dspec0, dspec1,
                  pl.BlockSpec((1, T - n_ctx, tc), lambda j, b: (b, 0, jnp.minimum(j, n_direct - 1)))],
        out_specs=[pl.BlockSpec((1, T, tc), lambda j, b: (b, 0, j)), pl.BlockSpec((8, tc), lambda j, b: (0, j))],
        out_shape=[jax.ShapeDtypeStruct((nb, T, XBC), BF16), jax.ShapeDtypeStruct((8, XBC), F32)],
        scratch_shapes=[_halo_scratch(T, n_ctx, tc), _halo_scratch(T, n_ctx, tc)],
        compiler_params=_cparams("arbitrary", "arbitrary"),
    )(u, w8, bias, dxbc, dxbc, dxs_direct)


GLU_TC = 256


def glu_interleave(w_up):
    blocks = []
    for j in range(D_FF // GLU_TC):
        blocks += [w_up[:, j * GLU_TC:(j + 1) * GLU_TC], w_up[:, D_FF + j * GLU_TC:D_FF + (j + 1) * GLU_TC]]
    return jnp.concatenate(blocks, axis=1)


def glu_deinterleave(g):
    nj = D_FF // GLU_TC
    gate = [g[:, 2 * j * GLU_TC:(2 * j + 1) * GLU_TC] for j in range(nj)]
    val = [g[:, (2 * j + 1) * GLU_TC:(2 * j + 2) * GLU_TC] for j in range(nj)]
    return jnp.concatenate(gate + val, axis=1)


def glu_fwd(up, w8, bias):
    nb, S, _ = up.shape
    tc = GLU_TC

    segs, blk = _segments(S, 0), _conv_block(S, 0)
    (_, _, off), = segs

    def body(u_ref, w_ref, b_ref, o_ref, xs):
        _zero_halos(xs, segs)
        xs[off:off + S, :] = u_ref[0, :, :tc]
        w, bias_v = w_ref[...], b_ref[...]

        def block(r0, carry):
            gc = bias_v + _taps(_tap_blocks(_window(xs, off, r0, blk), FFN_K, 1), w)
            o_ref[0, pl.ds(r0, blk), :] = (_gelu(gc) * u_ref[0, pl.ds(r0, blk), tc:]).astype(o_ref.dtype)
            return carry

        _row_blocks(S, blk, block)

    return pl.pallas_call(
        body, name="glu_fwd", grid=(nb, D_FF // tc),
        in_specs=[pl.BlockSpec((1, S, 2 * tc), lambda b, j: (b, 0, j)),
                  pl.BlockSpec((8, tc), lambda b, j: (0, j)), pl.BlockSpec((1, tc), lambda b, j: (0, j))],
        out_specs=pl.BlockSpec((1, S, tc), lambda b, j: (b, 0, j)),
        out_shape=jax.ShapeDtypeStruct((nb, S, D_FF), BF16),
        scratch_shapes=[_halo_scratch(S, 0, tc)],
        compiler_params=_cparams("arbitrary", "arbitrary"),
    )(up, w8, bias)


def glu_bwd(up, w8, bias, dact):
    nb, S, _ = up.shape
    tc = GLU_TC

    segs, blk = _segments(S, 0), _conv_block(S, 0)
    (_, _, off), = segs

    def body(u_ref, w_ref, b_ref, d_ref, du_ref, dw_ref, xs, ds):
        b = pl.program_id(1)
        _zero_halos(xs, segs)
        _zero_halos(ds, segs)
        xs[off:off + S, :] = u_ref[0, :, :tc]
        w, bias_v = w_ref[...], b_ref[...]

        def block(r0, acc):
            here = pl.ds(r0, blk)
            xblocks = _tap_blocks(_window(xs, off, r0, blk), FFN_K, 1)
            act, act_grad = _gelu_and_grad(bias_v + _taps(xblocks, w))
            d = d_ref[0, here, :].astype(F32)
            du_ref[0, here, tc:] = (d * act).astype(du_ref.dtype)
            dpre = d * u_ref[0, here, tc:] * act_grad
            ds[pl.ds(pl.multiple_of(off + r0, 8), blk), :] = dpre
            return acc + _tap_grads(xblocks, dpre)

        rows = _row_blocks(S, blk, block, jnp.zeros((8, tc), F32))

        def block_dx(r0, carry):
            du_ref[0, pl.ds(r0, blk), :tc] = _taps(_tap_blocks(_window(ds, off, r0, blk), FFN_K, -1), w).astype(du_ref.dtype)
            return carry

        _row_blocks(S, blk, block_dx)

        @pl.when(b == 0)
        def _():
            dw_ref[...] = rows

        @pl.when(b > 0)
        def _():
            dw_ref[...] += rows

    pair = pl.BlockSpec((1, S, 2 * tc), lambda j, b: (b, 0, j))
    return pl.pallas_call(
        body, name="glu_bwd", grid=(D_FF // tc, nb),
        in_specs=[pair, pl.BlockSpec((8, tc), lambda j, b: (0, j)), pl.BlockSpec((1, tc), lambda j, b: (0, j)),
                  pl.BlockSpec((1, S, tc), lambda j, b: (b, 0, j))],
        out_specs=[pair, pl.BlockSpec((8, tc), lambda j, b: (0, j))],
        out_shape=[jax.ShapeDtypeStruct((nb, S, 2 * D_FF), BF16), jax.ShapeDtypeStruct((8, D_FF), F32)],
        scratch_shapes=[_halo_scratch(S, 0, tc), _halo_scratch(S, 0, tc)],
        compiler_params=_cparams("arbitrary", "arbitrary"),
    )(up, w8, bias, dact)


def _chunk_of(d, k, n_cc, n_ch):
    rev = jnp.where(k < n_cc, n_cc - 1 - k, n_cc + n_ch - 1 - k)
    return jnp.where(d == 1, rev, k)


def _lane_pick(v, lane_iota, l):
    return jnp.sum(jnp.where(lane_iota == l, v, 0.0), axis=1, keepdims=True)


def head_spread_matrix():
    return (jnp.arange(LANE)[:, None] == (jnp.arange(D_INNER)[None, :] // SSD_P)).astype(BF16)


def _split_dot(x, e, dims):
    hi = x.astype(BF16)
    lo = (x - hi.astype(F32)).astype(BF16)
    return (lax.dot_general(hi, e, dims, preferred_element_type=F32)
            + lax.dot_general(lo, e, dims, preferred_element_type=F32))


def _spread(x, e):
    return _split_dot(x, e, (((1,), (0,)), ((), ())))


def _gather_heads(y, e):
    return _split_dot(y, e, (((1,), (1,)), ((), ())))


def _softplus(x):
    return jnp.maximum(x, 0.0) + jnp.log(1.0 + jnp.exp(-jnp.abs(x)))


def ssd_dt_inputs(u, a_log, dt_bias):
    pad = LANE - SSD_HEADS
    dt = u[..., OFF_DT:OFF_DT + 2 * SSD_HEADS]
    dt2 = jnp.stack([jnp.pad(dt[..., i * SSD_HEADS:(i + 1) * SSD_HEADS], ((0, 0), (0, 0), (0, pad))) for i in range(2)])

    def lanes(v):
        return jnp.pad(v.reshape(2, 1, SSD_HEADS), ((0, 0), (0, 0), (0, pad)))

    return dt2, lanes(a_log), lanes(dt_bias)


def _ssd_common(d, dt_raw, alog, dtb):
    Q = dt_raw.shape[0]
    row = lax.broadcasted_iota(jnp.int32, (Q, Q), 0)
    col = lax.broadcasted_iota(jnp.int32, (Q, Q), 1)
    rev = d == 1
    maskb = jnp.where(rev, row, col) <= jnp.where(rev, col, row)
    tri = maskb.astype(F32)
    A = -jnp.exp(alog)
    dtv = _softplus(dt_raw + dtb)
    a = dtv * A
    cum = lax.dot_general(tri, a, (((1,), (0,)), ((), ())), precision=lax.Precision.HIGHEST, preferred_element_type=F32)
    tot = jnp.sum(a, axis=0, keepdims=True)
    return maskb, tri, A, dtv, cum, tot


def ssd_fwd(xbc, dt2, alog2, dtb2, n_ctx, hosted):
    nb, T, _ = xbc.shape
    S = T - n_ctx
    n_ch, n_cc = T // CHUNK, n_ctx // CHUNK
    Q = CHUNK
    n_pairs = SSD_HEADS // 2
    n_ex = hosted.n
    n_in = 5

    def body(*refs):
        x_ref, dt_ref, al_ref, db_ref, e_ref = refs[:n_in]
        send_refs = refs[n_in:n_in + n_ex]
        y_ref, hin_ref = refs[n_in + n_ex:n_in + 2 + n_ex]
        recv_refs = refs[n_in + 2 + n_ex:n_in + 2 + 2 * n_ex]
        H, *sems = refs[n_in + 2 + 2 * n_ex:]
        d, k = pl.program_id(1), pl.program_id(2)
        first_step = jnp.logical_and(jnp.logical_and(pl.program_id(0) == 0, d == 0), k == 0)
        last_step = jnp.logical_and(jnp.logical_and(pl.program_id(0) == nb - 1, d == 1), k == n_ch - 1)
        begin_exchange, end_exchange = hosted.steps(send_refs, recv_refs, sems, first_step, last_step)
        begin_exchange()

        @pl.when(k == 0)
        def _():
            H[...] = jnp.zeros_like(H)

        maskb, tri, A, dtv, cum, tot = _ssd_common(d, dt_ref[0, 0], al_ref[0], db_ref[0])
        e = e_ref[...]
        cumT = cum.T
        cum_e, dt_e = _spread(cum, e), _spread(dtv, e)
        tot_e = _spread(jnp.broadcast_to(tot, (8, LANE)), e)[0:1]
        hin_ref[0, 0, 0] = H[...].astype(BF16)
        lane = lax.broadcasted_iota(jnp.int32, (Q, LANE), 1)
        lane1 = lax.broadcasted_iota(jnp.int32, (1, LANE), 1)
        subc = lax.broadcasted_iota(jnp.int32, (LANE, 1), 0)
        half = lane < SSD_P
        for g in range(SSD_GROUPS):
            Bg = x_ref[0, :, D_INNER + g * SSD_N:D_INNER + (g + 1) * SSD_N].astype(BF16)
            Cg = x_ref[0, :, D_INNER + GN + g * SSD_N:D_INNER + GN + (g + 1) * SSD_N].astype(BF16)
            Gm = lax.dot_general(Cg, Bg, (((1,), (1,)), ((), ())), preferred_element_type=F32)
            for pr in range(n_pairs // SSD_GROUPS):
                p = g * (n_pairs // SSD_GROUPS) + pr
                sc, dtp, totp = [t[:, p * LANE:(p + 1) * LANE] for t in (cum_e, dt_e, tot_e)]
                swapped = pltpu.roll(sc, SSD_P, 1)
                s0c, s1c = jnp.where(half, sc, swapped), jnp.where(half, swapped, sc)
                s0r, s1r = cumT[2 * p:2 * p + 1, :], cumT[2 * p + 1:2 * p + 2, :]
                tot0, tot1 = _lane_pick(tot, lane1, 2 * p), _lane_pick(tot, lane1, 2 * p + 1)
                M0 = (Gm * jnp.exp(jnp.where(maskb, s0c - s0r, NEG_BIG))).astype(BF16)
                M1 = (Gm * jnp.exp(jnp.where(maskb, s1c - s1r, NEG_BIG))).astype(BF16)
                xd = x_ref[0, :, p * LANE:(p + 1) * LANE] * dtp
                xdb = xd.astype(BF16)
                yd = jnp.where(half,
                               lax.dot_general(M0, xdb, (((1,), (0,)), ((), ())), preferred_element_type=F32),
                               lax.dot_general(M1, xdb, (((1,), (0,)), ((), ())), preferred_element_type=F32))
                Hp = H[p * LANE:(p + 1) * LANE, :]
                yo = lax.dot_general(Cg, Hp.astype(BF16), (((1,), (1,)), ((), ())), preferred_element_type=F32) * jnp.exp(sc)
                y_ref[0, 0, :, p * LANE:(p + 1) * LANE] = yd + yo
                xdw = (xd * jnp.exp(totp - sc)).astype(BF16)
                etot = jnp.exp(jnp.where(subc < SSD_P, tot0, tot1))
                H[p * LANE:(p + 1) * LANE, :] = Hp * etot + lax.dot_general(
                    xdw, Bg, (((0,), (0,)), ((), ())), preferred_element_type=F32)
        end_exchange()

    def ymap(b, d, k):
        return (d, b, _chunk_of(d, jnp.maximum(k, n_cc), n_cc, n_ch) - n_cc, 0)

    return pl.pallas_call(
        body, name="ssd_fwd", grid=(nb, 2, n_ch),
        in_specs=[pl.BlockSpec((1, Q, XBC), lambda b, d, k: (b, _chunk_of(d, k, n_cc, n_ch), 0)),
                  pl.BlockSpec((1, 1, Q, LANE), lambda b, d, k: (d, b, _chunk_of(d, k, n_cc, n_ch), 0)),
                  pl.BlockSpec((1, 1, LANE), lambda b, d, k: (d, 0, 0)), pl.BlockSpec((1, 1, LANE), lambda b, d, k: (d, 0, 0)),
                  pl.BlockSpec((LANE, D_INNER), lambda b, d, k: (0, 0))] + hosted.specs,
        out_specs=[pl.BlockSpec((1, 1, Q, D_INNER), ymap),
                   pl.BlockSpec((1, 1, 1, D_INNER, SSD_N), lambda b, d, k: (d, b, k, 0, 0))] + hosted.specs,
        out_shape=[jax.ShapeDtypeStruct((2, nb, S, D_INNER), F32),
                   jax.ShapeDtypeStruct((2, nb, n_ch, D_INNER, SSD_N), BF16)] + hosted.out_shape,
        scratch_shapes=[pltpu.VMEM((D_INNER, SSD_N), F32)] + hosted.scratch,
        compiler_params=_cparams("arbitrary", "arbitrary", "arbitrary"),
    )(xbc, dt2, alog2, dtb2, head_spread_matrix(), *hosted.arrays)


def ssd_bwd(xbc, dt2, alog2, dtb2, hin, dy, n_ctx, hosted):
    nb, T, _ = xbc.shape
    n_ex = hosted.n
    n_ch, n_cc = T // CHUNK, n_ctx // CHUNK
    n_in = 7
    Q = CHUNK
    n_pairs = SSD_HEADS // 2
    NT = (((1,), (1,)), ((), ()))
    NN = (((1,), (0,)), ((), ()))
    TN = (((0,), (0,)), ((), ()))

    def dot(a, b, dims):
        return lax.dot_general(a.astype(BF16), b.astype(BF16), dims, preferred_element_type=F32)

    def body(*refs):
        x_ref, dt_ref, al_ref, db_ref, e_ref, hin_ref, dy_ref = refs[:n_in]
        send_refs = refs[n_in:n_in + n_ex]
        dx_ref, ddt_ref, st_ref = refs[n_in + n_ex:n_in + 3 + n_ex]
        recv_refs = refs[n_in + 3 + n_ex:n_in + 3 + 2 * n_ex]
        dH, dce, dde, *sems = refs[n_in + 3 + 2 * n_ex:]
        d, kk = pl.program_id(1), pl.program_id(2)
        ks = n_ch - 1 - kk
        first_step = jnp.logical_and(jnp.logical_and(pl.program_id(0) == 0, d == 0), kk == 0)
        last_step = jnp.logical_and(jnp.logical_and(pl.program_id(0) == nb - 1, d == 1), kk == n_ch - 1)
        begin_exchange, end_exchange = hosted.steps(send_refs, recv_refs, sems, first_step, last_step)
        begin_exchange()

        @pl.when(kk == 0)
        def _():
            dH[...] = jnp.zeros_like(dH)

        @pl.when(jnp.logical_and(jnp.logical_and(pl.program_id(0) == 0, d == 0), kk == 0))
        def _():
            st_ref[...] = jnp.zeros_like(st_ref)

        dt_raw = dt_ref[0, 0]
        alog, dtb_v = al_ref[0], db_ref[0]
        maskb, tri, A, dtv, cum, tot = _ssd_common(d, dt_raw, alog, dtb_v)
        e = e_ref[...]
        cumT = cum.T
        cum_e, dt_e = _spread(cum, e), _spread(dtv, e)
        tot_e = _spread(jnp.broadcast_to(tot, (8, LANE)), e)[0:1]
        live = (ks >= n_cc).astype(F32)
        lane = lax.broadcasted_iota(jnp.int32, (Q, LANE), 1)
        lane1 = lax.broadcasted_iota(jnp.int32, (1, LANE), 1)
        sub = lax.broadcasted_iota(jnp.int32, (LANE, Q), 0)
        subc = lax.broadcasted_iota(jnp.int32, (LANE, 1), 0)
        half = lane < SSD_P
        halfc = subc < SSD_P
        pair_ones = ((lax.broadcasted_iota(jnp.int32, (2 * Q, LANE), 0) >= Q).astype(jnp.int32)
                     == (lax.broadcasted_iota(jnp.int32, (2 * Q, LANE), 1) >= SSD_P).astype(jnp.int32)).astype(BF16)
        dcumT = jnp.zeros((LANE, Q), F32)
        dtot = jnp.zeros((1, LANE), F32)
        dtot_parts = []
        for g in range(SSD_GROUPS):
            Bg = x_ref[0, :, D_INNER + g * SSD_N:D_INNER + (g + 1) * SSD_N].astype(BF16)
            Cg = x_ref[0, :, D_INNER + GN + g * SSD_N:D_INNER + GN + (g + 1) * SSD_N].astype(BF16)
            Gm = lax.dot_general(Cg, Bg, NT, preferred_element_type=F32)
            dG = jnp.zeros((Q, Q), F32)
            dC = jnp.zeros((Q, SSD_N), F32)
            dB = jnp.zeros((Q, SSD_N), F32)
            for pr in range(n_pairs // SSD_GROUPS):
                p = g * (n_pairs // SSD_GROUPS) + pr
                l0, l1 = 2 * p, 2 * p + 1
                sc, dtp, totp = [t[:, p * LANE:(p + 1) * LANE] for t in (cum_e, dt_e, tot_e)]
                swapped = pltpu.roll(sc, SSD_P, 1)
                s0c, s1c = jnp.where(half, sc, swapped), jnp.where(half, swapped, sc)
                s0r, s1r = cumT[l0:l0 + 1, :], cumT[l1:l1 + 1, :]
                tot0, tot1 = _lane_pick(tot, lane1, l0), _lane_pick(tot, lane1, l1)
                L0 = jnp.exp(jnp.where(maskb, s0c - s0r, NEG_BIG))
                L1 = jnp.exp(jnp.where(maskb, s1c - s1r, NEG_BIG))
                M0, M1 = Gm * L0, Gm * L1
                xs = x_ref[0, :, p * LANE:(p + 1) * LANE]
                xd = xs * dtp
                es = jnp.exp(sc)
                dte = jnp.exp(totp - sc)
                etot = jnp.exp(jnp.where(halfc, tot0, tot1))
                dyp = dy_ref[0, :, p * LANE:(p + 1) * LANE] * live
                Hp = hin_ref[0, 0, 0, p * LANE:(p + 1) * LANE, :]
                dHp = dH[p * LANE:(p + 1) * LANE, :]
                bdh = dot(Bg, dHp, NT)
                mtdy = dot(jnp.concatenate([M0, M1], axis=1), dyp, TN)
                dxd = jnp.where(half, mtdy[:Q], mtdy[Q:]) + bdh * dte
                dy0 = jnp.where(half, dyp, 0.0)
                dm = dot(jnp.concatenate([dy0, dyp - dy0], axis=0), xd, NT)
                dM0, dM1 = dm[:Q], dm[Q:]
                dG = dG + dM0 * L0 + dM1 * L1
                dyes = dyp * es
                xdw = xd * dte
                dC = dC + dot(dyes, Hp, NN)
                dB = dB + dot(xdw, dHp, NN)
                W0, W1 = dM0 * M0, dM1 * M1
                yoff = dot(Cg, Hp, NT) * es
                r_off = dyp * yoff
                r_st = xd * bdh * dte
                hh = jnp.sum(dHp * Hp.astype(F32), axis=1, keepdims=True) * etot
                w_rows = _split_dot(jnp.concatenate([W0, W1], axis=1), pair_ones, NN) * (1.0 / SSD_P)
                dce[:, p * LANE:(p + 1) * LANE] = r_off - r_st + w_rows
                dde[:, p * LANE:(p + 1) * LANE] = dxd * xs
                dtot_parts.append(jnp.sum(r_st, axis=0, keepdims=True))
                for (l, W, hselc) in ((l0, W0, halfc), (l1, W1, jnp.logical_not(halfc))):
                    row_g = -jnp.sum(W, axis=0, keepdims=True)
                    dcumT = dcumT + jnp.where(sub == l, row_g, 0.0)
                    dtot = dtot + jnp.where(lane1 == l, jnp.sum(jnp.where(hselc, hh, 0.0), axis=0, keepdims=True), 0.0)
                dx_ref[0, 0, :, p * LANE:(p + 1) * LANE] = dxd * dtp
                dH[p * LANE:(p + 1) * LANE, :] = dHp * etot + dot(dyes, Cg, TN)
            dx_ref[0, 0, :, D_INNER + g * SSD_N:D_INNER + (g + 1) * SSD_N] = dB + dot(dG, Cg, TN)
            dx_ref[0, 0, :, D_INNER + GN + g * SSD_N:D_INNER + GN + (g + 1) * SSD_N] = dC + dot(dG, Bg, NN)
        dcum_all = dcumT.T + _gather_heads(dce[...], e)
        dtot_e = jnp.broadcast_to(jnp.concatenate(dtot_parts, axis=1), (8, D_INNER))
        dtot = dtot + _gather_heads(dtot_e, e)[0:1]
        da = lax.dot_general(tri, dcum_all, TN, precision=lax.Precision.HIGHEST, preferred_element_type=F32) + dtot
        ddtv = _gather_heads(dde[...], e) + da * A
        ddt_raw = ddtv * jax.nn.sigmoid(dt_raw + dtb_v)
        ddt_ref[0, 0] = ddt_raw
        sub8 = lax.broadcasted_iota(jnp.int32, (8, LANE), 0)
        st_ref[...] += (jnp.where(sub8 == 2 * d, jnp.sum(da * dtv * A, axis=0, keepdims=True), 0.0)
                        + jnp.where(sub8 == 2 * d + 1, jnp.sum(ddt_raw, axis=0, keepdims=True), 0.0))
        end_exchange()

    def cmap(d, kk):
        return _chunk_of(d, n_ch - 1 - kk, n_cc, n_ch)

    def dymap(b, d, kk):
        return (b, _chunk_of(d, jnp.maximum(n_ch - 1 - kk, n_cc), n_cc, n_ch) - n_cc, 0)

    return pl.pallas_call(
        body, name="ssd_bwd", grid=(nb, 2, n_ch),
        in_specs=[pl.BlockSpec((1, Q, XBC), lambda b, d, kk: (b, cmap(d, kk), 0)),
                  pl.BlockSpec((1, 1, Q, LANE), lambda b, d, kk: (d, b, cmap(d, kk), 0)),
                  pl.BlockSpec((1, 1, LANE), lambda b, d, kk: (d, 0, 0)), pl.BlockSpec((1, 1, LANE), lambda b, d, kk: (d, 0, 0)),
                  pl.BlockSpec((LANE, D_INNER), lambda b, d, kk: (0, 0)),
                  pl.BlockSpec((1, 1, 1, D_INNER, SSD_N), lambda b, d, kk: (d, b, n_ch - 1 - kk, 0, 0)),
                  pl.BlockSpec((1, Q, D_INNER), dymap)] + hosted.specs,
        out_specs=[pl.BlockSpec((1, 1, Q, XBC), lambda b, d, kk: (d, b, cmap(d, kk), 0)),
                   pl.BlockSpec((1, 1, Q, LANE), lambda b, d, kk: (d, b, cmap(d, kk), 0)),
                   pl.BlockSpec((8, LANE), lambda b, d, kk: (0, 0))] + hosted.specs,
        out_shape=[jax.ShapeDtypeStruct((2, nb, T, XBC), F32), jax.ShapeDtypeStruct((2, nb, T, LANE), F32),
                   jax.ShapeDtypeStruct((8, LANE), F32)] + hosted.out_shape,
        scratch_shapes=[pltpu.VMEM((D_INNER, SSD_N), F32), pltpu.VMEM((Q, D_INNER), F32), pltpu.VMEM((Q, D_INNER), F32)] + hosted.scratch,
        compiler_params=_cparams("arbitrary", "arbitrary", "arbitrary"),
    )(xbc, dt2, alog2, dtb2, head_spread_matrix(), hin, dy, *hosted.arrays)


def _adamw(w, g, m, v):
    mn = ADAM_B1 * m + (1.0 - ADAM_B1) * g
    vn = ADAM_B2 * v + (1.0 - ADAM_B2) * jnp.square(g)
    m_hat = mn / (1.0 - ADAM_B1 ** ADAM_STEP)
    v_hat = vn / (1.0 - ADAM_B2 ** ADAM_STEP)
    return -ADAM_LR * (m_hat / (jnp.sqrt(v_hat) + ADAM_EPS) + ADAM_WD * w), mn, vn


def adamw_matrix(name, w, g_slots, m, v):
    K, n = w.shape
    s = g_slots.shape[0]
    tr = _tile(K, 256, 8)

    def body(w_ref, g_ref, m_ref, v_ref, go_ref, d_ref, mo_ref, vo_ref):
        g = g_ref[0].astype(F32)
        for j in range(1, s):
            g = g + g_ref[j].astype(F32)
        go_ref[...] = g
        d_ref[...], mo_ref[...], vo_ref[...] = _adamw(w_ref[...], g, m_ref[...], v_ref[...])

    spec = pl.BlockSpec((tr, n), lambda i: (i, 0))
    return pl.pallas_call(
        body, name=name, grid=(K // tr,),
        in_specs=[spec, pl.BlockSpec((s, tr, n), lambda i: (0, i, 0)), spec, spec], out_specs=[spec] * 4,
        out_shape=[jax.ShapeDtypeStruct((K, n), F32)] * 4,
        compiler_params=_cparams("arbitrary"),
    )(w, g_slots, m, v)


def adamw_small(ws, gs, ms, vs):
    n = len(ws)

    def body(*refs):
        for i in range(n):
            d, mn, vn = _adamw(refs[i][...], refs[n + i][...], refs[2 * n + i][...], refs[3 * n + i][...])
            refs[4 * n + i][...] = d
            refs[5 * n + i][...] = mn
            refs[6 * n + i][...] = vn

    shapes = [jax.ShapeDtypeStruct(w.shape, F32) for w in ws]
    out = pl.pallas_call(body, name="adamw_small", out_shape=shapes * 3)(*ws, *gs, *ms, *vs)
    return out[:n], out[n:2 * n], out[2 * n:]


def sum_slots(name, x):
    n = x.shape[0]

    def fn(t):
        acc = t[0]
        for j in range(1, n):
            acc = acc + t[j]
        return (acc,)

    return ew_call(name, fn, [x], [(x.shape[1:], F32)])[0]


def _pack_rows(parts):
    rows = []
    for p in parts:
        flat = p.reshape(1, -1)
        n = flat.shape[1]
        rows.append(jnp.pad(flat, ((0, 0), (0, -(-n // (8 * LANE)) * 8 * LANE - n))).reshape(-1, LANE))
    return jnp.concatenate(rows, axis=0)


def _unpack_rows(pack, shapes):
    out, r = [], 0
    for s in shapes:
        n = int(np.prod(s))
        nr = -(-n // (8 * LANE)) * 8
        out.append(pack[r:r + nr].reshape(1, -1)[:, :n].reshape(s))
        r += nr
    return out


def _mesh_pos():
    return lax.axis_index("x"), lax.axis_index("y"), lax.axis_index("c")


N_PEERS = N_DEV - 1


def all_gather(name, vs):
    n = len(vs)

    def body(*refs):
        _ag_start(refs[:n], refs[n:2 * n], *refs[2 * n:])
        _ag_finish(refs[:n], refs[n:2 * n], *refs[2 * n:])

    hbm = pl.BlockSpec(memory_space=pl.ANY)
    return pl.pallas_call(
        body, name=name, out_shape=_ag_out_shape(vs), in_specs=[hbm] * n, out_specs=[hbm] * n,
        scratch_shapes=_a2a_scratch(n),
    )(*vs)


def _ag_out_shape(vs):
    return [jax.ShapeDtypeStruct((N_DEV,) + v.shape, v.dtype) for v in vs]


def _ag_copies(x_refs, out_refs, send_sems, recv_sems, local_sems):
    n = len(x_refs)
    x, y, c = _mesh_pos()
    me, sibling = (x, y, c), (x, y, 1 - c)
    chips = [(1 - x, y), (x, 1 - y), (1 - x, 1 - y)]

    def slot(a, px, py, pc):
        return out_refs[a].at[4 * px + 2 * py + pc]

    def copy(a, k, block, to, src=None):
        return pltpu.make_async_remote_copy(
            src_ref=slot(a, *block) if src is None else src, dst_ref=slot(a, *block),
            send_sem=send_sems.at[N_PEERS * a + k], recv_sem=recv_sems.at[N_PEERS * a + k],
            device_id=to, device_id_type=MESH)

    local = [pltpu.make_async_copy(x_refs[a], slot(a, *me), local_sems.at[a]) for a in range(n)]
    first = []
    for a in range(n):
        first.append(copy(a, 0, me, sibling, src=x_refs[a]))
        first += [copy(a, 1 + j, me, (*chip, c), src=x_refs[a]) for j, chip in enumerate(chips)]
    passed = [(copy(a, 1 + j, (*chip, c), me), copy(a, 4 + j, (*chip, c), sibling))
              for j, chip in enumerate(chips) for a in range(n)]
    from_sibling = []
    for a in range(n):
        from_sibling.append(copy(a, 0, sibling, me))
        from_sibling += [copy(a, 4 + j, (*chip, 1 - c), me) for j, chip in enumerate(chips)]
    return local, first, passed, from_sibling


def _ag_start(*refs):
    local, first, _, _ = _ag_copies(*refs)
    for cp in local + first:
        cp.start()


def _ag_finish(*refs):
    local, first, passed, from_sibling = _ag_copies(*refs)
    for arrived, hand_on in passed:
        arrived.wait_recv()
        hand_on.start()
    for cp in from_sibling:
        cp.wait_recv()
    for cp in first + [hand_on for _, hand_on in passed]:
        cp.wait_send()
    for cp in local:
        cp.wait()


def _a2a_scratch(n):
    return [pltpu.SemaphoreType.DMA((N_PEERS * n,)), pltpu.SemaphoreType.DMA((N_PEERS * n,)), pltpu.SemaphoreType.DMA((n,))]


def _a2a_copies(x_refs, out_refs, send_sems, recv_sems, local_sems):
    n = len(x_refs)
    x, y, c = _mesh_pos()
    me = 4 * x + 2 * y + c
    local = [pltpu.make_async_copy(x_refs[a].at[me], out_refs[a].at[me], local_sems.at[a]) for a in range(n)]
    remote = []
    for k in range(1, N_DEV):
        px, py, pc = x ^ ((k >> 2) & 1), y ^ ((k >> 1) & 1), c ^ (k & 1)
        for a in range(n):
            remote.append(pltpu.make_async_remote_copy(
                src_ref=x_refs[a].at[4 * px + 2 * py + pc], dst_ref=out_refs[a].at[me],
                send_sem=send_sems.at[N_PEERS * a + k - 1], recv_sem=recv_sems.at[N_PEERS * a + k - 1],
                device_id=(px, py, pc), device_id_type=MESH))
    return local, remote


def _a2a_start(local, remote):
    for cp in local + remote:
        cp.start()


def _a2a_wait(local, remote):
    for cp in remote:
        cp.wait_recv()
    for cp in remote:
        cp.wait_send()
    for cp in local:
        cp.wait()


class Hosted:
    def __init__(self, start=None, finish=None, arrays=(), out_shape=()):
        self.start, self.finish, self.arrays, self.out_shape = start, finish, list(arrays), list(out_shape)
        self.n = len(self.arrays)
        self.specs = [pl.BlockSpec(memory_space=pl.ANY)] * self.n
        self.scratch = _a2a_scratch(self.n) if self.n else []

    def steps(self, send_refs, recv_refs, sems, first_step, last_step):
        def begin():
            if self.n:
                pl.when(first_step)(lambda: self.start(send_refs, recv_refs, *sems))

        def end():
            if self.n:
                pl.when(last_step)(lambda: self.finish(send_refs, recv_refs, *sems))

        return begin, end


def hosted_all_to_all(vs):
    return Hosted(lambda *r: _a2a_start(*_a2a_copies(*r)), lambda *r: _a2a_wait(*_a2a_copies(*r)), vs,
                  [jax.ShapeDtypeStruct(v.shape, v.dtype) for v in vs])


def hosted_all_gather(vs):
    return Hosted(_ag_start, _ag_finish, vs, _ag_out_shape(vs))


def _taps8(w):
    return jnp.concatenate([w, jnp.zeros((8 - w.shape[0], w.shape[1]), w.dtype)], axis=0)


FIRST = ("w_in",)
LATE_WEIGHTS = ("w_out", "w_up", "w_down", "w_q_up", "w_kv_up")


def first_weights_to_internal(w_in):
    cq, ckv, kr, z, xbc, dt = jnp.split(w_in, np.cumsum(IN_SPLITS)[:-1].tolist(), axis=1)
    K = w_in.shape[0]

    def zeros(n):
        return jnp.zeros((K, n), w_in.dtype)

    w_in_p = jnp.concatenate([cq, zeros(KR_LANE), kr, zeros(LANE - KR_LANE - ROPE), ckv, zeros(OFF_Z - OFF_CKV - KV_RANK),
                              z, xbc, dt, zeros(WIN_P - OFF_DT - 2 * SSD_HEADS)], axis=1)
    return dict(w_in_p=w_in_p)


def late_weights_to_internal(w_out, w_up, w_down, w_q_up, w_kv_up):
    attn_rows = w_out[:N_HEADS * V_DIM].reshape(N_HEADS, V_DIM, -1)
    w_out_p = jnp.concatenate([jnp.pad(attn_rows, ((0, 0), (HEAD_BLOCK - V_DIM, 0), (0, 0))).reshape(QP, -1),
                               w_out[N_HEADS * V_DIM:]], axis=0)
    w_q_p = jnp.pad(w_q_up.reshape(Q_RANK, N_HEADS, NOPE + ROPE), ((0, 0), (0, 0), (0, HEAD_BLOCK - NOPE - ROPE))).reshape(Q_RANK, QP)
    return dict(w_out_p=w_out_p, w_up=glu_interleave(w_up), w_down=w_down, w_q_p=w_q_p, w_kv=w_kv_up)


def _q_grad(g_q_p):
    return g_q_p.reshape(Q_RANK, N_HEADS, HEAD_BLOCK)[:, :, :NOPE + ROPE].reshape(Q_RANK, -1)


def _out_grad(g_out_p):
    return jnp.concatenate([g_out_p[:QP].reshape(N_HEADS, HEAD_BLOCK, -1)[:, HEAD_BLOCK - V_DIM:].reshape(N_HEADS * V_DIM, -1),
                            g_out_p[QP:]], axis=0)


EARLY = ("w_out", "w_up", "w_down", "w_q_up", "w_kv_up")


def local_step(x, ctx, target, mod_x, mod_c, W, late_shards, V):
    nb, S, D = x.shape
    C = ctx.shape[1]
    T = C + S
    tr = _tile(math.gcd(C, S), 256, 8)
    tq = _tile(S, 256, 8)
    tc = 256
    cblk = C // tr
    m = [mod_x[:, i * D:(i + 1) * D][:, None, :] for i in range(N_MOD)]
    mc = [mod_c[:, i * D:(i + 1) * D] for i in range(2)]
    ssd_w8, ffn_w8 = _taps8(V["ssd_conv_w"]), _taps8(V["ffn_conv_w"])
    dexp = jnp.repeat(V["ssd_d"].reshape(-1), SSD_P).reshape(1, D_INNER)
    cosT, sinT = rope_tables(C, S)
    cosS, sinS = cosT[C:], sinT[C:]

    (h1x,) = rows_fwd("prenorm_x", fn_prenorm, nb, S // tr, tr, [(x, D, 0, 0)], [m[0], m[1]], [V["mix_pre_norm"]], [(D, BF16)])
    (h1c,) = rows_fwd("prenorm_c", fn_prenorm, nb, C // tr, tr, [(ctx, D, 0, 0)], [], [mc[0], mc[1], V["mix_pre_norm"]], [(D, BF16)])
    h1 = jnp.concatenate([h1c, h1x], axis=1).reshape(nb * T, D)
    u = matmul("in_proj", [(h1, W["w_in_p"])], "nn", F32).reshape(nb, T, WIN_P)
    xbc = ssd_conv_fwd(u, ssd_w8, V["ssd_conv_b"], C, tc)
    dt2, alog2, dtb2 = ssd_dt_inputs(u, V["ssd_a_log"], V["ssd_dt_bias"])
    y2, hin, *late = ssd_fwd(xbc, dt2, alog2, dtb2, C, hosted_all_gather(late_shards))
    W = dict(W, **late_weights_to_internal(*[_whole(s, n) for s, n in zip(late, LATE_WEIGHTS)]))
    y2 = y2.reshape(2 * nb, S, D_INNER)
    (qn,) = rows_fwd("q_norm", fn_rms, nb, S // tr, tr, [(u, Q_RANK, OFF_CQ // Q_RANK, cblk)], [], [V["q_norm"]], [(Q_RANK, BF16)])
    (kvn,) = rows_fwd("kv_norm", fn_rms, nb, T // tr, tr, [(u, KV_RANK, OFF_CKV // KV_RANK, 0)], [], [V["kv_norm"]], [(KV_RANK, BF16)])
    qn2, kvn2 = qn.reshape(nb * S, Q_RANK), kvn.reshape(nb * T, KV_RANK)
    q_raw = matmul("q_up", [(qn2, W["w_q_p"])], "nn", F32).reshape(nb, S, QP)
    kv = matmul("kv_up", [(kvn2, W["w_kv"])], "nn", BF16).reshape(nb, T, QP)
    cos_q, sin_q = cosS * Q_PRESCALE, sinS * Q_PRESCALE
    kr = rope_call("rope_k", u, LANE, OFF_KR // LANE, cosT, sinT, BF16, tr)
    o = attn_fwd(q_raw, kv, kr, cos_q, sin_q, tq)
    fin_rows = [(y2, D_INNER, 0, 0, 0), (y2, D_INNER, 0, 0, nb), (xbc, D_INNER, 0, cblk), (u, D_INNER, OFF_Z // D_INNER, cblk)]
    fin_gl = [dexp, V["ssd_norm"]]
    (ssd,) = rows_fwd("ssd_finish", fn_ssd_finish, nb, S // tr, tr, fin_rows, [], fin_gl, [(D_INNER, BF16)])
    o2, ssd2 = o.reshape(nb * S, QP), ssd.reshape(nb * S, D_INNER)
    mix = matmul("out_proj", [(o2, W["w_out_p"][:QP]), (ssd2, W["w_out_p"][QP:])], "nn", F32).reshape(nb, S, D)
    pm_rows = [(x, D, 0, 0), (mix, D, 0, 0)]
    pm_pb = [m[2], m[4], m[3]]
    pm_gl = [V["mix_post_norm"], V["ffn_pre_norm"]]
    x1, h2 = rows_fwd("postmix", fn_postmix, nb, S // tr, tr, pm_rows, pm_pb, pm_gl, [(D, F32), (D, BF16)])
    h22 = h2.reshape(nb * S, D)
    up = matmul("up_proj", [(h22, W["w_up"])], "nn", F32).reshape(nb, S, 2 * D_FF)
    act = glu_fwd(up, ffn_w8, V["ffn_conv_b"])
    act2 = act.reshape(nb * S, D_FF)
    ffn = matmul("down_proj", [(act2, W["w_down"])], "nn", F32).reshape(nb, S, D)
    dx1, dffn, dgate2, d_ffn_post, loss = final_call(x1, ffn, target, m[5], V["ffn_post_norm"], tr)

    dffn2 = dffn.reshape(nb * S, D)
    dact = matmul("down_dgrad", [(dffn2, W["w_down"])], "nt", BF16).reshape(nb, S, D_FF)
    g_down = matmul_tn("down_wgrad", act2, dffn2)
    dup, ffn_rows = glu_bwd(up, ffn_w8, V["ffn_conv_b"], dact)
    dup2 = dup.reshape(nb * S, 2 * D_FF)
    dh2 = matmul("up_dgrad", [(dup2, W["w_up"])], "nt", BF16).reshape(nb, S, D)
    g_up = matmul_tn("up_wgrad", h22, dup2)
    dx_a, dmix, dgate1, dscale2, dshift2, d_mix_post, d_ffn_pre = rows_bwd(
        "postmix_bwd", fn_postmix, nb, S // tr, tr, pm_rows, pm_pb, pm_gl,
        [(dx1, D, 0, 0), (dh2, D, 0, 0)], [(0, F32), (1, BF16)])
    dmix2 = dmix.reshape(nb * S, D)
    dcat = matmul("out_dgrad", [(dmix2, W["w_out_p"])], "nt", BF16).reshape(nb, S, QP + D_INNER)
    g_out_p = jnp.concatenate([matmul_tn("out_wgrad_attn", o2, dmix2), matmul_tn("out_wgrad_ssd", ssd2, dmix2)], axis=0)
    dy, dxs_direct, dz, d_dexp, d_ssd_norm = rows_bwd(
        "ssd_finish_bwd", fn_ssd_finish, nb, S // tr, tr, fin_rows, [], fin_gl,
        [(dcat, D_INNER, QP // D_INNER, 0)], [(0, F32), (2, F32), (3, BF16)])
    dq_pre, dkv, dkr = attn_bwd(q_raw, kv, kr, dcat, cos_q, sin_q, cosS, sinS, tq)
    dq_pre = dq_pre.reshape(nb * S, QP)
    dkr_pre = rope_call("rope_dk", dkr, LANE, 0, cosT, -sinT, BF16, tr)
    dkv2 = dkv.reshape(nb * T, QP)
    dqn = matmul("q_dgrad", [(dq_pre, W["w_q_p"])], "nt", F32).reshape(nb, S, Q_RANK)
    g_q_p = matmul_tn("q_wgrad", qn2, dq_pre)
    dkvn = matmul("kv_dgrad", [(dkv2, W["w_kv"])], "nt", F32).reshape(nb, T, KV_RANK)
    g_kv = matmul_tn("kv_wgrad", kvn2, dkv2)
    early_grads = (_out_grad(g_out_p), glu_deinterleave(g_up), g_down, _q_grad(g_q_p), g_kv)
    early = hosted_all_to_all([_per_device(g, n) for g, n in zip(early_grads, EARLY)])
    dxbc2, ddt2, ssd_stats, *received = ssd_bwd(xbc, dt2, alog2, dtb2, hin, dy, C, early)
    ddt_block = jnp.concatenate([ddt2[0][..., :SSD_HEADS], ddt2[1][..., :SSD_HEADS],
                                 jnp.zeros((nb, T, LANE - 2 * SSD_HEADS), F32)], axis=-1).astype(BF16)
    dxbc_raw, ssd_rows = ssd_conv_bwd(u, ssd_w8, V["ssd_conv_b"], dxbc2, dxs_direct, C, tc)
    dcq, d_q_norm = rows_bwd("q_norm_bwd", fn_rms, nb, S // tr, tr, [(u, Q_RANK, OFF_CQ // Q_RANK, cblk)], [], [V["q_norm"]],
                             [(dqn, Q_RANK, 0, 0)], [(0, BF16)])
    dckv, d_kv_norm = rows_bwd("kv_norm_bwd", fn_rms, nb, T // tr, tr, [(u, KV_RANK, OFF_CKV // KV_RANK, 0)], [], [V["kv_norm"]],
                               [(dkvn, KV_RANK, 0, 0)], [(0, BF16)])

    def ctx_rows(t):
        return jnp.pad(t, ((0, 0), (C, 0), (0, 0)))

    du = [("cq", ctx_rows(dcq), OFF_CQ, Q_RANK), ("kr", dkr_pre, OFF_KR, LANE), ("ckv", dckv, OFF_CKV, KV_RANK),
          ("z", ctx_rows(dz), OFF_Z, D_INNER), ("xbc", dxbc_raw, OFF_XBC, XBC), ("dt", ddt_block, OFF_DT, LANE)]
    du = [(name, t.reshape(nb * T, w), off, w) for (name, t, off, w) in du]
    g = {name: matmul_tn("in_wgrad_" + name, h1, t) for (name, t, _, _) in du}
    g_in = jnp.concatenate([g["cq"], g["ckv"], g["kr"][:, KR_LANE:KR_LANE + ROPE], g["z"], g["xbc"],
                            g["dt"][:, :2 * SSD_HEADS]], axis=1)
    dh1, received_in = matmul("in_dgrad", [(t, W["w_in_p"][:, off:off + w]) for (_, t, off, w) in du], "nt", BF16,
                              hosted=hosted_all_to_all([_per_device(g_in, "w_in").astype(BF16)]))
    dh1 = dh1.reshape(nb, T, D)

    def fn_prenorm_res(xv, shift, scale, g):
        return fn_prenorm(xv, shift, scale, g) + (xv,)

    grad_x, dshift1, dscale1, d_mix_pre_x = rows_bwd(
        "prenorm_x_bwd", fn_prenorm_res, nb, S // tr, tr, [(x, D, 0, 0)], [m[0], m[1]], [V["mix_pre_norm"]],
        [(dh1, D, 0, cblk), (dx_a, D, 0, 0)], [(0, F32)])
    dshift_c, dscale_c, d_mix_pre_c = rows_bwd(
        "prenorm_c_bwd", fn_prenorm, nb, C // tr, tr, [(ctx, D, 0, 0)], [], [mc[0], mc[1], V["mix_pre_norm"]],
        [(dh1, D, 0, 0)], [])

    dmod_x = jnp.concatenate([dshift1, dscale1, dgate1, dshift2, dscale2, dgate2], axis=-1).reshape(nb, N_MOD * D)
    dmod_c = jnp.concatenate([dshift_c, dscale_c, jnp.zeros((1, (N_MOD - 2) * D), F32)], axis=-1)
    gv = dict(
        mix_pre_norm=d_mix_pre_x + d_mix_pre_c, mix_post_norm=d_mix_post, q_norm=d_q_norm, kv_norm=d_kv_norm,
        ssd_conv_w=ssd_rows[:SSD_K], ssd_conv_b=ssd_rows[SSD_K:SSD_K + 1],
        ssd_a_log=jnp.concatenate([ssd_stats[0:1, :SSD_HEADS], ssd_stats[2:3, :SSD_HEADS]], axis=1),
        ssd_dt_bias=jnp.concatenate([ssd_stats[1:2, :SSD_HEADS], ssd_stats[3:4, :SSD_HEADS]], axis=1),
        ssd_d=jnp.sum(d_dexp.reshape(SSD_HEADS, SSD_P), axis=1).reshape(1, SSD_HEADS), ssd_norm=d_ssd_norm,
        ffn_pre_norm=d_ffn_pre, ffn_post_norm=d_ffn_post,
        ffn_conv_w=ffn_rows[:FFN_K], ffn_conv_b=ffn_rows[FFN_K:FFN_K + 1])
    return loss, grad_x, dmod_x, dmod_c, gv, dict(zip(EARLY, received), w_in=received_in)


WEIGHT_ORDER = ("c_ctx", "w_mod", "b_mod", "mix_pre_norm", "mix_post_norm", "w_in", "q_norm", "w_q_up", "kv_norm",
                "w_kv_up", "ssd_conv_w", "ssd_conv_b", "ssd_a_log", "ssd_dt_bias", "ssd_d", "ssd_norm", "w_out",
                "ffn_pre_norm", "ffn_post_norm", "w_up", "ffn_conv_w", "ffn_conv_b", "w_down")
MATRICES = ("w_in", "w_q_up", "w_kv_up", "w_out", "w_up", "w_down")
ROW_SHARDED = ("w_out", "w_down")
SMALL_SUMMED = ("c_ctx", "mix_pre_norm", "mix_post_norm", "q_norm", "kv_norm", "ssd_conv_w", "ssd_conv_b", "ssd_a_log",
                "ssd_dt_bias", "ssd_d", "ssd_norm", "ffn_pre_norm", "ffn_post_norm", "ffn_conv_w", "ffn_conv_b")
MOD_ROWS = 8


def _whole(shards, name):
    if name in ROW_SHARDED:
        return shards.reshape(-1, shards.shape[-1])
    return jnp.concatenate([shards[j] for j in range(N_DEV)], axis=1)


def _per_device(g, name):
    if name in ROW_SHARDED:
        return g.reshape(N_DEV, -1, g.shape[-1])
    return jnp.stack(jnp.split(g, N_DEV, axis=1))


def kernel(x, c, ctx, c_ctx, w_mod, b_mod, mix_pre_norm, mix_post_norm, w_in, q_norm, w_q_up, kv_norm, w_kv_up, ssd_conv_w, ssd_conv_b, ssd_a_log, ssd_dt_bias, ssd_d, ssd_norm, w_out, ffn_pre_norm, ffn_post_norm, w_up, ffn_conv_w, ffn_conv_b, w_down, loss_target, m_c_ctx, m_w_mod, m_b_mod, m_mix_pre_norm, m_mix_post_norm, m_w_in, m_q_norm, m_w_q_up, m_kv_norm, m_w_kv_up, m_ssd_conv_w, m_ssd_conv_b, m_ssd_a_log, m_ssd_dt_bias, m_ssd_d, m_ssd_norm, m_w_out, m_ffn_pre_norm, m_ffn_post_norm, m_w_up, m_ffn_conv_w, m_ffn_conv_b, m_w_down, v_c_ctx, v_w_mod, v_b_mod, v_mix_pre_norm, v_mix_post_norm, v_w_in, v_q_norm, v_w_q_up, v_kv_norm, v_w_kv_up, v_ssd_conv_w, v_ssd_conv_b, v_ssd_a_log, v_ssd_dt_bias, v_ssd_d, v_ssd_norm, v_w_out, v_ffn_pre_norm, v_ffn_post_norm, v_w_up, v_ffn_conv_w, v_ffn_conv_b, v_w_down):
    weights = dict(c_ctx=c_ctx, w_mod=w_mod, b_mod=b_mod, mix_pre_norm=mix_pre_norm, mix_post_norm=mix_post_norm, w_in=w_in, q_norm=q_norm, w_q_up=w_q_up, kv_norm=kv_norm, w_kv_up=w_kv_up, ssd_conv_w=ssd_conv_w, ssd_conv_b=ssd_conv_b, ssd_a_log=ssd_a_log, ssd_dt_bias=ssd_dt_bias, ssd_d=ssd_d, ssd_norm=ssd_norm, w_out=w_out, ffn_pre_norm=ffn_pre_norm, ffn_post_norm=ffn_post_norm, w_up=w_up, ffn_conv_w=ffn_conv_w, ffn_conv_b=ffn_conv_b, w_down=w_down)
    mom1 = dict(c_ctx=m_c_ctx, w_mod=m_w_mod, b_mod=m_b_mod, mix_pre_norm=m_mix_pre_norm, mix_post_norm=m_mix_post_norm, w_in=m_w_in, q_norm=m_q_norm, w_q_up=m_w_q_up, kv_norm=m_kv_norm, w_kv_up=m_w_kv_up, ssd_conv_w=m_ssd_conv_w, ssd_conv_b=m_ssd_conv_b, ssd_a_log=m_ssd_a_log, ssd_dt_bias=m_ssd_dt_bias, ssd_d=m_ssd_d, ssd_norm=m_ssd_norm, w_out=m_w_out, ffn_pre_norm=m_ffn_pre_norm, ffn_post_norm=m_ffn_post_norm, w_up=m_w_up, ffn_conv_w=m_ffn_conv_w, ffn_conv_b=m_ffn_conv_b, w_down=m_w_down)
    mom2 = dict(c_ctx=v_c_ctx, w_mod=v_w_mod, b_mod=v_b_mod, mix_pre_norm=v_mix_pre_norm, mix_post_norm=v_mix_post_norm, w_in=v_w_in, q_norm=v_q_norm, w_q_up=v_w_q_up, kv_norm=v_kv_norm, w_kv_up=v_w_kv_up, ssd_conv_w=v_ssd_conv_w, ssd_conv_b=v_ssd_conv_b, ssd_a_log=v_ssd_a_log, ssd_dt_bias=v_ssd_dt_bias, ssd_d=v_ssd_d, ssd_norm=v_ssd_norm, w_out=v_w_out, ffn_pre_norm=v_ffn_pre_norm, ffn_post_norm=v_ffn_post_norm, w_up=v_w_up, ffn_conv_w=v_ffn_conv_w, ffn_conv_b=v_ffn_conv_b, w_down=v_w_down)
    nb, S, D = x.shape
    me = 4 * lax.axis_index("x") + 2 * lax.axis_index("y") + lax.axis_index("c")

    *first, c_all, ssd_w_sh, ffn_w_sh = all_gather(
        "gather_first", [weights[n][0].astype(BF16) for n in FIRST] + [c, ssd_conv_w[0], ffn_conv_w[0]])
    W = first_weights_to_internal(*[_whole(s, n) for n, s in zip(FIRST, first)])
    late_shards = [weights[n][0].astype(BF16) for n in LATE_WEIGHTS]
    V = {n: weights[n].reshape(1, -1) for n in SMALL_SUMMED if n != "c_ctx"}
    V["ssd_conv_w"] = _whole(ssd_w_sh, "ssd_conv_w")
    V["ffn_conv_w"] = _whole(ffn_w_sh, "ffn_conv_w")

    n_all = N_DEV * nb
    mod_rows = -(-(n_all + 1) // 8) * 8
    c_pad = jnp.concatenate([c_all.reshape(n_all, D), c_ctx.reshape(1, D), jnp.zeros((mod_rows - n_all - 1, D), F32)], axis=0)
    mod_cols = w_mod.shape[2]
    b_mine = lax.dynamic_slice(b_mod, (0, me * mod_cols), (1, mod_cols))
    mod_part = matmul("mod_proj", [(c_pad, w_mod[0])], "nn", F32, bias=b_mine, silu_a=True)
    mod_all = _whole(all_gather("gather_mod", [mod_part])[0], "w_mod")
    mod_x = lax.dynamic_slice(mod_all, (me * nb, 0), (nb, mod_all.shape[1]))
    mod_c = mod_all[n_all:n_all + 1]

    loss, grad_x, dmod_x, dmod_c, gv, slots = local_step(x, ctx, loss_target, mod_x, mod_c, W, late_shards, V)

    dmod_mine = jnp.concatenate([dmod_x, dmod_c, jnp.zeros((MOD_ROWS - nb - 1, dmod_x.shape[1]), F32)], axis=0)
    dmod_all = all_gather("gather_dmod", [dmod_mine])[0]
    dmod_ctx = sum_slots("sum_dmod_ctx", dmod_all[:, nb:nb + 1].reshape(N_DEV, -1, LANE)).reshape(1, -1)
    dmod_full = jnp.concatenate([dmod_all[:, :nb].reshape(n_all, -1), dmod_ctx,
                                 jnp.zeros((mod_rows - n_all - 1, dmod_ctx.shape[1]), F32)], axis=0)
    (g_b_mod,) = ew_call("mod_bias_grad", lambda t: (jnp.sum(t, axis=0, keepdims=True),), [dmod_full], [((1, dmod_full.shape[1]), F32)])
    dmod_cols = lax.dynamic_slice(dmod_full, (0, me * mod_cols), (mod_rows, mod_cols))
    g_w_mod = matmul_tn("mod_wgrad", c_pad, dmod_cols, silu_a=True)
    dsilu_ctx = matmul("mod_dgrad_ctx", [(dmod_cols[n_all:n_all + 8], w_mod[0])], "nt", F32)[0:1]

    def silu_vjp(cc, ct):
        return (jax.vjp(_silu, cc)[1](ct)[0],)

    (g_c_ctx_part,) = ew_call("c_ctx_grad", silu_vjp, [c_ctx.reshape(1, D), dsilu_ctx], [((1, D), F32)])

    gv = dict(gv, c_ctx=g_c_ctx_part)
    small_parts = [loss] + [gv[n] for n in SMALL_SUMMED]
    small_sum = sum_slots("sum_small", all_gather("gather_small_grads", [_pack_rows(small_parts)])[0])
    summed = _unpack_rows(small_sum, [p.shape for p in small_parts])
    loss_out = summed[0][0, 0]
    grads = {n: g.reshape(weights[n].shape) if n not in ("ssd_conv_w", "ffn_conv_w") else g for n, g in zip(SMALL_SUMMED, summed[1:])}
    for n in ("ssd_conv_w", "ffn_conv_w"):
        cols = weights[n].shape[2]
        grads[n] = lax.dynamic_slice(grads[n], (0, me * cols), (grads[n].shape[0], cols)).reshape(weights[n].shape)
    grads["b_mod"] = g_b_mod.reshape(b_mod.shape)

    slots = dict(slots, w_mod=g_w_mod[None])
    delta, new_m, new_v = {}, {}, {}
    for n in MATRICES + ("w_mod",):
        g, d, mn, vn = adamw_matrix("adamw_" + n, weights[n][0], slots[n], mom1[n][0], mom2[n][0])
        grads[n], delta[n], new_m[n], new_v[n] = [t.reshape(weights[n].shape) for t in (g, d, mn, vn)]
    small = [n for n in WEIGHT_ORDER if n not in slots]

    def two_d(t):
        return t.reshape(-1, t.shape[-1])

    ds, ms, vs = adamw_small(*[[two_d(t[n]) for n in small] for t in (weights, grads, mom1, mom2)])
    for n, d, mn, vn in zip(small, ds, ms, vs):
        delta[n], new_m[n], new_v[n] = [t.reshape(weights[n].shape) for t in (d, mn, vn)]
    return (loss_out, grad_x, *[t[n] for t in (grads, delta, new_m, new_v) for n in WEIGHT_ORDER])
```
